```python
import math
import jax, jax.numpy as jnp
from jax import lax
import numpy as np

D_MODEL = 1024
BATCH = 8
SEQ = 2048
DEPTH = 4

N_MIXERS = 2
CHUNK = 64
A_HEADS = 8
A_DK = 128
A_DV = 128
A_CONV = 5
A_W = A_HEADS * A_DK
A_VW = A_HEADS * A_DV
A_CONV_CH = 2 * A_W + A_VW
A_IN = 2 * A_W + 2 * A_VW + 4 * A_HEADS
B_HEADS = 4
B_DK = 128
B_DV = 256
B_RANK = 16
B_TAU = 16.0
B_KW = B_HEADS * B_DK
B_VW = B_HEADS * B_DV
B_IN = 2 * B_KW + 2 * B_VW + 2 * B_RANK
D_FF = 4 * D_MODEL
DEEPNORM_ALPHA = (2 * DEPTH) ** 0.25
DEEPNORM_BETA = (8 * DEPTH) ** -0.25
LN_EPS = 1e-5
RMS_EPS = 1e-6
N_A_LAYERS = (DEPTH + 1) // 2
N_B_LAYERS = DEPTH // 2

kernel_name = "hybrid_gdn_gla_deepnorm_encoder"

F32 = jnp.float32


def _layernorm(x, g, b):
    xf = x.astype(F32)
    mu = jnp.mean(xf, axis=-1, keepdims=True)
    xc = xf - mu
    var = jnp.mean(xc * xc, axis=-1, keepdims=True)
    return (xc * lax.rsqrt(var + LN_EPS) * g.astype(F32) + b.astype(F32)).astype(x.dtype)


def _rmsnorm(x, g):
    return x * lax.rsqrt(jnp.mean(x * x, axis=-1, keepdims=True) + RMS_EPS) * g.astype(F32)


def _l2norm(x):
    return x * lax.rsqrt(jnp.sum(x * x, axis=-1, keepdims=True) + 1e-6)


def _flip(t):
    return jnp.flip(t, axis=2)


def _depthwise_conv(u, w):
    k = w.shape[0]
    return lax.conv_general_dilated(
        u, w[:, None, :], window_strides=(1,), padding=[(k // 2, k // 2)],
        dimension_numbers=("NWC", "WIO", "NWC"), feature_group_count=u.shape[-1])


def _gated_delta_chunked(q, k, v, beta, g):
    bn, h, s, dk = q.shape
    dv = v.shape[-1]
    n = s // CHUNK
    q = q.reshape(bn, h, n, CHUNK, dk)
    k = k.reshape(bn, h, n, CHUNK, dk)
    v = v.reshape(bn, h, n, CHUNK, dv)
    beta = beta.reshape(bn, h, n, CHUNK)
    gc = jnp.cumsum(g.reshape(bn, h, n, CHUNK), axis=-1)
    causal = jnp.tril(jnp.ones((CHUNK, CHUNK), bool))
    strict = jnp.tril(jnp.ones((CHUNK, CHUNK), bool), -1)
    decay_mat = jnp.exp(jnp.where(causal, gc[..., :, None] - gc[..., None, :], -jnp.inf))
    kb = k * beta[..., None]
    a_mat = jnp.where(strict, jnp.einsum('bhncd,bhnsd->bhncs', kb, k) * decay_mat, 0.0)
    lhs = a_mat + jnp.eye(CHUNK, dtype=F32)
    rhs = jnp.concatenate([v * beta[..., None], kb * jnp.exp(gc)[..., None]], axis=-1)
    sol = lax.linalg.triangular_solve(lhs, rhs, left_side=True, lower=True, unit_diagonal=True)
    u, w = sol[..., :dv], sol[..., dv:]
    qk = jnp.einsum('bhncd,bhnsd->bhncs', q, k) * decay_mat
    q_dec = q * jnp.exp(gc)[..., None]
    k_dec = k * jnp.exp(gc[..., -1:] - gc)[..., None]
    g_last = jnp.exp(gc[..., -1])
    xs = tuple(jnp.moveaxis(t, 2, 0) for t in (qk, q_dec, k_dec, u, w, g_last))

    def step(state, inp):
        qk_c, qd_c, kd_c, u_c, w_c, gl_c = inp
        v_new = u_c - jnp.einsum('bhcd,bhde->bhce', w_c, state)
        o = jnp.einsum('bhcd,bhde->bhce', qd_c, state) + jnp.einsum('bhcs,bhse->bhce', qk_c, v_new)
        state = state * gl_c[..., None, None] + jnp.einsum('bhcd,bhce->bhde', kd_c, v_new)
        return state, o

    s0 = jnp.zeros((bn, h, dk, dv), F32)
    _, o = lax.scan(step, s0, xs)
    return jnp.moveaxis(o, 0, 2).reshape(bn, h, s, dv)


def _gla_chunked(q, k, v, log_a):
    bn, h, s, dk = q.shape
    dv = v.shape[-1]
    n = s // CHUNK

    def to_chunks(t):
        return jnp.moveaxis(t.reshape(bn, h, n, CHUNK, t.shape[-1]), 2, 0)

    qc, kc, vc = to_chunks(q), to_chunks(k), to_chunks(v)
    bc = jnp.cumsum(to_chunks(log_a), axis=3)
    causal = jnp.tril(jnp.ones((CHUNK, CHUNK), bool))[:, :, None]

    def step(state, inp):
        q_c, k_c, v_c, b_c = inp
        dec = jnp.exp(jnp.where(causal, b_c[:, :, :, None, :] - b_c[:, :, None, :, :], -jnp.inf))
        scores = jnp.einsum('bhid,bhjd,bhijd->bhij', q_c, k_c, dec)
        o = jnp.einsum('bhid,bhde->bhie', q_c * jnp.exp(b_c), state) + jnp.einsum('bhij,bhje->bhie', scores, v_c)
        b_last = b_c[:, :, -1:, :]
        state = jnp.exp(b_last[:, :, 0, :])[..., None] * state + jnp.einsum(
            'bhjd,bhje->bhde', k_c * jnp.exp(b_last - b_c), v_c)
        return state, o

    s0 = jnp.zeros((bn, h, dk, dv), F32)
    _, o = lax.scan(step, s0, (qc, kc, vc, bc))
    return jnp.moveaxis(o, 0, 2).reshape(bn, h, s, dv)


def _mixer_gdn(h, w_in, conv_w, a_log, dt_bias, norm_g, w_out):
    bn, s, _ = h.shape
    proj = jnp.matmul(h, w_in).astype(F32)
    qkv = jax.nn.silu(_depthwise_conv(proj[..., :A_CONV_CH], conv_w.astype(F32)))
    z = proj[..., A_CONV_CH:A_CONV_CH + A_VW]
    ba = proj[..., A_CONV_CH + A_VW:].reshape(bn, s, 2, 2, A_HEADS)

    def heads(t, d):
        return t.reshape(bn, s, A_HEADS, d).transpose(0, 2, 1, 3)

    q = _l2norm(heads(qkv[..., :A_W], A_DK)) * (A_DK ** -0.5)
    k = _l2norm(heads(qkv[..., A_W:2 * A_W], A_DK))
    v = heads(qkv[..., 2 * A_W:], A_DV)
    beta = jax.nn.sigmoid(ba[:, :, 0]).transpose(2, 0, 3, 1)
    g = (-jnp.exp(a_log.astype(F32)) * jax.nn.softplus(ba[:, :, 1] + dt_bias.astype(F32))).transpose(2, 0, 3, 1)
    o_f = _gated_delta_chunked(q, k, v, beta[0], g[0])
    o_b = _flip(_gated_delta_chunked(_flip(q), _flip(k), _flip(v), _flip(beta[1]), _flip(g[1])))
    o = (o_f + o_b).transpose(0, 2, 1, 3)
    o = _rmsnorm(o, norm_g) * jax.nn.silu(z.reshape(bn, s, A_HEADS, A_DV))
    return jnp.matmul(o.reshape(bn, s, A_VW).astype(w_out.dtype), w_out)


def _mixer_gla(h, w_in, gate_w2, gate_b, norm_g, w_out):
    bn, s, _ = h.shape
    proj = jnp.matmul(h, w_in).astype(F32)

    def heads(t, d):
        return t.reshape(bn, s, B_HEADS, d).transpose(0, 2, 1, 3)

    q = heads(proj[..., :B_KW], B_DK) * (B_DK ** -0.5)
    k = heads(proj[..., B_KW:2 * B_KW], B_DK)
    v = heads(proj[..., 2 * B_KW:2 * B_KW + B_VW], B_DV)
    r = proj[..., 2 * B_KW + B_VW:2 * B_KW + 2 * B_VW]
    gl = proj[..., 2 * B_KW + 2 * B_VW:].reshape(bn, s, 2, B_RANK)
    gate_logit = jnp.einsum('bsnr,nrk->nbsk', gl, gate_w2.astype(F32)) + gate_b.astype(F32)[:, None, None, :]
    log_a = (jax.nn.log_sigmoid(gate_logit) / B_TAU).reshape(2, bn, s, B_HEADS, B_DK).transpose(0, 1, 3, 2, 4)
    o_f = _gla_chunked(q, k, v, log_a[0])
    o_b = _flip(_gla_chunked(_flip(q), _flip(k), _flip(v), _flip(log_a[1])))
    o = (o_f + o_b).transpose(0, 2, 1, 3)
    o = _rmsnorm(o, norm_g) * jax.nn.silu(r.reshape(bn, s, B_HEADS, B_DV))
    return jnp.matmul(o.reshape(bn, s, B_VW).astype(w_out.dtype), w_out)


def _sq_relu_mlp(h, w1, w2):
    a = jax.nn.relu(jnp.matmul(h, w1))
    return jnp.matmul(a * a, w2)


def _fwd_setup_inputs(seed: int = 0) -> dict:
    key = jax.random.key(seed)
    ks = jax.random.split(key, 20)
    nrm = jax.random.normal
    x = nrm(ks[0], (BATCH, SEQ, D_MODEL), F32)
    a_w_in = nrm(ks[1], (N_A_LAYERS, D_MODEL, A_IN), F32) * D_MODEL ** -0.5
    a_conv = nrm(ks[2], (N_A_LAYERS, A_CONV, A_CONV_CH), F32) * A_CONV ** -0.5
    a_alog = jnp.log(jax.random.uniform(ks[3], (N_A_LAYERS, 2, A_HEADS), F32, 1.0, 16.0))
    dt = jnp.exp(jax.random.uniform(ks[4], (N_A_LAYERS, 2, A_HEADS), F32, math.log(1e-3), math.log(1e-1)))
    a_dt_bias = dt + jnp.log(-jnp.expm1(-dt))
    a_norm_g = 1.0 + 0.02 * nrm(ks[5], (N_A_LAYERS, A_DV), F32)
    a_w_out = nrm(ks[6], (N_A_LAYERS, A_VW, D_MODEL), F32) * (A_VW ** -0.5 * DEEPNORM_BETA)
    b_w_in = nrm(ks[7], (N_B_LAYERS, D_MODEL, B_IN), F32) * D_MODEL ** -0.5
    b_gate_w2 = nrm(ks[8], (N_B_LAYERS, 2, B_RANK, B_KW), F32) * B_RANK ** -0.5
    b_gate_b = 0.1 * nrm(ks[9], (N_B_LAYERS, 2, B_KW), F32)
    b_norm_g = 1.0 + 0.02 * nrm(ks[10], (N_B_LAYERS, B_DV), F32)
    b_w_out = nrm(ks[11], (N_B_LAYERS, B_VW, D_MODEL), F32) * (B_VW ** -0.5 * DEEPNORM_BETA)
    ln1_g = 1.0 + 0.02 * nrm(ks[12], (DEPTH, D_MODEL), F32)
    ln1_b = 0.02 * nrm(ks[13], (DEPTH, D_MODEL), F32)
    mlp_w1 = nrm(ks[14], (DEPTH, D_MODEL, D_FF), F32) * D_MODEL ** -0.5
    mlp_w2 = nrm(ks[15], (DEPTH, D_FF, D_MODEL), F32) * (D_FF ** -0.5 * DEEPNORM_BETA)
    ln2_g = 1.0 + 0.02 * nrm(ks[16], (DEPTH, D_MODEL), F32)
    ln2_b = 0.02 * nrm(ks[17], (DEPTH, D_MODEL), F32)
    return {"x": x, "a_w_in": a_w_in, "a_conv": a_conv, "a_alog": a_alog, "a_dt_bias": a_dt_bias,
            "a_norm_g": a_norm_g, "a_w_out": a_w_out, "b_w_in": b_w_in, "b_gate_w2": b_gate_w2,
            "b_gate_b": b_gate_b, "b_norm_g": b_norm_g, "b_w_out": b_w_out, "ln1_g": ln1_g,
            "ln1_b": ln1_b, "mlp_w1": mlp_w1, "mlp_w2": mlp_w2, "ln2_g": ln2_g, "ln2_b": ln2_b}


def _fwd_reference(x, a_w_in, a_conv, a_alog, a_dt_bias, a_norm_g, a_w_out, b_w_in, b_gate_w2,
              b_gate_b, b_norm_g, b_w_out, ln1_g, ln1_b, mlp_w1, mlp_w2, ln2_g, ln2_b):
    for i in range(DEPTH):
        j = i // N_MIXERS
        if i % N_MIXERS == 0:
            m = _mixer_gdn(x, a_w_in[j], a_conv[j], a_alog[j], a_dt_bias[j], a_norm_g[j], a_w_out[j])
        else:
            m = _mixer_gla(x, b_w_in[j], b_gate_w2[j], b_gate_b[j], b_norm_g[j], b_w_out[j])
        x = _layernorm(DEEPNORM_ALPHA * x + m.astype(x.dtype), ln1_g[i], ln1_b[i])
        x = _layernorm(DEEPNORM_ALPHA * x + _sq_relu_mlp(x, mlp_w1[i], mlp_w2[i]).astype(x.dtype), ln2_g[i], ln2_b[i])
    return x


import jax as _jax
import jax.numpy as _jnp

TWIN_FORMAT = 'train_step'
FWD_PARAMS = ['x', 'a_w_in', 'a_conv', 'a_alog', 'a_dt_bias', 'a_norm_g', 'a_w_out', 'b_w_in', 'b_gate_w2', 'b_gate_b', 'b_norm_g', 'b_w_out', 'ln1_g', 'ln1_b', 'mlp_w1', 'mlp_w2', 'ln2_g', 'ln2_b']
TWIN_WEIGHTS = ['a_w_in', 'a_conv', 'a_alog', 'a_dt_bias', 'a_norm_g', 'a_w_out', 'b_w_in', 'b_gate_w2', 'b_gate_b', 'b_norm_g', 'b_w_out', 'ln1_g', 'ln1_b', 'mlp_w1', 'mlp_w2', 'ln2_g', 'ln2_b']
TWIN_DIFF_INPUT = 'x'
TWIN_INPUTS = ['x', 'a_w_in', 'a_conv', 'a_alog', 'a_dt_bias', 'a_norm_g', 'a_w_out', 'b_w_in', 'b_gate_w2', 'b_gate_b', 'b_norm_g', 'b_w_out', 'ln1_g', 'ln1_b', 'mlp_w1', 'mlp_w2', 'ln2_g', 'ln2_b', 'loss_target', 'm_a_w_in', 'm_a_conv', 'm_a_alog', 'm_a_dt_bias', 'm_a_norm_g', 'm_a_w_out', 'm_b_w_in', 'm_b_gate_w2', 'm_b_gate_b', 'm_b_norm_g', 'm_b_w_out', 'm_ln1_g', 'm_ln1_b', 'm_mlp_w1', 'm_mlp_w2', 'm_ln2_g', 'm_ln2_b', 'v_a_w_in', 'v_a_conv', 'v_a_alog', 'v_a_dt_bias', 'v_a_norm_g', 'v_a_w_out', 'v_b_w_in', 'v_b_gate_w2', 'v_b_gate_b', 'v_b_norm_g', 'v_b_w_out', 'v_ln1_g', 'v_ln1_b', 'v_mlp_w1', 'v_mlp_w2', 'v_ln2_g', 'v_ln2_b']
TWIN_OUTPUTS = ['loss', 'grad_x', 'grad_a_w_in', 'grad_a_conv', 'grad_a_alog', 'grad_a_dt_bias', 'grad_a_norm_g', 'grad_a_w_out', 'grad_b_w_in', 'grad_b_gate_w2', 'grad_b_gate_b', 'grad_b_norm_g', 'grad_b_w_out', 'grad_ln1_g', 'grad_ln1_b', 'grad_mlp_w1', 'grad_mlp_w2', 'grad_ln2_g', 'grad_ln2_b', 'delta_a_w_in', 'delta_a_conv', 'delta_a_alog', 'delta_a_dt_bias', 'delta_a_norm_g', 'delta_a_w_out', 'delta_b_w_in', 'delta_b_gate_w2', 'delta_b_gate_b', 'delta_b_norm_g', 'delta_b_w_out', 'delta_ln1_g', 'delta_ln1_b', 'delta_mlp_w1', 'delta_mlp_w2', 'delta_ln2_g', 'delta_ln2_b', 'new_m_a_w_in', 'new_m_a_conv', 'new_m_a_alog', 'new_m_a_dt_bias', 'new_m_a_norm_g', 'new_m_a_w_out', 'new_m_b_w_in', 'new_m_b_gate_w2', 'new_m_b_gate_b', 'new_m_b_norm_g', 'new_m_b_w_out', 'new_m_ln1_g', 'new_m_ln1_b', 'new_m_mlp_w1', 'new_m_mlp_w2', 'new_m_ln2_g', 'new_m_ln2_b', 'new_v_a_w_in', 'new_v_a_conv', 'new_v_a_alog', 'new_v_a_dt_bias', 'new_v_a_norm_g', 'new_v_a_w_out', 'new_v_b_w_in', 'new_v_b_gate_w2', 'new_v_b_gate_b', 'new_v_b_norm_g', 'new_v_b_w_out', 'new_v_ln1_g', 'new_v_ln1_b', 'new_v_mlp_w1', 'new_v_mlp_w2', 'new_v_ln2_g', 'new_v_ln2_b']
TWIN_LEAF_KINDS = {'loss': 'loss', 'grad_x': 'grad_x', 'grad_a_w_in': 'grad_w', 'grad_a_conv': 'grad_w', 'grad_a_alog': 'grad_w', 'grad_a_dt_bias': 'grad_w', 'grad_a_norm_g': 'grad_w', 'grad_a_w_out': 'grad_w', 'grad_b_w_in': 'grad_w', 'grad_b_gate_w2': 'grad_w', 'grad_b_gate_b': 'grad_w', 'grad_b_norm_g': 'grad_w', 'grad_b_w_out': 'grad_w', 'grad_ln1_g': 'grad_w', 'grad_ln1_b': 'grad_w', 'grad_mlp_w1': 'grad_w', 'grad_mlp_w2': 'grad_w', 'grad_ln2_g': 'grad_w', 'grad_ln2_b': 'grad_w', 'delta_a_w_in': 'delta_w', 'delta_a_conv': 'delta_w', 'delta_a_alog': 'delta_w', 'delta_a_dt_bias': 'delta_w', 'delta_a_norm_g': 'delta_w', 'delta_a_w_out': 'delta_w', 'delta_b_w_in': 'delta_w', 'delta_b_gate_w2': 'delta_w', 'delta_b_gate_b': 'delta_w', 'delta_b_norm_g': 'delta_w', 'delta_b_w_out': 'delta_w', 'delta_ln1_g': 'delta_w', 'delta_ln1_b': 'delta_w', 'delta_mlp_w1': 'delta_w', 'delta_mlp_w2': 'delta_w', 'delta_ln2_g': 'delta_w', 'delta_ln2_b': 'delta_w', 'new_m_a_w_in': 'new_m', 'new_m_a_conv': 'new_m', 'new_m_a_alog': 'new_m', 'new_m_a_dt_bias': 'new_m', 'new_m_a_norm_g': 'new_m', 'new_m_a_w_out': 'new_m', 'new_m_b_w_in': 'new_m', 'new_m_b_gate_w2': 'new_m', 'new_m_b_gate_b': 'new_m', 'new_m_b_norm_g': 'new_m', 'new_m_b_w_out': 'new_m', 'new_m_ln1_g': 'new_m', 'new_m_ln1_b': 'new_m', 'new_m_mlp_w1': 'new_m', 'new_m_mlp_w2': 'new_m', 'new_m_ln2_g': 'new_m', 'new_m_ln2_b': 'new_m', 'new_v_a_w_in': 'new_v', 'new_v_a_conv': 'new_v', 'new_v_a_alog': 'new_v', 'new_v_a_dt_bias': 'new_v', 'new_v_a_norm_g': 'new_v', 'new_v_a_w_out': 'new_v', 'new_v_b_w_in': 'new_v', 'new_v_b_gate_w2': 'new_v', 'new_v_b_gate_b': 'new_v', 'new_v_b_norm_g': 'new_v', 'new_v_b_w_out': 'new_v', 'new_v_ln1_g': 'new_v', 'new_v_ln1_b': 'new_v', 'new_v_mlp_w1': 'new_v', 'new_v_mlp_w2': 'new_v', 'new_v_ln2_g': 'new_v', 'new_v_ln2_b': 'new_v'}


def _forward(args):
    return _fwd_reference(*[args[k] for k in FWD_PARAMS])


def _output_shape():
    out = _jax.eval_shape(lambda: _forward(_fwd_setup_inputs(0)))
    return out.shape, out.dtype

N_MICROBATCH = 1
ADAM_LR = 0.001
ADAM_B1 = 0.9
ADAM_B2 = 0.999
ADAM_EPS = 1e-08
ADAM_WD = 0.01
ADAM_STEP = 10
PER_EXAMPLE_BATCH_AXIS = {'x': 0, 'loss_target': 0}
SHARED_INPUTS = []
_WEIGHT_DTYPES = {'a_w_in': _jnp.float32, 'a_conv': _jnp.float32, 'a_alog': _jnp.float32, 'a_dt_bias': _jnp.float32, 'a_norm_g': _jnp.float32, 'a_w_out': _jnp.float32, 'b_w_in': _jnp.float32, 'b_gate_w2': _jnp.float32, 'b_gate_b': _jnp.float32, 'b_norm_g': _jnp.float32, 'b_w_out': _jnp.float32, 'ln1_g': _jnp.float32, 'ln1_b': _jnp.float32, 'mlp_w1': _jnp.float32, 'mlp_w2': _jnp.float32, 'ln2_g': _jnp.float32, 'ln2_b': _jnp.float32}
MOMENT_SCALE = {'a_w_in': 1.625907e-02, 'a_conv': 1.497313e-02, 'a_alog': 6.379369e-02, 'a_dt_bias': 6.337455e-02, 'a_norm_g': 6.378164e-02, 'a_w_out': 5.305896e-02, 'b_w_in': 2.299733e-02, 'b_gate_w2': 2.538830e-03, 'b_gate_b': 8.962032e-03, 'b_norm_g': 3.820773e-02, 'b_w_out': 4.643683e-02, 'ln1_g': 4.868812e-01, 'ln1_b': 2.930061e-01, 'mlp_w1': 2.276899e-02, 'mlp_w2': 1.160560e-01, 'ln2_g': 8.050704e+00, 'ln2_b': 1.842525e+00}


def _to_microbatches(a, axis):
    t = _jnp.moveaxis(a, axis, 0)
    t = t.reshape((N_MICROBATCH, t.shape[0] // N_MICROBATCH) + t.shape[1:])
    return _jnp.moveaxis(t, 1, axis + 1)


def setup_inputs(seed: int = 0) -> dict:
    inp = _fwd_setup_inputs(seed)
    key = _jax.random.fold_in(_jax.random.key(seed), 7919)
    shape, _ = _output_shape()
    out = dict(inp)
    out["loss_target"] = _jax.random.normal(_jax.random.fold_in(key, 0), shape, _jnp.float32)
    for i, name in enumerate(TWIN_WEIGHTS):
        w = inp[name].astype(_jnp.float32)
        if MOMENT_SCALE is None:
            s = _jnp.sqrt(_jnp.mean(_jnp.square(w)) + 1e-30)
        else:
            s = MOMENT_SCALE[name]
        km, kv = _jax.random.split(_jax.random.fold_in(key, i + 1))
        out[name] = w
        out["m_" + name] = s * _jax.random.normal(km, w.shape, _jnp.float32)
        out["v_" + name] = (s * s) * _jax.random.uniform(kv, w.shape, _jnp.float32, 0.5, 1.5)
    if N_MICROBATCH > 1:
        for name, axis in PER_EXAMPLE_BATCH_AXIS.items():
            out[name] = _to_microbatches(out[name], axis)
    return {'x': out['x'], 'a_w_in': out['a_w_in'], 'a_conv': out['a_conv'], 'a_alog': out['a_alog'], 'a_dt_bias': out['a_dt_bias'], 'a_norm_g': out['a_norm_g'], 'a_w_out': out['a_w_out'], 'b_w_in': out['b_w_in'], 'b_gate_w2': out['b_gate_w2'], 'b_gate_b': out['b_gate_b'], 'b_norm_g': out['b_norm_g'], 'b_w_out': out['b_w_out'], 'ln1_g': out['ln1_g'], 'ln1_b': out['ln1_b'], 'mlp_w1': out['mlp_w1'], 'mlp_w2': out['mlp_w2'], 'ln2_g': out['ln2_g'], 'ln2_b': out['ln2_b'], 'loss_target': out['loss_target'], 'm_a_w_in': out['m_a_w_in'], 'm_a_conv': out['m_a_conv'], 'm_a_alog': out['m_a_alog'], 'm_a_dt_bias': out['m_a_dt_bias'], 'm_a_norm_g': out['m_a_norm_g'], 'm_a_w_out': out['m_a_w_out'], 'm_b_w_in': out['m_b_w_in'], 'm_b_gate_w2': out['m_b_gate_w2'], 'm_b_gate_b': out['m_b_gate_b'], 'm_b_norm_g': out['m_b_norm_g'], 'm_b_w_out': out['m_b_w_out'], 'm_ln1_g': out['m_ln1_g'], 'm_ln1_b': out['m_ln1_b'], 'm_mlp_w1': out['m_mlp_w1'], 'm_mlp_w2': out['m_mlp_w2'], 'm_ln2_g': out['m_ln2_g'], 'm_ln2_b': out['m_ln2_b'], 'v_a_w_in': out['v_a_w_in'], 'v_a_conv': out['v_a_conv'], 'v_a_alog': out['v_a_alog'], 'v_a_dt_bias': out['v_a_dt_bias'], 'v_a_norm_g': out['v_a_norm_g'], 'v_a_w_out': out['v_a_w_out'], 'v_b_w_in': out['v_b_w_in'], 'v_b_gate_w2': out['v_b_gate_w2'], 'v_b_gate_b': out['v_b_gate_b'], 'v_b_norm_g': out['v_b_norm_g'], 'v_b_w_out': out['v_b_w_out'], 'v_ln1_g': out['v_ln1_g'], 'v_ln1_b': out['v_ln1_b'], 'v_mlp_w1': out['v_mlp_w1'], 'v_mlp_w2': out['v_mlp_w2'], 'v_ln2_g': out['v_ln2_g'], 'v_ln2_b': out['v_ln2_b']}


def _loss(weights, diff, rest, loss_target):
    with _jax.named_scope("forward"):
        args = {**rest, TWIN_DIFF_INPUT: diff, **{k: w.astype(_WEIGHT_DTYPES[k]) for k, w in weights.items()}}
        y = _forward(args)
    with _jax.named_scope("loss_head"):
        err = _jnp.square(y.astype(_jnp.float32) - loss_target)
        return 0.5 * _jnp.sum(_jnp.mean(err, axis=-1)) if err.ndim else 0.5 * err


def _adamw(w, g, m, v):
    m = ADAM_B1 * m + (1.0 - ADAM_B1) * g
    v = ADAM_B2 * v + (1.0 - ADAM_B2) * _jnp.square(g)
    m_hat = m / (1.0 - ADAM_B1 ** ADAM_STEP)
    v_hat = v / (1.0 - ADAM_B2 ** ADAM_STEP)
    delta = -ADAM_LR * (m_hat / (_jnp.sqrt(v_hat) + ADAM_EPS) + ADAM_WD * w)
    return delta, m, v


def reference(x, a_w_in, a_conv, a_alog, a_dt_bias, a_norm_g, a_w_out, b_w_in, b_gate_w2, b_gate_b, b_norm_g, b_w_out, ln1_g, ln1_b, mlp_w1, mlp_w2, ln2_g, ln2_b, loss_target, m_a_w_in, m_a_conv, m_a_alog, m_a_dt_bias, m_a_norm_g, m_a_w_out, m_b_w_in, m_b_gate_w2, m_b_gate_b, m_b_norm_g, m_b_w_out, m_ln1_g, m_ln1_b, m_mlp_w1, m_mlp_w2, m_ln2_g, m_ln2_b, v_a_w_in, v_a_conv, v_a_alog, v_a_dt_bias, v_a_norm_g, v_a_w_out, v_b_w_in, v_b_gate_w2, v_b_gate_b, v_b_norm_g, v_b_w_out, v_ln1_g, v_ln1_b, v_mlp_w1, v_mlp_w2, v_ln2_g, v_ln2_b):
    given = dict(x=x, a_w_in=a_w_in, a_conv=a_conv, a_alog=a_alog, a_dt_bias=a_dt_bias, a_norm_g=a_norm_g, a_w_out=a_w_out, b_w_in=b_w_in, b_gate_w2=b_gate_w2, b_gate_b=b_gate_b, b_norm_g=b_norm_g, b_w_out=b_w_out, ln1_g=ln1_g, ln1_b=ln1_b, mlp_w1=mlp_w1, mlp_w2=mlp_w2, ln2_g=ln2_g, ln2_b=ln2_b, loss_target=loss_target, m_a_w_in=m_a_w_in, m_a_conv=m_a_conv, m_a_alog=m_a_alog, m_a_dt_bias=m_a_dt_bias, m_a_norm_g=m_a_norm_g, m_a_w_out=m_a_w_out, m_b_w_in=m_b_w_in, m_b_gate_w2=m_b_gate_w2, m_b_gate_b=m_b_gate_b, m_b_norm_g=m_b_norm_g, m_b_w_out=m_b_w_out, m_ln1_g=m_ln1_g, m_ln1_b=m_ln1_b, m_mlp_w1=m_mlp_w1, m_mlp_w2=m_mlp_w2, m_ln2_g=m_ln2_g, m_ln2_b=m_ln2_b, v_a_w_in=v_a_w_in, v_a_conv=v_a_conv, v_a_alog=v_a_alog, v_a_dt_bias=v_a_dt_bias, v_a_norm_g=v_a_norm_g, v_a_w_out=v_a_w_out, v_b_w_in=v_b_w_in, v_b_gate_w2=v_b_gate_w2, v_b_gate_b=v_b_gate_b, v_b_norm_g=v_b_norm_g, v_b_w_out=v_b_w_out, v_ln1_g=v_ln1_g, v_ln1_b=v_ln1_b, v_mlp_w1=v_mlp_w1, v_mlp_w2=v_mlp_w2, v_ln2_g=v_ln2_g, v_ln2_b=v_ln2_b)
    weights = {n: given[n] for n in TWIN_WEIGHTS}
    shared = {n: given[n] for n in SHARED_INPUTS}
    per_example = {n: given[n] for n in ['x']}
    grad_fn = _jax.value_and_grad(_loss, argnums=(0, 1))

    def one_microbatch(ex, loss_target):
        ex = dict(ex)
        diff = ex.pop(TWIN_DIFF_INPUT)
        return grad_fn(weights, diff, {**shared, **ex}, loss_target)

    if N_MICROBATCH == 1:
        loss, (grad_w, grad_x) = one_microbatch(per_example, given["loss_target"])
    else:
        def body(carry, xs):
            loss_sum, grad_sum = carry
            l_k, (gw_k, gx_k) = one_microbatch(xs[0], xs[1])
            with _jax.named_scope("update"):
                return (loss_sum + l_k, _jax.tree.map(_jnp.add, grad_sum, gw_k)), gx_k

        init = (_jnp.zeros((), _jnp.float32), _jax.tree.map(_jnp.zeros_like, weights))
        (loss, grad_w), grad_x = _jax.lax.scan(body, init, (per_example, given["loss_target"]))
    with _jax.named_scope("update"):
        delta_w, new_m, new_v = {}, {}, {}
        for n in TWIN_WEIGHTS:
            delta_w[n], new_m[n], new_v[n] = _adamw(weights[n], grad_w[n], given["m_" + n], given["v_" + n])
    return (loss, grad_x, *[grad_w[n] for n in TWIN_WEIGHTS], *[delta_w[n] for n in TWIN_WEIGHTS],
            *[new_m[n] for n in TWIN_WEIGHTS], *[new_v[n] for n in TWIN_WEIGHTS])
```

```python
import functools
import math

import jax
import jax.numpy as jnp
from jax import lax
from jax.experimental import pallas as pl
from jax.experimental.pallas import tpu as pltpu

F32 = jnp.float32
BF16 = jnp.bfloat16

D_MODEL = 1024
DEPTH = 4
CHUNK = 64
A_HEADS = 8
A_DK = 128
A_W = 1024
A_CONV = 5
B_HEADS = 4
B_DK = 128
B_DV = 256
B_RANK = 16
B_TAU = 16.0
B_KW = 512
B_VW = 1024
ALPHA = (2 * DEPTH) ** 0.25
LN_EPS = 1e-5
RMS_EPS = 1e-6
L2_EPS = 1e-6
ADAM_LR = 0.001
ADAM_B1 = 0.9
ADAM_B2 = 0.999
ADAM_EPS = 1e-08
ADAM_WD = 0.01
ADAM_STEP = 10
LANES = 128
NEG_INF = float("-inf")
VMEM_LIMIT = 56 * 1024 * 1024


def _cparams(sem=None):
    return pltpu.CompilerParams(dimension_semantics=sem, vmem_limit_bytes=VMEM_LIMIT)


def _dg(a, b, ca, cb):
    return lax.dot_general(a.astype(BF16), b.astype(BF16), (((ca,), (cb,)), ((), ())),
                           preferred_element_type=F32)


@functools.partial(jax.custom_vjp, nondiff_argnums=(2, 3))
def bdot(a, b, ca, cb):
    return _dg(a, b, ca, cb)


def _bdot_fwd(a, b, ca, cb):
    return _dg(a, b, ca, cb), (a, b)


def _bdot_bwd(ca, cb, res, g):
    a, b = res
    da = _dg(g, b, 1, 1 - cb) if ca == 1 else _dg(b, g, 1 - cb, 1)
    db = _dg(a, g, 1 - ca, 0) if cb == 0 else _dg(g, a, 0, 1 - ca)
    return da, db


bdot.defvjp(_bdot_fwd, _bdot_bwd)


def nn(a, b):
    return bdot(a, b, 1, 0)


def nt(a, b):
    return bdot(a, b, 1, 1)


def tn(a, b):
    return bdot(a, b, 0, 0)


def xdot(a, b):
    return jnp.dot(a, b, precision=lax.Precision.HIGHEST, preferred_element_type=F32)


def _sigmoid(x):
    return 1.0 / (1.0 + jnp.exp(-x))


def _softplus(x):
    return jnp.maximum(x, 0.0) + jnp.log(1.0 + jnp.exp(-jnp.abs(x)))


def _chunk_masks(rev):
    ii = lax.broadcasted_iota(jnp.int32, (CHUNK, CHUNK), 0)
    jj = lax.broadcasted_iota(jnp.int32, (CHUNK, CHUNK), 1)
    d = (ii - jj) * (1 - 2 * rev)
    return d >= 0, d > 0, ii == jj, (ii >> 3) == (jj >> 3)


def _unit_triangular_inverse(a, eye, blockdiag):
    ident = eye.astype(F32)
    ad = jnp.where(blockdiag, a, 0.0)
    e = a - ad
    dinv = ident - ad
    p = xdot(ad, ad)
    dinv = dinv + xdot(dinv, p)
    p = xdot(p, p)
    dinv = dinv + xdot(dinv, p)
    g = -xdot(dinv, e)
    finv = ident + g
    p = xdot(g, g)
    finv = finv + xdot(finv, p)
    p = xdot(p, p)
    finv = finv + xdot(finv, p)
    return xdot(finv, dinv)


def _gdn_step(state, q, k, v, bb, gb, rev):
    causal, strict, eye, blockdiag = _chunk_masks(rev)
    gcb = xdot(causal.astype(F32), gb)
    gcol = gcb[:, :CHUNK]
    grow = xdot(jnp.ones((CHUNK, CHUNK), F32), jnp.where(eye, gcol, 0.0))
    decay = jnp.exp(jnp.where(causal, gcol - grow, NEG_INF))
    kb = k * bb
    a = jnp.where(strict, nt(kb, k) * decay, 0.0)
    t = _unit_triangular_inverse(a, eye, blockdiag)
    egc = jnp.exp(gcb)
    u = xdot(t, v * bb)
    w = xdot(t, kb * egc)
    qk = nt(q, k) * decay
    glast = jnp.sum(gb, axis=0, keepdims=True)
    v_new = u - nn(w, state)
    o = nn(q * egc, state) + nn(qk, v_new)
    state_new = state * jnp.exp(glast) + tn(k * jnp.exp(glast - gcb), v_new)
    return state_new, o


def _gla_step(state_t, q, k, v, la, rev):
    causal, _, _, _ = _chunk_masks(rev)
    b = xdot(causal.astype(F32), la)
    q = q * (B_DK ** -0.5)
    shp = (CHUNK, CHUNK, B_DK)
    d3 = (lax.broadcasted_iota(jnp.int32, shp, 0) - lax.broadcasted_iota(jnp.int32, shp, 1)) * (1 - 2 * rev)
    dec = jnp.exp(jnp.where(d3 >= 0, b[:, None, :] - b[None, :, :], NEG_INF))
    scores = jnp.sum(q[:, None, :] * k[None, :, :] * dec, axis=-1)
    blast = jnp.sum(la, axis=0, keepdims=True)
    o = nt(q * jnp.exp(b), state_t) + nn(scores, v)
    state_new = jnp.exp(blast) * state_t + tn(v, k * jnp.exp(blast - b))
    return state_new, o


def _chunk_pos(d, m, n):
    return m + d * (n - 1 - 2 * m)


def gdn_rec_fwd(q, k, v, beta_b, g_b):
    s = q.shape[0]
    n = s // CHUNK

    def body(q_ref, k_ref, v_ref, bb_ref, gb_ref, o_ref, st_ref, state):
        d = pl.program_id(0)

        @pl.when(pl.program_id(2) == 0)
        def _():
            state[...] = jnp.zeros_like(state)

        st = state[...]
        st_ref[...] = st
        new, o = _gdn_step(st, q_ref[...], k_ref[...], v_ref[...], bb_ref[...], gb_ref[...], d)
        state[...] = new
        o_ref[...] = o

    blk = pl.BlockSpec((CHUNK, LANES), lambda d, h, m: (_chunk_pos(d, m, n), h))
    gate = pl.BlockSpec((CHUNK, LANES), lambda d, h, m: (_chunk_pos(d, m, n), d * A_HEADS + h))
    return pl.pallas_call(
        body, name="gdn_rec_fwd", grid=(2, A_HEADS, n),
        in_specs=[blk, blk, blk, gate, gate],
        out_specs=[pl.BlockSpec((None, CHUNK, LANES), lambda d, h, m: (d, _chunk_pos(d, m, n), h)),
                   pl.BlockSpec((None, None, None, A_DK, LANES), lambda d, h, m: (d, h, _chunk_pos(d, m, n), 0, 0))],
        out_shape=[jax.ShapeDtypeStruct((2, s, A_W), F32), jax.ShapeDtypeStruct((2, A_HEADS, n, A_DK, LANES), F32)],
        scratch_shapes=[pltpu.VMEM((A_DK, LANES), F32)],
        compiler_params=_cparams(("arbitrary", "arbitrary", "arbitrary")),
    )(q, k, v, beta_b, g_b)


def gdn_rec_bwd(q, k, v, beta_b, g_b, states, do):
    s = q.shape[0]
    n = s // CHUNK

    def body(q_ref, k_ref, v_ref, bb_ref, gb_ref, st_ref, do_ref, dq_ref, dk_ref, dv_ref, dbb_ref, dgb_ref, dstate):
        d = pl.program_id(0)

        @pl.when(pl.program_id(2) == 0)
        def _():
            dstate[...] = jnp.zeros_like(dstate)

        step = functools.partial(_gdn_step, rev=d)
        _, vjp = jax.vjp(step, st_ref[...], q_ref[...], k_ref[...], v_ref[...], bb_ref[...], gb_ref[...])
        dst, dq, dk, dv, dbb, dgb = vjp((dstate[...], do_ref[...]))
        dstate[...] = dst
        dq_ref[...] = dq
        dk_ref[...] = dk
        dv_ref[...] = dv
        dbb_ref[...] = dbb
        dgb_ref[...] = dgb

    pos = lambda d, m: _chunk_pos(1 - d, m, n)
    blk = pl.BlockSpec((CHUNK, LANES), lambda d, h, m: (pos(d, m), h))
    gate = pl.BlockSpec((CHUNK, LANES), lambda d, h, m: (pos(d, m), d * A_HEADS + h))
    oblk = pl.BlockSpec((None, CHUNK, LANES), lambda d, h, m: (d, pos(d, m), h))
    return pl.pallas_call(
        body, name="gdn_rec_bwd", grid=(2, A_HEADS, n),
        in_specs=[blk, blk, blk, gate, gate,
                  pl.BlockSpec((None, None, None, A_DK, LANES), lambda d, h, m: (d, h, pos(d, m), 0, 0)), blk],
        out_specs=[oblk, oblk, oblk, gate, gate],
        out_shape=[jax.ShapeDtypeStruct((2, s, A_W), F32)] * 3 + [jax.ShapeDtypeStruct(beta_b.shape, F32)] * 2,
        scratch_shapes=[pltpu.VMEM((A_DK, LANES), F32)],
        compiler_params=_cparams(("arbitrary", "arbitrary", "arbitrary")),
    )(q, k, v, beta_b, g_b, states, do)


def gla_rec_fwd(proj, log_a):
    s = proj.shape[0]
    n = s // CHUNK

    def body(q_ref, k_ref, v_ref, la_ref, o_ref, st_ref, state):
        d = pl.program_id(0)

        @pl.when(pl.program_id(2) == 0)
        def _():
            state[...] = jnp.zeros_like(state)

        st = state[...]
        st_ref[...] = st
        new, o = _gla_step(st, q_ref[...], k_ref[...], v_ref[...], la_ref[...], d)
        state[...] = new
        o_ref[...] = o

    return pl.pallas_call(
        body, name="gla_rec_fwd", grid=(2, B_HEADS, n),
        in_specs=[pl.BlockSpec((CHUNK, B_DK), lambda d, h, m: (_chunk_pos(d, m, n), h)),
                  pl.BlockSpec((CHUNK, B_DK), lambda d, h, m: (_chunk_pos(d, m, n), B_KW // B_DK + h)),
                  pl.BlockSpec((CHUNK, B_DV), lambda d, h, m: (_chunk_pos(d, m, n), 2 * B_KW // B_DV + h)),
                  pl.BlockSpec((None, CHUNK, B_DK), lambda d, h, m: (d, _chunk_pos(d, m, n), h))],
        out_specs=[pl.BlockSpec((None, CHUNK, B_DV), lambda d, h, m: (d, _chunk_pos(d, m, n), h)),
                   pl.BlockSpec((None, None, None, B_DV, B_DK), lambda d, h, m: (d, h, _chunk_pos(d, m, n), 0, 0))],
        out_shape=[jax.ShapeDtypeStruct((2, s, B_VW), F32), jax.ShapeDtypeStruct((2, B_HEADS, n, B_DV, B_DK), F32)],
        scratch_shapes=[pltpu.VMEM((B_DV, B_DK), F32)],
        compiler_params=_cparams(("arbitrary", "arbitrary", "arbitrary")),
    )(proj, proj, proj, log_a)


def gla_rec_bwd(proj, log_a, states, do):
    s = proj.shape[0]
    n = s // CHUNK

    def body(q_ref, k_ref, v_ref, la_ref, st_ref, do_ref, dq_ref, dk_ref, dv_ref, dla_ref, dstate):
        d = pl.program_id(0)

        @pl.when(pl.program_id(2) == 0)
        def _():
            dstate[...] = jnp.zeros_like(dstate)

        step = functools.partial(_gla_step, rev=d)
        _, vjp = jax.vjp(step, st_ref[...], q_ref[...], k_ref[...], v_ref[...], la_ref[...])
        dst, dq, dk, dv, dla = vjp((dstate[...], do_ref[...]))
        dstate[...] = dst
        dq_ref[...] = dq
        dk_ref[...] = dk
        dv_ref[...] = dv
        dla_ref[...] = dla

    pos = lambda d, m: _chunk_pos(1 - d, m, n)
    kblk = pl.BlockSpec((None, CHUNK, B_DK), lambda d, h, m: (d, pos(d, m), h))
    return pl.pallas_call(
        body, name="gla_rec_bwd", grid=(2, B_HEADS, n),
        in_specs=[pl.BlockSpec((CHUNK, B_DK), lambda d, h, m: (pos(d, m), h)),
                  pl.BlockSpec((CHUNK, B_DK), lambda d, h, m: (pos(d, m), B_KW // B_DK + h)),
                  pl.BlockSpec((CHUNK, B_DV), lambda d, h, m: (pos(d, m), 2 * B_KW // B_DV + h)),
                  kblk,
                  pl.BlockSpec((None, None, None, B_DV, B_DK), lambda d, h, m: (d, h, pos(d, m), 0, 0)),
                  pl.BlockSpec((CHUNK, B_DV), lambda d, h, m: (pos(d, m), h))],
        out_specs=[kblk, kblk, pl.BlockSpec((None, CHUNK, B_DV), lambda d, h, m: (d, pos(d, m), h)), kblk],
        out_shape=[jax.ShapeDtypeStruct((2, s, B_KW), F32), jax.ShapeDtypeStruct((2, s, B_KW), F32),
                   jax.ShapeDtypeStruct((2, s, B_VW), F32), jax.ShapeDtypeStruct((2, s, B_KW), F32)],
        scratch_shapes=[pltpu.VMEM((B_DV, B_DK), F32)],
        compiler_params=_cparams(("arbitrary", "arbitrary", "arbitrary")),
    )(proj, proj, proj, log_a, states, do)


MM_TILE = 512


def _tile(n, pref):
    return pref if n % pref == 0 else n


def mm(a, b, mode="nn", act=None, epi=None, extra=None, alpha=1.0, name="mm"):
    if mode == "tn":
        kk, m = a.shape
    else:
        m, kk = a.shape
    nn_ = b.shape[0] if mode == "nt" else b.shape[1]
    tm, tn_, tk = _tile(m, MM_TILE), _tile(nn_, MM_TILE), _tile(kk, MM_TILE)
    nk = kk // tk
    ca, cb = {"nn": (1, 0), "nt": (1, 1), "tn": (0, 0)}[mode]

    def body(*refs):
        if epi is None:
            a_ref, b_ref, o_ref = refs
        else:
            a_ref, b_ref, e_ref, o_ref = refs
        kstep = pl.program_id(2)
        at = a_ref[...]
        if act == "sqrelu":
            at = jnp.square(jnp.maximum(at, 0.0))
        part = _dg(at, b_ref[...], ca, cb)

        @pl.when(kstep == 0)
        def _():
            o_ref[...] = part

        @pl.when(kstep > 0)
        def _():
            o_ref[...] += part

        if epi is not None:
            @pl.when(kstep == nk - 1)
            def _():
                if epi == "dsqrelu":
                    o_ref[...] = o_ref[...] * (2.0 * jnp.maximum(e_ref[...], 0.0))
                else:
                    o_ref[...] = o_ref[...] + alpha * e_ref[...]

    a_spec = pl.BlockSpec((tk, tm), lambda i, j, k: (k, i)) if mode == "tn" else pl.BlockSpec((tm, tk), lambda i, j, k: (i, k))
    b_spec = pl.BlockSpec((tn_, tk), lambda i, j, k: (j, k)) if mode == "nt" else pl.BlockSpec((tk, tn_), lambda i, j, k: (k, j))
    o_spec = pl.BlockSpec((tm, tn_), lambda i, j, k: (i, j))
    ins, specs = [a, b], [a_spec, b_spec]
    if epi is not None:
        ins.append(extra)
        specs.append(o_spec)
    return pl.pallas_call(
        body, name=name, grid=(m // tm, nn_ // tn_, nk), in_specs=specs, out_specs=o_spec,
        out_shape=jax.ShapeDtypeStruct((m, nn_), F32),
        compiler_params=_cparams(("parallel", "parallel", "arbitrary")),
    )(*ins)


ROWS = 256


def _ln_core(x, m, g, b):
    r = ALPHA * x + m
    mu = jnp.mean(r, axis=-1, keepdims=True)
    xc = r - mu
    var = jnp.mean(xc * xc, axis=-1, keepdims=True)
    rstd = lax.rsqrt(var + LN_EPS)
    xhat = xc * rstd
    return xhat * g + b, xhat, rstd


def ln_fwd(x, m, g, b):
    s, dm = x.shape

    def body(x_ref, m_ref, g_ref, b_ref, o_ref):
        o_ref[...] = _ln_core(x_ref[...], m_ref[...], g_ref[...], b_ref[...])[0]

    row = pl.BlockSpec((ROWS, dm), lambda i: (i, 0))
    vec = pl.BlockSpec((1, dm), lambda i: (0, 0))
    return pl.pallas_call(body, name="ln_fwd", grid=(s // ROWS,), in_specs=[row, row, vec, vec], out_specs=row,
                          out_shape=jax.ShapeDtypeStruct((s, dm), F32), compiler_params=_cparams(("parallel",)))(x, m, g, b)


def ln_bwd(x, m, g, dy):
    s, dm = x.shape

    def body(x_ref, m_ref, g_ref, dy_ref, dr_ref, dg_ref, db_ref):
        gv = g_ref[...]
        _, xhat, rstd = _ln_core(x_ref[...], m_ref[...], gv, jnp.zeros_like(gv))
        dy = dy_ref[...]
        dxh = dy * gv
        dr_ref[...] = rstd * (dxh - jnp.mean(dxh, axis=-1, keepdims=True)
                              - xhat * jnp.mean(dxh * xhat, axis=-1, keepdims=True))

        @pl.when(pl.program_id(0) == 0)
        def _():
            dg_ref[...] = jnp.zeros_like(dg_ref)
            db_ref[...] = jnp.zeros_like(db_ref)

        dg_ref[...] += jnp.sum(dy * xhat, axis=0, keepdims=True)
        db_ref[...] += jnp.sum(dy, axis=0, keepdims=True)

    row = pl.BlockSpec((ROWS, dm), lambda i: (i, 0))
    vec = pl.BlockSpec((1, dm), lambda i: (0, 0))
    return pl.pallas_call(body, name="ln_bwd", grid=(s // ROWS,), in_specs=[row, row, vec, row], out_specs=[row, vec, vec],
                          out_shape=[jax.ShapeDtypeStruct((s, dm), F32), jax.ShapeDtypeStruct((1, dm), F32),
                                     jax.ShapeDtypeStruct((1, dm), F32)],
                          compiler_params=_cparams(("arbitrary",)))(x, m, g, dy)


def loss_head(y, target):
    s, dm = y.shape

    def body(y_ref, t_ref, dy_ref, l_ref):
        e = y_ref[...] - t_ref[...]
        dy_ref[...] = e * (1.0 / dm)

        @pl.when(pl.program_id(0) == 0)
        def _():
            l_ref[...] = jnp.zeros_like(l_ref)

        col = jnp.sum(e * e, axis=0, keepdims=True) * (0.5 / dm)
        acc = col[:, :LANES]
        for c in range(1, dm // LANES):
            acc = acc + col[:, c * LANES:(c + 1) * LANES]
        l_ref[...] += acc

    row = pl.BlockSpec((ROWS, dm), lambda i: (i, 0))
    return pl.pallas_call(body, name="loss_head", grid=(s // ROWS,), in_specs=[row, row],
                          out_specs=[row, pl.BlockSpec((1, LANES), lambda i: (0, 0))],
                          out_shape=[jax.ShapeDtypeStruct((s, dm), F32), jax.ShapeDtypeStruct((1, LANES), F32)],
                          compiler_params=_cparams(("arbitrary",)))(y, target)


def _shift_rows_impl(x, d):
    n = x.shape[0]
    if d == 0:
        return x
    t = lax.broadcasted_iota(jnp.int32, x.shape, 0)
    return jnp.where((t + d >= 0) & (t + d < n), pltpu.roll(x, (-d) % n, 0), 0.0)


@functools.partial(jax.custom_vjp, nondiff_argnums=(1,))
def _shift_rows(x, d):
    return _shift_rows_impl(x, d)


_shift_rows.defvjp(lambda x, d: (_shift_rows_impl(x, d), None), lambda d, _, g: (_shift_rows_impl(g, -d),))


def _gdn_pre_fn(u, w, kind):
    rows = lax.broadcasted_iota(jnp.int32, w.shape, 0)
    c = None
    for tap in range(A_CONV):
        w_tap = jnp.sum(jnp.where(rows == tap, w, 0.0), axis=0, keepdims=True)
        term = _shift_rows(u, tap - A_CONV // 2) * w_tap
        c = term if c is None else c + term
    y = c * _sigmoid(c)
    if kind == "v":
        return y
    y = y * lax.rsqrt(jnp.sum(y * y, axis=-1, keepdims=True) + L2_EPS)
    return y * (A_DK ** -0.5) if kind == "q" else y


_KIND_OFF = {"q": 0, "k": A_HEADS, "v": 2 * A_HEADS}


def gdn_pre(proj, conv_w, kind):
    s = proj.shape[0]
    off = _KIND_OFF[kind]

    def body(u_ref, w_ref, o_ref):
        o_ref[...] = _gdn_pre_fn(u_ref[...], w_ref[...], kind)

    return pl.pallas_call(
        body, name="gdn_pre_" + kind, grid=(A_HEADS,),
        in_specs=[pl.BlockSpec((s, LANES), lambda h: (0, off + h)), pl.BlockSpec((A_CONV, LANES), lambda h: (0, off + h))],
        out_specs=pl.BlockSpec((s, LANES), lambda h: (0, h)),
        out_shape=jax.ShapeDtypeStruct((s, A_W), F32), compiler_params=_cparams(("parallel",)))(proj, conv_w)


def gdn_pre_bwd(proj, conv_w, dt2, kind):
    s = proj.shape[0]
    off = _KIND_OFF[kind]

    def body(u_ref, w_ref, d0_ref, d1_ref, du_ref, dw_ref):
        _, vjp = jax.vjp(functools.partial(_gdn_pre_fn, kind=kind), u_ref[...], w_ref[...])
        du, dw = vjp(d0_ref[...] + d1_ref[...])
        du_ref[...] = du
        dw_ref[...] = dw

    return pl.pallas_call(
        body, name="gdn_pre_bwd_" + kind, grid=(A_HEADS,),
        in_specs=[pl.BlockSpec((s, LANES), lambda h: (0, off + h)), pl.BlockSpec((A_CONV, LANES), lambda h: (0, off + h)),
                  pl.BlockSpec((None, s, LANES), lambda h: (0, 0, h)), pl.BlockSpec((None, s, LANES), lambda h: (1, 0, h))],
        out_specs=[pl.BlockSpec((s, LANES), lambda h: (0, h)), pl.BlockSpec((A_CONV, LANES), lambda h: (0, h))],
        out_shape=[jax.ShapeDtypeStruct((s, A_W), F32), jax.ShapeDtypeStruct((A_CONV, A_W), F32)],
        compiler_params=_cparams(("parallel",)))(proj, conv_w, dt2, dt2)


N_GATE = 2 * A_HEADS


def _gdn_gates_fn(ba, alog_row, dt_row):
    r = lax.broadcasted_iota(jnp.int32, (LANES, N_GATE * LANES), 0)
    c = lax.broadcasted_iota(jnp.int32, (LANES, N_GATE * LANES), 1) >> 7
    beta_b = xdot(_sigmoid(ba), (r == c).astype(F32))
    g = -(jnp.exp(alog_row) * _softplus(ba + dt_row))
    g_b = xdot(g, (r == c + N_GATE).astype(F32))
    return beta_b, g_b


def gdn_gates(ba, alog_row, dt_row):
    s = ba.shape[0]

    def body(ba_ref, al_ref, dt_ref, bb_ref, gb_ref):
        bb_ref[...], gb_ref[...] = _gdn_gates_fn(ba_ref[...], al_ref[...], dt_ref[...])

    row = pl.BlockSpec((ROWS, LANES), lambda i: (i, 0))
    vec = pl.BlockSpec((1, LANES), lambda i: (0, 0))
    wide = pl.BlockSpec((ROWS, N_GATE * LANES), lambda i: (i, 0))
    return pl.pallas_call(body, name="gdn_gates", grid=(s // ROWS,), in_specs=[row, vec, vec], out_specs=[wide, wide],
                          out_shape=[jax.ShapeDtypeStruct((s, N_GATE * LANES), F32)] * 2,
                          compiler_params=_cparams(("parallel",)))(ba, alog_row, dt_row)


def gdn_gates_bwd(ba, alog_row, dt_row, dbeta_b, dg_b):
    s = ba.shape[0]

    def body(ba_ref, al_ref, dt_ref, dbb_ref, dgb_ref, dba_ref, dal_ref, ddt_ref):
        _, vjp = jax.vjp(_gdn_gates_fn, ba_ref[...], al_ref[...], dt_ref[...])
        dba, dal, ddt = vjp((dbb_ref[...], dgb_ref[...]))
        dba_ref[...] = dba

        @pl.when(pl.program_id(0) == 0)
        def _():
            dal_ref[...] = jnp.zeros_like(dal_ref)
            ddt_ref[...] = jnp.zeros_like(ddt_ref)

        dal_ref[...] += dal
        ddt_ref[...] += ddt

    row = pl.BlockSpec((ROWS, LANES), lambda i: (i, 0))
    vec = pl.BlockSpec((1, LANES), lambda i: (0, 0))
    wide = pl.BlockSpec((ROWS, N_GATE * LANES), lambda i: (i, 0))
    return pl.pallas_call(body, name="gdn_gates_bwd", grid=(s // ROWS,), in_specs=[row, vec, vec, wide, wide],
                          out_specs=[row, vec, vec],
                          out_shape=[jax.ShapeDtypeStruct((s, LANES), F32), jax.ShapeDtypeStruct((1, LANES), F32),
                                     jax.ShapeDtypeStruct((1, LANES), F32)],
                          compiler_params=_cparams(("arbitrary",)))(ba, alog_row, dt_row, dbeta_b, dg_b)


def _post_fn(o, z, g):
    y = o * lax.rsqrt(jnp.mean(o * o, axis=-1, keepdims=True) + RMS_EPS) * g
    return y * (z * _sigmoid(z))


def mixer_post(o2, proj, norm_g, width, gate_off, name):
    s = o2.shape[1]
    nh = o2.shape[2] // width

    def body(o0_ref, o1_ref, z_ref, g_ref, y_ref):
        y_ref[...] = _post_fn(o0_ref[...] + o1_ref[...], z_ref[...], g_ref[...])

    ospec = lambda d: pl.BlockSpec((None, ROWS, width), lambda i, h: (d, i, h))
    return pl.pallas_call(
        body, name=name, grid=(s // ROWS, nh),
        in_specs=[ospec(0), ospec(1), pl.BlockSpec((ROWS, width), lambda i, h: (i, gate_off + h)),
                  pl.BlockSpec((1, width), lambda i, h: (0, 0))],
        out_specs=pl.BlockSpec((ROWS, width), lambda i, h: (i, h)),
        out_shape=jax.ShapeDtypeStruct((s, o2.shape[2]), F32),
        compiler_params=_cparams(("parallel", "parallel")))(o2, o2, proj, norm_g)


def mixer_post_bwd(o2, proj, norm_g, dy, width, gate_off, name):
    s = o2.shape[1]
    nh = o2.shape[2] // width

    def body(o0_ref, o1_ref, z_ref, g_ref, dy_ref, do_ref, dz_ref, dg_ref):
        _, vjp = jax.vjp(_post_fn, o0_ref[...] + o1_ref[...], z_ref[...], g_ref[...])
        do, dz, dg = vjp(dy_ref[...])
        do_ref[...] = do
        dz_ref[...] = dz

        @pl.when((pl.program_id(0) == 0) & (pl.program_id(1) == 0))
        def _():
            dg_ref[...] = jnp.zeros_like(dg_ref)

        dg_ref[...] += dg

    ospec = lambda d: pl.BlockSpec((None, ROWS, width), lambda i, h: (d, i, h))
    blk = pl.BlockSpec((ROWS, width), lambda i, h: (i, h))
    vec = pl.BlockSpec((1, width), lambda i, h: (0, 0))
    return pl.pallas_call(
        body, name=name, grid=(s // ROWS, nh),
        in_specs=[ospec(0), ospec(1), pl.BlockSpec((ROWS, width), lambda i, h: (i, gate_off + h)), vec, blk],
        out_specs=[blk, blk, vec],
        out_shape=[jax.ShapeDtypeStruct((s, o2.shape[2]), F32)] * 2 + [jax.ShapeDtypeStruct((1, width), F32)],
        compiler_params=_cparams(("arbitrary", "arbitrary")))(o2, o2, proj, norm_g, dy)


def _log_gate(z):
    return (jnp.minimum(z, 0.0) - jnp.log(1.0 + jnp.exp(-jnp.abs(z)))) * (1.0 / B_TAU)


def gla_gate(gl, w2, gb):
    s = gl.shape[0]

    def body(gl_ref, w_ref, b_ref, o_ref):
        for n in range(2):
            o_ref[n] = _log_gate(nn(gl_ref[...], w_ref[n]) + b_ref[n])

    full = lambda shp: pl.BlockSpec(shp, lambda i: (0,) * len(shp))
    return pl.pallas_call(
        body, name="gla_gate", grid=(s // ROWS,),
        in_specs=[pl.BlockSpec((ROWS, LANES), lambda i: (i, 0)), full(w2.shape), full(gb.shape)],
        out_specs=pl.BlockSpec((2, ROWS, B_KW), lambda i: (0, i, 0)),
        out_shape=jax.ShapeDtypeStruct((2, s, B_KW), F32), compiler_params=_cparams(("parallel",)))(gl, w2, gb)


def gla_gate_bwd(gl, w2, gb, dla):
    s = gl.shape[0]

    def body(gl_ref, w_ref, b_ref, dla_ref, dgl_ref, dz_ref, db0_ref, db1_ref):
        @pl.when(pl.program_id(0) == 0)
        def _():
            db0_ref[...] = jnp.zeros_like(db0_ref)
            db1_ref[...] = jnp.zeros_like(db1_ref)

        dgl = None
        for n, db_ref in enumerate((db0_ref, db1_ref)):
            _, vjp = jax.vjp(_log_gate, nn(gl_ref[...], w_ref[n]) + b_ref[n])
            dz, = vjp(dla_ref[n])
            dz_ref[n] = dz
            db_ref[...] += jnp.sum(dz, axis=0, keepdims=True)
            part = nt(dz, w_ref[n])
            dgl = part if dgl is None else dgl + part
        dgl_ref[...] = dgl

    full = lambda shp: pl.BlockSpec(shp, lambda i: (0,) * len(shp))
    row = pl.BlockSpec((ROWS, LANES), lambda i: (i, 0))
    wide = pl.BlockSpec((2, ROWS, B_KW), lambda i: (0, i, 0))
    vec = pl.BlockSpec((1, B_KW), lambda i: (0, 0))
    return pl.pallas_call(
        body, name="gla_gate_bwd", grid=(s // ROWS,),
        in_specs=[row, full(w2.shape), full(gb.shape), wide],
        out_specs=[row, wide, vec, vec],
        out_shape=[jax.ShapeDtypeStruct((s, LANES), F32), jax.ShapeDtypeStruct((2, s, B_KW), F32),
                   jax.ShapeDtypeStruct((1, B_KW), F32), jax.ShapeDtypeStruct((1, B_KW), F32)],
        compiler_params=_cparams(("arbitrary",)))(gl, w2, gb, dla)


PACK_TILE = 512


def cast_bf16(x):
    r, c = x.shape

    def body(x_ref, o_ref):
        o_ref[...] = x_ref[...].astype(BF16)

    blk = pl.BlockSpec((PACK_TILE, c), lambda i: (i, 0))
    return pl.pallas_call(body, name="cast_bf16", grid=(r // PACK_TILE,), in_specs=[blk], out_specs=blk,
                          out_shape=jax.ShapeDtypeStruct((r, c), BF16), compiler_params=_cparams(("parallel",)))(x)


def sum_slots(x, name):
    n, r, c = x.shape
    tr = _tile(r, PACK_TILE)

    def body(x_ref, o_ref):
        acc = x_ref[0]
        for k in range(1, n):
            acc = acc + x_ref[k]
        o_ref[...] = acc

    return pl.pallas_call(body, name=name, grid=(r // tr,), in_specs=[pl.BlockSpec((n, tr, c), lambda i: (0, i, 0))],
                          out_specs=pl.BlockSpec((tr, c), lambda i: (i, 0)),
                          out_shape=jax.ShapeDtypeStruct((r, c), F32), compiler_params=_cparams(("parallel",)))(x)


def adamw(w, m, v, grads, g_row_off, name):
    r, c = w.shape
    tr = next(t for t in (PACK_TILE, r) if r % t == 0 and g_row_off % t == 0)
    ob = g_row_off // tr
    ng = len(grads)

    def body(*refs):
        w_ref, m_ref, v_ref = refs[:3]
        g_refs = refs[3:3 + ng]
        g_ref, d_ref, nm_ref, nv_ref = refs[3 + ng:]
        g = g_refs[0][...]
        for gr in g_refs[1:]:
            g = g + gr[...]
        m_new = ADAM_B1 * m_ref[...] + (1.0 - ADAM_B1) * g
        v_new = ADAM_B2 * v_ref[...] + (1.0 - ADAM_B2) * jnp.square(g)
        m_hat = m_new / (1.0 - ADAM_B1 ** ADAM_STEP)
        v_hat = v_new / (1.0 - ADAM_B2 ** ADAM_STEP)
        g_ref[...] = g
        d_ref[...] = -ADAM_LR * (m_hat / (jnp.sqrt(v_hat) + ADAM_EPS) + ADAM_WD * w_ref[...])
        nm_ref[...] = m_new
        nv_ref[...] = v_new

    blk = pl.BlockSpec((tr, c), lambda i: (i, 0))
    gblk = pl.BlockSpec((tr, c), lambda i: (i + ob, 0))
    return pl.pallas_call(body, name=name, grid=(r // tr,), in_specs=[blk, blk, blk] + [gblk] * ng, out_specs=[blk] * 4,
                          out_shape=[jax.ShapeDtypeStruct((r, c), F32)] * 4,
                          compiler_params=_cparams(("parallel",)))(w, m, v, *grads)


MESH = pl.DeviceIdType.MESH
HBM = pl.BlockSpec(memory_space=pl.ANY)
CHIP_FLIPS = ((1, 0), (0, 1), (1, 1))


def _place():
    return lax.axis_index("x"), lax.axis_index("y"), lax.axis_index("c")


def allgather_chips(pack):
    r, c = pack.shape

    def body(src_ref, out_ref, send_sems, recv_sems, local_sem):
        x, y, cc = _place()
        mine = pltpu.make_async_copy(src_ref, out_ref.at[2 * x + y], local_sem)
        mine.start()
        sends, recvs = [], []
        for k, (fx, fy) in enumerate(CHIP_FLIPS):
            px, py = (1 - x if fx else x), (1 - y if fy else y)
            sends.append(pltpu.make_async_remote_copy(
                src_ref=src_ref, dst_ref=out_ref.at[2 * x + y], send_sem=send_sems.at[k], recv_sem=recv_sems.at[k],
                device_id=(px, py, cc), device_id_type=MESH))
            recvs.append(pltpu.make_async_remote_copy(
                src_ref=src_ref, dst_ref=out_ref.at[2 * px + py], send_sem=send_sems.at[k], recv_sem=recv_sems.at[k],
                device_id=(px, py, cc), device_id_type=MESH))
        for cp in sends:
            cp.start()
        for cp in recvs:
            cp.wait_recv()
        for cp in sends:
            cp.wait_send()
        mine.wait()

    return pl.pallas_call(
        body, name="allgather_chips", in_specs=[HBM], out_specs=HBM,
        out_shape=jax.ShapeDtypeStruct((4, r, c), pack.dtype),
        scratch_shapes=[pltpu.SemaphoreType.DMA((3,)), pltpu.SemaphoreType.DMA((3,)), pltpu.SemaphoreType.DMA],
    )(pack)


def scatter_chips(gpack):
    def body(src_ref, out_ref, send_sems, recv_sems, local_sem):
        x, y, cc = _place()
        mine = pltpu.make_async_copy(src_ref.at[2 * x + y], out_ref.at[3], local_sem)
        mine.start()
        sends = []
        for k, (fx, fy) in enumerate(CHIP_FLIPS):
            px, py = (1 - x if fx else x), (1 - y if fy else y)
            sends.append(pltpu.make_async_remote_copy(
                src_ref=src_ref.at[2 * px + py], dst_ref=out_ref.at[k], send_sem=send_sems.at[k], recv_sem=recv_sems.at[k],
                device_id=(px, py, cc), device_id_type=MESH))
        for cp in sends:
            cp.start()
        for cp in sends:
            cp.wait_recv()
        for cp in sends:
            cp.wait_send()
        mine.wait()

    return pl.pallas_call(
        body, name="scatter_chips", in_specs=[HBM], out_specs=HBM,
        out_shape=jax.ShapeDtypeStruct(gpack.shape, gpack.dtype),
        scratch_shapes=[pltpu.SemaphoreType.DMA((3,)), pltpu.SemaphoreType.DMA((3,)), pltpu.SemaphoreType.DMA],
    )(gpack)


def swap_sibling(part):
    def body(src_ref, out_ref, send_sem, recv_sem):
        x, y, cc = _place()
        cp = pltpu.make_async_remote_copy(src_ref=src_ref, dst_ref=out_ref, send_sem=send_sem, recv_sem=recv_sem,
                                          device_id=(x, y, 1 - cc), device_id_type=MESH)
        cp.start()
        cp.wait()

    return pl.pallas_call(
        body, name="swap_sibling", in_specs=[HBM], out_specs=HBM,
        out_shape=jax.ShapeDtypeStruct(part.shape, part.dtype),
        scratch_shapes=[pltpu.SemaphoreType.DMA, pltpu.SemaphoreType.DMA],
    )(part)


def exchange_all(v, name):
    r, c = v.shape

    def body(v_ref, out_ref, send_sems, recv_sems):
        x, y, cc = _place()
        out_ref[4 * x + 2 * y + cc] = v_ref[...]
        sends, recvs = [], []
        for k in range(1, 8):
            px = 1 - x if k & 4 else x
            py = 1 - y if k & 2 else y
            pc = 1 - cc if k & 1 else cc
            sends.append(pltpu.make_async_remote_copy(
                src_ref=v_ref, dst_ref=out_ref.at[4 * x + 2 * y + cc], send_sem=send_sems.at[k - 1],
                recv_sem=recv_sems.at[k - 1], device_id=(px, py, pc), device_id_type=MESH))
            recvs.append(pltpu.make_async_remote_copy(
                src_ref=v_ref, dst_ref=out_ref.at[4 * px + 2 * py + pc], send_sem=send_sems.at[k - 1],
                recv_sem=recv_sems.at[k - 1], device_id=(px, py, pc), device_id_type=MESH))
        for cp in sends:
            cp.start()
        for cp in recvs:
            cp.wait_recv()
        for cp in sends:
            cp.wait_send()

    vm = pl.BlockSpec(memory_space=pltpu.VMEM)
    return pl.pallas_call(
        body, name=name, in_specs=[vm], out_specs=vm, out_shape=jax.ShapeDtypeStruct((8, r, c), v.dtype),
        scratch_shapes=[pltpu.SemaphoreType.DMA((7,)), pltpu.SemaphoreType.DMA((7,))],
        compiler_params=pltpu.CompilerParams(vmem_limit_bytes=VMEM_LIMIT),
    )(v)


def _pack_rows(arrays, rows, width):
    flat = jnp.concatenate([a.reshape(-1) for a in arrays])
    return jnp.pad(flat, (0, rows * width - flat.shape[0])).reshape(rows, width)


def _unpack_rows(pack, shapes):
    flat = pack.reshape(-1)
    out, off = [], 0
    for shp in shapes:
        n = math.prod(shp)
        out.append(flat[off:off + n].reshape(shp))
        off += n
    return out


def _rows_for(shapes, width, mult=8):
    n = sum(math.prod(s) for s in shapes)
    return -(-n // (width * mult)) * mult


def _gdn_fwd(x, p):
    proj = mm(x, p["w_main"], name="gdn_proj")
    ba = mm(x, p["w_gate"], name="gdn_proj_gate")
    q, k, v = (gdn_pre(proj, p["conv"], kind) for kind in "qkv")
    beta_b, g_b = gdn_gates(ba, p["alog_row"], p["dt_row"])
    o2, st = gdn_rec_fwd(q, k, v, beta_b, g_b)
    y = mixer_post(o2, proj, p["norm_g"], A_DK, 3 * A_HEADS, "gdn_post")
    m = mm(y, p["w_out"], name="gdn_out")
    return m, (x, proj, ba, q, k, v, beta_b, g_b, o2, st, y)


def _gdn_bwd(saved, p, dm):
    x, proj, ba, q, k, v, beta_b, g_b, o2, st, y = saved
    d_w_out = mm(y, dm, "tn", name="gdn_dw_out")
    dy = mm(dm, p["w_out"], "nt", name="gdn_dy")
    do, dz, d_norm_g = mixer_post_bwd(o2, proj, p["norm_g"], dy, A_DK, 3 * A_HEADS, "gdn_post_bwd")
    dq2, dk2, dv2, dbb, dgb = gdn_rec_bwd(q, k, v, beta_b, g_b, st, do)
    dba, d_alog_row, d_dt_row = gdn_gates_bwd(ba, p["alog_row"], p["dt_row"], dbb, dgb)
    du, dconv = zip(*(gdn_pre_bwd(proj, p["conv"], d2, kind) for d2, kind in ((dq2, "q"), (dk2, "k"), (dv2, "v"))))
    dproj = jnp.concatenate(list(du) + [dz], axis=1)
    d_w_main = mm(x, dproj, "tn", name="gdn_dw_main")
    d_w_gate = mm(x, dba, "tn", name="gdn_dw_gate")
    dx = mm(dba, p["w_gate"], "nt", epi="add", extra=dm, alpha=ALPHA, name="gdn_dx_gate")
    dx = mm(dproj, p["w_main"], "nt", epi="add", extra=dx, name="gdn_dx")
    grads = dict(w_in=jnp.concatenate([d_w_main, d_w_gate[:, :2 * N_GATE]], axis=1), conv=jnp.concatenate(dconv, axis=1),
                 alog=d_alog_row[0, N_GATE:2 * N_GATE].reshape(2, A_HEADS), dt=d_dt_row[0, N_GATE:2 * N_GATE].reshape(2, A_HEADS),
                 norm_g=d_norm_g[0], w_out=d_w_out)
    return dx, grads


def _gla_fwd(x, p):
    proj = mm(x, p["w_main"], name="gla_proj")
    gl = mm(x, p["w_gate"], name="gla_proj_gate")
    log_a = gla_gate(gl, p["w2"], p["gate_b"])
    o2, st = gla_rec_fwd(proj, log_a)
    y = mixer_post(o2, proj, p["norm_g"], B_DV, (2 * B_KW + B_VW) // B_DV, "gla_post")
    m = mm(y, p["w_out"], name="gla_out")
    return m, (x, proj, gl, log_a, o2, st, y)


def _gla_bwd(saved, p, dm):
    x, proj, gl, log_a, o2, st, y = saved
    d_w_out = mm(y, dm, "tn", name="gla_dw_out")
    dy = mm(dm, p["w_out"], "nt", name="gla_dy")
    do, dr, d_norm_g = mixer_post_bwd(o2, proj, p["norm_g"], dy, B_DV, (2 * B_KW + B_VW) // B_DV, "gla_post_bwd")
    dq2, dk2, dv2, dla = gla_rec_bwd(proj, log_a, st, do)
    dgl, dz, d_b0, d_b1 = gla_gate_bwd(gl, p["w2"], p["gate_b"], dla)
    d_w2 = [mm(gl, dz[n], "tn", name="gla_dw_gate_w2") for n in range(2)]
    dproj = jnp.concatenate([dq2[0] + dq2[1], dk2[0] + dk2[1], dv2[0] + dv2[1], dr], axis=1)
    d_w_main = mm(x, dproj, "tn", name="gla_dw_main")
    d_w_gate = mm(x, dgl, "tn", name="gla_dw_gate")
    dx = mm(dgl, p["w_gate"], "nt", epi="add", extra=dm, alpha=ALPHA, name="gla_dx_gate")
    dx = mm(dproj, p["w_main"], "nt", epi="add", extra=dx, name="gla_dx")
    grads = dict(w_in=jnp.concatenate([d_w_main, d_w_gate[:, :2 * B_RANK]], axis=1),
                 gate_w2=jnp.stack([d_w2[n][n * B_RANK:(n + 1) * B_RANK] for n in range(2)]),
                 gate_b=jnp.concatenate([d_b0, d_b1]), norm_g=d_norm_g[0], w_out=d_w_out)
    return dx, grads


def _pad_cols(w, width=LANES):
    return jnp.pad(w, ((0, 0), (0, width - w.shape[1])))


def _local_step(x, target, a_w_in, a_conv, a_alog, a_dt_bias, a_norm_g, a_w_out, b_w_in, b_gate_w2, b_gate_b, b_norm_g,
                b_w_out, ln1_g, ln1_b, mlp_w1, mlp_w2, ln2_g, ln2_b):
    layer_p = []
    for i in range(DEPTH):
        j = i // 2
        if i % 2 == 0:
            layer_p.append(dict(
                w_main=a_w_in[j][:, :4 * A_W], w_gate=_pad_cols(a_w_in[j][:, 4 * A_W:]), conv=a_conv[j],
                alog_row=jnp.pad(a_alog[j].reshape(1, N_GATE), ((0, 0), (N_GATE, LANES - 2 * N_GATE))),
                dt_row=jnp.pad(a_dt_bias[j].reshape(1, N_GATE), ((0, 0), (N_GATE, LANES - 2 * N_GATE))),
                norm_g=a_norm_g[j].reshape(1, A_DK), w_out=a_w_out[j]))
        else:
            w2 = jnp.stack([jnp.pad(b_gate_w2[j][n], ((n * B_RANK, LANES - (n + 1) * B_RANK), (0, 0))) for n in range(2)])
            layer_p.append(dict(
                w_main=b_w_in[j][:, :2 * B_KW + 2 * B_VW], w_gate=_pad_cols(b_w_in[j][:, 2 * B_KW + 2 * B_VW:]),
                w2=w2, gate_b=b_gate_b[j].reshape(2, 1, B_KW), norm_g=b_norm_g[j].reshape(1, B_DV), w_out=b_w_out[j]))

    saved = []
    h = x
    for i in range(DEPTH):
        p = layer_p[i]
        m, sv = (_gdn_fwd if i % 2 == 0 else _gla_fwd)(h, p)
        x1 = ln_fwd(h, m, ln1_g[i:i + 1], ln1_b[i:i + 1])
        h1 = mm(x1, mlp_w1[i], name="mlp_up")
        mlp = mm(h1, mlp_w2[i], act="sqrelu", name="mlp_down")
        x2 = ln_fwd(x1, mlp, ln2_g[i:i + 1], ln2_b[i:i + 1])
        saved.append((sv, h, m, x1, h1, mlp))
        h = x2

    dh, loss_part = loss_head(h, target)

    g_a, g_b, g_ln1g, g_ln1b, g_ln2g, g_ln2b, g_w1, g_w2 = {}, {}, {}, {}, {}, {}, {}, {}
    for i in reversed(range(DEPTH)):
        sv, xin, m, x1, h1, mlp = saved[i]
        p = layer_p[i]
        dr2, g_ln2g[i], g_ln2b[i] = ln_bwd(x1, mlp, ln2_g[i:i + 1], dh)
        g_w2[i] = mm(h1, dr2, "tn", act="sqrelu", name="mlp_dw_down")
        dh1 = mm(dr2, mlp_w2[i], "nt", epi="dsqrelu", extra=h1, name="mlp_dh")
        g_w1[i] = mm(x1, dh1, "tn", name="mlp_dw_up")
        dx1 = mm(dh1, mlp_w1[i], "nt", epi="add", extra=dr2, alpha=ALPHA, name="mlp_dx")
        dr1, g_ln1g[i], g_ln1b[i] = ln_bwd(xin, m, ln1_g[i:i + 1], dx1)
        dh, g = (_gdn_bwd if i % 2 == 0 else _gla_bwd)(sv, p, dr1)
        (g_a if i % 2 == 0 else g_b)[i // 2] = g

    st = lambda d, key=None: jnp.stack([(d[i] if key is None else d[i][key]) for i in sorted(d)])
    grads = dict(
        a_w_in=st(g_a, "w_in"), a_conv=st(g_a, "conv"), a_alog=st(g_a, "alog"), a_dt_bias=st(g_a, "dt"),
        a_norm_g=st(g_a, "norm_g"), a_w_out=st(g_a, "w_out"), b_w_in=st(g_b, "w_in"), b_gate_w2=st(g_b, "gate_w2"),
        b_gate_b=st(g_b, "gate_b"), b_norm_g=st(g_b, "norm_g"), b_w_out=st(g_b, "w_out"),
        ln1_g=st(g_ln1g)[:, 0], ln1_b=st(g_ln1b)[:, 0], mlp_w1=st(g_w1), mlp_w2=st(g_w2),
        ln2_g=st(g_ln2g)[:, 0], ln2_b=st(g_ln2b)[:, 0])
    return loss_part, dh, grads


WEIGHTS = ("a_w_in", "a_conv", "a_alog", "a_dt_bias", "a_norm_g", "a_w_out", "b_w_in", "b_gate_w2", "b_gate_b",
           "b_norm_g", "b_w_out", "ln1_g", "ln1_b", "mlp_w1", "mlp_w2", "ln2_g", "ln2_b")
BIG = ("mlp_w1", "mlp_w2", "a_w_out", "b_w_out", "a_w_in", "b_w_in")
SHARD_AXIS = {"mlp_w1": 2, "mlp_w2": 1, "a_w_out": 1, "b_w_out": 1, "a_w_in": 2, "b_w_in": 2}
SMALL = tuple(n for n in WEIGHTS if n not in BIG)
SMALL_SHARD_AXIS = {"a_conv": 2, "b_gate_w2": 3, "b_gate_b": 2, "b_norm_g": 1}


def _to_chip_major(full, axis):
    shp = full.shape
    t = full.reshape(shp[:axis] + (4, shp[axis] // 4) + shp[axis + 1:])
    return jnp.moveaxis(t, axis, 0)


def _from_chip_major(stacked, axis):
    t = jnp.moveaxis(stacked, 0, axis)
    shp = t.shape
    return t.reshape(shp[:axis] + (shp[axis] * shp[axis + 1],) + shp[axis + 2:])


def kernel(x, a_w_in, a_conv, a_alog, a_dt_bias, a_norm_g, a_w_out, b_w_in, b_gate_w2, b_gate_b, b_norm_g, b_w_out, ln1_g, ln1_b, mlp_w1, mlp_w2, ln2_g, ln2_b, loss_target, m_a_w_in, m_a_conv, m_a_alog, m_a_dt_bias, m_a_norm_g, m_a_w_out, m_b_w_in, m_b_gate_w2, m_b_gate_b, m_b_norm_g, m_b_w_out, m_ln1_g, m_ln1_b, m_mlp_w1, m_mlp_w2, m_ln2_g, m_ln2_b, v_a_w_in, v_a_conv, v_a_alog, v_a_dt_bias, v_a_norm_g, v_a_w_out, v_b_w_in, v_b_gate_w2, v_b_gate_b, v_b_norm_g, v_b_w_out, v_ln1_g, v_ln1_b, v_mlp_w1, v_mlp_w2, v_ln2_g, v_ln2_b):
    w = dict(a_w_in=a_w_in, a_conv=a_conv, a_alog=a_alog, a_dt_bias=a_dt_bias, a_norm_g=a_norm_g, a_w_out=a_w_out,
             b_w_in=b_w_in, b_gate_w2=b_gate_w2, b_gate_b=b_gate_b, b_norm_g=b_norm_g, b_w_out=b_w_out, ln1_g=ln1_g,
             ln1_b=ln1_b, mlp_w1=mlp_w1, mlp_w2=mlp_w2, ln2_g=ln2_g, ln2_b=ln2_b)
    mom = dict(a_w_in=m_a_w_in, a_conv=m_a_conv, a_alog=m_a_alog, a_dt_bias=m_a_dt_bias, a_norm_g=m_a_norm_g,
               a_w_out=m_a_w_out, b_w_in=m_b_w_in, b_gate_w2=m_b_gate_w2, b_gate_b=m_b_gate_b, b_norm_g=m_b_norm_g,
               b_w_out=m_b_w_out, ln1_g=m_ln1_g, ln1_b=m_ln1_b, mlp_w1=m_mlp_w1, mlp_w2=m_mlp_w2, ln2_g=m_ln2_g,
               ln2_b=m_ln2_b)
    var = dict(a_w_in=v_a_w_in, a_conv=v_a_conv, a_alog=v_a_alog, a_dt_bias=v_a_dt_bias, a_norm_g=v_a_norm_g,
               a_w_out=v_a_w_out, b_w_in=v_b_w_in, b_gate_w2=v_b_gate_w2, b_gate_b=v_b_gate_b, b_norm_g=v_b_norm_g,
               b_w_out=v_b_w_out, ln1_g=v_ln1_g, ln1_b=v_ln1_b, mlp_w1=v_mlp_w1, mlp_w2=v_mlp_w2, ln2_g=v_ln2_g,
               ln2_b=v_ln2_b)
    chip = 2 * lax.axis_index("x") + lax.axis_index("y")

    seg_rows = [w[n].size // D_MODEL for n in BIG]
    seg_off = [sum(seg_rows[:i]) for i in range(len(BIG))]
    rows = -(-sum(seg_rows) // PACK_TILE) * PACK_TILE
    shard_pack = _pack_rows([w[n] for n in BIG], rows, D_MODEL)
    gathered = allgather_chips(cast_bf16(shard_pack))
    full = {}
    for n, off, nr in zip(BIG, seg_off, seg_rows):
        stacked = gathered[:, off:off + nr].reshape((4,) + w[n].shape)
        full[n] = _from_chip_major(stacked, SHARD_AXIS[n])
    sharded_small = tuple(SMALL_SHARD_AXIS)
    sm_shapes = [w[n].shape for n in sharded_small]
    sm_rows = _rows_for(sm_shapes, LANES)
    sm_all = exchange_all(_pack_rows([w[n] for n in sharded_small], sm_rows, LANES), "gather_small")
    per_chip = [_unpack_rows(sm_all[2 * pch], sm_shapes) for pch in range(4)]
    for idx, n in enumerate(sharded_small):
        full[n] = jnp.concatenate([per_chip[pch][idx] for pch in range(4)], axis=SMALL_SHARD_AXIS[n])
    for n in WEIGHTS:
        full.setdefault(n, w[n])

    loss_part, grad_x, grads = _local_step(x[0], loss_target[0], *[full[n] for n in WEIGHTS])
    loss = lax.psum(jnp.sum(loss_part), ("x", "y", "c"))

    gpack = jnp.concatenate(
        [_to_chip_major(grads[n], SHARD_AXIS[n]).reshape(4, nr, D_MODEL) for n, nr in zip(BIG, seg_rows)]
        + [jnp.zeros((4, rows - sum(seg_rows), D_MODEL), F32)], axis=1)
    part = sum_slots(scatter_chips(gpack), "sum_chips")
    other = swap_sibling(part)
    out_g, out_d, out_m, out_v = {}, {}, {}, {}
    for n, off, nr in zip(BIG, seg_off, seg_rows):
        if w[n].shape[-1] == D_MODEL:
            view = lambda t: t.reshape(-1, D_MODEL)
            res = adamw(view(w[n]), view(mom[n]), view(var[n]), (part, other), off, "adamw_" + n)
        else:
            cols = w[n].shape[-1]
            view = lambda t: t.reshape(-1, cols)
            gs = tuple(view(t[off:off + nr]) for t in (part, other))
            res = adamw(view(w[n]), view(mom[n]), view(var[n]), gs, 0, "adamw_" + n)
        out_g[n], out_d[n], out_m[n], out_v[n] = (t.reshape(w[n].shape) for t in res)

    all_shapes = [full[n].shape for n in SMALL]
    g_rows = _rows_for(all_shapes, LANES)
    g_all = exchange_all(_pack_rows([grads[n] for n in SMALL], g_rows, LANES), "gather_small_grads")
    g_sum = _unpack_rows(sum_slots(g_all, "sum_small_grads"), all_shapes)
    g_mine = []
    for n, g in zip(SMALL, g_sum):
        if n in SMALL_SHARD_AXIS:
            ax = SMALL_SHARD_AXIS[n]
            g = lax.dynamic_slice_in_dim(g, chip * w[n].shape[ax], w[n].shape[ax], axis=ax)
        g_mine.append(g)
    my_shapes = [w[n].shape for n in SMALL]
    s_rows = _rows_for(my_shapes, LANES)
    pk = lambda d: _pack_rows([d[n] for n in SMALL], s_rows, LANES)
    res = adamw(pk(w), pk(mom), pk(var), (_pack_rows(g_mine, s_rows, LANES),), 0, "adamw_small")
    for dst, pack in zip((out_g, out_d, out_m, out_v), res):
        for n, t in zip(SMALL, _unpack_rows(pack, my_shapes)):
            dst[n] = t

    return (loss, grad_x[None], *[out_g[n] for n in WEIGHTS], *[out_d[n] for n in WEIGHTS],
            *[out_m[n] for n in WEIGHTS], *[out_v[n] for n in WEIGHTS])
```

```python
import functools
import math

import jax
import jax.numpy as jnp
from jax import lax
from jax.experimental import pallas as pl
from jax.experimental.pallas import tpu as pltpu

F32 = jnp.float32
BF16 = jnp.bfloat16

D_MODEL = 1024
DEPTH = 4
CHUNK = 64
A_HEADS = 8
A_DK = 128
A_W = 1024
A_CONV = 5
B_HEADS = 4
B_DK = 128
B_DV = 256
B_RANK = 16
B_TAU = 16.0
B_KW = 512
B_VW = 1024
ALPHA = (2 * DEPTH) ** 0.25
LN_EPS = 1e-5
RMS_EPS = 1e-6
L2_EPS = 1e-6
ADAM_LR = 0.001
ADAM_B1 = 0.9
ADAM_B2 = 0.999
ADAM_EPS = 1e-08
ADAM_WD = 0.01
ADAM_STEP = 10
LANES = 128
NEG_INF = float("-inf")
VMEM_LIMIT = 56 * 1024 * 1024


def _cparams(sem=None):
    return pltpu.CompilerParams(dimension_semantics=sem, vmem_limit_bytes=VMEM_LIMIT)


def _dg(a, b, ca, cb):
    return lax.dot_general(a.astype(BF16), b.astype(BF16), (((ca,), (cb,)), ((), ())),
                           preferred_element_type=F32)


@functools.partial(jax.custom_vjp, nondiff_argnums=(2, 3))
def bdot(a, b, ca, cb):
    return _dg(a, b, ca, cb)


def _bdot_fwd(a, b, ca, cb):
    return _dg(a, b, ca, cb), (a, b)


def _bdot_bwd(ca, cb, res, g):
    a, b = res
    da = _dg(g, b, 1, 1 - cb) if ca == 1 else _dg(b, g, 1 - cb, 1)
    db = _dg(a, g, 1 - ca, 0) if cb == 0 else _dg(g, a, 0, 1 - ca)
    return da, db


bdot.defvjp(_bdot_fwd, _bdot_bwd)


def nn(a, b):
    return bdot(a, b, 1, 0)


def nt(a, b):
    return bdot(a, b, 1, 1)


def tn(a, b):
    return bdot(a, b, 0, 0)


def xdot(a, b):
    return jnp.dot(a, b, precision=lax.Precision.HIGHEST, preferred_element_type=F32)


def _sigmoid(x):
    return 1.0 / (1.0 + jnp.exp(-x))


def _softplus(x):
    return jnp.maximum(x, 0.0) + jnp.log(1.0 + jnp.exp(-jnp.abs(x)))


def _chunk_masks(rev):
    ii = lax.broadcasted_iota(jnp.int32, (CHUNK, CHUNK), 0)
    jj = lax.broadcasted_iota(jnp.int32, (CHUNK, CHUNK), 1)
    d = (ii - jj) * (1 - 2 * rev)
    return d >= 0, d > 0, ii == jj, (ii >> 3) == (jj >> 3)


def _each(f, *lists):
    return [f(*xs) for xs in zip(*lists)]


def _unit_triangular_inverse(a, eye, blockdiag):
    ident = eye.astype(F32)
    ad = _each(lambda x: jnp.where(blockdiag, x, 0.0), a)
    e = _each(lambda x, y: x - y, a, ad)
    dinv = _each(lambda x: ident - x, ad)
    p = _each(xdot, ad, ad)
    dinv = _each(lambda x, y: x + xdot(x, y), dinv, p)
    p = _each(xdot, p, p)
    dinv = _each(lambda x, y: x + xdot(x, y), dinv, p)
    g = _each(lambda x, y: -xdot(x, y), dinv, e)
    finv = _each(lambda x: ident + x, g)
    p = _each(xdot, g, g)
    finv = _each(lambda x, y: x + xdot(x, y), finv, p)
    p = _each(xdot, p, p)
    finv = _each(lambda x, y: x + xdot(x, y), finv, p)
    return _each(xdot, finv, dinv)


def _gdn_step(state, q, k, v, bb, gb, rev):
    causal, strict, eye, blockdiag = _chunk_masks(rev)
    lower = causal.astype(F32)
    ones = jnp.ones((CHUNK, CHUNK), F32)
    gcb = _each(lambda x: xdot(lower, x), gb)
    gcol = _each(lambda x: x[:, :CHUNK], gcb)
    grow = _each(lambda x: xdot(ones, jnp.where(eye, x, 0.0)), gcol)
    decay = _each(lambda x, y: jnp.exp(jnp.where(causal, x - y, NEG_INF)), gcol, grow)
    kb = _each(lambda x, y: x * y, k, bb)
    a = _each(lambda x, y, z: jnp.where(strict, nt(x, y) * z, 0.0), kb, k, decay)
    t = _unit_triangular_inverse(a, eye, blockdiag)
    egc = _each(jnp.exp, gcb)
    u = _each(lambda x, y, z: xdot(x, y * z), t, v, bb)
    w = _each(lambda x, y, z: xdot(x, y * z), t, kb, egc)
    qk = _each(lambda x, y, z: nt(x, y) * z, q, k, decay)
    glast = _each(lambda x: jnp.sum(x, axis=0, keepdims=True), gb)
    v_new = _each(lambda x, y, z: x - nn(y, z), u, w, state)
    o = _each(lambda x, y, z, p, r: nn(x * y, z) + nn(p, r), q, egc, state, qk, v_new)
    k_dec = _each(lambda x, y, z: x * jnp.exp(y - z), k, glast, gcb)
    state_new = _each(lambda x, y, z, p: x * jnp.exp(y) + tn(z, p), state, glast, k_dec, v_new)
    return state_new, o


def _gla_step(state_t, q, k, v, la, rev):
    causal, _, _, _ = _chunk_masks(rev)
    b = xdot(causal.astype(F32), la)
    q = q * (B_DK ** -0.5)
    shp = (CHUNK, CHUNK, B_DK)
    d3 = (lax.broadcasted_iota(jnp.int32, shp, 0) - lax.broadcasted_iota(jnp.int32, shp, 1)) * (1 - 2 * rev)
    dec = jnp.exp(jnp.where(d3 >= 0, b[:, None, :] - b[None, :, :], NEG_INF))
    scores = jnp.sum(q[:, None, :] * k[None, :, :] * dec, axis=-1)
    blast = jnp.sum(la, axis=0, keepdims=True)
    o = nt(q * jnp.exp(b), state_t) + nn(scores, v)
    state_new = jnp.exp(blast) * state_t + tn(v, k * jnp.exp(blast - b))
    return state_new, o


def _chunk_pos(d, m, n):
    return m + d * (n - 1 - 2 * m)


GDN_HEADS_PER_STEP = 8
def gdn_rec_fwd(q, k, v, beta_b, g_b):
    s = q.shape[0]
    n = s // CHUNK

    hb = GDN_HEADS_PER_STEP
    wide = hb * LANES

    def body(q_ref, k_ref, v_ref, bb_ref, gb_ref, o_ref, st_ref, state):
        d = pl.program_id(0)

        @pl.when(pl.program_id(2) == 0)
        def _():
            state[...] = jnp.zeros_like(state)

        cols = [slice(hh * LANES, (hh + 1) * LANES) for hh in range(hb)]
        st = [state[hh] for hh in range(hb)]
        new, o = _gdn_step(st, *([r[:, c] for c in cols] for r in (q_ref, k_ref, v_ref, bb_ref, gb_ref)), d)
        for hh, c in enumerate(cols):
            st_ref[hh] = st[hh]
            state[hh] = new[hh]
            o_ref[:, c] = o[hh]

    blk = pl.BlockSpec((CHUNK, wide), lambda d, h, m: (_chunk_pos(d, m, n), h))
    gate = pl.BlockSpec((CHUNK, wide), lambda d, h, m: (_chunk_pos(d, m, n), d * (A_HEADS // hb) + h))
    return pl.pallas_call(
        body, name="gdn_rec_fwd", grid=(2, A_HEADS // hb, n),
        in_specs=[blk, blk, blk, gate, gate],
        out_specs=[pl.BlockSpec((None, CHUNK, wide), lambda d, h, m: (d, _chunk_pos(d, m, n), h)),
                   pl.BlockSpec((None, hb, None, A_DK, LANES), lambda d, h, m: (d, h, _chunk_pos(d, m, n), 0, 0))],
        out_shape=[jax.ShapeDtypeStruct((2, s, A_W), F32), jax.ShapeDtypeStruct((2, A_HEADS, n, A_DK, LANES), F32)],
        scratch_shapes=[pltpu.VMEM((hb, A_DK, LANES), F32)],
        compiler_params=_cparams(("arbitrary", "arbitrary", "arbitrary")),
    )(q, k, v, beta_b, g_b)


def gdn_rec_bwd(q, k, v, beta_b, g_b, states, do):
    s = q.shape[0]
    n = s // CHUNK

    hb = GDN_HEADS_PER_STEP
    wide = hb * LANES

    def body(q_ref, k_ref, v_ref, bb_ref, gb_ref, st_ref, do_ref, dq_ref, dk_ref, dv_ref, dbb_ref, dgb_ref, dstate):
        d = pl.program_id(0)

        @pl.when(pl.program_id(2) == 0)
        def _():
            dstate[...] = jnp.zeros_like(dstate)

        step = functools.partial(_gdn_step, rev=d)
        cols = [slice(hh * LANES, (hh + 1) * LANES) for hh in range(hb)]
        _, vjp = jax.vjp(step, [st_ref[hh] for hh in range(hb)],
                         *([r[:, c] for c in cols] for r in (q_ref, k_ref, v_ref, bb_ref, gb_ref)))
        grads = vjp(([dstate[hh] for hh in range(hb)], [do_ref[:, c] for c in cols]))
        for hh, c in enumerate(cols):
            dstate[hh], dq_ref[:, c], dk_ref[:, c], dv_ref[:, c], dbb_ref[:, c], dgb_ref[:, c] = (g[hh] for g in grads)

    pos = lambda d, m: _chunk_pos(1 - d, m, n)
    blk = pl.BlockSpec((CHUNK, wide), lambda d, h, m: (pos(d, m), h))
    gate = pl.BlockSpec((CHUNK, wide), lambda d, h, m: (pos(d, m), d * (A_HEADS // hb) + h))
    oblk = pl.BlockSpec((None, CHUNK, wide), lambda d, h, m: (d, pos(d, m), h))
    return pl.pallas_call(
        body, name="gdn_rec_bwd", grid=(2, A_HEADS // hb, n),
        in_specs=[blk, blk, blk, gate, gate,
                  pl.BlockSpec((None, hb, None, A_DK, LANES), lambda d, h, m: (d, h, pos(d, m), 0, 0)), blk],
        out_specs=[oblk, oblk, oblk, gate, gate],
        out_shape=[jax.ShapeDtypeStruct((2, s, A_W), F32)] * 3 + [jax.ShapeDtypeStruct(beta_b.shape, F32)] * 2,
        scratch_shapes=[pltpu.VMEM((hb, A_DK, LANES), F32)],
        compiler_params=_cparams(("arbitrary", "arbitrary", "arbitrary")),
    )(q, k, v, beta_b, g_b, states, do)


def gla_rec_fwd(proj, log_a):
    s = proj.shape[0]
    n = s // CHUNK

    def body(q_ref, k_ref, v_ref, la_ref, o_ref, st_ref, state):
        d = pl.program_id(0)

        @pl.when(pl.program_id(2) == 0)
        def _():
            state[...] = jnp.zeros_like(state)

        st = state[...]
        st_ref[...] = st
        new, o = _gla_step(st, q_ref[...], k_ref[...], v_ref[...], la_ref[...], d)
        state[...] = new
        o_ref[...] = o

    return pl.pallas_call(
        body, name="gla_rec_fwd", grid=(2, B_HEADS, n),
        in_specs=[pl.BlockSpec((CHUNK, B_DK), lambda d, h, m: (_chunk_pos(d, m, n), h)),
                  pl.BlockSpec((CHUNK, B_DK), lambda d, h, m: (_chunk_pos(d, m, n), B_KW // B_DK + h)),
                  pl.BlockSpec((CHUNK, B_DV), lambda d, h, m: (_chunk_pos(d, m, n), 2 * B_KW // B_DV + h)),
                  pl.BlockSpec((None, CHUNK, B_DK), lambda d, h, m: (d, _chunk_pos(d, m, n), h))],
        out_specs=[pl.BlockSpec((None, CHUNK, B_DV), lambda d, h, m: (d, _chunk_pos(d, m, n), h)),
                   pl.BlockSpec((None, None, None, B_DV, B_DK), lambda d, h, m: (d, h, _chunk_pos(d, m, n), 0, 0))],
        out_shape=[jax.ShapeDtypeStruct((2, s, B_VW), F32), jax.ShapeDtypeStruct((2, B_HEADS, n, B_DV, B_DK), F32)],
        scratch_shapes=[pltpu.VMEM((B_DV, B_DK), F32)],
        compiler_params=_cparams(("arbitrary", "arbitrary", "arbitrary")),
    )(proj, proj, proj, log_a)


def gla_rec_bwd(proj, log_a, states, do):
    s = proj.shape[0]
    n = s // CHUNK

    def body(q_ref, k_ref, v_ref, la_ref, st_ref, do_ref, dq_ref, dk_ref, dv_ref, dla_ref, dstate):
        d = pl.program_id(0)

        @pl.when(pl.program_id(2) == 0)
        def _():
            dstate[...] = jnp.zeros_like(dstate)

        step = functools.partial(_gla_step, rev=d)
        _, vjp = jax.vjp(step, st_ref[...], q_ref[...], k_ref[...], v_ref[...], la_ref[...])
        dst, dq, dk, dv, dla = vjp((dstate[...], do_ref[...]))
        dstate[...] = dst
        dq_ref[...] = dq
        dk_ref[...] = dk
        dv_ref[...] = dv
        dla_ref[...] = dla

    pos = lambda d, m: _chunk_pos(1 - d, m, n)
    kblk = pl.BlockSpec((None, CHUNK, B_DK), lambda d, h, m: (d, pos(d, m), h))
    return pl.pallas_call(
        body, name="gla_rec_bwd", grid=(2, B_HEADS, n),
        in_specs=[pl.BlockSpec((CHUNK, B_DK), lambda d, h, m: (pos(d, m), h)),
                  pl.BlockSpec((CHUNK, B_DK), lambda d, h, m: (pos(d, m), B_KW // B_DK + h)),
                  pl.BlockSpec((CHUNK, B_DV), lambda d, h, m: (pos(d, m), 2 * B_KW // B_DV + h)),
                  kblk,
                  pl.BlockSpec((None, None, None, B_DV, B_DK), lambda d, h, m: (d, h, pos(d, m), 0, 0)),
                  pl.BlockSpec((CHUNK, B_DV), lambda d, h, m: (pos(d, m), h))],
        out_specs=[kblk, kblk, pl.BlockSpec((None, CHUNK, B_DV), lambda d, h, m: (d, pos(d, m), h)), kblk],
        out_shape=[jax.ShapeDtypeStruct((2, s, B_KW), F32), jax.ShapeDtypeStruct((2, s, B_KW), F32),
                   jax.ShapeDtypeStruct((2, s, B_VW), F32), jax.ShapeDtypeStruct((2, s, B_KW), F32)],
        scratch_shapes=[pltpu.VMEM((B_DV, B_DK), F32)],
        compiler_params=_cparams(("arbitrary", "arbitrary", "arbitrary")),
    )(proj, proj, proj, log_a, states, do)


MM_TILE_OUT = 1024
MM_TILE_K = 512


def _tile(n, pref):
    return pref if n % pref == 0 else n


def mm(a, b, mode="nn", act=None, epi=None, extra=None, alpha=1.0, name="mm"):
    if mode == "tn":
        kk, m = a.shape
    else:
        m, kk = a.shape
    nn_ = b.shape[0] if mode == "nt" else b.shape[1]
    tm, tn_, tk = _tile(m, MM_TILE_OUT), _tile(nn_, MM_TILE_OUT), _tile(kk, MM_TILE_K)
    nk = kk // tk
    ca, cb = {"nn": (1, 0), "nt": (1, 1), "tn": (0, 0)}[mode]

    def body(*refs):
        if epi is None:
            a_ref, b_ref, o_ref = refs
        else:
            a_ref, b_ref, e_ref, o_ref = refs
        kstep = pl.program_id(2)
        at = a_ref[...]
        if act == "sqrelu":
            at = jnp.square(jnp.maximum(at, 0.0))
        part = _dg(at, b_ref[...], ca, cb)

        @pl.when(kstep == 0)
        def _():
            o_ref[...] = part

        @pl.when(kstep > 0)
        def _():
            o_ref[...] += part

        if epi is not None:
            @pl.when(kstep == nk - 1)
            def _():
                if epi == "dsqrelu":
                    o_ref[...] = o_ref[...] * (2.0 * jnp.maximum(e_ref[...], 0.0))
                else:
                    o_ref[...] = o_ref[...] + alpha * e_ref[...]

    a_spec = pl.BlockSpec((tk, tm), lambda i, j, k: (k, i)) if mode == "tn" else pl.BlockSpec((tm, tk), lambda i, j, k: (i, k))
    b_spec = pl.BlockSpec((tn_, tk), lambda i, j, k: (j, k)) if mode == "nt" else pl.BlockSpec((tk, tn_), lambda i, j, k: (k, j))
    o_spec = pl.BlockSpec((tm, tn_), lambda i, j, k: (i, j))
    ins, specs = [a, b], [a_spec, b_spec]
    if epi is not None:
        ins.append(extra)
        specs.append(o_spec)
    return pl.pallas_call(
        body, name=name, grid=(m // tm, nn_ // tn_, nk), in_specs=specs, out_specs=o_spec,
        out_shape=jax.ShapeDtypeStruct((m, nn_), F32),
        compiler_params=_cparams(("parallel", "parallel", "arbitrary")),
    )(*ins)


ROWS = 256


def _ln_core(x, m, g, b):
    r = ALPHA * x + m
    mu = jnp.mean(r, axis=-1, keepdims=True)
    xc = r - mu
    var = jnp.mean(xc * xc, axis=-1, keepdims=True)
    rstd = lax.rsqrt(var + LN_EPS)
    xhat = xc * rstd
    return xhat * g + b, xhat, rstd


def ln_fwd(x, m, g, b):
    s, dm = x.shape

    def body(x_ref, m_ref, g_ref, b_ref, o_ref):
        o_ref[...] = _ln_core(x_ref[...], m_ref[...], g_ref[...], b_ref[...])[0]

    row = pl.BlockSpec((ROWS, dm), lambda i: (i, 0))
    vec = pl.BlockSpec((1, dm), lambda i: (0, 0))
    return pl.pallas_call(body, name="ln_fwd", grid=(s // ROWS,), in_specs=[row, row, vec, vec], out_specs=row,
                          out_shape=jax.ShapeDtypeStruct((s, dm), F32), compiler_params=_cparams(("parallel",)))(x, m, g, b)


def ln_bwd(x, m, g, dy):
    s, dm = x.shape

    def body(x_ref, m_ref, g_ref, dy_ref, dr_ref, dg_ref, db_ref):
        gv = g_ref[...]
        _, xhat, rstd = _ln_core(x_ref[...], m_ref[...], gv, jnp.zeros_like(gv))
        dy = dy_ref[...]
        dxh = dy * gv
        dr_ref[...] = rstd * (dxh - jnp.mean(dxh, axis=-1, keepdims=True)
                              - xhat * jnp.mean(dxh * xhat, axis=-1, keepdims=True))

        @pl.when(pl.program_id(0) == 0)
        def _():
            dg_ref[...] = jnp.zeros_like(dg_ref)
            db_ref[...] = jnp.zeros_like(db_ref)

        dg_ref[...] += jnp.sum(dy * xhat, axis=0, keepdims=True)
        db_ref[...] += jnp.sum(dy, axis=0, keepdims=True)

    row = pl.BlockSpec((ROWS, dm), lambda i: (i, 0))
    vec = pl.BlockSpec((1, dm), lambda i: (0, 0))
    return pl.pallas_call(body, name="ln_bwd", grid=(s // ROWS,), in_specs=[row, row, vec, row], out_specs=[row, vec, vec],
                          out_shape=[jax.ShapeDtypeStruct((s, dm), F32), jax.ShapeDtypeStruct((1, dm), F32),
                                     jax.ShapeDtypeStruct((1, dm), F32)],
                          compiler_params=_cparams(("arbitrary",)))(x, m, g, dy)


def loss_head(y, target):
    s, dm = y.shape

    def body(y_ref, t_ref, dy_ref, l_ref):
        e = y_ref[...] - t_ref[...]
        dy_ref[...] = e * (1.0 / dm)

        @pl.when(pl.program_id(0) == 0)
        def _():
            l_ref[...] = jnp.zeros_like(l_ref)

        col = jnp.sum(e * e, axis=0, keepdims=True) * (0.5 / dm)
        acc = col[:, :LANES]
        for c in range(1, dm // LANES):
            acc = acc + col[:, c * LANES:(c + 1) * LANES]
        l_ref[...] += acc

    row = pl.BlockSpec((ROWS, dm), lambda i: (i, 0))
    return pl.pallas_call(body, name="loss_head", grid=(s // ROWS,), in_specs=[row, row],
                          out_specs=[row, pl.BlockSpec((1, LANES), lambda i: (0, 0))],
                          out_shape=[jax.ShapeDtypeStruct((s, dm), F32), jax.ShapeDtypeStruct((1, LANES), F32)],
                          compiler_params=_cparams(("arbitrary",)))(y, target)


def _shift_rows_impl(x, d):
    n = x.shape[0]
    if d == 0:
        return x
    t = lax.broadcasted_iota(jnp.int32, x.shape, 0)
    return jnp.where((t + d >= 0) & (t + d < n), pltpu.roll(x, (-d) % n, 0), 0.0)


@functools.partial(jax.custom_vjp, nondiff_argnums=(1,))
def _shift_rows(x, d):
    return _shift_rows_impl(x, d)


_shift_rows.defvjp(lambda x, d: (_shift_rows_impl(x, d), None), lambda d, _, g: (_shift_rows_impl(g, -d),))


def _gdn_pre_fn(u, w, kind):
    rows = lax.broadcasted_iota(jnp.int32, w.shape, 0)
    c = None
    for tap in range(A_CONV):
        w_tap = jnp.sum(jnp.where(rows == tap, w, 0.0), axis=0, keepdims=True)
        term = _shift_rows(u, tap - A_CONV // 2) * w_tap
        c = term if c is None else c + term
    y = c * _sigmoid(c)
    if kind == "v":
        return y
    y = y * lax.rsqrt(jnp.sum(y * y, axis=-1, keepdims=True) + L2_EPS)
    return y * (A_DK ** -0.5) if kind == "q" else y


_KIND_OFF = {"q": 0, "k": A_HEADS, "v": 2 * A_HEADS}


def gdn_pre(proj, conv_w, kind):
    s = proj.shape[0]
    off = _KIND_OFF[kind]

    def body(u_ref, w_ref, o_ref):
        o_ref[...] = _gdn_pre_fn(u_ref[...], w_ref[...], kind)

    return pl.pallas_call(
        body, name="gdn_pre_" + kind, grid=(A_HEADS,),
        in_specs=[pl.BlockSpec((s, LANES), lambda h: (0, off + h)), pl.BlockSpec((A_CONV, LANES), lambda h: (0, off + h))],
        out_specs=pl.BlockSpec((s, LANES), lambda h: (0, h)),
        out_shape=jax.ShapeDtypeStruct((s, A_W), F32), compiler_params=_cparams(("parallel",)))(proj, conv_w)


def gdn_pre_bwd(proj, conv_w, dt2, kind):
    s = proj.shape[0]
    off = _KIND_OFF[kind]

    def body(u_ref, w_ref, d0_ref, d1_ref, du_ref, dw_ref):
        _, vjp = jax.vjp(functools.partial(_gdn_pre_fn, kind=kind), u_ref[...], w_ref[...])
        du, dw = vjp(d0_ref[...] + d1_ref[...])
        du_ref[...] = du
        dw_ref[...] = dw

    return pl.pallas_call(
        body, name="gdn_pre_bwd_" + kind, grid=(A_HEADS,),
        in_specs=[pl.BlockSpec((s, LANES), lambda h: (0, off + h)), pl.BlockSpec((A_CONV, LANES), lambda h: (0, off + h)),
                  pl.BlockSpec((None, s, LANES), lambda h: (0, 0, h)), pl.BlockSpec((None, s, LANES), lambda h: (1, 0, h))],
        out_specs=[pl.BlockSpec((s, LANES), lambda h: (0, h)), pl.BlockSpec((A_CONV, LANES), lambda h: (0, h))],
        out_shape=[jax.ShapeDtypeStruct((s, A_W), F32), jax.ShapeDtypeStruct((A_CONV, A_W), F32)],
        compiler_params=_cparams(("parallel",)))(proj, conv_w, dt2, dt2)


N_GATE = 2 * A_HEADS


def _gdn_gates_fn(ba, alog_row, dt_row):
    r = lax.broadcasted_iota(jnp.int32, (LANES, N_GATE * LANES), 0)
    c = lax.broadcasted_iota(jnp.int32, (LANES, N_GATE * LANES), 1) >> 7
    beta_b = xdot(_sigmoid(ba), (r == c).astype(F32))
    g = -(jnp.exp(alog_row) * _softplus(ba + dt_row))
    g_b = xdot(g, (r == c + N_GATE).astype(F32))
    return beta_b, g_b


def gdn_gates(ba, alog_row, dt_row):
    s = ba.shape[0]

    def body(ba_ref, al_ref, dt_ref, bb_ref, gb_ref):
        bb_ref[...], gb_ref[...] = _gdn_gates_fn(ba_ref[...], al_ref[...], dt_ref[...])

    row = pl.BlockSpec((ROWS, LANES), lambda i: (i, 0))
    vec = pl.BlockSpec((1, LANES), lambda i: (0, 0))
    wide = pl.BlockSpec((ROWS, N_GATE * LANES), lambda i: (i, 0))
    return pl.pallas_call(body, name="gdn_gates", grid=(s // ROWS,), in_specs=[row, vec, vec], out_specs=[wide, wide],
                          out_shape=[jax.ShapeDtypeStruct((s, N_GATE * LANES), F32)] * 2,
                          compiler_params=_cparams(("parallel",)))(ba, alog_row, dt_row)


def gdn_gates_bwd(ba, alog_row, dt_row, dbeta_b, dg_b):
    s = ba.shape[0]

    def body(ba_ref, al_ref, dt_ref, dbb_ref, dgb_ref, dba_ref, dal_ref, ddt_ref):
        _, vjp = jax.vjp(_gdn_gates_fn, ba_ref[...], al_ref[...], dt_ref[...])
        dba, dal, ddt = vjp((dbb_ref[...], dgb_ref[...]))
        dba_ref[...] = dba

        @pl.when(pl.program_id(0) == 0)
        def _():
            dal_ref[...] = jnp.zeros_like(dal_ref)
            ddt_ref[...] = jnp.zeros_like(ddt_ref)

        dal_ref[...] += dal
        ddt_ref[...] += ddt

    row = pl.BlockSpec((ROWS, LANES), lambda i: (i, 0))
    vec = pl.BlockSpec((1, LANES), lambda i: (0, 0))
    wide = pl.BlockSpec((ROWS, N_GATE * LANES), lambda i: (i, 0))
    return pl.pallas_call(body, name="gdn_gates_bwd", grid=(s // ROWS,), in_specs=[row, vec, vec, wide, wide],
                          out_specs=[row, vec, vec],
                          out_shape=[jax.ShapeDtypeStruct((s, LANES), F32), jax.ShapeDtypeStruct((1, LANES), F32),
                                     jax.ShapeDtypeStruct((1, LANES), F32)],
                          compiler_params=_cparams(("arbitrary",)))(ba, alog_row, dt_row, dbeta_b, dg_b)


def _post_fn(o, z, g):
    y = o * lax.rsqrt(jnp.mean(o * o, axis=-1, keepdims=True) + RMS_EPS) * g
    return y * (z * _sigmoid(z))


def mixer_post(o2, proj, norm_g, width, gate_off, name):
    s = o2.shape[1]
    nh = o2.shape[2] // width

    def body(o0_ref, o1_ref, z_ref, g_ref, y_ref):
        y_ref[...] = _post_fn(o0_ref[...] + o1_ref[...], z_ref[...], g_ref[...])

    ospec = lambda d: pl.BlockSpec((None, ROWS, width), lambda i, h: (d, i, h))
    return pl.pallas_call(
        body, name=name, grid=(s // ROWS, nh),
        in_specs=[ospec(0), ospec(1), pl.BlockSpec((ROWS, width), lambda i, h: (i, gate_off + h)),
                  pl.BlockSpec((1, width), lambda i, h: (0, 0))],
        out_specs=pl.BlockSpec((ROWS, width), lambda i, h: (i, h)),
        out_shape=jax.ShapeDtypeStruct((s, o2.shape[2]), F32),
        compiler_params=_cparams(("parallel", "parallel")))(o2, o2, proj, norm_g)


def mixer_post_bwd(o2, proj, norm_g, dy, width, gate_off, name):
    s = o2.shape[1]
    nh = o2.shape[2] // width

    def body(o0_ref, o1_ref, z_ref, g_ref, dy_ref, do_ref, dz_ref, dg_ref):
        _, vjp = jax.vjp(_post_fn, o0_ref[...] + o1_ref[...], z_ref[...], g_ref[...])
        do, dz, dg = vjp(dy_ref[...])
        do_ref[...] = do
        dz_ref[...] = dz

        @pl.when((pl.program_id(0) == 0) & (pl.program_id(1) == 0))
        def _():
            dg_ref[...] = jnp.zeros_like(dg_ref)

        dg_ref[...] += dg

    ospec = lambda d: pl.BlockSpec((None, ROWS, width), lambda i, h: (d, i, h))
    blk = pl.BlockSpec((ROWS, width), lambda i, h: (i, h))
    vec = pl.BlockSpec((1, width), lambda i, h: (0, 0))
    return pl.pallas_call(
        body, name=name, grid=(s // ROWS, nh),
        in_specs=[ospec(0), ospec(1), pl.BlockSpec((ROWS, width), lambda i, h: (i, gate_off + h)), vec, blk],
        out_specs=[blk, blk, vec],
        out_shape=[jax.ShapeDtypeStruct((s, o2.shape[2]), F32)] * 2 + [jax.ShapeDtypeStruct((1, width), F32)],
        compiler_params=_cparams(("arbitrary", "arbitrary")))(o2, o2, proj, norm_g, dy)


def _log_gate(z):
    return (jnp.minimum(z, 0.0) - jnp.log(1.0 + jnp.exp(-jnp.abs(z)))) * (1.0 / B_TAU)


def gla_gate(gl, w2, gb):
    s = gl.shape[0]

    def body(gl_ref, w_ref, b_ref, o_ref):
        for n in range(2):
            o_ref[n] = _log_gate(nn(gl_ref[...], w_ref[n]) + b_ref[n])

    full = lambda shp: pl.BlockSpec(shp, lambda i: (0,) * len(shp))
    return pl.pallas_call(
        body, name="gla_gate", grid=(s // ROWS,),
        in_specs=[pl.BlockSpec((ROWS, LANES), lambda i: (i, 0)), full(w2.shape), full(gb.shape)],
        out_specs=pl.BlockSpec((2, ROWS, B_KW), lambda i: (0, i, 0)),
        out_shape=jax.ShapeDtypeStruct((2, s, B_KW), F32), compiler_params=_cparams(("parallel",)))(gl, w2, gb)


def gla_gate_bwd(gl, w2, gb, dla):
    s = gl.shape[0]

    def body(gl_ref, w_ref, b_ref, dla_ref, dgl_ref, dz_ref, db0_ref, db1_ref):
        @pl.when(pl.program_id(0) == 0)
        def _():
            db0_ref[...] = jnp.zeros_like(db0_ref)
            db1_ref[...] = jnp.zeros_like(db1_ref)

        dgl = None
        for n, db_ref in enumerate((db0_ref, db1_ref)):
            _, vjp = jax.vjp(_log_gate, nn(gl_ref[...], w_ref[n]) + b_ref[n])
            dz, = vjp(dla_ref[n])
            dz_ref[n] = dz
            db_ref[...] += jnp.sum(dz, axis=0, keepdims=True)
            part = nt(dz, w_ref[n])
            dgl = part if dgl is None else dgl + part
        dgl_ref[...] = dgl

    full = lambda shp: pl.BlockSpec(shp, lambda i: (0,) * len(shp))
    row = pl.BlockSpec((ROWS, LANES), lambda i: (i, 0))
    wide = pl.BlockSpec((2, ROWS, B_KW), lambda i: (0, i, 0))
    vec = pl.BlockSpec((1, B_KW), lambda i: (0, 0))
    return pl.pallas_call(
        body, name="gla_gate_bwd", grid=(s // ROWS,),
        in_specs=[row, full(w2.shape), full(gb.shape), wide],
        out_specs=[row, wide, vec, vec],
        out_shape=[jax.ShapeDtypeStruct((s, LANES), F32), jax.ShapeDtypeStruct((2, s, B_KW), F32),
                   jax.ShapeDtypeStruct((1, B_KW), F32), jax.ShapeDtypeStruct((1, B_KW), F32)],
        compiler_params=_cparams(("arbitrary",)))(gl, w2, gb, dla)


PACK_TILE = 512


def cast_bf16(x):
    r, c = x.shape

    def body(x_ref, o_ref):
        o_ref[...] = x_ref[...].astype(BF16)

    blk = pl.BlockSpec((PACK_TILE, c), lambda i: (i, 0))
    return pl.pallas_call(body, name="cast_bf16", grid=(r // PACK_TILE,), in_specs=[blk], out_specs=blk,
                          out_shape=jax.ShapeDtypeStruct((r, c), BF16), compiler_params=_cparams(("parallel",)))(x)


def sum_slots(x, name):
    n, r, c = x.shape
    tr = _tile(r, PACK_TILE)

    def body(x_ref, o_ref):
        acc = x_ref[0]
        for k in range(1, n):
            acc = acc + x_ref[k]
        o_ref[...] = acc

    return pl.pallas_call(body, name=name, grid=(r // tr,), in_specs=[pl.BlockSpec((n, tr, c), lambda i: (0, i, 0))],
                          out_specs=pl.BlockSpec((tr, c), lambda i: (i, 0)),
                          out_shape=jax.ShapeDtypeStruct((r, c), F32), compiler_params=_cparams(("parallel",)))(x)


def adamw(w, m, v, grads, g_row_off, name):
    r, c = w.shape
    tr = next(t for t in (PACK_TILE, r) if r % t == 0 and g_row_off % t == 0)
    ob = g_row_off // tr
    ng = len(grads)

    def body(*refs):
        w_ref, m_ref, v_ref = refs[:3]
        g_refs = refs[3:3 + ng]
        g_ref, d_ref, nm_ref, nv_ref = refs[3 + ng:]
        g = g_refs[0][...]
        for gr in g_refs[1:]:
            g = g + gr[...]
        m_new = ADAM_B1 * m_ref[...] + (1.0 - ADAM_B1) * g
        v_new = ADAM_B2 * v_ref[...] + (1.0 - ADAM_B2) * jnp.square(g)
        m_hat = m_new / (1.0 - ADAM_B1 ** ADAM_STEP)
        v_hat = v_new / (1.0 - ADAM_B2 ** ADAM_STEP)
        g_ref[...] = g
        d_ref[...] = -ADAM_LR * (m_hat / (jnp.sqrt(v_hat) + ADAM_EPS) + ADAM_WD * w_ref[...])
        nm_ref[...] = m_new
        nv_ref[...] = v_new

    blk = pl.BlockSpec((tr, c), lambda i: (i, 0))
    gblk = pl.BlockSpec((tr, c), lambda i: (i + ob, 0))
    return pl.pallas_call(body, name=name, grid=(r // tr,), in_specs=[blk, blk, blk] + [gblk] * ng, out_specs=[blk] * 4,
                          out_shape=[jax.ShapeDtypeStruct((r, c), F32)] * 4,
                          compiler_params=_cparams(("parallel",)))(w, m, v, *grads)


MESH = pl.DeviceIdType.MESH
HBM = pl.BlockSpec(memory_space=pl.ANY)
CHIP_FLIPS = ((1, 0), (0, 1), (1, 1))


def _place():
    return lax.axis_index("x"), lax.axis_index("y"), lax.axis_index("c")


def allgather_chips(pack):
    r, c = pack.shape

    def body(src_ref, out_ref, send_sems, recv_sems, local_sem):
        x, y, cc = _place()
        mine = pltpu.make_async_copy(src_ref, out_ref.at[2 * x + y], local_sem)
        mine.start()
        sends, recvs = [], []
        for k, (fx, fy) in enumerate(CHIP_FLIPS):
            px, py = (1 - x if fx else x), (1 - y if fy else y)
            sends.append(pltpu.make_async_remote_copy(
                src_ref=src_ref, dst_ref=out_ref.at[2 * x + y], send_sem=send_sems.at[k], recv_sem=recv_sems.at[k],
                device_id=(px, py, cc), device_id_type=MESH))
            recvs.append(pltpu.make_async_remote_copy(
                src_ref=src_ref, dst_ref=out_ref.at[2 * px + py], send_sem=send_sems.at[k], recv_sem=recv_sems.at[k],
                device_id=(px, py, cc), device_id_type=MESH))
        for cp in sends:
            cp.start()
        for cp in recvs:
            cp.wait_recv()
        for cp in sends:
            cp.wait_send()
        mine.wait()

    return pl.pallas_call(
        body, name="allgather_chips", in_specs=[HBM], out_specs=HBM,
        out_shape=jax.ShapeDtypeStruct((4, r, c), pack.dtype),
        scratch_shapes=[pltpu.SemaphoreType.DMA((3,)), pltpu.SemaphoreType.DMA((3,)), pltpu.SemaphoreType.DMA],
    )(pack)


def scatter_chips(gpack):
    def body(src_ref, out_ref, send_sems, recv_sems, local_sem):
        x, y, cc = _place()
        mine = pltpu.make_async_copy(src_ref.at[2 * x + y], out_ref.at[3], local_sem)
        mine.start()
        sends = []
        for k, (fx, fy) in enumerate(CHIP_FLIPS):
            px, py = (1 - x if fx else x), (1 - y if fy else y)
            sends.append(pltpu.make_async_remote_copy(
                src_ref=src_ref.at[2 * px + py], dst_ref=out_ref.at[k], send_sem=send_sems.at[k], recv_sem=recv_sems.at[k],
                device_id=(px, py, cc), device_id_type=MESH))
        for cp in sends:
            cp.start()
        for cp in sends:
            cp.wait_recv()
        for cp in sends:
            cp.wait_send()
        mine.wait()

    return pl.pallas_call(
        body, name="scatter_chips", in_specs=[HBM], out_specs=HBM,
        out_shape=jax.ShapeDtypeStruct(gpack.shape, gpack.dtype),
        scratch_shapes=[pltpu.SemaphoreType.DMA((3,)), pltpu.SemaphoreType.DMA((3,)), pltpu.SemaphoreType.DMA],
    )(gpack)


def swap_sibling(part):
    def body(src_ref, out_ref, send_sem, recv_sem):
        x, y, cc = _place()
        cp = pltpu.make_async_remote_copy(src_ref=src_ref, dst_ref=out_ref, send_sem=send_sem, recv_sem=recv_sem,
                                          device_id=(x, y, 1 - cc), device_id_type=MESH)
        cp.start()
        cp.wait()

    return pl.pallas_call(
        body, name="swap_sibling", in_specs=[HBM], out_specs=HBM,
        out_shape=jax.ShapeDtypeStruct(part.shape, part.dtype),
        scratch_shapes=[pltpu.SemaphoreType.DMA, pltpu.SemaphoreType.DMA],
    )(part)


def exchange_all(v, name):
    r, c = v.shape

    def body(v_ref, out_ref, send_sems, recv_sems):
        x, y, cc = _place()
        out_ref[4 * x + 2 * y + cc] = v_ref[...]
        sends, recvs = [], []
        for k in range(1, 8):
            px = 1 - x if k & 4 else x
            py = 1 - y if k & 2 else y
            pc = 1 - cc if k & 1 else cc
            sends.append(pltpu.make_async_remote_copy(
                src_ref=v_ref, dst_ref=out_ref.at[4 * x + 2 * y + cc], send_sem=send_sems.at[k - 1],
                recv_sem=recv_sems.at[k - 1], device_id=(px, py, pc), device_id_type=MESH))
            recvs.append(pltpu.make_async_remote_copy(
                src_ref=v_ref, dst_ref=out_ref.at[4 * px + 2 * py + pc], send_sem=send_sems.at[k - 1],
                recv_sem=recv_sems.at[k - 1], device_id=(px, py, pc), device_id_type=MESH))
        for cp in sends:
            cp.start()
        for cp in recvs:
            cp.wait_recv()
        for cp in sends:
            cp.wait_send()

    vm = pl.BlockSpec(memory_space=pltpu.VMEM)
    return pl.pallas_call(
        body, name=name, in_specs=[vm], out_specs=vm, out_shape=jax.ShapeDtypeStruct((8, r, c), v.dtype),
        scratch_shapes=[pltpu.SemaphoreType.DMA((7,)), pltpu.SemaphoreType.DMA((7,))],
        compiler_params=pltpu.CompilerParams(vmem_limit_bytes=VMEM_LIMIT),
    )(v)


def _pack_rows(arrays, rows, width):
    flat = jnp.concatenate([a.reshape(-1) for a in arrays])
    return jnp.pad(flat, (0, rows * width - flat.shape[0])).reshape(rows, width)


def _unpack_rows(pack, shapes):
    flat = pack.reshape(-1)
    out, off = [], 0
    for shp in shapes:
        n = math.prod(shp)
        out.append(flat[off:off + n].reshape(shp))
        off += n
    return out


def _rows_for(shapes, width, mult=8):
    n = sum(math.prod(s) for s in shapes)
    return -(-n // (width * mult)) * mult


def _gdn_fwd(x, p):
    proj = mm(x, p["w_main"], name="gdn_proj")
    ba = mm(x, p["w_gate"], name="gdn_proj_gate")
    q, k, v = (gdn_pre(proj, p["conv"], kind) for kind in "qkv")
    beta_b, g_b = gdn_gates(ba, p["alog_row"], p["dt_row"])
    o2, st = gdn_rec_fwd(q, k, v, beta_b, g_b)
    y = mixer_post(o2, proj, p["norm_g"], A_DK, 3 * A_HEADS, "gdn_post")
    m = mm(y, p["w_out"], name="gdn_out")
    return m, (x, proj, ba, q, k, v, beta_b, g_b, o2, st, y)


def _gdn_bwd(saved, p, dm):
    x, proj, ba, q, k, v, beta_b, g_b, o2, st, y = saved
    d_w_out = mm(y, dm, "tn", name="gdn_dw_out")
    dy = mm(dm, p["w_out"], "nt", name="gdn_dy")
    do, dz, d_norm_g = mixer_post_bwd(o2, proj, p["norm_g"], dy, A_DK, 3 * A_HEADS, "gdn_post_bwd")
    dq2, dk2, dv2, dbb, dgb = gdn_rec_bwd(q, k, v, beta_b, g_b, st, do)
    dba, d_alog_row, d_dt_row = gdn_gates_bwd(ba, p["alog_row"], p["dt_row"], dbb, dgb)
    du, dconv = zip(*(gdn_pre_bwd(proj, p["conv"], d2, kind) for d2, kind in ((dq2, "q"), (dk2, "k"), (dv2, "v"))))
    dproj = jnp.concatenate(list(du) + [dz], axis=1)
    d_w_main = mm(x, dproj, "tn", name="gdn_dw_main")
    d_w_gate = mm(x, dba, "tn", name="gdn_dw_gate")
    dx = mm(dba, p["w_gate"], "nt", epi="add", extra=dm, alpha=ALPHA, name="gdn_dx_gate")
    dx = mm(dproj, p["w_main"], "nt", epi="add", extra=dx, name="gdn_dx")
    grads = dict(w_in=jnp.concatenate([d_w_main, d_w_gate[:, :2 * N_GATE]], axis=1), conv=jnp.concatenate(dconv, axis=1),
                 alog=d_alog_row[0, N_GATE:2 * N_GATE].reshape(2, A_HEADS), dt=d_dt_row[0, N_GATE:2 * N_GATE].reshape(2, A_HEADS),
                 norm_g=d_norm_g[0], w_out=d_w_out)
    return dx, grads


def _gla_fwd(x, p):
    proj = mm(x, p["w_main"], name="gla_proj")
    gl = mm(x, p["w_gate"], name="gla_proj_gate")
    log_a = gla_gate(gl, p["w2"], p["gate_b"])
    o2, st = gla_rec_fwd(proj, log_a)
    y = mixer_post(o2, proj, p["norm_g"], B_DV, (2 * B_KW + B_VW) // B_DV, "gla_post")
    m = mm(y, p["w_out"], name="gla_out")
    return m, (x, proj, gl, log_a, o2, st, y)


def _gla_bwd(saved, p, dm):
    x, proj, gl, log_a, o2, st, y = saved
    d_w_out = mm(y, dm, "tn", name="gla_dw_out")
    dy = mm(dm, p["w_out"], "nt", name="gla_dy")
    do, dr, d_norm_g = mixer_post_bwd(o2, proj, p["norm_g"], dy, B_DV, (2 * B_KW + B_VW) // B_DV, "gla_post_bwd")
    dq2, dk2, dv2, dla = gla_rec_bwd(proj, log_a, st, do)
    dgl, dz, d_b0, d_b1 = gla_gate_bwd(gl, p["w2"], p["gate_b"], dla)
    d_w2 = [mm(gl, dz[n], "tn", name="gla_dw_gate_w2") for n in range(2)]
    dproj = jnp.concatenate([dq2[0] + dq2[1], dk2[0] + dk2[1], dv2[0] + dv2[1], dr], axis=1)
    d_w_main = mm(x, dproj, "tn", name="gla_dw_main")
    d_w_gate = mm(x, dgl, "tn", name="gla_dw_gate")
    dx = mm(dgl, p["w_gate"], "nt", epi="add", extra=dm, alpha=ALPHA, name="gla_dx_gate")
    dx = mm(dproj, p["w_main"], "nt", epi="add", extra=dx, name="gla_dx")
    grads = dict(w_in=jnp.concatenate([d_w_main, d_w_gate[:, :2 * B_RANK]], axis=1),
                 gate_w2=jnp.stack([d_w2[n][n * B_RANK:(n + 1) * B_RANK] for n in range(2)]),
                 gate_b=jnp.concatenate([d_b0, d_b1]), norm_g=d_norm_g[0], w_out=d_w_out)
    return dx, grads


def _pad_cols(w, width=LANES):
    return jnp.pad(w, ((0, 0), (0, width - w.shape[1])))


def _local_step(x, target, a_w_in, a_conv, a_alog, a_dt_bias, a_norm_g, a_w_out, b_w_in, b_gate_w2, b_gate_b, b_norm_g,
                b_w_out, ln1_g, ln1_b, mlp_w1, mlp_w2, ln2_g, ln2_b):
    layer_p = []
    for i in range(DEPTH):
        j = i // 2
        if i % 2 == 0:
            layer_p.append(dict(
                w_main=a_w_in[j][:, :4 * A_W], w_gate=_pad_cols(a_w_in[j][:, 4 * A_W:]), conv=a_conv[j],
                alog_row=jnp.pad(a_alog[j].reshape(1, N_GATE), ((0, 0), (N_GATE, LANES - 2 * N_GATE))),
                dt_row=jnp.pad(a_dt_bias[j].reshape(1, N_GATE), ((0, 0), (N_GATE, LANES - 2 * N_GATE))),
                norm_g=a_norm_g[j].reshape(1, A_DK), w_out=a_w_out[j]))
        else:
            w2 = jnp.stack([jnp.pad(b_gate_w2[j][n], ((n * B_RANK, LANES - (n + 1) * B_RANK), (0, 0))) for n in range(2)])
            layer_p.append(dict(
                w_main=b_w_in[j][:, :2 * B_KW + 2 * B_VW], w_gate=_pad_cols(b_w_in[j][:, 2 * B_KW + 2 * B_VW:]),
                w2=w2, gate_b=b_gate_b[j].reshape(2, 1, B_KW), norm_g=b_norm_g[j].reshape(1, B_DV), w_out=b_w_out[j]))

    saved = []
    h = x
    for i in range(DEPTH):
        p = layer_p[i]
        m, sv = (_gdn_fwd if i % 2 == 0 else _gla_fwd)(h, p)
        x1 = ln_fwd(h, m, ln1_g[i:i + 1], ln1_b[i:i + 1])
        h1 = mm(x1, mlp_w1[i], name="mlp_up")
        mlp = mm(h1, mlp_w2[i], act="sqrelu", name="mlp_down")
        x2 = ln_fwd(x1, mlp, ln2_g[i:i + 1], ln2_b[i:i + 1])
        saved.append((sv, h, m, x1, h1, mlp))
        h = x2

    dh, loss_part = loss_head(h, target)

    g_a, g_b, g_ln1g, g_ln1b, g_ln2g, g_ln2b, g_w1, g_w2 = {}, {}, {}, {}, {}, {}, {}, {}
    for i in reversed(range(DEPTH)):
        sv, xin, m, x1, h1, mlp = saved[i]
        p = layer_p[i]
        dr2, g_ln2g[i], g_ln2b[i] = ln_bwd(x1, mlp, ln2_g[i:i + 1], dh)
        g_w2[i] = mm(h1, dr2, "tn", act="sqrelu", name="mlp_dw_down")
        dh1 = mm(dr2, mlp_w2[i], "nt", epi="dsqrelu", extra=h1, name="mlp_dh")
        g_w1[i] = mm(x1, dh1, "tn", name="mlp_dw_up")
        dx1 = mm(dh1, mlp_w1[i], "nt", epi="add", extra=dr2, alpha=ALPHA, name="mlp_dx")
        dr1, g_ln1g[i], g_ln1b[i] = ln_bwd(xin, m, ln1_g[i:i + 1], dx1)
        dh, g = (_gdn_bwd if i % 2 == 0 else _gla_bwd)(sv, p, dr1)
        (g_a if i % 2 == 0 else g_b)[i // 2] = g

    st = lambda d, key=None: jnp.stack([(d[i] if key is None else d[i][key]) for i in sorted(d)])
    grads = dict(
        a_w_in=st(g_a, "w_in"), a_conv=st(g_a, "conv"), a_alog=st(g_a, "alog"), a_dt_bias=st(g_a, "dt"),
        a_norm_g=st(g_a, "norm_g"), a_w_out=st(g_a, "w_out"), b_w_in=st(g_b, "w_in"), b_gate_w2=st(g_b, "gate_w2"),
        b_gate_b=st(g_b, "gate_b"), b_norm_g=st(g_b, "norm_g"), b_w_out=st(g_b, "w_out"),
        ln1_g=st(g_ln1g)[:, 0], ln1_b=st(g_ln1b)[:, 0], mlp_w1=st(g_w1), mlp_w2=st(g_w2),
        ln2_g=st(g_ln2g)[:, 0], ln2_b=st(g_ln2b)[:, 0])
    return loss_part, dh, grads


WEIGHTS = ("a_w_in", "a_conv", "a_alog", "a_dt_bias", "a_norm_g", "a_w_out", "b_w_in", "b_gate_w2", "b_gate_b",
           "b_norm_g", "b_w_out", "ln1_g", "ln1_b", "mlp_w1", "mlp_w2", "ln2_g", "ln2_b")
BIG = ("mlp_w1", "mlp_w2", "a_w_out", "b_w_out", "a_w_in", "b_w_in")
SHARD_AXIS = {"mlp_w1": 2, "mlp_w2": 1, "a_w_out": 1, "b_w_out": 1, "a_w_in": 2, "b_w_in": 2}
SMALL = tuple(n for n in WEIGHTS if n not in BIG)
SMALL_SHARD_AXIS = {"a_conv": 2, "b_gate_w2": 3, "b_gate_b": 2, "b_norm_g": 1}


def _to_chip_major(full, axis):
    shp = full.shape
    t = full.reshape(shp[:axis] + (4, shp[axis] // 4) + shp[axis + 1:])
    return jnp.moveaxis(t, axis, 0)


def _from_chip_major(stacked, axis):
    t = jnp.moveaxis(stacked, 0, axis)
    shp = t.shape
    return t.reshape(shp[:axis] + (shp[axis] * shp[axis + 1],) + shp[axis + 2:])


def kernel(x, a_w_in, a_conv, a_alog, a_dt_bias, a_norm_g, a_w_out, b_w_in, b_gate_w2, b_gate_b, b_norm_g, b_w_out, ln1_g, ln1_b, mlp_w1, mlp_w2, ln2_g, ln2_b, loss_target, m_a_w_in, m_a_conv, m_a_alog, m_a_dt_bias, m_a_norm_g, m_a_w_out, m_b_w_in, m_b_gate_w2, m_b_gate_b, m_b_norm_g, m_b_w_out, m_ln1_g, m_ln1_b, m_mlp_w1, m_mlp_w2, m_ln2_g, m_ln2_b, v_a_w_in, v_a_conv, v_a_alog, v_a_dt_bias, v_a_norm_g, v_a_w_out, v_b_w_in, v_b_gate_w2, v_b_gate_b, v_b_norm_g, v_b_w_out, v_ln1_g, v_ln1_b, v_mlp_w1, v_mlp_w2, v_ln2_g, v_ln2_b):
    w = dict(a_w_in=a_w_in, a_conv=a_conv, a_alog=a_alog, a_dt_bias=a_dt_bias, a_norm_g=a_norm_g, a_w_out=a_w_out,
             b_w_in=b_w_in, b_gate_w2=b_gate_w2, b_gate_b=b_gate_b, b_norm_g=b_norm_g, b_w_out=b_w_out, ln1_g=ln1_g,
             ln1_b=ln1_b, mlp_w1=mlp_w1, mlp_w2=mlp_w2, ln2_g=ln2_g, ln2_b=ln2_b)
    mom = dict(a_w_in=m_a_w_in, a_conv=m_a_conv, a_alog=m_a_alog, a_dt_bias=m_a_dt_bias, a_norm_g=m_a_norm_g,
               a_w_out=m_a_w_out, b_w_in=m_b_w_in, b_gate_w2=m_b_gate_w2, b_gate_b=m_b_gate_b, b_norm_g=m_b_norm_g,
               b_w_out=m_b_w_out, ln1_g=m_ln1_g, ln1_b=m_ln1_b, mlp_w1=m_mlp_w1, mlp_w2=m_mlp_w2, ln2_g=m_ln2_g,
               ln2_b=m_ln2_b)
    var = dict(a_w_in=v_a_w_in, a_conv=v_a_conv, a_alog=v_a_alog, a_dt_bias=v_a_dt_bias, a_norm_g=v_a_norm_g,
               a_w_out=v_a_w_out, b_w_in=v_b_w_in, b_gate_w2=v_b_gate_w2, b_gate_b=v_b_gate_b, b_norm_g=v_b_norm_g,
               b_w_out=v_b_w_out, ln1_g=v_ln1_g, ln1_b=v_ln1_b, mlp_w1=v_mlp_w1, mlp_w2=v_mlp_w2, ln2_g=v_ln2_g,
               ln2_b=v_ln2_b)
    chip = 2 * lax.axis_index("x") + lax.axis_index("y")

    seg_rows = [w[n].size // D_MODEL for n in BIG]
    seg_off = [sum(seg_rows[:i]) for i in range(len(BIG))]
    rows = -(-sum(seg_rows) // PACK_TILE) * PACK_TILE
    shard_pack = _pack_rows([w[n] for n in BIG], rows, D_MODEL)
    gathered = allgather_chips(cast_bf16(shard_pack))
    full = {}
    for n, off, nr in zip(BIG, seg_off, seg_rows):
        stacked = gathered[:, off:off + nr].reshape((4,) + w[n].shape)
        full[n] = _from_chip_major(stacked, SHARD_AXIS[n])
    sharded_small = tuple(SMALL_SHARD_AXIS)
    sm_shapes = [w[n].shape for n in sharded_small]
    sm_rows = _rows_for(sm_shapes, LANES)
    sm_all = exchange_all(_pack_rows([w[n] for n in sharded_small], sm_rows, LANES), "gather_small")
    per_chip = [_unpack_rows(sm_all[2 * pch], sm_shapes) for pch in range(4)]
    for idx, n in enumerate(sharded_small):
        full[n] = jnp.concatenate([per_chip[pch][idx] for pch in range(4)], axis=SMALL_SHARD_AXIS[n])
    for n in WEIGHTS:
        full.setdefault(n, w[n])

    loss_part, grad_x, grads = _local_step(x[0], loss_target[0], *[full[n] for n in WEIGHTS])
    loss = lax.psum(jnp.sum(loss_part), ("x", "y", "c"))

    gpack = jnp.concatenate(
        [_to_chip_major(grads[n], SHARD_AXIS[n]).reshape(4, nr, D_MODEL) for n, nr in zip(BIG, seg_rows)]
        + [jnp.zeros((4, rows - sum(seg_rows), D_MODEL), F32)], axis=1)
    part = sum_slots(scatter_chips(gpack), "sum_chips")
    other = swap_sibling(part)
    out_g, out_d, out_m, out_v = {}, {}, {}, {}
    for n, off, nr in zip(BIG, seg_off, seg_rows):
        if w[n].shape[-1] == D_MODEL:
            view = lambda t: t.reshape(-1, D_MODEL)
            res = adamw(view(w[n]), view(mom[n]), view(var[n]), (part, other), off, "adamw_" + n)
        else:
            cols = w[n].shape[-1]
            view = lambda t: t.reshape(-1, cols)
            gs = tuple(view(t[off:off + nr]) for t in (part, other))
            res = adamw(view(w[n]), view(mom[n]), view(var[n]), gs, 0, "adamw_" + n)
        out_g[n], out_d[n], out_m[n], out_v[n] = (t.reshape(w[n].shape) for t in res)

    all_shapes = [full[n].shape for n in SMALL]
    g_rows = _rows_for(all_shapes, LANES)
    g_all = exchange_all(_pack_rows([grads[n] for n in SMALL], g_rows, LANES), "gather_small_grads")
    g_sum = _unpack_rows(sum_slots(g_all, "sum_small_grads"), all_shapes)
    g_mine = []
    for n, g in zip(SMALL, g_sum):
        if n in SMALL_SHARD_AXIS:
            ax = SMALL_SHARD_AXIS[n]
            g = lax.dynamic_slice_in_dim(g, chip * w[n].shape[ax], w[n].shape[ax], axis=ax)
        g_mine.append(g)
    my_shapes = [w[n].shape for n in SMALL]
    s_rows = _rows_for(my_shapes, LANES)
    pk = lambda d: _pack_rows([d[n] for n in SMALL], s_rows, LANES)
    res = adamw(pk(w), pk(mom), pk(var), (_pack_rows(g_mine, s_rows, LANES),), 0, "adamw_small")
    for dst, pack in zip((out_g, out_d, out_m, out_v), res):
        for n, t in zip(SMALL, _unpack_rows(pack, my_shapes)):
            dst[n] = t

    return (loss, grad_x[None], *[out_g[n] for n in WEIGHTS], *[out_d[n] for n in WEIGHTS],
            *[out_m[n] for n in WEIGHTS], *[out_v[n] for n in WEIGHTS])
```

```python
import functools
import math

import jax
import jax.numpy as jnp
from jax import lax
from jax.experimental import pallas as pl
from jax.experimental.pallas import tpu as pltpu

F32 = jnp.float32
BF16 = jnp.bfloat16

D_MODEL = 1024
DEPTH = 4
CHUNK = 64
A_HEADS = 8
A_DK = 128
A_W = 1024
A_CONV = 5
B_HEADS = 4
B_DK = 128
B_DV = 256
B_RANK = 16
B_TAU = 16.0
B_KW = 512
B_VW = 1024
ALPHA = (2 * DEPTH) ** 0.25
LN_EPS = 1e-5
RMS_EPS = 1e-6
L2_EPS = 1e-6
ADAM_LR = 0.001
ADAM_B1 = 0.9
ADAM_B2 = 0.999
ADAM_EPS = 1e-08
ADAM_WD = 0.01
ADAM_STEP = 10
LANES = 128
NEG_INF = float("-inf")
VMEM_LIMIT = 56 * 1024 * 1024


def _cparams(sem=None):
    return pltpu.CompilerParams(dimension_semantics=sem, vmem_limit_bytes=VMEM_LIMIT)


def _dg(a, b, ca, cb):
    return lax.dot_general(a.astype(BF16), b.astype(BF16), (((ca,), (cb,)), ((), ())),
                           preferred_element_type=F32)


@functools.partial(jax.custom_vjp, nondiff_argnums=(2, 3))
def bdot(a, b, ca, cb):
    return _dg(a, b, ca, cb)


def _bdot_fwd(a, b, ca, cb):
    return _dg(a, b, ca, cb), (a, b)


def _bdot_bwd(ca, cb, res, g):
    a, b = res
    da = _dg(g, b, 1, 1 - cb) if ca == 1 else _dg(b, g, 1 - cb, 1)
    db = _dg(a, g, 1 - ca, 0) if cb == 0 else _dg(g, a, 0, 1 - ca)
    return da, db


bdot.defvjp(_bdot_fwd, _bdot_bwd)


def nn(a, b):
    return bdot(a, b, 1, 0)


def nt(a, b):
    return bdot(a, b, 1, 1)


def tn(a, b):
    return bdot(a, b, 0, 0)


def xdot(a, b):
    return jnp.dot(a, b, precision=lax.Precision.HIGHEST, preferred_element_type=F32)


def _sigmoid(x):
    return 1.0 / (1.0 + jnp.exp(-x))


def _softplus(x):
    return jnp.maximum(x, 0.0) + jnp.log(1.0 + jnp.exp(-jnp.abs(x)))


def _chunk_masks(rev):
    ii = lax.broadcasted_iota(jnp.int32, (CHUNK, CHUNK), 0)
    jj = lax.broadcasted_iota(jnp.int32, (CHUNK, CHUNK), 1)
    d = (ii - jj) * (1 - 2 * rev)
    return d >= 0, d > 0, ii == jj, (ii >> 3) == (jj >> 3)


def _each(f, *lists):
    return [f(*xs) for xs in zip(*lists)]


def _unit_triangular_inverse(a, eye, blockdiag):
    ident = eye.astype(F32)
    ad = _each(lambda x: jnp.where(blockdiag, x, 0.0), a)
    e = _each(lambda x, y: x - y, a, ad)
    dinv = _each(lambda x: ident - x, ad)
    p = _each(xdot, ad, ad)
    dinv = _each(lambda x, y: x + xdot(x, y), dinv, p)
    p = _each(xdot, p, p)
    dinv = _each(lambda x, y: x + xdot(x, y), dinv, p)
    g = _each(lambda x, y: -xdot(x, y), dinv, e)
    finv = _each(lambda x: ident + x, g)
    p = _each(xdot, g, g)
    finv = _each(lambda x, y: x + xdot(x, y), finv, p)
    p = _each(xdot, p, p)
    finv = _each(lambda x, y: x + xdot(x, y), finv, p)
    return _each(xdot, finv, dinv)


def _gdn_step(state, q, k, v, bb, gb, rev):
    causal, strict, eye, blockdiag = _chunk_masks(rev)
    lower = causal.astype(F32)
    ones = jnp.ones((CHUNK, CHUNK), F32)
    gcb = _each(lambda x: xdot(lower, x), gb)
    gcol = _each(lambda x: x[:, :CHUNK], gcb)
    grow = _each(lambda x: xdot(ones, jnp.where(eye, x, 0.0)), gcol)
    decay = _each(lambda x, y: jnp.exp(jnp.where(causal, x - y, NEG_INF)), gcol, grow)
    kb = _each(lambda x, y: x * y, k, bb)
    a = _each(lambda x, y, z: jnp.where(strict, nt(x, y) * z, 0.0), kb, k, decay)
    t = _unit_triangular_inverse(a, eye, blockdiag)
    egc = _each(jnp.exp, gcb)
    u = _each(lambda x, y, z: xdot(x, y * z), t, v, bb)
    w = _each(lambda x, y, z: xdot(x, y * z), t, kb, egc)
    qk = _each(lambda x, y, z: nt(x, y) * z, q, k, decay)
    glast = _each(lambda x: jnp.sum(x, axis=0, keepdims=True), gb)
    v_new = _each(lambda x, y, z: x - nn(y, z), u, w, state)
    o = _each(lambda x, y, z, p, r: nn(x * y, z) + nn(p, r), q, egc, state, qk, v_new)
    k_dec = _each(lambda x, y, z: x * jnp.exp(y - z), k, glast, gcb)
    state_new = _each(lambda x, y, z, p: x * jnp.exp(y) + tn(z, p), state, glast, k_dec, v_new)
    return state_new, o


def _gla_step(state_t, q, k, v, la, rev):
    causal, _, _, _ = _chunk_masks(rev)
    b = xdot(causal.astype(F32), la)
    q = q * (B_DK ** -0.5)
    shp = (CHUNK, CHUNK, B_DK)
    d3 = (lax.broadcasted_iota(jnp.int32, shp, 0) - lax.broadcasted_iota(jnp.int32, shp, 1)) * (1 - 2 * rev)
    dec = jnp.exp(jnp.where(d3 >= 0, b[:, None, :] - b[None, :, :], NEG_INF))
    scores = jnp.sum(q[:, None, :] * k[None, :, :] * dec, axis=-1)
    blast = jnp.sum(la, axis=0, keepdims=True)
    o = nt(q * jnp.exp(b), state_t) + nn(scores, v)
    state_new = jnp.exp(blast) * state_t + tn(v, k * jnp.exp(blast - b))
    return state_new, o


def _chunk_pos(d, m, n):
    return m + d * (n - 1 - 2 * m)


GDN_HEADS_PER_STEP = 8
def gdn_rec_fwd(q, k, v, beta_b, g_b):
    s = q.shape[0]
    n = s // CHUNK

    hb = GDN_HEADS_PER_STEP
    wide = hb * LANES

    def body(q_ref, k_ref, v_ref, bb_ref, gb_ref, o_ref, st_ref, state):
        d = pl.program_id(0)

        @pl.when(pl.program_id(2) == 0)
        def _():
            state[...] = jnp.zeros_like(state)

        cols = [slice(hh * LANES, (hh + 1) * LANES) for hh in range(hb)]
        st = [state[hh] for hh in range(hb)]
        new, o = _gdn_step(st, *([r[:, c] for c in cols] for r in (q_ref, k_ref, v_ref, bb_ref, gb_ref)), d)
        for hh, c in enumerate(cols):
            st_ref[hh] = st[hh]
            state[hh] = new[hh]
            o_ref[:, c] = o[hh]

    blk = pl.BlockSpec((CHUNK, wide), lambda d, h, m: (_chunk_pos(d, m, n), h))
    gate = pl.BlockSpec((CHUNK, wide), lambda d, h, m: (_chunk_pos(d, m, n), d * (A_HEADS // hb) + h))
    return pl.pallas_call(
        body, name="gdn_rec_fwd", grid=(2, A_HEADS // hb, n),
        in_specs=[blk, blk, blk, gate, gate],
        out_specs=[pl.BlockSpec((None, CHUNK, wide), lambda d, h, m: (d, _chunk_pos(d, m, n), h)),
                   pl.BlockSpec((None, hb, None, A_DK, LANES), lambda d, h, m: (d, h, _chunk_pos(d, m, n), 0, 0))],
        out_shape=[jax.ShapeDtypeStruct((2, s, A_W), F32), jax.ShapeDtypeStruct((2, A_HEADS, n, A_DK, LANES), F32)],
        scratch_shapes=[pltpu.VMEM((hb, A_DK, LANES), F32)],
        compiler_params=_cparams(("arbitrary", "arbitrary", "arbitrary")),
    )(q, k, v, beta_b, g_b)


def gdn_rec_bwd(q, k, v, beta_b, g_b, states, do):
    s = q.shape[0]
    n = s // CHUNK

    hb = GDN_HEADS_PER_STEP
    wide = hb * LANES

    def body(q_ref, k_ref, v_ref, bb_ref, gb_ref, st_ref, do_ref, dq_ref, dk_ref, dv_ref, dbb_ref, dgb_ref, dstate):
        d = pl.program_id(0)

        @pl.when(pl.program_id(2) == 0)
        def _():
            dstate[...] = jnp.zeros_like(dstate)

        step = functools.partial(_gdn_step, rev=d)
        cols = [slice(hh * LANES, (hh + 1) * LANES) for hh in range(hb)]
        _, vjp = jax.vjp(step, [st_ref[hh] for hh in range(hb)],
                         *([r[:, c] for c in cols] for r in (q_ref, k_ref, v_ref, bb_ref, gb_ref)))
        grads = vjp(([dstate[hh] for hh in range(hb)], [do_ref[:, c] for c in cols]))
        for hh, c in enumerate(cols):
            dstate[hh], dq_ref[:, c], dk_ref[:, c], dv_ref[:, c], dbb_ref[:, c], dgb_ref[:, c] = (g[hh] for g in grads)

    pos = lambda d, m: _chunk_pos(1 - d, m, n)
    blk = pl.BlockSpec((CHUNK, wide), lambda d, h, m: (pos(d, m), h))
    gate = pl.BlockSpec((CHUNK, wide), lambda d, h, m: (pos(d, m), d * (A_HEADS // hb) + h))
    oblk = pl.BlockSpec((None, CHUNK, wide), lambda d, h, m: (d, pos(d, m), h))
    return pl.pallas_call(
        body, name="gdn_rec_bwd", grid=(2, A_HEADS // hb, n),
        in_specs=[blk, blk, blk, gate, gate,
                  pl.BlockSpec((None, hb, None, A_DK, LANES), lambda d, h, m: (d, h, pos(d, m), 0, 0)), blk],
        out_specs=[oblk, oblk, oblk, gate, gate],
        out_shape=[jax.ShapeDtypeStruct((2, s, A_W), F32)] * 3 + [jax.ShapeDtypeStruct(beta_b.shape, F32)] * 2,
        scratch_shapes=[pltpu.VMEM((hb, A_DK, LANES), F32)],
        compiler_params=_cparams(("arbitrary", "arbitrary", "arbitrary")),
    )(q, k, v, beta_b, g_b, states, do)


def gla_rec_fwd(proj, log_a):
    s = proj.shape[0]
    n = s // CHUNK

    def body(q_ref, k_ref, v_ref, la_ref, o_ref, st_ref, state):
        d = pl.program_id(0)

        @pl.when(pl.program_id(2) == 0)
        def _():
            state[...] = jnp.zeros_like(state)

        st = state[...]
        st_ref[...] = st
        new, o = _gla_step(st, q_ref[...], k_ref[...], v_ref[...], la_ref[...], d)
        state[...] = new
        o_ref[...] = o

    return pl.pallas_call(
        body, name="gla_rec_fwd", grid=(2, B_HEADS, n),
        in_specs=[pl.BlockSpec((CHUNK, B_DK), lambda d, h, m: (_chunk_pos(d, m, n), h)),
                  pl.BlockSpec((CHUNK, B_DK), lambda d, h, m: (_chunk_pos(d, m, n), B_KW // B_DK + h)),
                  pl.BlockSpec((CHUNK, B_DV), lambda d, h, m: (_chunk_pos(d, m, n), 2 * B_KW // B_DV + h)),
                  pl.BlockSpec((None, CHUNK, B_DK), lambda d, h, m: (d, _chunk_pos(d, m, n), h))],
        out_specs=[pl.BlockSpec((None, CHUNK, B_DV), lambda d, h, m: (d, _chunk_pos(d, m, n), h)),
                   pl.BlockSpec((None, None, None, B_DV, B_DK), lambda d, h, m: (d, h, _chunk_pos(d, m, n), 0, 0))],
        out_shape=[jax.ShapeDtypeStruct((2, s, B_VW), F32), jax.ShapeDtypeStruct((2, B_HEADS, n, B_DV, B_DK), F32)],
        scratch_shapes=[pltpu.VMEM((B_DV, B_DK), F32)],
        compiler_params=_cparams(("arbitrary", "arbitrary", "arbitrary")),
    )(proj, proj, proj, log_a)


def gla_rec_bwd(proj, log_a, states, do):
    s = proj.shape[0]
    n = s // CHUNK

    def body(q_ref, k_ref, v_ref, la_ref, st_ref, do_ref, dq_ref, dk_ref, dv_ref, dla_ref, dstate):
        d = pl.program_id(0)

        @pl.when(pl.program_id(2) == 0)
        def _():
            dstate[...] = jnp.zeros_like(dstate)

        step = functools.partial(_gla_step, rev=d)
        _, vjp = jax.vjp(step, st_ref[...], q_ref[...], k_ref[...], v_ref[...], la_ref[...])
        dst, dq, dk, dv, dla = vjp((dstate[...], do_ref[...]))
        dstate[...] = dst
        dq_ref[...] = dq
        dk_ref[...] = dk
        dv_ref[...] = dv
        dla_ref[...] = dla

    pos = lambda d, m: _chunk_pos(1 - d, m, n)
    kblk = pl.BlockSpec((None, CHUNK, B_DK), lambda d, h, m: (d, pos(d, m), h))
    return pl.pallas_call(
        body, name="gla_rec_bwd", grid=(2, B_HEADS, n),
        in_specs=[pl.BlockSpec((CHUNK, B_DK), lambda d, h, m: (pos(d, m), h)),
                  pl.BlockSpec((CHUNK, B_DK), lambda d, h, m: (pos(d, m), B_KW // B_DK + h)),
                  pl.BlockSpec((CHUNK, B_DV), lambda d, h, m: (pos(d, m), 2 * B_KW // B_DV + h)),
                  kblk,
                  pl.BlockSpec((None, None, None, B_DV, B_DK), lambda d, h, m: (d, h, pos(d, m), 0, 0)),
                  pl.BlockSpec((CHUNK, B_DV), lambda d, h, m: (pos(d, m), h))],
        out_specs=[kblk, kblk, pl.BlockSpec((None, CHUNK, B_DV), lambda d, h, m: (d, pos(d, m), h)), kblk],
        out_shape=[jax.ShapeDtypeStruct((2, s, B_KW), F32), jax.ShapeDtypeStruct((2, s, B_KW), F32),
                   jax.ShapeDtypeStruct((2, s, B_VW), F32), jax.ShapeDtypeStruct((2, s, B_KW), F32)],
        scratch_shapes=[pltpu.VMEM((B_DV, B_DK), F32)],
        compiler_params=_cparams(("arbitrary", "arbitrary", "arbitrary")),
    )(proj, proj, proj, log_a, states, do)


MM_TILE_OUT = 1024
MM_TILE_K = 512


def _tile(n, pref):
    return pref if n % pref == 0 else n


def mm(a, b, mode="nn", act=None, epi=None, extra=None, alpha=1.0, name="mm"):
    if mode == "tn":
        kk, m = a.shape
    else:
        m, kk = a.shape
    nn_ = b.shape[0] if mode == "nt" else b.shape[1]
    tm, tn_, tk = _tile(m, MM_TILE_OUT), _tile(nn_, MM_TILE_OUT), _tile(kk, MM_TILE_K)
    nk = kk // tk
    ca, cb = {"nn": (1, 0), "nt": (1, 1), "tn": (0, 0)}[mode]

    def body(*refs):
        if epi is None:
            a_ref, b_ref, o_ref = refs
        else:
            a_ref, b_ref, e_ref, o_ref = refs
        kstep = pl.program_id(2)
        at = a_ref[...]
        if act == "sqrelu":
            at = jnp.square(jnp.maximum(at, 0.0))
        part = _dg(at, b_ref[...], ca, cb)

        @pl.when(kstep == 0)
        def _():
            o_ref[...] = part

        @pl.when(kstep > 0)
        def _():
            o_ref[...] += part

        if epi is not None:
            @pl.when(kstep == nk - 1)
            def _():
                if epi == "dsqrelu":
                    o_ref[...] = o_ref[...] * (2.0 * jnp.maximum(e_ref[...], 0.0))
                else:
                    o_ref[...] = o_ref[...] + alpha * e_ref[...]

    a_spec = pl.BlockSpec((tk, tm), lambda i, j, k: (k, i)) if mode == "tn" else pl.BlockSpec((tm, tk), lambda i, j, k: (i, k))
    b_spec = pl.BlockSpec((tn_, tk), lambda i, j, k: (j, k)) if mode == "nt" else pl.BlockSpec((tk, tn_), lambda i, j, k: (k, j))
    o_spec = pl.BlockSpec((tm, tn_), lambda i, j, k: (i, j))
    ins, specs = [a, b], [a_spec, b_spec]
    if epi is not None:
        ins.append(extra)
        specs.append(o_spec)
    return pl.pallas_call(
        body, name=name, grid=(m // tm, nn_ // tn_, nk), in_specs=specs, out_specs=o_spec,
        out_shape=jax.ShapeDtypeStruct((m, nn_), F32),
        compiler_params=_cparams(("parallel", "parallel", "arbitrary")),
    )(*ins)


ROWS = 256


def _ln_core(x, m, g, b):
    r = ALPHA * x + m
    mu = jnp.mean(r, axis=-1, keepdims=True)
    xc = r - mu
    var = jnp.mean(xc * xc, axis=-1, keepdims=True)
    rstd = lax.rsqrt(var + LN_EPS)
    xhat = xc * rstd
    return xhat * g + b, xhat, rstd


def ln_fwd(x, m, g, b):
    s, dm = x.shape

    def body(x_ref, m_ref, g_ref, b_ref, o_ref):
        o_ref[...] = _ln_core(x_ref[...], m_ref[...], g_ref[...], b_ref[...])[0]

    row = pl.BlockSpec((ROWS, dm), lambda i: (i, 0))
    vec = pl.BlockSpec((1, dm), lambda i: (0, 0))
    return pl.pallas_call(body, name="ln_fwd", grid=(s // ROWS,), in_specs=[row, row, vec, vec], out_specs=row,
                          out_shape=jax.ShapeDtypeStruct((s, dm), F32), compiler_params=_cparams(("parallel",)))(x, m, g, b)


def ln_bwd(x, m, g, dy):
    s, dm = x.shape

    def body(x_ref, m_ref, g_ref, dy_ref, dr_ref, dg_ref, db_ref):
        gv = g_ref[...]
        _, xhat, rstd = _ln_core(x_ref[...], m_ref[...], gv, jnp.zeros_like(gv))
        dy = dy_ref[...]
        dxh = dy * gv
        dr_ref[...] = rstd * (dxh - jnp.mean(dxh, axis=-1, keepdims=True)
                              - xhat * jnp.mean(dxh * xhat, axis=-1, keepdims=True))

        @pl.when(pl.program_id(0) == 0)
        def _():
            dg_ref[...] = jnp.zeros_like(dg_ref)
            db_ref[...] = jnp.zeros_like(db_ref)

        dg_ref[...] += jnp.sum(dy * xhat, axis=0, keepdims=True)
        db_ref[...] += jnp.sum(dy, axis=0, keepdims=True)

    row = pl.BlockSpec((ROWS, dm), lambda i: (i, 0))
    vec = pl.BlockSpec((1, dm), lambda i: (0, 0))
    return pl.pallas_call(body, name="ln_bwd", grid=(s // ROWS,), in_specs=[row, row, vec, row], out_specs=[row, vec, vec],
                          out_shape=[jax.ShapeDtypeStruct((s, dm), F32), jax.ShapeDtypeStruct((1, dm), F32),
                                     jax.ShapeDtypeStruct((1, dm), F32)],
                          compiler_params=_cparams(("arbitrary",)))(x, m, g, dy)


def loss_head(y, target):
    s, dm = y.shape

    def body(y_ref, t_ref, dy_ref, l_ref):
        e = y_ref[...] - t_ref[...]
        dy_ref[...] = e * (1.0 / dm)

        @pl.when(pl.program_id(0) == 0)
        def _():
            l_ref[...] = jnp.zeros_like(l_ref)

        col = jnp.sum(e * e, axis=0, keepdims=True) * (0.5 / dm)
        acc = col[:, :LANES]
        for c in range(1, dm // LANES):
            acc = acc + col[:, c * LANES:(c + 1) * LANES]
        l_ref[...] += acc

    row = pl.BlockSpec((ROWS, dm), lambda i: (i, 0))
    return pl.pallas_call(body, name="loss_head", grid=(s // ROWS,), in_specs=[row, row],
                          out_specs=[row, pl.BlockSpec((1, LANES), lambda i: (0, 0))],
                          out_shape=[jax.ShapeDtypeStruct((s, dm), F32), jax.ShapeDtypeStruct((1, LANES), F32)],
                          compiler_params=_cparams(("arbitrary",)))(y, target)


def _shift_rows_impl(x, d):
    n = x.shape[0]
    if d == 0:
        return x
    t = lax.broadcasted_iota(jnp.int32, x.shape, 0)
    return jnp.where((t + d >= 0) & (t + d < n), pltpu.roll(x, (-d) % n, 0), 0.0)


@functools.partial(jax.custom_vjp, nondiff_argnums=(1,))
def _shift_rows(x, d):
    return _shift_rows_impl(x, d)


_shift_rows.defvjp(lambda x, d: (_shift_rows_impl(x, d), None), lambda d, _, g: (_shift_rows_impl(g, -d),))


def _gdn_pre_fn(u, w, kind):
    rows = lax.broadcasted_iota(jnp.int32, w.shape, 0)
    c = None
    for tap in range(A_CONV):
        w_tap = jnp.sum(jnp.where(rows == tap, w, 0.0), axis=0, keepdims=True)
        term = _shift_rows(u, tap - A_CONV // 2) * w_tap
        c = term if c is None else c + term
    y = c * _sigmoid(c)
    if kind == "v":
        return y
    y = y * lax.rsqrt(jnp.sum(y * y, axis=-1, keepdims=True) + L2_EPS)
    return y * (A_DK ** -0.5) if kind == "q" else y


_KIND_OFF = {"q": 0, "k": A_HEADS, "v": 2 * A_HEADS}


def gdn_pre(proj, conv_w, kind):
    s = proj.shape[0]
    off = _KIND_OFF[kind]

    def body(u_ref, w_ref, o_ref):
        o_ref[...] = _gdn_pre_fn(u_ref[...], w_ref[...], kind)

    return pl.pallas_call(
        body, name="gdn_pre_" + kind, grid=(A_HEADS,),
        in_specs=[pl.BlockSpec((s, LANES), lambda h: (0, off + h)), pl.BlockSpec((A_CONV, LANES), lambda h: (0, off + h))],
        out_specs=pl.BlockSpec((s, LANES), lambda h: (0, h)),
        out_shape=jax.ShapeDtypeStruct((s, A_W), F32), compiler_params=_cparams(("parallel",)))(proj, conv_w)


def gdn_pre_bwd(proj, conv_w, dt2, kind):
    s = proj.shape[0]
    off = _KIND_OFF[kind]

    def body(u_ref, w_ref, d0_ref, d1_ref, du_ref, dw_ref):
        _, vjp = jax.vjp(functools.partial(_gdn_pre_fn, kind=kind), u_ref[...], w_ref[...])
        du, dw = vjp(d0_ref[...] + d1_ref[...])
        du_ref[...] = du
        dw_ref[...] = dw

    return pl.pallas_call(
        body, name="gdn_pre_bwd_" + kind, grid=(A_HEADS,),
        in_specs=[pl.BlockSpec((s, LANES), lambda h: (0, off + h)), pl.BlockSpec((A_CONV, LANES), lambda h: (0, off + h)),
                  pl.BlockSpec((None, s, LANES), lambda h: (0, 0, h)), pl.BlockSpec((None, s, LANES), lambda h: (1, 0, h))],
        out_specs=[pl.BlockSpec((s, LANES), lambda h: (0, h)), pl.BlockSpec((A_CONV, LANES), lambda h: (0, h))],
        out_shape=[jax.ShapeDtypeStruct((s, A_W), F32), jax.ShapeDtypeStruct((A_CONV, A_W), F32)],
        compiler_params=_cparams(("parallel",)))(proj, conv_w, dt2, dt2)


N_GATE = 2 * A_HEADS


def _gdn_gates_fn(ba, alog_row, dt_row):
    r = lax.broadcasted_iota(jnp.int32, (LANES, N_GATE * LANES), 0)
    c = lax.broadcasted_iota(jnp.int32, (LANES, N_GATE * LANES), 1) >> 7
    beta_b = xdot(_sigmoid(ba), (r == c).astype(F32))
    g = -(jnp.exp(alog_row) * _softplus(ba + dt_row))
    g_b = xdot(g, (r == c + N_GATE).astype(F32))
    return beta_b, g_b


def gdn_gates(ba, alog_row, dt_row):
    s = ba.shape[0]

    def body(ba_ref, al_ref, dt_ref, bb_ref, gb_ref):
        bb_ref[...], gb_ref[...] = _gdn_gates_fn(ba_ref[...], al_ref[...], dt_ref[...])

    row = pl.BlockSpec((ROWS, LANES), lambda i: (i, 0))
    vec = pl.BlockSpec((1, LANES), lambda i: (0, 0))
    wide = pl.BlockSpec((ROWS, N_GATE * LANES), lambda i: (i, 0))
    return pl.pallas_call(body, name="gdn_gates", grid=(s // ROWS,), in_specs=[row, vec, vec], out_specs=[wide, wide],
                          out_shape=[jax.ShapeDtypeStruct((s, N_GATE * LANES), F32)] * 2,
                          compiler_params=_cparams(("parallel",)))(ba, alog_row, dt_row)


def gdn_gates_bwd(ba, alog_row, dt_row, dbeta_b, dg_b):
    s = ba.shape[0]

    def body(ba_ref, al_ref, dt_ref, dbb_ref, dgb_ref, dba_ref, dal_ref, ddt_ref):
        _, vjp = jax.vjp(_gdn_gates_fn, ba_ref[...], al_ref[...], dt_ref[...])
        dba, dal, ddt = vjp((dbb_ref[...], dgb_ref[...]))
        dba_ref[...] = dba

        @pl.when(pl.program_id(0) == 0)
        def _():
            dal_ref[...] = jnp.zeros_like(dal_ref)
            ddt_ref[...] = jnp.zeros_like(ddt_ref)

        dal_ref[...] += dal
        ddt_ref[...] += ddt

    row = pl.BlockSpec((ROWS, LANES), lambda i: (i, 0))
    vec = pl.BlockSpec((1, LANES), lambda i: (0, 0))
    wide = pl.BlockSpec((ROWS, N_GATE * LANES), lambda i: (i, 0))
    return pl.pallas_call(body, name="gdn_gates_bwd", grid=(s // ROWS,), in_specs=[row, vec, vec, wide, wide],
                          out_specs=[row, vec, vec],
                          out_shape=[jax.ShapeDtypeStruct((s, LANES), F32), jax.ShapeDtypeStruct((1, LANES), F32),
                                     jax.ShapeDtypeStruct((1, LANES), F32)],
                          compiler_params=_cparams(("arbitrary",)))(ba, alog_row, dt_row, dbeta_b, dg_b)


def _post_fn(o, z, g):
    y = o * lax.rsqrt(jnp.mean(o * o, axis=-1, keepdims=True) + RMS_EPS) * g
    return y * (z * _sigmoid(z))


def mixer_post(o2, proj, norm_g, width, gate_off, name):
    s = o2.shape[1]
    nh = o2.shape[2] // width

    def body(o0_ref, o1_ref, z_ref, g_ref, y_ref):
        y_ref[...] = _post_fn(o0_ref[...] + o1_ref[...], z_ref[...], g_ref[...])

    ospec = lambda d: pl.BlockSpec((None, ROWS, width), lambda i, h: (d, i, h))
    return pl.pallas_call(
        body, name=name, grid=(s // ROWS, nh),
        in_specs=[ospec(0), ospec(1), pl.BlockSpec((ROWS, width), lambda i, h: (i, gate_off + h)),
                  pl.BlockSpec((1, width), lambda i, h: (0, 0))],
        out_specs=pl.BlockSpec((ROWS, width), lambda i, h: (i, h)),
        out_shape=jax.ShapeDtypeStruct((s, o2.shape[2]), F32),
        compiler_params=_cparams(("parallel", "parallel")))(o2, o2, proj, norm_g)


def mixer_post_bwd(o2, proj, norm_g, dy, width, gate_off, name):
    s = o2.shape[1]
    nh = o2.shape[2] // width

    def body(o0_ref, o1_ref, z_ref, g_ref, dy_ref, do_ref, dz_ref, dg_ref):
        _, vjp = jax.vjp(_post_fn, o0_ref[...] + o1_ref[...], z_ref[...], g_ref[...])
        do, dz, dg = vjp(dy_ref[...])
        do_ref[...] = do
        dz_ref[...] = dz

        @pl.when((pl.program_id(0) == 0) & (pl.program_id(1) == 0))
        def _():
            dg_ref[...] = jnp.zeros_like(dg_ref)

        dg_ref[...] += dg

    ospec = lambda d: pl.BlockSpec((None, ROWS, width), lambda i, h: (d, i, h))
    blk = pl.BlockSpec((ROWS, width), lambda i, h: (i, h))
    vec = pl.BlockSpec((1, width), lambda i, h: (0, 0))
    return pl.pallas_call(
        body, name=name, grid=(s // ROWS, nh),
        in_specs=[ospec(0), ospec(1), pl.BlockSpec((ROWS, width), lambda i, h: (i, gate_off + h)), vec, blk],
        out_specs=[blk, blk, vec],
        out_shape=[jax.ShapeDtypeStruct((s, o2.shape[2]), F32)] * 2 + [jax.ShapeDtypeStruct((1, width), F32)],
        compiler_params=_cparams(("arbitrary", "arbitrary")))(o2, o2, proj, norm_g, dy)


def _log_gate(z):
    return (jnp.minimum(z, 0.0) - jnp.log(1.0 + jnp.exp(-jnp.abs(z)))) * (1.0 / B_TAU)


def gla_gate(gl, w2, gb):
    s = gl.shape[0]

    def body(gl_ref, w_ref, b_ref, o_ref):
        for n in range(2):
            o_ref[n] = _log_gate(nn(gl_ref[...], w_ref[n]) + b_ref[n])

    full = lambda shp: pl.BlockSpec(shp, lambda i: (0,) * len(shp))
    return pl.pallas_call(
        body, name="gla_gate", grid=(s // ROWS,),
        in_specs=[pl.BlockSpec((ROWS, LANES), lambda i: (i, 0)), full(w2.shape), full(gb.shape)],
        out_specs=pl.BlockSpec((2, ROWS, B_KW), lambda i: (0, i, 0)),
        out_shape=jax.ShapeDtypeStruct((2, s, B_KW), F32), compiler_params=_cparams(("parallel",)))(gl, w2, gb)


def gla_gate_bwd(gl, w2, gb, dla):
    s = gl.shape[0]

    def body(gl_ref, w_ref, b_ref, dla_ref, dgl_ref, dz_ref, db0_ref, db1_ref):
        @pl.when(pl.program_id(0) == 0)
        def _():
            db0_ref[...] = jnp.zeros_like(db0_ref)
            db1_ref[...] = jnp.zeros_like(db1_ref)

        dgl = None
        for n, db_ref in enumerate((db0_ref, db1_ref)):
            _, vjp = jax.vjp(_log_gate, nn(gl_ref[...], w_ref[n]) + b_ref[n])
            dz, = vjp(dla_ref[n])
            dz_ref[n] = dz
            db_ref[...] += jnp.sum(dz, axis=0, keepdims=True)
            part = nt(dz, w_ref[n])
            dgl = part if dgl is None else dgl + part
        dgl_ref[...] = dgl

    full = lambda shp: pl.BlockSpec(shp, lambda i: (0,) * len(shp))
    row = pl.BlockSpec((ROWS, LANES), lambda i: (i, 0))
    wide = pl.BlockSpec((2, ROWS, B_KW), lambda i: (0, i, 0))
    vec = pl.BlockSpec((1, B_KW), lambda i: (0, 0))
    return pl.pallas_call(
        body, name="gla_gate_bwd", grid=(s // ROWS,),
        in_specs=[row, full(w2.shape), full(gb.shape), wide],
        out_specs=[row, wide, vec, vec],
        out_shape=[jax.ShapeDtypeStruct((s, LANES), F32), jax.ShapeDtypeStruct((2, s, B_KW), F32),
                   jax.ShapeDtypeStruct((1, B_KW), F32), jax.ShapeDtypeStruct((1, B_KW), F32)],
        compiler_params=_cparams(("arbitrary",)))(gl, w2, gb, dla)


PACK_TILE = 512


def cast_bf16(x):
    r, c = x.shape

    def body(x_ref, o_ref):
        o_ref[...] = x_ref[...].astype(BF16)

    blk = pl.BlockSpec((PACK_TILE, c), lambda i: (i, 0))
    return pl.pallas_call(body, name="cast_bf16", grid=(r // PACK_TILE,), in_specs=[blk], out_specs=blk,
                          out_shape=jax.ShapeDtypeStruct((r, c), BF16), compiler_params=_cparams(("parallel",)))(x)


def sum_slots(x, name):
    n, r, c = x.shape
    tr = _tile(r, PACK_TILE)

    def body(x_ref, o_ref):
        acc = x_ref[0].astype(F32)
        for k in range(1, n):
            acc = acc + x_ref[k].astype(F32)
        o_ref[...] = acc

    return pl.pallas_call(body, name=name, grid=(r // tr,), in_specs=[pl.BlockSpec((n, tr, c), lambda i: (0, i, 0))],
                          out_specs=pl.BlockSpec((tr, c), lambda i: (i, 0)),
                          out_shape=jax.ShapeDtypeStruct((r, c), F32), compiler_params=_cparams(("parallel",)))(x)


def add_sibling_half(gpack, theirs, core):
    n, r, c = gpack.shape
    half_rows = r // 2
    tr = _tile(half_rows, PACK_TILE)
    nblk = half_rows // tr

    def body(core_ref, g_ref, t_ref, o_ref):
        o_ref[...] = (g_ref[...] + t_ref[...]).astype(BF16)

    blk = pl.BlockSpec((None, tr, c), lambda s, i, core_ref: (s, i, 0))
    return pl.pallas_call(
        body, name="add_sibling_half",
        grid_spec=pltpu.PrefetchScalarGridSpec(
            num_scalar_prefetch=1, grid=(n, nblk),
            in_specs=[pl.BlockSpec((None, tr, c), lambda s, i, core_ref: (s, core_ref[0] * nblk + i, 0)), blk],
            out_specs=blk),
        out_shape=jax.ShapeDtypeStruct((n, half_rows, c), BF16),
        compiler_params=_cparams(("parallel", "parallel")))(core, gpack, theirs)


def adamw(w, m, v, grads, g_row_off, name):
    r, c = w.shape
    tr = next(t for t in (PACK_TILE, r) if r % t == 0 and g_row_off % t == 0)
    ob = g_row_off // tr
    ng = len(grads)

    def body(*refs):
        w_ref, m_ref, v_ref = refs[:3]
        g_refs = refs[3:3 + ng]
        g_ref, d_ref, nm_ref, nv_ref = refs[3 + ng:]
        g = g_refs[0][...]
        for gr in g_refs[1:]:
            g = g + gr[...]
        m_new = ADAM_B1 * m_ref[...] + (1.0 - ADAM_B1) * g
        v_new = ADAM_B2 * v_ref[...] + (1.0 - ADAM_B2) * jnp.square(g)
        m_hat = m_new / (1.0 - ADAM_B1 ** ADAM_STEP)
        v_hat = v_new / (1.0 - ADAM_B2 ** ADAM_STEP)
        g_ref[...] = g
        d_ref[...] = -ADAM_LR * (m_hat / (jnp.sqrt(v_hat) + ADAM_EPS) + ADAM_WD * w_ref[...])
        nm_ref[...] = m_new
        nv_ref[...] = v_new

    blk = pl.BlockSpec((tr, c), lambda i: (i, 0))
    gblk = pl.BlockSpec((tr, c), lambda i: (i + ob, 0))
    return pl.pallas_call(body, name=name, grid=(r // tr,), in_specs=[blk, blk, blk] + [gblk] * ng, out_specs=[blk] * 4,
                          out_shape=[jax.ShapeDtypeStruct((r, c), F32)] * 4,
                          compiler_params=_cparams(("parallel",)))(w, m, v, *grads)


MESH = pl.DeviceIdType.MESH
HBM = pl.BlockSpec(memory_space=pl.ANY)
CHIP_FLIPS = ((1, 0), (0, 1), (1, 1))


def _place():
    return lax.axis_index("x"), lax.axis_index("y"), lax.axis_index("c")


def allgather_chips(pack):
    r, c = pack.shape
    half_rows = r // 2

    def body(src_ref, out_ref, send_sems, recv_sems, local_sem):
        x, y, cc = _place()
        half = pl.ds(cc * half_rows, half_rows)
        other = pl.ds((1 - cc) * half_rows, half_rows)
        mine = pltpu.make_async_copy(src_ref, out_ref.at[2 * x + y], local_sem)
        mine.start()

        def copy(k, src, dst, to):
            return pltpu.make_async_remote_copy(src_ref=src, dst_ref=dst, send_sem=send_sems.at[k],
                                                recv_sem=recv_sems.at[k], device_id=to, device_id_type=MESH)

        chips = [((1 - x if fx else x), (1 - y if fy else y)) for fx, fy in CHIP_FLIPS]
        first = [copy(k, src_ref.at[half], out_ref.at[2 * x + y, half], (px, py, cc)) for k, (px, py) in enumerate(chips)]
        for cp in first:
            cp.start()
        passed = []
        for k, (px, py) in enumerate(chips):
            landed = out_ref.at[2 * px + py, half]
            copy(k, landed, landed, (px, py, cc)).wait_recv()
            passed.append(copy(3 + k, landed, landed, (x, y, 1 - cc)))
            passed[-1].start()
        for k, (px, py) in enumerate(chips):
            theirs = out_ref.at[2 * px + py, other]
            copy(3 + k, theirs, theirs, (x, y, 1 - cc)).wait_recv()
        for cp in first + passed:
            cp.wait_send()
        mine.wait()

    return pl.pallas_call(
        body, name="allgather_chips", in_specs=[HBM], out_specs=HBM,
        out_shape=jax.ShapeDtypeStruct((4, r, c), pack.dtype),
        scratch_shapes=[pltpu.SemaphoreType.DMA((6,)), pltpu.SemaphoreType.DMA((6,)), pltpu.SemaphoreType.DMA],
    )(pack)


def scatter_chips(gpack):
    def body(src_ref, out_ref, send_sems, recv_sems, local_sem):
        x, y, cc = _place()
        mine = pltpu.make_async_copy(src_ref.at[2 * x + y], out_ref.at[3], local_sem)
        mine.start()
        sends = []
        for k, (fx, fy) in enumerate(CHIP_FLIPS):
            px, py = (1 - x if fx else x), (1 - y if fy else y)
            sends.append(pltpu.make_async_remote_copy(
                src_ref=src_ref.at[2 * px + py], dst_ref=out_ref.at[k], send_sem=send_sems.at[k], recv_sem=recv_sems.at[k],
                device_id=(px, py, cc), device_id_type=MESH))
        for cp in sends:
            cp.start()
        for cp in sends:
            cp.wait_recv()
        for cp in sends:
            cp.wait_send()
        mine.wait()

    return pl.pallas_call(
        body, name="scatter_chips", in_specs=[HBM], out_specs=HBM,
        out_shape=jax.ShapeDtypeStruct(gpack.shape, gpack.dtype),
        scratch_shapes=[pltpu.SemaphoreType.DMA((3,)), pltpu.SemaphoreType.DMA((3,)), pltpu.SemaphoreType.DMA],
    )(gpack)


def sibling_halves(gpack):
    n, r, c = gpack.shape
    half_rows = r // 2

    def body(src_ref, out_ref, send_sem, recv_sem):
        x, y, cc = _place()
        cp = pltpu.make_async_remote_copy(
            src_ref=src_ref.at[:, pl.ds((1 - cc) * half_rows, half_rows)], dst_ref=out_ref, send_sem=send_sem,
            recv_sem=recv_sem, device_id=(x, y, 1 - cc), device_id_type=MESH)
        cp.start()
        cp.wait()

    return pl.pallas_call(
        body, name="sibling_halves", in_specs=[HBM], out_specs=HBM,
        out_shape=jax.ShapeDtypeStruct((n, half_rows, c), gpack.dtype),
        scratch_shapes=[pltpu.SemaphoreType.DMA, pltpu.SemaphoreType.DMA],
    )(gpack)


def join_halves(mine):
    half_rows, c = mine.shape

    def body(src_ref, out_ref, send_sem, recv_sem, local_sem):
        x, y, cc = _place()
        half = pl.ds(cc * half_rows, half_rows)
        other = pl.ds((1 - cc) * half_rows, half_rows)
        keep = pltpu.make_async_copy(src_ref, out_ref.at[half], local_sem)
        keep.start()
        send = pltpu.make_async_remote_copy(src_ref=src_ref, dst_ref=out_ref.at[half], send_sem=send_sem,
                                            recv_sem=recv_sem, device_id=(x, y, 1 - cc), device_id_type=MESH)
        send.start()
        pltpu.make_async_remote_copy(src_ref=src_ref, dst_ref=out_ref.at[other], send_sem=send_sem, recv_sem=recv_sem,
                                     device_id=(x, y, 1 - cc), device_id_type=MESH).wait_recv()
        send.wait_send()
        keep.wait()

    return pl.pallas_call(
        body, name="join_halves", in_specs=[HBM], out_specs=HBM,
        out_shape=jax.ShapeDtypeStruct((2 * half_rows, c), mine.dtype),
        scratch_shapes=[pltpu.SemaphoreType.DMA, pltpu.SemaphoreType.DMA, pltpu.SemaphoreType.DMA],
    )(mine)


def exchange_all(v, name):
    r, c = v.shape

    def body(v_ref, out_ref, send_sems, recv_sems):
        x, y, cc = _place()
        out_ref[4 * x + 2 * y + cc] = v_ref[...]
        sends, recvs = [], []
        for k in range(1, 8):
            px = 1 - x if k & 4 else x
            py = 1 - y if k & 2 else y
            pc = 1 - cc if k & 1 else cc
            sends.append(pltpu.make_async_remote_copy(
                src_ref=v_ref, dst_ref=out_ref.at[4 * x + 2 * y + cc], send_sem=send_sems.at[k - 1],
                recv_sem=recv_sems.at[k - 1], device_id=(px, py, pc), device_id_type=MESH))
            recvs.append(pltpu.make_async_remote_copy(
                src_ref=v_ref, dst_ref=out_ref.at[4 * px + 2 * py + pc], send_sem=send_sems.at[k - 1],
                recv_sem=recv_sems.at[k - 1], device_id=(px, py, pc), device_id_type=MESH))
        for cp in sends:
            cp.start()
        for cp in recvs:
            cp.wait_recv()
        for cp in sends:
            cp.wait_send()

    vm = pl.BlockSpec(memory_space=pltpu.VMEM)
    return pl.pallas_call(
        body, name=name, in_specs=[vm], out_specs=vm, out_shape=jax.ShapeDtypeStruct((8, r, c), v.dtype),
        scratch_shapes=[pltpu.SemaphoreType.DMA((7,)), pltpu.SemaphoreType.DMA((7,))],
        compiler_params=pltpu.CompilerParams(vmem_limit_bytes=VMEM_LIMIT),
    )(v)


def _pack_rows(arrays, rows, width):
    flat = jnp.concatenate([a.reshape(-1) for a in arrays])
    return jnp.pad(flat, (0, rows * width - flat.shape[0])).reshape(rows, width)


def _unpack_rows(pack, shapes):
    flat = pack.reshape(-1)
    out, off = [], 0
    for shp in shapes:
        n = math.prod(shp)
        out.append(flat[off:off + n].reshape(shp))
        off += n
    return out


def _rows_for(shapes, width, mult=8):
    n = sum(math.prod(s) for s in shapes)
    return -(-n // (width * mult)) * mult


def _gdn_fwd(x, p):
    proj = mm(x, p["w_main"], name="gdn_proj")
    ba = mm(x, p["w_gate"], name="gdn_proj_gate")
    q, k, v = (gdn_pre(proj, p["conv"], kind) for kind in "qkv")
    beta_b, g_b = gdn_gates(ba, p["alog_row"], p["dt_row"])
    o2, st = gdn_rec_fwd(q, k, v, beta_b, g_b)
    y = mixer_post(o2, proj, p["norm_g"], A_DK, 3 * A_HEADS, "gdn_post")
    m = mm(y, p["w_out"], name="gdn_out")
    return m, (x, proj, ba, q, k, v, beta_b, g_b, o2, st, y)


def _gdn_bwd(saved, p, dm):
    x, proj, ba, q, k, v, beta_b, g_b, o2, st, y = saved
    d_w_out = mm(y, dm, "tn", name="gdn_dw_out")
    dy = mm(dm, p["w_out"], "nt", name="gdn_dy")
    do, dz, d_norm_g = mixer_post_bwd(o2, proj, p["norm_g"], dy, A_DK, 3 * A_HEADS, "gdn_post_bwd")
    dq2, dk2, dv2, dbb, dgb = gdn_rec_bwd(q, k, v, beta_b, g_b, st, do)
    dba, d_alog_row, d_dt_row = gdn_gates_bwd(ba, p["alog_row"], p["dt_row"], dbb, dgb)
    du, dconv = zip(*(gdn_pre_bwd(proj, p["conv"], d2, kind) for d2, kind in ((dq2, "q"), (dk2, "k"), (dv2, "v"))))
    dproj = jnp.concatenate(list(du) + [dz], axis=1)
    d_w_main = mm(x, dproj, "tn", name="gdn_dw_main")
    d_w_gate = mm(x, dba, "tn", name="gdn_dw_gate")
    dx = mm(dba, p["w_gate"], "nt", epi="add", extra=dm, alpha=ALPHA, name="gdn_dx_gate")
    dx = mm(dproj, p["w_main"], "nt", epi="add", extra=dx, name="gdn_dx")
    grads = dict(w_in=jnp.concatenate([d_w_main, d_w_gate[:, :2 * N_GATE]], axis=1), conv=jnp.concatenate(dconv, axis=1),
                 alog=d_alog_row[0, N_GATE:2 * N_GATE].reshape(2, A_HEADS), dt=d_dt_row[0, N_GATE:2 * N_GATE].reshape(2, A_HEADS),
                 norm_g=d_norm_g[0], w_out=d_w_out)
    return dx, grads


def _gla_fwd(x, p):
    proj = mm(x, p["w_main"], name="gla_proj")
    gl = mm(x, p["w_gate"], name="gla_proj_gate")
    log_a = gla_gate(gl, p["w2"], p["gate_b"])
    o2, st = gla_rec_fwd(proj, log_a)
    y = mixer_post(o2, proj, p["norm_g"], B_DV, (2 * B_KW + B_VW) // B_DV, "gla_post")
    m = mm(y, p["w_out"], name="gla_out")
    return m, (x, proj, gl, log_a, o2, st, y)


def _gla_bwd(saved, p, dm):
    x, proj, gl, log_a, o2, st, y = saved
    d_w_out = mm(y, dm, "tn", name="gla_dw_out")
    dy = mm(dm, p["w_out"], "nt", name="gla_dy")
    do, dr, d_norm_g = mixer_post_bwd(o2, proj, p["norm_g"], dy, B_DV, (2 * B_KW + B_VW) // B_DV, "gla_post_bwd")
    dq2, dk2, dv2, dla = gla_rec_bwd(proj, log_a, st, do)
    dgl, dz, d_b0, d_b1 = gla_gate_bwd(gl, p["w2"], p["gate_b"], dla)
    d_w2 = [mm(gl, dz[n], "tn", name="gla_dw_gate_w2") for n in range(2)]
    dproj = jnp.concatenate([dq2[0] + dq2[1], dk2[0] + dk2[1], dv2[0] + dv2[1], dr], axis=1)
    d_w_main = mm(x, dproj, "tn", name="gla_dw_main")
    d_w_gate = mm(x, dgl, "tn", name="gla_dw_gate")
    dx = mm(dgl, p["w_gate"], "nt", epi="add", extra=dm, alpha=ALPHA, name="gla_dx_gate")
    dx = mm(dproj, p["w_main"], "nt", epi="add", extra=dx, name="gla_dx")
    grads = dict(w_in=jnp.concatenate([d_w_main, d_w_gate[:, :2 * B_RANK]], axis=1),
                 gate_w2=jnp.stack([d_w2[n][n * B_RANK:(n + 1) * B_RANK] for n in range(2)]),
                 gate_b=jnp.concatenate([d_b0, d_b1]), norm_g=d_norm_g[0], w_out=d_w_out)
    return dx, grads


def _pad_cols(w, width=LANES):
    return jnp.pad(w, ((0, 0), (0, width - w.shape[1])))


def _local_step(x, target, a_w_in, a_conv, a_alog, a_dt_bias, a_norm_g, a_w_out, b_w_in, b_gate_w2, b_gate_b, b_norm_g,
                b_w_out, ln1_g, ln1_b, mlp_w1, mlp_w2, ln2_g, ln2_b):
    layer_p = []
    for i in range(DEPTH):
        j = i // 2
        if i % 2 == 0:
            layer_p.append(dict(
                w_main=a_w_in[j][:, :4 * A_W], w_gate=_pad_cols(a_w_in[j][:, 4 * A_W:]), conv=a_conv[j],
                alog_row=jnp.pad(a_alog[j].reshape(1, N_GATE), ((0, 0), (N_GATE, LANES - 2 * N_GATE))),
                dt_row=jnp.pad(a_dt_bias[j].reshape(1, N_GATE), ((0, 0), (N_GATE, LANES - 2 * N_GATE))),
                norm_g=a_norm_g[j].reshape(1, A_DK), w_out=a_w_out[j]))
        else:
            w2 = jnp.stack([jnp.pad(b_gate_w2[j][n], ((n * B_RANK, LANES - (n + 1) * B_RANK), (0, 0))) for n in range(2)])
            layer_p.append(dict(
                w_main=b_w_in[j][:, :2 * B_KW + 2 * B_VW], w_gate=_pad_cols(b_w_in[j][:, 2 * B_KW + 2 * B_VW:]),
                w2=w2, gate_b=b_gate_b[j].reshape(2, 1, B_KW), norm_g=b_norm_g[j].reshape(1, B_DV), w_out=b_w_out[j]))

    saved = []
    h = x
    for i in range(DEPTH):
        p = layer_p[i]
        m, sv = (_gdn_fwd if i % 2 == 0 else _gla_fwd)(h, p)
        x1 = ln_fwd(h, m, ln1_g[i:i + 1], ln1_b[i:i + 1])
        h1 = mm(x1, mlp_w1[i], name="mlp_up")
        mlp = mm(h1, mlp_w2[i], act="sqrelu", name="mlp_down")
        x2 = ln_fwd(x1, mlp, ln2_g[i:i + 1], ln2_b[i:i + 1])
        saved.append((sv, h, m, x1, h1, mlp))
        h = x2

    dh, loss_part = loss_head(h, target)

    g_a, g_b, g_ln1g, g_ln1b, g_ln2g, g_ln2b, g_w1, g_w2 = {}, {}, {}, {}, {}, {}, {}, {}
    for i in reversed(range(DEPTH)):
        sv, xin, m, x1, h1, mlp = saved[i]
        p = layer_p[i]
        dr2, g_ln2g[i], g_ln2b[i] = ln_bwd(x1, mlp, ln2_g[i:i + 1], dh)
        g_w2[i] = mm(h1, dr2, "tn", act="sqrelu", name="mlp_dw_down")
        dh1 = mm(dr2, mlp_w2[i], "nt", epi="dsqrelu", extra=h1, name="mlp_dh")
        g_w1[i] = mm(x1, dh1, "tn", name="mlp_dw_up")
        dx1 = mm(dh1, mlp_w1[i], "nt", epi="add", extra=dr2, alpha=ALPHA, name="mlp_dx")
        dr1, g_ln1g[i], g_ln1b[i] = ln_bwd(xin, m, ln1_g[i:i + 1], dx1)
        dh, g = (_gdn_bwd if i % 2 == 0 else _gla_bwd)(sv, p, dr1)
        (g_a if i % 2 == 0 else g_b)[i // 2] = g

    st = lambda d, key=None: jnp.stack([(d[i] if key is None else d[i][key]) for i in sorted(d)])
    grads = dict(
        a_w_in=st(g_a, "w_in"), a_conv=st(g_a, "conv"), a_alog=st(g_a, "alog"), a_dt_bias=st(g_a, "dt"),
        a_norm_g=st(g_a, "norm_g"), a_w_out=st(g_a, "w_out"), b_w_in=st(g_b, "w_in"), b_gate_w2=st(g_b, "gate_w2"),
        b_gate_b=st(g_b, "gate_b"), b_norm_g=st(g_b, "norm_g"), b_w_out=st(g_b, "w_out"),
        ln1_g=st(g_ln1g)[:, 0], ln1_b=st(g_ln1b)[:, 0], mlp_w1=st(g_w1), mlp_w2=st(g_w2),
        ln2_g=st(g_ln2g)[:, 0], ln2_b=st(g_ln2b)[:, 0])
    return loss_part, dh, grads


WEIGHTS = ("a_w_in", "a_conv", "a_alog", "a_dt_bias", "a_norm_g", "a_w_out", "b_w_in", "b_gate_w2", "b_gate_b",
           "b_norm_g", "b_w_out", "ln1_g", "ln1_b", "mlp_w1", "mlp_w2", "ln2_g", "ln2_b")
BIG = ("mlp_w1", "mlp_w2", "a_w_out", "b_w_out", "a_w_in", "b_w_in")
SHARD_AXIS = {"mlp_w1": 2, "mlp_w2": 1, "a_w_out": 1, "b_w_out": 1, "a_w_in": 2, "b_w_in": 2}
SMALL = tuple(n for n in WEIGHTS if n not in BIG)
SMALL_SHARD_AXIS = {"a_conv": 2, "b_gate_w2": 3, "b_gate_b": 2, "b_norm_g": 1}


def _to_chip_major(full, axis):
    shp = full.shape
    t = full.reshape(shp[:axis] + (4, shp[axis] // 4) + shp[axis + 1:])
    return jnp.moveaxis(t, axis, 0)


def _from_chip_major(stacked, axis):
    t = jnp.moveaxis(stacked, 0, axis)
    shp = t.shape
    return t.reshape(shp[:axis] + (shp[axis] * shp[axis + 1],) + shp[axis + 2:])


def kernel(x, a_w_in, a_conv, a_alog, a_dt_bias, a_norm_g, a_w_out, b_w_in, b_gate_w2, b_gate_b, b_norm_g, b_w_out, ln1_g, ln1_b, mlp_w1, mlp_w2, ln2_g, ln2_b, loss_target, m_a_w_in, m_a_conv, m_a_alog, m_a_dt_bias, m_a_norm_g, m_a_w_out, m_b_w_in, m_b_gate_w2, m_b_gate_b, m_b_norm_g, m_b_w_out, m_ln1_g, m_ln1_b, m_mlp_w1, m_mlp_w2, m_ln2_g, m_ln2_b, v_a_w_in, v_a_conv, v_a_alog, v_a_dt_bias, v_a_norm_g, v_a_w_out, v_b_w_in, v_b_gate_w2, v_b_gate_b, v_b_norm_g, v_b_w_out, v_ln1_g, v_ln1_b, v_mlp_w1, v_mlp_w2, v_ln2_g, v_ln2_b):
    w = dict(a_w_in=a_w_in, a_conv=a_conv, a_alog=a_alog, a_dt_bias=a_dt_bias, a_norm_g=a_norm_g, a_w_out=a_w_out,
             b_w_in=b_w_in, b_gate_w2=b_gate_w2, b_gate_b=b_gate_b, b_norm_g=b_norm_g, b_w_out=b_w_out, ln1_g=ln1_g,
             ln1_b=ln1_b, mlp_w1=mlp_w1, mlp_w2=mlp_w2, ln2_g=ln2_g, ln2_b=ln2_b)
    mom = dict(a_w_in=m_a_w_in, a_conv=m_a_conv, a_alog=m_a_alog, a_dt_bias=m_a_dt_bias, a_norm_g=m_a_norm_g,
               a_w_out=m_a_w_out, b_w_in=m_b_w_in, b_gate_w2=m_b_gate_w2, b_gate_b=m_b_gate_b, b_norm_g=m_b_norm_g,
               b_w_out=m_b_w_out, ln1_g=m_ln1_g, ln1_b=m_ln1_b, mlp_w1=m_mlp_w1, mlp_w2=m_mlp_w2, ln2_g=m_ln2_g,
               ln2_b=m_ln2_b)
    var = dict(a_w_in=v_a_w_in, a_conv=v_a_conv, a_alog=v_a_alog, a_dt_bias=v_a_dt_bias, a_norm_g=v_a_norm_g,
               a_w_out=v_a_w_out, b_w_in=v_b_w_in, b_gate_w2=v_b_gate_w2, b_gate_b=v_b_gate_b, b_norm_g=v_b_norm_g,
               b_w_out=v_b_w_out, ln1_g=v_ln1_g, ln1_b=v_ln1_b, mlp_w1=v_mlp_w1, mlp_w2=v_mlp_w2, ln2_g=v_ln2_g,
               ln2_b=v_ln2_b)
    chip = 2 * lax.axis_index("x") + lax.axis_index("y")

    seg_rows = [w[n].size // D_MODEL for n in BIG]
    seg_off = [sum(seg_rows[:i]) for i in range(len(BIG))]
    rows = -(-sum(seg_rows) // PACK_TILE) * PACK_TILE
    shard_pack = _pack_rows([w[n] for n in BIG], rows, D_MODEL)
    gathered = allgather_chips(cast_bf16(shard_pack))
    full = {}
    for n, off, nr in zip(BIG, seg_off, seg_rows):
        stacked = gathered[:, off:off + nr].reshape((4,) + w[n].shape)
        full[n] = _from_chip_major(stacked, SHARD_AXIS[n])
    sharded_small = tuple(SMALL_SHARD_AXIS)
    sm_shapes = [w[n].shape for n in sharded_small]
    sm_rows = _rows_for(sm_shapes, LANES)
    sm_all = exchange_all(_pack_rows([w[n] for n in sharded_small], sm_rows, LANES), "gather_small")
    per_chip = [_unpack_rows(sm_all[2 * pch], sm_shapes) for pch in range(4)]
    for idx, n in enumerate(sharded_small):
        full[n] = jnp.concatenate([per_chip[pch][idx] for pch in range(4)], axis=SMALL_SHARD_AXIS[n])
    for n in WEIGHTS:
        full.setdefault(n, w[n])

    loss_part, grad_x, grads = _local_step(x[0], loss_target[0], *[full[n] for n in WEIGHTS])
    loss = lax.psum(jnp.sum(loss_part), ("x", "y", "c"))

    gpack = jnp.concatenate(
        [_to_chip_major(grads[n], SHARD_AXIS[n]).reshape(4, nr, D_MODEL) for n, nr in zip(BIG, seg_rows)]
        + [jnp.zeros((4, rows - sum(seg_rows), D_MODEL), F32)], axis=1)
    core = lax.axis_index("c").astype(jnp.int32).reshape(1)
    chip_sum = add_sibling_half(gpack, sibling_halves(gpack), core)
    reduced = join_halves(sum_slots(scatter_chips(chip_sum), "sum_chips"))
    out_g, out_d, out_m, out_v = {}, {}, {}, {}
    for n, off, nr in zip(BIG, seg_off, seg_rows):
        if w[n].shape[-1] == D_MODEL:
            view = lambda t: t.reshape(-1, D_MODEL)
            res = adamw(view(w[n]), view(mom[n]), view(var[n]), (reduced,), off, "adamw_" + n)
        else:
            cols = w[n].shape[-1]
            view = lambda t: t.reshape(-1, cols)
            res = adamw(view(w[n]), view(mom[n]), view(var[n]), (view(reduced[off:off + nr]),), 0, "adamw_" + n)
        out_g[n], out_d[n], out_m[n], out_v[n] = (t.reshape(w[n].shape) for t in res)

    all_shapes = [full[n].shape for n in SMALL]
    g_rows = _rows_for(all_shapes, LANES)
    g_all = exchange_all(_pack_rows([grads[n] for n in SMALL], g_rows, LANES), "gather_small_grads")
    g_sum = _unpack_rows(sum_slots(g_all, "sum_small_grads"), all_shapes)
    g_mine = []
    for n, g in zip(SMALL, g_sum):
        if n in SMALL_SHARD_AXIS:
            ax = SMALL_SHARD_AXIS[n]
            g = lax.dynamic_slice_in_dim(g, chip * w[n].shape[ax], w[n].shape[ax], axis=ax)
        g_mine.append(g)
    my_shapes = [w[n].shape for n in SMALL]
    s_rows = _rows_for(my_shapes, LANES)
    pk = lambda d: _pack_rows([d[n] for n in SMALL], s_rows, LANES)
    res = adamw(pk(w), pk(mom), pk(var), (_pack_rows(g_mine, s_rows, LANES),), 0, "adamw_small")
    for dst, pack in zip((out_g, out_d, out_m, out_v), res):
        for n, t in zip(SMALL, _unpack_rows(pack, my_shapes)):
            dst[n] = t

    return (loss, grad_x[None], *[out_g[n] for n in WEIGHTS], *[out_d[n] for n in WEIGHTS],
            *[out_m[n] for n in WEIGHTS], *[out_v[n] for n in WEIGHTS])
```

```python
import functools
import math

import jax
import jax.numpy as jnp
from jax import lax
from jax.experimental import pallas as pl
from jax.experimental.pallas import tpu as pltpu

F32 = jnp.float32
BF16 = jnp.bfloat16

D_MODEL = 1024
DEPTH = 4
CHUNK = 64
A_HEADS = 8
A_DK = 128
A_W = 1024
A_CONV = 5
B_HEADS = 4
B_DK = 128
B_DV = 256
B_RANK = 16
B_TAU = 16.0
B_KW = 512
B_VW = 1024
ALPHA = (2 * DEPTH) ** 0.25
LN_EPS = 1e-5
RMS_EPS = 1e-6
L2_EPS = 1e-6
ADAM_LR = 0.001
ADAM_B1 = 0.9
ADAM_B2 = 0.999
ADAM_EPS = 1e-08
ADAM_WD = 0.01
ADAM_STEP = 10
LANES = 128
NEG_INF = float("-inf")
VMEM_LIMIT = 56 * 1024 * 1024


def _cparams(sem=None):
    return pltpu.CompilerParams(dimension_semantics=sem, vmem_limit_bytes=VMEM_LIMIT)


def _dg(a, b, ca, cb):
    return lax.dot_general(a.astype(BF16), b.astype(BF16), (((ca,), (cb,)), ((), ())),
                           preferred_element_type=F32)


def _split(x):
    hi = x.astype(BF16)
    return hi, (x - hi.astype(F32)).astype(BF16)


def _dg3(a, b, ca, cb):
    (a1, a2), (b1, b2) = _split(a), _split(b)
    return (_dg(a1, b2, ca, cb) + _dg(a2, b1, ca, cb)) + _dg(a1, b1, ca, cb)


def _dot_with_vjp(dg):
    @functools.partial(jax.custom_vjp, nondiff_argnums=(2, 3))
    def dot(a, b, ca, cb):
        return dg(a, b, ca, cb)

    def fwd(a, b, ca, cb):
        return dg(a, b, ca, cb), (a, b)

    def bwd(ca, cb, res, g):
        a, b = res
        da = dg(g, b, 1, 1 - cb) if ca == 1 else dg(b, g, 1 - cb, 1)
        db = dg(a, g, 1 - ca, 0) if cb == 0 else dg(g, a, 0, 1 - ca)
        return da, db

    dot.defvjp(fwd, bwd)
    return dot


bdot = _dot_with_vjp(_dg)
xdot3 = _dot_with_vjp(_dg3)


def nn(a, b):
    return bdot(a, b, 1, 0)


def nt(a, b):
    return bdot(a, b, 1, 1)


def tn(a, b):
    return bdot(a, b, 0, 0)


def xdot(a, b):
    return xdot3(a, b, 1, 0)


def _sigmoid(x):
    return 1.0 / (1.0 + jnp.exp(-x))


def _softplus(x):
    return jnp.maximum(x, 0.0) + jnp.log(1.0 + jnp.exp(-jnp.abs(x)))


def _chunk_masks(rev):
    ii = lax.broadcasted_iota(jnp.int32, (CHUNK, CHUNK), 0)
    jj = lax.broadcasted_iota(jnp.int32, (CHUNK, CHUNK), 1)
    d = (ii - jj) * (1 - 2 * rev)
    return d >= 0, d > 0, ii == jj, (ii >> 3) == (jj >> 3)


def _each(f, *lists):
    return [f(*xs) for xs in zip(*lists)]


def _unit_triangular_inverse(a, eye, blockdiag):
    ident = eye.astype(F32)
    ad = _each(lambda x: jnp.where(blockdiag, x, 0.0), a)
    e = _each(lambda x, y: x - y, a, ad)
    dinv = _each(lambda x: ident - x, ad)
    p = _each(xdot, ad, ad)
    dinv = _each(lambda x, y: x + xdot(x, y), dinv, p)
    p = _each(xdot, p, p)
    dinv = _each(lambda x, y: x + xdot(x, y), dinv, p)
    g = _each(lambda x, y: -xdot(x, y), dinv, e)
    finv = _each(lambda x: ident + x, g)
    p = _each(xdot, g, g)
    finv = _each(lambda x, y: x + xdot(x, y), finv, p)
    p = _each(xdot, p, p)
    finv = _each(lambda x, y: x + xdot(x, y), finv, p)
    return _each(xdot, finv, dinv)


def _gdn_step(state, q, k, v, bb, gb, rev):
    causal, strict, eye, blockdiag = _chunk_masks(rev)
    lower = causal.astype(F32)
    ones = jnp.ones((CHUNK, CHUNK), F32)
    gcb = _each(lambda x: xdot(lower, x), gb)
    gcol = _each(lambda x: x[:, :CHUNK], gcb)
    grow = _each(lambda x: xdot(ones, jnp.where(eye, x, 0.0)), gcol)
    decay = _each(lambda x, y: jnp.exp(jnp.where(causal, x - y, NEG_INF)), gcol, grow)
    kb = _each(lambda x, y: x * y, k, bb)
    a = _each(lambda x, y, z: jnp.where(strict, nt(x, y) * z, 0.0), kb, k, decay)
    t = _unit_triangular_inverse(a, eye, blockdiag)
    egc = _each(jnp.exp, gcb)
    u = _each(lambda x, y, z: xdot(x, y * z), t, v, bb)
    w = _each(lambda x, y, z: xdot(x, y * z), t, kb, egc)
    qk = _each(lambda x, y, z: nt(x, y) * z, q, k, decay)
    glast = _each(lambda x: jnp.sum(x, axis=0, keepdims=True), gb)
    v_new = _each(lambda x, y, z: x - nn(y, z), u, w, state)
    o = _each(lambda x, y, z, p, r: nn(x * y, z) + nn(p, r), q, egc, state, qk, v_new)
    k_dec = _each(lambda x, y, z: x * jnp.exp(y - z), k, glast, gcb)
    state_new = _each(lambda x, y, z, p: x * jnp.exp(y) + tn(z, p), state, glast, k_dec, v_new)
    return state_new, o


def _gla_step(state_t, q, k, v, la, rev):
    causal, _, _, _ = _chunk_masks(rev)
    sign = 1 - 2 * rev
    b = xdot(causal.astype(F32), la)
    q = q * (B_DK ** -0.5)
    row = lax.broadcasted_iota(jnp.int32, (CHUNK, B_DK), 0)
    sub = row // GLA_SUB
    scores = None
    for blk in range(CHUNK // GLA_SUB):
        r_at = jnp.where(rev == 1, GLA_SUB * (blk + 1), GLA_SUB * blk - 1)
        r = jnp.sum(jnp.where(row == r_at, b, 0.0), axis=0, keepdims=True)
        q_blk = q * jnp.exp(jnp.where(sub == blk, b - r, NEG_INF))
        k_past = k * jnp.exp(jnp.where((sub - blk) * sign < 0, r - b, NEG_INF))
        part = xdot3(q_blk, k_past, 1, 1)
        scores = part if scores is None else scores + part
    shp = (GLA_SUB, GLA_SUB, B_DK)
    d3 = (lax.broadcasted_iota(jnp.int32, shp, 0) - lax.broadcasted_iota(jnp.int32, shp, 1)) * sign
    place_r = lax.broadcasted_iota(jnp.int32, (GLA_SUB, CHUNK), 0)
    place_c = lax.broadcasted_iota(jnp.int32, (GLA_SUB, CHUNK), 1)
    diag = []
    for blk in range(CHUNK // GLA_SUB):
        rows = slice(blk * GLA_SUB, (blk + 1) * GLA_SUB)
        qb, kb, bb = q[rows], k[rows], b[rows]
        dec = jnp.exp(jnp.where(d3 >= 0, bb[:, None, :] - bb[None, :, :], NEG_INF))
        pairs = jnp.sum(qb[:, None, :] * kb[None, :, :] * dec, axis=-1)
        diag.append(xdot(pairs, (place_c == place_r + blk * GLA_SUB).astype(F32)))
    scores = scores + jnp.concatenate(diag, axis=0)
    blast = jnp.sum(la, axis=0, keepdims=True)
    o = nt(q * jnp.exp(b), state_t) + nn(scores, v)
    state_new = jnp.exp(blast) * state_t + tn(v, k * jnp.exp(blast - b))
    return state_new, o


def _chunk_pos(d, m, n):
    return m + d * (n - 1 - 2 * m)


GLA_SUB = 16
GDN_HEADS_PER_STEP = 8
def gdn_rec_fwd(q, k, v, beta_b, g_b):
    s = q.shape[0]
    n = s // CHUNK

    hb = GDN_HEADS_PER_STEP
    wide = hb * LANES

    def body(q_ref, k_ref, v_ref, bb_ref, gb_ref, o_ref, st_ref, state):
        d = pl.program_id(0)

        @pl.when(pl.program_id(2) == 0)
        def _():
            state[...] = jnp.zeros_like(state)

        cols = [slice(hh * LANES, (hh + 1) * LANES) for hh in range(hb)]
        st = [state[hh] for hh in range(hb)]
        new, o = _gdn_step(st, *([r[:, c] for c in cols] for r in (q_ref, k_ref, v_ref, bb_ref, gb_ref)), d)
        for hh, c in enumerate(cols):
            st_ref[hh] = st[hh]
            state[hh] = new[hh]
            o_ref[:, c] = o[hh]

    blk = pl.BlockSpec((CHUNK, wide), lambda d, h, m: (_chunk_pos(d, m, n), h))
    gate = pl.BlockSpec((CHUNK, wide), lambda d, h, m: (_chunk_pos(d, m, n), d * (A_HEADS // hb) + h))
    return pl.pallas_call(
        body, name="gdn_rec_fwd", grid=(2, A_HEADS // hb, n),
        in_specs=[blk, blk, blk, gate, gate],
        out_specs=[pl.BlockSpec((None, CHUNK, wide), lambda d, h, m: (d, _chunk_pos(d, m, n), h)),
                   pl.BlockSpec((None, hb, None, A_DK, LANES), lambda d, h, m: (d, h, _chunk_pos(d, m, n), 0, 0))],
        out_shape=[jax.ShapeDtypeStruct((2, s, A_W), F32), jax.ShapeDtypeStruct((2, A_HEADS, n, A_DK, LANES), F32)],
        scratch_shapes=[pltpu.VMEM((hb, A_DK, LANES), F32)],
        compiler_params=_cparams(("arbitrary", "arbitrary", "arbitrary")),
    )(q, k, v, beta_b, g_b)


def gdn_rec_bwd(q, k, v, beta_b, g_b, states, do):
    s = q.shape[0]
    n = s // CHUNK

    hb = GDN_HEADS_PER_STEP
    wide = hb * LANES

    def body(q_ref, k_ref, v_ref, bb_ref, gb_ref, st_ref, do_ref, dq_ref, dk_ref, dv_ref, dbb_ref, dgb_ref, dstate):
        d = pl.program_id(0)

        @pl.when(pl.program_id(2) == 0)
        def _():
            dstate[...] = jnp.zeros_like(dstate)

        step = functools.partial(_gdn_step, rev=d)
        cols = [slice(hh * LANES, (hh + 1) * LANES) for hh in range(hb)]
        _, vjp = jax.vjp(step, [st_ref[hh] for hh in range(hb)],
                         *([r[:, c] for c in cols] for r in (q_ref, k_ref, v_ref, bb_ref, gb_ref)))
        grads = vjp(([dstate[hh] for hh in range(hb)], [do_ref[:, c] for c in cols]))
        for hh, c in enumerate(cols):
            dstate[hh], dq_ref[:, c], dk_ref[:, c], dv_ref[:, c], dbb_ref[:, c], dgb_ref[:, c] = (g[hh] for g in grads)

    pos = lambda d, m: _chunk_pos(1 - d, m, n)
    blk = pl.BlockSpec((CHUNK, wide), lambda d, h, m: (pos(d, m), h))
    gate = pl.BlockSpec((CHUNK, wide), lambda d, h, m: (pos(d, m), d * (A_HEADS // hb) + h))
    oblk = pl.BlockSpec((None, CHUNK, wide), lambda d, h, m: (d, pos(d, m), h))
    return pl.pallas_call(
        body, name="gdn_rec_bwd", grid=(2, A_HEADS // hb, n),
        in_specs=[blk, blk, blk, gate, gate,
                  pl.BlockSpec((None, hb, None, A_DK, LANES), lambda d, h, m: (d, h, pos(d, m), 0, 0)), blk],
        out_specs=[oblk, oblk, oblk, gate, gate],
        out_shape=[jax.ShapeDtypeStruct((2, s, A_W), F32)] * 3 + [jax.ShapeDtypeStruct(beta_b.shape, F32)] * 2,
        scratch_shapes=[pltpu.VMEM((hb, A_DK, LANES), F32)],
        compiler_params=_cparams(("arbitrary", "arbitrary", "arbitrary")),
    )(q, k, v, beta_b, g_b, states, do)


def gla_rec_fwd(proj, log_a):
    s = proj.shape[0]
    n = s // CHUNK

    def body(q_ref, k_ref, v_ref, la_ref, o_ref, st_ref, state):
        d = pl.program_id(0)

        @pl.when(pl.program_id(2) == 0)
        def _():
            state[...] = jnp.zeros_like(state)

        st = state[...]
        st_ref[...] = st
        new, o = _gla_step(st, q_ref[...], k_ref[...], v_ref[...], la_ref[...], d)
        state[...] = new
        o_ref[...] = o

    return pl.pallas_call(
        body, name="gla_rec_fwd", grid=(2, B_HEADS, n),
        in_specs=[pl.BlockSpec((CHUNK, B_DK), lambda d, h, m: (_chunk_pos(d, m, n), h)),
                  pl.BlockSpec((CHUNK, B_DK), lambda d, h, m: (_chunk_pos(d, m, n), B_KW // B_DK + h)),
                  pl.BlockSpec((CHUNK, B_DV), lambda d, h, m: (_chunk_pos(d, m, n), 2 * B_KW // B_DV + h)),
                  pl.BlockSpec((None, CHUNK, B_DK), lambda d, h, m: (d, _chunk_pos(d, m, n), h))],
        out_specs=[pl.BlockSpec((None, CHUNK, B_DV), lambda d, h, m: (d, _chunk_pos(d, m, n), h)),
                   pl.BlockSpec((None, None, None, B_DV, B_DK), lambda d, h, m: (d, h, _chunk_pos(d, m, n), 0, 0))],
        out_shape=[jax.ShapeDtypeStruct((2, s, B_VW), F32), jax.ShapeDtypeStruct((2, B_HEADS, n, B_DV, B_DK), F32)],
        scratch_shapes=[pltpu.VMEM((B_DV, B_DK), F32)],
        compiler_params=_cparams(("arbitrary", "arbitrary", "arbitrary")),
    )(proj, proj, proj, log_a)


def gla_rec_bwd(proj, log_a, states, do):
    s = proj.shape[0]
    n = s // CHUNK

    def body(q_ref, k_ref, v_ref, la_ref, st_ref, do_ref, dq_ref, dk_ref, dv_ref, dla_ref, dstate):
        d = pl.program_id(0)

        @pl.when(pl.program_id(2) == 0)
        def _():
            dstate[...] = jnp.zeros_like(dstate)

        step = functools.partial(_gla_step, rev=d)
        _, vjp = jax.vjp(step, st_ref[...], q_ref[...], k_ref[...], v_ref[...], la_ref[...])
        dst, dq, dk, dv, dla = vjp((dstate[...], do_ref[...]))
        dstate[...] = dst
        dq_ref[...] = dq
        dk_ref[...] = dk
        dv_ref[...] = dv
        dla_ref[...] = dla

    pos = lambda d, m: _chunk_pos(1 - d, m, n)
    kblk = pl.BlockSpec((None, CHUNK, B_DK), lambda d, h, m: (d, pos(d, m), h))
    return pl.pallas_call(
        body, name="gla_rec_bwd", grid=(2, B_HEADS, n),
        in_specs=[pl.BlockSpec((CHUNK, B_DK), lambda d, h, m: (pos(d, m), h)),
                  pl.BlockSpec((CHUNK, B_DK), lambda d, h, m: (pos(d, m), B_KW // B_DK + h)),
                  pl.BlockSpec((CHUNK, B_DV), lambda d, h, m: (pos(d, m), 2 * B_KW // B_DV + h)),
                  kblk,
                  pl.BlockSpec((None, None, None, B_DV, B_DK), lambda d, h, m: (d, h, pos(d, m), 0, 0)),
                  pl.BlockSpec((CHUNK, B_DV), lambda d, h, m: (pos(d, m), h))],
        out_specs=[kblk, kblk, pl.BlockSpec((None, CHUNK, B_DV), lambda d, h, m: (d, pos(d, m), h)), kblk],
        out_shape=[jax.ShapeDtypeStruct((2, s, B_KW), F32), jax.ShapeDtypeStruct((2, s, B_KW), F32),
                   jax.ShapeDtypeStruct((2, s, B_VW), F32), jax.ShapeDtypeStruct((2, s, B_KW), F32)],
        scratch_shapes=[pltpu.VMEM((B_DV, B_DK), F32)],
        compiler_params=_cparams(("arbitrary", "arbitrary", "arbitrary")),
    )(proj, proj, proj, log_a, states, do)


MM_TILE_OUT = 1024
MM_TILE_K = 512


def _tile(n, pref):
    return pref if n % pref == 0 else n


def mm(a, b, mode="nn", act=None, epi=None, extra=None, alpha=1.0, name="mm"):
    if mode == "tn":
        kk, m = a.shape
    else:
        m, kk = a.shape
    nn_ = b.shape[0] if mode == "nt" else b.shape[1]
    tm, tn_, tk = _tile(m, MM_TILE_OUT), _tile(nn_, MM_TILE_OUT), _tile(kk, MM_TILE_K)
    nk = kk // tk
    ca, cb = {"nn": (1, 0), "nt": (1, 1), "tn": (0, 0)}[mode]

    def body(*refs):
        if epi is None:
            a_ref, b_ref, o_ref = refs
        else:
            a_ref, b_ref, e_ref, o_ref = refs
        kstep = pl.program_id(2)
        at = a_ref[...]
        if act == "sqrelu":
            at = jnp.square(jnp.maximum(at, 0.0))
        part = _dg(at, b_ref[...], ca, cb)

        @pl.when(kstep == 0)
        def _():
            o_ref[...] = part

        @pl.when(kstep > 0)
        def _():
            o_ref[...] += part

        if epi is not None:
            @pl.when(kstep == nk - 1)
            def _():
                if epi == "dsqrelu":
                    o_ref[...] = o_ref[...] * (2.0 * jnp.maximum(e_ref[...], 0.0))
                else:
                    o_ref[...] = o_ref[...] + alpha * e_ref[...]

    a_spec = pl.BlockSpec((tk, tm), lambda i, j, k: (k, i)) if mode == "tn" else pl.BlockSpec((tm, tk), lambda i, j, k: (i, k))
    b_spec = pl.BlockSpec((tn_, tk), lambda i, j, k: (j, k)) if mode == "nt" else pl.BlockSpec((tk, tn_), lambda i, j, k: (k, j))
    o_spec = pl.BlockSpec((tm, tn_), lambda i, j, k: (i, j))
    ins, specs = [a, b], [a_spec, b_spec]
    if epi is not None:
        ins.append(extra)
        specs.append(o_spec)
    return pl.pallas_call(
        body, name=name, grid=(m // tm, nn_ // tn_, nk), in_specs=specs, out_specs=o_spec,
        out_shape=jax.ShapeDtypeStruct((m, nn_), F32),
        compiler_params=_cparams(("parallel", "parallel", "arbitrary")),
    )(*ins)


ROWS = 256


def _ln_core(x, m, g, b):
    r = ALPHA * x + m
    mu = jnp.mean(r, axis=-1, keepdims=True)
    xc = r - mu
    var = jnp.mean(xc * xc, axis=-1, keepdims=True)
    rstd = lax.rsqrt(var + LN_EPS)
    xhat = xc * rstd
    return xhat * g + b, xhat, rstd


def ln_fwd(x, m, g, b):
    s, dm = x.shape

    def body(x_ref, m_ref, g_ref, b_ref, o_ref):
        o_ref[...] = _ln_core(x_ref[...], m_ref[...], g_ref[...], b_ref[...])[0]

    row = pl.BlockSpec((ROWS, dm), lambda i: (i, 0))
    vec = pl.BlockSpec((1, dm), lambda i: (0, 0))
    return pl.pallas_call(body, name="ln_fwd", grid=(s // ROWS,), in_specs=[row, row, vec, vec], out_specs=row,
                          out_shape=jax.ShapeDtypeStruct((s, dm), F32), compiler_params=_cparams(("parallel",)))(x, m, g, b)


def ln_bwd(x, m, g, dy):
    s, dm = x.shape

    def body(x_ref, m_ref, g_ref, dy_ref, dr_ref, dg_ref, db_ref):
        gv = g_ref[...]
        _, xhat, rstd = _ln_core(x_ref[...], m_ref[...], gv, jnp.zeros_like(gv))
        dy = dy_ref[...]
        dxh = dy * gv
        dr_ref[...] = rstd * (dxh - jnp.mean(dxh, axis=-1, keepdims=True)
                              - xhat * jnp.mean(dxh * xhat, axis=-1, keepdims=True))

        @pl.when(pl.program_id(0) == 0)
        def _():
            dg_ref[...] = jnp.zeros_like(dg_ref)
            db_ref[...] = jnp.zeros_like(db_ref)

        dg_ref[...] += jnp.sum(dy * xhat, axis=0, keepdims=True)
        db_ref[...] += jnp.sum(dy, axis=0, keepdims=True)

    row = pl.BlockSpec((ROWS, dm), lambda i: (i, 0))
    vec = pl.BlockSpec((1, dm), lambda i: (0, 0))
    return pl.pallas_call(body, name="ln_bwd", grid=(s // ROWS,), in_specs=[row, row, vec, row], out_specs=[row, vec, vec],
                          out_shape=[jax.ShapeDtypeStruct((s, dm), F32), jax.ShapeDtypeStruct((1, dm), F32),
                                     jax.ShapeDtypeStruct((1, dm), F32)],
                          compiler_params=_cparams(("arbitrary",)))(x, m, g, dy)


def loss_head(y, target):
    s, dm = y.shape

    def body(y_ref, t_ref, dy_ref, l_ref):
        e = y_ref[...] - t_ref[...]
        dy_ref[...] = e * (1.0 / dm)

        @pl.when(pl.program_id(0) == 0)
        def _():
            l_ref[...] = jnp.zeros_like(l_ref)

        col = jnp.sum(e * e, axis=0, keepdims=True) * (0.5 / dm)
        acc = col[:, :LANES]
        for c in range(1, dm // LANES):
            acc = acc + col[:, c * LANES:(c + 1) * LANES]
        l_ref[...] += acc

    row = pl.BlockSpec((ROWS, dm), lambda i: (i, 0))
    return pl.pallas_call(body, name="loss_head", grid=(s // ROWS,), in_specs=[row, row],
                          out_specs=[row, pl.BlockSpec((1, LANES), lambda i: (0, 0))],
                          out_shape=[jax.ShapeDtypeStruct((s, dm), F32), jax.ShapeDtypeStruct((1, LANES), F32)],
                          compiler_params=_cparams(("arbitrary",)))(y, target)


def _shift_rows_impl(x, d):
    n = x.shape[0]
    if d == 0:
        return x
    t = lax.broadcasted_iota(jnp.int32, x.shape, 0)
    return jnp.where((t + d >= 0) & (t + d < n), pltpu.roll(x, (-d) % n, 0), 0.0)


@functools.partial(jax.custom_vjp, nondiff_argnums=(1,))
def _shift_rows(x, d):
    return _shift_rows_impl(x, d)


_shift_rows.defvjp(lambda x, d: (_shift_rows_impl(x, d), None), lambda d, _, g: (_shift_rows_impl(g, -d),))


def _gdn_pre_fn(u, w, kind):
    rows = lax.broadcasted_iota(jnp.int32, w.shape, 0)
    c = None
    for tap in range(A_CONV):
        w_tap = jnp.sum(jnp.where(rows == tap, w, 0.0), axis=0, keepdims=True)
        term = _shift_rows(u, tap - A_CONV // 2) * w_tap
        c = term if c is None else c + term
    y = c * _sigmoid(c)
    if kind == "v":
        return y
    y = y * lax.rsqrt(jnp.sum(y * y, axis=-1, keepdims=True) + L2_EPS)
    return y * (A_DK ** -0.5) if kind == "q" else y


_KIND_OFF = {"q": 0, "k": A_HEADS, "v": 2 * A_HEADS}


def gdn_pre(proj, conv_w, kind):
    s = proj.shape[0]
    off = _KIND_OFF[kind]

    def body(u_ref, w_ref, o_ref):
        o_ref[...] = _gdn_pre_fn(u_ref[...], w_ref[...], kind)

    return pl.pallas_call(
        body, name="gdn_pre_" + kind, grid=(A_HEADS,),
        in_specs=[pl.BlockSpec((s, LANES), lambda h: (0, off + h)), pl.BlockSpec((A_CONV, LANES), lambda h: (0, off + h))],
        out_specs=pl.BlockSpec((s, LANES), lambda h: (0, h)),
        out_shape=jax.ShapeDtypeStruct((s, A_W), F32), compiler_params=_cparams(("parallel",)))(proj, conv_w)


def gdn_pre_bwd(proj, conv_w, dt2, kind):
    s = proj.shape[0]
    off = _KIND_OFF[kind]

    def body(u_ref, w_ref, d0_ref, d1_ref, du_ref, dw_ref):
        _, vjp = jax.vjp(functools.partial(_gdn_pre_fn, kind=kind), u_ref[...], w_ref[...])
        du, dw = vjp(d0_ref[...] + d1_ref[...])
        du_ref[...] = du
        dw_ref[...] = dw

    return pl.pallas_call(
        body, name="gdn_pre_bwd_" + kind, grid=(A_HEADS,),
        in_specs=[pl.BlockSpec((s, LANES), lambda h: (0, off + h)), pl.BlockSpec((A_CONV, LANES), lambda h: (0, off + h)),
                  pl.BlockSpec((None, s, LANES), lambda h: (0, 0, h)), pl.BlockSpec((None, s, LANES), lambda h: (1, 0, h))],
        out_specs=[pl.BlockSpec((s, LANES), lambda h: (0, h)), pl.BlockSpec((A_CONV, LANES), lambda h: (0, h))],
        out_shape=[jax.ShapeDtypeStruct((s, A_W), F32), jax.ShapeDtypeStruct((A_CONV, A_W), F32)],
        compiler_params=_cparams(("parallel",)))(proj, conv_w, dt2, dt2)


N_GATE = 2 * A_HEADS


def _gdn_gates_fn(ba, alog_row, dt_row):
    r = lax.broadcasted_iota(jnp.int32, (LANES, N_GATE * LANES), 0)
    c = lax.broadcasted_iota(jnp.int32, (LANES, N_GATE * LANES), 1) >> 7
    beta_b = xdot(_sigmoid(ba), (r == c).astype(F32))
    g = -(jnp.exp(alog_row) * _softplus(ba + dt_row))
    g_b = xdot(g, (r == c + N_GATE).astype(F32))
    return beta_b, g_b


def gdn_gates(ba, alog_row, dt_row):
    s = ba.shape[0]

    def body(ba_ref, al_ref, dt_ref, bb_ref, gb_ref):
        bb_ref[...], gb_ref[...] = _gdn_gates_fn(ba_ref[...], al_ref[...], dt_ref[...])

    row = pl.BlockSpec((ROWS, LANES), lambda i: (i, 0))
    vec = pl.BlockSpec((1, LANES), lambda i: (0, 0))
    wide = pl.BlockSpec((ROWS, N_GATE * LANES), lambda i: (i, 0))
    return pl.pallas_call(body, name="gdn_gates", grid=(s // ROWS,), in_specs=[row, vec, vec], out_specs=[wide, wide],
                          out_shape=[jax.ShapeDtypeStruct((s, N_GATE * LANES), F32)] * 2,
                          compiler_params=_cparams(("parallel",)))(ba, alog_row, dt_row)


def gdn_gates_bwd(ba, alog_row, dt_row, dbeta_b, dg_b):
    s = ba.shape[0]

    def body(ba_ref, al_ref, dt_ref, dbb_ref, dgb_ref, dba_ref, dal_ref, ddt_ref):
        _, vjp = jax.vjp(_gdn_gates_fn, ba_ref[...], al_ref[...], dt_ref[...])
        dba, dal, ddt = vjp((dbb_ref[...], dgb_ref[...]))
        dba_ref[...] = dba

        @pl.when(pl.program_id(0) == 0)
        def _():
            dal_ref[...] = jnp.zeros_like(dal_ref)
            ddt_ref[...] = jnp.zeros_like(ddt_ref)

        dal_ref[...] += dal
        ddt_ref[...] += ddt

    row = pl.BlockSpec((ROWS, LANES), lambda i: (i, 0))
    vec = pl.BlockSpec((1, LANES), lambda i: (0, 0))
    wide = pl.BlockSpec((ROWS, N_GATE * LANES), lambda i: (i, 0))
    return pl.pallas_call(body, name="gdn_gates_bwd", grid=(s // ROWS,), in_specs=[row, vec, vec, wide, wide],
                          out_specs=[row, vec, vec],
                          out_shape=[jax.ShapeDtypeStruct((s, LANES), F32), jax.ShapeDtypeStruct((1, LANES), F32),
                                     jax.ShapeDtypeStruct((1, LANES), F32)],
                          compiler_params=_cparams(("arbitrary",)))(ba, alog_row, dt_row, dbeta_b, dg_b)


def _post_fn(o, z, g):
    y = o * lax.rsqrt(jnp.mean(o * o, axis=-1, keepdims=True) + RMS_EPS) * g
    return y * (z * _sigmoid(z))


def mixer_post(o2, proj, norm_g, width, gate_off, name):
    s = o2.shape[1]
    nh = o2.shape[2] // width

    def body(o0_ref, o1_ref, z_ref, g_ref, y_ref):
        y_ref[...] = _post_fn(o0_ref[...] + o1_ref[...], z_ref[...], g_ref[...])

    ospec = lambda d: pl.BlockSpec((None, ROWS, width), lambda i, h: (d, i, h))
    return pl.pallas_call(
        body, name=name, grid=(s // ROWS, nh),
        in_specs=[ospec(0), ospec(1), pl.BlockSpec((ROWS, width), lambda i, h: (i, gate_off + h)),
                  pl.BlockSpec((1, width), lambda i, h: (0, 0))],
        out_specs=pl.BlockSpec((ROWS, width), lambda i, h: (i, h)),
        out_shape=jax.ShapeDtypeStruct((s, o2.shape[2]), F32),
        compiler_params=_cparams(("parallel", "parallel")))(o2, o2, proj, norm_g)


def mixer_post_bwd(o2, proj, norm_g, dy, width, gate_off, name):
    s = o2.shape[1]
    nh = o2.shape[2] // width

    def body(o0_ref, o1_ref, z_ref, g_ref, dy_ref, do_ref, dz_ref, dg_ref):
        _, vjp = jax.vjp(_post_fn, o0_ref[...] + o1_ref[...], z_ref[...], g_ref[...])
        do, dz, dg = vjp(dy_ref[...])
        do_ref[...] = do
        dz_ref[...] = dz

        @pl.when((pl.program_id(0) == 0) & (pl.program_id(1) == 0))
        def _():
            dg_ref[...] = jnp.zeros_like(dg_ref)

        dg_ref[...] += dg

    ospec = lambda d: pl.BlockSpec((None, ROWS, width), lambda i, h: (d, i, h))
    blk = pl.BlockSpec((ROWS, width), lambda i, h: (i, h))
    vec = pl.BlockSpec((1, width), lambda i, h: (0, 0))
    return pl.pallas_call(
        body, name=name, grid=(s // ROWS, nh),
        in_specs=[ospec(0), ospec(1), pl.BlockSpec((ROWS, width), lambda i, h: (i, gate_off + h)), vec, blk],
        out_specs=[blk, blk, vec],
        out_shape=[jax.ShapeDtypeStruct((s, o2.shape[2]), F32)] * 2 + [jax.ShapeDtypeStruct((1, width), F32)],
        compiler_params=_cparams(("arbitrary", "arbitrary")))(o2, o2, proj, norm_g, dy)


def _log_gate(z):
    return (jnp.minimum(z, 0.0) - jnp.log(1.0 + jnp.exp(-jnp.abs(z)))) * (1.0 / B_TAU)


def gla_gate(gl, w2, gb):
    s = gl.shape[0]

    def body(gl_ref, w_ref, b_ref, o_ref):
        for n in range(2):
            o_ref[n] = _log_gate(nn(gl_ref[...], w_ref[n]) + b_ref[n])

    full = lambda shp: pl.BlockSpec(shp, lambda i: (0,) * len(shp))
    return pl.pallas_call(
        body, name="gla_gate", grid=(s // ROWS,),
        in_specs=[pl.BlockSpec((ROWS, LANES), lambda i: (i, 0)), full(w2.shape), full(gb.shape)],
        out_specs=pl.BlockSpec((2, ROWS, B_KW), lambda i: (0, i, 0)),
        out_shape=jax.ShapeDtypeStruct((2, s, B_KW), F32), compiler_params=_cparams(("parallel",)))(gl, w2, gb)


def gla_gate_bwd(gl, w2, gb, dla):
    s = gl.shape[0]

    def body(gl_ref, w_ref, b_ref, dla_ref, dgl_ref, dz_ref, db0_ref, db1_ref):
        @pl.when(pl.program_id(0) == 0)
        def _():
            db0_ref[...] = jnp.zeros_like(db0_ref)
            db1_ref[...] = jnp.zeros_like(db1_ref)

        dgl = None
        for n, db_ref in enumerate((db0_ref, db1_ref)):
            _, vjp = jax.vjp(_log_gate, nn(gl_ref[...], w_ref[n]) + b_ref[n])
            dz, = vjp(dla_ref[n])
            dz_ref[n] = dz
            db_ref[...] += jnp.sum(dz, axis=0, keepdims=True)
            part = nt(dz, w_ref[n])
            dgl = part if dgl is None else dgl + part
        dgl_ref[...] = dgl

    full = lambda shp: pl.BlockSpec(shp, lambda i: (0,) * len(shp))
    row = pl.BlockSpec((ROWS, LANES), lambda i: (i, 0))
    wide = pl.BlockSpec((2, ROWS, B_KW), lambda i: (0, i, 0))
    vec = pl.BlockSpec((1, B_KW), lambda i: (0, 0))
    return pl.pallas_call(
        body, name="gla_gate_bwd", grid=(s // ROWS,),
        in_specs=[row, full(w2.shape), full(gb.shape), wide],
        out_specs=[row, wide, vec, vec],
        out_shape=[jax.ShapeDtypeStruct((s, LANES), F32), jax.ShapeDtypeStruct((2, s, B_KW), F32),
                   jax.ShapeDtypeStruct((1, B_KW), F32), jax.ShapeDtypeStruct((1, B_KW), F32)],
        compiler_params=_cparams(("arbitrary",)))(gl, w2, gb, dla)


PACK_TILE = 512


def cast_bf16(x):
    r, c = x.shape

    def body(x_ref, o_ref):
        o_ref[...] = x_ref[...].astype(BF16)

    blk = pl.BlockSpec((PACK_TILE, c), lambda i: (i, 0))
    return pl.pallas_call(body, name="cast_bf16", grid=(r // PACK_TILE,), in_specs=[blk], out_specs=blk,
                          out_shape=jax.ShapeDtypeStruct((r, c), BF16), compiler_params=_cparams(("parallel",)))(x)


def sum_slots(x, name):
    n, r, c = x.shape
    tr = _tile(r, PACK_TILE)

    def body(x_ref, o_ref):
        acc = x_ref[0].astype(F32)
        for k in range(1, n):
            acc = acc + x_ref[k].astype(F32)
        o_ref[...] = acc

    return pl.pallas_call(body, name=name, grid=(r // tr,), in_specs=[pl.BlockSpec((n, tr, c), lambda i: (0, i, 0))],
                          out_specs=pl.BlockSpec((tr, c), lambda i: (i, 0)),
                          out_shape=jax.ShapeDtypeStruct((r, c), F32), compiler_params=_cparams(("parallel",)))(x)


def add_sibling_half(gpack, theirs, core):
    n, r, c = gpack.shape
    half_rows = r // 2
    tr = _tile(half_rows, PACK_TILE)
    nblk = half_rows // tr

    def body(core_ref, g_ref, t_ref, o_ref):
        o_ref[...] = (g_ref[...] + t_ref[...]).astype(BF16)

    blk = pl.BlockSpec((None, tr, c), lambda s, i, core_ref: (s, i, 0))
    return pl.pallas_call(
        body, name="add_sibling_half",
        grid_spec=pltpu.PrefetchScalarGridSpec(
            num_scalar_prefetch=1, grid=(n, nblk),
            in_specs=[pl.BlockSpec((None, tr, c), lambda s, i, core_ref: (s, core_ref[0] * nblk + i, 0)), blk],
            out_specs=blk),
        out_shape=jax.ShapeDtypeStruct((n, half_rows, c), BF16),
        compiler_params=_cparams(("parallel", "parallel")))(core, gpack, theirs)


def adamw(w, m, v, grads, g_row_off, name):
    r, c = w.shape
    tr = next(t for t in (PACK_TILE, r) if r % t == 0 and g_row_off % t == 0)
    ob = g_row_off // tr
    ng = len(grads)

    def body(*refs):
        w_ref, m_ref, v_ref = refs[:3]
        g_refs = refs[3:3 + ng]
        g_ref, d_ref, nm_ref, nv_ref = refs[3 + ng:]
        g = g_refs[0][...]
        for gr in g_refs[1:]:
            g = g + gr[...]
        m_new = ADAM_B1 * m_ref[...] + (1.0 - ADAM_B1) * g
        v_new = ADAM_B2 * v_ref[...] + (1.0 - ADAM_B2) * jnp.square(g)
        m_hat = m_new / (1.0 - ADAM_B1 ** ADAM_STEP)
        v_hat = v_new / (1.0 - ADAM_B2 ** ADAM_STEP)
        g_ref[...] = g
        d_ref[...] = -ADAM_LR * (m_hat / (jnp.sqrt(v_hat) + ADAM_EPS) + ADAM_WD * w_ref[...])
        nm_ref[...] = m_new
        nv_ref[...] = v_new

    blk = pl.BlockSpec((tr, c), lambda i: (i, 0))
    gblk = pl.BlockSpec((tr, c), lambda i: (i + ob, 0))
    return pl.pallas_call(body, name=name, grid=(r // tr,), in_specs=[blk, blk, blk] + [gblk] * ng, out_specs=[blk] * 4,
                          out_shape=[jax.ShapeDtypeStruct((r, c), F32)] * 4,
                          compiler_params=_cparams(("parallel",)))(w, m, v, *grads)


MESH = pl.DeviceIdType.MESH
HBM = pl.BlockSpec(memory_space=pl.ANY)
CHIP_FLIPS = ((1, 0), (0, 1), (1, 1))


def _place():
    return lax.axis_index("x"), lax.axis_index("y"), lax.axis_index("c")


def allgather_chips(pack):
    r, c = pack.shape
    half_rows = r // 2

    def body(src_ref, out_ref, send_sems, recv_sems, local_sem):
        x, y, cc = _place()
        half = pl.ds(cc * half_rows, half_rows)
        other = pl.ds((1 - cc) * half_rows, half_rows)
        mine = pltpu.make_async_copy(src_ref, out_ref.at[2 * x + y], local_sem)
        mine.start()

        def copy(k, src, dst, to):
            return pltpu.make_async_remote_copy(src_ref=src, dst_ref=dst, send_sem=send_sems.at[k],
                                                recv_sem=recv_sems.at[k], device_id=to, device_id_type=MESH)

        chips = [((1 - x if fx else x), (1 - y if fy else y)) for fx, fy in CHIP_FLIPS]
        first = [copy(k, src_ref.at[half], out_ref.at[2 * x + y, half], (px, py, cc)) for k, (px, py) in enumerate(chips)]
        for cp in first:
            cp.start()
        passed = []
        for k, (px, py) in enumerate(chips):
            landed = out_ref.at[2 * px + py, half]
            copy(k, landed, landed, (px, py, cc)).wait_recv()
            passed.append(copy(3 + k, landed, landed, (x, y, 1 - cc)))
            passed[-1].start()
        for k, (px, py) in enumerate(chips):
            theirs = out_ref.at[2 * px + py, other]
            copy(3 + k, theirs, theirs, (x, y, 1 - cc)).wait_recv()
        for cp in first + passed:
            cp.wait_send()
        mine.wait()

    return pl.pallas_call(
        body, name="allgather_chips", in_specs=[HBM], out_specs=HBM,
        out_shape=jax.ShapeDtypeStruct((4, r, c), pack.dtype),
        scratch_shapes=[pltpu.SemaphoreType.DMA((6,)), pltpu.SemaphoreType.DMA((6,)), pltpu.SemaphoreType.DMA],
    )(pack)


def scatter_chips(gpack):
    def body(src_ref, out_ref, send_sems, recv_sems, local_sem):
        x, y, cc = _place()
        mine = pltpu.make_async_copy(src_ref.at[2 * x + y], out_ref.at[3], local_sem)
        mine.start()
        sends = []
        for k, (fx, fy) in enumerate(CHIP_FLIPS):
            px, py = (1 - x if fx else x), (1 - y if fy else y)
            sends.append(pltpu.make_async_remote_copy(
                src_ref=src_ref.at[2 * px + py], dst_ref=out_ref.at[k], send_sem=send_sems.at[k], recv_sem=recv_sems.at[k],
                device_id=(px, py, cc), device_id_type=MESH))
        for cp in sends:
            cp.start()
        for cp in sends:
            cp.wait_recv()
        for cp in sends:
            cp.wait_send()
        mine.wait()

    return pl.pallas_call(
        body, name="scatter_chips", in_specs=[HBM], out_specs=HBM,
        out_shape=jax.ShapeDtypeStruct(gpack.shape, gpack.dtype),
        scratch_shapes=[pltpu.SemaphoreType.DMA((3,)), pltpu.SemaphoreType.DMA((3,)), pltpu.SemaphoreType.DMA],
    )(gpack)


def sibling_halves(gpack):
    n, r, c = gpack.shape
    half_rows = r // 2

    def body(src_ref, out_ref, send_sem, recv_sem):
        x, y, cc = _place()
        cp = pltpu.make_async_remote_copy(
            src_ref=src_ref.at[:, pl.ds((1 - cc) * half_rows, half_rows)], dst_ref=out_ref, send_sem=send_sem,
            recv_sem=recv_sem, device_id=(x, y, 1 - cc), device_id_type=MESH)
        cp.start()
        cp.wait()

    return pl.pallas_call(
        body, name="sibling_halves", in_specs=[HBM], out_specs=HBM,
        out_shape=jax.ShapeDtypeStruct((n, half_rows, c), gpack.dtype),
        scratch_shapes=[pltpu.SemaphoreType.DMA, pltpu.SemaphoreType.DMA],
    )(gpack)


def join_halves(mine):
    half_rows, c = mine.shape

    def body(src_ref, out_ref, send_sem, recv_sem, local_sem):
        x, y, cc = _place()
        half = pl.ds(cc * half_rows, half_rows)
        other = pl.ds((1 - cc) * half_rows, half_rows)
        keep = pltpu.make_async_copy(src_ref, out_ref.at[half], local_sem)
        keep.start()
        send = pltpu.make_async_remote_copy(src_ref=src_ref, dst_ref=out_ref.at[half], send_sem=send_sem,
                                            recv_sem=recv_sem, device_id=(x, y, 1 - cc), device_id_type=MESH)
        send.start()
        pltpu.make_async_remote_copy(src_ref=src_ref, dst_ref=out_ref.at[other], send_sem=send_sem, recv_sem=recv_sem,
                                     device_id=(x, y, 1 - cc), device_id_type=MESH).wait_recv()
        send.wait_send()
        keep.wait()

    return pl.pallas_call(
        body, name="join_halves", in_specs=[HBM], out_specs=HBM,
        out_shape=jax.ShapeDtypeStruct((2 * half_rows, c), mine.dtype),
        scratch_shapes=[pltpu.SemaphoreType.DMA, pltpu.SemaphoreType.DMA, pltpu.SemaphoreType.DMA],
    )(mine)


def exchange_all(v, name):
    r, c = v.shape

    def body(v_ref, out_ref, send_sems, recv_sems):
        x, y, cc = _place()
        out_ref[4 * x + 2 * y + cc] = v_ref[...]
        sends, recvs = [], []
        for k in range(1, 8):
            px = 1 - x if k & 4 else x
            py = 1 - y if k & 2 else y
            pc = 1 - cc if k & 1 else cc
            sends.append(pltpu.make_async_remote_copy(
                src_ref=v_ref, dst_ref=out_ref.at[4 * x + 2 * y + cc], send_sem=send_sems.at[k - 1],
                recv_sem=recv_sems.at[k - 1], device_id=(px, py, pc), device_id_type=MESH))
            recvs.append(pltpu.make_async_remote_copy(
                src_ref=v_ref, dst_ref=out_ref.at[4 * px + 2 * py + pc], send_sem=send_sems.at[k - 1],
                recv_sem=recv_sems.at[k - 1], device_id=(px, py, pc), device_id_type=MESH))
        for cp in sends:
            cp.start()
        for cp in recvs:
            cp.wait_recv()
        for cp in sends:
            cp.wait_send()

    vm = pl.BlockSpec(memory_space=pltpu.VMEM)
    return pl.pallas_call(
        body, name=name, in_specs=[vm], out_specs=vm, out_shape=jax.ShapeDtypeStruct((8, r, c), v.dtype),
        scratch_shapes=[pltpu.SemaphoreType.DMA((7,)), pltpu.SemaphoreType.DMA((7,))],
        compiler_params=pltpu.CompilerParams(vmem_limit_bytes=VMEM_LIMIT),
    )(v)


def _pack_rows(arrays, rows, width):
    flat = jnp.concatenate([a.reshape(-1) for a in arrays])
    return jnp.pad(flat, (0, rows * width - flat.shape[0])).reshape(rows, width)


def _unpack_rows(pack, shapes):
    flat = pack.reshape(-1)
    out, off = [], 0
    for shp in shapes:
        n = math.prod(shp)
        out.append(flat[off:off + n].reshape(shp))
        off += n
    return out


def _rows_for(shapes, width, mult=8):
    n = sum(math.prod(s) for s in shapes)
    return -(-n // (width * mult)) * mult


def _gdn_fwd(x, p):
    proj = mm(x, p["w_main"], name="gdn_proj")
    ba = mm(x, p["w_gate"], name="gdn_proj_gate")
    q, k, v = (gdn_pre(proj, p["conv"], kind) for kind in "qkv")
    beta_b, g_b = gdn_gates(ba, p["alog_row"], p["dt_row"])
    o2, st = gdn_rec_fwd(q, k, v, beta_b, g_b)
    y = mixer_post(o2, proj, p["norm_g"], A_DK, 3 * A_HEADS, "gdn_post")
    m = mm(y, p["w_out"], name="gdn_out")
    return m, (x, proj, ba, q, k, v, beta_b, g_b, o2, st, y)


def _gdn_bwd(saved, p, dm):
    x, proj, ba, q, k, v, beta_b, g_b, o2, st, y = saved
    d_w_out = mm(y, dm, "tn", name="gdn_dw_out")
    dy = mm(dm, p["w_out"], "nt", name="gdn_dy")
    do, dz, d_norm_g = mixer_post_bwd(o2, proj, p["norm_g"], dy, A_DK, 3 * A_HEADS, "gdn_post_bwd")
    dq2, dk2, dv2, dbb, dgb = gdn_rec_bwd(q, k, v, beta_b, g_b, st, do)
    dba, d_alog_row, d_dt_row = gdn_gates_bwd(ba, p["alog_row"], p["dt_row"], dbb, dgb)
    du, dconv = zip(*(gdn_pre_bwd(proj, p["conv"], d2, kind) for d2, kind in ((dq2, "q"), (dk2, "k"), (dv2, "v"))))
    dproj = jnp.concatenate(list(du) + [dz], axis=1)
    d_w_main = mm(x, dproj, "tn", name="gdn_dw_main")
    d_w_gate = mm(x, dba, "tn", name="gdn_dw_gate")
    dx = mm(dba, p["w_gate"], "nt", epi="add", extra=dm, alpha=ALPHA, name="gdn_dx_gate")
    dx = mm(dproj, p["w_main"], "nt", epi="add", extra=dx, name="gdn_dx")
    grads = dict(w_in=jnp.concatenate([d_w_main, d_w_gate[:, :2 * N_GATE]], axis=1), conv=jnp.concatenate(dconv, axis=1),
                 alog=d_alog_row[0, N_GATE:2 * N_GATE].reshape(2, A_HEADS), dt=d_dt_row[0, N_GATE:2 * N_GATE].reshape(2, A_HEADS),
                 norm_g=d_norm_g[0], w_out=d_w_out)
    return dx, grads


def _gla_fwd(x, p):
    proj = mm(x, p["w_main"], name="gla_proj")
    gl = mm(x, p["w_gate"], name="gla_proj_gate")
    log_a = gla_gate(gl, p["w2"], p["gate_b"])
    o2, st = gla_rec_fwd(proj, log_a)
    y = mixer_post(o2, proj, p["norm_g"], B_DV, (2 * B_KW + B_VW) // B_DV, "gla_post")
    m = mm(y, p["w_out"], name="gla_out")
    return m, (x, proj, gl, log_a, o2, st, y)


def _gla_bwd(saved, p, dm):
    x, proj, gl, log_a, o2, st, y = saved
    d_w_out = mm(y, dm, "tn", name="gla_dw_out")
    dy = mm(dm, p["w_out"], "nt", name="gla_dy")
    do, dr, d_norm_g = mixer_post_bwd(o2, proj, p["norm_g"], dy, B_DV, (2 * B_KW + B_VW) // B_DV, "gla_post_bwd")
    dq2, dk2, dv2, dla = gla_rec_bwd(proj, log_a, st, do)
    dgl, dz, d_b0, d_b1 = gla_gate_bwd(gl, p["w2"], p["gate_b"], dla)
    d_w2 = [mm(gl, dz[n], "tn", name="gla_dw_gate_w2") for n in range(2)]
    dproj = jnp.concatenate([dq2[0] + dq2[1], dk2[0] + dk2[1], dv2[0] + dv2[1], dr], axis=1)
    d_w_main = mm(x, dproj, "tn", name="gla_dw_main")
    d_w_gate = mm(x, dgl, "tn", name="gla_dw_gate")
    dx = mm(dgl, p["w_gate"], "nt", epi="add", extra=dm, alpha=ALPHA, name="gla_dx_gate")
    dx = mm(dproj, p["w_main"], "nt", epi="add", extra=dx, name="gla_dx")
    grads = dict(w_in=jnp.concatenate([d_w_main, d_w_gate[:, :2 * B_RANK]], axis=1),
                 gate_w2=jnp.stack([d_w2[n][n * B_RANK:(n + 1) * B_RANK] for n in range(2)]),
                 gate_b=jnp.concatenate([d_b0, d_b1]), norm_g=d_norm_g[0], w_out=d_w_out)
    return dx, grads


def _pad_cols(w, width=LANES):
    return jnp.pad(w, ((0, 0), (0, width - w.shape[1])))


def _local_step(x, target, a_w_in, a_conv, a_alog, a_dt_bias, a_norm_g, a_w_out, b_w_in, b_gate_w2, b_gate_b, b_norm_g,
                b_w_out, ln1_g, ln1_b, mlp_w1, mlp_w2, ln2_g, ln2_b):
    layer_p = []
    for i in range(DEPTH):
        j = i // 2
        if i % 2 == 0:
            layer_p.append(dict(
                w_main=a_w_in[j][:, :4 * A_W], w_gate=_pad_cols(a_w_in[j][:, 4 * A_W:]), conv=a_conv[j],
                alog_row=jnp.pad(a_alog[j].reshape(1, N_GATE), ((0, 0), (N_GATE, LANES - 2 * N_GATE))),
                dt_row=jnp.pad(a_dt_bias[j].reshape(1, N_GATE), ((0, 0), (N_GATE, LANES - 2 * N_GATE))),
                norm_g=a_norm_g[j].reshape(1, A_DK), w_out=a_w_out[j]))
        else:
            w2 = jnp.stack([jnp.pad(b_gate_w2[j][n], ((n * B_RANK, LANES - (n + 1) * B_RANK), (0, 0))) for n in range(2)])
            layer_p.append(dict(
                w_main=b_w_in[j][:, :2 * B_KW + 2 * B_VW], w_gate=_pad_cols(b_w_in[j][:, 2 * B_KW + 2 * B_VW:]),
                w2=w2, gate_b=b_gate_b[j].reshape(2, 1, B_KW), norm_g=b_norm_g[j].reshape(1, B_DV), w_out=b_w_out[j]))

    saved = []
    h = x
    for i in range(DEPTH):
        p = layer_p[i]
        m, sv = (_gdn_fwd if i % 2 == 0 else _gla_fwd)(h, p)
        x1 = ln_fwd(h, m, ln1_g[i:i + 1], ln1_b[i:i + 1])
        h1 = mm(x1, mlp_w1[i], name="mlp_up")
        mlp = mm(h1, mlp_w2[i], act="sqrelu", name="mlp_down")
        x2 = ln_fwd(x1, mlp, ln2_g[i:i + 1], ln2_b[i:i + 1])
        saved.append((sv, h, m, x1, h1, mlp))
        h = x2

    dh, loss_part = loss_head(h, target)

    g_a, g_b, g_ln1g, g_ln1b, g_ln2g, g_ln2b, g_w1, g_w2 = {}, {}, {}, {}, {}, {}, {}, {}
    for i in reversed(range(DEPTH)):
        sv, xin, m, x1, h1, mlp = saved[i]
        p = layer_p[i]
        dr2, g_ln2g[i], g_ln2b[i] = ln_bwd(x1, mlp, ln2_g[i:i + 1], dh)
        g_w2[i] = mm(h1, dr2, "tn", act="sqrelu", name="mlp_dw_down")
        dh1 = mm(dr2, mlp_w2[i], "nt", epi="dsqrelu", extra=h1, name="mlp_dh")
        g_w1[i] = mm(x1, dh1, "tn", name="mlp_dw_up")
        dx1 = mm(dh1, mlp_w1[i], "nt", epi="add", extra=dr2, alpha=ALPHA, name="mlp_dx")
        dr1, g_ln1g[i], g_ln1b[i] = ln_bwd(xin, m, ln1_g[i:i + 1], dx1)
        dh, g = (_gdn_bwd if i % 2 == 0 else _gla_bwd)(sv, p, dr1)
        (g_a if i % 2 == 0 else g_b)[i // 2] = g

    per_layer = lambda d, key=None: [(d[i] if key is None else d[i][key]) for i in sorted(d)]
    st = lambda d, key=None: jnp.stack(per_layer(d, key))
    grads = dict(
        a_w_in=per_layer(g_a, "w_in"), a_conv=st(g_a, "conv"), a_alog=st(g_a, "alog"), a_dt_bias=st(g_a, "dt"),
        a_norm_g=st(g_a, "norm_g"), a_w_out=per_layer(g_a, "w_out"), b_w_in=per_layer(g_b, "w_in"),
        b_gate_w2=st(g_b, "gate_w2"), b_gate_b=st(g_b, "gate_b"), b_norm_g=st(g_b, "norm_g"),
        b_w_out=per_layer(g_b, "w_out"), ln1_g=st(g_ln1g)[:, 0], ln1_b=st(g_ln1b)[:, 0], mlp_w1=per_layer(g_w1),
        mlp_w2=per_layer(g_w2), ln2_g=st(g_ln2g)[:, 0], ln2_b=st(g_ln2b)[:, 0])
    return loss_part, dh, grads


WEIGHTS = ("a_w_in", "a_conv", "a_alog", "a_dt_bias", "a_norm_g", "a_w_out", "b_w_in", "b_gate_w2", "b_gate_b",
           "b_norm_g", "b_w_out", "ln1_g", "ln1_b", "mlp_w1", "mlp_w2", "ln2_g", "ln2_b")
BIG = ("mlp_w1", "mlp_w2", "a_w_out", "b_w_out", "a_w_in", "b_w_in")
SHARD_AXIS = {"mlp_w1": 2, "mlp_w2": 1, "a_w_out": 1, "b_w_out": 1, "a_w_in": 2, "b_w_in": 2}
SMALL = tuple(n for n in WEIGHTS if n not in BIG)
SMALL_SHARD_AXIS = {"a_conv": 2, "b_gate_w2": 3, "b_gate_b": 2, "b_norm_g": 1}


def _to_chip_major(full, axis):
    shp = full.shape
    t = full.reshape(shp[:axis] + (4, shp[axis] // 4) + shp[axis + 1:])
    return jnp.moveaxis(t, axis, 0)


def _from_chip_major(stacked, axis):
    t = jnp.moveaxis(stacked, 0, axis)
    shp = t.shape
    return t.reshape(shp[:axis] + (shp[axis] * shp[axis + 1],) + shp[axis + 2:])


def kernel(x, a_w_in, a_conv, a_alog, a_dt_bias, a_norm_g, a_w_out, b_w_in, b_gate_w2, b_gate_b, b_norm_g, b_w_out, ln1_g, ln1_b, mlp_w1, mlp_w2, ln2_g, ln2_b, loss_target, m_a_w_in, m_a_conv, m_a_alog, m_a_dt_bias, m_a_norm_g, m_a_w_out, m_b_w_in, m_b_gate_w2, m_b_gate_b, m_b_norm_g, m_b_w_out, m_ln1_g, m_ln1_b, m_mlp_w1, m_mlp_w2, m_ln2_g, m_ln2_b, v_a_w_in, v_a_conv, v_a_alog, v_a_dt_bias, v_a_norm_g, v_a_w_out, v_b_w_in, v_b_gate_w2, v_b_gate_b, v_b_norm_g, v_b_w_out, v_ln1_g, v_ln1_b, v_mlp_w1, v_mlp_w2, v_ln2_g, v_ln2_b):
    w = dict(a_w_in=a_w_in, a_conv=a_conv, a_alog=a_alog, a_dt_bias=a_dt_bias, a_norm_g=a_norm_g, a_w_out=a_w_out,
             b_w_in=b_w_in, b_gate_w2=b_gate_w2, b_gate_b=b_gate_b, b_norm_g=b_norm_g, b_w_out=b_w_out, ln1_g=ln1_g,
             ln1_b=ln1_b, mlp_w1=mlp_w1, mlp_w2=mlp_w2, ln2_g=ln2_g, ln2_b=ln2_b)
    mom = dict(a_w_in=m_a_w_in, a_conv=m_a_conv, a_alog=m_a_alog, a_dt_bias=m_a_dt_bias, a_norm_g=m_a_norm_g,
               a_w_out=m_a_w_out, b_w_in=m_b_w_in, b_gate_w2=m_b_gate_w2, b_gate_b=m_b_gate_b, b_norm_g=m_b_norm_g,
               b_w_out=m_b_w_out, ln1_g=m_ln1_g, ln1_b=m_ln1_b, mlp_w1=m_mlp_w1, mlp_w2=m_mlp_w2, ln2_g=m_ln2_g,
               ln2_b=m_ln2_b)
    var = dict(a_w_in=v_a_w_in, a_conv=v_a_conv, a_alog=v_a_alog, a_dt_bias=v_a_dt_bias, a_norm_g=v_a_norm_g,
               a_w_out=v_a_w_out, b_w_in=v_b_w_in, b_gate_w2=v_b_gate_w2, b_gate_b=v_b_gate_b, b_norm_g=v_b_norm_g,
               b_w_out=v_b_w_out, ln1_g=v_ln1_g, ln1_b=v_ln1_b, mlp_w1=v_mlp_w1, mlp_w2=v_mlp_w2, ln2_g=v_ln2_g,
               ln2_b=v_ln2_b)
    chip = 2 * lax.axis_index("x") + lax.axis_index("y")

    seg_rows = [w[n].size // D_MODEL for n in BIG]
    seg_off = [sum(seg_rows[:i]) for i in range(len(BIG))]
    rows = -(-sum(seg_rows) // PACK_TILE) * PACK_TILE
    shard_pack = _pack_rows([w[n] for n in BIG], rows, D_MODEL)
    gathered = allgather_chips(cast_bf16(shard_pack))
    full = {}
    for n, off, nr in zip(BIG, seg_off, seg_rows):
        stacked = gathered[:, off:off + nr].reshape((4,) + w[n].shape)
        full[n] = _from_chip_major(stacked, SHARD_AXIS[n])
    sharded_small = tuple(SMALL_SHARD_AXIS)
    sm_shapes = [w[n].shape for n in sharded_small]
    sm_rows = _rows_for(sm_shapes, LANES)
    sm_all = exchange_all(_pack_rows([w[n] for n in sharded_small], sm_rows, LANES), "gather_small")
    per_chip = [_unpack_rows(sm_all[2 * pch], sm_shapes) for pch in range(4)]
    for idx, n in enumerate(sharded_small):
        full[n] = jnp.concatenate([per_chip[pch][idx] for pch in range(4)], axis=SMALL_SHARD_AXIS[n])
    for n in WEIGHTS:
        full.setdefault(n, w[n])

    loss_part, grad_x, grads = _local_step(x[0], loss_target[0], *[full[n] for n in WEIGHTS])
    loss = lax.psum(jnp.sum(loss_part), ("x", "y", "c"))

    gpack = jnp.concatenate(
        [_to_chip_major(g, SHARD_AXIS[n] - 1).reshape(4, -1, D_MODEL) for n in BIG for g in grads[n]]
        + [jnp.zeros((4, rows - sum(seg_rows), D_MODEL), F32)], axis=1)
    core = lax.axis_index("c").astype(jnp.int32).reshape(1)
    chip_sum = add_sibling_half(gpack, sibling_halves(gpack), core)
    reduced = join_halves(sum_slots(scatter_chips(chip_sum), "sum_chips"))
    out_g, out_d, out_m, out_v = {}, {}, {}, {}
    for n, off, nr in zip(BIG, seg_off, seg_rows):
        if w[n].shape[-1] == D_MODEL:
            view = lambda t: t.reshape(-1, D_MODEL)
            res = adamw(view(w[n]), view(mom[n]), view(var[n]), (reduced,), off, "adamw_" + n)
        else:
            cols = w[n].shape[-1]
            view = lambda t: t.reshape(-1, cols)
            res = adamw(view(w[n]), view(mom[n]), view(var[n]), (view(reduced[off:off + nr]),), 0, "adamw_" + n)
        out_g[n], out_d[n], out_m[n], out_v[n] = (t.reshape(w[n].shape) for t in res)

    all_shapes = [full[n].shape for n in SMALL]
    g_rows = _rows_for(all_shapes, LANES)
    g_all = exchange_all(_pack_rows([grads[n] for n in SMALL], g_rows, LANES), "gather_small_grads")
    g_sum = _unpack_rows(sum_slots(g_all, "sum_small_grads"), all_shapes)
    g_mine = []
    for n, g in zip(SMALL, g_sum):
        if n in SMALL_SHARD_AXIS:
            ax = SMALL_SHARD_AXIS[n]
            g = lax.dynamic_slice_in_dim(g, chip * w[n].shape[ax], w[n].shape[ax], axis=ax)
        g_mine.append(g)
    my_shapes = [w[n].shape for n in SMALL]
    s_rows = _rows_for(my_shapes, LANES)
    pk = lambda d: _pack_rows([d[n] for n in SMALL], s_rows, LANES)
    res = adamw(pk(w), pk(mom), pk(var), (_pack_rows(g_mine, s_rows, LANES),), 0, "adamw_small")
    for dst, pack in zip((out_g, out_d, out_m, out_v), res):
        for n, t in zip(SMALL, _unpack_rows(pack, my_shapes)):
            dst[n] = t

    return (loss, grad_x[None], *[out_g[n] for n in WEIGHTS], *[out_d[n] for n in WEIGHTS],
            *[out_m[n] for n in WEIGHTS], *[out_v[n] for n in WEIGHTS])
```

```python
import functools
import math

import jax
import jax.numpy as jnp
from jax import lax
from jax.experimental import pallas as pl
from jax.experimental.pallas import tpu as pltpu

F32 = jnp.float32
BF16 = jnp.bfloat16

D_MODEL = 1024
DEPTH = 4
CHUNK = 64
A_HEADS = 8
A_DK = 128
A_W = 1024
A_CONV = 5
B_HEADS = 4
B_DK = 128
B_DV = 256
B_RANK = 16
B_TAU = 16.0
B_KW = 512
B_VW = 1024
ALPHA = (2 * DEPTH) ** 0.25
LN_EPS = 1e-5
RMS_EPS = 1e-6
L2_EPS = 1e-6
ADAM_LR = 0.001
ADAM_B1 = 0.9
ADAM_B2 = 0.999
ADAM_EPS = 1e-08
ADAM_WD = 0.01
ADAM_STEP = 10
LANES = 128
NEG_INF = float("-inf")
VMEM_LIMIT = 56 * 1024 * 1024


def _cparams(sem=None):
    return pltpu.CompilerParams(dimension_semantics=sem, vmem_limit_bytes=VMEM_LIMIT)


def _dg(a, b, ca, cb):
    return lax.dot_general(a.astype(BF16), b.astype(BF16), (((ca,), (cb,)), ((), ())),
                           preferred_element_type=F32)


def _split(x):
    hi = x.astype(BF16)
    return hi, (x - hi.astype(F32)).astype(BF16)


def _dg3(a, b, ca, cb):
    (a1, a2), (b1, b2) = _split(a), _split(b)
    return (_dg(a1, b2, ca, cb) + _dg(a2, b1, ca, cb)) + _dg(a1, b1, ca, cb)


def _dot_with_vjp(dg):
    @functools.partial(jax.custom_vjp, nondiff_argnums=(2, 3))
    def dot(a, b, ca, cb):
        return dg(a, b, ca, cb)

    def fwd(a, b, ca, cb):
        return dg(a, b, ca, cb), (a, b)

    def bwd(ca, cb, res, g):
        a, b = res
        da = dg(g, b, 1, 1 - cb) if ca == 1 else dg(b, g, 1 - cb, 1)
        db = dg(a, g, 1 - ca, 0) if cb == 0 else dg(g, a, 0, 1 - ca)
        return da, db

    dot.defvjp(fwd, bwd)
    return dot


bdot = _dot_with_vjp(_dg)
xdot3 = _dot_with_vjp(_dg3)


def nn(a, b):
    return bdot(a, b, 1, 0)


def nt(a, b):
    return bdot(a, b, 1, 1)


def tn(a, b):
    return bdot(a, b, 0, 0)


def xdot(a, b):
    return xdot3(a, b, 1, 0)


def _sigmoid(x):
    return 1.0 / (1.0 + jnp.exp(-x))


def _softplus(x):
    return jnp.maximum(x, 0.0) + jnp.log(1.0 + jnp.exp(-jnp.abs(x)))


def _chunk_masks(rev):
    ii = lax.broadcasted_iota(jnp.int32, (CHUNK, CHUNK), 0)
    jj = lax.broadcasted_iota(jnp.int32, (CHUNK, CHUNK), 1)
    d = (ii - jj) * (1 - 2 * rev)
    return d >= 0, d > 0, ii == jj, (ii >> 3) == (jj >> 3)


def _each(f, *lists):
    return [f(*xs) for xs in zip(*lists)]


def _unit_triangular_inverse(a, eye, blockdiag):
    ident = eye.astype(F32)
    ad = _each(lambda x: jnp.where(blockdiag, x, 0.0), a)
    e = _each(lambda x, y: x - y, a, ad)
    dinv = _each(lambda x: ident - x, ad)
    p = _each(xdot, ad, ad)
    dinv = _each(lambda x, y: x + xdot(x, y), dinv, p)
    p = _each(xdot, p, p)
    dinv = _each(lambda x, y: x + xdot(x, y), dinv, p)
    g = _each(lambda x, y: -xdot(x, y), dinv, e)
    finv = _each(lambda x: ident + x, g)
    p = _each(xdot, g, g)
    finv = _each(lambda x, y: x + xdot(x, y), finv, p)
    p = _each(xdot, p, p)
    finv = _each(lambda x, y: x + xdot(x, y), finv, p)
    return _each(xdot, finv, dinv)


def _gdn_step(state, q, k, v, bb, gb, rev):
    causal, strict, eye, blockdiag = _chunk_masks(rev)
    lower = causal.astype(F32)
    ones = jnp.ones((CHUNK, CHUNK), F32)
    gcb = _each(lambda x: xdot(lower, x), gb)
    gcol = _each(lambda x: x[:, :CHUNK], gcb)
    grow = _each(lambda x: xdot(ones, jnp.where(eye, x, 0.0)), gcol)
    decay = _each(lambda x, y: jnp.exp(jnp.where(causal, x - y, NEG_INF)), gcol, grow)
    kb = _each(lambda x, y: x * y, k, bb)
    a = _each(lambda x, y, z: jnp.where(strict, nt(x, y) * z, 0.0), kb, k, decay)
    t = _unit_triangular_inverse(a, eye, blockdiag)
    egc = _each(jnp.exp, gcb)
    u = _each(lambda x, y, z: xdot(x, y * z), t, v, bb)
    w = _each(lambda x, y, z: xdot(x, y * z), t, kb, egc)
    qk = _each(lambda x, y, z: nt(x, y) * z, q, k, decay)
    glast = _each(lambda x: jnp.sum(x, axis=0, keepdims=True), gb)
    v_new = _each(lambda x, y, z: x - nn(y, z), u, w, state)
    o = _each(lambda x, y, z, p, r: nn(x * y, z) + nn(p, r), q, egc, state, qk, v_new)
    k_dec = _each(lambda x, y, z: x * jnp.exp(y - z), k, glast, gcb)
    state_new = _each(lambda x, y, z, p: x * jnp.exp(y) + tn(z, p), state, glast, k_dec, v_new)
    return state_new, o


def _gla_step(state_t, q, k, v, la, rev):
    causal, _, _, _ = _chunk_masks(rev)
    sign = 1 - 2 * rev
    b = xdot(causal.astype(F32), la)
    q = q * (B_DK ** -0.5)
    row = lax.broadcasted_iota(jnp.int32, (CHUNK, B_DK), 0)
    sub = row // GLA_SUB
    scores = None
    for blk in range(CHUNK // GLA_SUB):
        r_at = jnp.where(rev == 1, GLA_SUB * (blk + 1), GLA_SUB * blk - 1)
        r = jnp.sum(jnp.where(row == r_at, b, 0.0), axis=0, keepdims=True)
        q_blk = q * jnp.exp(jnp.where(sub == blk, b - r, NEG_INF))
        k_past = k * jnp.exp(jnp.where((sub - blk) * sign < 0, r - b, NEG_INF))
        part = xdot3(q_blk, k_past, 1, 1)
        scores = part if scores is None else scores + part
    shp = (GLA_SUB, GLA_SUB, B_DK)
    d3 = (lax.broadcasted_iota(jnp.int32, shp, 0) - lax.broadcasted_iota(jnp.int32, shp, 1)) * sign
    place_r = lax.broadcasted_iota(jnp.int32, (GLA_SUB, CHUNK), 0)
    place_c = lax.broadcasted_iota(jnp.int32, (GLA_SUB, CHUNK), 1)
    diag = []
    for blk in range(CHUNK // GLA_SUB):
        rows = slice(blk * GLA_SUB, (blk + 1) * GLA_SUB)
        qb, kb, bb = q[rows], k[rows], b[rows]
        dec = jnp.exp(jnp.where(d3 >= 0, bb[:, None, :] - bb[None, :, :], NEG_INF))
        pairs = jnp.sum(qb[:, None, :] * kb[None, :, :] * dec, axis=-1)
        diag.append(xdot(pairs, (place_c == place_r + blk * GLA_SUB).astype(F32)))
    scores = scores + jnp.concatenate(diag, axis=0)
    blast = jnp.sum(la, axis=0, keepdims=True)
    o = nt(q * jnp.exp(b), state_t) + nn(scores, v)
    state_new = jnp.exp(blast) * state_t + tn(v, k * jnp.exp(blast - b))
    return state_new, o


def _chunk_pos(d, m, n):
    return m + d * (n - 1 - 2 * m)


GLA_SUB = 16
GDN_HEADS_PER_STEP = 8
def gdn_rec_fwd(q, k, v, beta_b, g_b):
    s = q.shape[0]
    n = s // CHUNK

    hb = GDN_HEADS_PER_STEP
    wide = hb * LANES

    def body(q_ref, k_ref, v_ref, bb_ref, gb_ref, o_ref, st_ref, state):
        d = pl.program_id(0)

        @pl.when(pl.program_id(2) == 0)
        def _():
            state[...] = jnp.zeros_like(state)

        cols = [slice(hh * LANES, (hh + 1) * LANES) for hh in range(hb)]
        st = [state[hh] for hh in range(hb)]
        new, o = _gdn_step(st, *([r[:, c] for c in cols] for r in (q_ref, k_ref, v_ref, bb_ref, gb_ref)), d)
        for hh, c in enumerate(cols):
            st_ref[hh] = st[hh]
            state[hh] = new[hh]
            o_ref[:, c] = o[hh]

    blk = pl.BlockSpec((CHUNK, wide), lambda d, h, m: (_chunk_pos(d, m, n), h))
    gate = pl.BlockSpec((CHUNK, wide), lambda d, h, m: (_chunk_pos(d, m, n), d * (A_HEADS // hb) + h))
    return pl.pallas_call(
        body, name="gdn_rec_fwd", grid=(2, A_HEADS // hb, n),
        in_specs=[blk, blk, blk, gate, gate],
        out_specs=[pl.BlockSpec((None, CHUNK, wide), lambda d, h, m: (d, _chunk_pos(d, m, n), h)),
                   pl.BlockSpec((None, hb, None, A_DK, LANES), lambda d, h, m: (d, h, _chunk_pos(d, m, n), 0, 0))],
        out_shape=[jax.ShapeDtypeStruct((2, s, A_W), F32), jax.ShapeDtypeStruct((2, A_HEADS, n, A_DK, LANES), F32)],
        scratch_shapes=[pltpu.VMEM((hb, A_DK, LANES), F32)],
        compiler_params=_cparams(("arbitrary", "arbitrary", "arbitrary")),
    )(q, k, v, beta_b, g_b)


def gdn_rec_bwd(q, k, v, beta_b, g_b, states, do):
    s = q.shape[0]
    n = s // CHUNK

    hb = GDN_HEADS_PER_STEP
    wide = hb * LANES

    def body(q_ref, k_ref, v_ref, bb_ref, gb_ref, st_ref, do_ref, dq_ref, dk_ref, dv_ref, dbb_ref, dgb_ref, dstate):
        d = pl.program_id(0)

        @pl.when(pl.program_id(2) == 0)
        def _():
            dstate[...] = jnp.zeros_like(dstate)

        step = functools.partial(_gdn_step, rev=d)
        cols = [slice(hh * LANES, (hh + 1) * LANES) for hh in range(hb)]
        _, vjp = jax.vjp(step, [st_ref[hh] for hh in range(hb)],
                         *([r[:, c] for c in cols] for r in (q_ref, k_ref, v_ref, bb_ref, gb_ref)))
        grads = vjp(([dstate[hh] for hh in range(hb)], [do_ref[:, c] for c in cols]))
        for hh, c in enumerate(cols):
            dstate[hh], dq_ref[:, c], dk_ref[:, c], dv_ref[:, c], dbb_ref[:, c], dgb_ref[:, c] = (g[hh] for g in grads)

    pos = lambda d, m: _chunk_pos(1 - d, m, n)
    blk = pl.BlockSpec((CHUNK, wide), lambda d, h, m: (pos(d, m), h))
    gate = pl.BlockSpec((CHUNK, wide), lambda d, h, m: (pos(d, m), d * (A_HEADS // hb) + h))
    oblk = pl.BlockSpec((None, CHUNK, wide), lambda d, h, m: (d, pos(d, m), h))
    return pl.pallas_call(
        body, name="gdn_rec_bwd", grid=(2, A_HEADS // hb, n),
        in_specs=[blk, blk, blk, gate, gate,
                  pl.BlockSpec((None, hb, None, A_DK, LANES), lambda d, h, m: (d, h, pos(d, m), 0, 0)), blk],
        out_specs=[oblk, oblk, oblk, gate, gate],
        out_shape=[jax.ShapeDtypeStruct((2, s, A_W), F32)] * 3 + [jax.ShapeDtypeStruct(beta_b.shape, F32)] * 2,
        scratch_shapes=[pltpu.VMEM((hb, A_DK, LANES), F32)],
        compiler_params=_cparams(("arbitrary", "arbitrary", "arbitrary")),
    )(q, k, v, beta_b, g_b, states, do)


def gla_rec_fwd(proj, log_a):
    s = proj.shape[0]
    n = s // CHUNK

    def body(q_ref, k_ref, v_ref, la_ref, o_ref, st_ref, state):
        d = pl.program_id(0)

        @pl.when(pl.program_id(2) == 0)
        def _():
            state[...] = jnp.zeros_like(state)

        st = state[...]
        st_ref[...] = st
        new, o = _gla_step(st, q_ref[...], k_ref[...], v_ref[...], la_ref[...], d)
        state[...] = new
        o_ref[...] = o

    return pl.pallas_call(
        body, name="gla_rec_fwd", grid=(2, B_HEADS, n),
        in_specs=[pl.BlockSpec((CHUNK, B_DK), lambda d, h, m: (_chunk_pos(d, m, n), h)),
                  pl.BlockSpec((CHUNK, B_DK), lambda d, h, m: (_chunk_pos(d, m, n), B_KW // B_DK + h)),
                  pl.BlockSpec((CHUNK, B_DV), lambda d, h, m: (_chunk_pos(d, m, n), 2 * B_KW // B_DV + h)),
                  pl.BlockSpec((None, CHUNK, B_DK), lambda d, h, m: (d, _chunk_pos(d, m, n), h))],
        out_specs=[pl.BlockSpec((None, CHUNK, B_DV), lambda d, h, m: (d, _chunk_pos(d, m, n), h)),
                   pl.BlockSpec((None, None, None, B_DV, B_DK), lambda d, h, m: (d, h, _chunk_pos(d, m, n), 0, 0))],
        out_shape=[jax.ShapeDtypeStruct((2, s, B_VW), F32), jax.ShapeDtypeStruct((2, B_HEADS, n, B_DV, B_DK), F32)],
        scratch_shapes=[pltpu.VMEM((B_DV, B_DK), F32)],
        compiler_params=_cparams(("arbitrary", "arbitrary", "arbitrary")),
    )(proj, proj, proj, log_a)


def gla_rec_bwd(proj, log_a, states, do):
    s = proj.shape[0]
    n = s // CHUNK

    def body(q_ref, k_ref, v_ref, la_ref, st_ref, do_ref, dq_ref, dk_ref, dv_ref, dla_ref, dstate):
        d = pl.program_id(0)

        @pl.when(pl.program_id(2) == 0)
        def _():
            dstate[...] = jnp.zeros_like(dstate)

        step = functools.partial(_gla_step, rev=d)
        _, vjp = jax.vjp(step, st_ref[...], q_ref[...], k_ref[...], v_ref[...], la_ref[...])
        dst, dq, dk, dv, dla = vjp((dstate[...], do_ref[...]))
        dstate[...] = dst
        dq_ref[...] = dq
        dk_ref[...] = dk
        dv_ref[...] = dv
        dla_ref[...] = dla

    pos = lambda d, m: _chunk_pos(1 - d, m, n)
    kblk = pl.BlockSpec((None, CHUNK, B_DK), lambda d, h, m: (d, pos(d, m), h))
    return pl.pallas_call(
        body, name="gla_rec_bwd", grid=(2, B_HEADS, n),
        in_specs=[pl.BlockSpec((CHUNK, B_DK), lambda d, h, m: (pos(d, m), h)),
                  pl.BlockSpec((CHUNK, B_DK), lambda d, h, m: (pos(d, m), B_KW // B_DK + h)),
                  pl.BlockSpec((CHUNK, B_DV), lambda d, h, m: (pos(d, m), 2 * B_KW // B_DV + h)),
                  kblk,
                  pl.BlockSpec((None, None, None, B_DV, B_DK), lambda d, h, m: (d, h, pos(d, m), 0, 0)),
                  pl.BlockSpec((CHUNK, B_DV), lambda d, h, m: (pos(d, m), h))],
        out_specs=[kblk, kblk, pl.BlockSpec((None, CHUNK, B_DV), lambda d, h, m: (d, pos(d, m), h)), kblk],
        out_shape=[jax.ShapeDtypeStruct((2, s, B_KW), F32), jax.ShapeDtypeStruct((2, s, B_KW), F32),
                   jax.ShapeDtypeStruct((2, s, B_VW), F32), jax.ShapeDtypeStruct((2, s, B_KW), F32)],
        scratch_shapes=[pltpu.VMEM((B_DV, B_DK), F32)],
        compiler_params=_cparams(("arbitrary", "arbitrary", "arbitrary")),
    )(proj, proj, proj, log_a, states, do)


MM_TILE_OUT = 1024
MM_TILE_K = 1024


def _tile(n, pref):
    return pref if n % pref == 0 else n


def mm(a, b, mode="nn", act=None, epi=None, extra=None, alpha=1.0, name="mm"):
    if mode == "tn":
        kk, m = a.shape
    else:
        m, kk = a.shape
    nn_ = b.shape[0] if mode == "nt" else b.shape[1]
    tm, tn_, tk = _tile(m, MM_TILE_OUT), _tile(nn_, MM_TILE_OUT), _tile(kk, MM_TILE_K)
    nk = kk // tk
    ca, cb = {"nn": (1, 0), "nt": (1, 1), "tn": (0, 0)}[mode]

    def body(*refs):
        if epi is None:
            a_ref, b_ref, o_ref = refs
        else:
            a_ref, b_ref, e_ref, o_ref = refs
        kstep = pl.program_id(2)
        at = a_ref[...]
        if act == "sqrelu":
            at = jnp.square(jnp.maximum(at, 0.0))
        part = _dg(at, b_ref[...], ca, cb)

        @pl.when(kstep == 0)
        def _():
            o_ref[...] = part

        @pl.when(kstep > 0)
        def _():
            o_ref[...] += part

        if epi is not None:
            @pl.when(kstep == nk - 1)
            def _():
                if epi == "dsqrelu":
                    o_ref[...] = o_ref[...] * (2.0 * jnp.maximum(e_ref[...], 0.0))
                else:
                    o_ref[...] = o_ref[...] + alpha * e_ref[...]

    a_spec = pl.BlockSpec((tk, tm), lambda i, j, k: (k, i)) if mode == "tn" else pl.BlockSpec((tm, tk), lambda i, j, k: (i, k))
    b_spec = pl.BlockSpec((tn_, tk), lambda i, j, k: (j, k)) if mode == "nt" else pl.BlockSpec((tk, tn_), lambda i, j, k: (k, j))
    o_spec = pl.BlockSpec((tm, tn_), lambda i, j, k: (i, j))
    ins, specs = [a, b], [a_spec, b_spec]
    if epi is not None:
        ins.append(extra)
        specs.append(o_spec)
    return pl.pallas_call(
        body, name=name, grid=(m // tm, nn_ // tn_, nk), in_specs=specs, out_specs=o_spec,
        out_shape=jax.ShapeDtypeStruct((m, nn_), F32),
        compiler_params=_cparams(("parallel", "parallel", "arbitrary")),
    )(*ins)


ROWS = 256


def _ln_core(x, m, g, b):
    r = ALPHA * x + m
    mu = jnp.mean(r, axis=-1, keepdims=True)
    xc = r - mu
    var = jnp.mean(xc * xc, axis=-1, keepdims=True)
    rstd = lax.rsqrt(var + LN_EPS)
    xhat = xc * rstd
    return xhat * g + b, xhat, rstd


def ln_fwd(x, m, g, b):
    s, dm = x.shape

    def body(x_ref, m_ref, g_ref, b_ref, o_ref):
        o_ref[...] = _ln_core(x_ref[...], m_ref[...], g_ref[...], b_ref[...])[0]

    row = pl.BlockSpec((ROWS, dm), lambda i: (i, 0))
    vec = pl.BlockSpec((1, dm), lambda i: (0, 0))
    return pl.pallas_call(body, name="ln_fwd", grid=(s // ROWS,), in_specs=[row, row, vec, vec], out_specs=row,
                          out_shape=jax.ShapeDtypeStruct((s, dm), F32), compiler_params=_cparams(("parallel",)))(x, m, g, b)


def ln_bwd(x, m, g, dy):
    s, dm = x.shape

    def body(x_ref, m_ref, g_ref, dy_ref, dr_ref, dg_ref, db_ref):
        gv = g_ref[...]
        _, xhat, rstd = _ln_core(x_ref[...], m_ref[...], gv, jnp.zeros_like(gv))
        dy = dy_ref[...]
        dxh = dy * gv
        dr_ref[...] = rstd * (dxh - jnp.mean(dxh, axis=-1, keepdims=True)
                              - xhat * jnp.mean(dxh * xhat, axis=-1, keepdims=True))

        @pl.when(pl.program_id(0) == 0)
        def _():
            dg_ref[...] = jnp.zeros_like(dg_ref)
            db_ref[...] = jnp.zeros_like(db_ref)

        dg_ref[...] += jnp.sum(dy * xhat, axis=0, keepdims=True)
        db_ref[...] += jnp.sum(dy, axis=0, keepdims=True)

    row = pl.BlockSpec((ROWS, dm), lambda i: (i, 0))
    vec = pl.BlockSpec((1, dm), lambda i: (0, 0))
    return pl.pallas_call(body, name="ln_bwd", grid=(s // ROWS,), in_specs=[row, row, vec, row], out_specs=[row, vec, vec],
                          out_shape=[jax.ShapeDtypeStruct((s, dm), F32), jax.ShapeDtypeStruct((1, dm), F32),
                                     jax.ShapeDtypeStruct((1, dm), F32)],
                          compiler_params=_cparams(("arbitrary",)))(x, m, g, dy)


def loss_head(y, target):
    s, dm = y.shape

    def body(y_ref, t_ref, dy_ref, l_ref):
        e = y_ref[...] - t_ref[...]
        dy_ref[...] = e * (1.0 / dm)

        @pl.when(pl.program_id(0) == 0)
        def _():
            l_ref[...] = jnp.zeros_like(l_ref)

        col = jnp.sum(e * e, axis=0, keepdims=True) * (0.5 / dm)
        acc = col[:, :LANES]
        for c in range(1, dm // LANES):
            acc = acc + col[:, c * LANES:(c + 1) * LANES]
        l_ref[...] += acc

    row = pl.BlockSpec((ROWS, dm), lambda i: (i, 0))
    return pl.pallas_call(body, name="loss_head", grid=(s // ROWS,), in_specs=[row, row],
                          out_specs=[row, pl.BlockSpec((1, LANES), lambda i: (0, 0))],
                          out_shape=[jax.ShapeDtypeStruct((s, dm), F32), jax.ShapeDtypeStruct((1, LANES), F32)],
                          compiler_params=_cparams(("arbitrary",)))(y, target)


def _shift_rows_impl(x, d):
    n = x.shape[0]
    if d == 0:
        return x
    t = lax.broadcasted_iota(jnp.int32, x.shape, 0)
    return jnp.where((t + d >= 0) & (t + d < n), pltpu.roll(x, (-d) % n, 0), 0.0)


@functools.partial(jax.custom_vjp, nondiff_argnums=(1,))
def _shift_rows(x, d):
    return _shift_rows_impl(x, d)


_shift_rows.defvjp(lambda x, d: (_shift_rows_impl(x, d), None), lambda d, _, g: (_shift_rows_impl(g, -d),))


def _gdn_pre_fn(u, w, kind):
    rows = lax.broadcasted_iota(jnp.int32, w.shape, 0)
    c = None
    for tap in range(A_CONV):
        w_tap = jnp.sum(jnp.where(rows == tap, w, 0.0), axis=0, keepdims=True)
        term = _shift_rows(u, tap - A_CONV // 2) * w_tap
        c = term if c is None else c + term
    y = c * _sigmoid(c)
    if kind == "v":
        return y
    y = y * lax.rsqrt(jnp.sum(y * y, axis=-1, keepdims=True) + L2_EPS)
    return y * (A_DK ** -0.5) if kind == "q" else y


_KIND_OFF = {"q": 0, "k": A_HEADS, "v": 2 * A_HEADS}


def gdn_pre(proj, conv_w, kind):
    s = proj.shape[0]
    off = _KIND_OFF[kind]

    def body(u_ref, w_ref, o_ref):
        o_ref[...] = _gdn_pre_fn(u_ref[...], w_ref[...], kind)

    return pl.pallas_call(
        body, name="gdn_pre_" + kind, grid=(A_HEADS,),
        in_specs=[pl.BlockSpec((s, LANES), lambda h: (0, off + h)), pl.BlockSpec((A_CONV, LANES), lambda h: (0, off + h))],
        out_specs=pl.BlockSpec((s, LANES), lambda h: (0, h)),
        out_shape=jax.ShapeDtypeStruct((s, A_W), F32), compiler_params=_cparams(("parallel",)))(proj, conv_w)


def gdn_pre_bwd(proj, conv_w, dt2, kind):
    s = proj.shape[0]
    off = _KIND_OFF[kind]

    def body(u_ref, w_ref, d0_ref, d1_ref, du_ref, dw_ref):
        _, vjp = jax.vjp(functools.partial(_gdn_pre_fn, kind=kind), u_ref[...], w_ref[...])
        du, dw = vjp(d0_ref[...] + d1_ref[...])
        du_ref[...] = du
        dw_ref[...] = dw

    return pl.pallas_call(
        body, name="gdn_pre_bwd_" + kind, grid=(A_HEADS,),
        in_specs=[pl.BlockSpec((s, LANES), lambda h: (0, off + h)), pl.BlockSpec((A_CONV, LANES), lambda h: (0, off + h)),
                  pl.BlockSpec((None, s, LANES), lambda h: (0, 0, h)), pl.BlockSpec((None, s, LANES), lambda h: (1, 0, h))],
        out_specs=[pl.BlockSpec((s, LANES), lambda h: (0, h)), pl.BlockSpec((A_CONV, LANES), lambda h: (0, h))],
        out_shape=[jax.ShapeDtypeStruct((s, A_W), F32), jax.ShapeDtypeStruct((A_CONV, A_W), F32)],
        compiler_params=_cparams(("parallel",)))(proj, conv_w, dt2, dt2)


N_GATE = 2 * A_HEADS


def _gdn_gates_fn(ba, alog_row, dt_row):
    r = lax.broadcasted_iota(jnp.int32, (LANES, N_GATE * LANES), 0)
    c = lax.broadcasted_iota(jnp.int32, (LANES, N_GATE * LANES), 1) >> 7
    beta_b = xdot(_sigmoid(ba), (r == c).astype(F32))
    g = -(jnp.exp(alog_row) * _softplus(ba + dt_row))
    g_b = xdot(g, (r == c + N_GATE).astype(F32))
    return beta_b, g_b


def gdn_gates(ba, alog_row, dt_row):
    s = ba.shape[0]

    def body(ba_ref, al_ref, dt_ref, bb_ref, gb_ref):
        bb_ref[...], gb_ref[...] = _gdn_gates_fn(ba_ref[...], al_ref[...], dt_ref[...])

    row = pl.BlockSpec((ROWS, LANES), lambda i: (i, 0))
    vec = pl.BlockSpec((1, LANES), lambda i: (0, 0))
    wide = pl.BlockSpec((ROWS, N_GATE * LANES), lambda i: (i, 0))
    return pl.pallas_call(body, name="gdn_gates", grid=(s // ROWS,), in_specs=[row, vec, vec], out_specs=[wide, wide],
                          out_shape=[jax.ShapeDtypeStruct((s, N_GATE * LANES), F32)] * 2,
                          compiler_params=_cparams(("parallel",)))(ba, alog_row, dt_row)


def gdn_gates_bwd(ba, alog_row, dt_row, dbeta_b, dg_b):
    s = ba.shape[0]

    def body(ba_ref, al_ref, dt_ref, dbb_ref, dgb_ref, dba_ref, dal_ref, ddt_ref):
        _, vjp = jax.vjp(_gdn_gates_fn, ba_ref[...], al_ref[...], dt_ref[...])
        dba, dal, ddt = vjp((dbb_ref[...], dgb_ref[...]))
        dba_ref[...] = dba

        @pl.when(pl.program_id(0) == 0)
        def _():
            dal_ref[...] = jnp.zeros_like(dal_ref)
            ddt_ref[...] = jnp.zeros_like(ddt_ref)

        dal_ref[...] += dal
        ddt_ref[...] += ddt

    row = pl.BlockSpec((ROWS, LANES), lambda i: (i, 0))
    vec = pl.BlockSpec((1, LANES), lambda i: (0, 0))
    wide = pl.BlockSpec((ROWS, N_GATE * LANES), lambda i: (i, 0))
    return pl.pallas_call(body, name="gdn_gates_bwd", grid=(s // ROWS,), in_specs=[row, vec, vec, wide, wide],
                          out_specs=[row, vec, vec],
                          out_shape=[jax.ShapeDtypeStruct((s, LANES), F32), jax.ShapeDtypeStruct((1, LANES), F32),
                                     jax.ShapeDtypeStruct((1, LANES), F32)],
                          compiler_params=_cparams(("arbitrary",)))(ba, alog_row, dt_row, dbeta_b, dg_b)


def _post_fn(o, z, g):
    y = o * lax.rsqrt(jnp.mean(o * o, axis=-1, keepdims=True) + RMS_EPS) * g
    return y * (z * _sigmoid(z))


def mixer_post(o2, proj, norm_g, width, gate_off, name):
    s = o2.shape[1]
    nh = o2.shape[2] // width

    def body(o0_ref, o1_ref, z_ref, g_ref, y_ref):
        y_ref[...] = _post_fn(o0_ref[...] + o1_ref[...], z_ref[...], g_ref[...])

    ospec = lambda d: pl.BlockSpec((None, ROWS, width), lambda i, h: (d, i, h))
    return pl.pallas_call(
        body, name=name, grid=(s // ROWS, nh),
        in_specs=[ospec(0), ospec(1), pl.BlockSpec((ROWS, width), lambda i, h: (i, gate_off + h)),
                  pl.BlockSpec((1, width), lambda i, h: (0, 0))],
        out_specs=pl.BlockSpec((ROWS, width), lambda i, h: (i, h)),
        out_shape=jax.ShapeDtypeStruct((s, o2.shape[2]), F32),
        compiler_params=_cparams(("parallel", "parallel")))(o2, o2, proj, norm_g)


def mixer_post_bwd(o2, proj, norm_g, dy, width, gate_off, name):
    s = o2.shape[1]
    nh = o2.shape[2] // width

    def body(o0_ref, o1_ref, z_ref, g_ref, dy_ref, do_ref, dz_ref, dg_ref):
        _, vjp = jax.vjp(_post_fn, o0_ref[...] + o1_ref[...], z_ref[...], g_ref[...])
        do, dz, dg = vjp(dy_ref[...])
        do_ref[...] = do
        dz_ref[...] = dz

        @pl.when((pl.program_id(0) == 0) & (pl.program_id(1) == 0))
        def _():
            dg_ref[...] = jnp.zeros_like(dg_ref)

        dg_ref[...] += dg

    ospec = lambda d: pl.BlockSpec((None, ROWS, width), lambda i, h: (d, i, h))
    blk = pl.BlockSpec((ROWS, width), lambda i, h: (i, h))
    vec = pl.BlockSpec((1, width), lambda i, h: (0, 0))
    return pl.pallas_call(
        body, name=name, grid=(s // ROWS, nh),
        in_specs=[ospec(0), ospec(1), pl.BlockSpec((ROWS, width), lambda i, h: (i, gate_off + h)), vec, blk],
        out_specs=[blk, blk, vec],
        out_shape=[jax.ShapeDtypeStruct((s, o2.shape[2]), F32)] * 2 + [jax.ShapeDtypeStruct((1, width), F32)],
        compiler_params=_cparams(("arbitrary", "arbitrary")))(o2, o2, proj, norm_g, dy)


def _log_gate(z):
    return (jnp.minimum(z, 0.0) - jnp.log(1.0 + jnp.exp(-jnp.abs(z)))) * (1.0 / B_TAU)


def gla_gate(gl, w2, gb):
    s = gl.shape[0]

    def body(gl_ref, w_ref, b_ref, o_ref):
        for n in range(2):
            o_ref[n] = _log_gate(nn(gl_ref[...], w_ref[n]) + b_ref[n])

    full = lambda shp: pl.BlockSpec(shp, lambda i: (0,) * len(shp))
    return pl.pallas_call(
        body, name="gla_gate", grid=(s // ROWS,),
        in_specs=[pl.BlockSpec((ROWS, LANES), lambda i: (i, 0)), full(w2.shape), full(gb.shape)],
        out_specs=pl.BlockSpec((2, ROWS, B_KW), lambda i: (0, i, 0)),
        out_shape=jax.ShapeDtypeStruct((2, s, B_KW), F32), compiler_params=_cparams(("parallel",)))(gl, w2, gb)


def gla_gate_bwd(gl, w2, gb, dla):
    s = gl.shape[0]

    def body(gl_ref, w_ref, b_ref, dla_ref, dgl_ref, dz_ref, db0_ref, db1_ref):
        @pl.when(pl.program_id(0) == 0)
        def _():
            db0_ref[...] = jnp.zeros_like(db0_ref)
            db1_ref[...] = jnp.zeros_like(db1_ref)

        dgl = None
        for n, db_ref in enumerate((db0_ref, db1_ref)):
            _, vjp = jax.vjp(_log_gate, nn(gl_ref[...], w_ref[n]) + b_ref[n])
            dz, = vjp(dla_ref[n])
            dz_ref[n] = dz
            db_ref[...] += jnp.sum(dz, axis=0, keepdims=True)
            part = nt(dz, w_ref[n])
            dgl = part if dgl is None else dgl + part
        dgl_ref[...] = dgl

    full = lambda shp: pl.BlockSpec(shp, lambda i: (0,) * len(shp))
    row = pl.BlockSpec((ROWS, LANES), lambda i: (i, 0))
    wide = pl.BlockSpec((2, ROWS, B_KW), lambda i: (0, i, 0))
    vec = pl.BlockSpec((1, B_KW), lambda i: (0, 0))
    return pl.pallas_call(
        body, name="gla_gate_bwd", grid=(s // ROWS,),
        in_specs=[row, full(w2.shape), full(gb.shape), wide],
        out_specs=[row, wide, vec, vec],
        out_shape=[jax.ShapeDtypeStruct((s, LANES), F32), jax.ShapeDtypeStruct((2, s, B_KW), F32),
                   jax.ShapeDtypeStruct((1, B_KW), F32), jax.ShapeDtypeStruct((1, B_KW), F32)],
        compiler_params=_cparams(("arbitrary",)))(gl, w2, gb, dla)


PACK_TILE = 512


def cast_into_slot(x, chip):
    r, c = x.shape

    def body(chip_ref, x_ref, o_ref):
        o_ref[...] = x_ref[...].astype(BF16)

    return pl.pallas_call(
        body, name="cast_into_slot",
        grid_spec=pltpu.PrefetchScalarGridSpec(
            num_scalar_prefetch=1, grid=(r // PACK_TILE,),
            in_specs=[pl.BlockSpec((PACK_TILE, c), lambda i, chip_ref: (i, 0))],
            out_specs=pl.BlockSpec((None, PACK_TILE, c), lambda i, chip_ref: (chip_ref[0], i, 0))),
        out_shape=jax.ShapeDtypeStruct((4, r, c), BF16), compiler_params=_cparams(("parallel",)))(chip, x)


def sum_received(chip_sum, recv, chip, core):
    _, h, c = chip_sum.shape
    tr = _tile(h, PACK_TILE)
    nblk = h // tr

    def body(chip_ref, core_ref, own_ref, r_ref, o_ref):
        acc = r_ref[0].astype(F32)
        for k in range(1, 3):
            acc = acc + r_ref[k].astype(F32)
        o_ref[...] = acc + own_ref[...].astype(F32)

    return pl.pallas_call(
        body, name="sum_received",
        grid_spec=pltpu.PrefetchScalarGridSpec(
            num_scalar_prefetch=2, grid=(nblk,),
            in_specs=[pl.BlockSpec((None, tr, c), lambda i, chip_ref, core_ref: (chip_ref[0], i, 0)),
                      pl.BlockSpec((3, tr, c), lambda i, chip_ref, core_ref: (0, i, 0))],
            out_specs=pl.BlockSpec((tr, c), lambda i, chip_ref, core_ref: (core_ref[0] * nblk + i, 0))),
        out_shape=jax.ShapeDtypeStruct((2 * h, c), F32), compiler_params=_cparams(("parallel",)))(chip, core, chip_sum, recv)


def sum_slots(x, name):
    n, r, c = x.shape
    tr = _tile(r, PACK_TILE)

    def body(x_ref, o_ref):
        acc = x_ref[0].astype(F32)
        for k in range(1, n):
            acc = acc + x_ref[k].astype(F32)
        o_ref[...] = acc

    return pl.pallas_call(body, name=name, grid=(r // tr,), in_specs=[pl.BlockSpec((n, tr, c), lambda i: (0, i, 0))],
                          out_specs=pl.BlockSpec((tr, c), lambda i: (i, 0)),
                          out_shape=jax.ShapeDtypeStruct((r, c), F32), compiler_params=_cparams(("parallel",)))(x)


def add_sibling_half(gpack, theirs, core):
    n, r, c = gpack.shape
    half_rows = r // 2
    tr = _tile(half_rows, PACK_TILE)
    nblk = half_rows // tr

    def body(core_ref, g_ref, t_ref, o_ref):
        o_ref[...] = (g_ref[...] + t_ref[...]).astype(BF16)

    blk = pl.BlockSpec((None, tr, c), lambda s, i, core_ref: (s, i, 0))
    return pl.pallas_call(
        body, name="add_sibling_half",
        grid_spec=pltpu.PrefetchScalarGridSpec(
            num_scalar_prefetch=1, grid=(n, nblk),
            in_specs=[pl.BlockSpec((None, tr, c), lambda s, i, core_ref: (s, core_ref[0] * nblk + i, 0)), blk],
            out_specs=blk),
        out_shape=jax.ShapeDtypeStruct((n, half_rows, c), BF16),
        compiler_params=_cparams(("parallel", "parallel")))(core, gpack, theirs)


def adamw(w, m, v, grads, g_row_off, name):
    r, c = w.shape
    tr = next(t for t in (PACK_TILE, r) if r % t == 0 and g_row_off % t == 0)
    ob = g_row_off // tr
    ng = len(grads)

    def body(*refs):
        w_ref, m_ref, v_ref = refs[:3]
        g_refs = refs[3:3 + ng]
        g_ref, d_ref, nm_ref, nv_ref = refs[3 + ng:]
        g = g_refs[0][...]
        for gr in g_refs[1:]:
            g = g + gr[...]
        m_new = ADAM_B1 * m_ref[...] + (1.0 - ADAM_B1) * g
        v_new = ADAM_B2 * v_ref[...] + (1.0 - ADAM_B2) * jnp.square(g)
        m_hat = m_new / (1.0 - ADAM_B1 ** ADAM_STEP)
        v_hat = v_new / (1.0 - ADAM_B2 ** ADAM_STEP)
        g_ref[...] = g
        d_ref[...] = -ADAM_LR * (m_hat / (jnp.sqrt(v_hat) + ADAM_EPS) + ADAM_WD * w_ref[...])
        nm_ref[...] = m_new
        nv_ref[...] = v_new

    blk = pl.BlockSpec((tr, c), lambda i: (i, 0))
    gblk = pl.BlockSpec((tr, c), lambda i: (i + ob, 0))
    return pl.pallas_call(body, name=name, grid=(r // tr,), in_specs=[blk, blk, blk] + [gblk] * ng, out_specs=[blk] * 4,
                          out_shape=[jax.ShapeDtypeStruct((r, c), F32)] * 4,
                          compiler_params=_cparams(("parallel",)))(w, m, v, *grads)


MESH = pl.DeviceIdType.MESH
HBM = pl.BlockSpec(memory_space=pl.ANY)
CHIP_FLIPS = ((1, 0), (0, 1), (1, 1))


def _place():
    return lax.axis_index("x"), lax.axis_index("y"), lax.axis_index("c")


def allgather_chips(buf):
    _, r, c = buf.shape
    half_rows = r // 2

    def body(_, out_ref, send_sems, recv_sems):
        x, y, cc = _place()
        half = pl.ds(cc * half_rows, half_rows)
        other = pl.ds((1 - cc) * half_rows, half_rows)

        def copy(k, rows, to):
            return pltpu.make_async_remote_copy(src_ref=rows, dst_ref=rows, send_sem=send_sems.at[k],
                                                recv_sem=recv_sems.at[k], device_id=to, device_id_type=MESH)

        chips = [((1 - x if fx else x), (1 - y if fy else y)) for fx, fy in CHIP_FLIPS]
        first = [copy(k, out_ref.at[2 * x + y, half], (px, py, cc)) for k, (px, py) in enumerate(chips)]
        for cp in first:
            cp.start()
        passed = []
        for k, (px, py) in enumerate(chips):
            landed = out_ref.at[2 * px + py, half]
            copy(k, landed, (px, py, cc)).wait_recv()
            passed.append(copy(3 + k, landed, (x, y, 1 - cc)))
            passed[-1].start()
        for k, (px, py) in enumerate(chips):
            copy(3 + k, out_ref.at[2 * px + py, other], (x, y, 1 - cc)).wait_recv()
        for cp in first + passed:
            cp.wait_send()

    return pl.pallas_call(
        body, name="allgather_chips", in_specs=[HBM], out_specs=HBM, input_output_aliases={0: 0},
        out_shape=jax.ShapeDtypeStruct(buf.shape, buf.dtype),
        scratch_shapes=[pltpu.SemaphoreType.DMA((6,)), pltpu.SemaphoreType.DMA((6,))],
    )(buf)


def scatter_chips(gpack):
    _, r, c = gpack.shape

    def body(src_ref, out_ref, send_sems, recv_sems):
        x, y, cc = _place()
        sends = []
        for k, (fx, fy) in enumerate(CHIP_FLIPS):
            px, py = (1 - x if fx else x), (1 - y if fy else y)
            sends.append(pltpu.make_async_remote_copy(
                src_ref=src_ref.at[2 * px + py], dst_ref=out_ref.at[k], send_sem=send_sems.at[k], recv_sem=recv_sems.at[k],
                device_id=(px, py, cc), device_id_type=MESH))
        for cp in sends:
            cp.start()
        for cp in sends:
            cp.wait_recv()
        for cp in sends:
            cp.wait_send()

    return pl.pallas_call(
        body, name="scatter_chips", in_specs=[HBM], out_specs=HBM,
        out_shape=jax.ShapeDtypeStruct((3, r, c), gpack.dtype),
        scratch_shapes=[pltpu.SemaphoreType.DMA((3,)), pltpu.SemaphoreType.DMA((3,))],
    )(gpack)


def sibling_halves(gpack):
    n, r, c = gpack.shape
    half_rows = r // 2

    def body(src_ref, out_ref, send_sem, recv_sem):
        x, y, cc = _place()
        cp = pltpu.make_async_remote_copy(
            src_ref=src_ref.at[:, pl.ds((1 - cc) * half_rows, half_rows)], dst_ref=out_ref, send_sem=send_sem,
            recv_sem=recv_sem, device_id=(x, y, 1 - cc), device_id_type=MESH)
        cp.start()
        cp.wait()

    return pl.pallas_call(
        body, name="sibling_halves", in_specs=[HBM], out_specs=HBM,
        out_shape=jax.ShapeDtypeStruct((n, half_rows, c), gpack.dtype),
        scratch_shapes=[pltpu.SemaphoreType.DMA, pltpu.SemaphoreType.DMA],
    )(gpack)


def join_halves(buf):
    r, c = buf.shape
    half_rows = r // 2

    def body(_, out_ref, send_sem, recv_sem):
        x, y, cc = _place()
        half = out_ref.at[pl.ds(cc * half_rows, half_rows)]
        other = out_ref.at[pl.ds((1 - cc) * half_rows, half_rows)]
        send = pltpu.make_async_remote_copy(src_ref=half, dst_ref=half, send_sem=send_sem, recv_sem=recv_sem,
                                            device_id=(x, y, 1 - cc), device_id_type=MESH)
        send.start()
        pltpu.make_async_remote_copy(src_ref=other, dst_ref=other, send_sem=send_sem, recv_sem=recv_sem,
                                     device_id=(x, y, 1 - cc), device_id_type=MESH).wait_recv()
        send.wait_send()

    return pl.pallas_call(
        body, name="join_halves", in_specs=[HBM], out_specs=HBM, input_output_aliases={0: 0},
        out_shape=jax.ShapeDtypeStruct(buf.shape, buf.dtype),
        scratch_shapes=[pltpu.SemaphoreType.DMA, pltpu.SemaphoreType.DMA],
    )(buf)


def exchange_all(v, name):
    r, c = v.shape

    def body(v_ref, out_ref, send_sems, recv_sems):
        x, y, cc = _place()
        out_ref[4 * x + 2 * y + cc] = v_ref[...]
        sends, recvs = [], []
        for k in range(1, 8):
            px = 1 - x if k & 4 else x
            py = 1 - y if k & 2 else y
            pc = 1 - cc if k & 1 else cc
            sends.append(pltpu.make_async_remote_copy(
                src_ref=v_ref, dst_ref=out_ref.at[4 * x + 2 * y + cc], send_sem=send_sems.at[k - 1],
                recv_sem=recv_sems.at[k - 1], device_id=(px, py, pc), device_id_type=MESH))
            recvs.append(pltpu.make_async_remote_copy(
                src_ref=v_ref, dst_ref=out_ref.at[4 * px + 2 * py + pc], send_sem=send_sems.at[k - 1],
                recv_sem=recv_sems.at[k - 1], device_id=(px, py, pc), device_id_type=MESH))
        for cp in sends:
            cp.start()
        for cp in recvs:
            cp.wait_recv()
        for cp in sends:
            cp.wait_send()

    vm = pl.BlockSpec(memory_space=pltpu.VMEM)
    return pl.pallas_call(
        body, name=name, in_specs=[vm], out_specs=vm, out_shape=jax.ShapeDtypeStruct((8, r, c), v.dtype),
        scratch_shapes=[pltpu.SemaphoreType.DMA((7,)), pltpu.SemaphoreType.DMA((7,))],
        compiler_params=pltpu.CompilerParams(vmem_limit_bytes=VMEM_LIMIT),
    )(v)


def _pack_rows(arrays, rows, width):
    flat = jnp.concatenate([a.reshape(-1) for a in arrays])
    return jnp.pad(flat, (0, rows * width - flat.shape[0])).reshape(rows, width)


def _unpack_rows(pack, shapes):
    flat = pack.reshape(-1)
    out, off = [], 0
    for shp in shapes:
        n = math.prod(shp)
        out.append(flat[off:off + n].reshape(shp))
        off += n
    return out


def _rows_for(shapes, width, mult=8):
    n = sum(math.prod(s) for s in shapes)
    return -(-n // (width * mult)) * mult


def _gdn_fwd(x, p):
    proj = mm(x, p["w_main"], name="gdn_proj")
    ba = mm(x, p["w_gate"], name="gdn_proj_gate")
    q, k, v = (gdn_pre(proj, p["conv"], kind) for kind in "qkv")
    beta_b, g_b = gdn_gates(ba, p["alog_row"], p["dt_row"])
    o2, st = gdn_rec_fwd(q, k, v, beta_b, g_b)
    y = mixer_post(o2, proj, p["norm_g"], A_DK, 3 * A_HEADS, "gdn_post")
    m = mm(y, p["w_out"], name="gdn_out")
    return m, (x, proj, ba, q, k, v, beta_b, g_b, o2, st, y)


def _gdn_bwd(saved, p, dm):
    x, proj, ba, q, k, v, beta_b, g_b, o2, st, y = saved
    d_w_out = mm(y, dm, "tn", name="gdn_dw_out")
    dy = mm(dm, p["w_out"], "nt", name="gdn_dy")
    do, dz, d_norm_g = mixer_post_bwd(o2, proj, p["norm_g"], dy, A_DK, 3 * A_HEADS, "gdn_post_bwd")
    dq2, dk2, dv2, dbb, dgb = gdn_rec_bwd(q, k, v, beta_b, g_b, st, do)
    dba, d_alog_row, d_dt_row = gdn_gates_bwd(ba, p["alog_row"], p["dt_row"], dbb, dgb)
    du, dconv = zip(*(gdn_pre_bwd(proj, p["conv"], d2, kind) for d2, kind in ((dq2, "q"), (dk2, "k"), (dv2, "v"))))
    dproj = jnp.concatenate(list(du) + [dz], axis=1)
    d_w_main = mm(x, dproj, "tn", name="gdn_dw_main")
    d_w_gate = mm(x, dba, "tn", name="gdn_dw_gate")
    dx = mm(dba, p["w_gate"], "nt", epi="add", extra=dm, alpha=ALPHA, name="gdn_dx_gate")
    dx = mm(dproj, p["w_main"], "nt", epi="add", extra=dx, name="gdn_dx")
    grads = dict(w_in=jnp.concatenate([d_w_main, d_w_gate[:, :2 * N_GATE]], axis=1), conv=jnp.concatenate(dconv, axis=1),
                 alog=d_alog_row[0, N_GATE:2 * N_GATE].reshape(2, A_HEADS), dt=d_dt_row[0, N_GATE:2 * N_GATE].reshape(2, A_HEADS),
                 norm_g=d_norm_g[0], w_out=d_w_out)
    return dx, grads


def _gla_fwd(x, p):
    proj = mm(x, p["w_main"], name="gla_proj")
    gl = mm(x, p["w_gate"], name="gla_proj_gate")
    log_a = gla_gate(gl, p["w2"], p["gate_b"])
    o2, st = gla_rec_fwd(proj, log_a)
    y = mixer_post(o2, proj, p["norm_g"], B_DV, (2 * B_KW + B_VW) // B_DV, "gla_post")
    m = mm(y, p["w_out"], name="gla_out")
    return m, (x, proj, gl, log_a, o2, st, y)


def _gla_bwd(saved, p, dm):
    x, proj, gl, log_a, o2, st, y = saved
    d_w_out = mm(y, dm, "tn", name="gla_dw_out")
    dy = mm(dm, p["w_out"], "nt", name="gla_dy")
    do, dr, d_norm_g = mixer_post_bwd(o2, proj, p["norm_g"], dy, B_DV, (2 * B_KW + B_VW) // B_DV, "gla_post_bwd")
    dq2, dk2, dv2, dla = gla_rec_bwd(proj, log_a, st, do)
    dgl, dz, d_b0, d_b1 = gla_gate_bwd(gl, p["w2"], p["gate_b"], dla)
    d_w2 = [mm(gl, dz[n], "tn", name="gla_dw_gate_w2") for n in range(2)]
    dproj = jnp.concatenate([dq2[0] + dq2[1], dk2[0] + dk2[1], dv2[0] + dv2[1], dr], axis=1)
    d_w_main = mm(x, dproj, "tn", name="gla_dw_main")
    d_w_gate = mm(x, dgl, "tn", name="gla_dw_gate")
    dx = mm(dgl, p["w_gate"], "nt", epi="add", extra=dm, alpha=ALPHA, name="gla_dx_gate")
    dx = mm(dproj, p["w_main"], "nt", epi="add", extra=dx, name="gla_dx")
    grads = dict(w_in=jnp.concatenate([d_w_main, d_w_gate[:, :2 * B_RANK]], axis=1),
                 gate_w2=jnp.stack([d_w2[n][n * B_RANK:(n + 1) * B_RANK] for n in range(2)]),
                 gate_b=jnp.concatenate([d_b0, d_b1]), norm_g=d_norm_g[0], w_out=d_w_out)
    return dx, grads


def _pad_cols(w, width=LANES):
    return jnp.pad(w, ((0, 0), (0, width - w.shape[1])))


def _local_step(x, target, a_w_in, a_conv, a_alog, a_dt_bias, a_norm_g, a_w_out, b_w_in, b_gate_w2, b_gate_b, b_norm_g,
                b_w_out, ln1_g, ln1_b, mlp_w1, mlp_w2, ln2_g, ln2_b):
    layer_p = []
    for i in range(DEPTH):
        j = i // 2
        if i % 2 == 0:
            layer_p.append(dict(
                w_main=a_w_in[j][:, :4 * A_W], w_gate=_pad_cols(a_w_in[j][:, 4 * A_W:]), conv=a_conv[j],
                alog_row=jnp.pad(a_alog[j].reshape(1, N_GATE), ((0, 0), (N_GATE, LANES - 2 * N_GATE))),
                dt_row=jnp.pad(a_dt_bias[j].reshape(1, N_GATE), ((0, 0), (N_GATE, LANES - 2 * N_GATE))),
                norm_g=a_norm_g[j].reshape(1, A_DK), w_out=a_w_out[j]))
        else:
            w2 = jnp.stack([jnp.pad(b_gate_w2[j][n], ((n * B_RANK, LANES - (n + 1) * B_RANK), (0, 0))) for n in range(2)])
            layer_p.append(dict(
                w_main=b_w_in[j][:, :2 * B_KW + 2 * B_VW], w_gate=_pad_cols(b_w_in[j][:, 2 * B_KW + 2 * B_VW:]),
                w2=w2, gate_b=b_gate_b[j].reshape(2, 1, B_KW), norm_g=b_norm_g[j].reshape(1, B_DV), w_out=b_w_out[j]))

    saved = []
    h = x
    for i in range(DEPTH):
        p = layer_p[i]
        m, sv = (_gdn_fwd if i % 2 == 0 else _gla_fwd)(h, p)
        x1 = ln_fwd(h, m, ln1_g[i:i + 1], ln1_b[i:i + 1])
        h1 = mm(x1, mlp_w1[i], name="mlp_up")
        mlp = mm(h1, mlp_w2[i], act="sqrelu", name="mlp_down")
        x2 = ln_fwd(x1, mlp, ln2_g[i:i + 1], ln2_b[i:i + 1])
        saved.append((sv, h, m, x1, h1, mlp))
        h = x2

    dh, loss_part = loss_head(h, target)

    g_a, g_b, g_ln1g, g_ln1b, g_ln2g, g_ln2b, g_w1, g_w2 = {}, {}, {}, {}, {}, {}, {}, {}
    for i in reversed(range(DEPTH)):
        sv, xin, m, x1, h1, mlp = saved[i]
        p = layer_p[i]
        dr2, g_ln2g[i], g_ln2b[i] = ln_bwd(x1, mlp, ln2_g[i:i + 1], dh)
        g_w2[i] = mm(h1, dr2, "tn", act="sqrelu", name="mlp_dw_down")
        dh1 = mm(dr2, mlp_w2[i], "nt", epi="dsqrelu", extra=h1, name="mlp_dh")
        g_w1[i] = mm(x1, dh1, "tn", name="mlp_dw_up")
        dx1 = mm(dh1, mlp_w1[i], "nt", epi="add", extra=dr2, alpha=ALPHA, name="mlp_dx")
        dr1, g_ln1g[i], g_ln1b[i] = ln_bwd(xin, m, ln1_g[i:i + 1], dx1)
        dh, g = (_gdn_bwd if i % 2 == 0 else _gla_bwd)(sv, p, dr1)
        (g_a if i % 2 == 0 else g_b)[i // 2] = g

    per_layer = lambda d, key=None: [(d[i] if key is None else d[i][key]) for i in sorted(d)]
    st = lambda d, key=None: jnp.stack(per_layer(d, key))
    grads = dict(
        a_w_in=per_layer(g_a, "w_in"), a_conv=st(g_a, "conv"), a_alog=st(g_a, "alog"), a_dt_bias=st(g_a, "dt"),
        a_norm_g=st(g_a, "norm_g"), a_w_out=per_layer(g_a, "w_out"), b_w_in=per_layer(g_b, "w_in"),
        b_gate_w2=st(g_b, "gate_w2"), b_gate_b=st(g_b, "gate_b"), b_norm_g=st(g_b, "norm_g"),
        b_w_out=per_layer(g_b, "w_out"), ln1_g=st(g_ln1g)[:, 0], ln1_b=st(g_ln1b)[:, 0], mlp_w1=per_layer(g_w1),
        mlp_w2=per_layer(g_w2), ln2_g=st(g_ln2g)[:, 0], ln2_b=st(g_ln2b)[:, 0])
    return loss_part, dh, grads


WEIGHTS = ("a_w_in", "a_conv", "a_alog", "a_dt_bias", "a_norm_g", "a_w_out", "b_w_in", "b_gate_w2", "b_gate_b",
           "b_norm_g", "b_w_out", "ln1_g", "ln1_b", "mlp_w1", "mlp_w2", "ln2_g", "ln2_b")
BIG = ("mlp_w1", "mlp_w2", "a_w_out", "b_w_out", "a_w_in", "b_w_in")
SHARD_AXIS = {"mlp_w1": 2, "mlp_w2": 1, "a_w_out": 1, "b_w_out": 1, "a_w_in": 2, "b_w_in": 2}
SMALL = tuple(n for n in WEIGHTS if n not in BIG)
SMALL_SHARD_AXIS = {"a_conv": 2, "b_gate_w2": 3, "b_gate_b": 2, "b_norm_g": 1}


def _to_chip_major(full, axis):
    shp = full.shape
    t = full.reshape(shp[:axis] + (4, shp[axis] // 4) + shp[axis + 1:])
    return jnp.moveaxis(t, axis, 0)


def _from_chip_major(stacked, axis):
    t = jnp.moveaxis(stacked, 0, axis)
    shp = t.shape
    return t.reshape(shp[:axis] + (shp[axis] * shp[axis + 1],) + shp[axis + 2:])


def kernel(x, a_w_in, a_conv, a_alog, a_dt_bias, a_norm_g, a_w_out, b_w_in, b_gate_w2, b_gate_b, b_norm_g, b_w_out, ln1_g, ln1_b, mlp_w1, mlp_w2, ln2_g, ln2_b, loss_target, m_a_w_in, m_a_conv, m_a_alog, m_a_dt_bias, m_a_norm_g, m_a_w_out, m_b_w_in, m_b_gate_w2, m_b_gate_b, m_b_norm_g, m_b_w_out, m_ln1_g, m_ln1_b, m_mlp_w1, m_mlp_w2, m_ln2_g, m_ln2_b, v_a_w_in, v_a_conv, v_a_alog, v_a_dt_bias, v_a_norm_g, v_a_w_out, v_b_w_in, v_b_gate_w2, v_b_gate_b, v_b_norm_g, v_b_w_out, v_ln1_g, v_ln1_b, v_mlp_w1, v_mlp_w2, v_ln2_g, v_ln2_b):
    w = dict(a_w_in=a_w_in, a_conv=a_conv, a_alog=a_alog, a_dt_bias=a_dt_bias, a_norm_g=a_norm_g, a_w_out=a_w_out,
             b_w_in=b_w_in, b_gate_w2=b_gate_w2, b_gate_b=b_gate_b, b_norm_g=b_norm_g, b_w_out=b_w_out, ln1_g=ln1_g,
             ln1_b=ln1_b, mlp_w1=mlp_w1, mlp_w2=mlp_w2, ln2_g=ln2_g, ln2_b=ln2_b)
    mom = dict(a_w_in=m_a_w_in, a_conv=m_a_conv, a_alog=m_a_alog, a_dt_bias=m_a_dt_bias, a_norm_g=m_a_norm_g,
               a_w_out=m_a_w_out, b_w_in=m_b_w_in, b_gate_w2=m_b_gate_w2, b_gate_b=m_b_gate_b, b_norm_g=m_b_norm_g,
               b_w_out=m_b_w_out, ln1_g=m_ln1_g, ln1_b=m_ln1_b, mlp_w1=m_mlp_w1, mlp_w2=m_mlp_w2, ln2_g=m_ln2_g,
               ln2_b=m_ln2_b)
    var = dict(a_w_in=v_a_w_in, a_conv=v_a_conv, a_alog=v_a_alog, a_dt_bias=v_a_dt_bias, a_norm_g=v_a_norm_g,
               a_w_out=v_a_w_out, b_w_in=v_b_w_in, b_gate_w2=v_b_gate_w2, b_gate_b=v_b_gate_b, b_norm_g=v_b_norm_g,
               b_w_out=v_b_w_out, ln1_g=v_ln1_g, ln1_b=v_ln1_b, mlp_w1=v_mlp_w1, mlp_w2=v_mlp_w2, ln2_g=v_ln2_g,
               ln2_b=v_ln2_b)
    chip = 2 * lax.axis_index("x") + lax.axis_index("y")

    seg_rows = [w[n].size // D_MODEL for n in BIG]
    seg_off = [sum(seg_rows[:i]) for i in range(len(BIG))]
    rows = -(-sum(seg_rows) // PACK_TILE) * PACK_TILE
    shard_pack = _pack_rows([w[n] for n in BIG], rows, D_MODEL)
    chip_idx = chip.astype(jnp.int32).reshape(1)
    gathered = allgather_chips(cast_into_slot(shard_pack, chip_idx))
    full = {}
    for n, off, nr in zip(BIG, seg_off, seg_rows):
        stacked = gathered[:, off:off + nr].reshape((4,) + w[n].shape)
        full[n] = _from_chip_major(stacked, SHARD_AXIS[n])
    sharded_small = tuple(SMALL_SHARD_AXIS)
    sm_shapes = [w[n].shape for n in sharded_small]
    sm_rows = _rows_for(sm_shapes, LANES)
    sm_all = exchange_all(_pack_rows([w[n] for n in sharded_small], sm_rows, LANES), "gather_small")
    per_chip = [_unpack_rows(sm_all[2 * pch], sm_shapes) for pch in range(4)]
    for idx, n in enumerate(sharded_small):
        full[n] = jnp.concatenate([per_chip[pch][idx] for pch in range(4)], axis=SMALL_SHARD_AXIS[n])
    for n in WEIGHTS:
        full.setdefault(n, w[n])

    loss_part, grad_x, grads = _local_step(x[0], loss_target[0], *[full[n] for n in WEIGHTS])
    loss = lax.psum(jnp.sum(loss_part), ("x", "y", "c"))

    gpack = jnp.concatenate(
        [_to_chip_major(g, SHARD_AXIS[n] - 1).reshape(4, -1, D_MODEL) for n in BIG for g in grads[n]]
        + [jnp.zeros((4, rows - sum(seg_rows), D_MODEL), F32)], axis=1)
    core = lax.axis_index("c").astype(jnp.int32).reshape(1)
    chip_sum = add_sibling_half(gpack, sibling_halves(gpack), core)
    reduced = join_halves(sum_received(chip_sum, scatter_chips(chip_sum), chip_idx, core))
    out_g, out_d, out_m, out_v = {}, {}, {}, {}
    for n, off, nr in zip(BIG, seg_off, seg_rows):
        if w[n].shape[-1] == D_MODEL:
            view = lambda t: t.reshape(-1, D_MODEL)
            res = adamw(view(w[n]), view(mom[n]), view(var[n]), (reduced,), off, "adamw_" + n)
        else:
            cols = w[n].shape[-1]
            view = lambda t: t.reshape(-1, cols)
            res = adamw(view(w[n]), view(mom[n]), view(var[n]), (view(reduced[off:off + nr]),), 0, "adamw_" + n)
        out_g[n], out_d[n], out_m[n], out_v[n] = (t.reshape(w[n].shape) for t in res)

    all_shapes = [full[n].shape for n in SMALL]
    g_rows = _rows_for(all_shapes, LANES)
    g_all = exchange_all(_pack_rows([grads[n] for n in SMALL], g_rows, LANES), "gather_small_grads")
    g_sum = _unpack_rows(sum_slots(g_all, "sum_small_grads"), all_shapes)
    g_mine = []
    for n, g in zip(SMALL, g_sum):
        if n in SMALL_SHARD_AXIS:
            ax = SMALL_SHARD_AXIS[n]
            g = lax.dynamic_slice_in_dim(g, chip * w[n].shape[ax], w[n].shape[ax], axis=ax)
        g_mine.append(g)
    my_shapes = [w[n].shape for n in SMALL]
    s_rows = _rows_for(my_shapes, LANES)
    pk = lambda d: _pack_rows([d[n] for n in SMALL], s_rows, LANES)
    res = adamw(pk(w), pk(mom), pk(var), (_pack_rows(g_mine, s_rows, LANES),), 0, "adamw_small")
    for dst, pack in zip((out_g, out_d, out_m, out_v), res):
        for n, t in zip(SMALL, _unpack_rows(pack, my_shapes)):
            dst[n] = t

    return (loss, grad_x[None], *[out_g[n] for n in WEIGHTS], *[out_d[n] for n in WEIGHTS],
            *[out_m[n] for n in WEIGHTS], *[out_v[n] for n in WEIGHTS])
```

```python
import functools
import math

import jax
import jax.numpy as jnp
from jax import lax
from jax.experimental import pallas as pl
from jax.experimental.pallas import tpu as pltpu

F32 = jnp.float32
BF16 = jnp.bfloat16

D_MODEL = 1024
DEPTH = 4
CHUNK = 64
A_HEADS = 8
A_DK = 128
A_W = 1024
A_CONV = 5
B_HEADS = 4
B_DK = 128
B_DV = 256
B_RANK = 16
B_TAU = 16.0
B_KW = 512
B_VW = 1024
ALPHA = (2 * DEPTH) ** 0.25
LN_EPS = 1e-5
RMS_EPS = 1e-6
L2_EPS = 1e-6
ADAM_LR = 0.001
ADAM_B1 = 0.9
ADAM_B2 = 0.999
ADAM_EPS = 1e-08
ADAM_WD = 0.01
ADAM_STEP = 10
LANES = 128
NEG_INF = float("-inf")
VMEM_LIMIT = 56 * 1024 * 1024


def _cparams(sem=None):
    return pltpu.CompilerParams(dimension_semantics=sem, vmem_limit_bytes=VMEM_LIMIT)


def _dg(a, b, ca, cb):
    return lax.dot_general(a.astype(BF16), b.astype(BF16), (((ca,), (cb,)), ((), ())),
                           preferred_element_type=F32)


def _split(x):
    hi = x.astype(BF16)
    return hi, (x - hi.astype(F32)).astype(BF16)


def _dg3(a, b, ca, cb):
    (a1, a2), (b1, b2) = _split(a), _split(b)
    return (_dg(a1, b2, ca, cb) + _dg(a2, b1, ca, cb)) + _dg(a1, b1, ca, cb)


def _dot_with_vjp(dg):
    @functools.partial(jax.custom_vjp, nondiff_argnums=(2, 3))
    def dot(a, b, ca, cb):
        return dg(a, b, ca, cb)

    def fwd(a, b, ca, cb):
        return dg(a, b, ca, cb), (a, b)

    def bwd(ca, cb, res, g):
        a, b = res
        da = dg(g, b, 1, 1 - cb) if ca == 1 else dg(b, g, 1 - cb, 1)
        db = dg(a, g, 1 - ca, 0) if cb == 0 else dg(g, a, 0, 1 - ca)
        return da, db

    dot.defvjp(fwd, bwd)
    return dot


bdot = _dot_with_vjp(_dg)
xdot3 = _dot_with_vjp(_dg3)


def nn(a, b):
    return bdot(a, b, 1, 0)


def nt(a, b):
    return bdot(a, b, 1, 1)


def tn(a, b):
    return bdot(a, b, 0, 0)


def xdot(a, b):
    return xdot3(a, b, 1, 0)


def _sigmoid(x):
    return 1.0 / (1.0 + jnp.exp(-x))


def _softplus(x):
    return jnp.maximum(x, 0.0) + jnp.log(1.0 + jnp.exp(-jnp.abs(x)))


def _chunk_masks(rev):
    ii = lax.broadcasted_iota(jnp.int32, (CHUNK, CHUNK), 0)
    jj = lax.broadcasted_iota(jnp.int32, (CHUNK, CHUNK), 1)
    d = (ii - jj) * (1 - 2 * rev)
    return d >= 0, d > 0, ii == jj, (ii >> 3) == (jj >> 3)


def _each(f, *lists):
    return [f(*xs) for xs in zip(*lists)]


def _unit_triangular_inverse(a, eye, blockdiag):
    ident = eye.astype(F32)
    ad = _each(lambda x: jnp.where(blockdiag, x, 0.0), a)
    e = _each(lambda x, y: x - y, a, ad)
    dinv = _each(lambda x: ident - x, ad)
    p = _each(xdot, ad, ad)
    dinv = _each(lambda x, y: x + xdot(x, y), dinv, p)
    p = _each(xdot, p, p)
    dinv = _each(lambda x, y: x + xdot(x, y), dinv, p)
    g = _each(lambda x, y: -xdot(x, y), dinv, e)
    finv = _each(lambda x: ident + x, g)
    p = _each(xdot, g, g)
    finv = _each(lambda x, y: x + xdot(x, y), finv, p)
    p = _each(xdot, p, p)
    finv = _each(lambda x, y: x + xdot(x, y), finv, p)
    return _each(xdot, finv, dinv)


def _gdn_step(state, q, k, v, bb, gb, rev):
    causal, strict, eye, blockdiag = _chunk_masks(rev)
    lower = causal.astype(F32)
    ones = jnp.ones((CHUNK, CHUNK), F32)
    gcb = _each(lambda x: xdot(lower, x), gb)
    gcol = _each(lambda x: x[:, :CHUNK], gcb)
    grow = _each(lambda x: xdot(ones, jnp.where(eye, x, 0.0)), gcol)
    decay = _each(lambda x, y: jnp.exp(jnp.where(causal, x - y, NEG_INF)), gcol, grow)
    kb = _each(lambda x, y: x * y, k, bb)
    a = _each(lambda x, y, z: jnp.where(strict, nt(x, y) * z, 0.0), kb, k, decay)
    t = _unit_triangular_inverse(a, eye, blockdiag)
    egc = _each(jnp.exp, gcb)
    u = _each(lambda x, y, z: xdot(x, y * z), t, v, bb)
    w = _each(lambda x, y, z: xdot(x, y * z), t, kb, egc)
    qk = _each(lambda x, y, z: nt(x, y) * z, q, k, decay)
    glast = _each(lambda x: jnp.sum(x, axis=0, keepdims=True), gb)
    v_new = _each(lambda x, y, z: x - nn(y, z), u, w, state)
    o = _each(lambda x, y, z, p, r: nn(x * y, z) + nn(p, r), q, egc, state, qk, v_new)
    k_dec = _each(lambda x, y, z: x * jnp.exp(y - z), k, glast, gcb)
    state_new = _each(lambda x, y, z, p: x * jnp.exp(y) + tn(z, p), state, glast, k_dec, v_new)
    return state_new, o


def _gla_step(state_t, q, k, v, la, rev):
    causal, _, _, _ = _chunk_masks(rev)
    lower = causal.astype(F32)
    sign = 1 - 2 * rev
    b = _each(lambda x: xdot(lower, x), la)
    q = _each(lambda x: x * (B_DK ** -0.5), q)
    row = lax.broadcasted_iota(jnp.int32, (CHUNK, B_DK), 0)
    sub = row // GLA_SUB
    scores = None
    for blk in range(CHUNK // GLA_SUB):
        r_at = jnp.where(rev == 1, GLA_SUB * (blk + 1), GLA_SUB * blk - 1)
        r = _each(lambda x: jnp.sum(jnp.where(row == r_at, x, 0.0), axis=0, keepdims=True), b)
        q_blk = _each(lambda x, y, z: x * jnp.exp(jnp.where(sub == blk, y - z, NEG_INF)), q, b, r)
        k_past = _each(lambda x, y, z: x * jnp.exp(jnp.where((sub - blk) * sign < 0, z - y, NEG_INF)), k, b, r)
        part = _each(lambda x, y: xdot3(x, y, 1, 1), q_blk, k_past)
        scores = part if scores is None else _each(lambda x, y: x + y, scores, part)
    shp = (GLA_SUB, GLA_SUB, B_DK)
    d3 = (lax.broadcasted_iota(jnp.int32, shp, 0) - lax.broadcasted_iota(jnp.int32, shp, 1)) * sign
    place_r = lax.broadcasted_iota(jnp.int32, (GLA_SUB, CHUNK), 0)
    place_c = lax.broadcasted_iota(jnp.int32, (GLA_SUB, CHUNK), 1)
    diag = []
    for blk in range(CHUNK // GLA_SUB):
        rows = slice(blk * GLA_SUB, (blk + 1) * GLA_SUB)
        place = (place_c == place_r + blk * GLA_SUB).astype(F32)

        def pairs(qh, kh, bh):
            qb, kb, bb = qh[rows], kh[rows], bh[rows]
            dec = jnp.exp(jnp.where(d3 >= 0, bb[:, None, :] - bb[None, :, :], NEG_INF))
            return xdot(jnp.sum(qb[:, None, :] * kb[None, :, :] * dec, axis=-1), place)

        diag.append(_each(pairs, q, k, b))
    scores = _each(lambda x, *d: x + jnp.concatenate(d, axis=0), scores, *diag)
    blast = _each(lambda x: jnp.sum(x, axis=0, keepdims=True), la)
    o = _each(lambda x, y, z, s, w: nt(x * jnp.exp(y), z) + nn(s, w), q, b, state_t, scores, v)
    k_dec = _each(lambda x, y, z: x * jnp.exp(y - z), k, blast, b)
    state_new = _each(lambda x, y, z, w: jnp.exp(x) * y + tn(z, w), blast, state_t, v, k_dec)
    return state_new, o


def _chunk_pos(d, m, n):
    return m + d * (n - 1 - 2 * m)


GLA_SUB = 16
GDN_HEADS_PER_STEP = 8
def gdn_rec_fwd(q, k, v, beta_b, g_b):
    s = q.shape[0]
    n = s // CHUNK

    hb = GDN_HEADS_PER_STEP
    wide = hb * LANES

    def body(q_ref, k_ref, v_ref, bb_ref, gb_ref, o_ref, st_ref, state):
        d = pl.program_id(0)

        @pl.when(pl.program_id(2) == 0)
        def _():
            state[...] = jnp.zeros_like(state)

        cols = [slice(hh * LANES, (hh + 1) * LANES) for hh in range(hb)]
        st = [state[hh] for hh in range(hb)]
        new, o = _gdn_step(st, *([r[:, c] for c in cols] for r in (q_ref, k_ref, v_ref, bb_ref, gb_ref)), d)
        for hh, c in enumerate(cols):
            st_ref[hh] = st[hh]
            state[hh] = new[hh]
            o_ref[:, c] = o[hh]

    blk = pl.BlockSpec((CHUNK, wide), lambda d, h, m: (_chunk_pos(d, m, n), h))
    gate = pl.BlockSpec((CHUNK, wide), lambda d, h, m: (_chunk_pos(d, m, n), d * (A_HEADS // hb) + h))
    return pl.pallas_call(
        body, name="gdn_rec_fwd", grid=(2, A_HEADS // hb, n),
        in_specs=[blk, blk, blk, gate, gate],
        out_specs=[pl.BlockSpec((None, CHUNK, wide), lambda d, h, m: (d, _chunk_pos(d, m, n), h)),
                   pl.BlockSpec((None, hb, None, A_DK, LANES), lambda d, h, m: (d, h, _chunk_pos(d, m, n), 0, 0))],
        out_shape=[jax.ShapeDtypeStruct((2, s, A_W), F32), jax.ShapeDtypeStruct((2, A_HEADS, n, A_DK, LANES), F32)],
        scratch_shapes=[pltpu.VMEM((hb, A_DK, LANES), F32)],
        compiler_params=_cparams(("arbitrary", "arbitrary", "arbitrary")),
    )(q, k, v, beta_b, g_b)


def gdn_rec_bwd(q, k, v, beta_b, g_b, states, do):
    s = q.shape[0]
    n = s // CHUNK

    hb = GDN_HEADS_PER_STEP
    wide = hb * LANES

    def body(q_ref, k_ref, v_ref, bb_ref, gb_ref, st_ref, do_ref, dq_ref, dk_ref, dv_ref, dbb_ref, dgb_ref, dstate):
        d = pl.program_id(0)

        @pl.when(pl.program_id(2) == 0)
        def _():
            dstate[...] = jnp.zeros_like(dstate)

        step = functools.partial(_gdn_step, rev=d)
        cols = [slice(hh * LANES, (hh + 1) * LANES) for hh in range(hb)]
        _, vjp = jax.vjp(step, [st_ref[hh] for hh in range(hb)],
                         *([r[:, c] for c in cols] for r in (q_ref, k_ref, v_ref, bb_ref, gb_ref)))
        grads = vjp(([dstate[hh] for hh in range(hb)], [do_ref[:, c] for c in cols]))
        for hh, c in enumerate(cols):
            dstate[hh], dq_ref[:, c], dk_ref[:, c], dv_ref[:, c], dbb_ref[:, c], dgb_ref[:, c] = (g[hh] for g in grads)

    pos = lambda d, m: _chunk_pos(1 - d, m, n)
    blk = pl.BlockSpec((CHUNK, wide), lambda d, h, m: (pos(d, m), h))
    gate = pl.BlockSpec((CHUNK, wide), lambda d, h, m: (pos(d, m), d * (A_HEADS // hb) + h))
    oblk = pl.BlockSpec((None, CHUNK, wide), lambda d, h, m: (d, pos(d, m), h))
    return pl.pallas_call(
        body, name="gdn_rec_bwd", grid=(2, A_HEADS // hb, n),
        in_specs=[blk, blk, blk, gate, gate,
                  pl.BlockSpec((None, hb, None, A_DK, LANES), lambda d, h, m: (d, h, pos(d, m), 0, 0)), blk],
        out_specs=[oblk, oblk, oblk, gate, gate],
        out_shape=[jax.ShapeDtypeStruct((2, s, A_W), F32)] * 3 + [jax.ShapeDtypeStruct(beta_b.shape, F32)] * 2,
        scratch_shapes=[pltpu.VMEM((hb, A_DK, LANES), F32)],
        compiler_params=_cparams(("arbitrary", "arbitrary", "arbitrary")),
    )(q, k, v, beta_b, g_b, states, do)


def gla_rec_fwd(proj, log_a):
    s = proj.shape[0]
    n = s // CHUNK

    kcols = [slice(h * B_DK, (h + 1) * B_DK) for h in range(B_HEADS)]
    vcols = [slice(h * B_DV, (h + 1) * B_DV) for h in range(B_HEADS)]

    def body(q_ref, k_ref, v_ref, la_ref, o_ref, st_ref, state):
        d = pl.program_id(0)

        @pl.when(pl.program_id(1) == 0)
        def _():
            state[...] = jnp.zeros_like(state)

        st = [state[h] for h in range(B_HEADS)]
        new, o = _gla_step(st, [q_ref[:, c] for c in kcols], [k_ref[:, c] for c in kcols], [v_ref[:, c] for c in vcols],
                           [la_ref[:, c] for c in kcols], d)
        for h in range(B_HEADS):
            st_ref[h] = st[h]
            state[h] = new[h]
            o_ref[:, vcols[h]] = o[h]

    pos = lambda d, m: _chunk_pos(d, m, n)
    return pl.pallas_call(
        body, name="gla_rec_fwd", grid=(2, n),
        in_specs=[pl.BlockSpec((CHUNK, B_KW), lambda d, m: (pos(d, m), 0)),
                  pl.BlockSpec((CHUNK, B_KW), lambda d, m: (pos(d, m), 1)),
                  pl.BlockSpec((CHUNK, B_VW), lambda d, m: (pos(d, m), 2 * B_KW // B_VW)),
                  pl.BlockSpec((None, CHUNK, B_KW), lambda d, m: (d, pos(d, m), 0))],
        out_specs=[pl.BlockSpec((None, CHUNK, B_VW), lambda d, m: (d, pos(d, m), 0)),
                   pl.BlockSpec((None, B_HEADS, None, B_DV, B_DK), lambda d, m: (d, 0, pos(d, m), 0, 0))],
        out_shape=[jax.ShapeDtypeStruct((2, s, B_VW), F32), jax.ShapeDtypeStruct((2, B_HEADS, n, B_DV, B_DK), F32)],
        scratch_shapes=[pltpu.VMEM((B_HEADS, B_DV, B_DK), F32)],
        compiler_params=_cparams(("arbitrary", "arbitrary")),
    )(proj, proj, proj, log_a)


def gla_rec_bwd(proj, log_a, states, do):
    s = proj.shape[0]
    n = s // CHUNK

    kcols = [slice(h * B_DK, (h + 1) * B_DK) for h in range(B_HEADS)]
    vcols = [slice(h * B_DV, (h + 1) * B_DV) for h in range(B_HEADS)]

    def body(q_ref, k_ref, v_ref, la_ref, st_ref, do_ref, dq_ref, dk_ref, dv_ref, dla_ref, dstate):
        d = pl.program_id(0)

        @pl.when(pl.program_id(1) == 0)
        def _():
            dstate[...] = jnp.zeros_like(dstate)

        step = functools.partial(_gla_step, rev=d)
        _, vjp = jax.vjp(step, [st_ref[h] for h in range(B_HEADS)], [q_ref[:, c] for c in kcols],
                         [k_ref[:, c] for c in kcols], [v_ref[:, c] for c in vcols], [la_ref[:, c] for c in kcols])
        dst, dq, dk, dv, dla = vjp(([dstate[h] for h in range(B_HEADS)], [do_ref[:, c] for c in vcols]))
        for h in range(B_HEADS):
            dstate[h] = dst[h]
            dq_ref[:, kcols[h]] = dq[h]
            dk_ref[:, kcols[h]] = dk[h]
            dv_ref[:, vcols[h]] = dv[h]
            dla_ref[:, kcols[h]] = dla[h]

    pos = lambda d, m: _chunk_pos(1 - d, m, n)
    kblk = pl.BlockSpec((None, CHUNK, B_KW), lambda d, m: (d, pos(d, m), 0))
    return pl.pallas_call(
        body, name="gla_rec_bwd", grid=(2, n),
        in_specs=[pl.BlockSpec((CHUNK, B_KW), lambda d, m: (pos(d, m), 0)),
                  pl.BlockSpec((CHUNK, B_KW), lambda d, m: (pos(d, m), 1)),
                  pl.BlockSpec((CHUNK, B_VW), lambda d, m: (pos(d, m), 2 * B_KW // B_VW)),
                  kblk,
                  pl.BlockSpec((None, B_HEADS, None, B_DV, B_DK), lambda d, m: (d, 0, pos(d, m), 0, 0)),
                  pl.BlockSpec((CHUNK, B_VW), lambda d, m: (pos(d, m), 0))],
        out_specs=[kblk, kblk, pl.BlockSpec((None, CHUNK, B_VW), lambda d, m: (d, pos(d, m), 0)), kblk],
        out_shape=[jax.ShapeDtypeStruct((2, s, B_KW), F32), jax.ShapeDtypeStruct((2, s, B_KW), F32),
                   jax.ShapeDtypeStruct((2, s, B_VW), F32), jax.ShapeDtypeStruct((2, s, B_KW), F32)],
        scratch_shapes=[pltpu.VMEM((B_HEADS, B_DV, B_DK), F32)],
        compiler_params=_cparams(("arbitrary", "arbitrary")),
    )(proj, proj, proj, log_a, states, do)


MM_TILE_OUT = 1024
MM_TILE_K = 1024


def _tile(n, pref):
    return pref if n % pref == 0 else n


def mm(a, b, mode="nn", act=None, epi=None, extra=None, alpha=1.0, name="mm"):
    if mode == "tn":
        kk, m = a.shape
    else:
        m, kk = a.shape
    nn_ = b.shape[0] if mode == "nt" else b.shape[1]
    tm, tn_, tk = _tile(m, MM_TILE_OUT), _tile(nn_, MM_TILE_OUT), _tile(kk, MM_TILE_K)
    nk = kk // tk
    ca, cb = {"nn": (1, 0), "nt": (1, 1), "tn": (0, 0)}[mode]

    def body(*refs):
        if epi is None:
            a_ref, b_ref, o_ref = refs
        else:
            a_ref, b_ref, e_ref, o_ref = refs
        kstep = pl.program_id(2)
        at = a_ref[...]
        if act == "sqrelu":
            at = jnp.square(jnp.maximum(at, 0.0))
        part = _dg(at, b_ref[...], ca, cb)

        @pl.when(kstep == 0)
        def _():
            o_ref[...] = part

        @pl.when(kstep > 0)
        def _():
            o_ref[...] += part

        if epi is not None:
            @pl.when(kstep == nk - 1)
            def _():
                if epi == "dsqrelu":
                    o_ref[...] = o_ref[...] * (2.0 * jnp.maximum(e_ref[...], 0.0))
                else:
                    o_ref[...] = o_ref[...] + alpha * e_ref[...]

    a_spec = pl.BlockSpec((tk, tm), lambda i, j, k: (k, i)) if mode == "tn" else pl.BlockSpec((tm, tk), lambda i, j, k: (i, k))
    b_spec = pl.BlockSpec((tn_, tk), lambda i, j, k: (j, k)) if mode == "nt" else pl.BlockSpec((tk, tn_), lambda i, j, k: (k, j))
    o_spec = pl.BlockSpec((tm, tn_), lambda i, j, k: (i, j))
    ins, specs = [a, b], [a_spec, b_spec]
    if epi is not None:
        ins.append(extra)
        specs.append(o_spec)
    return pl.pallas_call(
        body, name=name, grid=(m // tm, nn_ // tn_, nk), in_specs=specs, out_specs=o_spec,
        out_shape=jax.ShapeDtypeStruct((m, nn_), F32),
        compiler_params=_cparams(("parallel", "parallel", "arbitrary")),
    )(*ins)


ROWS = 256


def _ln_core(x, m, g, b):
    r = ALPHA * x + m
    mu = jnp.mean(r, axis=-1, keepdims=True)
    xc = r - mu
    var = jnp.mean(xc * xc, axis=-1, keepdims=True)
    rstd = lax.rsqrt(var + LN_EPS)
    xhat = xc * rstd
    return xhat * g + b, xhat, rstd


def ln_fwd(x, m, g, b):
    s, dm = x.shape

    def body(x_ref, m_ref, g_ref, b_ref, o_ref):
        o_ref[...] = _ln_core(x_ref[...], m_ref[...], g_ref[...], b_ref[...])[0]

    row = pl.BlockSpec((ROWS, dm), lambda i: (i, 0))
    vec = pl.BlockSpec((1, dm), lambda i: (0, 0))
    return pl.pallas_call(body, name="ln_fwd", grid=(s // ROWS,), in_specs=[row, row, vec, vec], out_specs=row,
                          out_shape=jax.ShapeDtypeStruct((s, dm), F32), compiler_params=_cparams(("parallel",)))(x, m, g, b)


def ln_bwd(x, m, g, dy):
    s, dm = x.shape

    def body(x_ref, m_ref, g_ref, dy_ref, dr_ref, dg_ref, db_ref):
        gv = g_ref[...]
        _, xhat, rstd = _ln_core(x_ref[...], m_ref[...], gv, jnp.zeros_like(gv))
        dy = dy_ref[...]
        dxh = dy * gv
        dr_ref[...] = rstd * (dxh - jnp.mean(dxh, axis=-1, keepdims=True)
                              - xhat * jnp.mean(dxh * xhat, axis=-1, keepdims=True))

        @pl.when(pl.program_id(0) == 0)
        def _():
            dg_ref[...] = jnp.zeros_like(dg_ref)
            db_ref[...] = jnp.zeros_like(db_ref)

        dg_ref[...] += jnp.sum(dy * xhat, axis=0, keepdims=True)
        db_ref[...] += jnp.sum(dy, axis=0, keepdims=True)

    row = pl.BlockSpec((ROWS, dm), lambda i: (i, 0))
    vec = pl.BlockSpec((1, dm), lambda i: (0, 0))
    return pl.pallas_call(body, name="ln_bwd", grid=(s // ROWS,), in_specs=[row, row, vec, row], out_specs=[row, vec, vec],
                          out_shape=[jax.ShapeDtypeStruct((s, dm), F32), jax.ShapeDtypeStruct((1, dm), F32),
                                     jax.ShapeDtypeStruct((1, dm), F32)],
                          compiler_params=_cparams(("arbitrary",)))(x, m, g, dy)


def loss_head(y, target):
    s, dm = y.shape

    def body(y_ref, t_ref, dy_ref, l_ref):
        e = y_ref[...] - t_ref[...]
        dy_ref[...] = e * (1.0 / dm)

        @pl.when(pl.program_id(0) == 0)
        def _():
            l_ref[...] = jnp.zeros_like(l_ref)

        col = jnp.sum(e * e, axis=0, keepdims=True) * (0.5 / dm)
        acc = col[:, :LANES]
        for c in range(1, dm // LANES):
            acc = acc + col[:, c * LANES:(c + 1) * LANES]
        l_ref[...] += acc

    row = pl.BlockSpec((ROWS, dm), lambda i: (i, 0))
    return pl.pallas_call(body, name="loss_head", grid=(s // ROWS,), in_specs=[row, row],
                          out_specs=[row, pl.BlockSpec((1, LANES), lambda i: (0, 0))],
                          out_shape=[jax.ShapeDtypeStruct((s, dm), F32), jax.ShapeDtypeStruct((1, LANES), F32)],
                          compiler_params=_cparams(("arbitrary",)))(y, target)


def _shift_rows_impl(x, d):
    n = x.shape[0]
    if d == 0:
        return x
    t = lax.broadcasted_iota(jnp.int32, x.shape, 0)
    return jnp.where((t + d >= 0) & (t + d < n), pltpu.roll(x, (-d) % n, 0), 0.0)


@functools.partial(jax.custom_vjp, nondiff_argnums=(1,))
def _shift_rows(x, d):
    return _shift_rows_impl(x, d)


_shift_rows.defvjp(lambda x, d: (_shift_rows_impl(x, d), None), lambda d, _, g: (_shift_rows_impl(g, -d),))


def _gdn_pre_fn(u, w, kind):
    rows = lax.broadcasted_iota(jnp.int32, w.shape, 0)
    c = None
    for tap in range(A_CONV):
        w_tap = jnp.sum(jnp.where(rows == tap, w, 0.0), axis=0, keepdims=True)
        term = _shift_rows(u, tap - A_CONV // 2) * w_tap
        c = term if c is None else c + term
    y = c * _sigmoid(c)
    if kind == "v":
        return y
    y = y * lax.rsqrt(jnp.sum(y * y, axis=-1, keepdims=True) + L2_EPS)
    return y * (A_DK ** -0.5) if kind == "q" else y


_KIND_OFF = {"q": 0, "k": A_HEADS, "v": 2 * A_HEADS}


def gdn_pre(proj, conv_w, kind):
    s = proj.shape[0]
    off = _KIND_OFF[kind]

    def body(u_ref, w_ref, o_ref):
        o_ref[...] = _gdn_pre_fn(u_ref[...], w_ref[...], kind)

    return pl.pallas_call(
        body, name="gdn_pre_" + kind, grid=(A_HEADS,),
        in_specs=[pl.BlockSpec((s, LANES), lambda h: (0, off + h)), pl.BlockSpec((A_CONV, LANES), lambda h: (0, off + h))],
        out_specs=pl.BlockSpec((s, LANES), lambda h: (0, h)),
        out_shape=jax.ShapeDtypeStruct((s, A_W), F32), compiler_params=_cparams(("parallel",)))(proj, conv_w)


def gdn_pre_bwd(proj, conv_w, dt2, kind):
    s = proj.shape[0]
    off = _KIND_OFF[kind]

    def body(u_ref, w_ref, d0_ref, d1_ref, du_ref, dw_ref):
        _, vjp = jax.vjp(functools.partial(_gdn_pre_fn, kind=kind), u_ref[...], w_ref[...])
        du, dw = vjp(d0_ref[...] + d1_ref[...])
        du_ref[...] = du
        dw_ref[...] = dw

    return pl.pallas_call(
        body, name="gdn_pre_bwd_" + kind, grid=(A_HEADS,),
        in_specs=[pl.BlockSpec((s, LANES), lambda h: (0, off + h)), pl.BlockSpec((A_CONV, LANES), lambda h: (0, off + h)),
                  pl.BlockSpec((None, s, LANES), lambda h: (0, 0, h)), pl.BlockSpec((None, s, LANES), lambda h: (1, 0, h))],
        out_specs=[pl.BlockSpec((s, LANES), lambda h: (0, h)), pl.BlockSpec((A_CONV, LANES), lambda h: (0, h))],
        out_shape=[jax.ShapeDtypeStruct((s, A_W), F32), jax.ShapeDtypeStruct((A_CONV, A_W), F32)],
        compiler_params=_cparams(("parallel",)))(proj, conv_w, dt2, dt2)


N_GATE = 2 * A_HEADS


def _gdn_gates_fn(ba, alog_row, dt_row):
    r = lax.broadcasted_iota(jnp.int32, (LANES, N_GATE * LANES), 0)
    c = lax.broadcasted_iota(jnp.int32, (LANES, N_GATE * LANES), 1) >> 7
    beta_b = xdot(_sigmoid(ba), (r == c).astype(F32))
    g = -(jnp.exp(alog_row) * _softplus(ba + dt_row))
    g_b = xdot(g, (r == c + N_GATE).astype(F32))
    return beta_b, g_b


def gdn_gates(ba, alog_row, dt_row):
    s = ba.shape[0]

    def body(ba_ref, al_ref, dt_ref, bb_ref, gb_ref):
        bb_ref[...], gb_ref[...] = _gdn_gates_fn(ba_ref[...], al_ref[...], dt_ref[...])

    row = pl.BlockSpec((ROWS, LANES), lambda i: (i, 0))
    vec = pl.BlockSpec((1, LANES), lambda i: (0, 0))
    wide = pl.BlockSpec((ROWS, N_GATE * LANES), lambda i: (i, 0))
    return pl.pallas_call(body, name="gdn_gates", grid=(s // ROWS,), in_specs=[row, vec, vec], out_specs=[wide, wide],
                          out_shape=[jax.ShapeDtypeStruct((s, N_GATE * LANES), F32)] * 2,
                          compiler_params=_cparams(("parallel",)))(ba, alog_row, dt_row)


def gdn_gates_bwd(ba, alog_row, dt_row, dbeta_b, dg_b):
    s = ba.shape[0]

    def body(ba_ref, al_ref, dt_ref, dbb_ref, dgb_ref, dba_ref, dal_ref, ddt_ref):
        _, vjp = jax.vjp(_gdn_gates_fn, ba_ref[...], al_ref[...], dt_ref[...])
        dba, dal, ddt = vjp((dbb_ref[...], dgb_ref[...]))
        dba_ref[...] = dba

        @pl.when(pl.program_id(0) == 0)
        def _():
            dal_ref[...] = jnp.zeros_like(dal_ref)
            ddt_ref[...] = jnp.zeros_like(ddt_ref)

        dal_ref[...] += dal
        ddt_ref[...] += ddt

    row = pl.BlockSpec((ROWS, LANES), lambda i: (i, 0))
    vec = pl.BlockSpec((1, LANES), lambda i: (0, 0))
    wide = pl.BlockSpec((ROWS, N_GATE * LANES), lambda i: (i, 0))
    return pl.pallas_call(body, name="gdn_gates_bwd", grid=(s // ROWS,), in_specs=[row, vec, vec, wide, wide],
                          out_specs=[row, vec, vec],
                          out_shape=[jax.ShapeDtypeStruct((s, LANES), F32), jax.ShapeDtypeStruct((1, LANES), F32),
                                     jax.ShapeDtypeStruct((1, LANES), F32)],
                          compiler_params=_cparams(("arbitrary",)))(ba, alog_row, dt_row, dbeta_b, dg_b)


def _post_fn(o, z, g):
    y = o * lax.rsqrt(jnp.mean(o * o, axis=-1, keepdims=True) + RMS_EPS) * g
    return y * (z * _sigmoid(z))


def mixer_post(o2, proj, norm_g, width, gate_off, name):
    s = o2.shape[1]
    nh = o2.shape[2] // width

    def body(o0_ref, o1_ref, z_ref, g_ref, y_ref):
        y_ref[...] = _post_fn(o0_ref[...] + o1_ref[...], z_ref[...], g_ref[...])

    ospec = lambda d: pl.BlockSpec((None, ROWS, width), lambda i, h: (d, i, h))
    return pl.pallas_call(
        body, name=name, grid=(s // ROWS, nh),
        in_specs=[ospec(0), ospec(1), pl.BlockSpec((ROWS, width), lambda i, h: (i, gate_off + h)),
                  pl.BlockSpec((1, width), lambda i, h: (0, 0))],
        out_specs=pl.BlockSpec((ROWS, width), lambda i, h: (i, h)),
        out_shape=jax.ShapeDtypeStruct((s, o2.shape[2]), F32),
        compiler_params=_cparams(("parallel", "parallel")))(o2, o2, proj, norm_g)


def mixer_post_bwd(o2, proj, norm_g, dy, width, gate_off, name):
    s = o2.shape[1]
    nh = o2.shape[2] // width

    def body(o0_ref, o1_ref, z_ref, g_ref, dy_ref, do_ref, dz_ref, dg_ref):
        _, vjp = jax.vjp(_post_fn, o0_ref[...] + o1_ref[...], z_ref[...], g_ref[...])
        do, dz, dg = vjp(dy_ref[...])
        do_ref[...] = do
        dz_ref[...] = dz

        @pl.when((pl.program_id(0) == 0) & (pl.program_id(1) == 0))
        def _():
            dg_ref[...] = jnp.zeros_like(dg_ref)

        dg_ref[...] += dg

    ospec = lambda d: pl.BlockSpec((None, ROWS, width), lambda i, h: (d, i, h))
    blk = pl.BlockSpec((ROWS, width), lambda i, h: (i, h))
    vec = pl.BlockSpec((1, width), lambda i, h: (0, 0))
    return pl.pallas_call(
        body, name=name, grid=(s // ROWS, nh),
        in_specs=[ospec(0), ospec(1), pl.BlockSpec((ROWS, width), lambda i, h: (i, gate_off + h)), vec, blk],
        out_specs=[blk, blk, vec],
        out_shape=[jax.ShapeDtypeStruct((s, o2.shape[2]), F32)] * 2 + [jax.ShapeDtypeStruct((1, width), F32)],
        compiler_params=_cparams(("arbitrary", "arbitrary")))(o2, o2, proj, norm_g, dy)


def _log_gate(z):
    return (jnp.minimum(z, 0.0) - jnp.log(1.0 + jnp.exp(-jnp.abs(z)))) * (1.0 / B_TAU)


def gla_gate(gl, w2, gb):
    s = gl.shape[0]

    def body(gl_ref, w_ref, b_ref, o_ref):
        for n in range(2):
            o_ref[n] = _log_gate(nn(gl_ref[...], w_ref[n]) + b_ref[n])

    full = lambda shp: pl.BlockSpec(shp, lambda i: (0,) * len(shp))
    return pl.pallas_call(
        body, name="gla_gate", grid=(s // ROWS,),
        in_specs=[pl.BlockSpec((ROWS, LANES), lambda i: (i, 0)), full(w2.shape), full(gb.shape)],
        out_specs=pl.BlockSpec((2, ROWS, B_KW), lambda i: (0, i, 0)),
        out_shape=jax.ShapeDtypeStruct((2, s, B_KW), F32), compiler_params=_cparams(("parallel",)))(gl, w2, gb)


def gla_gate_bwd(gl, w2, gb, dla):
    s = gl.shape[0]

    def body(gl_ref, w_ref, b_ref, dla_ref, dgl_ref, dz_ref, db0_ref, db1_ref):
        @pl.when(pl.program_id(0) == 0)
        def _():
            db0_ref[...] = jnp.zeros_like(db0_ref)
            db1_ref[...] = jnp.zeros_like(db1_ref)

        dgl = None
        for n, db_ref in enumerate((db0_ref, db1_ref)):
            _, vjp = jax.vjp(_log_gate, nn(gl_ref[...], w_ref[n]) + b_ref[n])
            dz, = vjp(dla_ref[n])
            dz_ref[n] = dz
            db_ref[...] += jnp.sum(dz, axis=0, keepdims=True)
            part = nt(dz, w_ref[n])
            dgl = part if dgl is None else dgl + part
        dgl_ref[...] = dgl

    full = lambda shp: pl.BlockSpec(shp, lambda i: (0,) * len(shp))
    row = pl.BlockSpec((ROWS, LANES), lambda i: (i, 0))
    wide = pl.BlockSpec((2, ROWS, B_KW), lambda i: (0, i, 0))
    vec = pl.BlockSpec((1, B_KW), lambda i: (0, 0))
    return pl.pallas_call(
        body, name="gla_gate_bwd", grid=(s // ROWS,),
        in_specs=[row, full(w2.shape), full(gb.shape), wide],
        out_specs=[row, wide, vec, vec],
        out_shape=[jax.ShapeDtypeStruct((s, LANES), F32), jax.ShapeDtypeStruct((2, s, B_KW), F32),
                   jax.ShapeDtypeStruct((1, B_KW), F32), jax.ShapeDtypeStruct((1, B_KW), F32)],
        compiler_params=_cparams(("arbitrary",)))(gl, w2, gb, dla)


PACK_TILE = 512


def cast_into_slot(x, chip):
    r, c = x.shape

    def body(chip_ref, x_ref, o_ref):
        o_ref[...] = x_ref[...].astype(BF16)

    return pl.pallas_call(
        body, name="cast_into_slot",
        grid_spec=pltpu.PrefetchScalarGridSpec(
            num_scalar_prefetch=1, grid=(r // PACK_TILE,),
            in_specs=[pl.BlockSpec((PACK_TILE, c), lambda i, chip_ref: (i, 0))],
            out_specs=pl.BlockSpec((None, PACK_TILE, c), lambda i, chip_ref: (chip_ref[0], i, 0))),
        out_shape=jax.ShapeDtypeStruct((4, r, c), BF16), compiler_params=_cparams(("parallel",)))(chip, x)


def sum_received(chip_sum, recv, chip, core):
    _, h, c = chip_sum.shape
    tr = _tile(h, PACK_TILE)
    nblk = h // tr

    def body(chip_ref, core_ref, own_ref, r_ref, o_ref):
        acc = r_ref[0].astype(F32)
        for k in range(1, 3):
            acc = acc + r_ref[k].astype(F32)
        o_ref[...] = acc + own_ref[...].astype(F32)

    return pl.pallas_call(
        body, name="sum_received",
        grid_spec=pltpu.PrefetchScalarGridSpec(
            num_scalar_prefetch=2, grid=(nblk,),
            in_specs=[pl.BlockSpec((None, tr, c), lambda i, chip_ref, core_ref: (chip_ref[0], i, 0)),
                      pl.BlockSpec((3, tr, c), lambda i, chip_ref, core_ref: (0, i, 0))],
            out_specs=pl.BlockSpec((tr, c), lambda i, chip_ref, core_ref: (core_ref[0] * nblk + i, 0))),
        out_shape=jax.ShapeDtypeStruct((2 * h, c), F32), compiler_params=_cparams(("parallel",)))(chip, core, chip_sum, recv)


def sum_slots(x, name):
    n, r, c = x.shape
    tr = _tile(r, PACK_TILE)

    def body(x_ref, o_ref):
        acc = x_ref[0].astype(F32)
        for k in range(1, n):
            acc = acc + x_ref[k].astype(F32)
        o_ref[...] = acc

    return pl.pallas_call(body, name=name, grid=(r // tr,), in_specs=[pl.BlockSpec((n, tr, c), lambda i: (0, i, 0))],
                          out_specs=pl.BlockSpec((tr, c), lambda i: (i, 0)),
                          out_shape=jax.ShapeDtypeStruct((r, c), F32), compiler_params=_cparams(("parallel",)))(x)


def add_sibling_half(gpack, theirs, core):
    n, r, c = gpack.shape
    half_rows = r // 2
    tr = _tile(half_rows, PACK_TILE)
    nblk = half_rows // tr

    def body(core_ref, g_ref, t_ref, o_ref):
        o_ref[...] = (g_ref[...] + t_ref[...]).astype(BF16)

    blk = pl.BlockSpec((None, tr, c), lambda s, i, core_ref: (s, i, 0))
    return pl.pallas_call(
        body, name="add_sibling_half",
        grid_spec=pltpu.PrefetchScalarGridSpec(
            num_scalar_prefetch=1, grid=(n, nblk),
            in_specs=[pl.BlockSpec((None, tr, c), lambda s, i, core_ref: (s, core_ref[0] * nblk + i, 0)), blk],
            out_specs=blk),
        out_shape=jax.ShapeDtypeStruct((n, half_rows, c), BF16),
        compiler_params=_cparams(("parallel", "parallel")))(core, gpack, theirs)


def adamw(w, m, v, grads, g_row_off, name):
    r, c = w.shape
    tr = next(t for t in (PACK_TILE, r) if r % t == 0 and g_row_off % t == 0)
    ob = g_row_off // tr
    ng = len(grads)

    def body(*refs):
        w_ref, m_ref, v_ref = refs[:3]
        g_refs = refs[3:3 + ng]
        g_ref, d_ref, nm_ref, nv_ref = refs[3 + ng:]
        g = g_refs[0][...]
        for gr in g_refs[1:]:
            g = g + gr[...]
        m_new = ADAM_B1 * m_ref[...] + (1.0 - ADAM_B1) * g
        v_new = ADAM_B2 * v_ref[...] + (1.0 - ADAM_B2) * jnp.square(g)
        m_hat = m_new / (1.0 - ADAM_B1 ** ADAM_STEP)
        v_hat = v_new / (1.0 - ADAM_B2 ** ADAM_STEP)
        g_ref[...] = g
        d_ref[...] = -ADAM_LR * (m_hat / (jnp.sqrt(v_hat) + ADAM_EPS) + ADAM_WD * w_ref[...])
        nm_ref[...] = m_new
        nv_ref[...] = v_new

    blk = pl.BlockSpec((tr, c), lambda i: (i, 0))
    gblk = pl.BlockSpec((tr, c), lambda i: (i + ob, 0))
    return pl.pallas_call(body, name=name, grid=(r // tr,), in_specs=[blk, blk, blk] + [gblk] * ng, out_specs=[blk] * 4,
                          out_shape=[jax.ShapeDtypeStruct((r, c), F32)] * 4,
                          compiler_params=_cparams(("parallel",)))(w, m, v, *grads)


MESH = pl.DeviceIdType.MESH
HBM = pl.BlockSpec(memory_space=pl.ANY)
CHIP_FLIPS = ((1, 0), (0, 1), (1, 1))


def _place():
    return lax.axis_index("x"), lax.axis_index("y"), lax.axis_index("c")


def allgather_chips(buf):
    _, r, c = buf.shape
    half_rows = r // 2

    def body(_, out_ref, send_sems, recv_sems):
        x, y, cc = _place()
        half = pl.ds(cc * half_rows, half_rows)
        other = pl.ds((1 - cc) * half_rows, half_rows)

        def copy(k, rows, to):
            return pltpu.make_async_remote_copy(src_ref=rows, dst_ref=rows, send_sem=send_sems.at[k],
                                                recv_sem=recv_sems.at[k], device_id=to, device_id_type=MESH)

        chips = [((1 - x if fx else x), (1 - y if fy else y)) for fx, fy in CHIP_FLIPS]
        first = [copy(k, out_ref.at[2 * x + y, half], (px, py, cc)) for k, (px, py) in enumerate(chips)]
        for cp in first:
            cp.start()
        passed = []
        for k, (px, py) in enumerate(chips):
            landed = out_ref.at[2 * px + py, half]
            copy(k, landed, (px, py, cc)).wait_recv()
            passed.append(copy(3 + k, landed, (x, y, 1 - cc)))
            passed[-1].start()
        for k, (px, py) in enumerate(chips):
            copy(3 + k, out_ref.at[2 * px + py, other], (x, y, 1 - cc)).wait_recv()
        for cp in first + passed:
            cp.wait_send()

    return pl.pallas_call(
        body, name="allgather_chips", in_specs=[HBM], out_specs=HBM, input_output_aliases={0: 0},
        out_shape=jax.ShapeDtypeStruct(buf.shape, buf.dtype),
        scratch_shapes=[pltpu.SemaphoreType.DMA((6,)), pltpu.SemaphoreType.DMA((6,))],
    )(buf)


def scatter_chips(gpack):
    _, r, c = gpack.shape

    def body(src_ref, out_ref, send_sems, recv_sems):
        x, y, cc = _place()
        sends = []
        for k, (fx, fy) in enumerate(CHIP_FLIPS):
            px, py = (1 - x if fx else x), (1 - y if fy else y)
            sends.append(pltpu.make_async_remote_copy(
                src_ref=src_ref.at[2 * px + py], dst_ref=out_ref.at[k], send_sem=send_sems.at[k], recv_sem=recv_sems.at[k],
                device_id=(px, py, cc), device_id_type=MESH))
        for cp in sends:
            cp.start()
        for cp in sends:
            cp.wait_recv()
        for cp in sends:
            cp.wait_send()

    return pl.pallas_call(
        body, name="scatter_chips", in_specs=[HBM], out_specs=HBM,
        out_shape=jax.ShapeDtypeStruct((3, r, c), gpack.dtype),
        scratch_shapes=[pltpu.SemaphoreType.DMA((3,)), pltpu.SemaphoreType.DMA((3,))],
    )(gpack)


def sibling_halves(gpack):
    n, r, c = gpack.shape
    half_rows = r // 2

    def body(src_ref, out_ref, send_sem, recv_sem):
        x, y, cc = _place()
        cp = pltpu.make_async_remote_copy(
            src_ref=src_ref.at[:, pl.ds((1 - cc) * half_rows, half_rows)], dst_ref=out_ref, send_sem=send_sem,
            recv_sem=recv_sem, device_id=(x, y, 1 - cc), device_id_type=MESH)
        cp.start()
        cp.wait()

    return pl.pallas_call(
        body, name="sibling_halves", in_specs=[HBM], out_specs=HBM,
        out_shape=jax.ShapeDtypeStruct((n, half_rows, c), gpack.dtype),
        scratch_shapes=[pltpu.SemaphoreType.DMA, pltpu.SemaphoreType.DMA],
    )(gpack)


def join_halves(buf):
    r, c = buf.shape
    half_rows = r // 2

    def body(_, out_ref, send_sem, recv_sem):
        x, y, cc = _place()
        half = out_ref.at[pl.ds(cc * half_rows, half_rows)]
        other = out_ref.at[pl.ds((1 - cc) * half_rows, half_rows)]
        send = pltpu.make_async_remote_copy(src_ref=half, dst_ref=half, send_sem=send_sem, recv_sem=recv_sem,
                                            device_id=(x, y, 1 - cc), device_id_type=MESH)
        send.start()
        pltpu.make_async_remote_copy(src_ref=other, dst_ref=other, send_sem=send_sem, recv_sem=recv_sem,
                                     device_id=(x, y, 1 - cc), device_id_type=MESH).wait_recv()
        send.wait_send()

    return pl.pallas_call(
        body, name="join_halves", in_specs=[HBM], out_specs=HBM, input_output_aliases={0: 0},
        out_shape=jax.ShapeDtypeStruct(buf.shape, buf.dtype),
        scratch_shapes=[pltpu.SemaphoreType.DMA, pltpu.SemaphoreType.DMA],
    )(buf)


def exchange_all(v, name):
    r, c = v.shape

    def body(v_ref, out_ref, send_sems, recv_sems):
        x, y, cc = _place()
        out_ref[4 * x + 2 * y + cc] = v_ref[...]
        sends, recvs = [], []
        for k in range(1, 8):
            px = 1 - x if k & 4 else x
            py = 1 - y if k & 2 else y
            pc = 1 - cc if k & 1 else cc
            sends.append(pltpu.make_async_remote_copy(
                src_ref=v_ref, dst_ref=out_ref.at[4 * x + 2 * y + cc], send_sem=send_sems.at[k - 1],
                recv_sem=recv_sems.at[k - 1], device_id=(px, py, pc), device_id_type=MESH))
            recvs.append(pltpu.make_async_remote_copy(
                src_ref=v_ref, dst_ref=out_ref.at[4 * px + 2 * py + pc], send_sem=send_sems.at[k - 1],
                recv_sem=recv_sems.at[k - 1], device_id=(px, py, pc), device_id_type=MESH))
        for cp in sends:
            cp.start()
        for cp in recvs:
            cp.wait_recv()
        for cp in sends:
            cp.wait_send()

    vm = pl.BlockSpec(memory_space=pltpu.VMEM)
    return pl.pallas_call(
        body, name=name, in_specs=[vm], out_specs=vm, out_shape=jax.ShapeDtypeStruct((8, r, c), v.dtype),
        scratch_shapes=[pltpu.SemaphoreType.DMA((7,)), pltpu.SemaphoreType.DMA((7,))],
        compiler_params=pltpu.CompilerParams(vmem_limit_bytes=VMEM_LIMIT),
    )(v)


def _pack_rows(arrays, rows, width):
    flat = jnp.concatenate([a.reshape(-1) for a in arrays])
    return jnp.pad(flat, (0, rows * width - flat.shape[0])).reshape(rows, width)


def _unpack_rows(pack, shapes):
    flat = pack.reshape(-1)
    out, off = [], 0
    for shp in shapes:
        n = math.prod(shp)
        out.append(flat[off:off + n].reshape(shp))
        off += n
    return out


def _rows_for(shapes, width, mult=8):
    n = sum(math.prod(s) for s in shapes)
    return -(-n // (width * mult)) * mult


def _gdn_fwd(x, p):
    proj = mm(x, p["w_main"], name="gdn_proj")
    ba = mm(x, p["w_gate"], name="gdn_proj_gate")
    q, k, v = (gdn_pre(proj, p["conv"], kind) for kind in "qkv")
    beta_b, g_b = gdn_gates(ba, p["alog_row"], p["dt_row"])
    o2, st = gdn_rec_fwd(q, k, v, beta_b, g_b)
    y = mixer_post(o2, proj, p["norm_g"], A_DK, 3 * A_HEADS, "gdn_post")
    m = mm(y, p["w_out"], name="gdn_out")
    return m, (x, proj, ba, q, k, v, beta_b, g_b, o2, st, y)


def _gdn_bwd(saved, p, dm):
    x, proj, ba, q, k, v, beta_b, g_b, o2, st, y = saved
    d_w_out = mm(y, dm, "tn", name="gdn_dw_out")
    dy = mm(dm, p["w_out"], "nt", name="gdn_dy")
    do, dz, d_norm_g = mixer_post_bwd(o2, proj, p["norm_g"], dy, A_DK, 3 * A_HEADS, "gdn_post_bwd")
    dq2, dk2, dv2, dbb, dgb = gdn_rec_bwd(q, k, v, beta_b, g_b, st, do)
    dba, d_alog_row, d_dt_row = gdn_gates_bwd(ba, p["alog_row"], p["dt_row"], dbb, dgb)
    du, dconv = zip(*(gdn_pre_bwd(proj, p["conv"], d2, kind) for d2, kind in ((dq2, "q"), (dk2, "k"), (dv2, "v"))))
    dproj = jnp.concatenate(list(du) + [dz], axis=1)
    d_w_main = mm(x, dproj, "tn", name="gdn_dw_main")
    d_w_gate = mm(x, dba, "tn", name="gdn_dw_gate")
    dx = mm(dba, p["w_gate"], "nt", epi="add", extra=dm, alpha=ALPHA, name="gdn_dx_gate")
    dx = mm(dproj, p["w_main"], "nt", epi="add", extra=dx, name="gdn_dx")
    grads = dict(w_in=jnp.concatenate([d_w_main, d_w_gate[:, :2 * N_GATE]], axis=1), conv=jnp.concatenate(dconv, axis=1),
                 alog=d_alog_row[0, N_GATE:2 * N_GATE].reshape(2, A_HEADS), dt=d_dt_row[0, N_GATE:2 * N_GATE].reshape(2, A_HEADS),
                 norm_g=d_norm_g[0], w_out=d_w_out)
    return dx, grads


def _gla_fwd(x, p):
    proj = mm(x, p["w_main"], name="gla_proj")
    gl = mm(x, p["w_gate"], name="gla_proj_gate")
    log_a = gla_gate(gl, p["w2"], p["gate_b"])
    o2, st = gla_rec_fwd(proj, log_a)
    y = mixer_post(o2, proj, p["norm_g"], B_DV, (2 * B_KW + B_VW) // B_DV, "gla_post")
    m = mm(y, p["w_out"], name="gla_out")
    return m, (x, proj, gl, log_a, o2, st, y)


def _gla_bwd(saved, p, dm):
    x, proj, gl, log_a, o2, st, y = saved
    d_w_out = mm(y, dm, "tn", name="gla_dw_out")
    dy = mm(dm, p["w_out"], "nt", name="gla_dy")
    do, dr, d_norm_g = mixer_post_bwd(o2, proj, p["norm_g"], dy, B_DV, (2 * B_KW + B_VW) // B_DV, "gla_post_bwd")
    dq2, dk2, dv2, dla = gla_rec_bwd(proj, log_a, st, do)
    dgl, dz, d_b0, d_b1 = gla_gate_bwd(gl, p["w2"], p["gate_b"], dla)
    d_w2 = [mm(gl, dz[n], "tn", name="gla_dw_gate_w2") for n in range(2)]
    dproj = jnp.concatenate([dq2[0] + dq2[1], dk2[0] + dk2[1], dv2[0] + dv2[1], dr], axis=1)
    d_w_main = mm(x, dproj, "tn", name="gla_dw_main")
    d_w_gate = mm(x, dgl, "tn", name="gla_dw_gate")
    dx = mm(dgl, p["w_gate"], "nt", epi="add", extra=dm, alpha=ALPHA, name="gla_dx_gate")
    dx = mm(dproj, p["w_main"], "nt", epi="add", extra=dx, name="gla_dx")
    grads = dict(w_in=jnp.concatenate([d_w_main, d_w_gate[:, :2 * B_RANK]], axis=1),
                 gate_w2=jnp.stack([d_w2[n][n * B_RANK:(n + 1) * B_RANK] for n in range(2)]),
                 gate_b=jnp.concatenate([d_b0, d_b1]), norm_g=d_norm_g[0], w_out=d_w_out)
    return dx, grads


def _pad_cols(w, width=LANES):
    return jnp.pad(w, ((0, 0), (0, width - w.shape[1])))


def _local_step(x, target, a_w_in, a_conv, a_alog, a_dt_bias, a_norm_g, a_w_out, b_w_in, b_gate_w2, b_gate_b, b_norm_g,
                b_w_out, ln1_g, ln1_b, mlp_w1, mlp_w2, ln2_g, ln2_b):
    layer_p = []
    for i in range(DEPTH):
        j = i // 2
        if i % 2 == 0:
            layer_p.append(dict(
                w_main=a_w_in[j][:, :4 * A_W], w_gate=_pad_cols(a_w_in[j][:, 4 * A_W:]), conv=a_conv[j],
                alog_row=jnp.pad(a_alog[j].reshape(1, N_GATE), ((0, 0), (N_GATE, LANES - 2 * N_GATE))),
                dt_row=jnp.pad(a_dt_bias[j].reshape(1, N_GATE), ((0, 0), (N_GATE, LANES - 2 * N_GATE))),
                norm_g=a_norm_g[j].reshape(1, A_DK), w_out=a_w_out[j]))
        else:
            w2 = jnp.stack([jnp.pad(b_gate_w2[j][n], ((n * B_RANK, LANES - (n + 1) * B_RANK), (0, 0))) for n in range(2)])
            layer_p.append(dict(
                w_main=b_w_in[j][:, :2 * B_KW + 2 * B_VW], w_gate=_pad_cols(b_w_in[j][:, 2 * B_KW + 2 * B_VW:]),
                w2=w2, gate_b=b_gate_b[j].reshape(2, 1, B_KW), norm_g=b_norm_g[j].reshape(1, B_DV), w_out=b_w_out[j]))

    saved = []
    h = x
    for i in range(DEPTH):
        p = layer_p[i]
        m, sv = (_gdn_fwd if i % 2 == 0 else _gla_fwd)(h, p)
        x1 = ln_fwd(h, m, ln1_g[i:i + 1], ln1_b[i:i + 1])
        h1 = mm(x1, mlp_w1[i], name="mlp_up")
        mlp = mm(h1, mlp_w2[i], act="sqrelu", name="mlp_down")
        x2 = ln_fwd(x1, mlp, ln2_g[i:i + 1], ln2_b[i:i + 1])
        saved.append((sv, h, m, x1, h1, mlp))
        h = x2

    dh, loss_part = loss_head(h, target)

    g_a, g_b, g_ln1g, g_ln1b, g_ln2g, g_ln2b, g_w1, g_w2 = {}, {}, {}, {}, {}, {}, {}, {}
    for i in reversed(range(DEPTH)):
        sv, xin, m, x1, h1, mlp = saved[i]
        p = layer_p[i]
        dr2, g_ln2g[i], g_ln2b[i] = ln_bwd(x1, mlp, ln2_g[i:i + 1], dh)
        g_w2[i] = mm(h1, dr2, "tn", act="sqrelu", name="mlp_dw_down")
        dh1 = mm(dr2, mlp_w2[i], "nt", epi="dsqrelu", extra=h1, name="mlp_dh")
        g_w1[i] = mm(x1, dh1, "tn", name="mlp_dw_up")
        dx1 = mm(dh1, mlp_w1[i], "nt", epi="add", extra=dr2, alpha=ALPHA, name="mlp_dx")
        dr1, g_ln1g[i], g_ln1b[i] = ln_bwd(xin, m, ln1_g[i:i + 1], dx1)
        dh, g = (_gdn_bwd if i % 2 == 0 else _gla_bwd)(sv, p, dr1)
        (g_a if i % 2 == 0 else g_b)[i // 2] = g

    per_layer = lambda d, key=None: [(d[i] if key is None else d[i][key]) for i in sorted(d)]
    st = lambda d, key=None: jnp.stack(per_layer(d, key))
    grads = dict(
        a_w_in=per_layer(g_a, "w_in"), a_conv=st(g_a, "conv"), a_alog=st(g_a, "alog"), a_dt_bias=st(g_a, "dt"),
        a_norm_g=st(g_a, "norm_g"), a_w_out=per_layer(g_a, "w_out"), b_w_in=per_layer(g_b, "w_in"),
        b_gate_w2=st(g_b, "gate_w2"), b_gate_b=st(g_b, "gate_b"), b_norm_g=st(g_b, "norm_g"),
        b_w_out=per_layer(g_b, "w_out"), ln1_g=st(g_ln1g)[:, 0], ln1_b=st(g_ln1b)[:, 0], mlp_w1=per_layer(g_w1),
        mlp_w2=per_layer(g_w2), ln2_g=st(g_ln2g)[:, 0], ln2_b=st(g_ln2b)[:, 0])
    return loss_part, dh, grads


WEIGHTS = ("a_w_in", "a_conv", "a_alog", "a_dt_bias", "a_norm_g", "a_w_out", "b_w_in", "b_gate_w2", "b_gate_b",
           "b_norm_g", "b_w_out", "ln1_g", "ln1_b", "mlp_w1", "mlp_w2", "ln2_g", "ln2_b")
BIG = ("mlp_w1", "mlp_w2", "a_w_out", "b_w_out", "a_w_in", "b_w_in")
SHARD_AXIS = {"mlp_w1": 2, "mlp_w2": 1, "a_w_out": 1, "b_w_out": 1, "a_w_in": 2, "b_w_in": 2}
SMALL = tuple(n for n in WEIGHTS if n not in BIG)
SMALL_SHARD_AXIS = {"a_conv": 2, "b_gate_w2": 3, "b_gate_b": 2, "b_norm_g": 1}


def _to_chip_major(full, axis):
    shp = full.shape
    t = full.reshape(shp[:axis] + (4, shp[axis] // 4) + shp[axis + 1:])
    return jnp.moveaxis(t, axis, 0)


def _from_chip_major(stacked, axis):
    t = jnp.moveaxis(stacked, 0, axis)
    shp = t.shape
    return t.reshape(shp[:axis] + (shp[axis] * shp[axis + 1],) + shp[axis + 2:])


def kernel(x, a_w_in, a_conv, a_alog, a_dt_bias, a_norm_g, a_w_out, b_w_in, b_gate_w2, b_gate_b, b_norm_g, b_w_out, ln1_g, ln1_b, mlp_w1, mlp_w2, ln2_g, ln2_b, loss_target, m_a_w_in, m_a_conv, m_a_alog, m_a_dt_bias, m_a_norm_g, m_a_w_out, m_b_w_in, m_b_gate_w2, m_b_gate_b, m_b_norm_g, m_b_w_out, m_ln1_g, m_ln1_b, m_mlp_w1, m_mlp_w2, m_ln2_g, m_ln2_b, v_a_w_in, v_a_conv, v_a_alog, v_a_dt_bias, v_a_norm_g, v_a_w_out, v_b_w_in, v_b_gate_w2, v_b_gate_b, v_b_norm_g, v_b_w_out, v_ln1_g, v_ln1_b, v_mlp_w1, v_mlp_w2, v_ln2_g, v_ln2_b):
    w = dict(a_w_in=a_w_in, a_conv=a_conv, a_alog=a_alog, a_dt_bias=a_dt_bias, a_norm_g=a_norm_g, a_w_out=a_w_out,
             b_w_in=b_w_in, b_gate_w2=b_gate_w2, b_gate_b=b_gate_b, b_norm_g=b_norm_g, b_w_out=b_w_out, ln1_g=ln1_g,
             ln1_b=ln1_b, mlp_w1=mlp_w1, mlp_w2=mlp_w2, ln2_g=ln2_g, ln2_b=ln2_b)
    mom = dict(a_w_in=m_a_w_in, a_conv=m_a_conv, a_alog=m_a_alog, a_dt_bias=m_a_dt_bias, a_norm_g=m_a_norm_g,
               a_w_out=m_a_w_out, b_w_in=m_b_w_in, b_gate_w2=m_b_gate_w2, b_gate_b=m_b_gate_b, b_norm_g=m_b_norm_g,
               b_w_out=m_b_w_out, ln1_g=m_ln1_g, ln1_b=m_ln1_b, mlp_w1=m_mlp_w1, mlp_w2=m_mlp_w2, ln2_g=m_ln2_g,
               ln2_b=m_ln2_b)
    var = dict(a_w_in=v_a_w_in, a_conv=v_a_conv, a_alog=v_a_alog, a_dt_bias=v_a_dt_bias, a_norm_g=v_a_norm_g,
               a_w_out=v_a_w_out, b_w_in=v_b_w_in, b_gate_w2=v_b_gate_w2, b_gate_b=v_b_gate_b, b_norm_g=v_b_norm_g,
               b_w_out=v_b_w_out, ln1_g=v_ln1_g, ln1_b=v_ln1_b, mlp_w1=v_mlp_w1, mlp_w2=v_mlp_w2, ln2_g=v_ln2_g,
               ln2_b=v_ln2_b)
    chip = 2 * lax.axis_index("x") + lax.axis_index("y")

    seg_rows = [w[n].size // D_MODEL for n in BIG]
    seg_off = [sum(seg_rows[:i]) for i in range(len(BIG))]
    rows = -(-sum(seg_rows) // PACK_TILE) * PACK_TILE
    shard_pack = jnp.concatenate([w[n].reshape(-1, D_MODEL) for n in BIG]
                                 + [jnp.zeros((rows - sum(seg_rows), D_MODEL), F32)], axis=0)
    chip_idx = chip.astype(jnp.int32).reshape(1)
    gathered = allgather_chips(cast_into_slot(shard_pack, chip_idx))
    full = {}
    for n, off, nr in zip(BIG, seg_off, seg_rows):
        stacked = gathered[:, off:off + nr].reshape((4,) + w[n].shape)
        full[n] = _from_chip_major(stacked, SHARD_AXIS[n])
    sharded_small = tuple(SMALL_SHARD_AXIS)
    sm_shapes = [w[n].shape for n in sharded_small]
    sm_rows = _rows_for(sm_shapes, LANES)
    sm_all = exchange_all(_pack_rows([w[n] for n in sharded_small], sm_rows, LANES), "gather_small")
    per_chip = [_unpack_rows(sm_all[2 * pch], sm_shapes) for pch in range(4)]
    for idx, n in enumerate(sharded_small):
        full[n] = jnp.concatenate([per_chip[pch][idx] for pch in range(4)], axis=SMALL_SHARD_AXIS[n])
    for n in WEIGHTS:
        full.setdefault(n, w[n])

    loss_part, grad_x, grads = _local_step(x[0], loss_target[0], *[full[n] for n in WEIGHTS])
    loss = lax.psum(jnp.sum(loss_part), ("x", "y", "c"))

    gpack = jnp.concatenate(
        [_to_chip_major(g, SHARD_AXIS[n] - 1).reshape(4, -1, D_MODEL) for n in BIG for g in grads[n]]
        + [jnp.zeros((4, rows - sum(seg_rows), D_MODEL), F32)], axis=1)
    core = lax.axis_index("c").astype(jnp.int32).reshape(1)
    chip_sum = add_sibling_half(gpack, sibling_halves(gpack), core)
    reduced = join_halves(sum_received(chip_sum, scatter_chips(chip_sum), chip_idx, core))
    out_g, out_d, out_m, out_v = {}, {}, {}, {}
    for n, off, nr in zip(BIG, seg_off, seg_rows):
        if w[n].shape[-1] == D_MODEL:
            view = lambda t: t.reshape(-1, D_MODEL)
            res = adamw(view(w[n]), view(mom[n]), view(var[n]), (reduced,), off, "adamw_" + n)
        else:
            cols = w[n].shape[-1]
            view = lambda t: t.reshape(-1, cols)
            res = adamw(view(w[n]), view(mom[n]), view(var[n]), (view(reduced[off:off + nr]),), 0, "adamw_" + n)
        out_g[n], out_d[n], out_m[n], out_v[n] = (t.reshape(w[n].shape) for t in res)

    all_shapes = [full[n].shape for n in SMALL]
    g_rows = _rows_for(all_shapes, LANES)
    g_all = exchange_all(_pack_rows([grads[n] for n in SMALL], g_rows, LANES), "gather_small_grads")
    g_sum = _unpack_rows(sum_slots(g_all, "sum_small_grads"), all_shapes)
    g_mine = []
    for n, g in zip(SMALL, g_sum):
        if n in SMALL_SHARD_AXIS:
            ax = SMALL_SHARD_AXIS[n]
            g = lax.dynamic_slice_in_dim(g, chip * w[n].shape[ax], w[n].shape[ax], axis=ax)
        g_mine.append(g)
    my_shapes = [w[n].shape for n in SMALL]
    s_rows = _rows_for(my_shapes, LANES)
    pk = lambda d: _pack_rows([d[n] for n in SMALL], s_rows, LANES)
    res = adamw(pk(w), pk(mom), pk(var), (_pack_rows(g_mine, s_rows, LANES),), 0, "adamw_small")
    for dst, pack in zip((out_g, out_d, out_m, out_v), res):
        for n, t in zip(SMALL, _unpack_rows(pack, my_shapes)):
            dst[n] = t

    return (loss, grad_x[None], *[out_g[n] for n in WEIGHTS], *[out_d[n] for n in WEIGHTS],
            *[out_m[n] for n in WEIGHTS], *[out_v[n] for n in WEIGHTS])
```

```python
import functools
import math

import jax
import jax.numpy as jnp
from jax import lax
from jax.experimental import pallas as pl
from jax.experimental.pallas import tpu as pltpu

F32 = jnp.float32
BF16 = jnp.bfloat16

D_MODEL = 1024
DEPTH = 4
CHUNK = 64
A_HEADS = 8
A_DK = 128
A_W = 1024
A_CONV = 5
B_HEADS = 4
B_DK = 128
B_DV = 256
B_RANK = 16
B_TAU = 16.0
B_KW = 512
B_VW = 1024
ALPHA = (2 * DEPTH) ** 0.25
LN_EPS = 1e-5
RMS_EPS = 1e-6
L2_EPS = 1e-6
ADAM_LR = 0.001
ADAM_B1 = 0.9
ADAM_B2 = 0.999
ADAM_EPS = 1e-08
ADAM_WD = 0.01
ADAM_STEP = 10
LANES = 128
NEG_INF = float("-inf")
VMEM_LIMIT = 56 * 1024 * 1024


def _cparams(sem=None):
    return pltpu.CompilerParams(dimension_semantics=sem, vmem_limit_bytes=VMEM_LIMIT)


def _dg(a, b, ca, cb):
    return lax.dot_general(a.astype(BF16), b.astype(BF16), (((ca,), (cb,)), ((), ())),
                           preferred_element_type=F32)


def _split(x):
    hi = x.astype(BF16)
    return hi, (x - hi.astype(F32)).astype(BF16)


def _dg3(a, b, ca, cb):
    (a1, a2), (b1, b2) = _split(a), _split(b)
    return (_dg(a1, b2, ca, cb) + _dg(a2, b1, ca, cb)) + _dg(a1, b1, ca, cb)


def _dot_with_vjp(dg):
    @functools.partial(jax.custom_vjp, nondiff_argnums=(2, 3))
    def dot(a, b, ca, cb):
        return dg(a, b, ca, cb)

    def fwd(a, b, ca, cb):
        return dg(a, b, ca, cb), (a, b)

    def bwd(ca, cb, res, g):
        a, b = res
        da = dg(g, b, 1, 1 - cb) if ca == 1 else dg(b, g, 1 - cb, 1)
        db = dg(a, g, 1 - ca, 0) if cb == 0 else dg(g, a, 0, 1 - ca)
        return da, db

    dot.defvjp(fwd, bwd)
    return dot


bdot = _dot_with_vjp(_dg)
xdot3 = _dot_with_vjp(_dg3)


def nn(a, b):
    return bdot(a, b, 1, 0)


def nt(a, b):
    return bdot(a, b, 1, 1)


def tn(a, b):
    return bdot(a, b, 0, 0)


def xdot(a, b):
    return xdot3(a, b, 1, 0)


def _sigmoid(x):
    return 1.0 / (1.0 + jnp.exp(-x))


def _softplus(x):
    return jnp.maximum(x, 0.0) + jnp.log(1.0 + jnp.exp(-jnp.abs(x)))


def _chunk_masks(rev):
    ii = lax.broadcasted_iota(jnp.int32, (CHUNK, CHUNK), 0)
    jj = lax.broadcasted_iota(jnp.int32, (CHUNK, CHUNK), 1)
    d = (ii - jj) * (1 - 2 * rev)
    return d >= 0, d > 0, ii == jj, (ii >> 3) == (jj >> 3)


def _each(f, *lists):
    return [f(*xs) for xs in zip(*lists)]


def _unit_triangular_inverse(a, eye, blockdiag):
    ident = eye.astype(F32)
    ad = _each(lambda x: jnp.where(blockdiag, x, 0.0), a)
    e = _each(lambda x, y: x - y, a, ad)
    dinv = _each(lambda x: ident - x, ad)
    p = _each(xdot, ad, ad)
    dinv = _each(lambda x, y: x + xdot(x, y), dinv, p)
    p = _each(xdot, p, p)
    dinv = _each(lambda x, y: x + xdot(x, y), dinv, p)
    g = _each(lambda x, y: -xdot(x, y), dinv, e)
    finv = _each(lambda x: ident + x, g)
    p = _each(xdot, g, g)
    finv = _each(lambda x, y: x + xdot(x, y), finv, p)
    p = _each(xdot, p, p)
    finv = _each(lambda x, y: x + xdot(x, y), finv, p)
    return _each(xdot, finv, dinv)


def _gdn_step(state, q, k, v, bb, gb, rev):
    causal, strict, eye, blockdiag = _chunk_masks(rev)
    lower = causal.astype(F32)
    ones = jnp.ones((CHUNK, CHUNK), F32)
    gcb = _each(lambda x: xdot(lower, x), gb)
    gcol = _each(lambda x: x[:, :CHUNK], gcb)
    grow = _each(lambda x: xdot(ones, jnp.where(eye, x, 0.0)), gcol)
    decay = _each(lambda x, y: jnp.exp(jnp.where(causal, x - y, NEG_INF)), gcol, grow)
    kb = _each(lambda x, y: x * y, k, bb)
    a = _each(lambda x, y, z: jnp.where(strict, nt(x, y) * z, 0.0), kb, k, decay)
    t = _unit_triangular_inverse(a, eye, blockdiag)
    egc = _each(jnp.exp, gcb)
    u = _each(lambda x, y, z: xdot(x, y * z), t, v, bb)
    w = _each(lambda x, y, z: xdot(x, y * z), t, kb, egc)
    qk = _each(lambda x, y, z: nt(x, y) * z, q, k, decay)
    glast = _each(lambda x: jnp.sum(x, axis=0, keepdims=True), gb)
    v_new = _each(lambda x, y, z: x - nn(y, z), u, w, state)
    o = _each(lambda x, y, z, p, r: nn(x * y, z) + nn(p, r), q, egc, state, qk, v_new)
    k_dec = _each(lambda x, y, z: x * jnp.exp(y - z), k, glast, gcb)
    state_new = _each(lambda x, y, z, p: x * jnp.exp(y) + tn(z, p), state, glast, k_dec, v_new)
    return state_new, o


def _gla_step(state_t, q, k, v, la, rev):
    causal, _, _, _ = _chunk_masks(rev)
    lower = causal.astype(F32)
    sign = 1 - 2 * rev
    b = _each(lambda x: xdot(lower, x), la)
    q = _each(lambda x: x * (B_DK ** -0.5), q)
    row = lax.broadcasted_iota(jnp.int32, (CHUNK, B_DK), 0)
    sub = row // GLA_SUB
    scores = None
    for blk in range(CHUNK // GLA_SUB):
        r_at = jnp.where(rev == 1, GLA_SUB * (blk + 1), GLA_SUB * blk - 1)
        r = _each(lambda x: jnp.sum(jnp.where(row == r_at, x, 0.0), axis=0, keepdims=True), b)
        q_blk = _each(lambda x, y, z: x * jnp.exp(jnp.where(sub == blk, y - z, NEG_INF)), q, b, r)
        k_past = _each(lambda x, y, z: x * jnp.exp(jnp.where((sub - blk) * sign < 0, z - y, NEG_INF)), k, b, r)
        part = _each(lambda x, y: xdot3(x, y, 1, 1), q_blk, k_past)
        scores = part if scores is None else _each(lambda x, y: x + y, scores, part)
    shp = (GLA_SUB, GLA_SUB, B_DK)
    d3 = (lax.broadcasted_iota(jnp.int32, shp, 0) - lax.broadcasted_iota(jnp.int32, shp, 1)) * sign
    place_r = lax.broadcasted_iota(jnp.int32, (GLA_SUB, CHUNK), 0)
    place_c = lax.broadcasted_iota(jnp.int32, (GLA_SUB, CHUNK), 1)
    diag = []
    for blk in range(CHUNK // GLA_SUB):
        rows = slice(blk * GLA_SUB, (blk + 1) * GLA_SUB)
        place = (place_c == place_r + blk * GLA_SUB).astype(F32)

        def pairs(qh, kh, bh):
            qb, kb, bb = qh[rows], kh[rows], bh[rows]
            dec = jnp.exp(jnp.where(d3 >= 0, bb[:, None, :] - bb[None, :, :], NEG_INF))
            return xdot(jnp.sum(qb[:, None, :] * kb[None, :, :] * dec, axis=-1), place)

        diag.append(_each(pairs, q, k, b))
    scores = _each(lambda x, *d: x + jnp.concatenate(d, axis=0), scores, *diag)
    blast = _each(lambda x: jnp.sum(x, axis=0, keepdims=True), la)
    o = _each(lambda x, y, z, s, w: nt(x * jnp.exp(y), z) + nn(s, w), q, b, state_t, scores, v)
    k_dec = _each(lambda x, y, z: x * jnp.exp(y - z), k, blast, b)
    state_new = _each(lambda x, y, z, w: jnp.exp(x) * y + tn(z, w), blast, state_t, v, k_dec)
    return state_new, o


def _chunk_pos(d, m, n):
    return m + d * (n - 1 - 2 * m)


GLA_SUB = 16
GDN_HEADS_PER_STEP = 8
def gdn_rec_fwd(q, k, v, beta_b, g_b):
    s = q.shape[0]
    n = s // CHUNK

    hb = GDN_HEADS_PER_STEP
    wide = hb * LANES

    def body(q_ref, k_ref, v_ref, bb_ref, gb_ref, o_ref, st_ref, state):
        d = pl.program_id(0)

        @pl.when(pl.program_id(2) == 0)
        def _():
            state[...] = jnp.zeros_like(state)

        cols = [slice(hh * LANES, (hh + 1) * LANES) for hh in range(hb)]
        st = [state[hh] for hh in range(hb)]
        new, o = _gdn_step(st, *([r[:, c] for c in cols] for r in (q_ref, k_ref, v_ref, bb_ref, gb_ref)), d)
        for hh, c in enumerate(cols):
            st_ref[hh] = st[hh]
            state[hh] = new[hh]
            o_ref[:, c] = o[hh]

    blk = pl.BlockSpec((CHUNK, wide), lambda d, h, m: (_chunk_pos(d, m, n), h))
    gate = pl.BlockSpec((CHUNK, wide), lambda d, h, m: (_chunk_pos(d, m, n), d * (A_HEADS // hb) + h))
    return pl.pallas_call(
        body, name="gdn_rec_fwd", grid=(2, A_HEADS // hb, n),
        in_specs=[blk, blk, blk, gate, gate],
        out_specs=[pl.BlockSpec((None, CHUNK, wide), lambda d, h, m: (d, _chunk_pos(d, m, n), h)),
                   pl.BlockSpec((None, hb, None, A_DK, LANES), lambda d, h, m: (d, h, _chunk_pos(d, m, n), 0, 0))],
        out_shape=[jax.ShapeDtypeStruct((2, s, A_W), F32), jax.ShapeDtypeStruct((2, A_HEADS, n, A_DK, LANES), F32)],
        scratch_shapes=[pltpu.VMEM((hb, A_DK, LANES), F32)],
        compiler_params=_cparams(("arbitrary", "arbitrary", "arbitrary")),
    )(q, k, v, beta_b, g_b)


def gdn_rec_bwd(q, k, v, beta_b, g_b, states, do):
    s = q.shape[0]
    n = s // CHUNK

    hb = GDN_HEADS_PER_STEP
    wide = hb * LANES

    def body(q_ref, k_ref, v_ref, bb_ref, gb_ref, st_ref, do_ref, dq_ref, dk_ref, dv_ref, dbb_ref, dgb_ref, dstate):
        d = pl.program_id(0)

        @pl.when(pl.program_id(2) == 0)
        def _():
            dstate[...] = jnp.zeros_like(dstate)

        step = functools.partial(_gdn_step, rev=d)
        cols = [slice(hh * LANES, (hh + 1) * LANES) for hh in range(hb)]
        _, vjp = jax.vjp(step, [st_ref[hh] for hh in range(hb)],
                         *([r[:, c] for c in cols] for r in (q_ref, k_ref, v_ref, bb_ref, gb_ref)))
        grads = vjp(([dstate[hh] for hh in range(hb)], [do_ref[:, c] for c in cols]))
        for hh, c in enumerate(cols):
            dstate[hh], dq_ref[:, c], dk_ref[:, c], dv_ref[:, c], dbb_ref[:, c], dgb_ref[:, c] = (g[hh] for g in grads)

    pos = lambda d, m: _chunk_pos(1 - d, m, n)
    blk = pl.BlockSpec((CHUNK, wide), lambda d, h, m: (pos(d, m), h))
    gate = pl.BlockSpec((CHUNK, wide), lambda d, h, m: (pos(d, m), d * (A_HEADS // hb) + h))
    oblk = pl.BlockSpec((None, CHUNK, wide), lambda d, h, m: (d, pos(d, m), h))
    return pl.pallas_call(
        body, name="gdn_rec_bwd", grid=(2, A_HEADS // hb, n),
        in_specs=[blk, blk, blk, gate, gate,
                  pl.BlockSpec((None, hb, None, A_DK, LANES), lambda d, h, m: (d, h, pos(d, m), 0, 0)), blk],
        out_specs=[oblk, oblk, oblk, gate, gate],
        out_shape=[jax.ShapeDtypeStruct((2, s, A_W), F32)] * 3 + [jax.ShapeDtypeStruct(beta_b.shape, F32)] * 2,
        scratch_shapes=[pltpu.VMEM((hb, A_DK, LANES), F32)],
        compiler_params=_cparams(("arbitrary", "arbitrary", "arbitrary")),
    )(q, k, v, beta_b, g_b, states, do)


def gla_rec_fwd(proj, log_a):
    s = proj.shape[0]
    n = s // CHUNK

    kcols = [slice(h * B_DK, (h + 1) * B_DK) for h in range(B_HEADS)]
    vcols = [slice(h * B_DV, (h + 1) * B_DV) for h in range(B_HEADS)]

    def body(q_ref, k_ref, v_ref, la_ref, o_ref, st_ref, state):
        d = pl.program_id(0)

        @pl.when(pl.program_id(1) == 0)
        def _():
            state[...] = jnp.zeros_like(state)

        st = [state[h] for h in range(B_HEADS)]
        new, o = _gla_step(st, [q_ref[:, c] for c in kcols], [k_ref[:, c] for c in kcols], [v_ref[:, c] for c in vcols],
                           [la_ref[:, c] for c in kcols], d)
        for h in range(B_HEADS):
            st_ref[h] = st[h]
            state[h] = new[h]
            o_ref[:, vcols[h]] = o[h]

    pos = lambda d, m: _chunk_pos(d, m, n)
    return pl.pallas_call(
        body, name="gla_rec_fwd", grid=(2, n),
        in_specs=[pl.BlockSpec((CHUNK, B_KW), lambda d, m: (pos(d, m), 0)),
                  pl.BlockSpec((CHUNK, B_KW), lambda d, m: (pos(d, m), 1)),
                  pl.BlockSpec((CHUNK, B_VW), lambda d, m: (pos(d, m), 2 * B_KW // B_VW)),
                  pl.BlockSpec((None, CHUNK, B_KW), lambda d, m: (d, pos(d, m), 0))],
        out_specs=[pl.BlockSpec((None, CHUNK, B_VW), lambda d, m: (d, pos(d, m), 0)),
                   pl.BlockSpec((None, B_HEADS, None, B_DV, B_DK), lambda d, m: (d, 0, pos(d, m), 0, 0))],
        out_shape=[jax.ShapeDtypeStruct((2, s, B_VW), F32), jax.ShapeDtypeStruct((2, B_HEADS, n, B_DV, B_DK), F32)],
        scratch_shapes=[pltpu.VMEM((B_HEADS, B_DV, B_DK), F32)],
        compiler_params=_cparams(("arbitrary", "arbitrary")),
    )(proj, proj, proj, log_a)


def gla_rec_bwd(proj, log_a, states, do):
    s = proj.shape[0]
    n = s // CHUNK

    kcols = [slice(h * B_DK, (h + 1) * B_DK) for h in range(B_HEADS)]
    vcols = [slice(h * B_DV, (h + 1) * B_DV) for h in range(B_HEADS)]

    def body(q_ref, k_ref, v_ref, la_ref, st_ref, do_ref, dq_ref, dk_ref, dv_ref, dla_ref, dstate):
        d = pl.program_id(0)

        @pl.when(pl.program_id(1) == 0)
        def _():
            dstate[...] = jnp.zeros_like(dstate)

        step = functools.partial(_gla_step, rev=d)
        _, vjp = jax.vjp(step, [st_ref[h] for h in range(B_HEADS)], [q_ref[:, c] for c in kcols],
                         [k_ref[:, c] for c in kcols], [v_ref[:, c] for c in vcols], [la_ref[:, c] for c in kcols])
        dst, dq, dk, dv, dla = vjp(([dstate[h] for h in range(B_HEADS)], [do_ref[:, c] for c in vcols]))
        for h in range(B_HEADS):
            dstate[h] = dst[h]
            dq_ref[:, kcols[h]] = dq[h]
            dk_ref[:, kcols[h]] = dk[h]
            dv_ref[:, vcols[h]] = dv[h]
            dla_ref[:, kcols[h]] = dla[h]

    pos = lambda d, m: _chunk_pos(1 - d, m, n)
    kblk = pl.BlockSpec((None, CHUNK, B_KW), lambda d, m: (d, pos(d, m), 0))
    return pl.pallas_call(
        body, name="gla_rec_bwd", grid=(2, n),
        in_specs=[pl.BlockSpec((CHUNK, B_KW), lambda d, m: (pos(d, m), 0)),
                  pl.BlockSpec((CHUNK, B_KW), lambda d, m: (pos(d, m), 1)),
                  pl.BlockSpec((CHUNK, B_VW), lambda d, m: (pos(d, m), 2 * B_KW // B_VW)),
                  kblk,
                  pl.BlockSpec((None, B_HEADS, None, B_DV, B_DK), lambda d, m: (d, 0, pos(d, m), 0, 0)),
                  pl.BlockSpec((CHUNK, B_VW), lambda d, m: (pos(d, m), 0))],
        out_specs=[kblk, kblk, pl.BlockSpec((None, CHUNK, B_VW), lambda d, m: (d, pos(d, m), 0)), kblk],
        out_shape=[jax.ShapeDtypeStruct((2, s, B_KW), F32), jax.ShapeDtypeStruct((2, s, B_KW), F32),
                   jax.ShapeDtypeStruct((2, s, B_VW), F32), jax.ShapeDtypeStruct((2, s, B_KW), F32)],
        scratch_shapes=[pltpu.VMEM((B_HEADS, B_DV, B_DK), F32)],
        compiler_params=_cparams(("arbitrary", "arbitrary")),
    )(proj, proj, proj, log_a, states, do)


MM_TILE_OUT = 1024
MM_TILE_K = 1024


def _tile(n, pref):
    return pref if n % pref == 0 else n


class Gathered:
    def __init__(self, g, off, kind):
        assert off % D_MODEL == 0 and MM_TILE_OUT == D_MODEL and MM_TILE_K == D_MODEL
        self.g, self.blk, self.kind = g, off // D_MODEL, kind
        self.shape = (D_MODEL, 4 * D_MODEL) if kind == "cols" else (4 * D_MODEL, D_MODEL)

    def spec(self, mode):
        blk = self.blk
        chip_is_k = (self.kind == "rows") == (mode == "nn")
        if chip_is_k:
            return pl.BlockSpec((None, D_MODEL, D_MODEL), lambda i, j, k: (k, blk, 0))
        return pl.BlockSpec((None, D_MODEL, D_MODEL), lambda i, j, k: (j, blk, 0))


def mm(a, b, mode="nn", act=None, epi=None, extra=None, alpha=1.0, chip_major=False, name="mm"):
    if mode == "tn":
        kk, m = a.shape
    else:
        m, kk = a.shape
    nn_ = b.shape[0] if mode == "nt" else b.shape[1]
    tm, tn_, tk = _tile(m, MM_TILE_OUT), _tile(nn_, MM_TILE_OUT), _tile(kk, MM_TILE_K)
    nk = kk // tk
    ca, cb = {"nn": (1, 0), "nt": (1, 1), "tn": (0, 0)}[mode]

    def body(*refs):
        if epi is None:
            a_ref, b_ref, o_ref = refs
        else:
            a_ref, b_ref, e_ref, o_ref = refs
        kstep = pl.program_id(2)
        at = a_ref[...]
        if act == "sqrelu":
            at = jnp.square(jnp.maximum(at, 0.0))
        part = _dg(at, b_ref[...], ca, cb)

        @pl.when(kstep == 0)
        def _():
            o_ref[...] = part

        @pl.when(kstep > 0)
        def _():
            o_ref[...] += part

        if epi is not None:
            @pl.when(kstep == nk - 1)
            def _():
                if epi == "dsqrelu":
                    o_ref[...] = o_ref[...] * (2.0 * jnp.maximum(e_ref[...], 0.0))
                else:
                    o_ref[...] = o_ref[...] + alpha * e_ref[...]

    a_spec = pl.BlockSpec((tk, tm), lambda i, j, k: (k, i)) if mode == "tn" else pl.BlockSpec((tm, tk), lambda i, j, k: (i, k))
    if isinstance(b, Gathered):
        assert mode in ("nn", "nt") and tn_ == D_MODEL and tk == D_MODEL
        b_spec, b = b.spec(mode), b.g
    elif mode == "nt":
        b_spec = pl.BlockSpec((tn_, tk), lambda i, j, k: (j, k))
    else:
        b_spec = pl.BlockSpec((tk, tn_), lambda i, j, k: (k, j))
    o_spec = pl.BlockSpec((tm, tn_), lambda i, j, k: (i, j))
    ins, specs = [a, b], [a_spec, b_spec]
    if epi is not None:
        ins.append(extra)
        specs.append(o_spec)
    out_shape = jax.ShapeDtypeStruct((m, nn_), F32)
    if chip_major:
        assert nn_ == 4 * D_MODEL and tn_ == D_MODEL
        o_spec = pl.BlockSpec((None, tm, D_MODEL), lambda i, j, k: (j, i, 0))
        out_shape = jax.ShapeDtypeStruct((4, m, D_MODEL), F32)
    return pl.pallas_call(
        body, name=name, grid=(m // tm, nn_ // tn_, nk), in_specs=specs, out_specs=o_spec, out_shape=out_shape,
        compiler_params=_cparams(("parallel", "parallel", "arbitrary")),
    )(*ins)


ROWS = 256


def _ln_core(x, m, g, b):
    r = ALPHA * x + m
    mu = jnp.mean(r, axis=-1, keepdims=True)
    xc = r - mu
    var = jnp.mean(xc * xc, axis=-1, keepdims=True)
    rstd = lax.rsqrt(var + LN_EPS)
    xhat = xc * rstd
    return xhat * g + b, xhat, rstd


def ln_fwd(x, m, g, b):
    s, dm = x.shape

    def body(x_ref, m_ref, g_ref, b_ref, o_ref):
        o_ref[...] = _ln_core(x_ref[...], m_ref[...], g_ref[...], b_ref[...])[0]

    row = pl.BlockSpec((ROWS, dm), lambda i: (i, 0))
    vec = pl.BlockSpec((1, dm), lambda i: (0, 0))
    return pl.pallas_call(body, name="ln_fwd", grid=(s // ROWS,), in_specs=[row, row, vec, vec], out_specs=row,
                          out_shape=jax.ShapeDtypeStruct((s, dm), F32), compiler_params=_cparams(("parallel",)))(x, m, g, b)


def ln_bwd(x, m, g, dy):
    s, dm = x.shape

    def body(x_ref, m_ref, g_ref, dy_ref, dr_ref, dg_ref, db_ref):
        gv = g_ref[...]
        _, xhat, rstd = _ln_core(x_ref[...], m_ref[...], gv, jnp.zeros_like(gv))
        dy = dy_ref[...]
        dxh = dy * gv
        dr_ref[...] = rstd * (dxh - jnp.mean(dxh, axis=-1, keepdims=True)
                              - xhat * jnp.mean(dxh * xhat, axis=-1, keepdims=True))

        @pl.when(pl.program_id(0) == 0)
        def _():
            dg_ref[...] = jnp.zeros_like(dg_ref)
            db_ref[...] = jnp.zeros_like(db_ref)

        dg_ref[...] += jnp.sum(dy * xhat, axis=0, keepdims=True)
        db_ref[...] += jnp.sum(dy, axis=0, keepdims=True)

    row = pl.BlockSpec((ROWS, dm), lambda i: (i, 0))
    vec = pl.BlockSpec((1, dm), lambda i: (0, 0))
    return pl.pallas_call(body, name="ln_bwd", grid=(s // ROWS,), in_specs=[row, row, vec, row], out_specs=[row, vec, vec],
                          out_shape=[jax.ShapeDtypeStruct((s, dm), F32), jax.ShapeDtypeStruct((1, dm), F32),
                                     jax.ShapeDtypeStruct((1, dm), F32)],
                          compiler_params=_cparams(("arbitrary",)))(x, m, g, dy)


def loss_head(y, target):
    s, dm = y.shape

    def body(y_ref, t_ref, dy_ref, l_ref):
        e = y_ref[...] - t_ref[...]
        dy_ref[...] = e * (1.0 / dm)

        @pl.when(pl.program_id(0) == 0)
        def _():
            l_ref[...] = jnp.zeros_like(l_ref)

        col = jnp.sum(e * e, axis=0, keepdims=True) * (0.5 / dm)
        acc = col[:, :LANES]
        for c in range(1, dm // LANES):
            acc = acc + col[:, c * LANES:(c + 1) * LANES]
        l_ref[...] += acc

    row = pl.BlockSpec((ROWS, dm), lambda i: (i, 0))
    return pl.pallas_call(body, name="loss_head", grid=(s // ROWS,), in_specs=[row, row],
                          out_specs=[row, pl.BlockSpec((1, LANES), lambda i: (0, 0))],
                          out_shape=[jax.ShapeDtypeStruct((s, dm), F32), jax.ShapeDtypeStruct((1, LANES), F32)],
                          compiler_params=_cparams(("arbitrary",)))(y, target)


def _shift_rows_impl(x, d):
    n = x.shape[0]
    if d == 0:
        return x
    t = lax.broadcasted_iota(jnp.int32, x.shape, 0)
    return jnp.where((t + d >= 0) & (t + d < n), pltpu.roll(x, (-d) % n, 0), 0.0)


@functools.partial(jax.custom_vjp, nondiff_argnums=(1,))
def _shift_rows(x, d):
    return _shift_rows_impl(x, d)


_shift_rows.defvjp(lambda x, d: (_shift_rows_impl(x, d), None), lambda d, _, g: (_shift_rows_impl(g, -d),))


def _gdn_pre_fn(u, w, kind):
    rows = lax.broadcasted_iota(jnp.int32, w.shape, 0)
    c = None
    for tap in range(A_CONV):
        w_tap = jnp.sum(jnp.where(rows == tap, w, 0.0), axis=0, keepdims=True)
        term = _shift_rows(u, tap - A_CONV // 2) * w_tap
        c = term if c is None else c + term
    y = c * _sigmoid(c)
    if kind == "v":
        return y
    y = y * lax.rsqrt(jnp.sum(y * y, axis=-1, keepdims=True) + L2_EPS)
    return y * (A_DK ** -0.5) if kind == "q" else y


_KIND_OFF = {"q": 0, "k": A_HEADS, "v": 2 * A_HEADS}


def gdn_pre(proj, conv_w, kind):
    s = proj.shape[0]
    off = _KIND_OFF[kind]

    def body(u_ref, w_ref, o_ref):
        o_ref[...] = _gdn_pre_fn(u_ref[...], w_ref[...], kind)

    return pl.pallas_call(
        body, name="gdn_pre_" + kind, grid=(A_HEADS,),
        in_specs=[pl.BlockSpec((s, LANES), lambda h: (0, off + h)), pl.BlockSpec((A_CONV, LANES), lambda h: (0, off + h))],
        out_specs=pl.BlockSpec((s, LANES), lambda h: (0, h)),
        out_shape=jax.ShapeDtypeStruct((s, A_W), F32), compiler_params=_cparams(("parallel",)))(proj, conv_w)


def gdn_pre_bwd(proj, conv_w, dt2, kind):
    s = proj.shape[0]
    off = _KIND_OFF[kind]

    def body(u_ref, w_ref, d0_ref, d1_ref, du_ref, dw_ref):
        _, vjp = jax.vjp(functools.partial(_gdn_pre_fn, kind=kind), u_ref[...], w_ref[...])
        du, dw = vjp(d0_ref[...] + d1_ref[...])
        du_ref[...] = du
        dw_ref[...] = dw

    return pl.pallas_call(
        body, name="gdn_pre_bwd_" + kind, grid=(A_HEADS,),
        in_specs=[pl.BlockSpec((s, LANES), lambda h: (0, off + h)), pl.BlockSpec((A_CONV, LANES), lambda h: (0, off + h)),
                  pl.BlockSpec((None, s, LANES), lambda h: (0, 0, h)), pl.BlockSpec((None, s, LANES), lambda h: (1, 0, h))],
        out_specs=[pl.BlockSpec((s, LANES), lambda h: (0, h)), pl.BlockSpec((A_CONV, LANES), lambda h: (0, h))],
        out_shape=[jax.ShapeDtypeStruct((s, A_W), F32), jax.ShapeDtypeStruct((A_CONV, A_W), F32)],
        compiler_params=_cparams(("parallel",)))(proj, conv_w, dt2, dt2)


N_GATE = 2 * A_HEADS


def _gdn_gates_fn(ba, alog_row, dt_row):
    r = lax.broadcasted_iota(jnp.int32, (LANES, N_GATE * LANES), 0)
    c = lax.broadcasted_iota(jnp.int32, (LANES, N_GATE * LANES), 1) >> 7
    beta_b = xdot(_sigmoid(ba), (r == c).astype(F32))
    g = -(jnp.exp(alog_row) * _softplus(ba + dt_row))
    g_b = xdot(g, (r == c + N_GATE).astype(F32))
    return beta_b, g_b


def gdn_gates(ba, alog_row, dt_row):
    s = ba.shape[0]

    def body(ba_ref, al_ref, dt_ref, bb_ref, gb_ref):
        bb_ref[...], gb_ref[...] = _gdn_gates_fn(ba_ref[...], al_ref[...], dt_ref[...])

    row = pl.BlockSpec((ROWS, LANES), lambda i: (i, 0))
    vec = pl.BlockSpec((1, LANES), lambda i: (0, 0))
    wide = pl.BlockSpec((ROWS, N_GATE * LANES), lambda i: (i, 0))
    return pl.pallas_call(body, name="gdn_gates", grid=(s // ROWS,), in_specs=[row, vec, vec], out_specs=[wide, wide],
                          out_shape=[jax.ShapeDtypeStruct((s, N_GATE * LANES), F32)] * 2,
                          compiler_params=_cparams(("parallel",)))(ba, alog_row, dt_row)


def gdn_gates_bwd(ba, alog_row, dt_row, dbeta_b, dg_b):
    s = ba.shape[0]

    def body(ba_ref, al_ref, dt_ref, dbb_ref, dgb_ref, dba_ref, dal_ref, ddt_ref):
        _, vjp = jax.vjp(_gdn_gates_fn, ba_ref[...], al_ref[...], dt_ref[...])
        dba, dal, ddt = vjp((dbb_ref[...], dgb_ref[...]))
        dba_ref[...] = dba

        @pl.when(pl.program_id(0) == 0)
        def _():
            dal_ref[...] = jnp.zeros_like(dal_ref)
            ddt_ref[...] = jnp.zeros_like(ddt_ref)

        dal_ref[...] += dal
        ddt_ref[...] += ddt

    row = pl.BlockSpec((ROWS, LANES), lambda i: (i, 0))
    vec = pl.BlockSpec((1, LANES), lambda i: (0, 0))
    wide = pl.BlockSpec((ROWS, N_GATE * LANES), lambda i: (i, 0))
    return pl.pallas_call(body, name="gdn_gates_bwd", grid=(s // ROWS,), in_specs=[row, vec, vec, wide, wide],
                          out_specs=[row, vec, vec],
                          out_shape=[jax.ShapeDtypeStruct((s, LANES), F32), jax.ShapeDtypeStruct((1, LANES), F32),
                                     jax.ShapeDtypeStruct((1, LANES), F32)],
                          compiler_params=_cparams(("arbitrary",)))(ba, alog_row, dt_row, dbeta_b, dg_b)


def _post_fn(o, z, g):
    y = o * lax.rsqrt(jnp.mean(o * o, axis=-1, keepdims=True) + RMS_EPS) * g
    return y * (z * _sigmoid(z))


def mixer_post(o2, proj, norm_g, width, gate_off, name):
    s = o2.shape[1]
    nh = o2.shape[2] // width

    def body(o0_ref, o1_ref, z_ref, g_ref, y_ref):
        y_ref[...] = _post_fn(o0_ref[...] + o1_ref[...], z_ref[...], g_ref[...])

    ospec = lambda d: pl.BlockSpec((None, ROWS, width), lambda i, h: (d, i, h))
    return pl.pallas_call(
        body, name=name, grid=(s // ROWS, nh),
        in_specs=[ospec(0), ospec(1), pl.BlockSpec((ROWS, width), lambda i, h: (i, gate_off + h)),
                  pl.BlockSpec((1, width), lambda i, h: (0, 0))],
        out_specs=pl.BlockSpec((ROWS, width), lambda i, h: (i, h)),
        out_shape=jax.ShapeDtypeStruct((s, o2.shape[2]), F32),
        compiler_params=_cparams(("parallel", "parallel")))(o2, o2, proj, norm_g)


def mixer_post_bwd(o2, proj, norm_g, dy, width, gate_off, name):
    s = o2.shape[1]
    nh = o2.shape[2] // width

    def body(o0_ref, o1_ref, z_ref, g_ref, dy_ref, do_ref, dz_ref, dg_ref):
        _, vjp = jax.vjp(_post_fn, o0_ref[...] + o1_ref[...], z_ref[...], g_ref[...])
        do, dz, dg = vjp(dy_ref[...])
        do_ref[...] = do
        dz_ref[...] = dz

        @pl.when((pl.program_id(0) == 0) & (pl.program_id(1) == 0))
        def _():
            dg_ref[...] = jnp.zeros_like(dg_ref)

        dg_ref[...] += dg

    ospec = lambda d: pl.BlockSpec((None, ROWS, width), lambda i, h: (d, i, h))
    blk = pl.BlockSpec((ROWS, width), lambda i, h: (i, h))
    vec = pl.BlockSpec((1, width), lambda i, h: (0, 0))
    return pl.pallas_call(
        body, name=name, grid=(s // ROWS, nh),
        in_specs=[ospec(0), ospec(1), pl.BlockSpec((ROWS, width), lambda i, h: (i, gate_off + h)), vec, blk],
        out_specs=[blk, blk, vec],
        out_shape=[jax.ShapeDtypeStruct((s, o2.shape[2]), F32)] * 2 + [jax.ShapeDtypeStruct((1, width), F32)],
        compiler_params=_cparams(("arbitrary", "arbitrary")))(o2, o2, proj, norm_g, dy)


def _log_gate(z):
    return (jnp.minimum(z, 0.0) - jnp.log(1.0 + jnp.exp(-jnp.abs(z)))) * (1.0 / B_TAU)


def gla_gate(gl, w2, gb):
    s = gl.shape[0]

    def body(gl_ref, w_ref, b_ref, o_ref):
        for n in range(2):
            o_ref[n] = _log_gate(nn(gl_ref[...], w_ref[n]) + b_ref[n])

    full = lambda shp: pl.BlockSpec(shp, lambda i: (0,) * len(shp))
    return pl.pallas_call(
        body, name="gla_gate", grid=(s // ROWS,),
        in_specs=[pl.BlockSpec((ROWS, LANES), lambda i: (i, 0)), full(w2.shape), full(gb.shape)],
        out_specs=pl.BlockSpec((2, ROWS, B_KW), lambda i: (0, i, 0)),
        out_shape=jax.ShapeDtypeStruct((2, s, B_KW), F32), compiler_params=_cparams(("parallel",)))(gl, w2, gb)


def gla_gate_bwd(gl, w2, gb, dla):
    s = gl.shape[0]

    def body(gl_ref, w_ref, b_ref, dla_ref, dgl_ref, dz_ref, db0_ref, db1_ref):
        @pl.when(pl.program_id(0) == 0)
        def _():
            db0_ref[...] = jnp.zeros_like(db0_ref)
            db1_ref[...] = jnp.zeros_like(db1_ref)

        dgl = None
        for n, db_ref in enumerate((db0_ref, db1_ref)):
            _, vjp = jax.vjp(_log_gate, nn(gl_ref[...], w_ref[n]) + b_ref[n])
            dz, = vjp(dla_ref[n])
            dz_ref[n] = dz
            db_ref[...] += jnp.sum(dz, axis=0, keepdims=True)
            part = nt(dz, w_ref[n])
            dgl = part if dgl is None else dgl + part
        dgl_ref[...] = dgl

    full = lambda shp: pl.BlockSpec(shp, lambda i: (0,) * len(shp))
    row = pl.BlockSpec((ROWS, LANES), lambda i: (i, 0))
    wide = pl.BlockSpec((2, ROWS, B_KW), lambda i: (0, i, 0))
    vec = pl.BlockSpec((1, B_KW), lambda i: (0, 0))
    return pl.pallas_call(
        body, name="gla_gate_bwd", grid=(s // ROWS,),
        in_specs=[row, full(w2.shape), full(gb.shape), wide],
        out_specs=[row, wide, vec, vec],
        out_shape=[jax.ShapeDtypeStruct((s, LANES), F32), jax.ShapeDtypeStruct((2, s, B_KW), F32),
                   jax.ShapeDtypeStruct((1, B_KW), F32), jax.ShapeDtypeStruct((1, B_KW), F32)],
        compiler_params=_cparams(("arbitrary",)))(gl, w2, gb, dla)


PACK_TILE = 512


def cast_into_slot(x, chip):
    r, c = x.shape

    def body(chip_ref, x_ref, o_ref):
        o_ref[...] = x_ref[...].astype(BF16)

    return pl.pallas_call(
        body, name="cast_into_slot",
        grid_spec=pltpu.PrefetchScalarGridSpec(
            num_scalar_prefetch=1, grid=(r // PACK_TILE,),
            in_specs=[pl.BlockSpec((PACK_TILE, c), lambda i, chip_ref: (i, 0))],
            out_specs=pl.BlockSpec((None, PACK_TILE, c), lambda i, chip_ref: (chip_ref[0], i, 0))),
        out_shape=jax.ShapeDtypeStruct((4, r, c), BF16), compiler_params=_cparams(("parallel",)))(chip, x)


def sum_received(chip_sum, recv, chip, core):
    _, h, c = chip_sum.shape
    tr = _tile(h, PACK_TILE)
    nblk = h // tr

    def body(chip_ref, core_ref, own_ref, r_ref, o_ref):
        acc = r_ref[0].astype(F32)
        for k in range(1, 3):
            acc = acc + r_ref[k].astype(F32)
        o_ref[...] = acc + own_ref[...].astype(F32)

    return pl.pallas_call(
        body, name="sum_received",
        grid_spec=pltpu.PrefetchScalarGridSpec(
            num_scalar_prefetch=2, grid=(nblk,),
            in_specs=[pl.BlockSpec((None, tr, c), lambda i, chip_ref, core_ref: (chip_ref[0], i, 0)),
                      pl.BlockSpec((3, tr, c), lambda i, chip_ref, core_ref: (0, i, 0))],
            out_specs=pl.BlockSpec((tr, c), lambda i, chip_ref, core_ref: (core_ref[0] * nblk + i, 0))),
        out_shape=jax.ShapeDtypeStruct((2 * h, c), F32), compiler_params=_cparams(("parallel",)))(chip, core, chip_sum, recv)


def sum_slots(x, name):
    n, r, c = x.shape
    tr = _tile(r, PACK_TILE)

    def body(x_ref, o_ref):
        acc = x_ref[0].astype(F32)
        for k in range(1, n):
            acc = acc + x_ref[k].astype(F32)
        o_ref[...] = acc

    return pl.pallas_call(body, name=name, grid=(r // tr,), in_specs=[pl.BlockSpec((n, tr, c), lambda i: (0, i, 0))],
                          out_specs=pl.BlockSpec((tr, c), lambda i: (i, 0)),
                          out_shape=jax.ShapeDtypeStruct((r, c), F32), compiler_params=_cparams(("parallel",)))(x)


def add_sibling_half(gpack, theirs, core):
    n, r, c = gpack.shape
    half_rows = r // 2
    tr = _tile(half_rows, PACK_TILE)
    nblk = half_rows // tr

    def body(core_ref, g_ref, t_ref, o_ref):
        o_ref[...] = (g_ref[...] + t_ref[...]).astype(BF16)

    blk = pl.BlockSpec((None, tr, c), lambda s, i, core_ref: (s, i, 0))
    return pl.pallas_call(
        body, name="add_sibling_half",
        grid_spec=pltpu.PrefetchScalarGridSpec(
            num_scalar_prefetch=1, grid=(n, nblk),
            in_specs=[pl.BlockSpec((None, tr, c), lambda s, i, core_ref: (s, core_ref[0] * nblk + i, 0)), blk],
            out_specs=blk),
        out_shape=jax.ShapeDtypeStruct((n, half_rows, c), BF16),
        compiler_params=_cparams(("parallel", "parallel")))(core, gpack, theirs)


def adamw(w, m, v, grads, g_row_off, name):
    r, c = w.shape
    tr = next(t for t in (PACK_TILE, r) if r % t == 0 and g_row_off % t == 0)
    ob = g_row_off // tr
    ng = len(grads)

    def body(*refs):
        w_ref, m_ref, v_ref = refs[:3]
        g_refs = refs[3:3 + ng]
        g_ref, d_ref, nm_ref, nv_ref = refs[3 + ng:]
        g = g_refs[0][...]
        for gr in g_refs[1:]:
            g = g + gr[...]
        m_new = ADAM_B1 * m_ref[...] + (1.0 - ADAM_B1) * g
        v_new = ADAM_B2 * v_ref[...] + (1.0 - ADAM_B2) * jnp.square(g)
        m_hat = m_new / (1.0 - ADAM_B1 ** ADAM_STEP)
        v_hat = v_new / (1.0 - ADAM_B2 ** ADAM_STEP)
        g_ref[...] = g
        d_ref[...] = -ADAM_LR * (m_hat / (jnp.sqrt(v_hat) + ADAM_EPS) + ADAM_WD * w_ref[...])
        nm_ref[...] = m_new
        nv_ref[...] = v_new

    blk = pl.BlockSpec((tr, c), lambda i: (i, 0))
    gblk = pl.BlockSpec((tr, c), lambda i: (i + ob, 0))
    return pl.pallas_call(body, name=name, grid=(r // tr,), in_specs=[blk, blk, blk] + [gblk] * ng, out_specs=[blk] * 4,
                          out_shape=[jax.ShapeDtypeStruct((r, c), F32)] * 4,
                          compiler_params=_cparams(("parallel",)))(w, m, v, *grads)


MESH = pl.DeviceIdType.MESH
HBM = pl.BlockSpec(memory_space=pl.ANY)
CHIP_FLIPS = ((1, 0), (0, 1), (1, 1))


def _place():
    return lax.axis_index("x"), lax.axis_index("y"), lax.axis_index("c")


def allgather_chips(buf):
    _, r, c = buf.shape
    half_rows = r // 2

    def body(_, out_ref, send_sems, recv_sems):
        x, y, cc = _place()
        half = pl.ds(cc * half_rows, half_rows)
        other = pl.ds((1 - cc) * half_rows, half_rows)

        def copy(k, rows, to):
            return pltpu.make_async_remote_copy(src_ref=rows, dst_ref=rows, send_sem=send_sems.at[k],
                                                recv_sem=recv_sems.at[k], device_id=to, device_id_type=MESH)

        chips = [((1 - x if fx else x), (1 - y if fy else y)) for fx, fy in CHIP_FLIPS]
        first = [copy(k, out_ref.at[2 * x + y, half], (px, py, cc)) for k, (px, py) in enumerate(chips)]
        for cp in first:
            cp.start()
        passed = []
        for k, (px, py) in enumerate(chips):
            landed = out_ref.at[2 * px + py, half]
            copy(k, landed, (px, py, cc)).wait_recv()
            passed.append(copy(3 + k, landed, (x, y, 1 - cc)))
            passed[-1].start()
        for k, (px, py) in enumerate(chips):
            copy(3 + k, out_ref.at[2 * px + py, other], (x, y, 1 - cc)).wait_recv()
        for cp in first + passed:
            cp.wait_send()

    return pl.pallas_call(
        body, name="allgather_chips", in_specs=[HBM], out_specs=HBM, input_output_aliases={0: 0},
        out_shape=jax.ShapeDtypeStruct(buf.shape, buf.dtype),
        scratch_shapes=[pltpu.SemaphoreType.DMA((6,)), pltpu.SemaphoreType.DMA((6,))],
    )(buf)


def scatter_chips(gpack):
    _, r, c = gpack.shape

    def body(src_ref, out_ref, send_sems, recv_sems):
        x, y, cc = _place()
        sends = []
        for k, (fx, fy) in enumerate(CHIP_FLIPS):
            px, py = (1 - x if fx else x), (1 - y if fy else y)
            sends.append(pltpu.make_async_remote_copy(
                src_ref=src_ref.at[2 * px + py], dst_ref=out_ref.at[k], send_sem=send_sems.at[k], recv_sem=recv_sems.at[k],
                device_id=(px, py, cc), device_id_type=MESH))
        for cp in sends:
            cp.start()
        for cp in sends:
            cp.wait_recv()
        for cp in sends:
            cp.wait_send()

    return pl.pallas_call(
        body, name="scatter_chips", in_specs=[HBM], out_specs=HBM,
        out_shape=jax.ShapeDtypeStruct((3, r, c), gpack.dtype),
        scratch_shapes=[pltpu.SemaphoreType.DMA((3,)), pltpu.SemaphoreType.DMA((3,))],
    )(gpack)


def sibling_halves(gpack):
    n, r, c = gpack.shape
    half_rows = r // 2

    def body(src_ref, out_ref, send_sem, recv_sem):
        x, y, cc = _place()
        cp = pltpu.make_async_remote_copy(
            src_ref=src_ref.at[:, pl.ds((1 - cc) * half_rows, half_rows)], dst_ref=out_ref, send_sem=send_sem,
            recv_sem=recv_sem, device_id=(x, y, 1 - cc), device_id_type=MESH)
        cp.start()
        cp.wait()

    return pl.pallas_call(
        body, name="sibling_halves", in_specs=[HBM], out_specs=HBM,
        out_shape=jax.ShapeDtypeStruct((n, half_rows, c), gpack.dtype),
        scratch_shapes=[pltpu.SemaphoreType.DMA, pltpu.SemaphoreType.DMA],
    )(gpack)


def join_halves(buf):
    r, c = buf.shape
    half_rows = r // 2

    def body(_, out_ref, send_sem, recv_sem):
        x, y, cc = _place()
        half = out_ref.at[pl.ds(cc * half_rows, half_rows)]
        other = out_ref.at[pl.ds((1 - cc) * half_rows, half_rows)]
        send = pltpu.make_async_remote_copy(src_ref=half, dst_ref=half, send_sem=send_sem, recv_sem=recv_sem,
                                            device_id=(x, y, 1 - cc), device_id_type=MESH)
        send.start()
        pltpu.make_async_remote_copy(src_ref=other, dst_ref=other, send_sem=send_sem, recv_sem=recv_sem,
                                     device_id=(x, y, 1 - cc), device_id_type=MESH).wait_recv()
        send.wait_send()

    return pl.pallas_call(
        body, name="join_halves", in_specs=[HBM], out_specs=HBM, input_output_aliases={0: 0},
        out_shape=jax.ShapeDtypeStruct(buf.shape, buf.dtype),
        scratch_shapes=[pltpu.SemaphoreType.DMA, pltpu.SemaphoreType.DMA],
    )(buf)


def exchange_all(v, name):
    r, c = v.shape

    def body(v_ref, out_ref, send_sems, recv_sems):
        x, y, cc = _place()
        out_ref[4 * x + 2 * y + cc] = v_ref[...]
        sends, recvs = [], []
        for k in range(1, 8):
            px = 1 - x if k & 4 else x
            py = 1 - y if k & 2 else y
            pc = 1 - cc if k & 1 else cc
            sends.append(pltpu.make_async_remote_copy(
                src_ref=v_ref, dst_ref=out_ref.at[4 * x + 2 * y + cc], send_sem=send_sems.at[k - 1],
                recv_sem=recv_sems.at[k - 1], device_id=(px, py, pc), device_id_type=MESH))
            recvs.append(pltpu.make_async_remote_copy(
                src_ref=v_ref, dst_ref=out_ref.at[4 * px + 2 * py + pc], send_sem=send_sems.at[k - 1],
                recv_sem=recv_sems.at[k - 1], device_id=(px, py, pc), device_id_type=MESH))
        for cp in sends:
            cp.start()
        for cp in recvs:
            cp.wait_recv()
        for cp in sends:
            cp.wait_send()

    vm = pl.BlockSpec(memory_space=pltpu.VMEM)
    return pl.pallas_call(
        body, name=name, in_specs=[vm], out_specs=vm, out_shape=jax.ShapeDtypeStruct((8, r, c), v.dtype),
        scratch_shapes=[pltpu.SemaphoreType.DMA((7,)), pltpu.SemaphoreType.DMA((7,))],
        compiler_params=pltpu.CompilerParams(vmem_limit_bytes=VMEM_LIMIT),
    )(v)


def _as_rows(a, width):
    n = math.prod(a.shape)
    if n % width == 0:
        return a.reshape(-1, width)
    return jnp.pad(a.reshape(1, -1), ((0, 0), (0, -n % width))).reshape(-1, width)


def _n_rows(shape, width):
    return -(-math.prod(shape) // width)


def _pack_rows(arrays, rows, width):
    parts = [_as_rows(a, width) for a in arrays]
    used = sum(p.shape[0] for p in parts)
    return jnp.concatenate(parts + [jnp.zeros((rows - used, width), arrays[0].dtype)], axis=0)


def _unpack_rows(pack, shapes):
    width = pack.shape[1]
    out, off = [], 0
    for shp in shapes:
        nr, n = _n_rows(shp, width), math.prod(shp)
        part = pack[off:off + nr]
        out.append(part.reshape(shp) if n % width == 0 else part.reshape(-1)[:n].reshape(shp))
        off += nr
    return out


def _rows_for(shapes, width, mult=8):
    n = sum(_n_rows(s, width) for s in shapes)
    return -(-n // mult) * mult


def _gdn_fwd(x, p):
    proj = mm(x, p["w_main"], name="gdn_proj")
    ba = mm(x, p["w_gate"], name="gdn_proj_gate")
    q, k, v = (gdn_pre(proj, p["conv"], kind) for kind in "qkv")
    beta_b, g_b = gdn_gates(ba, p["alog_row"], p["dt_row"])
    o2, st = gdn_rec_fwd(q, k, v, beta_b, g_b)
    y = mixer_post(o2, proj, p["norm_g"], A_DK, 3 * A_HEADS, "gdn_post")
    m = mm(y, p["w_out"], name="gdn_out")
    return m, (x, proj, ba, q, k, v, beta_b, g_b, o2, st, y)


def _gdn_bwd(saved, p, dm):
    x, proj, ba, q, k, v, beta_b, g_b, o2, st, y = saved
    d_w_out = mm(y, dm, "tn", name="gdn_dw_out")
    dy = mm(dm, p["w_out"], "nt", name="gdn_dy")
    do, dz, d_norm_g = mixer_post_bwd(o2, proj, p["norm_g"], dy, A_DK, 3 * A_HEADS, "gdn_post_bwd")
    dq2, dk2, dv2, dbb, dgb = gdn_rec_bwd(q, k, v, beta_b, g_b, st, do)
    dba, d_alog_row, d_dt_row = gdn_gates_bwd(ba, p["alog_row"], p["dt_row"], dbb, dgb)
    du, dconv = zip(*(gdn_pre_bwd(proj, p["conv"], d2, kind) for d2, kind in ((dq2, "q"), (dk2, "k"), (dv2, "v"))))
    dproj = jnp.concatenate(list(du) + [dz], axis=1)
    d_w_main = mm(x, dproj, "tn", name="gdn_dw_main")
    d_w_gate = mm(x, dba, "tn", name="gdn_dw_gate")
    dx = mm(dba, p["w_gate"], "nt", epi="add", extra=dm, alpha=ALPHA, name="gdn_dx_gate")
    dx = mm(dproj, p["w_main"], "nt", epi="add", extra=dx, name="gdn_dx")
    grads = dict(w_in=jnp.concatenate([d_w_main, d_w_gate[:, :2 * N_GATE]], axis=1), conv=jnp.concatenate(dconv, axis=1),
                 alog=d_alog_row[0, N_GATE:2 * N_GATE].reshape(2, A_HEADS), dt=d_dt_row[0, N_GATE:2 * N_GATE].reshape(2, A_HEADS),
                 norm_g=d_norm_g[0], w_out=d_w_out)
    return dx, grads


def _gla_fwd(x, p):
    proj = mm(x, p["w_main"], name="gla_proj")
    gl = mm(x, p["w_gate"], name="gla_proj_gate")
    log_a = gla_gate(gl, p["w2"], p["gate_b"])
    o2, st = gla_rec_fwd(proj, log_a)
    y = mixer_post(o2, proj, p["norm_g"], B_DV, (2 * B_KW + B_VW) // B_DV, "gla_post")
    m = mm(y, p["w_out"], name="gla_out")
    return m, (x, proj, gl, log_a, o2, st, y)


def _gla_bwd(saved, p, dm):
    x, proj, gl, log_a, o2, st, y = saved
    d_w_out = mm(y, dm, "tn", name="gla_dw_out")
    dy = mm(dm, p["w_out"], "nt", name="gla_dy")
    do, dr, d_norm_g = mixer_post_bwd(o2, proj, p["norm_g"], dy, B_DV, (2 * B_KW + B_VW) // B_DV, "gla_post_bwd")
    dq2, dk2, dv2, dla = gla_rec_bwd(proj, log_a, st, do)
    dgl, dz, d_b0, d_b1 = gla_gate_bwd(gl, p["w2"], p["gate_b"], dla)
    d_w2 = [mm(gl, dz[n], "tn", name="gla_dw_gate_w2") for n in range(2)]
    dproj = jnp.concatenate([dq2[0] + dq2[1], dk2[0] + dk2[1], dv2[0] + dv2[1], dr], axis=1)
    d_w_main = mm(x, dproj, "tn", name="gla_dw_main")
    d_w_gate = mm(x, dgl, "tn", name="gla_dw_gate")
    dx = mm(dgl, p["w_gate"], "nt", epi="add", extra=dm, alpha=ALPHA, name="gla_dx_gate")
    dx = mm(dproj, p["w_main"], "nt", epi="add", extra=dx, name="gla_dx")
    grads = dict(w_in=jnp.concatenate([d_w_main, d_w_gate[:, :2 * B_RANK]], axis=1),
                 gate_w2=jnp.stack([d_w2[n][n * B_RANK:(n + 1) * B_RANK] for n in range(2)]),
                 gate_b=jnp.concatenate([d_b0, d_b1]), norm_g=d_norm_g[0], w_out=d_w_out)
    return dx, grads


def _pad_cols(w, width=LANES):
    return jnp.pad(w, ((0, 0), (0, width - w.shape[1])))


def _local_step(x, target, a_w_in, a_conv, a_alog, a_dt_bias, a_norm_g, a_w_out, b_w_in, b_gate_w2, b_gate_b, b_norm_g,
                b_w_out, ln1_g, ln1_b, mlp_w1, mlp_w2, ln2_g, ln2_b):
    layer_p = []
    for i in range(DEPTH):
        j = i // 2
        if i % 2 == 0:
            layer_p.append(dict(
                w_main=a_w_in[j][:, :4 * A_W], w_gate=_pad_cols(a_w_in[j][:, 4 * A_W:]), conv=a_conv[j],
                alog_row=jnp.pad(a_alog[j].reshape(1, N_GATE), ((0, 0), (N_GATE, LANES - 2 * N_GATE))),
                dt_row=jnp.pad(a_dt_bias[j].reshape(1, N_GATE), ((0, 0), (N_GATE, LANES - 2 * N_GATE))),
                norm_g=a_norm_g[j].reshape(1, A_DK), w_out=a_w_out[j]))
        else:
            w2 = jnp.stack([jnp.pad(b_gate_w2[j][n], ((n * B_RANK, LANES - (n + 1) * B_RANK), (0, 0))) for n in range(2)])
            layer_p.append(dict(
                w_main=b_w_in[j][:, :2 * B_KW + 2 * B_VW], w_gate=_pad_cols(b_w_in[j][:, 2 * B_KW + 2 * B_VW:]),
                w2=w2, gate_b=b_gate_b[j].reshape(2, 1, B_KW), norm_g=b_norm_g[j].reshape(1, B_DV), w_out=b_w_out[j]))

    saved = []
    h = x
    for i in range(DEPTH):
        p = layer_p[i]
        m, sv = (_gdn_fwd if i % 2 == 0 else _gla_fwd)(h, p)
        x1 = ln_fwd(h, m, ln1_g[i:i + 1], ln1_b[i:i + 1])
        h1 = mm(x1, mlp_w1[i], name="mlp_up")
        mlp = mm(h1, mlp_w2[i], act="sqrelu", name="mlp_down")
        x2 = ln_fwd(x1, mlp, ln2_g[i:i + 1], ln2_b[i:i + 1])
        saved.append((sv, h, m, x1, h1, mlp))
        h = x2

    dh, loss_part = loss_head(h, target)

    g_a, g_b, g_ln1g, g_ln1b, g_ln2g, g_ln2b, g_w1, g_w2 = {}, {}, {}, {}, {}, {}, {}, {}
    for i in reversed(range(DEPTH)):
        sv, xin, m, x1, h1, mlp = saved[i]
        p = layer_p[i]
        dr2, g_ln2g[i], g_ln2b[i] = ln_bwd(x1, mlp, ln2_g[i:i + 1], dh)
        g_w2[i] = mm(h1, dr2, "tn", act="sqrelu", name="mlp_dw_down")
        dh1 = mm(dr2, mlp_w2[i], "nt", epi="dsqrelu", extra=h1, name="mlp_dh")
        g_w1[i] = mm(x1, dh1, "tn", chip_major=True, name="mlp_dw_up")
        dx1 = mm(dh1, mlp_w1[i], "nt", epi="add", extra=dr2, alpha=ALPHA, name="mlp_dx")
        dr1, g_ln1g[i], g_ln1b[i] = ln_bwd(xin, m, ln1_g[i:i + 1], dx1)
        dh, g = (_gdn_bwd if i % 2 == 0 else _gla_bwd)(sv, p, dr1)
        (g_a if i % 2 == 0 else g_b)[i // 2] = g

    per_layer = lambda d, key=None: [(d[i] if key is None else d[i][key]) for i in sorted(d)]
    st = lambda d, key=None: jnp.stack(per_layer(d, key))
    grads = dict(
        a_w_in=per_layer(g_a, "w_in"), a_conv=st(g_a, "conv"), a_alog=st(g_a, "alog"), a_dt_bias=st(g_a, "dt"),
        a_norm_g=st(g_a, "norm_g"), a_w_out=per_layer(g_a, "w_out"), b_w_in=per_layer(g_b, "w_in"),
        b_gate_w2=st(g_b, "gate_w2"), b_gate_b=st(g_b, "gate_b"), b_norm_g=st(g_b, "norm_g"),
        b_w_out=per_layer(g_b, "w_out"), ln1_g=st(g_ln1g)[:, 0], ln1_b=st(g_ln1b)[:, 0], mlp_w1=per_layer(g_w1),
        mlp_w2=per_layer(g_w2), ln2_g=st(g_ln2g)[:, 0], ln2_b=st(g_ln2b)[:, 0])
    return loss_part, dh, grads


WEIGHTS = ("a_w_in", "a_conv", "a_alog", "a_dt_bias", "a_norm_g", "a_w_out", "b_w_in", "b_gate_w2", "b_gate_b",
           "b_norm_g", "b_w_out", "ln1_g", "ln1_b", "mlp_w1", "mlp_w2", "ln2_g", "ln2_b")
BIG = ("mlp_w1", "mlp_w2", "a_w_out", "b_w_out", "a_w_in", "b_w_in")
SHARD_AXIS = {"mlp_w1": 2, "mlp_w2": 1, "a_w_out": 1, "b_w_out": 1, "a_w_in": 2, "b_w_in": 2}
SMALL = tuple(n for n in WEIGHTS if n not in BIG)
SMALL_SHARD_AXIS = {"a_conv": 2, "b_gate_w2": 3, "b_gate_b": 2, "b_norm_g": 1}


def _to_chip_major(full, axis):
    shp = full.shape
    t = full.reshape(shp[:axis] + (4, shp[axis] // 4) + shp[axis + 1:])
    return jnp.moveaxis(t, axis, 0)


def _from_chip_major(stacked, axis):
    t = jnp.moveaxis(stacked, 0, axis)
    shp = t.shape
    return t.reshape(shp[:axis] + (shp[axis] * shp[axis + 1],) + shp[axis + 2:])


def kernel(x, a_w_in, a_conv, a_alog, a_dt_bias, a_norm_g, a_w_out, b_w_in, b_gate_w2, b_gate_b, b_norm_g, b_w_out, ln1_g, ln1_b, mlp_w1, mlp_w2, ln2_g, ln2_b, loss_target, m_a_w_in, m_a_conv, m_a_alog, m_a_dt_bias, m_a_norm_g, m_a_w_out, m_b_w_in, m_b_gate_w2, m_b_gate_b, m_b_norm_g, m_b_w_out, m_ln1_g, m_ln1_b, m_mlp_w1, m_mlp_w2, m_ln2_g, m_ln2_b, v_a_w_in, v_a_conv, v_a_alog, v_a_dt_bias, v_a_norm_g, v_a_w_out, v_b_w_in, v_b_gate_w2, v_b_gate_b, v_b_norm_g, v_b_w_out, v_ln1_g, v_ln1_b, v_mlp_w1, v_mlp_w2, v_ln2_g, v_ln2_b):
    w = dict(a_w_in=a_w_in, a_conv=a_conv, a_alog=a_alog, a_dt_bias=a_dt_bias, a_norm_g=a_norm_g, a_w_out=a_w_out,
             b_w_in=b_w_in, b_gate_w2=b_gate_w2, b_gate_b=b_gate_b, b_norm_g=b_norm_g, b_w_out=b_w_out, ln1_g=ln1_g,
             ln1_b=ln1_b, mlp_w1=mlp_w1, mlp_w2=mlp_w2, ln2_g=ln2_g, ln2_b=ln2_b)
    mom = dict(a_w_in=m_a_w_in, a_conv=m_a_conv, a_alog=m_a_alog, a_dt_bias=m_a_dt_bias, a_norm_g=m_a_norm_g,
               a_w_out=m_a_w_out, b_w_in=m_b_w_in, b_gate_w2=m_b_gate_w2, b_gate_b=m_b_gate_b, b_norm_g=m_b_norm_g,
               b_w_out=m_b_w_out, ln1_g=m_ln1_g, ln1_b=m_ln1_b, mlp_w1=m_mlp_w1, mlp_w2=m_mlp_w2, ln2_g=m_ln2_g,
               ln2_b=m_ln2_b)
    var = dict(a_w_in=v_a_w_in, a_conv=v_a_conv, a_alog=v_a_alog, a_dt_bias=v_a_dt_bias, a_norm_g=v_a_norm_g,
               a_w_out=v_a_w_out, b_w_in=v_b_w_in, b_gate_w2=v_b_gate_w2, b_gate_b=v_b_gate_b, b_norm_g=v_b_norm_g,
               b_w_out=v_b_w_out, ln1_g=v_ln1_g, ln1_b=v_ln1_b, mlp_w1=v_mlp_w1, mlp_w2=v_mlp_w2, ln2_g=v_ln2_g,
               ln2_b=v_ln2_b)
    chip = 2 * lax.axis_index("x") + lax.axis_index("y")

    seg_rows = [w[n].size // D_MODEL for n in BIG]
    seg_off = [sum(seg_rows[:i]) for i in range(len(BIG))]
    rows = -(-sum(seg_rows) // PACK_TILE) * PACK_TILE
    shard_pack = jnp.concatenate([w[n].reshape(-1, D_MODEL) for n in BIG]
                                 + [jnp.zeros((rows - sum(seg_rows), D_MODEL), F32)], axis=0)
    chip_idx = chip.astype(jnp.int32).reshape(1)
    gathered = allgather_chips(cast_into_slot(shard_pack, chip_idx))
    full = {}
    for n, off, nr in zip(BIG, seg_off, seg_rows):
        if n in ("mlp_w1", "mlp_w2"):
            kind = "cols" if SHARD_AXIS[n] == 2 else "rows"
            full[n] = [Gathered(gathered, off + i * D_MODEL, kind) for i in range(DEPTH)]
            continue
        stacked = gathered[:, off:off + nr].reshape((4,) + w[n].shape)
        full[n] = _from_chip_major(stacked, SHARD_AXIS[n])
    sharded_small = tuple(SMALL_SHARD_AXIS)
    sm_shapes = [w[n].shape for n in sharded_small]
    sm_rows = _rows_for(sm_shapes, LANES)
    sm_all = exchange_all(_pack_rows([w[n] for n in sharded_small], sm_rows, LANES), "gather_small")
    per_chip = [_unpack_rows(sm_all[2 * pch], sm_shapes) for pch in range(4)]
    for idx, n in enumerate(sharded_small):
        full[n] = jnp.concatenate([per_chip[pch][idx] for pch in range(4)], axis=SMALL_SHARD_AXIS[n])
    for n in WEIGHTS:
        full.setdefault(n, w[n])

    loss_part, grad_x, grads = _local_step(x[0], loss_target[0], *[full[n] for n in WEIGHTS])
    loss = lax.psum(jnp.sum(loss_part), ("x", "y", "c"))

    gpack = jnp.concatenate(
        [(g if n == "mlp_w1" else _to_chip_major(g, SHARD_AXIS[n] - 1).reshape(4, -1, D_MODEL))
         for n in BIG for g in grads[n]]
        + [jnp.zeros((4, rows - sum(seg_rows), D_MODEL), F32)], axis=1)
    core = lax.axis_index("c").astype(jnp.int32).reshape(1)
    chip_sum = add_sibling_half(gpack, sibling_halves(gpack), core)
    reduced = join_halves(sum_received(chip_sum, scatter_chips(chip_sum), chip_idx, core))
    out_g, out_d, out_m, out_v = {}, {}, {}, {}
    for n, off, nr in zip(BIG, seg_off, seg_rows):
        if w[n].shape[-1] == D_MODEL:
            view = lambda t: t.reshape(-1, D_MODEL)
            res = adamw(view(w[n]), view(mom[n]), view(var[n]), (reduced,), off, "adamw_" + n)
        else:
            cols = w[n].shape[-1]
            view = lambda t: t.reshape(-1, cols)
            res = adamw(view(w[n]), view(mom[n]), view(var[n]), (view(reduced[off:off + nr]),), 0, "adamw_" + n)
        out_g[n], out_d[n], out_m[n], out_v[n] = (t.reshape(w[n].shape) for t in res)

    all_shapes = [full[n].shape for n in SMALL]
    g_rows = _rows_for(all_shapes, LANES)
    g_all = exchange_all(_pack_rows([grads[n] for n in SMALL], g_rows, LANES), "gather_small_grads")
    g_sum = _unpack_rows(sum_slots(g_all, "sum_small_grads"), all_shapes)
    g_mine = []
    for n, g in zip(SMALL, g_sum):
        if n in SMALL_SHARD_AXIS:
            ax = SMALL_SHARD_AXIS[n]
            g = lax.dynamic_slice_in_dim(g, chip * w[n].shape[ax], w[n].shape[ax], axis=ax)
        g_mine.append(g)
    my_shapes = [w[n].shape for n in SMALL]
    s_rows = _rows_for(my_shapes, LANES)
    pk = lambda d: _pack_rows([d[n] for n in SMALL], s_rows, LANES)
    res = adamw(pk(w), pk(mom), pk(var), (_pack_rows(g_mine, s_rows, LANES),), 0, "adamw_small")
    for dst, pack in zip((out_g, out_d, out_m, out_v), res):
        for n, t in zip(SMALL, _unpack_rows(pack, my_shapes)):
            dst[n] = t

    return (loss, grad_x[None], *[out_g[n] for n in WEIGHTS], *[out_d[n] for n in WEIGHTS],
            *[out_m[n] for n in WEIGHTS], *[out_v[n] for n in WEIGHTS])
```

```python
import functools
import math

import jax
import jax.numpy as jnp
from jax import lax
from jax.experimental import pallas as pl
from jax.experimental.pallas import tpu as pltpu

F32 = jnp.float32
BF16 = jnp.bfloat16

D_MODEL = 1024
DEPTH = 4
CHUNK = 64
A_HEADS = 8
A_DK = 128
A_W = 1024
A_CONV = 5
B_HEADS = 4
B_DK = 128
B_DV = 256
B_RANK = 16
B_TAU = 16.0
B_KW = 512
B_VW = 1024
ALPHA = (2 * DEPTH) ** 0.25
LN_EPS = 1e-5
RMS_EPS = 1e-6
L2_EPS = 1e-6
ADAM_LR = 0.001
ADAM_B1 = 0.9
ADAM_B2 = 0.999
ADAM_EPS = 1e-08
ADAM_WD = 0.01
ADAM_STEP = 10
LANES = 128
NEG_INF = float("-inf")
VMEM_LIMIT = 56 * 1024 * 1024


def _cparams(sem=None):
    return pltpu.CompilerParams(dimension_semantics=sem, vmem_limit_bytes=VMEM_LIMIT)


def _dg(a, b, ca, cb):
    return lax.dot_general(a.astype(BF16), b.astype(BF16), (((ca,), (cb,)), ((), ())),
                           preferred_element_type=F32)


def _split(x):
    hi = x.astype(BF16)
    return hi, (x - hi.astype(F32)).astype(BF16)


def _dg3(a, b, ca, cb):
    (a1, a2), (b1, b2) = _split(a), _split(b)
    return (_dg(a1, b2, ca, cb) + _dg(a2, b1, ca, cb)) + _dg(a1, b1, ca, cb)


def _dot_with_vjp(dg):
    @functools.partial(jax.custom_vjp, nondiff_argnums=(2, 3))
    def dot(a, b, ca, cb):
        return dg(a, b, ca, cb)

    def fwd(a, b, ca, cb):
        return dg(a, b, ca, cb), (a, b)

    def bwd(ca, cb, res, g):
        a, b = res
        da = dg(g, b, 1, 1 - cb) if ca == 1 else dg(b, g, 1 - cb, 1)
        db = dg(a, g, 1 - ca, 0) if cb == 0 else dg(g, a, 0, 1 - ca)
        return da, db

    dot.defvjp(fwd, bwd)
    return dot


bdot = _dot_with_vjp(_dg)
xdot3 = _dot_with_vjp(_dg3)


def nn(a, b):
    return bdot(a, b, 1, 0)


def nt(a, b):
    return bdot(a, b, 1, 1)


def tn(a, b):
    return bdot(a, b, 0, 0)


def xdot(a, b):
    return xdot3(a, b, 1, 0)


def _sigmoid(x):
    return 1.0 / (1.0 + jnp.exp(-x))


def _softplus(x):
    return jnp.maximum(x, 0.0) + jnp.log(1.0 + jnp.exp(-jnp.abs(x)))


def _chunk_masks(rev):
    ii = lax.broadcasted_iota(jnp.int32, (CHUNK, CHUNK), 0)
    jj = lax.broadcasted_iota(jnp.int32, (CHUNK, CHUNK), 1)
    d = (ii - jj) * (1 - 2 * rev)
    return d >= 0, d > 0, ii == jj, (ii >> 3) == (jj >> 3)


def _each(f, *lists):
    return [f(*xs) for xs in zip(*lists)]


@jax.custom_vjp
def _unit_triangular_inverse(a, ident, blockdiag):
    return _unit_triangular_inverse_impl(a, ident, blockdiag)


def _unit_triangular_inverse_fwd(a, ident, blockdiag):
    t = _unit_triangular_inverse_impl(a, ident, blockdiag)
    return t, (t, ident, blockdiag)


def _unit_triangular_inverse_bwd(res, g):
    t, ident, blockdiag = res
    left = _each(lambda x, y: xdot3(x, y, 0, 0), t, g)
    da = _each(lambda x, y: -xdot3(x, y, 1, 1), left, t)
    return da, jnp.zeros_like(ident), jnp.zeros_like(blockdiag)


_unit_triangular_inverse.defvjp(_unit_triangular_inverse_fwd, _unit_triangular_inverse_bwd)


@jax.custom_vjp
def _known_inverse(a, t):
    return t


def _known_inverse_bwd(t, g):
    left = _each(lambda x, y: xdot3(x, y, 0, 0), t, g)
    return _each(lambda x, y: -xdot3(x, y, 1, 1), left, t), _each(jnp.zeros_like, t)


_known_inverse.defvjp(lambda a, t: (t, t), _known_inverse_bwd)


def _unit_triangular_inverse_impl(a, ident, blockdiag):
    ad = _each(lambda x: x * blockdiag, a)
    e = _each(lambda x, y: x - y, a, ad)
    dinv = _each(lambda x: ident - x, ad)
    p = _each(xdot, ad, ad)
    dinv = _each(lambda x, y: x + xdot(x, y), dinv, p)
    p = _each(xdot, p, p)
    dinv = _each(lambda x, y: x + xdot(x, y), dinv, p)
    g = _each(lambda x, y: -xdot(x, y), dinv, e)
    finv = _each(lambda x: ident + x, g)
    p = _each(xdot, g, g)
    finv = _each(lambda x, y: x + xdot(x, y), finv, p)
    p = _each(xdot, p, p)
    finv = _each(lambda x, y: x + xdot(x, y), finv, p)
    return _each(xdot, finv, dinv)


def _gdn_step(state, q, k, v, bb, gb, rev, t_saved=None):
    causal, strict, eye, blockdiag = _chunk_masks(rev)
    lower = causal.astype(F32)
    ones = jnp.ones((CHUNK, CHUNK), F32)
    gcb = _each(lambda x: xdot(lower, x), gb)
    gcol = _each(lambda x: x[:, :CHUNK], gcb)
    grow = _each(lambda x: xdot(ones, jnp.where(eye, x, 0.0)), gcol)
    decay = _each(lambda x, y: jnp.exp(jnp.where(causal, x - y, NEG_INF)), gcol, grow)
    kb = _each(lambda x, y: x * y, k, bb)
    a = _each(lambda x, y, z: jnp.where(strict, nt(x, y) * z, 0.0), kb, k, decay)
    if t_saved is None:
        t = _unit_triangular_inverse(a, eye.astype(F32), blockdiag.astype(F32))
    else:
        t = _known_inverse(a, t_saved)
    egc = _each(jnp.exp, gcb)
    u = _each(lambda x, y, z: xdot(x, y * z), t, v, bb)
    w = _each(lambda x, y, z: xdot(x, y * z), t, kb, egc)
    qk = _each(lambda x, y, z: nt(x, y) * z, q, k, decay)
    glast = _each(lambda x: jnp.sum(x, axis=0, keepdims=True), gb)
    v_new = _each(lambda x, y, z: x - nn(y, z), u, w, state)
    o = _each(lambda x, y, z, p, r: nn(x * y, z) + nn(p, r), q, egc, state, qk, v_new)
    k_dec = _each(lambda x, y, z: x * jnp.exp(y - z), k, glast, gcb)
    state_new = _each(lambda x, y, z, p: x * jnp.exp(y) + tn(z, p), state, glast, k_dec, v_new)
    return state_new, o, t


def _gla_step(state_t, q, k, v, la, rev):
    causal, _, _, _ = _chunk_masks(rev)
    lower = causal.astype(F32)
    sign = 1 - 2 * rev
    b = _each(lambda x: xdot(lower, x), la)
    q = _each(lambda x: x * (B_DK ** -0.5), q)
    row = lax.broadcasted_iota(jnp.int32, (CHUNK, B_DK), 0)
    sub = row // GLA_SUB
    scores = None
    for blk in range(CHUNK // GLA_SUB):
        r_at = jnp.where(rev == 1, GLA_SUB * (blk + 1), GLA_SUB * blk - 1)
        r = _each(lambda x: jnp.sum(jnp.where(row == r_at, x, 0.0), axis=0, keepdims=True), b)
        q_blk = _each(lambda x, y, z: x * jnp.exp(jnp.where(sub == blk, y - z, NEG_INF)), q, b, r)
        k_past = _each(lambda x, y, z: x * jnp.exp(jnp.where((sub - blk) * sign < 0, z - y, NEG_INF)), k, b, r)
        part = _each(lambda x, y: xdot3(x, y, 1, 1), q_blk, k_past)
        scores = part if scores is None else _each(lambda x, y: x + y, scores, part)
    shp = (GLA_SUB, GLA_SUB, B_DK)
    d3 = (lax.broadcasted_iota(jnp.int32, shp, 0) - lax.broadcasted_iota(jnp.int32, shp, 1)) * sign
    place_r = lax.broadcasted_iota(jnp.int32, (GLA_SUB, CHUNK), 0)
    place_c = lax.broadcasted_iota(jnp.int32, (GLA_SUB, CHUNK), 1)
    diag = []
    for blk in range(CHUNK // GLA_SUB):
        rows = slice(blk * GLA_SUB, (blk + 1) * GLA_SUB)
        place = (place_c == place_r + blk * GLA_SUB).astype(F32)

        def pairs(qh, kh, bh):
            qb, kb, bb = qh[rows], kh[rows], bh[rows]
            dec = jnp.exp(jnp.where(d3 >= 0, bb[:, None, :] - bb[None, :, :], NEG_INF))
            return xdot(jnp.sum(qb[:, None, :] * kb[None, :, :] * dec, axis=-1), place)

        diag.append(_each(pairs, q, k, b))
    scores = _each(lambda x, *d: x + jnp.concatenate(d, axis=0), scores, *diag)
    blast = _each(lambda x: jnp.sum(x, axis=0, keepdims=True), la)
    o = _each(lambda x, y, z, s, w: nt(x * jnp.exp(y), z) + nn(s, w), q, b, state_t, scores, v)
    k_dec = _each(lambda x, y, z: x * jnp.exp(y - z), k, blast, b)
    state_new = _each(lambda x, y, z, w: jnp.exp(x) * y + tn(z, w), blast, state_t, v, k_dec)
    return state_new, o


def _chunk_pos(d, m, n):
    return m + d * (n - 1 - 2 * m)


GLA_SUB = 16
GDN_HEADS_PER_STEP = 8
def gdn_rec_fwd(q, k, v, beta_b, g_b):
    s = q.shape[0]
    n = s // CHUNK

    hb = GDN_HEADS_PER_STEP
    wide = hb * LANES

    def body(q_ref, k_ref, v_ref, bb_ref, gb_ref, o_ref, st_ref, t_ref, state):
        d = pl.program_id(0)

        @pl.when(pl.program_id(2) == 0)
        def _():
            state[...] = jnp.zeros_like(state)

        cols = [slice(hh * LANES, (hh + 1) * LANES) for hh in range(hb)]
        st = [state[hh] for hh in range(hb)]
        new, o, t = _gdn_step(st, *([r[:, c] for c in cols] for r in (q_ref, k_ref, v_ref, bb_ref, gb_ref)), d)
        for hh, c in enumerate(cols):
            st_ref[hh] = st[hh]
            t_ref[hh] = t[hh]
            state[hh] = new[hh]
            o_ref[:, c] = o[hh]

    blk = pl.BlockSpec((CHUNK, wide), lambda d, h, m: (_chunk_pos(d, m, n), h))
    gate = pl.BlockSpec((CHUNK, wide), lambda d, h, m: (_chunk_pos(d, m, n), d * (A_HEADS // hb) + h))
    return pl.pallas_call(
        body, name="gdn_rec_fwd", grid=(2, A_HEADS // hb, n),
        in_specs=[blk, blk, blk, gate, gate],
        out_specs=[pl.BlockSpec((None, CHUNK, wide), lambda d, h, m: (d, _chunk_pos(d, m, n), h)),
                   pl.BlockSpec((None, hb, None, A_DK, LANES), lambda d, h, m: (d, h, _chunk_pos(d, m, n), 0, 0)),
                   pl.BlockSpec((None, hb, None, CHUNK, CHUNK), lambda d, h, m: (d, h, _chunk_pos(d, m, n), 0, 0))],
        out_shape=[jax.ShapeDtypeStruct((2, s, A_W), F32), jax.ShapeDtypeStruct((2, A_HEADS, n, A_DK, LANES), F32),
                   jax.ShapeDtypeStruct((2, A_HEADS, n, CHUNK, CHUNK), F32)],
        scratch_shapes=[pltpu.VMEM((hb, A_DK, LANES), F32)],
        compiler_params=_cparams(("arbitrary", "arbitrary", "arbitrary")),
    )(q, k, v, beta_b, g_b)


def gdn_rec_bwd(q, k, v, beta_b, g_b, states, tinv, do):
    s = q.shape[0]
    n = s // CHUNK

    hb = GDN_HEADS_PER_STEP
    wide = hb * LANES

    def body(q_ref, k_ref, v_ref, bb_ref, gb_ref, st_ref, t_ref, do_ref, dq_ref, dk_ref, dv_ref, dbb_ref, dgb_ref, dstate):
        d = pl.program_id(0)

        @pl.when(pl.program_id(2) == 0)
        def _():
            dstate[...] = jnp.zeros_like(dstate)

        def step(*a):
            return _gdn_step(*a, d, t_saved=[t_ref[hh] for hh in range(hb)])[:2]

        cols = [slice(hh * LANES, (hh + 1) * LANES) for hh in range(hb)]
        _, vjp = jax.vjp(step, [st_ref[hh] for hh in range(hb)],
                         *([r[:, c] for c in cols] for r in (q_ref, k_ref, v_ref, bb_ref, gb_ref)))
        grads = vjp(([dstate[hh] for hh in range(hb)], [do_ref[:, c] for c in cols]))
        for hh, c in enumerate(cols):
            dstate[hh], dq_ref[:, c], dk_ref[:, c], dv_ref[:, c], dbb_ref[:, c], dgb_ref[:, c] = (g[hh] for g in grads)

    pos = lambda d, m: _chunk_pos(1 - d, m, n)
    blk = pl.BlockSpec((CHUNK, wide), lambda d, h, m: (pos(d, m), h))
    gate = pl.BlockSpec((CHUNK, wide), lambda d, h, m: (pos(d, m), d * (A_HEADS // hb) + h))
    oblk = pl.BlockSpec((None, CHUNK, wide), lambda d, h, m: (d, pos(d, m), h))
    return pl.pallas_call(
        body, name="gdn_rec_bwd", grid=(2, A_HEADS // hb, n),
        in_specs=[blk, blk, blk, gate, gate,
                  pl.BlockSpec((None, hb, None, A_DK, LANES), lambda d, h, m: (d, h, pos(d, m), 0, 0)),
                  pl.BlockSpec((None, hb, None, CHUNK, CHUNK), lambda d, h, m: (d, h, pos(d, m), 0, 0)), blk],
        out_specs=[oblk, oblk, oblk, gate, gate],
        out_shape=[jax.ShapeDtypeStruct((2, s, A_W), F32)] * 3 + [jax.ShapeDtypeStruct(beta_b.shape, F32)] * 2,
        scratch_shapes=[pltpu.VMEM((hb, A_DK, LANES), F32)],
        compiler_params=_cparams(("arbitrary", "arbitrary", "arbitrary")),
    )(q, k, v, beta_b, g_b, states, tinv, do)


def gla_rec_fwd(proj, log_a):
    s = proj.shape[0]
    n = s // CHUNK

    kcols = [slice(h * B_DK, (h + 1) * B_DK) for h in range(B_HEADS)]
    vcols = [slice(h * B_DV, (h + 1) * B_DV) for h in range(B_HEADS)]

    def body(q_ref, k_ref, v_ref, la_ref, o_ref, st_ref, state):
        d = pl.program_id(0)

        @pl.when(pl.program_id(1) == 0)
        def _():
            state[...] = jnp.zeros_like(state)

        st = [state[h] for h in range(B_HEADS)]
        new, o = _gla_step(st, [q_ref[:, c] for c in kcols], [k_ref[:, c] for c in kcols], [v_ref[:, c] for c in vcols],
                           [la_ref[:, c] for c in kcols], d)
        for h in range(B_HEADS):
            st_ref[h] = st[h]
            state[h] = new[h]
            o_ref[:, vcols[h]] = o[h]

    pos = lambda d, m: _chunk_pos(d, m, n)
    return pl.pallas_call(
        body, name="gla_rec_fwd", grid=(2, n),
        in_specs=[pl.BlockSpec((CHUNK, B_KW), lambda d, m: (pos(d, m), 0)),
                  pl.BlockSpec((CHUNK, B_KW), lambda d, m: (pos(d, m), 1)),
                  pl.BlockSpec((CHUNK, B_VW), lambda d, m: (pos(d, m), 2 * B_KW // B_VW)),
                  pl.BlockSpec((None, CHUNK, B_KW), lambda d, m: (d, pos(d, m), 0))],
        out_specs=[pl.BlockSpec((None, CHUNK, B_VW), lambda d, m: (d, pos(d, m), 0)),
                   pl.BlockSpec((None, B_HEADS, None, B_DV, B_DK), lambda d, m: (d, 0, pos(d, m), 0, 0))],
        out_shape=[jax.ShapeDtypeStruct((2, s, B_VW), F32), jax.ShapeDtypeStruct((2, B_HEADS, n, B_DV, B_DK), F32)],
        scratch_shapes=[pltpu.VMEM((B_HEADS, B_DV, B_DK), F32)],
        compiler_params=_cparams(("arbitrary", "arbitrary")),
    )(proj, proj, proj, log_a)


def gla_rec_bwd(proj, log_a, states, do):
    s = proj.shape[0]
    n = s // CHUNK

    kcols = [slice(h * B_DK, (h + 1) * B_DK) for h in range(B_HEADS)]
    vcols = [slice(h * B_DV, (h + 1) * B_DV) for h in range(B_HEADS)]

    def body(q_ref, k_ref, v_ref, la_ref, st_ref, do_ref, dq_ref, dk_ref, dv_ref, dla_ref, dstate):
        d = pl.program_id(0)

        @pl.when(pl.program_id(1) == 0)
        def _():
            dstate[...] = jnp.zeros_like(dstate)

        step = functools.partial(_gla_step, rev=d)
        _, vjp = jax.vjp(step, [st_ref[h] for h in range(B_HEADS)], [q_ref[:, c] for c in kcols],
                         [k_ref[:, c] for c in kcols], [v_ref[:, c] for c in vcols], [la_ref[:, c] for c in kcols])
        dst, dq, dk, dv, dla = vjp(([dstate[h] for h in range(B_HEADS)], [do_ref[:, c] for c in vcols]))
        for h in range(B_HEADS):
            dstate[h] = dst[h]
            dq_ref[:, kcols[h]] = dq[h]
            dk_ref[:, kcols[h]] = dk[h]
            dv_ref[:, vcols[h]] = dv[h]
            dla_ref[:, kcols[h]] = dla[h]

    pos = lambda d, m: _chunk_pos(1 - d, m, n)
    kblk = pl.BlockSpec((None, CHUNK, B_KW), lambda d, m: (d, pos(d, m), 0))
    return pl.pallas_call(
        body, name="gla_rec_bwd", grid=(2, n),
        in_specs=[pl.BlockSpec((CHUNK, B_KW), lambda d, m: (pos(d, m), 0)),
                  pl.BlockSpec((CHUNK, B_KW), lambda d, m: (pos(d, m), 1)),
                  pl.BlockSpec((CHUNK, B_VW), lambda d, m: (pos(d, m), 2 * B_KW // B_VW)),
                  kblk,
                  pl.BlockSpec((None, B_HEADS, None, B_DV, B_DK), lambda d, m: (d, 0, pos(d, m), 0, 0)),
                  pl.BlockSpec((CHUNK, B_VW), lambda d, m: (pos(d, m), 0))],
        out_specs=[kblk, kblk, pl.BlockSpec((None, CHUNK, B_VW), lambda d, m: (d, pos(d, m), 0)), kblk],
        out_shape=[jax.ShapeDtypeStruct((2, s, B_KW), F32), jax.ShapeDtypeStruct((2, s, B_KW), F32),
                   jax.ShapeDtypeStruct((2, s, B_VW), F32), jax.ShapeDtypeStruct((2, s, B_KW), F32)],
        scratch_shapes=[pltpu.VMEM((B_HEADS, B_DV, B_DK), F32)],
        compiler_params=_cparams(("arbitrary", "arbitrary")),
    )(proj, proj, proj, log_a, states, do)


MM_TILE_OUT = 1024
MM_TILE_K = 1024


def _tile(n, pref):
    return pref if n % pref == 0 else n


class Gathered:
    def __init__(self, g, off, kind):
        assert off % D_MODEL == 0 and MM_TILE_OUT == D_MODEL and MM_TILE_K == D_MODEL
        self.g, self.blk, self.kind = g, off // D_MODEL, kind
        self.shape = (D_MODEL, 4 * D_MODEL) if kind == "cols" else (4 * D_MODEL, D_MODEL)

    def spec(self, mode):
        blk = self.blk
        chip_is_k = (self.kind == "rows") == (mode == "nn")
        if chip_is_k:
            return pl.BlockSpec((None, D_MODEL, D_MODEL), lambda i, j, k: (k, blk, 0))
        return pl.BlockSpec((None, D_MODEL, D_MODEL), lambda i, j, k: (j, blk, 0))


def mm(a, b, mode="nn", act=None, epi=None, extra=None, alpha=1.0, chip_major=False, name="mm"):
    if mode == "tn":
        kk, m = a.shape
    else:
        m, kk = a.shape
    nn_ = b.shape[0] if mode == "nt" else b.shape[1]
    tm, tn_, tk = _tile(m, MM_TILE_OUT), _tile(nn_, MM_TILE_OUT), _tile(kk, MM_TILE_K)
    nk = kk // tk
    ca, cb = {"nn": (1, 0), "nt": (1, 1), "tn": (0, 0)}[mode]

    def body(*refs):
        if epi is None:
            a_ref, b_ref, o_ref = refs
        else:
            a_ref, b_ref, e_ref, o_ref = refs
        kstep = pl.program_id(2)
        at = a_ref[...]
        if act == "sqrelu":
            at = jnp.square(jnp.maximum(at, 0.0))
        part = _dg(at, b_ref[...], ca, cb)

        @pl.when(kstep == 0)
        def _():
            o_ref[...] = part

        @pl.when(kstep > 0)
        def _():
            o_ref[...] += part

        if epi is not None:
            @pl.when(kstep == nk - 1)
            def _():
                if epi == "dsqrelu":
                    o_ref[...] = o_ref[...] * (2.0 * jnp.maximum(e_ref[...], 0.0))
                else:
                    o_ref[...] = o_ref[...] + alpha * e_ref[...]

    a_spec = pl.BlockSpec((tk, tm), lambda i, j, k: (k, i)) if mode == "tn" else pl.BlockSpec((tm, tk), lambda i, j, k: (i, k))
    if isinstance(b, Gathered):
        assert mode in ("nn", "nt") and tn_ == D_MODEL and tk == D_MODEL
        b_spec, b = b.spec(mode), b.g
    elif mode == "nt":
        b_spec = pl.BlockSpec((tn_, tk), lambda i, j, k: (j, k))
    else:
        b_spec = pl.BlockSpec((tk, tn_), lambda i, j, k: (k, j))
    o_spec = pl.BlockSpec((tm, tn_), lambda i, j, k: (i, j))
    ins, specs = [a, b], [a_spec, b_spec]
    if epi is not None:
        ins.append(extra)
        specs.append(o_spec)
    out_shape = jax.ShapeDtypeStruct((m, nn_), F32)
    if chip_major:
        assert nn_ == 4 * D_MODEL and tn_ == D_MODEL
        o_spec = pl.BlockSpec((None, tm, D_MODEL), lambda i, j, k: (j, i, 0))
        out_shape = jax.ShapeDtypeStruct((4, m, D_MODEL), F32)
    return pl.pallas_call(
        body, name=name, grid=(m // tm, nn_ // tn_, nk), in_specs=specs, out_specs=o_spec, out_shape=out_shape,
        compiler_params=_cparams(("parallel", "parallel", "arbitrary")),
    )(*ins)


ROWS = 256
POST_ROWS = 1024


def _ln_core(x, m, g, b):
    r = ALPHA * x + m
    mu = jnp.mean(r, axis=-1, keepdims=True)
    xc = r - mu
    var = jnp.mean(xc * xc, axis=-1, keepdims=True)
    rstd = lax.rsqrt(var + LN_EPS)
    xhat = xc * rstd
    return xhat * g + b, xhat, rstd


def ln_fwd(x, m, g, b):
    s, dm = x.shape

    def body(x_ref, m_ref, g_ref, b_ref, o_ref):
        o_ref[...] = _ln_core(x_ref[...], m_ref[...], g_ref[...], b_ref[...])[0]

    row = pl.BlockSpec((ROWS, dm), lambda i: (i, 0))
    vec = pl.BlockSpec((1, dm), lambda i: (0, 0))
    return pl.pallas_call(body, name="ln_fwd", grid=(s // ROWS,), in_specs=[row, row, vec, vec], out_specs=row,
                          out_shape=jax.ShapeDtypeStruct((s, dm), F32), compiler_params=_cparams(("parallel",)))(x, m, g, b)


def ln_bwd(x, m, g, dy):
    s, dm = x.shape

    def body(x_ref, m_ref, g_ref, dy_ref, dr_ref, dg_ref, db_ref):
        gv = g_ref[...]
        _, xhat, rstd = _ln_core(x_ref[...], m_ref[...], gv, jnp.zeros_like(gv))
        dy = dy_ref[...]
        dxh = dy * gv
        dr_ref[...] = rstd * (dxh - jnp.mean(dxh, axis=-1, keepdims=True)
                              - xhat * jnp.mean(dxh * xhat, axis=-1, keepdims=True))

        @pl.when(pl.program_id(0) == 0)
        def _():
            dg_ref[...] = jnp.zeros_like(dg_ref)
            db_ref[...] = jnp.zeros_like(db_ref)

        dg_ref[...] += jnp.sum(dy * xhat, axis=0, keepdims=True)
        db_ref[...] += jnp.sum(dy, axis=0, keepdims=True)

    row = pl.BlockSpec((ROWS, dm), lambda i: (i, 0))
    vec = pl.BlockSpec((1, dm), lambda i: (0, 0))
    return pl.pallas_call(body, name="ln_bwd", grid=(s // ROWS,), in_specs=[row, row, vec, row], out_specs=[row, vec, vec],
                          out_shape=[jax.ShapeDtypeStruct((s, dm), F32), jax.ShapeDtypeStruct((1, dm), F32),
                                     jax.ShapeDtypeStruct((1, dm), F32)],
                          compiler_params=_cparams(("arbitrary",)))(x, m, g, dy)


def loss_head(y, target):
    s, dm = y.shape

    def body(y_ref, t_ref, dy_ref, l_ref):
        e = y_ref[...] - t_ref[...]
        dy_ref[...] = e * (1.0 / dm)

        @pl.when(pl.program_id(0) == 0)
        def _():
            l_ref[...] = jnp.zeros_like(l_ref)

        col = jnp.sum(e * e, axis=0, keepdims=True) * (0.5 / dm)
        acc = col[:, :LANES]
        for c in range(1, dm // LANES):
            acc = acc + col[:, c * LANES:(c + 1) * LANES]
        l_ref[...] += acc

    row = pl.BlockSpec((ROWS, dm), lambda i: (i, 0))
    return pl.pallas_call(body, name="loss_head", grid=(s // ROWS,), in_specs=[row, row],
                          out_specs=[row, pl.BlockSpec((1, LANES), lambda i: (0, 0))],
                          out_shape=[jax.ShapeDtypeStruct((s, dm), F32), jax.ShapeDtypeStruct((1, LANES), F32)],
                          compiler_params=_cparams(("arbitrary",)))(y, target)


def _shift_rows_impl(x, d):
    n = x.shape[0]
    if d == 0:
        return x
    t = lax.broadcasted_iota(jnp.int32, x.shape, 0)
    return jnp.where((t + d >= 0) & (t + d < n), pltpu.roll(x, (-d) % n, 0), 0.0)


@functools.partial(jax.custom_vjp, nondiff_argnums=(1,))
def _shift_rows(x, d):
    return _shift_rows_impl(x, d)


_shift_rows.defvjp(lambda x, d: (_shift_rows_impl(x, d), None), lambda d, _, g: (_shift_rows_impl(g, -d),))


def _gdn_pre_fn(u, w, kind):
    rows = lax.broadcasted_iota(jnp.int32, w.shape, 0)
    c = None
    for tap in range(A_CONV):
        w_tap = jnp.sum(jnp.where(rows == tap, w, 0.0), axis=0, keepdims=True)
        term = _shift_rows(u, tap - A_CONV // 2) * w_tap
        c = term if c is None else c + term
    y = c * _sigmoid(c)
    if kind == "v":
        return y
    y = y * lax.rsqrt(jnp.sum(y * y, axis=-1, keepdims=True) + L2_EPS)
    return y * (A_DK ** -0.5) if kind == "q" else y


_KIND_OFF = {"q": 0, "k": A_HEADS, "v": 2 * A_HEADS}


def gdn_pre(proj, conv_w, kind):
    s = proj.shape[0]
    off = _KIND_OFF[kind]

    def body(u_ref, w_ref, o_ref):
        o_ref[...] = _gdn_pre_fn(u_ref[...], w_ref[...], kind)

    return pl.pallas_call(
        body, name="gdn_pre_" + kind, grid=(A_HEADS,),
        in_specs=[pl.BlockSpec((s, LANES), lambda h: (0, off + h)), pl.BlockSpec((A_CONV, LANES), lambda h: (0, off + h))],
        out_specs=pl.BlockSpec((s, LANES), lambda h: (0, h)),
        out_shape=jax.ShapeDtypeStruct((s, A_W), F32), compiler_params=_cparams(("parallel",)))(proj, conv_w)


def gdn_pre_bwd(proj, conv_w, dt2, kind):
    s = proj.shape[0]
    off = _KIND_OFF[kind]

    def body(u_ref, w_ref, d0_ref, d1_ref, du_ref, dw_ref):
        _, vjp = jax.vjp(functools.partial(_gdn_pre_fn, kind=kind), u_ref[...], w_ref[...])
        du, dw = vjp(d0_ref[...] + d1_ref[...])
        du_ref[...] = du
        dw_ref[...] = dw

    return pl.pallas_call(
        body, name="gdn_pre_bwd_" + kind, grid=(A_HEADS,),
        in_specs=[pl.BlockSpec((s, LANES), lambda h: (0, off + h)), pl.BlockSpec((A_CONV, LANES), lambda h: (0, off + h)),
                  pl.BlockSpec((None, s, LANES), lambda h: (0, 0, h)), pl.BlockSpec((None, s, LANES), lambda h: (1, 0, h))],
        out_specs=[pl.BlockSpec((s, LANES), lambda h: (0, h)), pl.BlockSpec((A_CONV, LANES), lambda h: (0, h))],
        out_shape=[jax.ShapeDtypeStruct((s, A_W), F32), jax.ShapeDtypeStruct((A_CONV, A_W), F32)],
        compiler_params=_cparams(("parallel",)))(proj, conv_w, dt2, dt2)


N_GATE = 2 * A_HEADS


def _gdn_gates_fn(ba, alog_row, dt_row):
    r = lax.broadcasted_iota(jnp.int32, (LANES, N_GATE * LANES), 0)
    c = lax.broadcasted_iota(jnp.int32, (LANES, N_GATE * LANES), 1) >> 7
    beta_b = xdot(_sigmoid(ba), (r == c).astype(F32))
    g = -(jnp.exp(alog_row) * _softplus(ba + dt_row))
    g_b = xdot(g, (r == c + N_GATE).astype(F32))
    return beta_b, g_b


def gdn_gates(ba, alog_row, dt_row):
    s = ba.shape[0]

    def body(ba_ref, al_ref, dt_ref, bb_ref, gb_ref):
        bb_ref[...], gb_ref[...] = _gdn_gates_fn(ba_ref[...], al_ref[...], dt_ref[...])

    row = pl.BlockSpec((ROWS, LANES), lambda i: (i, 0))
    vec = pl.BlockSpec((1, LANES), lambda i: (0, 0))
    wide = pl.BlockSpec((ROWS, N_GATE * LANES), lambda i: (i, 0))
    return pl.pallas_call(body, name="gdn_gates", grid=(s // ROWS,), in_specs=[row, vec, vec], out_specs=[wide, wide],
                          out_shape=[jax.ShapeDtypeStruct((s, N_GATE * LANES), F32)] * 2,
                          compiler_params=_cparams(("parallel",)))(ba, alog_row, dt_row)


def gdn_gates_bwd(ba, alog_row, dt_row, dbeta_b, dg_b):
    s = ba.shape[0]

    def body(ba_ref, al_ref, dt_ref, dbb_ref, dgb_ref, dba_ref, dal_ref, ddt_ref):
        _, vjp = jax.vjp(_gdn_gates_fn, ba_ref[...], al_ref[...], dt_ref[...])
        dba, dal, ddt = vjp((dbb_ref[...], dgb_ref[...]))
        dba_ref[...] = dba

        @pl.when(pl.program_id(0) == 0)
        def _():
            dal_ref[...] = jnp.zeros_like(dal_ref)
            ddt_ref[...] = jnp.zeros_like(ddt_ref)

        dal_ref[...] += dal
        ddt_ref[...] += ddt

    row = pl.BlockSpec((ROWS, LANES), lambda i: (i, 0))
    vec = pl.BlockSpec((1, LANES), lambda i: (0, 0))
    wide = pl.BlockSpec((ROWS, N_GATE * LANES), lambda i: (i, 0))
    return pl.pallas_call(body, name="gdn_gates_bwd", grid=(s // ROWS,), in_specs=[row, vec, vec, wide, wide],
                          out_specs=[row, vec, vec],
                          out_shape=[jax.ShapeDtypeStruct((s, LANES), F32), jax.ShapeDtypeStruct((1, LANES), F32),
                                     jax.ShapeDtypeStruct((1, LANES), F32)],
                          compiler_params=_cparams(("arbitrary",)))(ba, alog_row, dt_row, dbeta_b, dg_b)


def _post_fn(o, z, g):
    y = o * lax.rsqrt(jnp.mean(o * o, axis=-1, keepdims=True) + RMS_EPS) * g
    return y * (z * _sigmoid(z))


def mixer_post(o2, proj, norm_g, width, gate_off, name):
    s = o2.shape[1]
    nh = o2.shape[2] // width

    rows = _tile(s, POST_ROWS)

    def body(o0_ref, o1_ref, z_ref, g_ref, y_ref):
        y_ref[...] = _post_fn(o0_ref[...] + o1_ref[...], z_ref[...], g_ref[...])

    ospec = lambda d: pl.BlockSpec((None, rows, width), lambda i, h: (d, i, h))
    return pl.pallas_call(
        body, name=name, grid=(s // rows, nh),
        in_specs=[ospec(0), ospec(1), pl.BlockSpec((rows, width), lambda i, h: (i, gate_off + h)),
                  pl.BlockSpec((1, width), lambda i, h: (0, 0))],
        out_specs=pl.BlockSpec((rows, width), lambda i, h: (i, h)),
        out_shape=jax.ShapeDtypeStruct((s, o2.shape[2]), F32),
        compiler_params=_cparams(("parallel", "parallel")))(o2, o2, proj, norm_g)


def mixer_post_bwd(o2, proj, norm_g, dy, width, gate_off, name):
    s = o2.shape[1]
    nh = o2.shape[2] // width

    def body(o0_ref, o1_ref, z_ref, g_ref, dy_ref, do_ref, dz_ref, dg_ref):
        _, vjp = jax.vjp(_post_fn, o0_ref[...] + o1_ref[...], z_ref[...], g_ref[...])
        do, dz, dg = vjp(dy_ref[...])
        do_ref[...] = do
        dz_ref[...] = dz

        @pl.when((pl.program_id(0) == 0) & (pl.program_id(1) == 0))
        def _():
            dg_ref[...] = jnp.zeros_like(dg_ref)

        dg_ref[...] += dg

    rows = _tile(s, POST_ROWS)
    ospec = lambda d: pl.BlockSpec((None, rows, width), lambda i, h: (d, i, h))
    blk = pl.BlockSpec((rows, width), lambda i, h: (i, h))
    vec = pl.BlockSpec((1, width), lambda i, h: (0, 0))
    return pl.pallas_call(
        body, name=name, grid=(s // rows, nh),
        in_specs=[ospec(0), ospec(1), pl.BlockSpec((rows, width), lambda i, h: (i, gate_off + h)), vec, blk],
        out_specs=[blk, blk, vec],
        out_shape=[jax.ShapeDtypeStruct((s, o2.shape[2]), F32)] * 2 + [jax.ShapeDtypeStruct((1, width), F32)],
        compiler_params=_cparams(("arbitrary", "arbitrary")))(o2, o2, proj, norm_g, dy)


def _log_gate(z):
    return (jnp.minimum(z, 0.0) - jnp.log(1.0 + jnp.exp(-jnp.abs(z)))) * (1.0 / B_TAU)


def gla_gate(gl, w2, gb):
    s = gl.shape[0]

    def body(gl_ref, w_ref, b_ref, o_ref):
        for n in range(2):
            o_ref[n] = _log_gate(nn(gl_ref[...], w_ref[n]) + b_ref[n])

    full = lambda shp: pl.BlockSpec(shp, lambda i: (0,) * len(shp))
    return pl.pallas_call(
        body, name="gla_gate", grid=(s // ROWS,),
        in_specs=[pl.BlockSpec((ROWS, LANES), lambda i: (i, 0)), full(w2.shape), full(gb.shape)],
        out_specs=pl.BlockSpec((2, ROWS, B_KW), lambda i: (0, i, 0)),
        out_shape=jax.ShapeDtypeStruct((2, s, B_KW), F32), compiler_params=_cparams(("parallel",)))(gl, w2, gb)


def gla_gate_bwd(gl, w2, gb, dla):
    s = gl.shape[0]

    def body(gl_ref, w_ref, b_ref, dla_ref, dgl_ref, dz_ref, db0_ref, db1_ref):
        @pl.when(pl.program_id(0) == 0)
        def _():
            db0_ref[...] = jnp.zeros_like(db0_ref)
            db1_ref[...] = jnp.zeros_like(db1_ref)

        dgl = None
        for n, db_ref in enumerate((db0_ref, db1_ref)):
            _, vjp = jax.vjp(_log_gate, nn(gl_ref[...], w_ref[n]) + b_ref[n])
            dz, = vjp(dla_ref[n])
            dz_ref[n] = dz
            db_ref[...] += jnp.sum(dz, axis=0, keepdims=True)
            part = nt(dz, w_ref[n])
            dgl = part if dgl is None else dgl + part
        dgl_ref[...] = dgl

    full = lambda shp: pl.BlockSpec(shp, lambda i: (0,) * len(shp))
    row = pl.BlockSpec((ROWS, LANES), lambda i: (i, 0))
    wide = pl.BlockSpec((2, ROWS, B_KW), lambda i: (0, i, 0))
    vec = pl.BlockSpec((1, B_KW), lambda i: (0, 0))
    return pl.pallas_call(
        body, name="gla_gate_bwd", grid=(s // ROWS,),
        in_specs=[row, full(w2.shape), full(gb.shape), wide],
        out_specs=[row, wide, vec, vec],
        out_shape=[jax.ShapeDtypeStruct((s, LANES), F32), jax.ShapeDtypeStruct((2, s, B_KW), F32),
                   jax.ShapeDtypeStruct((1, B_KW), F32), jax.ShapeDtypeStruct((1, B_KW), F32)],
        compiler_params=_cparams(("arbitrary",)))(gl, w2, gb, dla)


PACK_TILE = 512


def cast_into_slot(x, chip):
    r, c = x.shape

    def body(chip_ref, x_ref, o_ref):
        o_ref[...] = x_ref[...].astype(BF16)

    return pl.pallas_call(
        body, name="cast_into_slot",
        grid_spec=pltpu.PrefetchScalarGridSpec(
            num_scalar_prefetch=1, grid=(r // PACK_TILE,),
            in_specs=[pl.BlockSpec((PACK_TILE, c), lambda i, chip_ref: (i, 0))],
            out_specs=pl.BlockSpec((None, PACK_TILE, c), lambda i, chip_ref: (chip_ref[0], i, 0))),
        out_shape=jax.ShapeDtypeStruct((4, r, c), BF16), compiler_params=_cparams(("parallel",)))(chip, x)


def sum_received(chip_sum, recv, chip, core):
    _, h, c = chip_sum.shape
    tr = _tile(h, PACK_TILE)
    nblk = h // tr

    def body(chip_ref, core_ref, own_ref, r_ref, o_ref):
        acc = r_ref[0].astype(F32)
        for k in range(1, 3):
            acc = acc + r_ref[k].astype(F32)
        o_ref[...] = acc + own_ref[...].astype(F32)

    return pl.pallas_call(
        body, name="sum_received",
        grid_spec=pltpu.PrefetchScalarGridSpec(
            num_scalar_prefetch=2, grid=(nblk,),
            in_specs=[pl.BlockSpec((None, tr, c), lambda i, chip_ref, core_ref: (chip_ref[0], i, 0)),
                      pl.BlockSpec((3, tr, c), lambda i, chip_ref, core_ref: (0, i, 0))],
            out_specs=pl.BlockSpec((tr, c), lambda i, chip_ref, core_ref: (core_ref[0] * nblk + i, 0))),
        out_shape=jax.ShapeDtypeStruct((2 * h, c), F32), compiler_params=_cparams(("parallel",)))(chip, core, chip_sum, recv)


def sum_slots(x, name):
    n, r, c = x.shape
    tr = _tile(r, PACK_TILE)

    def body(x_ref, o_ref):
        acc = x_ref[0].astype(F32)
        for k in range(1, n):
            acc = acc + x_ref[k].astype(F32)
        o_ref[...] = acc

    return pl.pallas_call(body, name=name, grid=(r // tr,), in_specs=[pl.BlockSpec((n, tr, c), lambda i: (0, i, 0))],
                          out_specs=pl.BlockSpec((tr, c), lambda i: (i, 0)),
                          out_shape=jax.ShapeDtypeStruct((r, c), F32), compiler_params=_cparams(("parallel",)))(x)


def add_sibling_half(gpack, theirs, core):
    n, r, c = gpack.shape
    half_rows = r // 2
    tr = _tile(half_rows, PACK_TILE)
    nblk = half_rows // tr

    def body(core_ref, g_ref, t_ref, o_ref):
        o_ref[...] = (g_ref[...] + t_ref[...]).astype(BF16)

    blk = pl.BlockSpec((None, tr, c), lambda s, i, core_ref: (s, i, 0))
    return pl.pallas_call(
        body, name="add_sibling_half",
        grid_spec=pltpu.PrefetchScalarGridSpec(
            num_scalar_prefetch=1, grid=(n, nblk),
            in_specs=[pl.BlockSpec((None, tr, c), lambda s, i, core_ref: (s, core_ref[0] * nblk + i, 0)), blk],
            out_specs=blk),
        out_shape=jax.ShapeDtypeStruct((n, half_rows, c), BF16),
        compiler_params=_cparams(("parallel", "parallel")))(core, gpack, theirs)


def adamw(w, m, v, grads, g_row_off, name):
    r, c = w.shape
    tr = next(t for t in (PACK_TILE, r) if r % t == 0 and g_row_off % t == 0)
    ob = g_row_off // tr
    ng = len(grads)

    def body(*refs):
        w_ref, m_ref, v_ref = refs[:3]
        g_refs = refs[3:3 + ng]
        g_ref, d_ref, nm_ref, nv_ref = refs[3 + ng:]
        g = g_refs[0][...]
        for gr in g_refs[1:]:
            g = g + gr[...]
        m_new = ADAM_B1 * m_ref[...] + (1.0 - ADAM_B1) * g
        v_new = ADAM_B2 * v_ref[...] + (1.0 - ADAM_B2) * jnp.square(g)
        m_hat = m_new / (1.0 - ADAM_B1 ** ADAM_STEP)
        v_hat = v_new / (1.0 - ADAM_B2 ** ADAM_STEP)
        g_ref[...] = g
        d_ref[...] = -ADAM_LR * (m_hat / (jnp.sqrt(v_hat) + ADAM_EPS) + ADAM_WD * w_ref[...])
        nm_ref[...] = m_new
        nv_ref[...] = v_new

    blk = pl.BlockSpec((tr, c), lambda i: (i, 0))
    gblk = pl.BlockSpec((tr, c), lambda i: (i + ob, 0))
    return pl.pallas_call(body, name=name, grid=(r // tr,), in_specs=[blk, blk, blk] + [gblk] * ng, out_specs=[blk] * 4,
                          out_shape=[jax.ShapeDtypeStruct((r, c), F32)] * 4,
                          compiler_params=_cparams(("parallel",)))(w, m, v, *grads)


MESH = pl.DeviceIdType.MESH
HBM = pl.BlockSpec(memory_space=pl.ANY)
CHIP_FLIPS = ((1, 0), (0, 1), (1, 1))


def _place():
    return lax.axis_index("x"), lax.axis_index("y"), lax.axis_index("c")


def allgather_chips(buf):
    _, r, c = buf.shape
    half_rows = r // 2

    def body(_, out_ref, send_sems, recv_sems):
        x, y, cc = _place()
        half = pl.ds(cc * half_rows, half_rows)
        other = pl.ds((1 - cc) * half_rows, half_rows)

        def copy(k, rows, to):
            return pltpu.make_async_remote_copy(src_ref=rows, dst_ref=rows, send_sem=send_sems.at[k],
                                                recv_sem=recv_sems.at[k], device_id=to, device_id_type=MESH)

        chips = [((1 - x if fx else x), (1 - y if fy else y)) for fx, fy in CHIP_FLIPS]
        first = [copy(k, out_ref.at[2 * x + y, half], (px, py, cc)) for k, (px, py) in enumerate(chips)]
        for cp in first:
            cp.start()
        passed = []
        for k, (px, py) in enumerate(chips):
            landed = out_ref.at[2 * px + py, half]
            copy(k, landed, (px, py, cc)).wait_recv()
            passed.append(copy(3 + k, landed, (x, y, 1 - cc)))
            passed[-1].start()
        for k, (px, py) in enumerate(chips):
            copy(3 + k, out_ref.at[2 * px + py, other], (x, y, 1 - cc)).wait_recv()
        for cp in first + passed:
            cp.wait_send()

    return pl.pallas_call(
        body, name="allgather_chips", in_specs=[HBM], out_specs=HBM, input_output_aliases={0: 0},
        out_shape=jax.ShapeDtypeStruct(buf.shape, buf.dtype),
        scratch_shapes=[pltpu.SemaphoreType.DMA((6,)), pltpu.SemaphoreType.DMA((6,))],
    )(buf)


def scatter_chips(gpack):
    _, r, c = gpack.shape

    def body(src_ref, out_ref, send_sems, recv_sems):
        x, y, cc = _place()
        sends = []
        for k, (fx, fy) in enumerate(CHIP_FLIPS):
            px, py = (1 - x if fx else x), (1 - y if fy else y)
            sends.append(pltpu.make_async_remote_copy(
                src_ref=src_ref.at[2 * px + py], dst_ref=out_ref.at[k], send_sem=send_sems.at[k], recv_sem=recv_sems.at[k],
                device_id=(px, py, cc), device_id_type=MESH))
        for cp in sends:
            cp.start()
        for cp in sends:
            cp.wait_recv()
        for cp in sends:
            cp.wait_send()

    return pl.pallas_call(
        body, name="scatter_chips", in_specs=[HBM], out_specs=HBM,
        out_shape=jax.ShapeDtypeStruct((3, r, c), gpack.dtype),
        scratch_shapes=[pltpu.SemaphoreType.DMA((3,)), pltpu.SemaphoreType.DMA((3,))],
    )(gpack)


def sibling_halves(gpack):
    n, r, c = gpack.shape
    half_rows = r // 2

    def body(src_ref, out_ref, send_sem, recv_sem):
        x, y, cc = _place()
        cp = pltpu.make_async_remote_copy(
            src_ref=src_ref.at[:, pl.ds((1 - cc) * half_rows, half_rows)], dst_ref=out_ref, send_sem=send_sem,
            recv_sem=recv_sem, device_id=(x, y, 1 - cc), device_id_type=MESH)
        cp.start()
        cp.wait()

    return pl.pallas_call(
        body, name="sibling_halves", in_specs=[HBM], out_specs=HBM,
        out_shape=jax.ShapeDtypeStruct((n, half_rows, c), gpack.dtype),
        scratch_shapes=[pltpu.SemaphoreType.DMA, pltpu.SemaphoreType.DMA],
    )(gpack)


def join_halves(buf):
    r, c = buf.shape
    half_rows = r // 2

    def body(_, out_ref, send_sem, recv_sem):
        x, y, cc = _place()
        half = out_ref.at[pl.ds(cc * half_rows, half_rows)]
        other = out_ref.at[pl.ds((1 - cc) * half_rows, half_rows)]
        send = pltpu.make_async_remote_copy(src_ref=half, dst_ref=half, send_sem=send_sem, recv_sem=recv_sem,
                                            device_id=(x, y, 1 - cc), device_id_type=MESH)
        send.start()
        pltpu.make_async_remote_copy(src_ref=other, dst_ref=other, send_sem=send_sem, recv_sem=recv_sem,
                                     device_id=(x, y, 1 - cc), device_id_type=MESH).wait_recv()
        send.wait_send()

    return pl.pallas_call(
        body, name="join_halves", in_specs=[HBM], out_specs=HBM, input_output_aliases={0: 0},
        out_shape=jax.ShapeDtypeStruct(buf.shape, buf.dtype),
        scratch_shapes=[pltpu.SemaphoreType.DMA, pltpu.SemaphoreType.DMA],
    )(buf)


def exchange_all(v, name):
    r, c = v.shape

    def body(v_ref, out_ref, send_sems, recv_sems):
        x, y, cc = _place()
        out_ref[4 * x + 2 * y + cc] = v_ref[...]
        sends, recvs = [], []
        for k in range(1, 8):
            px = 1 - x if k & 4 else x
            py = 1 - y if k & 2 else y
            pc = 1 - cc if k & 1 else cc
            sends.append(pltpu.make_async_remote_copy(
                src_ref=v_ref, dst_ref=out_ref.at[4 * x + 2 * y + cc], send_sem=send_sems.at[k - 1],
                recv_sem=recv_sems.at[k - 1], device_id=(px, py, pc), device_id_type=MESH))
            recvs.append(pltpu.make_async_remote_copy(
                src_ref=v_ref, dst_ref=out_ref.at[4 * px + 2 * py + pc], send_sem=send_sems.at[k - 1],
                recv_sem=recv_sems.at[k - 1], device_id=(px, py, pc), device_id_type=MESH))
        for cp in sends:
            cp.start()
        for cp in recvs:
            cp.wait_recv()
        for cp in sends:
            cp.wait_send()

    vm = pl.BlockSpec(memory_space=pltpu.VMEM)
    return pl.pallas_call(
        body, name=name, in_specs=[vm], out_specs=vm, out_shape=jax.ShapeDtypeStruct((8, r, c), v.dtype),
        scratch_shapes=[pltpu.SemaphoreType.DMA((7,)), pltpu.SemaphoreType.DMA((7,))],
        compiler_params=pltpu.CompilerParams(vmem_limit_bytes=VMEM_LIMIT),
    )(v)


def _as_rows(a, width):
    n = math.prod(a.shape)
    if n % width == 0:
        return a.reshape(-1, width)
    return jnp.pad(a.reshape(1, -1), ((0, 0), (0, -n % width))).reshape(-1, width)


def _n_rows(shape, width):
    return -(-math.prod(shape) // width)


def _pack_rows(arrays, rows, width):
    parts = [_as_rows(a, width) for a in arrays]
    used = sum(p.shape[0] for p in parts)
    return jnp.concatenate(parts + [jnp.zeros((rows - used, width), arrays[0].dtype)], axis=0)


def _unpack_rows(pack, shapes):
    width = pack.shape[1]
    out, off = [], 0
    for shp in shapes:
        nr, n = _n_rows(shp, width), math.prod(shp)
        part = pack[off:off + nr]
        out.append(part.reshape(shp) if n % width == 0 else part.reshape(-1)[:n].reshape(shp))
        off += nr
    return out


def _rows_for(shapes, width, mult=8):
    n = sum(_n_rows(s, width) for s in shapes)
    return -(-n // mult) * mult


def _gdn_fwd(x, p):
    proj = mm(x, p["w_main"], name="gdn_proj")
    ba = mm(x, p["w_gate"], name="gdn_proj_gate")
    q, k, v = (gdn_pre(proj, p["conv"], kind) for kind in "qkv")
    beta_b, g_b = gdn_gates(ba, p["alog_row"], p["dt_row"])
    o2, st, tinv = gdn_rec_fwd(q, k, v, beta_b, g_b)
    y = mixer_post(o2, proj, p["norm_g"], A_DK, 3 * A_HEADS, "gdn_post")
    m = mm(y, p["w_out"], name="gdn_out")
    return m, (x, proj, ba, q, k, v, beta_b, g_b, o2, st, tinv, y)


def _gdn_bwd(saved, p, dm):
    x, proj, ba, q, k, v, beta_b, g_b, o2, st, tinv, y = saved
    d_w_out = mm(y, dm, "tn", name="gdn_dw_out")
    dy = mm(dm, p["w_out"], "nt", name="gdn_dy")
    do, dz, d_norm_g = mixer_post_bwd(o2, proj, p["norm_g"], dy, A_DK, 3 * A_HEADS, "gdn_post_bwd")
    dq2, dk2, dv2, dbb, dgb = gdn_rec_bwd(q, k, v, beta_b, g_b, st, tinv, do)
    dba, d_alog_row, d_dt_row = gdn_gates_bwd(ba, p["alog_row"], p["dt_row"], dbb, dgb)
    du, dconv = zip(*(gdn_pre_bwd(proj, p["conv"], d2, kind) for d2, kind in ((dq2, "q"), (dk2, "k"), (dv2, "v"))))
    dproj = jnp.concatenate(list(du) + [dz], axis=1)
    d_w_main = mm(x, dproj, "tn", name="gdn_dw_main")
    d_w_gate = mm(x, dba, "tn", name="gdn_dw_gate")
    dx = mm(dba, p["w_gate"], "nt", epi="add", extra=dm, alpha=ALPHA, name="gdn_dx_gate")
    dx = mm(dproj, p["w_main"], "nt", epi="add", extra=dx, name="gdn_dx")
    grads = dict(w_in=jnp.concatenate([d_w_main, d_w_gate[:, :2 * N_GATE]], axis=1), conv=jnp.concatenate(dconv, axis=1),
                 alog=d_alog_row[0, N_GATE:2 * N_GATE].reshape(2, A_HEADS), dt=d_dt_row[0, N_GATE:2 * N_GATE].reshape(2, A_HEADS),
                 norm_g=d_norm_g[0], w_out=d_w_out)
    return dx, grads


def _gla_fwd(x, p):
    proj = mm(x, p["w_main"], name="gla_proj")
    gl = mm(x, p["w_gate"], name="gla_proj_gate")
    log_a = gla_gate(gl, p["w2"], p["gate_b"])
    o2, st = gla_rec_fwd(proj, log_a)
    y = mixer_post(o2, proj, p["norm_g"], B_DV, (2 * B_KW + B_VW) // B_DV, "gla_post")
    m = mm(y, p["w_out"], name="gla_out")
    return m, (x, proj, gl, log_a, o2, st, y)


def _gla_bwd(saved, p, dm):
    x, proj, gl, log_a, o2, st, y = saved
    d_w_out = mm(y, dm, "tn", name="gla_dw_out")
    dy = mm(dm, p["w_out"], "nt", name="gla_dy")
    do, dr, d_norm_g = mixer_post_bwd(o2, proj, p["norm_g"], dy, B_DV, (2 * B_KW + B_VW) // B_DV, "gla_post_bwd")
    dq2, dk2, dv2, dla = gla_rec_bwd(proj, log_a, st, do)
    dgl, dz, d_b0, d_b1 = gla_gate_bwd(gl, p["w2"], p["gate_b"], dla)
    d_w2 = [mm(gl, dz[n], "tn", name="gla_dw_gate_w2") for n in range(2)]
    dproj = jnp.concatenate([dq2[0] + dq2[1], dk2[0] + dk2[1], dv2[0] + dv2[1], dr], axis=1)
    d_w_main = mm(x, dproj, "tn", name="gla_dw_main")
    d_w_gate = mm(x, dgl, "tn", name="gla_dw_gate")
    dx = mm(dgl, p["w_gate"], "nt", epi="add", extra=dm, alpha=ALPHA, name="gla_dx_gate")
    dx = mm(dproj, p["w_main"], "nt", epi="add", extra=dx, name="gla_dx")
    grads = dict(w_in=jnp.concatenate([d_w_main, d_w_gate[:, :2 * B_RANK]], axis=1),
                 gate_w2=jnp.stack([d_w2[n][n * B_RANK:(n + 1) * B_RANK] for n in range(2)]),
                 gate_b=jnp.concatenate([d_b0, d_b1]), norm_g=d_norm_g[0], w_out=d_w_out)
    return dx, grads


def _pad_cols(w, width=LANES):
    return jnp.pad(w, ((0, 0), (0, width - w.shape[1])))


def _local_step(x, target, a_w_in, a_conv, a_alog, a_dt_bias, a_norm_g, a_w_out, b_w_in, b_gate_w2, b_gate_b, b_norm_g,
                b_w_out, ln1_g, ln1_b, mlp_w1, mlp_w2, ln2_g, ln2_b):
    layer_p = []
    for i in range(DEPTH):
        j = i // 2
        if i % 2 == 0:
            layer_p.append(dict(
                w_main=a_w_in[j][:, :4 * A_W], w_gate=_pad_cols(a_w_in[j][:, 4 * A_W:]), conv=a_conv[j],
                alog_row=jnp.pad(a_alog[j].reshape(1, N_GATE), ((0, 0), (N_GATE, LANES - 2 * N_GATE))),
                dt_row=jnp.pad(a_dt_bias[j].reshape(1, N_GATE), ((0, 0), (N_GATE, LANES - 2 * N_GATE))),
                norm_g=a_norm_g[j].reshape(1, A_DK), w_out=a_w_out[j]))
        else:
            w2 = jnp.stack([jnp.pad(b_gate_w2[j][n], ((n * B_RANK, LANES - (n + 1) * B_RANK), (0, 0))) for n in range(2)])
            layer_p.append(dict(
                w_main=b_w_in[j][:, :2 * B_KW + 2 * B_VW], w_gate=_pad_cols(b_w_in[j][:, 2 * B_KW + 2 * B_VW:]),
                w2=w2, gate_b=b_gate_b[j].reshape(2, 1, B_KW), norm_g=b_norm_g[j].reshape(1, B_DV), w_out=b_w_out[j]))

    saved = []
    h = x
    for i in range(DEPTH):
        p = layer_p[i]
        m, sv = (_gdn_fwd if i % 2 == 0 else _gla_fwd)(h, p)
        x1 = ln_fwd(h, m, ln1_g[i:i + 1], ln1_b[i:i + 1])
        h1 = mm(x1, mlp_w1[i], name="mlp_up")
        mlp = mm(h1, mlp_w2[i], act="sqrelu", name="mlp_down")
        x2 = ln_fwd(x1, mlp, ln2_g[i:i + 1], ln2_b[i:i + 1])
        saved.append((sv, h, m, x1, h1, mlp))
        h = x2

    dh, loss_part = loss_head(h, target)

    g_a, g_b, g_ln1g, g_ln1b, g_ln2g, g_ln2b, g_w1, g_w2 = {}, {}, {}, {}, {}, {}, {}, {}
    for i in reversed(range(DEPTH)):
        sv, xin, m, x1, h1, mlp = saved[i]
        p = layer_p[i]
        dr2, g_ln2g[i], g_ln2b[i] = ln_bwd(x1, mlp, ln2_g[i:i + 1], dh)
        g_w2[i] = mm(h1, dr2, "tn", act="sqrelu", name="mlp_dw_down")
        dh1 = mm(dr2, mlp_w2[i], "nt", epi="dsqrelu", extra=h1, name="mlp_dh")
        g_w1[i] = mm(x1, dh1, "tn", chip_major=True, name="mlp_dw_up")
        dx1 = mm(dh1, mlp_w1[i], "nt", epi="add", extra=dr2, alpha=ALPHA, name="mlp_dx")
        dr1, g_ln1g[i], g_ln1b[i] = ln_bwd(xin, m, ln1_g[i:i + 1], dx1)
        dh, g = (_gdn_bwd if i % 2 == 0 else _gla_bwd)(sv, p, dr1)
        (g_a if i % 2 == 0 else g_b)[i // 2] = g

    per_layer = lambda d, key=None: [(d[i] if key is None else d[i][key]) for i in sorted(d)]
    st = lambda d, key=None: jnp.stack(per_layer(d, key))
    grads = dict(
        a_w_in=per_layer(g_a, "w_in"), a_conv=st(g_a, "conv"), a_alog=st(g_a, "alog"), a_dt_bias=st(g_a, "dt"),
        a_norm_g=st(g_a, "norm_g"), a_w_out=per_layer(g_a, "w_out"), b_w_in=per_layer(g_b, "w_in"),
        b_gate_w2=st(g_b, "gate_w2"), b_gate_b=st(g_b, "gate_b"), b_norm_g=st(g_b, "norm_g"),
        b_w_out=per_layer(g_b, "w_out"), ln1_g=st(g_ln1g)[:, 0], ln1_b=st(g_ln1b)[:, 0], mlp_w1=per_layer(g_w1),
        mlp_w2=per_layer(g_w2), ln2_g=st(g_ln2g)[:, 0], ln2_b=st(g_ln2b)[:, 0])
    return loss_part, dh, grads


WEIGHTS = ("a_w_in", "a_conv", "a_alog", "a_dt_bias", "a_norm_g", "a_w_out", "b_w_in", "b_gate_w2", "b_gate_b",
           "b_norm_g", "b_w_out", "ln1_g", "ln1_b", "mlp_w1", "mlp_w2", "ln2_g", "ln2_b")
BIG = ("mlp_w1", "mlp_w2", "a_w_out", "b_w_out", "a_w_in", "b_w_in")
SHARD_AXIS = {"mlp_w1": 2, "mlp_w2": 1, "a_w_out": 1, "b_w_out": 1, "a_w_in": 2, "b_w_in": 2}
SMALL = tuple(n for n in WEIGHTS if n not in BIG)
SMALL_SHARD_AXIS = {"a_conv": 2, "b_gate_w2": 3, "b_gate_b": 2, "b_norm_g": 1}


def _to_chip_major(full, axis):
    shp = full.shape
    t = full.reshape(shp[:axis] + (4, shp[axis] // 4) + shp[axis + 1:])
    return jnp.moveaxis(t, axis, 0)


def _from_chip_major(stacked, axis):
    t = jnp.moveaxis(stacked, 0, axis)
    shp = t.shape
    return t.reshape(shp[:axis] + (shp[axis] * shp[axis + 1],) + shp[axis + 2:])


def kernel(x, a_w_in, a_conv, a_alog, a_dt_bias, a_norm_g, a_w_out, b_w_in, b_gate_w2, b_gate_b, b_norm_g, b_w_out, ln1_g, ln1_b, mlp_w1, mlp_w2, ln2_g, ln2_b, loss_target, m_a_w_in, m_a_conv, m_a_alog, m_a_dt_bias, m_a_norm_g, m_a_w_out, m_b_w_in, m_b_gate_w2, m_b_gate_b, m_b_norm_g, m_b_w_out, m_ln1_g, m_ln1_b, m_mlp_w1, m_mlp_w2, m_ln2_g, m_ln2_b, v_a_w_in, v_a_conv, v_a_alog, v_a_dt_bias, v_a_norm_g, v_a_w_out, v_b_w_in, v_b_gate_w2, v_b_gate_b, v_b_norm_g, v_b_w_out, v_ln1_g, v_ln1_b, v_mlp_w1, v_mlp_w2, v_ln2_g, v_ln2_b):
    w = dict(a_w_in=a_w_in, a_conv=a_conv, a_alog=a_alog, a_dt_bias=a_dt_bias, a_norm_g=a_norm_g, a_w_out=a_w_out,
             b_w_in=b_w_in, b_gate_w2=b_gate_w2, b_gate_b=b_gate_b, b_norm_g=b_norm_g, b_w_out=b_w_out, ln1_g=ln1_g,
             ln1_b=ln1_b, mlp_w1=mlp_w1, mlp_w2=mlp_w2, ln2_g=ln2_g, ln2_b=ln2_b)
    mom = dict(a_w_in=m_a_w_in, a_conv=m_a_conv, a_alog=m_a_alog, a_dt_bias=m_a_dt_bias, a_norm_g=m_a_norm_g,
               a_w_out=m_a_w_out, b_w_in=m_b_w_in, b_gate_w2=m_b_gate_w2, b_gate_b=m_b_gate_b, b_norm_g=m_b_norm_g,
               b_w_out=m_b_w_out, ln1_g=m_ln1_g, ln1_b=m_ln1_b, mlp_w1=m_mlp_w1, mlp_w2=m_mlp_w2, ln2_g=m_ln2_g,
               ln2_b=m_ln2_b)
    var = dict(a_w_in=v_a_w_in, a_conv=v_a_conv, a_alog=v_a_alog, a_dt_bias=v_a_dt_bias, a_norm_g=v_a_norm_g,
               a_w_out=v_a_w_out, b_w_in=v_b_w_in, b_gate_w2=v_b_gate_w2, b_gate_b=v_b_gate_b, b_norm_g=v_b_norm_g,
               b_w_out=v_b_w_out, ln1_g=v_ln1_g, ln1_b=v_ln1_b, mlp_w1=v_mlp_w1, mlp_w2=v_mlp_w2, ln2_g=v_ln2_g,
               ln2_b=v_ln2_b)
    chip = 2 * lax.axis_index("x") + lax.axis_index("y")

    seg_rows = [w[n].size // D_MODEL for n in BIG]
    seg_off = [sum(seg_rows[:i]) for i in range(len(BIG))]
    rows = -(-sum(seg_rows) // PACK_TILE) * PACK_TILE
    shard_pack = jnp.concatenate([w[n].reshape(-1, D_MODEL) for n in BIG]
                                 + [jnp.zeros((rows - sum(seg_rows), D_MODEL), F32)], axis=0)
    chip_idx = chip.astype(jnp.int32).reshape(1)
    gathered = allgather_chips(cast_into_slot(shard_pack, chip_idx))
    full = {}
    for n, off, nr in zip(BIG, seg_off, seg_rows):
        if n in ("mlp_w1", "mlp_w2"):
            kind = "cols" if SHARD_AXIS[n] == 2 else "rows"
            full[n] = [Gathered(gathered, off + i * D_MODEL, kind) for i in range(DEPTH)]
            continue
        stacked = gathered[:, off:off + nr].reshape((4,) + w[n].shape)
        full[n] = _from_chip_major(stacked, SHARD_AXIS[n])
    sharded_small = tuple(SMALL_SHARD_AXIS)
    sm_shapes = [w[n].shape for n in sharded_small]
    sm_rows = _rows_for(sm_shapes, LANES)
    sm_all = exchange_all(_pack_rows([w[n] for n in sharded_small], sm_rows, LANES), "gather_small")
    per_chip = [_unpack_rows(sm_all[2 * pch], sm_shapes) for pch in range(4)]
    for idx, n in enumerate(sharded_small):
        full[n] = jnp.concatenate([per_chip[pch][idx] for pch in range(4)], axis=SMALL_SHARD_AXIS[n])
    for n in WEIGHTS:
        full.setdefault(n, w[n])

    loss_part, grad_x, grads = _local_step(x[0], loss_target[0], *[full[n] for n in WEIGHTS])
    loss = lax.psum(jnp.sum(loss_part), ("x", "y", "c"))

    gpack = jnp.concatenate(
        [(g if n == "mlp_w1" else _to_chip_major(g, SHARD_AXIS[n] - 1).reshape(4, -1, D_MODEL))
         for n in BIG for g in grads[n]]
        + [jnp.zeros((4, rows - sum(seg_rows), D_MODEL), F32)], axis=1)
    core = lax.axis_index("c").astype(jnp.int32).reshape(1)
    chip_sum = add_sibling_half(gpack, sibling_halves(gpack), core)
    reduced = join_halves(sum_received(chip_sum, scatter_chips(chip_sum), chip_idx, core))
    out_g, out_d, out_m, out_v = {}, {}, {}, {}
    for n, off, nr in zip(BIG, seg_off, seg_rows):
        if w[n].shape[-1] == D_MODEL:
            view = lambda t: t.reshape(-1, D_MODEL)
            res = adamw(view(w[n]), view(mom[n]), view(var[n]), (reduced,), off, "adamw_" + n)
        else:
            cols = w[n].shape[-1]
            view = lambda t: t.reshape(-1, cols)
            res = adamw(view(w[n]), view(mom[n]), view(var[n]), (view(reduced[off:off + nr]),), 0, "adamw_" + n)
        out_g[n], out_d[n], out_m[n], out_v[n] = (t.reshape(w[n].shape) for t in res)

    all_shapes = [full[n].shape for n in SMALL]
    g_rows = _rows_for(all_shapes, LANES)
    g_all = exchange_all(_pack_rows([grads[n] for n in SMALL], g_rows, LANES), "gather_small_grads")
    g_sum = _unpack_rows(sum_slots(g_all, "sum_small_grads"), all_shapes)
    g_mine = []
    for n, g in zip(SMALL, g_sum):
        if n in SMALL_SHARD_AXIS:
            ax = SMALL_SHARD_AXIS[n]
            g = lax.dynamic_slice_in_dim(g, chip * w[n].shape[ax], w[n].shape[ax], axis=ax)
        g_mine.append(g)
    my_shapes = [w[n].shape for n in SMALL]
    s_rows = _rows_for(my_shapes, LANES)
    pk = lambda d: _pack_rows([d[n] for n in SMALL], s_rows, LANES)
    res = adamw(pk(w), pk(mom), pk(var), (_pack_rows(g_mine, s_rows, LANES),), 0, "adamw_small")
    for dst, pack in zip((out_g, out_d, out_m, out_v), res):
        for n, t in zip(SMALL, _unpack_rows(pack, my_shapes)):
            dst[n] = t

    return (loss, grad_x[None], *[out_g[n] for n in WEIGHTS], *[out_d[n] for n in WEIGHTS],
            *[out_m[n] for n in WEIGHTS], *[out_v[n] for n in WEIGHTS])
```

```python
import functools
import math

import jax
import jax.numpy as jnp
from jax import lax
from jax.experimental import pallas as pl
from jax.experimental.pallas import tpu as pltpu

F32 = jnp.float32
BF16 = jnp.bfloat16

D_MODEL = 1024
DEPTH = 4
CHUNK = 64
A_HEADS = 8
A_DK = 128
A_W = 1024
A_CONV = 5
B_HEADS = 4
B_DK = 128
B_DV = 256
B_RANK = 16
B_TAU = 16.0
B_KW = 512
B_VW = 1024
ALPHA = (2 * DEPTH) ** 0.25
LN_EPS = 1e-5
RMS_EPS = 1e-6
L2_EPS = 1e-6
ADAM_LR = 0.001
ADAM_B1 = 0.9
ADAM_B2 = 0.999
ADAM_EPS = 1e-08
ADAM_WD = 0.01
ADAM_STEP = 10
LANES = 128
NEG_INF = float("-inf")
VMEM_LIMIT = 56 * 1024 * 1024


def _cparams(sem=None):
    return pltpu.CompilerParams(dimension_semantics=sem, vmem_limit_bytes=VMEM_LIMIT)


def _dg(a, b, ca, cb):
    return lax.dot_general(a.astype(BF16), b.astype(BF16), (((ca,), (cb,)), ((), ())),
                           preferred_element_type=F32)


def _split(x):
    hi = x.astype(BF16)
    return hi, (x - hi.astype(F32)).astype(BF16)


def _dg3(a, b, ca, cb):
    (a1, a2), (b1, b2) = _split(a), _split(b)
    return (_dg(a1, b2, ca, cb) + _dg(a2, b1, ca, cb)) + _dg(a1, b1, ca, cb)


def _dot_with_vjp(dg):
    @functools.partial(jax.custom_vjp, nondiff_argnums=(2, 3))
    def dot(a, b, ca, cb):
        return dg(a, b, ca, cb)

    def fwd(a, b, ca, cb):
        return dg(a, b, ca, cb), (a, b)

    def bwd(ca, cb, res, g):
        a, b = res
        da = dg(g, b, 1, 1 - cb) if ca == 1 else dg(b, g, 1 - cb, 1)
        db = dg(a, g, 1 - ca, 0) if cb == 0 else dg(g, a, 0, 1 - ca)
        return da, db

    dot.defvjp(fwd, bwd)
    return dot


bdot = _dot_with_vjp(_dg)
xdot3 = _dot_with_vjp(_dg3)


def nn(a, b):
    return bdot(a, b, 1, 0)


def nt(a, b):
    return bdot(a, b, 1, 1)


def tn(a, b):
    return bdot(a, b, 0, 0)


def xdot(a, b):
    return xdot3(a, b, 1, 0)


def _sigmoid(x):
    return 1.0 / (1.0 + jnp.exp(-x))


def _softplus(x):
    return jnp.maximum(x, 0.0) + jnp.log(1.0 + jnp.exp(-jnp.abs(x)))


def _chunk_masks(rev):
    ii = lax.broadcasted_iota(jnp.int32, (CHUNK, CHUNK), 0)
    jj = lax.broadcasted_iota(jnp.int32, (CHUNK, CHUNK), 1)
    d = (ii - jj) * (1 - 2 * rev)
    return d >= 0, d > 0, ii == jj, (ii >> 3) == (jj >> 3)


def _each(f, *lists):
    return [f(*xs) for xs in zip(*lists)]


@jax.custom_vjp
def _unit_triangular_inverse(a, ident, blockdiag):
    return _unit_triangular_inverse_impl(a, ident, blockdiag)


def _unit_triangular_inverse_fwd(a, ident, blockdiag):
    t = _unit_triangular_inverse_impl(a, ident, blockdiag)
    return t, (t, ident, blockdiag)


def _unit_triangular_inverse_bwd(res, g):
    t, ident, blockdiag = res
    left = _each(lambda x, y: xdot3(x, y, 0, 0), t, g)
    da = _each(lambda x, y: -xdot3(x, y, 1, 1), left, t)
    return da, jnp.zeros_like(ident), jnp.zeros_like(blockdiag)


_unit_triangular_inverse.defvjp(_unit_triangular_inverse_fwd, _unit_triangular_inverse_bwd)


@jax.custom_vjp
def _known_inverse(a, t):
    return t


def _known_inverse_bwd(t, g):
    left = _each(lambda x, y: xdot3(x, y, 0, 0), t, g)
    return _each(lambda x, y: -xdot3(x, y, 1, 1), left, t), _each(jnp.zeros_like, t)


_known_inverse.defvjp(lambda a, t: (t, t), _known_inverse_bwd)


def _unit_triangular_inverse_impl(a, ident, blockdiag):
    ad = _each(lambda x: x * blockdiag, a)
    e = _each(lambda x, y: x - y, a, ad)
    dinv = _each(lambda x: ident - x, ad)
    p = _each(xdot, ad, ad)
    dinv = _each(lambda x, y: x + xdot(x, y), dinv, p)
    p = _each(xdot, p, p)
    dinv = _each(lambda x, y: x + xdot(x, y), dinv, p)
    g = _each(lambda x, y: -xdot(x, y), dinv, e)
    finv = _each(lambda x: ident + x, g)
    p = _each(xdot, g, g)
    finv = _each(lambda x, y: x + xdot(x, y), finv, p)
    p = _each(xdot, p, p)
    finv = _each(lambda x, y: x + xdot(x, y), finv, p)
    return _each(xdot, finv, dinv)


def _gdn_step(state, q, k, v, bb, gb, rev, t_saved=None):
    causal, strict, eye, blockdiag = _chunk_masks(rev)
    lower = causal.astype(F32)
    ones = jnp.ones((CHUNK, CHUNK), F32)
    gcb = _each(lambda x: xdot(lower, x), gb)
    gcol = _each(lambda x: x[:, :CHUNK], gcb)
    grow = _each(lambda x: xdot(ones, jnp.where(eye, x, 0.0)), gcol)
    decay = _each(lambda x, y: jnp.exp(jnp.where(causal, x - y, NEG_INF)), gcol, grow)
    kb = _each(lambda x, y: x * y, k, bb)
    a = _each(lambda x, y, z: jnp.where(strict, nt(x, y) * z, 0.0), kb, k, decay)
    if t_saved is None:
        t = _unit_triangular_inverse(a, eye.astype(F32), blockdiag.astype(F32))
    else:
        t = _known_inverse(a, t_saved)
    egc = _each(jnp.exp, gcb)
    u = _each(lambda x, y, z: xdot(x, y * z), t, v, bb)
    w = _each(lambda x, y, z: xdot(x, y * z), t, kb, egc)
    qk = _each(lambda x, y, z: nt(x, y) * z, q, k, decay)
    glast = _each(lambda x: jnp.sum(x, axis=0, keepdims=True), gb)
    v_new = _each(lambda x, y, z: x - nn(y, z), u, w, state)
    o = _each(lambda x, y, z, p, r: nn(x * y, z) + nn(p, r), q, egc, state, qk, v_new)
    k_dec = _each(lambda x, y, z: x * jnp.exp(y - z), k, glast, gcb)
    state_new = _each(lambda x, y, z, p: x * jnp.exp(y) + tn(z, p), state, glast, k_dec, v_new)
    return state_new, o, t


def _gla_step(state_t, q, k, v, la, rev):
    causal, _, _, _ = _chunk_masks(rev)
    lower = causal.astype(F32)
    sign = 1 - 2 * rev
    b = _each(lambda x: xdot(lower, x), la)
    q = _each(lambda x: x * (B_DK ** -0.5), q)
    row = lax.broadcasted_iota(jnp.int32, (CHUNK, B_DK), 0)
    sub = row // GLA_SUB
    scores = None
    for blk in range(CHUNK // GLA_SUB):
        r_at = jnp.where(rev == 1, GLA_SUB * (blk + 1), GLA_SUB * blk - 1)
        r = _each(lambda x: jnp.sum(jnp.where(row == r_at, x, 0.0), axis=0, keepdims=True), b)
        q_blk = _each(lambda x, y, z: x * jnp.exp(jnp.where(sub == blk, y - z, NEG_INF)), q, b, r)
        k_past = _each(lambda x, y, z: x * jnp.exp(jnp.where((sub - blk) * sign < 0, z - y, NEG_INF)), k, b, r)
        part = _each(lambda x, y: xdot3(x, y, 1, 1), q_blk, k_past)
        scores = part if scores is None else _each(lambda x, y: x + y, scores, part)
    shp = (GLA_SUB, GLA_SUB, B_DK)
    d3 = (lax.broadcasted_iota(jnp.int32, shp, 0) - lax.broadcasted_iota(jnp.int32, shp, 1)) * sign
    place_r = lax.broadcasted_iota(jnp.int32, (GLA_SUB, CHUNK), 0)
    place_c = lax.broadcasted_iota(jnp.int32, (GLA_SUB, CHUNK), 1)
    diag = []
    for blk in range(CHUNK // GLA_SUB):
        rows = slice(blk * GLA_SUB, (blk + 1) * GLA_SUB)
        place = (place_c == place_r + blk * GLA_SUB).astype(F32)

        def pairs(qh, kh, bh):
            qb, kb, bb = qh[rows], kh[rows], bh[rows]
            dec = jnp.exp(jnp.where(d3 >= 0, bb[:, None, :] - bb[None, :, :], NEG_INF))
            return xdot(jnp.sum(qb[:, None, :] * kb[None, :, :] * dec, axis=-1), place)

        diag.append(_each(pairs, q, k, b))
    scores = _each(lambda x, *d: x + jnp.concatenate(d, axis=0), scores, *diag)
    blast = _each(lambda x: jnp.sum(x, axis=0, keepdims=True), la)
    o = _each(lambda x, y, z, s, w: nt(x * jnp.exp(y), z) + nn(s, w), q, b, state_t, scores, v)
    k_dec = _each(lambda x, y, z: x * jnp.exp(y - z), k, blast, b)
    state_new = _each(lambda x, y, z, w: jnp.exp(x) * y + tn(z, w), blast, state_t, v, k_dec)
    return state_new, o


def _chunk_pos(d, m, n):
    return m + d * (n - 1 - 2 * m)


GLA_SUB = 16
GDN_HEADS_PER_STEP = 8
def gdn_rec_fwd(q, k, v, beta_b, g_b):
    s = q.shape[0]
    n = s // CHUNK

    hb = GDN_HEADS_PER_STEP
    wide = hb * LANES

    def body(q_ref, k_ref, v_ref, bb_ref, gb_ref, o_ref, st_ref, t_ref, state):
        d = pl.program_id(0)

        @pl.when(pl.program_id(2) == 0)
        def _():
            state[...] = jnp.zeros_like(state)

        cols = [slice(hh * LANES, (hh + 1) * LANES) for hh in range(hb)]
        st = [state[hh] for hh in range(hb)]
        new, o, t = _gdn_step(st, *([r[:, c] for c in cols] for r in (q_ref, k_ref, v_ref, bb_ref, gb_ref)), d)
        for hh, c in enumerate(cols):
            st_ref[hh] = st[hh]
            t_ref[hh] = t[hh]
            state[hh] = new[hh]
            o_ref[:, c] = o[hh]

    blk = pl.BlockSpec((CHUNK, wide), lambda d, h, m: (_chunk_pos(d, m, n), h))
    gate = pl.BlockSpec((CHUNK, wide), lambda d, h, m: (_chunk_pos(d, m, n), d * (A_HEADS // hb) + h))
    return pl.pallas_call(
        body, name="gdn_rec_fwd", grid=(2, A_HEADS // hb, n),
        in_specs=[blk, blk, blk, gate, gate],
        out_specs=[pl.BlockSpec((None, CHUNK, wide), lambda d, h, m: (d, _chunk_pos(d, m, n), h)),
                   pl.BlockSpec((None, hb, None, A_DK, LANES), lambda d, h, m: (d, h, _chunk_pos(d, m, n), 0, 0)),
                   pl.BlockSpec((None, hb, None, CHUNK, CHUNK), lambda d, h, m: (d, h, _chunk_pos(d, m, n), 0, 0))],
        out_shape=[jax.ShapeDtypeStruct((2, s, A_W), F32), jax.ShapeDtypeStruct((2, A_HEADS, n, A_DK, LANES), F32),
                   jax.ShapeDtypeStruct((2, A_HEADS, n, CHUNK, CHUNK), F32)],
        scratch_shapes=[pltpu.VMEM((hb, A_DK, LANES), F32)],
        compiler_params=_cparams(("arbitrary", "arbitrary", "arbitrary")),
    )(q, k, v, beta_b, g_b)


def gdn_rec_bwd(q, k, v, beta_b, g_b, states, tinv, do):
    s = q.shape[0]
    n = s // CHUNK

    hb = GDN_HEADS_PER_STEP
    wide = hb * LANES

    def body(q_ref, k_ref, v_ref, bb_ref, gb_ref, st_ref, t_ref, do_ref, dq_ref, dk_ref, dv_ref, dbb_ref, dgb_ref, dstate):
        d = pl.program_id(0)

        @pl.when(pl.program_id(2) == 0)
        def _():
            dstate[...] = jnp.zeros_like(dstate)

        def step(*a):
            return _gdn_step(*a, d, t_saved=[t_ref[hh] for hh in range(hb)])[:2]

        cols = [slice(hh * LANES, (hh + 1) * LANES) for hh in range(hb)]
        _, vjp = jax.vjp(step, [st_ref[hh] for hh in range(hb)],
                         *([r[:, c] for c in cols] for r in (q_ref, k_ref, v_ref, bb_ref, gb_ref)))
        grads = vjp(([dstate[hh] for hh in range(hb)], [do_ref[:, c] for c in cols]))
        for hh, c in enumerate(cols):
            dstate[hh], dq_ref[:, c], dk_ref[:, c], dv_ref[:, c], dbb_ref[:, c], dgb_ref[:, c] = (g[hh] for g in grads)

    pos = lambda d, m: _chunk_pos(1 - d, m, n)
    blk = pl.BlockSpec((CHUNK, wide), lambda d, h, m: (pos(d, m), h))
    gate = pl.BlockSpec((CHUNK, wide), lambda d, h, m: (pos(d, m), d * (A_HEADS // hb) + h))
    oblk = pl.BlockSpec((None, CHUNK, wide), lambda d, h, m: (d, pos(d, m), h))
    return pl.pallas_call(
        body, name="gdn_rec_bwd", grid=(2, A_HEADS // hb, n),
        in_specs=[blk, blk, blk, gate, gate,
                  pl.BlockSpec((None, hb, None, A_DK, LANES), lambda d, h, m: (d, h, pos(d, m), 0, 0)),
                  pl.BlockSpec((None, hb, None, CHUNK, CHUNK), lambda d, h, m: (d, h, pos(d, m), 0, 0)), blk],
        out_specs=[oblk, oblk, oblk, gate, gate],
        out_shape=[jax.ShapeDtypeStruct((2, s, A_W), F32)] * 3 + [jax.ShapeDtypeStruct(beta_b.shape, F32)] * 2,
        scratch_shapes=[pltpu.VMEM((hb, A_DK, LANES), F32)],
        compiler_params=_cparams(("arbitrary", "arbitrary", "arbitrary")),
    )(q, k, v, beta_b, g_b, states, tinv, do)


def gla_rec_fwd(proj, log_a):
    s = proj.shape[0]
    n = s // CHUNK

    kcols = [slice(h * B_DK, (h + 1) * B_DK) for h in range(B_HEADS)]
    vcols = [slice(h * B_DV, (h + 1) * B_DV) for h in range(B_HEADS)]

    def body(q_ref, k_ref, v_ref, la_ref, o_ref, st_ref, state):
        d = pl.program_id(0)

        @pl.when(pl.program_id(1) == 0)
        def _():
            state[...] = jnp.zeros_like(state)

        st = [state[h] for h in range(B_HEADS)]
        new, o = _gla_step(st, [q_ref[:, c] for c in kcols], [k_ref[:, c] for c in kcols], [v_ref[:, c] for c in vcols],
                           [la_ref[:, c] for c in kcols], d)
        for h in range(B_HEADS):
            st_ref[h] = st[h]
            state[h] = new[h]
            o_ref[:, vcols[h]] = o[h]

    pos = lambda d, m: _chunk_pos(d, m, n)
    return pl.pallas_call(
        body, name="gla_rec_fwd", grid=(2, n),
        in_specs=[pl.BlockSpec((CHUNK, B_KW), lambda d, m: (pos(d, m), 0)),
                  pl.BlockSpec((CHUNK, B_KW), lambda d, m: (pos(d, m), 1)),
                  pl.BlockSpec((CHUNK, B_VW), lambda d, m: (pos(d, m), 2 * B_KW // B_VW)),
                  pl.BlockSpec((None, CHUNK, B_KW), lambda d, m: (d, pos(d, m), 0))],
        out_specs=[pl.BlockSpec((None, CHUNK, B_VW), lambda d, m: (d, pos(d, m), 0)),
                   pl.BlockSpec((None, B_HEADS, None, B_DV, B_DK), lambda d, m: (d, 0, pos(d, m), 0, 0))],
        out_shape=[jax.ShapeDtypeStruct((2, s, B_VW), F32), jax.ShapeDtypeStruct((2, B_HEADS, n, B_DV, B_DK), F32)],
        scratch_shapes=[pltpu.VMEM((B_HEADS, B_DV, B_DK), F32)],
        compiler_params=_cparams(("arbitrary", "arbitrary")),
    )(proj, proj, proj, log_a)


def gla_rec_bwd(proj, log_a, states, do):
    s = proj.shape[0]
    n = s // CHUNK

    kcols = [slice(h * B_DK, (h + 1) * B_DK) for h in range(B_HEADS)]
    vcols = [slice(h * B_DV, (h + 1) * B_DV) for h in range(B_HEADS)]

    def body(q_ref, k_ref, v_ref, la_ref, st_ref, do_ref, dq_ref, dk_ref, dv_ref, dla_ref, dstate):
        d = pl.program_id(0)

        @pl.when(pl.program_id(1) == 0)
        def _():
            dstate[...] = jnp.zeros_like(dstate)

        step = functools.partial(_gla_step, rev=d)
        _, vjp = jax.vjp(step, [st_ref[h] for h in range(B_HEADS)], [q_ref[:, c] for c in kcols],
                         [k_ref[:, c] for c in kcols], [v_ref[:, c] for c in vcols], [la_ref[:, c] for c in kcols])
        dst, dq, dk, dv, dla = vjp(([dstate[h] for h in range(B_HEADS)], [do_ref[:, c] for c in vcols]))
        for h in range(B_HEADS):
            dstate[h] = dst[h]
            dq_ref[:, kcols[h]] = dq[h]
            dk_ref[:, kcols[h]] = dk[h]
            dv_ref[:, vcols[h]] = dv[h]
            dla_ref[:, kcols[h]] = dla[h]

    pos = lambda d, m: _chunk_pos(1 - d, m, n)
    kblk = pl.BlockSpec((None, CHUNK, B_KW), lambda d, m: (d, pos(d, m), 0))
    return pl.pallas_call(
        body, name="gla_rec_bwd", grid=(2, n),
        in_specs=[pl.BlockSpec((CHUNK, B_KW), lambda d, m: (pos(d, m), 0)),
                  pl.BlockSpec((CHUNK, B_KW), lambda d, m: (pos(d, m), 1)),
                  pl.BlockSpec((CHUNK, B_VW), lambda d, m: (pos(d, m), 2 * B_KW // B_VW)),
                  kblk,
                  pl.BlockSpec((None, B_HEADS, None, B_DV, B_DK), lambda d, m: (d, 0, pos(d, m), 0, 0)),
                  pl.BlockSpec((CHUNK, B_VW), lambda d, m: (pos(d, m), 0))],
        out_specs=[kblk, kblk, pl.BlockSpec((None, CHUNK, B_VW), lambda d, m: (d, pos(d, m), 0)), kblk],
        out_shape=[jax.ShapeDtypeStruct((2, s, B_KW), F32), jax.ShapeDtypeStruct((2, s, B_KW), F32),
                   jax.ShapeDtypeStruct((2, s, B_VW), F32), jax.ShapeDtypeStruct((2, s, B_KW), F32)],
        scratch_shapes=[pltpu.VMEM((B_HEADS, B_DV, B_DK), F32)],
        compiler_params=_cparams(("arbitrary", "arbitrary")),
    )(proj, proj, proj, log_a, states, do)


MM_TILE_OUT = 1024
MM_TILE_K = 1024


def _tile(n, pref):
    return pref if n % pref == 0 else n


class Gathered:
    def __init__(self, g, off, kind):
        assert off % D_MODEL == 0 and MM_TILE_OUT == D_MODEL and MM_TILE_K == D_MODEL
        self.g, self.blk, self.kind = g, off // D_MODEL, kind
        self.shape = (D_MODEL, 4 * D_MODEL) if kind == "cols" else (4 * D_MODEL, D_MODEL)

    def spec(self, mode):
        blk = self.blk
        chip_is_k = (self.kind == "rows") == (mode == "nn")
        if chip_is_k:
            return pl.BlockSpec((None, D_MODEL, D_MODEL), lambda i, j, k: (k, blk, 0))
        return pl.BlockSpec((None, D_MODEL, D_MODEL), lambda i, j, k: (j, blk, 0))


def mm(a, b, mode="nn", act=None, epi=None, extra=None, alpha=1.0, chip_major=False, name="mm"):
    if mode == "tn":
        kk, m = a.shape
    else:
        m, kk = a.shape
    nn_ = b.shape[0] if mode == "nt" else b.shape[1]
    tm, tn_, tk = _tile(m, MM_TILE_OUT), _tile(nn_, MM_TILE_OUT), _tile(kk, MM_TILE_K)
    nk = kk // tk
    ca, cb = {"nn": (1, 0), "nt": (1, 1), "tn": (0, 0)}[mode]

    def body(*refs):
        if epi is None:
            a_ref, b_ref, o_ref = refs
        else:
            a_ref, b_ref, e_ref, o_ref = refs
        kstep = pl.program_id(2)
        at = a_ref[...]
        if act == "sqrelu":
            at = jnp.square(jnp.maximum(at, 0.0))
        part = _dg(at, b_ref[...], ca, cb)

        @pl.when(kstep == 0)
        def _():
            o_ref[...] = part

        @pl.when(kstep > 0)
        def _():
            o_ref[...] += part

        if epi is not None:
            @pl.when(kstep == nk - 1)
            def _():
                if epi == "dsqrelu":
                    o_ref[...] = o_ref[...] * (2.0 * jnp.maximum(e_ref[...], 0.0))
                else:
                    o_ref[...] = o_ref[...] + alpha * e_ref[...]

    a_spec = pl.BlockSpec((tk, tm), lambda i, j, k: (k, i)) if mode == "tn" else pl.BlockSpec((tm, tk), lambda i, j, k: (i, k))
    if isinstance(b, Gathered):
        assert mode in ("nn", "nt") and tn_ == D_MODEL and tk == D_MODEL
        b_spec, b = b.spec(mode), b.g
    elif mode == "nt":
        b_spec = pl.BlockSpec((tn_, tk), lambda i, j, k: (j, k))
    else:
        b_spec = pl.BlockSpec((tk, tn_), lambda i, j, k: (k, j))
    o_spec = pl.BlockSpec((tm, tn_), lambda i, j, k: (i, j))
    ins, specs = [a, b], [a_spec, b_spec]
    if epi is not None:
        ins.append(extra)
        specs.append(o_spec)
    out_shape = jax.ShapeDtypeStruct((m, nn_), F32)
    if chip_major:
        assert nn_ == 4 * D_MODEL and tn_ == D_MODEL
        o_spec = pl.BlockSpec((None, tm, D_MODEL), lambda i, j, k: (j, i, 0))
        out_shape = jax.ShapeDtypeStruct((4, m, D_MODEL), F32)
    return pl.pallas_call(
        body, name=name, grid=(m // tm, nn_ // tn_, nk), in_specs=specs, out_specs=o_spec, out_shape=out_shape,
        compiler_params=_cparams(("parallel", "parallel", "arbitrary")),
    )(*ins)


ROWS = 256
POST_ROWS = 1024


def _ln_core(x, m, g, b):
    r = ALPHA * x + m
    mu = jnp.mean(r, axis=-1, keepdims=True)
    xc = r - mu
    var = jnp.mean(xc * xc, axis=-1, keepdims=True)
    rstd = lax.rsqrt(var + LN_EPS)
    xhat = xc * rstd
    return xhat * g + b, xhat, rstd


def ln_fwd(x, m, g, b):
    s, dm = x.shape

    def body(x_ref, m_ref, g_ref, b_ref, o_ref):
        o_ref[...] = _ln_core(x_ref[...], m_ref[...], g_ref[...], b_ref[...])[0]

    row = pl.BlockSpec((ROWS, dm), lambda i: (i, 0))
    vec = pl.BlockSpec((1, dm), lambda i: (0, 0))
    return pl.pallas_call(body, name="ln_fwd", grid=(s // ROWS,), in_specs=[row, row, vec, vec], out_specs=row,
                          out_shape=jax.ShapeDtypeStruct((s, dm), F32), compiler_params=_cparams(("parallel",)))(x, m, g, b)


def ln_bwd(x, m, g, dy):
    s, dm = x.shape

    def body(x_ref, m_ref, g_ref, dy_ref, dr_ref, dg_ref, db_ref):
        gv = g_ref[...]
        _, xhat, rstd = _ln_core(x_ref[...], m_ref[...], gv, jnp.zeros_like(gv))
        dy = dy_ref[...]
        dxh = dy * gv
        dr_ref[...] = rstd * (dxh - jnp.mean(dxh, axis=-1, keepdims=True)
                              - xhat * jnp.mean(dxh * xhat, axis=-1, keepdims=True))

        @pl.when(pl.program_id(0) == 0)
        def _():
            dg_ref[...] = jnp.zeros_like(dg_ref)
            db_ref[...] = jnp.zeros_like(db_ref)

        dg_ref[...] += jnp.sum(dy * xhat, axis=0, keepdims=True)
        db_ref[...] += jnp.sum(dy, axis=0, keepdims=True)

    row = pl.BlockSpec((ROWS, dm), lambda i: (i, 0))
    vec = pl.BlockSpec((1, dm), lambda i: (0, 0))
    return pl.pallas_call(body, name="ln_bwd", grid=(s // ROWS,), in_specs=[row, row, vec, row], out_specs=[row, vec, vec],
                          out_shape=[jax.ShapeDtypeStruct((s, dm), F32), jax.ShapeDtypeStruct((1, dm), F32),
                                     jax.ShapeDtypeStruct((1, dm), F32)],
                          compiler_params=_cparams(("arbitrary",)))(x, m, g, dy)


def loss_head(y, target):
    s, dm = y.shape

    def body(y_ref, t_ref, dy_ref, l_ref):
        e = y_ref[...] - t_ref[...]
        dy_ref[...] = e * (1.0 / dm)

        @pl.when(pl.program_id(0) == 0)
        def _():
            l_ref[...] = jnp.zeros_like(l_ref)

        col = jnp.sum(e * e, axis=0, keepdims=True) * (0.5 / dm)
        acc = col[:, :LANES]
        for c in range(1, dm // LANES):
            acc = acc + col[:, c * LANES:(c + 1) * LANES]
        l_ref[...] += acc

    row = pl.BlockSpec((ROWS, dm), lambda i: (i, 0))
    return pl.pallas_call(body, name="loss_head", grid=(s // ROWS,), in_specs=[row, row],
                          out_specs=[row, pl.BlockSpec((1, LANES), lambda i: (0, 0))],
                          out_shape=[jax.ShapeDtypeStruct((s, dm), F32), jax.ShapeDtypeStruct((1, LANES), F32)],
                          compiler_params=_cparams(("arbitrary",)))(y, target)


def _shift_rows_impl(x, d):
    n = x.shape[0]
    if d == 0:
        return x
    t = lax.broadcasted_iota(jnp.int32, x.shape, 0)
    return jnp.where((t + d >= 0) & (t + d < n), pltpu.roll(x, (-d) % n, 0), 0.0)


@functools.partial(jax.custom_vjp, nondiff_argnums=(1,))
def _shift_rows(x, d):
    return _shift_rows_impl(x, d)


_shift_rows.defvjp(lambda x, d: (_shift_rows_impl(x, d), None), lambda d, _, g: (_shift_rows_impl(g, -d),))


def _gdn_pre_fn(u, w, kind):
    rows = lax.broadcasted_iota(jnp.int32, w.shape, 0)
    c = None
    for tap in range(A_CONV):
        w_tap = jnp.sum(jnp.where(rows == tap, w, 0.0), axis=0, keepdims=True)
        term = _shift_rows(u, tap - A_CONV // 2) * w_tap
        c = term if c is None else c + term
    y = c * _sigmoid(c)
    if kind == "v":
        return y
    y = y * lax.rsqrt(jnp.sum(y * y, axis=-1, keepdims=True) + L2_EPS)
    return y * (A_DK ** -0.5) if kind == "q" else y


_KIND_OFF = {"q": 0, "k": A_HEADS, "v": 2 * A_HEADS}


def gdn_pre(proj, conv_w, kind):
    s = proj.shape[0]
    off = _KIND_OFF[kind]

    def body(u_ref, w_ref, o_ref):
        o_ref[...] = _gdn_pre_fn(u_ref[...], w_ref[...], kind)

    return pl.pallas_call(
        body, name="gdn_pre_" + kind, grid=(A_HEADS,),
        in_specs=[pl.BlockSpec((s, LANES), lambda h: (0, off + h)), pl.BlockSpec((A_CONV, LANES), lambda h: (0, off + h))],
        out_specs=pl.BlockSpec((s, LANES), lambda h: (0, h)),
        out_shape=jax.ShapeDtypeStruct((s, A_W), F32), compiler_params=_cparams(("parallel",)))(proj, conv_w)


def gdn_pre_bwd(proj, conv_w, dt2, kind):
    s = proj.shape[0]
    off = _KIND_OFF[kind]

    def body(u_ref, w_ref, d0_ref, d1_ref, du_ref, dw_ref):
        _, vjp = jax.vjp(functools.partial(_gdn_pre_fn, kind=kind), u_ref[...], w_ref[...])
        du, dw = vjp(d0_ref[...] + d1_ref[...])
        du_ref[...] = du
        dw_ref[...] = dw

    return pl.pallas_call(
        body, name="gdn_pre_bwd_" + kind, grid=(A_HEADS,),
        in_specs=[pl.BlockSpec((s, LANES), lambda h: (0, off + h)), pl.BlockSpec((A_CONV, LANES), lambda h: (0, off + h)),
                  pl.BlockSpec((None, s, LANES), lambda h: (0, 0, h)), pl.BlockSpec((None, s, LANES), lambda h: (1, 0, h))],
        out_specs=[pl.BlockSpec((s, LANES), lambda h: (0, h)), pl.BlockSpec((A_CONV, LANES), lambda h: (0, h))],
        out_shape=[jax.ShapeDtypeStruct((s, A_W), F32), jax.ShapeDtypeStruct((A_CONV, A_W), F32)],
        compiler_params=_cparams(("parallel",)))(proj, conv_w, dt2, dt2)


N_GATE = 2 * A_HEADS


def _gdn_gates_fn(ba, alog_row, dt_row):
    r = lax.broadcasted_iota(jnp.int32, (LANES, N_GATE * LANES), 0)
    c = lax.broadcasted_iota(jnp.int32, (LANES, N_GATE * LANES), 1) >> 7
    beta_b = xdot(_sigmoid(ba), (r == c).astype(F32))
    g = -(jnp.exp(alog_row) * _softplus(ba + dt_row))
    g_b = xdot(g, (r == c + N_GATE).astype(F32))
    return beta_b, g_b


def gdn_gates(ba, alog_row, dt_row):
    s = ba.shape[0]

    def body(ba_ref, al_ref, dt_ref, bb_ref, gb_ref):
        bb_ref[...], gb_ref[...] = _gdn_gates_fn(ba_ref[...], al_ref[...], dt_ref[...])

    row = pl.BlockSpec((ROWS, LANES), lambda i: (i, 0))
    vec = pl.BlockSpec((1, LANES), lambda i: (0, 0))
    wide = pl.BlockSpec((ROWS, N_GATE * LANES), lambda i: (i, 0))
    return pl.pallas_call(body, name="gdn_gates", grid=(s // ROWS,), in_specs=[row, vec, vec], out_specs=[wide, wide],
                          out_shape=[jax.ShapeDtypeStruct((s, N_GATE * LANES), F32)] * 2,
                          compiler_params=_cparams(("parallel",)))(ba, alog_row, dt_row)


def gdn_gates_bwd(ba, alog_row, dt_row, dbeta_b, dg_b):
    s = ba.shape[0]

    def body(ba_ref, al_ref, dt_ref, dbb_ref, dgb_ref, dba_ref, dal_ref, ddt_ref):
        _, vjp = jax.vjp(_gdn_gates_fn, ba_ref[...], al_ref[...], dt_ref[...])
        dba, dal, ddt = vjp((dbb_ref[...], dgb_ref[...]))
        dba_ref[...] = dba

        @pl.when(pl.program_id(0) == 0)
        def _():
            dal_ref[...] = jnp.zeros_like(dal_ref)
            ddt_ref[...] = jnp.zeros_like(ddt_ref)

        dal_ref[...] += dal
        ddt_ref[...] += ddt

    row = pl.BlockSpec((ROWS, LANES), lambda i: (i, 0))
    vec = pl.BlockSpec((1, LANES), lambda i: (0, 0))
    wide = pl.BlockSpec((ROWS, N_GATE * LANES), lambda i: (i, 0))
    return pl.pallas_call(body, name="gdn_gates_bwd", grid=(s // ROWS,), in_specs=[row, vec, vec, wide, wide],
                          out_specs=[row, vec, vec],
                          out_shape=[jax.ShapeDtypeStruct((s, LANES), F32), jax.ShapeDtypeStruct((1, LANES), F32),
                                     jax.ShapeDtypeStruct((1, LANES), F32)],
                          compiler_params=_cparams(("arbitrary",)))(ba, alog_row, dt_row, dbeta_b, dg_b)


def _post_fn(o, z, g):
    y = o * lax.rsqrt(jnp.mean(o * o, axis=-1, keepdims=True) + RMS_EPS) * g
    return y * (z * _sigmoid(z))


def mixer_post(o2, proj, norm_g, width, gate_off, name):
    s = o2.shape[1]
    nh = o2.shape[2] // width

    rows = _tile(s, POST_ROWS)

    def body(o0_ref, o1_ref, z_ref, g_ref, y_ref):
        y_ref[...] = _post_fn(o0_ref[...] + o1_ref[...], z_ref[...], g_ref[...])

    ospec = lambda d: pl.BlockSpec((None, rows, width), lambda i, h: (d, i, h))
    return pl.pallas_call(
        body, name=name, grid=(s // rows, nh),
        in_specs=[ospec(0), ospec(1), pl.BlockSpec((rows, width), lambda i, h: (i, gate_off + h)),
                  pl.BlockSpec((1, width), lambda i, h: (0, 0))],
        out_specs=pl.BlockSpec((rows, width), lambda i, h: (i, h)),
        out_shape=jax.ShapeDtypeStruct((s, o2.shape[2]), F32),
        compiler_params=_cparams(("parallel", "parallel")))(o2, o2, proj, norm_g)


def mixer_post_bwd(o2, proj, norm_g, dy, width, gate_off, name):
    s = o2.shape[1]
    nh = o2.shape[2] // width

    def body(o0_ref, o1_ref, z_ref, g_ref, dy_ref, do_ref, dz_ref, dg_ref):
        _, vjp = jax.vjp(_post_fn, o0_ref[...] + o1_ref[...], z_ref[...], g_ref[...])
        do, dz, dg = vjp(dy_ref[...])
        do_ref[...] = do
        dz_ref[...] = dz

        @pl.when((pl.program_id(0) == 0) & (pl.program_id(1) == 0))
        def _():
            dg_ref[...] = jnp.zeros_like(dg_ref)

        dg_ref[...] += dg

    rows = _tile(s, POST_ROWS)
    ospec = lambda d: pl.BlockSpec((None, rows, width), lambda i, h: (d, i, h))
    blk = pl.BlockSpec((rows, width), lambda i, h: (i, h))
    vec = pl.BlockSpec((1, width), lambda i, h: (0, 0))
    return pl.pallas_call(
        body, name=name, grid=(s // rows, nh),
        in_specs=[ospec(0), ospec(1), pl.BlockSpec((rows, width), lambda i, h: (i, gate_off + h)), vec, blk],
        out_specs=[blk, blk, vec],
        out_shape=[jax.ShapeDtypeStruct((s, o2.shape[2]), F32)] * 2 + [jax.ShapeDtypeStruct((1, width), F32)],
        compiler_params=_cparams(("arbitrary", "arbitrary")))(o2, o2, proj, norm_g, dy)


def _log_gate(z):
    return (jnp.minimum(z, 0.0) - jnp.log(1.0 + jnp.exp(-jnp.abs(z)))) * (1.0 / B_TAU)


def gla_gate(gl, w2, gb):
    s = gl.shape[0]

    def body(gl_ref, w_ref, b_ref, o_ref):
        for n in range(2):
            o_ref[n] = _log_gate(nn(gl_ref[...], w_ref[n]) + b_ref[n])

    full = lambda shp: pl.BlockSpec(shp, lambda i: (0,) * len(shp))
    return pl.pallas_call(
        body, name="gla_gate", grid=(s // ROWS,),
        in_specs=[pl.BlockSpec((ROWS, LANES), lambda i: (i, 0)), full(w2.shape), full(gb.shape)],
        out_specs=pl.BlockSpec((2, ROWS, B_KW), lambda i: (0, i, 0)),
        out_shape=jax.ShapeDtypeStruct((2, s, B_KW), F32), compiler_params=_cparams(("parallel",)))(gl, w2, gb)


def gla_gate_bwd(gl, w2, gb, dla):
    s = gl.shape[0]

    def body(gl_ref, w_ref, b_ref, dla_ref, dgl_ref, dz_ref, db0_ref, db1_ref):
        @pl.when(pl.program_id(0) == 0)
        def _():
            db0_ref[...] = jnp.zeros_like(db0_ref)
            db1_ref[...] = jnp.zeros_like(db1_ref)

        dgl = None
        for n, db_ref in enumerate((db0_ref, db1_ref)):
            _, vjp = jax.vjp(_log_gate, nn(gl_ref[...], w_ref[n]) + b_ref[n])
            dz, = vjp(dla_ref[n])
            dz_ref[n] = dz
            db_ref[...] += jnp.sum(dz, axis=0, keepdims=True)
            part = nt(dz, w_ref[n])
            dgl = part if dgl is None else dgl + part
        dgl_ref[...] = dgl

    full = lambda shp: pl.BlockSpec(shp, lambda i: (0,) * len(shp))
    row = pl.BlockSpec((ROWS, LANES), lambda i: (i, 0))
    wide = pl.BlockSpec((2, ROWS, B_KW), lambda i: (0, i, 0))
    vec = pl.BlockSpec((1, B_KW), lambda i: (0, 0))
    return pl.pallas_call(
        body, name="gla_gate_bwd", grid=(s // ROWS,),
        in_specs=[row, full(w2.shape), full(gb.shape), wide],
        out_specs=[row, wide, vec, vec],
        out_shape=[jax.ShapeDtypeStruct((s, LANES), F32), jax.ShapeDtypeStruct((2, s, B_KW), F32),
                   jax.ShapeDtypeStruct((1, B_KW), F32), jax.ShapeDtypeStruct((1, B_KW), F32)],
        compiler_params=_cparams(("arbitrary",)))(gl, w2, gb, dla)


PACK_TILE = 512


def cast_into_slot(x, chip):
    r, c = x.shape

    def body(chip_ref, x_ref, o_ref):
        o_ref[...] = x_ref[...].astype(BF16)

    return pl.pallas_call(
        body, name="cast_into_slot",
        grid_spec=pltpu.PrefetchScalarGridSpec(
            num_scalar_prefetch=1, grid=(r // PACK_TILE,),
            in_specs=[pl.BlockSpec((PACK_TILE, c), lambda i, chip_ref: (i, 0))],
            out_specs=pl.BlockSpec((None, PACK_TILE, c), lambda i, chip_ref: (chip_ref[0], i, 0))),
        out_shape=jax.ShapeDtypeStruct((4, r, c), BF16), compiler_params=_cparams(("parallel",)))(chip, x)


def sum_received(chip_sum, recv, chip, core):
    _, h, c = chip_sum.shape
    tr = _tile(h, PACK_TILE)
    nblk = h // tr

    def body(chip_ref, core_ref, own_ref, r_ref, o_ref):
        acc = r_ref[0].astype(F32)
        for k in range(1, 3):
            acc = acc + r_ref[k].astype(F32)
        o_ref[...] = acc + own_ref[...].astype(F32)

    return pl.pallas_call(
        body, name="sum_received",
        grid_spec=pltpu.PrefetchScalarGridSpec(
            num_scalar_prefetch=2, grid=(nblk,),
            in_specs=[pl.BlockSpec((None, tr, c), lambda i, chip_ref, core_ref: (chip_ref[0], i, 0)),
                      pl.BlockSpec((3, tr, c), lambda i, chip_ref, core_ref: (0, i, 0))],
            out_specs=pl.BlockSpec((tr, c), lambda i, chip_ref, core_ref: (core_ref[0] * nblk + i, 0))),
        out_shape=jax.ShapeDtypeStruct((2 * h, c), F32), compiler_params=_cparams(("parallel",)))(chip, core, chip_sum, recv)


def sum_slots(x, name):
    n, r, c = x.shape
    tr = _tile(r, PACK_TILE)

    def body(x_ref, o_ref):
        acc = x_ref[0].astype(F32)
        for k in range(1, n):
            acc = acc + x_ref[k].astype(F32)
        o_ref[...] = acc

    return pl.pallas_call(body, name=name, grid=(r // tr,), in_specs=[pl.BlockSpec((n, tr, c), lambda i: (0, i, 0))],
                          out_specs=pl.BlockSpec((tr, c), lambda i: (i, 0)),
                          out_shape=jax.ShapeDtypeStruct((r, c), F32), compiler_params=_cparams(("parallel",)))(x)


def add_sibling_half(gpack, theirs, core):
    n, r, c = gpack.shape
    half_rows = r // 2
    tr = _tile(half_rows, PACK_TILE)
    nblk = half_rows // tr

    def body(core_ref, g_ref, t_ref, o_ref):
        o_ref[...] = (g_ref[...] + t_ref[...]).astype(BF16)

    blk = pl.BlockSpec((None, tr, c), lambda s, i, core_ref: (s, i, 0))
    return pl.pallas_call(
        body, name="add_sibling_half",
        grid_spec=pltpu.PrefetchScalarGridSpec(
            num_scalar_prefetch=1, grid=(n, nblk),
            in_specs=[pl.BlockSpec((None, tr, c), lambda s, i, core_ref: (s, core_ref[0] * nblk + i, 0)), blk],
            out_specs=blk),
        out_shape=jax.ShapeDtypeStruct((n, half_rows, c), BF16),
        compiler_params=_cparams(("parallel", "parallel")))(core, gpack, theirs)


def adamw(w, m, v, grads, g_row_off, name):
    r, c = w.shape
    tr = next(t for t in (PACK_TILE, r) if r % t == 0 and g_row_off % t == 0)
    ob = g_row_off // tr
    ng = len(grads)

    def body(*refs):
        w_ref, m_ref, v_ref = refs[:3]
        g_refs = refs[3:3 + ng]
        g_ref, d_ref, nm_ref, nv_ref = refs[3 + ng:]
        g = g_refs[0][...]
        for gr in g_refs[1:]:
            g = g + gr[...]
        m_new = ADAM_B1 * m_ref[...] + (1.0 - ADAM_B1) * g
        v_new = ADAM_B2 * v_ref[...] + (1.0 - ADAM_B2) * jnp.square(g)
        m_hat = m_new / (1.0 - ADAM_B1 ** ADAM_STEP)
        v_hat = v_new / (1.0 - ADAM_B2 ** ADAM_STEP)
        g_ref[...] = g
        d_ref[...] = -ADAM_LR * (m_hat / (jnp.sqrt(v_hat) + ADAM_EPS) + ADAM_WD * w_ref[...])
        nm_ref[...] = m_new
        nv_ref[...] = v_new

    blk = pl.BlockSpec((tr, c), lambda i: (i, 0))
    gblk = pl.BlockSpec((tr, c), lambda i: (i + ob, 0))
    return pl.pallas_call(body, name=name, grid=(r // tr,), in_specs=[blk, blk, blk] + [gblk] * ng, out_specs=[blk] * 4,
                          out_shape=[jax.ShapeDtypeStruct((r, c), F32)] * 4,
                          compiler_params=_cparams(("parallel",)))(w, m, v, *grads)


MESH = pl.DeviceIdType.MESH
HBM = pl.BlockSpec(memory_space=pl.ANY)
CHIP_FLIPS = ((1, 0), (0, 1), (1, 1))


def _place():
    return lax.axis_index("x"), lax.axis_index("y"), lax.axis_index("c")


def allgather_chips(buf):
    _, r, c = buf.shape
    half_rows = r // 2

    def body(_, out_ref, send_sems, recv_sems):
        x, y, cc = _place()
        half = pl.ds(cc * half_rows, half_rows)
        other = pl.ds((1 - cc) * half_rows, half_rows)

        def copy(k, rows, to):
            return pltpu.make_async_remote_copy(src_ref=rows, dst_ref=rows, send_sem=send_sems.at[k],
                                                recv_sem=recv_sems.at[k], device_id=to, device_id_type=MESH)

        nbr_x, nbr_y, diag = (1 - x, y), (x, 1 - y), (1 - x, 1 - y)
        slot = lambda chip: 2 * chip[0] + chip[1]
        sibling = (x, y, 1 - cc)
        first = [copy(0, out_ref.at[slot((x, y)), half], (*nbr_x, cc)), copy(1, out_ref.at[slot((x, y)), half], (*nbr_y, cc))]
        for cp in first:
            cp.start()
        passed = []
        for k, chip in enumerate((nbr_x, nbr_y)):
            landed = out_ref.at[slot(chip), half]
            copy(k, landed, (*chip, cc)).wait_recv()
            passed.append(copy(3 + k, landed, sibling))
            passed[-1].start()
        via = (1 - x + cc * (2 * x - 1), y + cc * (1 - 2 * y))
        to = (x + cc * (1 - 2 * x), 1 - y + cc * (2 * y - 1))
        hop = copy(2, out_ref.at[slot(via), half], (*to, cc))
        hop.start()
        landed = out_ref.at[slot(diag), half]
        copy(2, landed, (*to, cc)).wait_recv()
        passed.append(copy(5, landed, sibling))
        passed[-1].start()
        for k, chip in enumerate((nbr_x, nbr_y, diag)):
            copy(3 + k, out_ref.at[slot(chip), other], sibling).wait_recv()
        for cp in first + [hop] + passed:
            cp.wait_send()

    return pl.pallas_call(
        body, name="allgather_chips", in_specs=[HBM], out_specs=HBM, input_output_aliases={0: 0},
        out_shape=jax.ShapeDtypeStruct(buf.shape, buf.dtype),
        scratch_shapes=[pltpu.SemaphoreType.DMA((6,)), pltpu.SemaphoreType.DMA((6,))],
    )(buf)


def scatter_chips(gpack):
    _, r, c = gpack.shape

    def body(src_ref, out_ref, send_sems, recv_sems):
        x, y, cc = _place()
        sends = []
        for k, (fx, fy) in enumerate(CHIP_FLIPS):
            px, py = (1 - x if fx else x), (1 - y if fy else y)
            sends.append(pltpu.make_async_remote_copy(
                src_ref=src_ref.at[2 * px + py], dst_ref=out_ref.at[k], send_sem=send_sems.at[k], recv_sem=recv_sems.at[k],
                device_id=(px, py, cc), device_id_type=MESH))
        for cp in sends:
            cp.start()
        for cp in sends:
            cp.wait_recv()
        for cp in sends:
            cp.wait_send()

    return pl.pallas_call(
        body, name="scatter_chips", in_specs=[HBM], out_specs=HBM,
        out_shape=jax.ShapeDtypeStruct((3, r, c), gpack.dtype),
        scratch_shapes=[pltpu.SemaphoreType.DMA((3,)), pltpu.SemaphoreType.DMA((3,))],
    )(gpack)


def sibling_halves(gpack):
    n, r, c = gpack.shape
    half_rows = r // 2

    def body(src_ref, out_ref, send_sem, recv_sem):
        x, y, cc = _place()
        cp = pltpu.make_async_remote_copy(
            src_ref=src_ref.at[:, pl.ds((1 - cc) * half_rows, half_rows)], dst_ref=out_ref, send_sem=send_sem,
            recv_sem=recv_sem, device_id=(x, y, 1 - cc), device_id_type=MESH)
        cp.start()
        cp.wait()

    return pl.pallas_call(
        body, name="sibling_halves", in_specs=[HBM], out_specs=HBM,
        out_shape=jax.ShapeDtypeStruct((n, half_rows, c), gpack.dtype),
        scratch_shapes=[pltpu.SemaphoreType.DMA, pltpu.SemaphoreType.DMA],
    )(gpack)


def join_halves(buf):
    r, c = buf.shape
    half_rows = r // 2

    def body(_, out_ref, send_sem, recv_sem):
        x, y, cc = _place()
        half = out_ref.at[pl.ds(cc * half_rows, half_rows)]
        other = out_ref.at[pl.ds((1 - cc) * half_rows, half_rows)]
        send = pltpu.make_async_remote_copy(src_ref=half, dst_ref=half, send_sem=send_sem, recv_sem=recv_sem,
                                            device_id=(x, y, 1 - cc), device_id_type=MESH)
        send.start()
        pltpu.make_async_remote_copy(src_ref=other, dst_ref=other, send_sem=send_sem, recv_sem=recv_sem,
                                     device_id=(x, y, 1 - cc), device_id_type=MESH).wait_recv()
        send.wait_send()

    return pl.pallas_call(
        body, name="join_halves", in_specs=[HBM], out_specs=HBM, input_output_aliases={0: 0},
        out_shape=jax.ShapeDtypeStruct(buf.shape, buf.dtype),
        scratch_shapes=[pltpu.SemaphoreType.DMA, pltpu.SemaphoreType.DMA],
    )(buf)


def exchange_all(v, name):
    r, c = v.shape

    def body(v_ref, out_ref, send_sems, recv_sems):
        x, y, cc = _place()
        out_ref[4 * x + 2 * y + cc] = v_ref[...]
        sends, recvs = [], []
        for k in range(1, 8):
            px = 1 - x if k & 4 else x
            py = 1 - y if k & 2 else y
            pc = 1 - cc if k & 1 else cc
            sends.append(pltpu.make_async_remote_copy(
                src_ref=v_ref, dst_ref=out_ref.at[4 * x + 2 * y + cc], send_sem=send_sems.at[k - 1],
                recv_sem=recv_sems.at[k - 1], device_id=(px, py, pc), device_id_type=MESH))
            recvs.append(pltpu.make_async_remote_copy(
                src_ref=v_ref, dst_ref=out_ref.at[4 * px + 2 * py + pc], send_sem=send_sems.at[k - 1],
                recv_sem=recv_sems.at[k - 1], device_id=(px, py, pc), device_id_type=MESH))
        for cp in sends:
            cp.start()
        for cp in recvs:
            cp.wait_recv()
        for cp in sends:
            cp.wait_send()

    vm = pl.BlockSpec(memory_space=pltpu.VMEM)
    return pl.pallas_call(
        body, name=name, in_specs=[vm], out_specs=vm, out_shape=jax.ShapeDtypeStruct((8, r, c), v.dtype),
        scratch_shapes=[pltpu.SemaphoreType.DMA((7,)), pltpu.SemaphoreType.DMA((7,))],
        compiler_params=pltpu.CompilerParams(vmem_limit_bytes=VMEM_LIMIT),
    )(v)


def _as_rows(a, width):
    n = math.prod(a.shape)
    if n % width == 0:
        return a.reshape(-1, width)
    return jnp.pad(a.reshape(1, -1), ((0, 0), (0, -n % width))).reshape(-1, width)


def _n_rows(shape, width):
    return -(-math.prod(shape) // width)


def _pack_rows(arrays, rows, width):
    parts = [_as_rows(a, width) for a in arrays]
    used = sum(p.shape[0] for p in parts)
    return jnp.concatenate(parts + [jnp.zeros((rows - used, width), arrays[0].dtype)], axis=0)


def _unpack_rows(pack, shapes):
    width = pack.shape[1]
    out, off = [], 0
    for shp in shapes:
        nr, n = _n_rows(shp, width), math.prod(shp)
        part = pack[off:off + nr]
        out.append(part.reshape(shp) if n % width == 0 else part.reshape(-1)[:n].reshape(shp))
        off += nr
    return out


def _rows_for(shapes, width, mult=8):
    n = sum(_n_rows(s, width) for s in shapes)
    return -(-n // mult) * mult


def _gdn_fwd(x, p):
    proj = mm(x, p["w_main"], name="gdn_proj")
    ba = mm(x, p["w_gate"], name="gdn_proj_gate")
    q, k, v = (gdn_pre(proj, p["conv"], kind) for kind in "qkv")
    beta_b, g_b = gdn_gates(ba, p["alog_row"], p["dt_row"])
    o2, st, tinv = gdn_rec_fwd(q, k, v, beta_b, g_b)
    y = mixer_post(o2, proj, p["norm_g"], A_DK, 3 * A_HEADS, "gdn_post")
    m = mm(y, p["w_out"], name="gdn_out")
    return m, (x, proj, ba, q, k, v, beta_b, g_b, o2, st, tinv, y)


def _gdn_bwd(saved, p, dm):
    x, proj, ba, q, k, v, beta_b, g_b, o2, st, tinv, y = saved
    d_w_out = mm(y, dm, "tn", name="gdn_dw_out")
    dy = mm(dm, p["w_out"], "nt", name="gdn_dy")
    do, dz, d_norm_g = mixer_post_bwd(o2, proj, p["norm_g"], dy, A_DK, 3 * A_HEADS, "gdn_post_bwd")
    dq2, dk2, dv2, dbb, dgb = gdn_rec_bwd(q, k, v, beta_b, g_b, st, tinv, do)
    dba, d_alog_row, d_dt_row = gdn_gates_bwd(ba, p["alog_row"], p["dt_row"], dbb, dgb)
    du, dconv = zip(*(gdn_pre_bwd(proj, p["conv"], d2, kind) for d2, kind in ((dq2, "q"), (dk2, "k"), (dv2, "v"))))
    dproj = jnp.concatenate(list(du) + [dz], axis=1)
    d_w_main = mm(x, dproj, "tn", name="gdn_dw_main")
    d_w_gate = mm(x, dba, "tn", name="gdn_dw_gate")
    dx = mm(dba, p["w_gate"], "nt", epi="add", extra=dm, alpha=ALPHA, name="gdn_dx_gate")
    dx = mm(dproj, p["w_main"], "nt", epi="add", extra=dx, name="gdn_dx")
    grads = dict(w_in=jnp.concatenate([d_w_main, d_w_gate[:, :2 * N_GATE]], axis=1), conv=jnp.concatenate(dconv, axis=1),
                 alog=d_alog_row[0, N_GATE:2 * N_GATE].reshape(2, A_HEADS), dt=d_dt_row[0, N_GATE:2 * N_GATE].reshape(2, A_HEADS),
                 norm_g=d_norm_g[0], w_out=d_w_out)
    return dx, grads


def _gla_fwd(x, p):
    proj = mm(x, p["w_main"], name="gla_proj")
    gl = mm(x, p["w_gate"], name="gla_proj_gate")
    log_a = gla_gate(gl, p["w2"], p["gate_b"])
    o2, st = gla_rec_fwd(proj, log_a)
    y = mixer_post(o2, proj, p["norm_g"], B_DV, (2 * B_KW + B_VW) // B_DV, "gla_post")
    m = mm(y, p["w_out"], name="gla_out")
    return m, (x, proj, gl, log_a, o2, st, y)


def _gla_bwd(saved, p, dm):
    x, proj, gl, log_a, o2, st, y = saved
    d_w_out = mm(y, dm, "tn", name="gla_dw_out")
    dy = mm(dm, p["w_out"], "nt", name="gla_dy")
    do, dr, d_norm_g = mixer_post_bwd(o2, proj, p["norm_g"], dy, B_DV, (2 * B_KW + B_VW) // B_DV, "gla_post_bwd")
    dq2, dk2, dv2, dla = gla_rec_bwd(proj, log_a, st, do)
    dgl, dz, d_b0, d_b1 = gla_gate_bwd(gl, p["w2"], p["gate_b"], dla)
    d_w2 = [mm(gl, dz[n], "tn", name="gla_dw_gate_w2") for n in range(2)]
    dproj = jnp.concatenate([dq2[0] + dq2[1], dk2[0] + dk2[1], dv2[0] + dv2[1], dr], axis=1)
    d_w_main = mm(x, dproj, "tn", name="gla_dw_main")
    d_w_gate = mm(x, dgl, "tn", name="gla_dw_gate")
    dx = mm(dgl, p["w_gate"], "nt", epi="add", extra=dm, alpha=ALPHA, name="gla_dx_gate")
    dx = mm(dproj, p["w_main"], "nt", epi="add", extra=dx, name="gla_dx")
    grads = dict(w_in=jnp.concatenate([d_w_main, d_w_gate[:, :2 * B_RANK]], axis=1),
                 gate_w2=jnp.stack([d_w2[n][n * B_RANK:(n + 1) * B_RANK] for n in range(2)]),
                 gate_b=jnp.concatenate([d_b0, d_b1]), norm_g=d_norm_g[0], w_out=d_w_out)
    return dx, grads


def _pad_cols(w, width=LANES):
    return jnp.pad(w, ((0, 0), (0, width - w.shape[1])))


def _local_step(x, target, a_w_in, a_conv, a_alog, a_dt_bias, a_norm_g, a_w_out, b_w_in, b_gate_w2, b_gate_b, b_norm_g,
                b_w_out, ln1_g, ln1_b, mlp_w1, mlp_w2, ln2_g, ln2_b):
    layer_p = []
    for i in range(DEPTH):
        j = i // 2
        if i % 2 == 0:
            layer_p.append(dict(
                w_main=a_w_in[j][:, :4 * A_W], w_gate=_pad_cols(a_w_in[j][:, 4 * A_W:]), conv=a_conv[j],
                alog_row=jnp.pad(a_alog[j].reshape(1, N_GATE), ((0, 0), (N_GATE, LANES - 2 * N_GATE))),
                dt_row=jnp.pad(a_dt_bias[j].reshape(1, N_GATE), ((0, 0), (N_GATE, LANES - 2 * N_GATE))),
                norm_g=a_norm_g[j].reshape(1, A_DK), w_out=a_w_out[j]))
        else:
            w2 = jnp.stack([jnp.pad(b_gate_w2[j][n], ((n * B_RANK, LANES - (n + 1) * B_RANK), (0, 0))) for n in range(2)])
            layer_p.append(dict(
                w_main=b_w_in[j][:, :2 * B_KW + 2 * B_VW], w_gate=_pad_cols(b_w_in[j][:, 2 * B_KW + 2 * B_VW:]),
                w2=w2, gate_b=b_gate_b[j].reshape(2, 1, B_KW), norm_g=b_norm_g[j].reshape(1, B_DV), w_out=b_w_out[j]))

    saved = []
    h = x
    for i in range(DEPTH):
        p = layer_p[i]
        m, sv = (_gdn_fwd if i % 2 == 0 else _gla_fwd)(h, p)
        x1 = ln_fwd(h, m, ln1_g[i:i + 1], ln1_b[i:i + 1])
        h1 = mm(x1, mlp_w1[i], name="mlp_up")
        mlp = mm(h1, mlp_w2[i], act="sqrelu", name="mlp_down")
        x2 = ln_fwd(x1, mlp, ln2_g[i:i + 1], ln2_b[i:i + 1])
        saved.append((sv, h, m, x1, h1, mlp))
        h = x2

    dh, loss_part = loss_head(h, target)

    g_a, g_b, g_ln1g, g_ln1b, g_ln2g, g_ln2b, g_w1, g_w2 = {}, {}, {}, {}, {}, {}, {}, {}
    for i in reversed(range(DEPTH)):
        sv, xin, m, x1, h1, mlp = saved[i]
        p = layer_p[i]
        dr2, g_ln2g[i], g_ln2b[i] = ln_bwd(x1, mlp, ln2_g[i:i + 1], dh)
        g_w2[i] = mm(h1, dr2, "tn", act="sqrelu", name="mlp_dw_down")
        dh1 = mm(dr2, mlp_w2[i], "nt", epi="dsqrelu", extra=h1, name="mlp_dh")
        g_w1[i] = mm(x1, dh1, "tn", chip_major=True, name="mlp_dw_up")
        dx1 = mm(dh1, mlp_w1[i], "nt", epi="add", extra=dr2, alpha=ALPHA, name="mlp_dx")
        dr1, g_ln1g[i], g_ln1b[i] = ln_bwd(xin, m, ln1_g[i:i + 1], dx1)
        dh, g = (_gdn_bwd if i % 2 == 0 else _gla_bwd)(sv, p, dr1)
        (g_a if i % 2 == 0 else g_b)[i // 2] = g

    per_layer = lambda d, key=None: [(d[i] if key is None else d[i][key]) for i in sorted(d)]
    st = lambda d, key=None: jnp.stack(per_layer(d, key))
    grads = dict(
        a_w_in=per_layer(g_a, "w_in"), a_conv=st(g_a, "conv"), a_alog=st(g_a, "alog"), a_dt_bias=st(g_a, "dt"),
        a_norm_g=st(g_a, "norm_g"), a_w_out=per_layer(g_a, "w_out"), b_w_in=per_layer(g_b, "w_in"),
        b_gate_w2=st(g_b, "gate_w2"), b_gate_b=st(g_b, "gate_b"), b_norm_g=st(g_b, "norm_g"),
        b_w_out=per_layer(g_b, "w_out"), ln1_g=st(g_ln1g)[:, 0], ln1_b=st(g_ln1b)[:, 0], mlp_w1=per_layer(g_w1),
        mlp_w2=per_layer(g_w2), ln2_g=st(g_ln2g)[:, 0], ln2_b=st(g_ln2b)[:, 0])
    return loss_part, dh, grads


WEIGHTS = ("a_w_in", "a_conv", "a_alog", "a_dt_bias", "a_norm_g", "a_w_out", "b_w_in", "b_gate_w2", "b_gate_b",
           "b_norm_g", "b_w_out", "ln1_g", "ln1_b", "mlp_w1", "mlp_w2", "ln2_g", "ln2_b")
BIG = ("mlp_w1", "mlp_w2", "a_w_out", "b_w_out", "a_w_in", "b_w_in")
SHARD_AXIS = {"mlp_w1": 2, "mlp_w2": 1, "a_w_out": 1, "b_w_out": 1, "a_w_in": 2, "b_w_in": 2}
SMALL = tuple(n for n in WEIGHTS if n not in BIG)
SMALL_SHARD_AXIS = {"a_conv": 2, "b_gate_w2": 3, "b_gate_b": 2, "b_norm_g": 1}


def _to_chip_major(full, axis):
    shp = full.shape
    t = full.reshape(shp[:axis] + (4, shp[axis] // 4) + shp[axis + 1:])
    return jnp.moveaxis(t, axis, 0)


def _from_chip_major(stacked, axis):
    t = jnp.moveaxis(stacked, 0, axis)
    shp = t.shape
    return t.reshape(shp[:axis] + (shp[axis] * shp[axis + 1],) + shp[axis + 2:])


def kernel(x, a_w_in, a_conv, a_alog, a_dt_bias, a_norm_g, a_w_out, b_w_in, b_gate_w2, b_gate_b, b_norm_g, b_w_out, ln1_g, ln1_b, mlp_w1, mlp_w2, ln2_g, ln2_b, loss_target, m_a_w_in, m_a_conv, m_a_alog, m_a_dt_bias, m_a_norm_g, m_a_w_out, m_b_w_in, m_b_gate_w2, m_b_gate_b, m_b_norm_g, m_b_w_out, m_ln1_g, m_ln1_b, m_mlp_w1, m_mlp_w2, m_ln2_g, m_ln2_b, v_a_w_in, v_a_conv, v_a_alog, v_a_dt_bias, v_a_norm_g, v_a_w_out, v_b_w_in, v_b_gate_w2, v_b_gate_b, v_b_norm_g, v_b_w_out, v_ln1_g, v_ln1_b, v_mlp_w1, v_mlp_w2, v_ln2_g, v_ln2_b):
    w = dict(a_w_in=a_w_in, a_conv=a_conv, a_alog=a_alog, a_dt_bias=a_dt_bias, a_norm_g=a_norm_g, a_w_out=a_w_out,
             b_w_in=b_w_in, b_gate_w2=b_gate_w2, b_gate_b=b_gate_b, b_norm_g=b_norm_g, b_w_out=b_w_out, ln1_g=ln1_g,
             ln1_b=ln1_b, mlp_w1=mlp_w1, mlp_w2=mlp_w2, ln2_g=ln2_g, ln2_b=ln2_b)
    mom = dict(a_w_in=m_a_w_in, a_conv=m_a_conv, a_alog=m_a_alog, a_dt_bias=m_a_dt_bias, a_norm_g=m_a_norm_g,
               a_w_out=m_a_w_out, b_w_in=m_b_w_in, b_gate_w2=m_b_gate_w2, b_gate_b=m_b_gate_b, b_norm_g=m_b_norm_g,
               b_w_out=m_b_w_out, ln1_g=m_ln1_g, ln1_b=m_ln1_b, mlp_w1=m_mlp_w1, mlp_w2=m_mlp_w2, ln2_g=m_ln2_g,
               ln2_b=m_ln2_b)
    var = dict(a_w_in=v_a_w_in, a_conv=v_a_conv, a_alog=v_a_alog, a_dt_bias=v_a_dt_bias, a_norm_g=v_a_norm_g,
               a_w_out=v_a_w_out, b_w_in=v_b_w_in, b_gate_w2=v_b_gate_w2, b_gate_b=v_b_gate_b, b_norm_g=v_b_norm_g,
               b_w_out=v_b_w_out, ln1_g=v_ln1_g, ln1_b=v_ln1_b, mlp_w1=v_mlp_w1, mlp_w2=v_mlp_w2, ln2_g=v_ln2_g,
               ln2_b=v_ln2_b)
    chip = 2 * lax.axis_index("x") + lax.axis_index("y")

    seg_rows = [w[n].size // D_MODEL for n in BIG]
    seg_off = [sum(seg_rows[:i]) for i in range(len(BIG))]
    rows = -(-sum(seg_rows) // PACK_TILE) * PACK_TILE
    shard_pack = jnp.concatenate([w[n].reshape(-1, D_MODEL) for n in BIG]
                                 + [jnp.zeros((rows - sum(seg_rows), D_MODEL), F32)], axis=0)
    chip_idx = chip.astype(jnp.int32).reshape(1)
    gathered = allgather_chips(cast_into_slot(shard_pack, chip_idx))
    full = {}
    for n, off, nr in zip(BIG, seg_off, seg_rows):
        if n in ("mlp_w1", "mlp_w2"):
            kind = "cols" if SHARD_AXIS[n] == 2 else "rows"
            full[n] = [Gathered(gathered, off + i * D_MODEL, kind) for i in range(DEPTH)]
            continue
        stacked = gathered[:, off:off + nr].reshape((4,) + w[n].shape)
        full[n] = _from_chip_major(stacked, SHARD_AXIS[n])
    sharded_small = tuple(SMALL_SHARD_AXIS)
    sm_shapes = [w[n].shape for n in sharded_small]
    sm_rows = _rows_for(sm_shapes, LANES)
    sm_all = exchange_all(_pack_rows([w[n] for n in sharded_small], sm_rows, LANES), "gather_small")
    per_chip = [_unpack_rows(sm_all[2 * pch], sm_shapes) for pch in range(4)]
    for idx, n in enumerate(sharded_small):
        full[n] = jnp.concatenate([per_chip[pch][idx] for pch in range(4)], axis=SMALL_SHARD_AXIS[n])
    for n in WEIGHTS:
        full.setdefault(n, w[n])

    loss_part, grad_x, grads = _local_step(x[0], loss_target[0], *[full[n] for n in WEIGHTS])
    loss = lax.psum(jnp.sum(loss_part), ("x", "y", "c"))

    gpack = jnp.concatenate(
        [(g if n == "mlp_w1" else _to_chip_major(g, SHARD_AXIS[n] - 1).reshape(4, -1, D_MODEL))
         for n in BIG for g in grads[n]]
        + [jnp.zeros((4, rows - sum(seg_rows), D_MODEL), F32)], axis=1)
    core = lax.axis_index("c").astype(jnp.int32).reshape(1)
    chip_sum = add_sibling_half(gpack, sibling_halves(gpack), core)
    reduced = join_halves(sum_received(chip_sum, scatter_chips(chip_sum), chip_idx, core))
    out_g, out_d, out_m, out_v = {}, {}, {}, {}
    for n, off, nr in zip(BIG, seg_off, seg_rows):
        if w[n].shape[-1] == D_MODEL:
            view = lambda t: t.reshape(-1, D_MODEL)
            res = adamw(view(w[n]), view(mom[n]), view(var[n]), (reduced,), off, "adamw_" + n)
        else:
            cols = w[n].shape[-1]
            view = lambda t: t.reshape(-1, cols)
            res = adamw(view(w[n]), view(mom[n]), view(var[n]), (view(reduced[off:off + nr]),), 0, "adamw_" + n)
        out_g[n], out_d[n], out_m[n], out_v[n] = (t.reshape(w[n].shape) for t in res)

    all_shapes = [full[n].shape for n in SMALL]
    g_rows = _rows_for(all_shapes, LANES)
    g_all = exchange_all(_pack_rows([grads[n] for n in SMALL], g_rows, LANES), "gather_small_grads")
    g_sum = _unpack_rows(sum_slots(g_all, "sum_small_grads"), all_shapes)
    g_mine = []
    for n, g in zip(SMALL, g_sum):
        if n in SMALL_SHARD_AXIS:
            ax = SMALL_SHARD_AXIS[n]
            g = lax.dynamic_slice_in_dim(g, chip * w[n].shape[ax], w[n].shape[ax], axis=ax)
        g_mine.append(g)
    my_shapes = [w[n].shape for n in SMALL]
    s_rows = _rows_for(my_shapes, LANES)
    pk = lambda d: _pack_rows([d[n] for n in SMALL], s_rows, LANES)
    res = adamw(pk(w), pk(mom), pk(var), (_pack_rows(g_mine, s_rows, LANES),), 0, "adamw_small")
    for dst, pack in zip((out_g, out_d, out_m, out_v), res):
        for n, t in zip(SMALL, _unpack_rows(pack, my_shapes)):
            dst[n] = t

    return (loss, grad_x[None], *[out_g[n] for n in WEIGHTS], *[out_d[n] for n in WEIGHTS],
            *[out_m[n] for n in WEIGHTS], *[out_v[n] for n in WEIGHTS])
```

```python
import functools
import math

import jax
import jax.numpy as jnp
from jax import lax
from jax.experimental import pallas as pl
from jax.experimental.pallas import tpu as pltpu

F32 = jnp.float32
BF16 = jnp.bfloat16

D_MODEL = 1024
DEPTH = 4
CHUNK = 64
A_HEADS = 8
A_DK = 128
A_W = 1024
A_CONV = 5
B_HEADS = 4
B_DK = 128
B_DV = 256
B_RANK = 16
B_TAU = 16.0
B_KW = 512
B_VW = 1024
ALPHA = (2 * DEPTH) ** 0.25
LN_EPS = 1e-5
RMS_EPS = 1e-6
L2_EPS = 1e-6
ADAM_LR = 0.001
ADAM_B1 = 0.9
ADAM_B2 = 0.999
ADAM_EPS = 1e-08
ADAM_WD = 0.01
ADAM_STEP = 10
LANES = 128
NEG_INF = float("-inf")
VMEM_LIMIT = 56 * 1024 * 1024


def _cparams(sem=None):
    return pltpu.CompilerParams(dimension_semantics=sem, vmem_limit_bytes=VMEM_LIMIT)


def _dg(a, b, ca, cb):
    return lax.dot_general(a.astype(BF16), b.astype(BF16), (((ca,), (cb,)), ((), ())),
                           preferred_element_type=F32)


def _split(x):
    hi = x.astype(BF16)
    return hi, (x - hi.astype(F32)).astype(BF16)


def _dg3(a, b, ca, cb):
    (a1, a2), (b1, b2) = _split(a), _split(b)
    return (_dg(a1, b2, ca, cb) + _dg(a2, b1, ca, cb)) + _dg(a1, b1, ca, cb)


def _dot_with_vjp(dg):
    @functools.partial(jax.custom_vjp, nondiff_argnums=(2, 3))
    def dot(a, b, ca, cb):
        return dg(a, b, ca, cb)

    def fwd(a, b, ca, cb):
        return dg(a, b, ca, cb), (a, b)

    def bwd(ca, cb, res, g):
        a, b = res
        da = dg(g, b, 1, 1 - cb) if ca == 1 else dg(b, g, 1 - cb, 1)
        db = dg(a, g, 1 - ca, 0) if cb == 0 else dg(g, a, 0, 1 - ca)
        return da, db

    dot.defvjp(fwd, bwd)
    return dot


bdot = _dot_with_vjp(_dg)
xdot3 = _dot_with_vjp(_dg3)


def nn(a, b):
    return bdot(a, b, 1, 0)


def nt(a, b):
    return bdot(a, b, 1, 1)


def tn(a, b):
    return bdot(a, b, 0, 0)


def xdot(a, b):
    return xdot3(a, b, 1, 0)


def _sigmoid(x):
    return 1.0 / (1.0 + jnp.exp(-x))


def _softplus(x):
    return jnp.maximum(x, 0.0) + jnp.log(1.0 + jnp.exp(-jnp.abs(x)))


def _chunk_masks(rev):
    ii = lax.broadcasted_iota(jnp.int32, (CHUNK, CHUNK), 0)
    jj = lax.broadcasted_iota(jnp.int32, (CHUNK, CHUNK), 1)
    d = (ii - jj) * (1 - 2 * rev)
    return d >= 0, d > 0, ii == jj, (ii >> 3) == (jj >> 3)


def _each(f, *lists):
    return [f(*xs) for xs in zip(*lists)]


@jax.custom_vjp
def _unit_triangular_inverse(a, ident, blockdiag):
    return _unit_triangular_inverse_impl(a, ident, blockdiag)


def _unit_triangular_inverse_fwd(a, ident, blockdiag):
    t = _unit_triangular_inverse_impl(a, ident, blockdiag)
    return t, (t, ident, blockdiag)


def _unit_triangular_inverse_bwd(res, g):
    t, ident, blockdiag = res
    left = _each(lambda x, y: xdot3(x, y, 0, 0), t, g)
    da = _each(lambda x, y: -xdot3(x, y, 1, 1), left, t)
    return da, jnp.zeros_like(ident), jnp.zeros_like(blockdiag)


_unit_triangular_inverse.defvjp(_unit_triangular_inverse_fwd, _unit_triangular_inverse_bwd)


@jax.custom_vjp
def _known_inverse(a, t):
    return t


def _known_inverse_bwd(t, g):
    left = _each(lambda x, y: xdot3(x, y, 0, 0), t, g)
    return _each(lambda x, y: -xdot3(x, y, 1, 1), left, t), _each(jnp.zeros_like, t)


_known_inverse.defvjp(lambda a, t: (t, t), _known_inverse_bwd)


def _unit_triangular_inverse_impl(a, ident, blockdiag):
    ad = _each(lambda x: x * blockdiag, a)
    e = _each(lambda x, y: x - y, a, ad)
    dinv = _each(lambda x: ident - x, ad)
    p = _each(xdot, ad, ad)
    dinv = _each(lambda x, y: x + xdot(x, y), dinv, p)
    p = _each(xdot, p, p)
    dinv = _each(lambda x, y: x + xdot(x, y), dinv, p)
    g = _each(lambda x, y: -xdot(x, y), dinv, e)
    finv = _each(lambda x: ident + x, g)
    p = _each(xdot, g, g)
    finv = _each(lambda x, y: x + xdot(x, y), finv, p)
    p = _each(xdot, p, p)
    finv = _each(lambda x, y: x + xdot(x, y), finv, p)
    return _each(xdot, finv, dinv)


def _gdn_step(state, q, k, v, bb, gb, rev, t_saved=None):
    causal, strict, eye, blockdiag = _chunk_masks(rev)
    lower = causal.astype(F32)
    ones = jnp.ones((CHUNK, CHUNK), F32)
    gcb = _each(lambda x: xdot(lower, x), gb)
    gcol = _each(lambda x: x[:, :CHUNK], gcb)
    grow = _each(lambda x: xdot(ones, jnp.where(eye, x, 0.0)), gcol)
    decay = _each(lambda x, y: jnp.exp(jnp.where(causal, x - y, NEG_INF)), gcol, grow)
    kb = _each(lambda x, y: x * y, k, bb)
    a = _each(lambda x, y, z: jnp.where(strict, nt(x, y) * z, 0.0), kb, k, decay)
    if t_saved is None:
        t = _unit_triangular_inverse(a, eye.astype(F32), blockdiag.astype(F32))
    else:
        t = _known_inverse(a, t_saved)
    egc = _each(jnp.exp, gcb)
    u = _each(lambda x, y, z: xdot(x, y * z), t, v, bb)
    w = _each(lambda x, y, z: xdot(x, y * z), t, kb, egc)
    qk = _each(lambda x, y, z: nt(x, y) * z, q, k, decay)
    glast = _each(lambda x: jnp.sum(x, axis=0, keepdims=True), gb)
    v_new = _each(lambda x, y, z: x - nn(y, z), u, w, state)
    o = _each(lambda x, y, z, p, r: nn(x * y, z) + nn(p, r), q, egc, state, qk, v_new)
    k_dec = _each(lambda x, y, z: x * jnp.exp(y - z), k, glast, gcb)
    state_new = _each(lambda x, y, z, p: x * jnp.exp(y) + tn(z, p), state, glast, k_dec, v_new)
    return state_new, o, t


def _gla_step(state_t, q, k, v, la, rev):
    causal, _, _, _ = _chunk_masks(rev)
    lower = causal.astype(F32)
    sign = 1 - 2 * rev
    b = _each(lambda x: xdot(lower, x), la)
    q = _each(lambda x: x * (B_DK ** -0.5), q)
    row = lax.broadcasted_iota(jnp.int32, (CHUNK, B_DK), 0)
    sub = row // GLA_SUB
    scores = None
    for blk in range(CHUNK // GLA_SUB):
        r_at = jnp.where(rev == 1, GLA_SUB * (blk + 1), GLA_SUB * blk - 1)
        r = _each(lambda x: jnp.sum(jnp.where(row == r_at, x, 0.0), axis=0, keepdims=True), b)
        q_blk = _each(lambda x, y, z: x * jnp.exp(jnp.where(sub == blk, y - z, NEG_INF)), q, b, r)
        k_past = _each(lambda x, y, z: x * jnp.exp(jnp.where((sub - blk) * sign < 0, z - y, NEG_INF)), k, b, r)
        part = _each(lambda x, y: xdot3(x, y, 1, 1), q_blk, k_past)
        scores = part if scores is None else _each(lambda x, y: x + y, scores, part)
    shp = (GLA_SUB, GLA_SUB, B_DK)
    d3 = (lax.broadcasted_iota(jnp.int32, shp, 0) - lax.broadcasted_iota(jnp.int32, shp, 1)) * sign
    place_r = lax.broadcasted_iota(jnp.int32, (GLA_SUB, CHUNK), 0)
    place_c = lax.broadcasted_iota(jnp.int32, (GLA_SUB, CHUNK), 1)
    diag = []
    for blk in range(CHUNK // GLA_SUB):
        rows = slice(blk * GLA_SUB, (blk + 1) * GLA_SUB)
        place = (place_c == place_r + blk * GLA_SUB).astype(F32)

        def pairs(qh, kh, bh):
            qb, kb, bb = qh[rows], kh[rows], bh[rows]
            dec = jnp.exp(jnp.where(d3 >= 0, bb[:, None, :] - bb[None, :, :], NEG_INF))
            return xdot(jnp.sum(qb[:, None, :] * kb[None, :, :] * dec, axis=-1), place)

        diag.append(_each(pairs, q, k, b))
    scores = _each(lambda x, *d: x + jnp.concatenate(d, axis=0), scores, *diag)
    blast = _each(lambda x: jnp.sum(x, axis=0, keepdims=True), la)
    o = _each(lambda x, y, z, s, w: nt(x * jnp.exp(y), z) + nn(s, w), q, b, state_t, scores, v)
    k_dec = _each(lambda x, y, z: x * jnp.exp(y - z), k, blast, b)
    state_new = _each(lambda x, y, z, w: jnp.exp(x) * y + tn(z, w), blast, state_t, v, k_dec)
    return state_new, o


def _chunk_pos(d, m, n):
    return m + d * (n - 1 - 2 * m)


GLA_SUB = 16
GDN_HEADS_PER_STEP = 8
def gdn_rec_fwd(q, k, v, beta_b, g_b):
    s = q.shape[0]
    n = s // CHUNK

    hb = GDN_HEADS_PER_STEP
    wide = hb * LANES

    def body(q_ref, k_ref, v_ref, bb_ref, gb_ref, o_ref, st_ref, t_ref, state):
        d = pl.program_id(0)

        @pl.when(pl.program_id(2) == 0)
        def _():
            state[...] = jnp.zeros_like(state)

        cols = [slice(hh * LANES, (hh + 1) * LANES) for hh in range(hb)]
        st = [state[hh] for hh in range(hb)]
        new, o, t = _gdn_step(st, *([r[:, c] for c in cols] for r in (q_ref, k_ref, v_ref, bb_ref, gb_ref)), d)
        for hh, c in enumerate(cols):
            st_ref[hh] = st[hh]
            t_ref[hh] = t[hh]
            state[hh] = new[hh]
            o_ref[:, c] = o[hh]

    blk = pl.BlockSpec((CHUNK, wide), lambda d, h, m: (_chunk_pos(d, m, n), h))
    gate = pl.BlockSpec((CHUNK, wide), lambda d, h, m: (_chunk_pos(d, m, n), d * (A_HEADS // hb) + h))
    return pl.pallas_call(
        body, name="gdn_rec_fwd", grid=(2, A_HEADS // hb, n),
        in_specs=[blk, blk, blk, gate, gate],
        out_specs=[pl.BlockSpec((None, CHUNK, wide), lambda d, h, m: (d, _chunk_pos(d, m, n), h)),
                   pl.BlockSpec((None, hb, None, A_DK, LANES), lambda d, h, m: (d, h, _chunk_pos(d, m, n), 0, 0)),
                   pl.BlockSpec((None, hb, None, CHUNK, CHUNK), lambda d, h, m: (d, h, _chunk_pos(d, m, n), 0, 0))],
        out_shape=[jax.ShapeDtypeStruct((2, s, A_W), F32), jax.ShapeDtypeStruct((2, A_HEADS, n, A_DK, LANES), F32),
                   jax.ShapeDtypeStruct((2, A_HEADS, n, CHUNK, CHUNK), F32)],
        scratch_shapes=[pltpu.VMEM((hb, A_DK, LANES), F32)],
        compiler_params=_cparams(("arbitrary", "arbitrary", "arbitrary")),
    )(q, k, v, beta_b, g_b)


def gdn_rec_bwd(q, k, v, beta_b, g_b, states, tinv, do):
    s = q.shape[0]
    n = s // CHUNK

    hb = GDN_HEADS_PER_STEP
    wide = hb * LANES

    def body(q_ref, k_ref, v_ref, bb_ref, gb_ref, st_ref, t_ref, do_ref, dq_ref, dk_ref, dv_ref, dbb_ref, dgb_ref, dstate):
        d = pl.program_id(0)

        @pl.when(pl.program_id(2) == 0)
        def _():
            dstate[...] = jnp.zeros_like(dstate)

        def step(*a):
            return _gdn_step(*a, d, t_saved=[t_ref[hh] for hh in range(hb)])[:2]

        cols = [slice(hh * LANES, (hh + 1) * LANES) for hh in range(hb)]
        _, vjp = jax.vjp(step, [st_ref[hh] for hh in range(hb)],
                         *([r[:, c] for c in cols] for r in (q_ref, k_ref, v_ref, bb_ref, gb_ref)))
        grads = vjp(([dstate[hh] for hh in range(hb)], [do_ref[:, c] for c in cols]))
        for hh, c in enumerate(cols):
            dstate[hh], dq_ref[:, c], dk_ref[:, c], dv_ref[:, c], dbb_ref[:, c], dgb_ref[:, c] = (g[hh] for g in grads)

    pos = lambda d, m: _chunk_pos(1 - d, m, n)
    blk = pl.BlockSpec((CHUNK, wide), lambda d, h, m: (pos(d, m), h))
    gate = pl.BlockSpec((CHUNK, wide), lambda d, h, m: (pos(d, m), d * (A_HEADS // hb) + h))
    oblk = pl.BlockSpec((None, CHUNK, wide), lambda d, h, m: (d, pos(d, m), h))
    return pl.pallas_call(
        body, name="gdn_rec_bwd", grid=(2, A_HEADS // hb, n),
        in_specs=[blk, blk, blk, gate, gate,
                  pl.BlockSpec((None, hb, None, A_DK, LANES), lambda d, h, m: (d, h, pos(d, m), 0, 0)),
                  pl.BlockSpec((None, hb, None, CHUNK, CHUNK), lambda d, h, m: (d, h, pos(d, m), 0, 0)), blk],
        out_specs=[oblk, oblk, oblk, gate, gate],
        out_shape=[jax.ShapeDtypeStruct((2, s, A_W), F32)] * 3 + [jax.ShapeDtypeStruct(beta_b.shape, F32)] * 2,
        scratch_shapes=[pltpu.VMEM((hb, A_DK, LANES), F32)],
        compiler_params=_cparams(("arbitrary", "arbitrary", "arbitrary")),
    )(q, k, v, beta_b, g_b, states, tinv, do)


def gla_rec_fwd(proj, log_a):
    s = proj.shape[0]
    n = s // CHUNK

    kcols = [slice(h * B_DK, (h + 1) * B_DK) for h in range(B_HEADS)]
    vcols = [slice(h * B_DV, (h + 1) * B_DV) for h in range(B_HEADS)]

    def body(q_ref, k_ref, v_ref, la_ref, o_ref, st_ref, state):
        d = pl.program_id(0)

        @pl.when(pl.program_id(1) == 0)
        def _():
            state[...] = jnp.zeros_like(state)

        st = [state[h] for h in range(B_HEADS)]
        new, o = _gla_step(st, [q_ref[:, c] for c in kcols], [k_ref[:, c] for c in kcols], [v_ref[:, c] for c in vcols],
                           [la_ref[:, c] for c in kcols], d)
        for h in range(B_HEADS):
            st_ref[h] = st[h]
            state[h] = new[h]
            o_ref[:, vcols[h]] = o[h]

    pos = lambda d, m: _chunk_pos(d, m, n)
    return pl.pallas_call(
        body, name="gla_rec_fwd", grid=(2, n),
        in_specs=[pl.BlockSpec((CHUNK, B_KW), lambda d, m: (pos(d, m), 0)),
                  pl.BlockSpec((CHUNK, B_KW), lambda d, m: (pos(d, m), 1)),
                  pl.BlockSpec((CHUNK, B_VW), lambda d, m: (pos(d, m), 2 * B_KW // B_VW)),
                  pl.BlockSpec((None, CHUNK, B_KW), lambda d, m: (d, pos(d, m), 0))],
        out_specs=[pl.BlockSpec((None, CHUNK, B_VW), lambda d, m: (d, pos(d, m), 0)),
                   pl.BlockSpec((None, B_HEADS, None, B_DV, B_DK), lambda d, m: (d, 0, pos(d, m), 0, 0))],
        out_shape=[jax.ShapeDtypeStruct((2, s, B_VW), F32), jax.ShapeDtypeStruct((2, B_HEADS, n, B_DV, B_DK), F32)],
        scratch_shapes=[pltpu.VMEM((B_HEADS, B_DV, B_DK), F32)],
        compiler_params=_cparams(("arbitrary", "arbitrary")),
    )(proj, proj, proj, log_a)


def gla_rec_bwd(proj, log_a, states, do):
    s = proj.shape[0]
    n = s // CHUNK

    kcols = [slice(h * B_DK, (h + 1) * B_DK) for h in range(B_HEADS)]
    vcols = [slice(h * B_DV, (h + 1) * B_DV) for h in range(B_HEADS)]

    def body(q_ref, k_ref, v_ref, la_ref, st_ref, do_ref, dq_ref, dk_ref, dv_ref, dla_ref, dstate):
        d = pl.program_id(0)

        @pl.when(pl.program_id(1) == 0)
        def _():
            dstate[...] = jnp.zeros_like(dstate)

        step = functools.partial(_gla_step, rev=d)
        _, vjp = jax.vjp(step, [st_ref[h] for h in range(B_HEADS)], [q_ref[:, c] for c in kcols],
                         [k_ref[:, c] for c in kcols], [v_ref[:, c] for c in vcols], [la_ref[:, c] for c in kcols])
        dst, dq, dk, dv, dla = vjp(([dstate[h] for h in range(B_HEADS)], [do_ref[:, c] for c in vcols]))
        for h in range(B_HEADS):
            dstate[h] = dst[h]
            dq_ref[:, kcols[h]] = dq[h]
            dk_ref[:, kcols[h]] = dk[h]
            dv_ref[:, vcols[h]] = dv[h]
            dla_ref[:, kcols[h]] = dla[h]

    pos = lambda d, m: _chunk_pos(1 - d, m, n)
    kblk = pl.BlockSpec((None, CHUNK, B_KW), lambda d, m: (d, pos(d, m), 0))
    return pl.pallas_call(
        body, name="gla_rec_bwd", grid=(2, n),
        in_specs=[pl.BlockSpec((CHUNK, B_KW), lambda d, m: (pos(d, m), 0)),
                  pl.BlockSpec((CHUNK, B_KW), lambda d, m: (pos(d, m), 1)),
                  pl.BlockSpec((CHUNK, B_VW), lambda d, m: (pos(d, m), 2 * B_KW // B_VW)),
                  kblk,
                  pl.BlockSpec((None, B_HEADS, None, B_DV, B_DK), lambda d, m: (d, 0, pos(d, m), 0, 0)),
                  pl.BlockSpec((CHUNK, B_VW), lambda d, m: (pos(d, m), 0))],
        out_specs=[kblk, kblk, pl.BlockSpec((None, CHUNK, B_VW), lambda d, m: (d, pos(d, m), 0)), kblk],
        out_shape=[jax.ShapeDtypeStruct((2, s, B_KW), F32), jax.ShapeDtypeStruct((2, s, B_KW), F32),
                   jax.ShapeDtypeStruct((2, s, B_VW), F32), jax.ShapeDtypeStruct((2, s, B_KW), F32)],
        scratch_shapes=[pltpu.VMEM((B_HEADS, B_DV, B_DK), F32)],
        compiler_params=_cparams(("arbitrary", "arbitrary")),
    )(proj, proj, proj, log_a, states, do)


MM_TILE_OUT = 1024
MM_TILE_K = 1024


def _tile(n, pref):
    return pref if n % pref == 0 else n


class Gathered:
    def __init__(self, g, off, kind):
        assert off % D_MODEL == 0 and MM_TILE_OUT == D_MODEL and MM_TILE_K == D_MODEL
        self.g, self.blk, self.kind = g, off // D_MODEL, kind
        self.shape = (D_MODEL, 4 * D_MODEL) if kind == "cols" else (4 * D_MODEL, D_MODEL)

    def spec(self, mode):
        blk = self.blk
        chip_is_k = (self.kind == "rows") == (mode == "nn")
        if chip_is_k:
            return pl.BlockSpec((None, D_MODEL, D_MODEL), lambda i, j, k: (k, blk, 0))
        return pl.BlockSpec((None, D_MODEL, D_MODEL), lambda i, j, k: (j, blk, 0))


def mm(a, b, mode="nn", act=None, epi=None, extra=None, alpha=1.0, chip_major=False, name="mm"):
    if mode == "tn":
        kk, m = a.shape
    else:
        m, kk = a.shape
    nn_ = b.shape[0] if mode == "nt" else b.shape[1]
    tm, tn_, tk = _tile(m, MM_TILE_OUT), _tile(nn_, MM_TILE_OUT), _tile(kk, MM_TILE_K)
    nk = kk // tk
    ca, cb = {"nn": (1, 0), "nt": (1, 1), "tn": (0, 0)}[mode]

    def body(*refs):
        if epi is None:
            a_ref, b_ref, o_ref = refs
        else:
            a_ref, b_ref, e_ref, o_ref = refs
        kstep = pl.program_id(2)
        at = a_ref[...]
        if act == "sqrelu":
            at = jnp.square(jnp.maximum(at, 0.0))
        part = _dg(at, b_ref[...], ca, cb)

        @pl.when(kstep == 0)
        def _():
            o_ref[...] = part

        @pl.when(kstep > 0)
        def _():
            o_ref[...] += part

        if epi is not None:
            @pl.when(kstep == nk - 1)
            def _():
                if epi == "dsqrelu":
                    o_ref[...] = o_ref[...] * (2.0 * jnp.maximum(e_ref[...], 0.0))
                else:
                    o_ref[...] = o_ref[...] + alpha * e_ref[...]

    a_spec = pl.BlockSpec((tk, tm), lambda i, j, k: (k, i)) if mode == "tn" else pl.BlockSpec((tm, tk), lambda i, j, k: (i, k))
    if isinstance(b, Gathered):
        assert mode in ("nn", "nt") and tn_ == D_MODEL and tk == D_MODEL
        b_spec, b = b.spec(mode), b.g
    elif mode == "nt":
        b_spec = pl.BlockSpec((tn_, tk), lambda i, j, k: (j, k))
    else:
        b_spec = pl.BlockSpec((tk, tn_), lambda i, j, k: (k, j))
    o_spec = pl.BlockSpec((tm, tn_), lambda i, j, k: (i, j))
    ins, specs = [a, b], [a_spec, b_spec]
    if epi is not None:
        ins.append(extra)
        specs.append(o_spec)
    out_shape = jax.ShapeDtypeStruct((m, nn_), F32)
    if chip_major:
        assert nn_ == 4 * D_MODEL and tn_ == D_MODEL
        o_spec = pl.BlockSpec((None, tm, D_MODEL), lambda i, j, k: (j, i, 0))
        out_shape = jax.ShapeDtypeStruct((4, m, D_MODEL), F32)
    return pl.pallas_call(
        body, name=name, grid=(m // tm, nn_ // tn_, nk), in_specs=specs, out_specs=o_spec, out_shape=out_shape,
        compiler_params=_cparams(("parallel", "parallel", "arbitrary")),
    )(*ins)


ROWS = 256
POST_ROWS = 1024


def _ln_core(x, m, g, b):
    r = ALPHA * x + m
    mu = jnp.mean(r, axis=-1, keepdims=True)
    xc = r - mu
    var = jnp.mean(xc * xc, axis=-1, keepdims=True)
    rstd = lax.rsqrt(var + LN_EPS)
    xhat = xc * rstd
    return xhat * g + b, xhat, rstd


def ln_fwd(x, m, g, b):
    s, dm = x.shape

    def body(x_ref, m_ref, g_ref, b_ref, o_ref):
        o_ref[...] = _ln_core(x_ref[...], m_ref[...], g_ref[...], b_ref[...])[0]

    row = pl.BlockSpec((ROWS, dm), lambda i: (i, 0))
    vec = pl.BlockSpec((1, dm), lambda i: (0, 0))
    return pl.pallas_call(body, name="ln_fwd", grid=(s // ROWS,), in_specs=[row, row, vec, vec], out_specs=row,
                          out_shape=jax.ShapeDtypeStruct((s, dm), F32), compiler_params=_cparams(("parallel",)))(x, m, g, b)


def ln_bwd(x, m, g, dy):
    s, dm = x.shape

    def body(x_ref, m_ref, g_ref, dy_ref, dr_ref, dg_ref, db_ref):
        gv = g_ref[...]
        _, xhat, rstd = _ln_core(x_ref[...], m_ref[...], gv, jnp.zeros_like(gv))
        dy = dy_ref[...]
        dxh = dy * gv
        dr_ref[...] = rstd * (dxh - jnp.mean(dxh, axis=-1, keepdims=True)
                              - xhat * jnp.mean(dxh * xhat, axis=-1, keepdims=True))

        @pl.when(pl.program_id(0) == 0)
        def _():
            dg_ref[...] = jnp.zeros_like(dg_ref)
            db_ref[...] = jnp.zeros_like(db_ref)

        dg_ref[...] += jnp.sum(dy * xhat, axis=0, keepdims=True)
        db_ref[...] += jnp.sum(dy, axis=0, keepdims=True)

    row = pl.BlockSpec((ROWS, dm), lambda i: (i, 0))
    vec = pl.BlockSpec((1, dm), lambda i: (0, 0))
    return pl.pallas_call(body, name="ln_bwd", grid=(s // ROWS,), in_specs=[row, row, vec, row], out_specs=[row, vec, vec],
                          out_shape=[jax.ShapeDtypeStruct((s, dm), F32), jax.ShapeDtypeStruct((1, dm), F32),
                                     jax.ShapeDtypeStruct((1, dm), F32)],
                          compiler_params=_cparams(("arbitrary",)))(x, m, g, dy)


def loss_head(y, target):
    s, dm = y.shape

    def body(y_ref, t_ref, dy_ref, l_ref):
        e = y_ref[...] - t_ref[...]
        dy_ref[...] = e * (1.0 / dm)

        @pl.when(pl.program_id(0) == 0)
        def _():
            l_ref[...] = jnp.zeros_like(l_ref)

        col = jnp.sum(e * e, axis=0, keepdims=True) * (0.5 / dm)
        acc = col[:, :LANES]
        for c in range(1, dm // LANES):
            acc = acc + col[:, c * LANES:(c + 1) * LANES]
        l_ref[...] += acc

    row = pl.BlockSpec((ROWS, dm), lambda i: (i, 0))
    return pl.pallas_call(body, name="loss_head", grid=(s // ROWS,), in_specs=[row, row],
                          out_specs=[row, pl.BlockSpec((1, LANES), lambda i: (0, 0))],
                          out_shape=[jax.ShapeDtypeStruct((s, dm), F32), jax.ShapeDtypeStruct((1, LANES), F32)],
                          compiler_params=_cparams(("arbitrary",)))(y, target)


def _shift_rows_impl(x, d):
    n = x.shape[0]
    if d == 0:
        return x
    t = lax.broadcasted_iota(jnp.int32, x.shape, 0)
    return jnp.where((t + d >= 0) & (t + d < n), pltpu.roll(x, (-d) % n, 0), 0.0)


@functools.partial(jax.custom_vjp, nondiff_argnums=(1,))
def _shift_rows(x, d):
    return _shift_rows_impl(x, d)


_shift_rows.defvjp(lambda x, d: (_shift_rows_impl(x, d), None), lambda d, _, g: (_shift_rows_impl(g, -d),))


def _gdn_pre_fn(u, w, kind):
    rows = lax.broadcasted_iota(jnp.int32, w.shape, 0)
    c = None
    for tap in range(A_CONV):
        w_tap = jnp.sum(jnp.where(rows == tap, w, 0.0), axis=0, keepdims=True)
        term = _shift_rows(u, tap - A_CONV // 2) * w_tap
        c = term if c is None else c + term
    y = c * _sigmoid(c)
    if kind == "v":
        return y
    y = y * lax.rsqrt(jnp.sum(y * y, axis=-1, keepdims=True) + L2_EPS)
    return y * (A_DK ** -0.5) if kind == "q" else y


_KIND_OFF = {"q": 0, "k": A_HEADS, "v": 2 * A_HEADS}


def gdn_pre(proj, conv_w, kind):
    s = proj.shape[0]
    off = _KIND_OFF[kind]

    def body(u_ref, w_ref, o_ref):
        o_ref[...] = _gdn_pre_fn(u_ref[...], w_ref[...], kind)

    return pl.pallas_call(
        body, name="gdn_pre_" + kind, grid=(A_HEADS,),
        in_specs=[pl.BlockSpec((s, LANES), lambda h: (0, off + h)), pl.BlockSpec((A_CONV, LANES), lambda h: (0, off + h))],
        out_specs=pl.BlockSpec((s, LANES), lambda h: (0, h)),
        out_shape=jax.ShapeDtypeStruct((s, A_W), F32), compiler_params=_cparams(("parallel",)))(proj, conv_w)


def gdn_pre_bwd(proj, conv_w, dt2, kind):
    s = proj.shape[0]
    off = _KIND_OFF[kind]

    def body(u_ref, w_ref, d0_ref, d1_ref, du_ref, dw_ref):
        _, vjp = jax.vjp(functools.partial(_gdn_pre_fn, kind=kind), u_ref[...], w_ref[...])
        du, dw = vjp(d0_ref[...] + d1_ref[...])
        du_ref[...] = du
        dw_ref[...] = dw

    return pl.pallas_call(
        body, name="gdn_pre_bwd_" + kind, grid=(A_HEADS,),
        in_specs=[pl.BlockSpec((s, LANES), lambda h: (0, off + h)), pl.BlockSpec((A_CONV, LANES), lambda h: (0, off + h)),
                  pl.BlockSpec((None, s, LANES), lambda h: (0, 0, h)), pl.BlockSpec((None, s, LANES), lambda h: (1, 0, h))],
        out_specs=[pl.BlockSpec((s, LANES), lambda h: (0, h)), pl.BlockSpec((A_CONV, LANES), lambda h: (0, h))],
        out_shape=[jax.ShapeDtypeStruct((s, A_W), F32), jax.ShapeDtypeStruct((A_CONV, A_W), F32)],
        compiler_params=_cparams(("parallel",)))(proj, conv_w, dt2, dt2)


N_GATE = 2 * A_HEADS


def _gdn_gates_fn(ba, alog_row, dt_row):
    r = lax.broadcasted_iota(jnp.int32, (LANES, N_GATE * LANES), 0)
    c = lax.broadcasted_iota(jnp.int32, (LANES, N_GATE * LANES), 1) >> 7
    beta_b = xdot(_sigmoid(ba), (r == c).astype(F32))
    g = -(jnp.exp(alog_row) * _softplus(ba + dt_row))
    g_b = xdot(g, (r == c + N_GATE).astype(F32))
    return beta_b, g_b


def gdn_gates(ba, alog_row, dt_row):
    s = ba.shape[0]

    def body(ba_ref, al_ref, dt_ref, bb_ref, gb_ref):
        bb_ref[...], gb_ref[...] = _gdn_gates_fn(ba_ref[...], al_ref[...], dt_ref[...])

    row = pl.BlockSpec((ROWS, LANES), lambda i: (i, 0))
    vec = pl.BlockSpec((1, LANES), lambda i: (0, 0))
    wide = pl.BlockSpec((ROWS, N_GATE * LANES), lambda i: (i, 0))
    return pl.pallas_call(body, name="gdn_gates", grid=(s // ROWS,), in_specs=[row, vec, vec], out_specs=[wide, wide],
                          out_shape=[jax.ShapeDtypeStruct((s, N_GATE * LANES), F32)] * 2,
                          compiler_params=_cparams(("parallel",)))(ba, alog_row, dt_row)


def gdn_gates_bwd(ba, alog_row, dt_row, dbeta_b, dg_b):
    s = ba.shape[0]

    def body(ba_ref, al_ref, dt_ref, dbb_ref, dgb_ref, dba_ref, dal_ref, ddt_ref):
        _, vjp = jax.vjp(_gdn_gates_fn, ba_ref[...], al_ref[...], dt_ref[...])
        dba, dal, ddt = vjp((dbb_ref[...], dgb_ref[...]))
        dba_ref[...] = dba

        @pl.when(pl.program_id(0) == 0)
        def _():
            dal_ref[...] = jnp.zeros_like(dal_ref)
            ddt_ref[...] = jnp.zeros_like(ddt_ref)

        dal_ref[...] += dal
        ddt_ref[...] += ddt

    row = pl.BlockSpec((ROWS, LANES), lambda i: (i, 0))
    vec = pl.BlockSpec((1, LANES), lambda i: (0, 0))
    wide = pl.BlockSpec((ROWS, N_GATE * LANES), lambda i: (i, 0))
    return pl.pallas_call(body, name="gdn_gates_bwd", grid=(s // ROWS,), in_specs=[row, vec, vec, wide, wide],
                          out_specs=[row, vec, vec],
                          out_shape=[jax.ShapeDtypeStruct((s, LANES), F32), jax.ShapeDtypeStruct((1, LANES), F32),
                                     jax.ShapeDtypeStruct((1, LANES), F32)],
                          compiler_params=_cparams(("arbitrary",)))(ba, alog_row, dt_row, dbeta_b, dg_b)


def _post_fn(o, z, g):
    y = o * lax.rsqrt(jnp.mean(o * o, axis=-1, keepdims=True) + RMS_EPS) * g
    return y * (z * _sigmoid(z))


def mixer_post(o2, proj, norm_g, width, gate_off, name):
    s = o2.shape[1]
    nh = o2.shape[2] // width

    rows = _tile(s, POST_ROWS)

    def body(o0_ref, o1_ref, z_ref, g_ref, y_ref):
        y_ref[...] = _post_fn(o0_ref[...] + o1_ref[...], z_ref[...], g_ref[...])

    ospec = lambda d: pl.BlockSpec((None, rows, width), lambda i, h: (d, i, h))
    return pl.pallas_call(
        body, name=name, grid=(s // rows, nh),
        in_specs=[ospec(0), ospec(1), pl.BlockSpec((rows, width), lambda i, h: (i, gate_off + h)),
                  pl.BlockSpec((1, width), lambda i, h: (0, 0))],
        out_specs=pl.BlockSpec((rows, width), lambda i, h: (i, h)),
        out_shape=jax.ShapeDtypeStruct((s, o2.shape[2]), F32),
        compiler_params=_cparams(("parallel", "parallel")))(o2, o2, proj, norm_g)


def mixer_post_bwd(o2, proj, norm_g, dy, width, gate_off, name):
    s = o2.shape[1]
    nh = o2.shape[2] // width

    def body(o0_ref, o1_ref, z_ref, g_ref, dy_ref, do_ref, dz_ref, dg_ref):
        _, vjp = jax.vjp(_post_fn, o0_ref[...] + o1_ref[...], z_ref[...], g_ref[...])
        do, dz, dg = vjp(dy_ref[...])
        do_ref[...] = do
        dz_ref[...] = dz

        @pl.when((pl.program_id(0) == 0) & (pl.program_id(1) == 0))
        def _():
            dg_ref[...] = jnp.zeros_like(dg_ref)

        dg_ref[...] += dg

    rows = _tile(s, POST_ROWS)
    ospec = lambda d: pl.BlockSpec((None, rows, width), lambda i, h: (d, i, h))
    blk = pl.BlockSpec((rows, width), lambda i, h: (i, h))
    vec = pl.BlockSpec((1, width), lambda i, h: (0, 0))
    return pl.pallas_call(
        body, name=name, grid=(s // rows, nh),
        in_specs=[ospec(0), ospec(1), pl.BlockSpec((rows, width), lambda i, h: (i, gate_off + h)), vec, blk],
        out_specs=[blk, blk, vec],
        out_shape=[jax.ShapeDtypeStruct((s, o2.shape[2]), F32)] * 2 + [jax.ShapeDtypeStruct((1, width), F32)],
        compiler_params=_cparams(("arbitrary", "arbitrary")))(o2, o2, proj, norm_g, dy)


def _log_gate(z):
    return (jnp.minimum(z, 0.0) - jnp.log(1.0 + jnp.exp(-jnp.abs(z)))) * (1.0 / B_TAU)


def gla_gate(gl, w2, gb):
    s = gl.shape[0]

    def body(gl_ref, w_ref, b_ref, o_ref):
        for n in range(2):
            o_ref[n] = _log_gate(nn(gl_ref[...], w_ref[n]) + b_ref[n])

    full = lambda shp: pl.BlockSpec(shp, lambda i: (0,) * len(shp))
    return pl.pallas_call(
        body, name="gla_gate", grid=(s // ROWS,),
        in_specs=[pl.BlockSpec((ROWS, LANES), lambda i: (i, 0)), full(w2.shape), full(gb.shape)],
        out_specs=pl.BlockSpec((2, ROWS, B_KW), lambda i: (0, i, 0)),
        out_shape=jax.ShapeDtypeStruct((2, s, B_KW), F32), compiler_params=_cparams(("parallel",)))(gl, w2, gb)


def gla_gate_bwd(gl, w2, gb, dla):
    s = gl.shape[0]

    def body(gl_ref, w_ref, b_ref, dla_ref, dgl_ref, dz_ref, db0_ref, db1_ref):
        @pl.when(pl.program_id(0) == 0)
        def _():
            db0_ref[...] = jnp.zeros_like(db0_ref)
            db1_ref[...] = jnp.zeros_like(db1_ref)

        dgl = None
        for n, db_ref in enumerate((db0_ref, db1_ref)):
            _, vjp = jax.vjp(_log_gate, nn(gl_ref[...], w_ref[n]) + b_ref[n])
            dz, = vjp(dla_ref[n])
            dz_ref[n] = dz
            db_ref[...] += jnp.sum(dz, axis=0, keepdims=True)
            part = nt(dz, w_ref[n])
            dgl = part if dgl is None else dgl + part
        dgl_ref[...] = dgl

    full = lambda shp: pl.BlockSpec(shp, lambda i: (0,) * len(shp))
    row = pl.BlockSpec((ROWS, LANES), lambda i: (i, 0))
    wide = pl.BlockSpec((2, ROWS, B_KW), lambda i: (0, i, 0))
    vec = pl.BlockSpec((1, B_KW), lambda i: (0, 0))
    return pl.pallas_call(
        body, name="gla_gate_bwd", grid=(s // ROWS,),
        in_specs=[row, full(w2.shape), full(gb.shape), wide],
        out_specs=[row, wide, vec, vec],
        out_shape=[jax.ShapeDtypeStruct((s, LANES), F32), jax.ShapeDtypeStruct((2, s, B_KW), F32),
                   jax.ShapeDtypeStruct((1, B_KW), F32), jax.ShapeDtypeStruct((1, B_KW), F32)],
        compiler_params=_cparams(("arbitrary",)))(gl, w2, gb, dla)


PACK_TILE = 512


def cast_into_slot(x, chip):
    r, c = x.shape

    def body(chip_ref, x_ref, o_ref):
        o_ref[...] = x_ref[...].astype(BF16)

    return pl.pallas_call(
        body, name="cast_into_slot",
        grid_spec=pltpu.PrefetchScalarGridSpec(
            num_scalar_prefetch=1, grid=(r // PACK_TILE,),
            in_specs=[pl.BlockSpec((PACK_TILE, c), lambda i, chip_ref: (i, 0))],
            out_specs=pl.BlockSpec((None, PACK_TILE, c), lambda i, chip_ref: (chip_ref[0], i, 0))),
        out_shape=jax.ShapeDtypeStruct((4, r, c), BF16), compiler_params=_cparams(("parallel",)))(chip, x)


def sum_received(chip_sum, recv, chip, core):
    _, h, c = chip_sum.shape
    n = recv.shape[0]
    tr = _tile(h, PACK_TILE)
    nblk = h // tr

    def body(chip_ref, core_ref, own_ref, r_ref, o_ref):
        acc = r_ref[0].astype(F32)
        for k in range(1, n):
            acc = acc + r_ref[k].astype(F32)
        o_ref[...] = acc + own_ref[...].astype(F32)

    return pl.pallas_call(
        body, name="sum_received",
        grid_spec=pltpu.PrefetchScalarGridSpec(
            num_scalar_prefetch=2, grid=(nblk,),
            in_specs=[pl.BlockSpec((None, tr, c), lambda i, chip_ref, core_ref: (chip_ref[0], i, 0)),
                      pl.BlockSpec((n, tr, c), lambda i, chip_ref, core_ref: (0, i, 0))],
            out_specs=pl.BlockSpec((tr, c), lambda i, chip_ref, core_ref: (core_ref[0] * nblk + i, 0))),
        out_shape=jax.ShapeDtypeStruct((2 * h, c), F32), compiler_params=_cparams(("parallel",)))(chip, core, chip_sum, recv)


def merge_first_hop(chip_sum, passed, slot_x, slot_y, core):
    _, h, c = chip_sum.shape
    tr = _tile(h, PACK_TILE)

    def body(sx_ref, sy_ref, core_ref, to_x_ref, to_y_ref, p_ref, o_ref):
        p = p_ref[...].astype(F32)
        is_y = core_ref[0].astype(F32)
        o_ref[0] = (to_x_ref[...].astype(F32) + p * (1.0 - is_y)).astype(BF16)
        o_ref[1] = (to_y_ref[...].astype(F32) + p * is_y).astype(BF16)

    return pl.pallas_call(
        body, name="merge_first_hop",
        grid_spec=pltpu.PrefetchScalarGridSpec(
            num_scalar_prefetch=3, grid=(h // tr,),
            in_specs=[pl.BlockSpec((None, tr, c), lambda i, sx_ref, sy_ref, core_ref: (sx_ref[0], i, 0)),
                      pl.BlockSpec((None, tr, c), lambda i, sx_ref, sy_ref, core_ref: (sy_ref[0], i, 0)),
                      pl.BlockSpec((tr, c), lambda i, sx_ref, sy_ref, core_ref: (i, 0))],
            out_specs=pl.BlockSpec((2, tr, c), lambda i, sx_ref, sy_ref, core_ref: (0, i, 0))),
        out_shape=jax.ShapeDtypeStruct((2, h, c), BF16),
        compiler_params=_cparams(("parallel",)))(slot_x, slot_y, core, chip_sum, chip_sum, passed)


def sum_slots(x, name):
    n, r, c = x.shape
    tr = _tile(r, PACK_TILE)

    def body(x_ref, o_ref):
        acc = x_ref[0].astype(F32)
        for k in range(1, n):
            acc = acc + x_ref[k].astype(F32)
        o_ref[...] = acc

    return pl.pallas_call(body, name=name, grid=(r // tr,), in_specs=[pl.BlockSpec((n, tr, c), lambda i: (0, i, 0))],
                          out_specs=pl.BlockSpec((tr, c), lambda i: (i, 0)),
                          out_shape=jax.ShapeDtypeStruct((r, c), F32), compiler_params=_cparams(("parallel",)))(x)


def add_sibling_half(gpack, theirs, core):
    n, r, c = gpack.shape
    half_rows = r // 2
    tr = _tile(half_rows, PACK_TILE)
    nblk = half_rows // tr

    def body(core_ref, g_ref, t_ref, o_ref):
        o_ref[...] = (g_ref[...] + t_ref[...]).astype(BF16)

    blk = pl.BlockSpec((None, tr, c), lambda s, i, core_ref: (s, i, 0))
    return pl.pallas_call(
        body, name="add_sibling_half",
        grid_spec=pltpu.PrefetchScalarGridSpec(
            num_scalar_prefetch=1, grid=(n, nblk),
            in_specs=[pl.BlockSpec((None, tr, c), lambda s, i, core_ref: (s, core_ref[0] * nblk + i, 0)), blk],
            out_specs=blk),
        out_shape=jax.ShapeDtypeStruct((n, half_rows, c), BF16),
        compiler_params=_cparams(("parallel", "parallel")))(core, gpack, theirs)


def adamw(w, m, v, grads, g_row_off, name):
    r, c = w.shape
    tr = next(t for t in (PACK_TILE, r) if r % t == 0 and g_row_off % t == 0)
    ob = g_row_off // tr
    ng = len(grads)

    def body(*refs):
        w_ref, m_ref, v_ref = refs[:3]
        g_refs = refs[3:3 + ng]
        g_ref, d_ref, nm_ref, nv_ref = refs[3 + ng:]
        g = g_refs[0][...]
        for gr in g_refs[1:]:
            g = g + gr[...]
        m_new = ADAM_B1 * m_ref[...] + (1.0 - ADAM_B1) * g
        v_new = ADAM_B2 * v_ref[...] + (1.0 - ADAM_B2) * jnp.square(g)
        m_hat = m_new / (1.0 - ADAM_B1 ** ADAM_STEP)
        v_hat = v_new / (1.0 - ADAM_B2 ** ADAM_STEP)
        g_ref[...] = g
        d_ref[...] = -ADAM_LR * (m_hat / (jnp.sqrt(v_hat) + ADAM_EPS) + ADAM_WD * w_ref[...])
        nm_ref[...] = m_new
        nv_ref[...] = v_new

    blk = pl.BlockSpec((tr, c), lambda i: (i, 0))
    gblk = pl.BlockSpec((tr, c), lambda i: (i + ob, 0))
    return pl.pallas_call(body, name=name, grid=(r // tr,), in_specs=[blk, blk, blk] + [gblk] * ng, out_specs=[blk] * 4,
                          out_shape=[jax.ShapeDtypeStruct((r, c), F32)] * 4,
                          compiler_params=_cparams(("parallel",)))(w, m, v, *grads)


MESH = pl.DeviceIdType.MESH
HBM = pl.BlockSpec(memory_space=pl.ANY)
CHIP_FLIPS = ((1, 0), (0, 1), (1, 1))


def _place():
    return lax.axis_index("x"), lax.axis_index("y"), lax.axis_index("c")


def allgather_chips(buf):
    _, r, c = buf.shape
    half_rows = r // 2

    def body(_, out_ref, send_sems, recv_sems):
        x, y, cc = _place()
        half = pl.ds(cc * half_rows, half_rows)
        other = pl.ds((1 - cc) * half_rows, half_rows)

        def copy(k, rows, to):
            return pltpu.make_async_remote_copy(src_ref=rows, dst_ref=rows, send_sem=send_sems.at[k],
                                                recv_sem=recv_sems.at[k], device_id=to, device_id_type=MESH)

        nbr_x, nbr_y, diag = (1 - x, y), (x, 1 - y), (1 - x, 1 - y)
        slot = lambda chip: 2 * chip[0] + chip[1]
        sibling = (x, y, 1 - cc)
        first = [copy(0, out_ref.at[slot((x, y)), half], (*nbr_x, cc)), copy(1, out_ref.at[slot((x, y)), half], (*nbr_y, cc))]
        for cp in first:
            cp.start()
        passed = []
        for k, chip in enumerate((nbr_x, nbr_y)):
            landed = out_ref.at[slot(chip), half]
            copy(k, landed, (*chip, cc)).wait_recv()
            passed.append(copy(3 + k, landed, sibling))
            passed[-1].start()
        via = (1 - x + cc * (2 * x - 1), y + cc * (1 - 2 * y))
        to = (x + cc * (1 - 2 * x), 1 - y + cc * (2 * y - 1))
        hop = copy(2, out_ref.at[slot(via), half], (*to, cc))
        hop.start()
        landed = out_ref.at[slot(diag), half]
        copy(2, landed, (*to, cc)).wait_recv()
        passed.append(copy(5, landed, sibling))
        passed[-1].start()
        for k, chip in enumerate((nbr_x, nbr_y, diag)):
            copy(3 + k, out_ref.at[slot(chip), other], sibling).wait_recv()
        for cp in first + [hop] + passed:
            cp.wait_send()

    return pl.pallas_call(
        body, name="allgather_chips", in_specs=[HBM], out_specs=HBM, input_output_aliases={0: 0},
        out_shape=jax.ShapeDtypeStruct(buf.shape, buf.dtype),
        scratch_shapes=[pltpu.SemaphoreType.DMA((6,)), pltpu.SemaphoreType.DMA((6,))],
    )(buf)


def scatter_first_hop(gpack):
    _, r, c = gpack.shape

    def body(src_ref, out_ref, send_sem, recv_sem):
        x, y, cc = _place()
        to = (x + cc * (1 - 2 * x), 1 - y + cc * (2 * y - 1), cc)
        cp = pltpu.make_async_remote_copy(src_ref=src_ref.at[2 * (1 - x) + (1 - y)], dst_ref=out_ref, send_sem=send_sem,
                                          recv_sem=recv_sem, device_id=to, device_id_type=MESH)
        cp.start()
        cp.wait()

    return pl.pallas_call(
        body, name="scatter_first_hop", in_specs=[HBM], out_specs=HBM,
        out_shape=jax.ShapeDtypeStruct((r, c), gpack.dtype),
        scratch_shapes=[pltpu.SemaphoreType.DMA, pltpu.SemaphoreType.DMA],
    )(gpack)


def scatter_second_hop(to_nbrs):
    def body(src_ref, out_ref, send_sems, recv_sems):
        x, y, cc = _place()
        sends = [pltpu.make_async_remote_copy(src_ref=src_ref.at[k], dst_ref=out_ref.at[k], send_sem=send_sems.at[k],
                                              recv_sem=recv_sems.at[k], device_id=to, device_id_type=MESH)
                 for k, to in enumerate(((1 - x, y, cc), (x, 1 - y, cc)))]
        for cp in sends:
            cp.start()
        for cp in sends:
            cp.wait_recv()
        for cp in sends:
            cp.wait_send()

    return pl.pallas_call(
        body, name="scatter_second_hop", in_specs=[HBM], out_specs=HBM,
        out_shape=jax.ShapeDtypeStruct(to_nbrs.shape, to_nbrs.dtype),
        scratch_shapes=[pltpu.SemaphoreType.DMA((2,)), pltpu.SemaphoreType.DMA((2,))],
    )(to_nbrs)


def sibling_halves(gpack):
    n, r, c = gpack.shape
    half_rows = r // 2

    def body(src_ref, out_ref, send_sem, recv_sem):
        x, y, cc = _place()
        cp = pltpu.make_async_remote_copy(
            src_ref=src_ref.at[:, pl.ds((1 - cc) * half_rows, half_rows)], dst_ref=out_ref, send_sem=send_sem,
            recv_sem=recv_sem, device_id=(x, y, 1 - cc), device_id_type=MESH)
        cp.start()
        cp.wait()

    return pl.pallas_call(
        body, name="sibling_halves", in_specs=[HBM], out_specs=HBM,
        out_shape=jax.ShapeDtypeStruct((n, half_rows, c), gpack.dtype),
        scratch_shapes=[pltpu.SemaphoreType.DMA, pltpu.SemaphoreType.DMA],
    )(gpack)


def join_halves(buf):
    r, c = buf.shape
    half_rows = r // 2

    def body(_, out_ref, send_sem, recv_sem):
        x, y, cc = _place()
        half = out_ref.at[pl.ds(cc * half_rows, half_rows)]
        other = out_ref.at[pl.ds((1 - cc) * half_rows, half_rows)]
        send = pltpu.make_async_remote_copy(src_ref=half, dst_ref=half, send_sem=send_sem, recv_sem=recv_sem,
                                            device_id=(x, y, 1 - cc), device_id_type=MESH)
        send.start()
        pltpu.make_async_remote_copy(src_ref=other, dst_ref=other, send_sem=send_sem, recv_sem=recv_sem,
                                     device_id=(x, y, 1 - cc), device_id_type=MESH).wait_recv()
        send.wait_send()

    return pl.pallas_call(
        body, name="join_halves", in_specs=[HBM], out_specs=HBM, input_output_aliases={0: 0},
        out_shape=jax.ShapeDtypeStruct(buf.shape, buf.dtype),
        scratch_shapes=[pltpu.SemaphoreType.DMA, pltpu.SemaphoreType.DMA],
    )(buf)


def exchange_all(v, name):
    r, c = v.shape

    def body(v_ref, out_ref, send_sems, recv_sems):
        x, y, cc = _place()
        out_ref[4 * x + 2 * y + cc] = v_ref[...]
        sends, recvs = [], []
        for k in range(1, 8):
            px = 1 - x if k & 4 else x
            py = 1 - y if k & 2 else y
            pc = 1 - cc if k & 1 else cc
            sends.append(pltpu.make_async_remote_copy(
                src_ref=v_ref, dst_ref=out_ref.at[4 * x + 2 * y + cc], send_sem=send_sems.at[k - 1],
                recv_sem=recv_sems.at[k - 1], device_id=(px, py, pc), device_id_type=MESH))
            recvs.append(pltpu.make_async_remote_copy(
                src_ref=v_ref, dst_ref=out_ref.at[4 * px + 2 * py + pc], send_sem=send_sems.at[k - 1],
                recv_sem=recv_sems.at[k - 1], device_id=(px, py, pc), device_id_type=MESH))
        for cp in sends:
            cp.start()
        for cp in recvs:
            cp.wait_recv()
        for cp in sends:
            cp.wait_send()

    vm = pl.BlockSpec(memory_space=pltpu.VMEM)
    return pl.pallas_call(
        body, name=name, in_specs=[vm], out_specs=vm, out_shape=jax.ShapeDtypeStruct((8, r, c), v.dtype),
        scratch_shapes=[pltpu.SemaphoreType.DMA((7,)), pltpu.SemaphoreType.DMA((7,))],
        compiler_params=pltpu.CompilerParams(vmem_limit_bytes=VMEM_LIMIT),
    )(v)


def _as_rows(a, width):
    n = math.prod(a.shape)
    if n % width == 0:
        return a.reshape(-1, width)
    return jnp.pad(a.reshape(1, -1), ((0, 0), (0, -n % width))).reshape(-1, width)


def _n_rows(shape, width):
    return -(-math.prod(shape) // width)


def _pack_rows(arrays, rows, width):
    parts = [_as_rows(a, width) for a in arrays]
    used = sum(p.shape[0] for p in parts)
    return jnp.concatenate(parts + [jnp.zeros((rows - used, width), arrays[0].dtype)], axis=0)


def _unpack_rows(pack, shapes):
    width = pack.shape[1]
    out, off = [], 0
    for shp in shapes:
        nr, n = _n_rows(shp, width), math.prod(shp)
        part = pack[off:off + nr]
        out.append(part.reshape(shp) if n % width == 0 else part.reshape(-1)[:n].reshape(shp))
        off += nr
    return out


def _rows_for(shapes, width, mult=8):
    n = sum(_n_rows(s, width) for s in shapes)
    return -(-n // mult) * mult


def _gdn_fwd(x, p):
    proj = mm(x, p["w_main"], name="gdn_proj")
    ba = mm(x, p["w_gate"], name="gdn_proj_gate")
    q, k, v = (gdn_pre(proj, p["conv"], kind) for kind in "qkv")
    beta_b, g_b = gdn_gates(ba, p["alog_row"], p["dt_row"])
    o2, st, tinv = gdn_rec_fwd(q, k, v, beta_b, g_b)
    y = mixer_post(o2, proj, p["norm_g"], A_DK, 3 * A_HEADS, "gdn_post")
    m = mm(y, p["w_out"], name="gdn_out")
    return m, (x, proj, ba, q, k, v, beta_b, g_b, o2, st, tinv, y)


def _gdn_bwd(saved, p, dm):
    x, proj, ba, q, k, v, beta_b, g_b, o2, st, tinv, y = saved
    d_w_out = mm(y, dm, "tn", name="gdn_dw_out")
    dy = mm(dm, p["w_out"], "nt", name="gdn_dy")
    do, dz, d_norm_g = mixer_post_bwd(o2, proj, p["norm_g"], dy, A_DK, 3 * A_HEADS, "gdn_post_bwd")
    dq2, dk2, dv2, dbb, dgb = gdn_rec_bwd(q, k, v, beta_b, g_b, st, tinv, do)
    dba, d_alog_row, d_dt_row = gdn_gates_bwd(ba, p["alog_row"], p["dt_row"], dbb, dgb)
    du, dconv = zip(*(gdn_pre_bwd(proj, p["conv"], d2, kind) for d2, kind in ((dq2, "q"), (dk2, "k"), (dv2, "v"))))
    dproj = jnp.concatenate(list(du) + [dz], axis=1)
    d_w_main = mm(x, dproj, "tn", name="gdn_dw_main")
    d_w_gate = mm(x, dba, "tn", name="gdn_dw_gate")
    dx = mm(dba, p["w_gate"], "nt", epi="add", extra=dm, alpha=ALPHA, name="gdn_dx_gate")
    dx = mm(dproj, p["w_main"], "nt", epi="add", extra=dx, name="gdn_dx")
    grads = dict(w_in=jnp.concatenate([d_w_main, d_w_gate[:, :2 * N_GATE]], axis=1), conv=jnp.concatenate(dconv, axis=1),
                 alog=d_alog_row[0, N_GATE:2 * N_GATE].reshape(2, A_HEADS), dt=d_dt_row[0, N_GATE:2 * N_GATE].reshape(2, A_HEADS),
                 norm_g=d_norm_g[0], w_out=d_w_out)
    return dx, grads


def _gla_fwd(x, p):
    proj = mm(x, p["w_main"], name="gla_proj")
    gl = mm(x, p["w_gate"], name="gla_proj_gate")
    log_a = gla_gate(gl, p["w2"], p["gate_b"])
    o2, st = gla_rec_fwd(proj, log_a)
    y = mixer_post(o2, proj, p["norm_g"], B_DV, (2 * B_KW + B_VW) // B_DV, "gla_post")
    m = mm(y, p["w_out"], name="gla_out")
    return m, (x, proj, gl, log_a, o2, st, y)


def _gla_bwd(saved, p, dm):
    x, proj, gl, log_a, o2, st, y = saved
    d_w_out = mm(y, dm, "tn", name="gla_dw_out")
    dy = mm(dm, p["w_out"], "nt", name="gla_dy")
    do, dr, d_norm_g = mixer_post_bwd(o2, proj, p["norm_g"], dy, B_DV, (2 * B_KW + B_VW) // B_DV, "gla_post_bwd")
    dq2, dk2, dv2, dla = gla_rec_bwd(proj, log_a, st, do)
    dgl, dz, d_b0, d_b1 = gla_gate_bwd(gl, p["w2"], p["gate_b"], dla)
    d_w2 = [mm(gl, dz[n], "tn", name="gla_dw_gate_w2") for n in range(2)]
    dproj = jnp.concatenate([dq2[0] + dq2[1], dk2[0] + dk2[1], dv2[0] + dv2[1], dr], axis=1)
    d_w_main = mm(x, dproj, "tn", name="gla_dw_main")
    d_w_gate = mm(x, dgl, "tn", name="gla_dw_gate")
    dx = mm(dgl, p["w_gate"], "nt", epi="add", extra=dm, alpha=ALPHA, name="gla_dx_gate")
    dx = mm(dproj, p["w_main"], "nt", epi="add", extra=dx, name="gla_dx")
    grads = dict(w_in=jnp.concatenate([d_w_main, d_w_gate[:, :2 * B_RANK]], axis=1),
                 gate_w2=jnp.stack([d_w2[n][n * B_RANK:(n + 1) * B_RANK] for n in range(2)]),
                 gate_b=jnp.concatenate([d_b0, d_b1]), norm_g=d_norm_g[0], w_out=d_w_out)
    return dx, grads


def _pad_cols(w, width=LANES):
    return jnp.pad(w, ((0, 0), (0, width - w.shape[1])))


def _local_step(x, target, a_w_in, a_conv, a_alog, a_dt_bias, a_norm_g, a_w_out, b_w_in, b_gate_w2, b_gate_b, b_norm_g,
                b_w_out, ln1_g, ln1_b, mlp_w1, mlp_w2, ln2_g, ln2_b):
    layer_p = []
    for i in range(DEPTH):
        j = i // 2
        if i % 2 == 0:
            layer_p.append(dict(
                w_main=a_w_in[j][:, :4 * A_W], w_gate=_pad_cols(a_w_in[j][:, 4 * A_W:]), conv=a_conv[j],
                alog_row=jnp.pad(a_alog[j].reshape(1, N_GATE), ((0, 0), (N_GATE, LANES - 2 * N_GATE))),
                dt_row=jnp.pad(a_dt_bias[j].reshape(1, N_GATE), ((0, 0), (N_GATE, LANES - 2 * N_GATE))),
                norm_g=a_norm_g[j].reshape(1, A_DK), w_out=a_w_out[j]))
        else:
            w2 = jnp.stack([jnp.pad(b_gate_w2[j][n], ((n * B_RANK, LANES - (n + 1) * B_RANK), (0, 0))) for n in range(2)])
            layer_p.append(dict(
                w_main=b_w_in[j][:, :2 * B_KW + 2 * B_VW], w_gate=_pad_cols(b_w_in[j][:, 2 * B_KW + 2 * B_VW:]),
                w2=w2, gate_b=b_gate_b[j].reshape(2, 1, B_KW), norm_g=b_norm_g[j].reshape(1, B_DV), w_out=b_w_out[j]))

    saved = []
    h = x
    for i in range(DEPTH):
        p = layer_p[i]
        m, sv = (_gdn_fwd if i % 2 == 0 else _gla_fwd)(h, p)
        x1 = ln_fwd(h, m, ln1_g[i:i + 1], ln1_b[i:i + 1])
        h1 = mm(x1, mlp_w1[i], name="mlp_up")
        mlp = mm(h1, mlp_w2[i], act="sqrelu", name="mlp_down")
        x2 = ln_fwd(x1, mlp, ln2_g[i:i + 1], ln2_b[i:i + 1])
        saved.append((sv, h, m, x1, h1, mlp))
        h = x2

    dh, loss_part = loss_head(h, target)

    g_a, g_b, g_ln1g, g_ln1b, g_ln2g, g_ln2b, g_w1, g_w2 = {}, {}, {}, {}, {}, {}, {}, {}
    for i in reversed(range(DEPTH)):
        sv, xin, m, x1, h1, mlp = saved[i]
        p = layer_p[i]
        dr2, g_ln2g[i], g_ln2b[i] = ln_bwd(x1, mlp, ln2_g[i:i + 1], dh)
        g_w2[i] = mm(h1, dr2, "tn", act="sqrelu", name="mlp_dw_down")
        dh1 = mm(dr2, mlp_w2[i], "nt", epi="dsqrelu", extra=h1, name="mlp_dh")
        g_w1[i] = mm(x1, dh1, "tn", chip_major=True, name="mlp_dw_up")
        dx1 = mm(dh1, mlp_w1[i], "nt", epi="add", extra=dr2, alpha=ALPHA, name="mlp_dx")
        dr1, g_ln1g[i], g_ln1b[i] = ln_bwd(xin, m, ln1_g[i:i + 1], dx1)
        dh, g = (_gdn_bwd if i % 2 == 0 else _gla_bwd)(sv, p, dr1)
        (g_a if i % 2 == 0 else g_b)[i // 2] = g

    per_layer = lambda d, key=None: [(d[i] if key is None else d[i][key]) for i in sorted(d)]
    st = lambda d, key=None: jnp.stack(per_layer(d, key))
    grads = dict(
        a_w_in=per_layer(g_a, "w_in"), a_conv=st(g_a, "conv"), a_alog=st(g_a, "alog"), a_dt_bias=st(g_a, "dt"),
        a_norm_g=st(g_a, "norm_g"), a_w_out=per_layer(g_a, "w_out"), b_w_in=per_layer(g_b, "w_in"),
        b_gate_w2=st(g_b, "gate_w2"), b_gate_b=st(g_b, "gate_b"), b_norm_g=st(g_b, "norm_g"),
        b_w_out=per_layer(g_b, "w_out"), ln1_g=st(g_ln1g)[:, 0], ln1_b=st(g_ln1b)[:, 0], mlp_w1=per_layer(g_w1),
        mlp_w2=per_layer(g_w2), ln2_g=st(g_ln2g)[:, 0], ln2_b=st(g_ln2b)[:, 0])
    return loss_part, dh, grads


WEIGHTS = ("a_w_in", "a_conv", "a_alog", "a_dt_bias", "a_norm_g", "a_w_out", "b_w_in", "b_gate_w2", "b_gate_b",
           "b_norm_g", "b_w_out", "ln1_g", "ln1_b", "mlp_w1", "mlp_w2", "ln2_g", "ln2_b")
BIG = ("mlp_w1", "mlp_w2", "a_w_out", "b_w_out", "a_w_in", "b_w_in")
SHARD_AXIS = {"mlp_w1": 2, "mlp_w2": 1, "a_w_out": 1, "b_w_out": 1, "a_w_in": 2, "b_w_in": 2}
SMALL = tuple(n for n in WEIGHTS if n not in BIG)
SMALL_SHARD_AXIS = {"a_conv": 2, "b_gate_w2": 3, "b_gate_b": 2, "b_norm_g": 1}


def _to_chip_major(full, axis):
    shp = full.shape
    t = full.reshape(shp[:axis] + (4, shp[axis] // 4) + shp[axis + 1:])
    return jnp.moveaxis(t, axis, 0)


def _from_chip_major(stacked, axis):
    t = jnp.moveaxis(stacked, 0, axis)
    shp = t.shape
    return t.reshape(shp[:axis] + (shp[axis] * shp[axis + 1],) + shp[axis + 2:])


def kernel(x, a_w_in, a_conv, a_alog, a_dt_bias, a_norm_g, a_w_out, b_w_in, b_gate_w2, b_gate_b, b_norm_g, b_w_out, ln1_g, ln1_b, mlp_w1, mlp_w2, ln2_g, ln2_b, loss_target, m_a_w_in, m_a_conv, m_a_alog, m_a_dt_bias, m_a_norm_g, m_a_w_out, m_b_w_in, m_b_gate_w2, m_b_gate_b, m_b_norm_g, m_b_w_out, m_ln1_g, m_ln1_b, m_mlp_w1, m_mlp_w2, m_ln2_g, m_ln2_b, v_a_w_in, v_a_conv, v_a_alog, v_a_dt_bias, v_a_norm_g, v_a_w_out, v_b_w_in, v_b_gate_w2, v_b_gate_b, v_b_norm_g, v_b_w_out, v_ln1_g, v_ln1_b, v_mlp_w1, v_mlp_w2, v_ln2_g, v_ln2_b):
    w = dict(a_w_in=a_w_in, a_conv=a_conv, a_alog=a_alog, a_dt_bias=a_dt_bias, a_norm_g=a_norm_g, a_w_out=a_w_out,
             b_w_in=b_w_in, b_gate_w2=b_gate_w2, b_gate_b=b_gate_b, b_norm_g=b_norm_g, b_w_out=b_w_out, ln1_g=ln1_g,
             ln1_b=ln1_b, mlp_w1=mlp_w1, mlp_w2=mlp_w2, ln2_g=ln2_g, ln2_b=ln2_b)
    mom = dict(a_w_in=m_a_w_in, a_conv=m_a_conv, a_alog=m_a_alog, a_dt_bias=m_a_dt_bias, a_norm_g=m_a_norm_g,
               a_w_out=m_a_w_out, b_w_in=m_b_w_in, b_gate_w2=m_b_gate_w2, b_gate_b=m_b_gate_b, b_norm_g=m_b_norm_g,
               b_w_out=m_b_w_out, ln1_g=m_ln1_g, ln1_b=m_ln1_b, mlp_w1=m_mlp_w1, mlp_w2=m_mlp_w2, ln2_g=m_ln2_g,
               ln2_b=m_ln2_b)
    var = dict(a_w_in=v_a_w_in, a_conv=v_a_conv, a_alog=v_a_alog, a_dt_bias=v_a_dt_bias, a_norm_g=v_a_norm_g,
               a_w_out=v_a_w_out, b_w_in=v_b_w_in, b_gate_w2=v_b_gate_w2, b_gate_b=v_b_gate_b, b_norm_g=v_b_norm_g,
               b_w_out=v_b_w_out, ln1_g=v_ln1_g, ln1_b=v_ln1_b, mlp_w1=v_mlp_w1, mlp_w2=v_mlp_w2, ln2_g=v_ln2_g,
               ln2_b=v_ln2_b)
    chip = 2 * lax.axis_index("x") + lax.axis_index("y")

    seg_rows = [w[n].size // D_MODEL for n in BIG]
    seg_off = [sum(seg_rows[:i]) for i in range(len(BIG))]
    rows = -(-sum(seg_rows) // PACK_TILE) * PACK_TILE
    shard_pack = jnp.concatenate([w[n].reshape(-1, D_MODEL) for n in BIG]
                                 + [jnp.zeros((rows - sum(seg_rows), D_MODEL), F32)], axis=0)
    chip_idx = chip.astype(jnp.int32).reshape(1)
    gathered = allgather_chips(cast_into_slot(shard_pack, chip_idx))
    full = {}
    for n, off, nr in zip(BIG, seg_off, seg_rows):
        if n in ("mlp_w1", "mlp_w2"):
            kind = "cols" if SHARD_AXIS[n] == 2 else "rows"
            full[n] = [Gathered(gathered, off + i * D_MODEL, kind) for i in range(DEPTH)]
            continue
        stacked = gathered[:, off:off + nr].reshape((4,) + w[n].shape)
        full[n] = _from_chip_major(stacked, SHARD_AXIS[n])
    sharded_small = tuple(SMALL_SHARD_AXIS)
    sm_shapes = [w[n].shape for n in sharded_small]
    sm_rows = _rows_for(sm_shapes, LANES)
    sm_all = exchange_all(_pack_rows([w[n] for n in sharded_small], sm_rows, LANES), "gather_small")
    per_chip = [_unpack_rows(sm_all[2 * pch], sm_shapes) for pch in range(4)]
    for idx, n in enumerate(sharded_small):
        full[n] = jnp.concatenate([per_chip[pch][idx] for pch in range(4)], axis=SMALL_SHARD_AXIS[n])
    for n in WEIGHTS:
        full.setdefault(n, w[n])

    loss_part, grad_x, grads = _local_step(x[0], loss_target[0], *[full[n] for n in WEIGHTS])
    loss = lax.psum(jnp.sum(loss_part), ("x", "y", "c"))

    gpack = jnp.concatenate(
        [(g if n == "mlp_w1" else _to_chip_major(g, SHARD_AXIS[n] - 1).reshape(4, -1, D_MODEL))
         for n in BIG for g in grads[n]]
        + [jnp.zeros((4, rows - sum(seg_rows), D_MODEL), F32)], axis=1)
    core = lax.axis_index("c").astype(jnp.int32).reshape(1)
    chip_sum = add_sibling_half(gpack, sibling_halves(gpack), core)
    ax, ay = lax.axis_index("x"), lax.axis_index("y")
    slot_x = (2 * (1 - ax) + ay).astype(jnp.int32).reshape(1)
    slot_y = (2 * ax + (1 - ay)).astype(jnp.int32).reshape(1)
    to_nbrs = merge_first_hop(chip_sum, scatter_first_hop(chip_sum), slot_x, slot_y, core)
    reduced = join_halves(sum_received(chip_sum, scatter_second_hop(to_nbrs), chip_idx, core))
    out_g, out_d, out_m, out_v = {}, {}, {}, {}
    for n, off, nr in zip(BIG, seg_off, seg_rows):
        if w[n].shape[-1] == D_MODEL:
            view = lambda t: t.reshape(-1, D_MODEL)
            res = adamw(view(w[n]), view(mom[n]), view(var[n]), (reduced,), off, "adamw_" + n)
        else:
            cols = w[n].shape[-1]
            view = lambda t: t.reshape(-1, cols)
            res = adamw(view(w[n]), view(mom[n]), view(var[n]), (view(reduced[off:off + nr]),), 0, "adamw_" + n)
        out_g[n], out_d[n], out_m[n], out_v[n] = (t.reshape(w[n].shape) for t in res)

    all_shapes = [full[n].shape for n in SMALL]
    g_rows = _rows_for(all_shapes, LANES)
    g_all = exchange_all(_pack_rows([grads[n] for n in SMALL], g_rows, LANES), "gather_small_grads")
    g_sum = _unpack_rows(sum_slots(g_all, "sum_small_grads"), all_shapes)
    g_mine = []
    for n, g in zip(SMALL, g_sum):
        if n in SMALL_SHARD_AXIS:
            ax = SMALL_SHARD_AXIS[n]
            g = lax.dynamic_slice_in_dim(g, chip * w[n].shape[ax], w[n].shape[ax], axis=ax)
        g_mine.append(g)
    my_shapes = [w[n].shape for n in SMALL]
    s_rows = _rows_for(my_shapes, LANES)
    pk = lambda d: _pack_rows([d[n] for n in SMALL], s_rows, LANES)
    res = adamw(pk(w), pk(mom), pk(var), (_pack_rows(g_mine, s_rows, LANES),), 0, "adamw_small")
    for dst, pack in zip((out_g, out_d, out_m, out_v), res):
        for n, t in zip(SMALL, _unpack_rows(pack, my_shapes)):
            dst[n] = t

    return (loss, grad_x[None], *[out_g[n] for n in WEIGHTS], *[out_d[n] for n in WEIGHTS],
            *[out_m[n] for n in WEIGHTS], *[out_v[n] for n in WEIGHTS])
```

```python
import functools
import math

import jax
import jax.numpy as jnp
from jax import lax
from jax.experimental import pallas as pl
from jax.experimental.pallas import tpu as pltpu

F32 = jnp.float32
BF16 = jnp.bfloat16

D_MODEL = 1024
DEPTH = 4
CHUNK = 64
A_HEADS = 8
A_DK = 128
A_W = 1024
A_CONV = 5
B_HEADS = 4
B_DK = 128
B_DV = 256
B_RANK = 16
B_TAU = 16.0
B_KW = 512
B_VW = 1024
ALPHA = (2 * DEPTH) ** 0.25
LN_EPS = 1e-5
RMS_EPS = 1e-6
L2_EPS = 1e-6
ADAM_LR = 0.001
ADAM_B1 = 0.9
ADAM_B2 = 0.999
ADAM_EPS = 1e-08
ADAM_WD = 0.01
ADAM_STEP = 10
LANES = 128
NEG_INF = float("-inf")
VMEM_LIMIT = 56 * 1024 * 1024


def _cparams(sem=None):
    return pltpu.CompilerParams(dimension_semantics=sem, vmem_limit_bytes=VMEM_LIMIT)


def _dg(a, b, ca, cb):
    return lax.dot_general(a.astype(BF16), b.astype(BF16), (((ca,), (cb,)), ((), ())),
                           preferred_element_type=F32)


def _split(x):
    hi = x.astype(BF16)
    return hi, (x - hi.astype(F32)).astype(BF16)


def _dg3(a, b, ca, cb):
    (a1, a2), (b1, b2) = _split(a), _split(b)
    return (_dg(a1, b2, ca, cb) + _dg(a2, b1, ca, cb)) + _dg(a1, b1, ca, cb)


def _dot_with_vjp(dg):
    @functools.partial(jax.custom_vjp, nondiff_argnums=(2, 3))
    def dot(a, b, ca, cb):
        return dg(a, b, ca, cb)

    def fwd(a, b, ca, cb):
        return dg(a, b, ca, cb), (a, b)

    def bwd(ca, cb, res, g):
        a, b = res
        da = dg(g, b, 1, 1 - cb) if ca == 1 else dg(b, g, 1 - cb, 1)
        db = dg(a, g, 1 - ca, 0) if cb == 0 else dg(g, a, 0, 1 - ca)
        return da, db

    dot.defvjp(fwd, bwd)
    return dot


bdot = _dot_with_vjp(_dg)
xdot3 = _dot_with_vjp(_dg3)


def nn(a, b):
    return bdot(a, b, 1, 0)


def nt(a, b):
    return bdot(a, b, 1, 1)


def tn(a, b):
    return bdot(a, b, 0, 0)


def xdot(a, b):
    return xdot3(a, b, 1, 0)


def _sigmoid(x):
    return 1.0 / (1.0 + jnp.exp(-x))


def _softplus(x):
    return jnp.maximum(x, 0.0) + jnp.log(1.0 + jnp.exp(-jnp.abs(x)))


def _chunk_masks(rev):
    ii = lax.broadcasted_iota(jnp.int32, (CHUNK, CHUNK), 0)
    jj = lax.broadcasted_iota(jnp.int32, (CHUNK, CHUNK), 1)
    d = (ii - jj) * (1 - 2 * rev)
    return d >= 0, d > 0, ii == jj, (ii >> 3) == (jj >> 3)


def _each(f, *lists):
    return [f(*xs) for xs in zip(*lists)]


@jax.custom_vjp
def _unit_triangular_inverse(a, ident, blockdiag):
    return _unit_triangular_inverse_impl(a, ident, blockdiag)


def _unit_triangular_inverse_fwd(a, ident, blockdiag):
    t = _unit_triangular_inverse_impl(a, ident, blockdiag)
    return t, (t, ident, blockdiag)


def _unit_triangular_inverse_bwd(res, g):
    t, ident, blockdiag = res
    left = _each(lambda x, y: xdot3(x, y, 0, 0), t, g)
    da = _each(lambda x, y: -xdot3(x, y, 1, 1), left, t)
    return da, jnp.zeros_like(ident), jnp.zeros_like(blockdiag)


_unit_triangular_inverse.defvjp(_unit_triangular_inverse_fwd, _unit_triangular_inverse_bwd)


@jax.custom_vjp
def _known_inverse(a, t):
    return t


def _known_inverse_bwd(t, g):
    left = _each(lambda x, y: xdot3(x, y, 0, 0), t, g)
    return _each(lambda x, y: -xdot3(x, y, 1, 1), left, t), _each(jnp.zeros_like, t)


_known_inverse.defvjp(lambda a, t: (t, t), _known_inverse_bwd)


def _unit_triangular_inverse_impl(a, ident, blockdiag):
    ad = _each(lambda x: x * blockdiag, a)
    e = _each(lambda x, y: x - y, a, ad)
    dinv = _each(lambda x: ident - x, ad)
    p = _each(xdot, ad, ad)
    dinv = _each(lambda x, y: x + xdot(x, y), dinv, p)
    p = _each(xdot, p, p)
    dinv = _each(lambda x, y: x + xdot(x, y), dinv, p)
    g = _each(lambda x, y: -xdot(x, y), dinv, e)
    finv = _each(lambda x: ident + x, g)
    p = _each(xdot, g, g)
    finv = _each(lambda x, y: x + xdot(x, y), finv, p)
    p = _each(xdot, p, p)
    finv = _each(lambda x, y: x + xdot(x, y), finv, p)
    return _each(xdot, finv, dinv)


def _gdn_step(state, q, k, v, bb, gb, rev, t_saved=None):
    causal, strict, eye, blockdiag = _chunk_masks(rev)
    lower = causal.astype(F32)
    ones = jnp.ones((CHUNK, CHUNK), F32)
    gcb = _each(lambda x: xdot(lower, x), gb)
    gcol = _each(lambda x: x[:, :CHUNK], gcb)
    grow = _each(lambda x: xdot(ones, jnp.where(eye, x, 0.0)), gcol)
    decay = _each(lambda x, y: jnp.exp(jnp.where(causal, x - y, NEG_INF)), gcol, grow)
    kb = _each(lambda x, y: x * y, k, bb)
    a = _each(lambda x, y, z: jnp.where(strict, nt(x, y) * z, 0.0), kb, k, decay)
    if t_saved is None:
        t = _unit_triangular_inverse(a, eye.astype(F32), blockdiag.astype(F32))
    else:
        t = _known_inverse(a, t_saved)
    egc = _each(jnp.exp, gcb)
    u = _each(lambda x, y, z: xdot(x, y * z), t, v, bb)
    w = _each(lambda x, y, z: xdot(x, y * z), t, kb, egc)
    qk = _each(lambda x, y, z: nt(x, y) * z, q, k, decay)
    glast = _each(lambda x: jnp.sum(x, axis=0, keepdims=True), gb)
    v_new = _each(lambda x, y, z: x - nn(y, z), u, w, state)
    o = _each(lambda x, y, z, p, r: nn(x * y, z) + nn(p, r), q, egc, state, qk, v_new)
    k_dec = _each(lambda x, y, z: x * jnp.exp(y - z), k, glast, gcb)
    state_new = _each(lambda x, y, z, p: x * jnp.exp(y) + tn(z, p), state, glast, k_dec, v_new)
    return state_new, o, t


def _gla_step(state_t, q, k, v, la, rev):
    causal, _, _, _ = _chunk_masks(rev)
    lower = causal.astype(F32)
    sign = 1 - 2 * rev
    b = _each(lambda x: xdot(lower, x), la)
    q = _each(lambda x: x * (B_DK ** -0.5), q)
    row = lax.broadcasted_iota(jnp.int32, (CHUNK, B_DK), 0)
    sub = row // GLA_SUB
    scores = None
    for blk in range(CHUNK // GLA_SUB):
        r_at = jnp.where(rev == 1, GLA_SUB * (blk + 1), GLA_SUB * blk - 1)
        r = _each(lambda x: jnp.sum(jnp.where(row == r_at, x, 0.0), axis=0, keepdims=True), b)
        q_blk = _each(lambda x, y, z: x * jnp.exp(jnp.where(sub == blk, y - z, NEG_INF)), q, b, r)
        k_past = _each(lambda x, y, z: x * jnp.exp(jnp.where((sub - blk) * sign < 0, z - y, NEG_INF)), k, b, r)
        part = _each(lambda x, y: xdot3(x, y, 1, 1), q_blk, k_past)
        scores = part if scores is None else _each(lambda x, y: x + y, scores, part)
    shp = (GLA_SUB, GLA_SUB, B_DK)
    d3 = (lax.broadcasted_iota(jnp.int32, shp, 0) - lax.broadcasted_iota(jnp.int32, shp, 1)) * sign
    place_r = lax.broadcasted_iota(jnp.int32, (GLA_SUB, CHUNK), 0)
    place_c = lax.broadcasted_iota(jnp.int32, (GLA_SUB, CHUNK), 1)
    diag = []
    for blk in range(CHUNK // GLA_SUB):
        rows = slice(blk * GLA_SUB, (blk + 1) * GLA_SUB)
        place = (place_c == place_r + blk * GLA_SUB).astype(F32)

        def pairs(qh, kh, bh):
            qb, kb, bb = qh[rows], kh[rows], bh[rows]
            dec = jnp.exp(jnp.where(d3 >= 0, bb[:, None, :] - bb[None, :, :], NEG_INF))
            return xdot(jnp.sum(qb[:, None, :] * kb[None, :, :] * dec, axis=-1), place)

        diag.append(_each(pairs, q, k, b))
    scores = _each(lambda x, *d: x + jnp.concatenate(d, axis=0), scores, *diag)
    blast = _each(lambda x: jnp.sum(x, axis=0, keepdims=True), la)
    o = _each(lambda x, y, z, s, w: nt(x * jnp.exp(y), z) + nn(s, w), q, b, state_t, scores, v)
    k_dec = _each(lambda x, y, z: x * jnp.exp(y - z), k, blast, b)
    state_new = _each(lambda x, y, z, w: jnp.exp(x) * y + tn(z, w), blast, state_t, v, k_dec)
    return state_new, o


def _chunk_pos(d, m, n):
    return m + d * (n - 1 - 2 * m)


GLA_SUB = 16
GDN_HEADS_PER_STEP = 8
def gdn_rec_fwd(q, k, v, beta_b, g_b):
    s = q.shape[0]
    n = s // CHUNK

    hb = GDN_HEADS_PER_STEP
    wide = hb * LANES

    def body(q_ref, k_ref, v_ref, bb_ref, gb_ref, o_ref, st_ref, t_ref, state):
        d = pl.program_id(0)

        @pl.when(pl.program_id(2) == 0)
        def _():
            state[...] = jnp.zeros_like(state)

        cols = [slice(hh * LANES, (hh + 1) * LANES) for hh in range(hb)]
        st = [state[hh] for hh in range(hb)]
        new, o, t = _gdn_step(st, *([r[:, c] for c in cols] for r in (q_ref, k_ref, v_ref, bb_ref, gb_ref)), d)
        for hh, c in enumerate(cols):
            st_ref[hh] = st[hh]
            t_ref[hh] = t[hh]
            state[hh] = new[hh]
            o_ref[:, c] = o[hh]

    blk = pl.BlockSpec((CHUNK, wide), lambda d, h, m: (_chunk_pos(d, m, n), h))
    gate = pl.BlockSpec((CHUNK, wide), lambda d, h, m: (_chunk_pos(d, m, n), d * (A_HEADS // hb) + h))
    return pl.pallas_call(
        body, name="gdn_rec_fwd", grid=(2, A_HEADS // hb, n),
        in_specs=[blk, blk, blk, gate, gate],
        out_specs=[pl.BlockSpec((None, CHUNK, wide), lambda d, h, m: (d, _chunk_pos(d, m, n), h)),
                   pl.BlockSpec((None, hb, None, A_DK, LANES), lambda d, h, m: (d, h, _chunk_pos(d, m, n), 0, 0)),
                   pl.BlockSpec((None, hb, None, CHUNK, CHUNK), lambda d, h, m: (d, h, _chunk_pos(d, m, n), 0, 0))],
        out_shape=[jax.ShapeDtypeStruct((2, s, A_W), F32), jax.ShapeDtypeStruct((2, A_HEADS, n, A_DK, LANES), F32),
                   jax.ShapeDtypeStruct((2, A_HEADS, n, CHUNK, CHUNK), F32)],
        scratch_shapes=[pltpu.VMEM((hb, A_DK, LANES), F32)],
        compiler_params=_cparams(("arbitrary", "arbitrary", "arbitrary")),
    )(q, k, v, beta_b, g_b)


def gdn_rec_bwd(q, k, v, beta_b, g_b, states, tinv, do):
    s = q.shape[0]
    n = s // CHUNK

    hb = GDN_HEADS_PER_STEP
    wide = hb * LANES

    def body(q_ref, k_ref, v_ref, bb_ref, gb_ref, st_ref, t_ref, do_ref, dq_ref, dk_ref, dv_ref, dbb_ref, dgb_ref, dstate):
        d = pl.program_id(0)

        @pl.when(pl.program_id(2) == 0)
        def _():
            dstate[...] = jnp.zeros_like(dstate)

        def step(*a):
            return _gdn_step(*a, d, t_saved=[t_ref[hh] for hh in range(hb)])[:2]

        cols = [slice(hh * LANES, (hh + 1) * LANES) for hh in range(hb)]
        _, vjp = jax.vjp(step, [st_ref[hh] for hh in range(hb)],
                         *([r[:, c] for c in cols] for r in (q_ref, k_ref, v_ref, bb_ref, gb_ref)))
        grads = vjp(([dstate[hh] for hh in range(hb)], [do_ref[:, c] for c in cols]))
        for hh, c in enumerate(cols):
            dstate[hh], dq_ref[:, c], dk_ref[:, c], dv_ref[:, c], dbb_ref[:, c], dgb_ref[:, c] = (g[hh] for g in grads)

    pos = lambda d, m: _chunk_pos(1 - d, m, n)
    blk = pl.BlockSpec((CHUNK, wide), lambda d, h, m: (pos(d, m), h))
    gate = pl.BlockSpec((CHUNK, wide), lambda d, h, m: (pos(d, m), d * (A_HEADS // hb) + h))
    oblk = pl.BlockSpec((None, CHUNK, wide), lambda d, h, m: (d, pos(d, m), h))
    return pl.pallas_call(
        body, name="gdn_rec_bwd", grid=(2, A_HEADS // hb, n),
        in_specs=[blk, blk, blk, gate, gate,
                  pl.BlockSpec((None, hb, None, A_DK, LANES), lambda d, h, m: (d, h, pos(d, m), 0, 0)),
                  pl.BlockSpec((None, hb, None, CHUNK, CHUNK), lambda d, h, m: (d, h, pos(d, m), 0, 0)), blk],
        out_specs=[oblk, oblk, oblk, gate, gate],
        out_shape=[jax.ShapeDtypeStruct((2, s, A_W), F32)] * 3 + [jax.ShapeDtypeStruct(beta_b.shape, F32)] * 2,
        scratch_shapes=[pltpu.VMEM((hb, A_DK, LANES), F32)],
        compiler_params=_cparams(("arbitrary", "arbitrary", "arbitrary")),
    )(q, k, v, beta_b, g_b, states, tinv, do)


def gla_rec_fwd(proj, log_a):
    s = proj.shape[0]
    n = s // CHUNK

    kcols = [slice(h * B_DK, (h + 1) * B_DK) for h in range(B_HEADS)]
    vcols = [slice(h * B_DV, (h + 1) * B_DV) for h in range(B_HEADS)]

    def body(q_ref, k_ref, v_ref, la_ref, o_ref, st_ref, state):
        d = pl.program_id(0)

        @pl.when(pl.program_id(1) == 0)
        def _():
            state[...] = jnp.zeros_like(state)

        st = [state[h] for h in range(B_HEADS)]
        new, o = _gla_step(st, [q_ref[:, c] for c in kcols], [k_ref[:, c] for c in kcols], [v_ref[:, c] for c in vcols],
                           [la_ref[:, c] for c in kcols], d)
        for h in range(B_HEADS):
            st_ref[h] = st[h]
            state[h] = new[h]
            o_ref[:, vcols[h]] = o[h]

    pos = lambda d, m: _chunk_pos(d, m, n)
    return pl.pallas_call(
        body, name="gla_rec_fwd", grid=(2, n),
        in_specs=[pl.BlockSpec((CHUNK, B_KW), lambda d, m: (pos(d, m), 0)),
                  pl.BlockSpec((CHUNK, B_KW), lambda d, m: (pos(d, m), 1)),
                  pl.BlockSpec((CHUNK, B_VW), lambda d, m: (pos(d, m), 2 * B_KW // B_VW)),
                  pl.BlockSpec((None, CHUNK, B_KW), lambda d, m: (d, pos(d, m), 0))],
        out_specs=[pl.BlockSpec((None, CHUNK, B_VW), lambda d, m: (d, pos(d, m), 0)),
                   pl.BlockSpec((None, B_HEADS, None, B_DV, B_DK), lambda d, m: (d, 0, pos(d, m), 0, 0))],
        out_shape=[jax.ShapeDtypeStruct((2, s, B_VW), F32), jax.ShapeDtypeStruct((2, B_HEADS, n, B_DV, B_DK), F32)],
        scratch_shapes=[pltpu.VMEM((B_HEADS, B_DV, B_DK), F32)],
        compiler_params=_cparams(("arbitrary", "arbitrary")),
    )(proj, proj, proj, log_a)


def gla_rec_bwd(proj, log_a, states, do):
    s = proj.shape[0]
    n = s // CHUNK

    kcols = [slice(h * B_DK, (h + 1) * B_DK) for h in range(B_HEADS)]
    vcols = [slice(h * B_DV, (h + 1) * B_DV) for h in range(B_HEADS)]

    def body(q_ref, k_ref, v_ref, la_ref, st_ref, do_ref, dq_ref, dk_ref, dv_ref, dla_ref, dstate):
        d = pl.program_id(0)

        @pl.when(pl.program_id(1) == 0)
        def _():
            dstate[...] = jnp.zeros_like(dstate)

        step = functools.partial(_gla_step, rev=d)
        _, vjp = jax.vjp(step, [st_ref[h] for h in range(B_HEADS)], [q_ref[:, c] for c in kcols],
                         [k_ref[:, c] for c in kcols], [v_ref[:, c] for c in vcols], [la_ref[:, c] for c in kcols])
        dst, dq, dk, dv, dla = vjp(([dstate[h] for h in range(B_HEADS)], [do_ref[:, c] for c in vcols]))
        for h in range(B_HEADS):
            dstate[h] = dst[h]
            dq_ref[:, kcols[h]] = dq[h]
            dk_ref[:, kcols[h]] = dk[h]
            dv_ref[:, vcols[h]] = dv[h]
            dla_ref[:, kcols[h]] = dla[h]

    pos = lambda d, m: _chunk_pos(1 - d, m, n)
    kblk = pl.BlockSpec((None, CHUNK, B_KW), lambda d, m: (d, pos(d, m), 0))
    return pl.pallas_call(
        body, name="gla_rec_bwd", grid=(2, n),
        in_specs=[pl.BlockSpec((CHUNK, B_KW), lambda d, m: (pos(d, m), 0)),
                  pl.BlockSpec((CHUNK, B_KW), lambda d, m: (pos(d, m), 1)),
                  pl.BlockSpec((CHUNK, B_VW), lambda d, m: (pos(d, m), 2 * B_KW // B_VW)),
                  kblk,
                  pl.BlockSpec((None, B_HEADS, None, B_DV, B_DK), lambda d, m: (d, 0, pos(d, m), 0, 0)),
                  pl.BlockSpec((CHUNK, B_VW), lambda d, m: (pos(d, m), 0))],
        out_specs=[kblk, kblk, pl.BlockSpec((None, CHUNK, B_VW), lambda d, m: (d, pos(d, m), 0)), kblk],
        out_shape=[jax.ShapeDtypeStruct((2, s, B_KW), F32), jax.ShapeDtypeStruct((2, s, B_KW), F32),
                   jax.ShapeDtypeStruct((2, s, B_VW), F32), jax.ShapeDtypeStruct((2, s, B_KW), F32)],
        scratch_shapes=[pltpu.VMEM((B_HEADS, B_DV, B_DK), F32)],
        compiler_params=_cparams(("arbitrary", "arbitrary")),
    )(proj, proj, proj, log_a, states, do)


MM_TILE_OUT = 1024
MM_TILE_K = 1024


def _tile(n, pref):
    return pref if n % pref == 0 else n


class Gathered:
    def __init__(self, g, off, kind):
        assert off % D_MODEL == 0 and MM_TILE_OUT == D_MODEL and MM_TILE_K == D_MODEL
        self.g, self.blk, self.kind = g, off // D_MODEL, kind
        self.shape = (D_MODEL, 4 * D_MODEL) if kind == "cols" else (4 * D_MODEL, D_MODEL)

    def spec(self, mode):
        blk = self.blk
        chip_is_k = (self.kind == "rows") == (mode == "nn")
        if chip_is_k:
            return pl.BlockSpec((None, D_MODEL, D_MODEL), lambda i, j, k: (k, blk, 0))
        return pl.BlockSpec((None, D_MODEL, D_MODEL), lambda i, j, k: (j, blk, 0))


def mm(a, b, mode="nn", act=None, epi=None, extra=None, alpha=1.0, chip_major=False, pack=None, name="mm"):
    if mode == "tn":
        kk, m = a.shape
    else:
        m, kk = a.shape
    nn_ = b.shape[0] if mode == "nt" else b.shape[1]
    tm, tn_, tk = _tile(m, MM_TILE_OUT), _tile(nn_, MM_TILE_OUT), _tile(kk, MM_TILE_K)
    nk = kk // tk
    ca, cb = {"nn": (1, 0), "nt": (1, 1), "tn": (0, 0)}[mode]

    def body(*refs):
        o_ref = refs[-1]
        a_ref, b_ref = refs[:2]
        if epi is not None:
            e_ref = refs[2]
        kstep = pl.program_id(2)
        at = a_ref[...]
        if act == "sqrelu":
            at = jnp.square(jnp.maximum(at, 0.0))
        part = _dg(at, b_ref[...], ca, cb)

        @pl.when(kstep == 0)
        def _():
            o_ref[...] = part

        @pl.when(kstep > 0)
        def _():
            o_ref[...] += part

        if epi is not None:
            @pl.when(kstep == nk - 1)
            def _():
                if epi == "dsqrelu":
                    o_ref[...] = o_ref[...] * (2.0 * jnp.maximum(e_ref[...], 0.0))
                else:
                    o_ref[...] = o_ref[...] + alpha * e_ref[...]

    a_spec = pl.BlockSpec((tk, tm), lambda i, j, k: (k, i)) if mode == "tn" else pl.BlockSpec((tm, tk), lambda i, j, k: (i, k))
    if isinstance(b, Gathered):
        assert mode in ("nn", "nt") and tn_ == D_MODEL and tk == D_MODEL
        b_spec, b = b.spec(mode), b.g
    elif mode == "nt":
        b_spec = pl.BlockSpec((tn_, tk), lambda i, j, k: (j, k))
    else:
        b_spec = pl.BlockSpec((tk, tn_), lambda i, j, k: (k, j))
    o_spec = pl.BlockSpec((tm, tn_), lambda i, j, k: (i, j))
    ins, specs = [a, b], [a_spec, b_spec]
    if epi is not None:
        ins.append(extra)
        specs.append(o_spec)
    out_shape = jax.ShapeDtypeStruct((m, nn_), F32)
    if chip_major:
        assert nn_ == 4 * D_MODEL and tn_ == D_MODEL
        o_spec = pl.BlockSpec((None, tm, D_MODEL), lambda i, j, k: (j, i, 0))
        out_shape = jax.ShapeDtypeStruct((4, m, D_MODEL), F32)
    aliases = {}
    if pack is not None:
        buf, rows, blk, kind = pack
        logical = (D_MODEL, 4 * D_MODEL) if kind == "cols" else (4 * D_MODEL, D_MODEL)
        assert epi is None and tm == D_MODEL and tn_ == D_MODEL and (m, nn_) == logical
        if kind == "cols":
            o_spec = pl.BlockSpec((None, D_MODEL, D_MODEL), lambda i, j, k: (j, blk, 0))
        else:
            o_spec = pl.BlockSpec((None, D_MODEL, D_MODEL), lambda i, j, k: (i, blk, 0))
        out_shape = jax.ShapeDtypeStruct((4, rows, D_MODEL), F32)
        if buf is not None:
            aliases = {len(ins): 0}
            ins.append(buf)
            specs.append(pl.BlockSpec(memory_space=pl.ANY))
    return pl.pallas_call(
        body, name=name, grid=(m // tm, nn_ // tn_, nk), in_specs=specs, out_specs=o_spec, out_shape=out_shape,
        input_output_aliases=aliases, compiler_params=_cparams(("parallel", "parallel", "arbitrary")),
    )(*ins)


ROWS = 256
POST_ROWS = 1024


def _ln_core(x, m, g, b):
    r = ALPHA * x + m
    mu = jnp.mean(r, axis=-1, keepdims=True)
    xc = r - mu
    var = jnp.mean(xc * xc, axis=-1, keepdims=True)
    rstd = lax.rsqrt(var + LN_EPS)
    xhat = xc * rstd
    return xhat * g + b, xhat, rstd


def ln_fwd(x, m, g, b):
    s, dm = x.shape

    def body(x_ref, m_ref, g_ref, b_ref, o_ref):
        o_ref[...] = _ln_core(x_ref[...], m_ref[...], g_ref[...], b_ref[...])[0]

    row = pl.BlockSpec((ROWS, dm), lambda i: (i, 0))
    vec = pl.BlockSpec((1, dm), lambda i: (0, 0))
    return pl.pallas_call(body, name="ln_fwd", grid=(s // ROWS,), in_specs=[row, row, vec, vec], out_specs=row,
                          out_shape=jax.ShapeDtypeStruct((s, dm), F32), compiler_params=_cparams(("parallel",)))(x, m, g, b)


def ln_bwd(x, m, g, dy):
    s, dm = x.shape

    def body(x_ref, m_ref, g_ref, dy_ref, dr_ref, dg_ref, db_ref):
        gv = g_ref[...]
        _, xhat, rstd = _ln_core(x_ref[...], m_ref[...], gv, jnp.zeros_like(gv))
        dy = dy_ref[...]
        dxh = dy * gv
        dr_ref[...] = rstd * (dxh - jnp.mean(dxh, axis=-1, keepdims=True)
                              - xhat * jnp.mean(dxh * xhat, axis=-1, keepdims=True))

        @pl.when(pl.program_id(0) == 0)
        def _():
            dg_ref[...] = jnp.zeros_like(dg_ref)
            db_ref[...] = jnp.zeros_like(db_ref)

        dg_ref[...] += jnp.sum(dy * xhat, axis=0, keepdims=True)
        db_ref[...] += jnp.sum(dy, axis=0, keepdims=True)

    row = pl.BlockSpec((ROWS, dm), lambda i: (i, 0))
    vec = pl.BlockSpec((1, dm), lambda i: (0, 0))
    return pl.pallas_call(body, name="ln_bwd", grid=(s // ROWS,), in_specs=[row, row, vec, row], out_specs=[row, vec, vec],
                          out_shape=[jax.ShapeDtypeStruct((s, dm), F32), jax.ShapeDtypeStruct((1, dm), F32),
                                     jax.ShapeDtypeStruct((1, dm), F32)],
                          compiler_params=_cparams(("arbitrary",)))(x, m, g, dy)


def loss_head(y, target):
    s, dm = y.shape

    def body(y_ref, t_ref, dy_ref, l_ref):
        e = y_ref[...] - t_ref[...]
        dy_ref[...] = e * (1.0 / dm)

        @pl.when(pl.program_id(0) == 0)
        def _():
            l_ref[...] = jnp.zeros_like(l_ref)

        col = jnp.sum(e * e, axis=0, keepdims=True) * (0.5 / dm)
        acc = col[:, :LANES]
        for c in range(1, dm // LANES):
            acc = acc + col[:, c * LANES:(c + 1) * LANES]
        l_ref[...] += acc

    row = pl.BlockSpec((ROWS, dm), lambda i: (i, 0))
    return pl.pallas_call(body, name="loss_head", grid=(s // ROWS,), in_specs=[row, row],
                          out_specs=[row, pl.BlockSpec((1, LANES), lambda i: (0, 0))],
                          out_shape=[jax.ShapeDtypeStruct((s, dm), F32), jax.ShapeDtypeStruct((1, LANES), F32)],
                          compiler_params=_cparams(("arbitrary",)))(y, target)


def _shift_rows_impl(x, d):
    n = x.shape[0]
    if d == 0:
        return x
    t = lax.broadcasted_iota(jnp.int32, x.shape, 0)
    return jnp.where((t + d >= 0) & (t + d < n), pltpu.roll(x, (-d) % n, 0), 0.0)


@functools.partial(jax.custom_vjp, nondiff_argnums=(1,))
def _shift_rows(x, d):
    return _shift_rows_impl(x, d)


_shift_rows.defvjp(lambda x, d: (_shift_rows_impl(x, d), None), lambda d, _, g: (_shift_rows_impl(g, -d),))


def _gdn_pre_fn(u, w, kind):
    rows = lax.broadcasted_iota(jnp.int32, w.shape, 0)
    c = None
    for tap in range(A_CONV):
        w_tap = jnp.sum(jnp.where(rows == tap, w, 0.0), axis=0, keepdims=True)
        term = _shift_rows(u, tap - A_CONV // 2) * w_tap
        c = term if c is None else c + term
    y = c * _sigmoid(c)
    if kind == "v":
        return y
    y = y * lax.rsqrt(jnp.sum(y * y, axis=-1, keepdims=True) + L2_EPS)
    return y * (A_DK ** -0.5) if kind == "q" else y


_KIND_OFF = {"q": 0, "k": A_HEADS, "v": 2 * A_HEADS}


def gdn_pre(proj, conv_w, kind):
    s = proj.shape[0]
    off = _KIND_OFF[kind]

    def body(u_ref, w_ref, o_ref):
        o_ref[...] = _gdn_pre_fn(u_ref[...], w_ref[...], kind)

    return pl.pallas_call(
        body, name="gdn_pre_" + kind, grid=(A_HEADS,),
        in_specs=[pl.BlockSpec((s, LANES), lambda h: (0, off + h)), pl.BlockSpec((A_CONV, LANES), lambda h: (0, off + h))],
        out_specs=pl.BlockSpec((s, LANES), lambda h: (0, h)),
        out_shape=jax.ShapeDtypeStruct((s, A_W), F32), compiler_params=_cparams(("parallel",)))(proj, conv_w)


def gdn_pre_bwd(proj, conv_w, dt2, kind):
    s = proj.shape[0]
    off = _KIND_OFF[kind]

    def body(u_ref, w_ref, d0_ref, d1_ref, du_ref, dw_ref):
        _, vjp = jax.vjp(functools.partial(_gdn_pre_fn, kind=kind), u_ref[...], w_ref[...])
        du, dw = vjp(d0_ref[...] + d1_ref[...])
        du_ref[...] = du
        dw_ref[...] = dw

    return pl.pallas_call(
        body, name="gdn_pre_bwd_" + kind, grid=(A_HEADS,),
        in_specs=[pl.BlockSpec((s, LANES), lambda h: (0, off + h)), pl.BlockSpec((A_CONV, LANES), lambda h: (0, off + h)),
                  pl.BlockSpec((None, s, LANES), lambda h: (0, 0, h)), pl.BlockSpec((None, s, LANES), lambda h: (1, 0, h))],
        out_specs=[pl.BlockSpec((s, LANES), lambda h: (0, h)), pl.BlockSpec((A_CONV, LANES), lambda h: (0, h))],
        out_shape=[jax.ShapeDtypeStruct((s, A_W), F32), jax.ShapeDtypeStruct((A_CONV, A_W), F32)],
        compiler_params=_cparams(("parallel",)))(proj, conv_w, dt2, dt2)


N_GATE = 2 * A_HEADS


def _gdn_gates_fn(ba, alog_row, dt_row):
    r = lax.broadcasted_iota(jnp.int32, (LANES, N_GATE * LANES), 0)
    c = lax.broadcasted_iota(jnp.int32, (LANES, N_GATE * LANES), 1) >> 7
    beta_b = xdot(_sigmoid(ba), (r == c).astype(F32))
    g = -(jnp.exp(alog_row) * _softplus(ba + dt_row))
    g_b = xdot(g, (r == c + N_GATE).astype(F32))
    return beta_b, g_b


def gdn_gates(ba, alog_row, dt_row):
    s = ba.shape[0]

    def body(ba_ref, al_ref, dt_ref, bb_ref, gb_ref):
        bb_ref[...], gb_ref[...] = _gdn_gates_fn(ba_ref[...], al_ref[...], dt_ref[...])

    row = pl.BlockSpec((ROWS, LANES), lambda i: (i, 0))
    vec = pl.BlockSpec((1, LANES), lambda i: (0, 0))
    wide = pl.BlockSpec((ROWS, N_GATE * LANES), lambda i: (i, 0))
    return pl.pallas_call(body, name="gdn_gates", grid=(s // ROWS,), in_specs=[row, vec, vec], out_specs=[wide, wide],
                          out_shape=[jax.ShapeDtypeStruct((s, N_GATE * LANES), F32)] * 2,
                          compiler_params=_cparams(("parallel",)))(ba, alog_row, dt_row)


def gdn_gates_bwd(ba, alog_row, dt_row, dbeta_b, dg_b):
    s = ba.shape[0]

    def body(ba_ref, al_ref, dt_ref, dbb_ref, dgb_ref, dba_ref, dal_ref, ddt_ref):
        _, vjp = jax.vjp(_gdn_gates_fn, ba_ref[...], al_ref[...], dt_ref[...])
        dba, dal, ddt = vjp((dbb_ref[...], dgb_ref[...]))
        dba_ref[...] = dba

        @pl.when(pl.program_id(0) == 0)
        def _():
            dal_ref[...] = jnp.zeros_like(dal_ref)
            ddt_ref[...] = jnp.zeros_like(ddt_ref)

        dal_ref[...] += dal
        ddt_ref[...] += ddt

    row = pl.BlockSpec((ROWS, LANES), lambda i: (i, 0))
    vec = pl.BlockSpec((1, LANES), lambda i: (0, 0))
    wide = pl.BlockSpec((ROWS, N_GATE * LANES), lambda i: (i, 0))
    return pl.pallas_call(body, name="gdn_gates_bwd", grid=(s // ROWS,), in_specs=[row, vec, vec, wide, wide],
                          out_specs=[row, vec, vec],
                          out_shape=[jax.ShapeDtypeStruct((s, LANES), F32), jax.ShapeDtypeStruct((1, LANES), F32),
                                     jax.ShapeDtypeStruct((1, LANES), F32)],
                          compiler_params=_cparams(("arbitrary",)))(ba, alog_row, dt_row, dbeta_b, dg_b)


def _post_fn(o, z, g):
    y = o * lax.rsqrt(jnp.mean(o * o, axis=-1, keepdims=True) + RMS_EPS) * g
    return y * (z * _sigmoid(z))


def mixer_post(o2, proj, norm_g, width, gate_off, name):
    s = o2.shape[1]
    nh = o2.shape[2] // width

    rows = _tile(s, POST_ROWS)

    def body(o0_ref, o1_ref, z_ref, g_ref, y_ref):
        y_ref[...] = _post_fn(o0_ref[...] + o1_ref[...], z_ref[...], g_ref[...])

    ospec = lambda d: pl.BlockSpec((None, rows, width), lambda i, h: (d, i, h))
    return pl.pallas_call(
        body, name=name, grid=(s // rows, nh),
        in_specs=[ospec(0), ospec(1), pl.BlockSpec((rows, width), lambda i, h: (i, gate_off + h)),
                  pl.BlockSpec((1, width), lambda i, h: (0, 0))],
        out_specs=pl.BlockSpec((rows, width), lambda i, h: (i, h)),
        out_shape=jax.ShapeDtypeStruct((s, o2.shape[2]), F32),
        compiler_params=_cparams(("parallel", "parallel")))(o2, o2, proj, norm_g)


def mixer_post_bwd(o2, proj, norm_g, dy, width, gate_off, name):
    s = o2.shape[1]
    nh = o2.shape[2] // width

    def body(o0_ref, o1_ref, z_ref, g_ref, dy_ref, do_ref, dz_ref, dg_ref):
        _, vjp = jax.vjp(_post_fn, o0_ref[...] + o1_ref[...], z_ref[...], g_ref[...])
        do, dz, dg = vjp(dy_ref[...])
        do_ref[...] = do
        dz_ref[...] = dz

        @pl.when((pl.program_id(0) == 0) & (pl.program_id(1) == 0))
        def _():
            dg_ref[...] = jnp.zeros_like(dg_ref)

        dg_ref[...] += dg

    rows = _tile(s, POST_ROWS)
    ospec = lambda d: pl.BlockSpec((None, rows, width), lambda i, h: (d, i, h))
    blk = pl.BlockSpec((rows, width), lambda i, h: (i, h))
    vec = pl.BlockSpec((1, width), lambda i, h: (0, 0))
    return pl.pallas_call(
        body, name=name, grid=(s // rows, nh),
        in_specs=[ospec(0), ospec(1), pl.BlockSpec((rows, width), lambda i, h: (i, gate_off + h)), vec, blk],
        out_specs=[blk, blk, vec],
        out_shape=[jax.ShapeDtypeStruct((s, o2.shape[2]), F32)] * 2 + [jax.ShapeDtypeStruct((1, width), F32)],
        compiler_params=_cparams(("arbitrary", "arbitrary")))(o2, o2, proj, norm_g, dy)


def _log_gate(z):
    return (jnp.minimum(z, 0.0) - jnp.log(1.0 + jnp.exp(-jnp.abs(z)))) * (1.0 / B_TAU)


def gla_gate(gl, w2, gb):
    s = gl.shape[0]

    def body(gl_ref, w_ref, b_ref, o_ref):
        for n in range(2):
            o_ref[n] = _log_gate(nn(gl_ref[...], w_ref[n]) + b_ref[n])

    full = lambda shp: pl.BlockSpec(shp, lambda i: (0,) * len(shp))
    return pl.pallas_call(
        body, name="gla_gate", grid=(s // ROWS,),
        in_specs=[pl.BlockSpec((ROWS, LANES), lambda i: (i, 0)), full(w2.shape), full(gb.shape)],
        out_specs=pl.BlockSpec((2, ROWS, B_KW), lambda i: (0, i, 0)),
        out_shape=jax.ShapeDtypeStruct((2, s, B_KW), F32), compiler_params=_cparams(("parallel",)))(gl, w2, gb)


def gla_gate_bwd(gl, w2, gb, dla):
    s = gl.shape[0]

    def body(gl_ref, w_ref, b_ref, dla_ref, dgl_ref, dz_ref, db0_ref, db1_ref):
        @pl.when(pl.program_id(0) == 0)
        def _():
            db0_ref[...] = jnp.zeros_like(db0_ref)
            db1_ref[...] = jnp.zeros_like(db1_ref)

        dgl = None
        for n, db_ref in enumerate((db0_ref, db1_ref)):
            _, vjp = jax.vjp(_log_gate, nn(gl_ref[...], w_ref[n]) + b_ref[n])
            dz, = vjp(dla_ref[n])
            dz_ref[n] = dz
            db_ref[...] += jnp.sum(dz, axis=0, keepdims=True)
            part = nt(dz, w_ref[n])
            dgl = part if dgl is None else dgl + part
        dgl_ref[...] = dgl

    full = lambda shp: pl.BlockSpec(shp, lambda i: (0,) * len(shp))
    row = pl.BlockSpec((ROWS, LANES), lambda i: (i, 0))
    wide = pl.BlockSpec((2, ROWS, B_KW), lambda i: (0, i, 0))
    vec = pl.BlockSpec((1, B_KW), lambda i: (0, 0))
    return pl.pallas_call(
        body, name="gla_gate_bwd", grid=(s // ROWS,),
        in_specs=[row, full(w2.shape), full(gb.shape), wide],
        out_specs=[row, wide, vec, vec],
        out_shape=[jax.ShapeDtypeStruct((s, LANES), F32), jax.ShapeDtypeStruct((2, s, B_KW), F32),
                   jax.ShapeDtypeStruct((1, B_KW), F32), jax.ShapeDtypeStruct((1, B_KW), F32)],
        compiler_params=_cparams(("arbitrary",)))(gl, w2, gb, dla)


PACK_TILE = 512


def cast_into_slot(x, chip):
    r, c = x.shape

    def body(chip_ref, x_ref, o_ref):
        o_ref[...] = x_ref[...].astype(BF16)

    return pl.pallas_call(
        body, name="cast_into_slot",
        grid_spec=pltpu.PrefetchScalarGridSpec(
            num_scalar_prefetch=1, grid=(r // PACK_TILE,),
            in_specs=[pl.BlockSpec((PACK_TILE, c), lambda i, chip_ref: (i, 0))],
            out_specs=pl.BlockSpec((None, PACK_TILE, c), lambda i, chip_ref: (chip_ref[0], i, 0))),
        out_shape=jax.ShapeDtypeStruct((4, r, c), BF16), compiler_params=_cparams(("parallel",)))(chip, x)


def sum_received(chip_sum, recv, chip, core):
    _, h, c = chip_sum.shape
    n = recv.shape[0]
    tr = _tile(h, PACK_TILE)
    nblk = h // tr

    def body(chip_ref, core_ref, own_ref, r_ref, o_ref):
        acc = r_ref[0].astype(F32)
        for k in range(1, n):
            acc = acc + r_ref[k].astype(F32)
        o_ref[...] = acc + own_ref[...].astype(F32)

    return pl.pallas_call(
        body, name="sum_received",
        grid_spec=pltpu.PrefetchScalarGridSpec(
            num_scalar_prefetch=2, grid=(nblk,),
            in_specs=[pl.BlockSpec((None, tr, c), lambda i, chip_ref, core_ref: (chip_ref[0], i, 0)),
                      pl.BlockSpec((n, tr, c), lambda i, chip_ref, core_ref: (0, i, 0))],
            out_specs=pl.BlockSpec((tr, c), lambda i, chip_ref, core_ref: (core_ref[0] * nblk + i, 0))),
        out_shape=jax.ShapeDtypeStruct((2 * h, c), F32), compiler_params=_cparams(("parallel",)))(chip, core, chip_sum, recv)


def merge_first_hop(chip_sum, passed, slot_x, slot_y, core):
    _, h, c = chip_sum.shape
    tr = _tile(h, PACK_TILE)

    def body(sx_ref, sy_ref, core_ref, to_x_ref, to_y_ref, p_ref, o_ref):
        p = p_ref[...].astype(F32)
        is_y = core_ref[0].astype(F32)
        o_ref[0] = (to_x_ref[...].astype(F32) + p * (1.0 - is_y)).astype(BF16)
        o_ref[1] = (to_y_ref[...].astype(F32) + p * is_y).astype(BF16)

    return pl.pallas_call(
        body, name="merge_first_hop",
        grid_spec=pltpu.PrefetchScalarGridSpec(
            num_scalar_prefetch=3, grid=(h // tr,),
            in_specs=[pl.BlockSpec((None, tr, c), lambda i, sx_ref, sy_ref, core_ref: (sx_ref[0], i, 0)),
                      pl.BlockSpec((None, tr, c), lambda i, sx_ref, sy_ref, core_ref: (sy_ref[0], i, 0)),
                      pl.BlockSpec((tr, c), lambda i, sx_ref, sy_ref, core_ref: (i, 0))],
            out_specs=pl.BlockSpec((2, tr, c), lambda i, sx_ref, sy_ref, core_ref: (0, i, 0))),
        out_shape=jax.ShapeDtypeStruct((2, h, c), BF16),
        compiler_params=_cparams(("parallel",)))(slot_x, slot_y, core, chip_sum, chip_sum, passed)


def sum_slots(x, name):
    n, r, c = x.shape
    tr = _tile(r, PACK_TILE)

    def body(x_ref, o_ref):
        acc = x_ref[0].astype(F32)
        for k in range(1, n):
            acc = acc + x_ref[k].astype(F32)
        o_ref[...] = acc

    return pl.pallas_call(body, name=name, grid=(r // tr,), in_specs=[pl.BlockSpec((n, tr, c), lambda i: (0, i, 0))],
                          out_specs=pl.BlockSpec((tr, c), lambda i: (i, 0)),
                          out_shape=jax.ShapeDtypeStruct((r, c), F32), compiler_params=_cparams(("parallel",)))(x)


def add_sibling_half(gpack, theirs, core):
    n, r, c = gpack.shape
    half_rows = r // 2
    tr = _tile(half_rows, PACK_TILE)
    nblk = half_rows // tr

    def body(core_ref, g_ref, t_ref, o_ref):
        o_ref[...] = (g_ref[...] + t_ref[...]).astype(BF16)

    blk = pl.BlockSpec((None, tr, c), lambda s, i, core_ref: (s, i, 0))
    return pl.pallas_call(
        body, name="add_sibling_half",
        grid_spec=pltpu.PrefetchScalarGridSpec(
            num_scalar_prefetch=1, grid=(n, nblk),
            in_specs=[pl.BlockSpec((None, tr, c), lambda s, i, core_ref: (s, core_ref[0] * nblk + i, 0)), blk],
            out_specs=blk),
        out_shape=jax.ShapeDtypeStruct((n, half_rows, c), BF16),
        compiler_params=_cparams(("parallel", "parallel")))(core, gpack, theirs)


def adamw(w, m, v, grads, g_row_off, name):
    r, c = w.shape
    tr = next(t for t in (PACK_TILE, r) if r % t == 0 and g_row_off % t == 0)
    ob = g_row_off // tr
    ng = len(grads)

    def body(*refs):
        w_ref, m_ref, v_ref = refs[:3]
        g_refs = refs[3:3 + ng]
        g_ref, d_ref, nm_ref, nv_ref = refs[3 + ng:]
        g = g_refs[0][...]
        for gr in g_refs[1:]:
            g = g + gr[...]
        m_new = ADAM_B1 * m_ref[...] + (1.0 - ADAM_B1) * g
        v_new = ADAM_B2 * v_ref[...] + (1.0 - ADAM_B2) * jnp.square(g)
        m_hat = m_new / (1.0 - ADAM_B1 ** ADAM_STEP)
        v_hat = v_new / (1.0 - ADAM_B2 ** ADAM_STEP)
        g_ref[...] = g
        d_ref[...] = -ADAM_LR * (m_hat / (jnp.sqrt(v_hat) + ADAM_EPS) + ADAM_WD * w_ref[...])
        nm_ref[...] = m_new
        nv_ref[...] = v_new

    blk = pl.BlockSpec((tr, c), lambda i: (i, 0))
    gblk = pl.BlockSpec((tr, c), lambda i: (i + ob, 0))
    return pl.pallas_call(body, name=name, grid=(r // tr,), in_specs=[blk, blk, blk] + [gblk] * ng, out_specs=[blk] * 4,
                          out_shape=[jax.ShapeDtypeStruct((r, c), F32)] * 4,
                          compiler_params=_cparams(("parallel",)))(w, m, v, *grads)


MESH = pl.DeviceIdType.MESH
HBM = pl.BlockSpec(memory_space=pl.ANY)
CHIP_FLIPS = ((1, 0), (0, 1), (1, 1))


def _place():
    return lax.axis_index("x"), lax.axis_index("y"), lax.axis_index("c")


def allgather_chips(buf):
    _, r, c = buf.shape
    half_rows = r // 2

    def body(_, out_ref, send_sems, recv_sems):
        x, y, cc = _place()
        half = pl.ds(cc * half_rows, half_rows)
        other = pl.ds((1 - cc) * half_rows, half_rows)

        def copy(k, rows, to):
            return pltpu.make_async_remote_copy(src_ref=rows, dst_ref=rows, send_sem=send_sems.at[k],
                                                recv_sem=recv_sems.at[k], device_id=to, device_id_type=MESH)

        nbr_x, nbr_y, diag = (1 - x, y), (x, 1 - y), (1 - x, 1 - y)
        slot = lambda chip: 2 * chip[0] + chip[1]
        sibling = (x, y, 1 - cc)
        first = [copy(0, out_ref.at[slot((x, y)), half], (*nbr_x, cc)), copy(1, out_ref.at[slot((x, y)), half], (*nbr_y, cc))]
        for cp in first:
            cp.start()
        passed = []
        for k, chip in enumerate((nbr_x, nbr_y)):
            landed = out_ref.at[slot(chip), half]
            copy(k, landed, (*chip, cc)).wait_recv()
            passed.append(copy(3 + k, landed, sibling))
            passed[-1].start()
        via = (1 - x + cc * (2 * x - 1), y + cc * (1 - 2 * y))
        to = (x + cc * (1 - 2 * x), 1 - y + cc * (2 * y - 1))
        hop = copy(2, out_ref.at[slot(via), half], (*to, cc))
        hop.start()
        landed = out_ref.at[slot(diag), half]
        copy(2, landed, (*to, cc)).wait_recv()
        passed.append(copy(5, landed, sibling))
        passed[-1].start()
        for k, chip in enumerate((nbr_x, nbr_y, diag)):
            copy(3 + k, out_ref.at[slot(chip), other], sibling).wait_recv()
        for cp in first + [hop] + passed:
            cp.wait_send()

    return pl.pallas_call(
        body, name="allgather_chips", in_specs=[HBM], out_specs=HBM, input_output_aliases={0: 0},
        out_shape=jax.ShapeDtypeStruct(buf.shape, buf.dtype),
        scratch_shapes=[pltpu.SemaphoreType.DMA((6,)), pltpu.SemaphoreType.DMA((6,))],
    )(buf)


def scatter_first_hop(gpack):
    _, r, c = gpack.shape

    def body(src_ref, out_ref, send_sem, recv_sem):
        x, y, cc = _place()
        to = (x + cc * (1 - 2 * x), 1 - y + cc * (2 * y - 1), cc)
        cp = pltpu.make_async_remote_copy(src_ref=src_ref.at[2 * (1 - x) + (1 - y)], dst_ref=out_ref, send_sem=send_sem,
                                          recv_sem=recv_sem, device_id=to, device_id_type=MESH)
        cp.start()
        cp.wait()

    return pl.pallas_call(
        body, name="scatter_first_hop", in_specs=[HBM], out_specs=HBM,
        out_shape=jax.ShapeDtypeStruct((r, c), gpack.dtype),
        scratch_shapes=[pltpu.SemaphoreType.DMA, pltpu.SemaphoreType.DMA],
    )(gpack)


def scatter_second_hop(to_nbrs):
    def body(src_ref, out_ref, send_sems, recv_sems):
        x, y, cc = _place()
        sends = [pltpu.make_async_remote_copy(src_ref=src_ref.at[k], dst_ref=out_ref.at[k], send_sem=send_sems.at[k],
                                              recv_sem=recv_sems.at[k], device_id=to, device_id_type=MESH)
                 for k, to in enumerate(((1 - x, y, cc), (x, 1 - y, cc)))]
        for cp in sends:
            cp.start()
        for cp in sends:
            cp.wait_recv()
        for cp in sends:
            cp.wait_send()

    return pl.pallas_call(
        body, name="scatter_second_hop", in_specs=[HBM], out_specs=HBM,
        out_shape=jax.ShapeDtypeStruct(to_nbrs.shape, to_nbrs.dtype),
        scratch_shapes=[pltpu.SemaphoreType.DMA((2,)), pltpu.SemaphoreType.DMA((2,))],
    )(to_nbrs)


def sibling_halves(gpack):
    n, r, c = gpack.shape
    half_rows = r // 2

    def body(src_ref, out_ref, send_sem, recv_sem):
        x, y, cc = _place()
        cp = pltpu.make_async_remote_copy(
            src_ref=src_ref.at[:, pl.ds((1 - cc) * half_rows, half_rows)], dst_ref=out_ref, send_sem=send_sem,
            recv_sem=recv_sem, device_id=(x, y, 1 - cc), device_id_type=MESH)
        cp.start()
        cp.wait()

    return pl.pallas_call(
        body, name="sibling_halves", in_specs=[HBM], out_specs=HBM,
        out_shape=jax.ShapeDtypeStruct((n, half_rows, c), gpack.dtype),
        scratch_shapes=[pltpu.SemaphoreType.DMA, pltpu.SemaphoreType.DMA],
    )(gpack)


def join_halves(buf):
    r, c = buf.shape
    half_rows = r // 2

    def body(_, out_ref, send_sem, recv_sem):
        x, y, cc = _place()
        half = out_ref.at[pl.ds(cc * half_rows, half_rows)]
        other = out_ref.at[pl.ds((1 - cc) * half_rows, half_rows)]
        send = pltpu.make_async_remote_copy(src_ref=half, dst_ref=half, send_sem=send_sem, recv_sem=recv_sem,
                                            device_id=(x, y, 1 - cc), device_id_type=MESH)
        send.start()
        pltpu.make_async_remote_copy(src_ref=other, dst_ref=other, send_sem=send_sem, recv_sem=recv_sem,
                                     device_id=(x, y, 1 - cc), device_id_type=MESH).wait_recv()
        send.wait_send()

    return pl.pallas_call(
        body, name="join_halves", in_specs=[HBM], out_specs=HBM, input_output_aliases={0: 0},
        out_shape=jax.ShapeDtypeStruct(buf.shape, buf.dtype),
        scratch_shapes=[pltpu.SemaphoreType.DMA, pltpu.SemaphoreType.DMA],
    )(buf)


def exchange_all(v, name):
    r, c = v.shape

    def body(v_ref, out_ref, send_sems, recv_sems):
        x, y, cc = _place()
        out_ref[4 * x + 2 * y + cc] = v_ref[...]
        sends, recvs = [], []
        for k in range(1, 8):
            px = 1 - x if k & 4 else x
            py = 1 - y if k & 2 else y
            pc = 1 - cc if k & 1 else cc
            sends.append(pltpu.make_async_remote_copy(
                src_ref=v_ref, dst_ref=out_ref.at[4 * x + 2 * y + cc], send_sem=send_sems.at[k - 1],
                recv_sem=recv_sems.at[k - 1], device_id=(px, py, pc), device_id_type=MESH))
            recvs.append(pltpu.make_async_remote_copy(
                src_ref=v_ref, dst_ref=out_ref.at[4 * px + 2 * py + pc], send_sem=send_sems.at[k - 1],
                recv_sem=recv_sems.at[k - 1], device_id=(px, py, pc), device_id_type=MESH))
        for cp in sends:
            cp.start()
        for cp in recvs:
            cp.wait_recv()
        for cp in sends:
            cp.wait_send()

    vm = pl.BlockSpec(memory_space=pltpu.VMEM)
    return pl.pallas_call(
        body, name=name, in_specs=[vm], out_specs=vm, out_shape=jax.ShapeDtypeStruct((8, r, c), v.dtype),
        scratch_shapes=[pltpu.SemaphoreType.DMA((7,)), pltpu.SemaphoreType.DMA((7,))],
        compiler_params=pltpu.CompilerParams(vmem_limit_bytes=VMEM_LIMIT),
    )(v)


def _as_rows(a, width):
    n = math.prod(a.shape)
    if n % width == 0:
        return a.reshape(-1, width)
    return jnp.pad(a.reshape(1, -1), ((0, 0), (0, -n % width))).reshape(-1, width)


def _n_rows(shape, width):
    return -(-math.prod(shape) // width)


def _pack_rows(arrays, rows, width):
    parts = [_as_rows(a, width) for a in arrays]
    used = sum(p.shape[0] for p in parts)
    return jnp.concatenate(parts + [jnp.zeros((rows - used, width), arrays[0].dtype)], axis=0)


def _unpack_rows(pack, shapes):
    width = pack.shape[1]
    out, off = [], 0
    for shp in shapes:
        nr, n = _n_rows(shp, width), math.prod(shp)
        part = pack[off:off + nr]
        out.append(part.reshape(shp) if n % width == 0 else part.reshape(-1)[:n].reshape(shp))
        off += nr
    return out


def _rows_for(shapes, width, mult=8):
    n = sum(_n_rows(s, width) for s in shapes)
    return -(-n // mult) * mult


def _gdn_fwd(x, p):
    proj = mm(x, p["w_main"], name="gdn_proj")
    ba = mm(x, p["w_gate"], name="gdn_proj_gate")
    q, k, v = (gdn_pre(proj, p["conv"], kind) for kind in "qkv")
    beta_b, g_b = gdn_gates(ba, p["alog_row"], p["dt_row"])
    o2, st, tinv = gdn_rec_fwd(q, k, v, beta_b, g_b)
    y = mixer_post(o2, proj, p["norm_g"], A_DK, 3 * A_HEADS, "gdn_post")
    m = mm(y, p["w_out"], name="gdn_out")
    return m, (x, proj, ba, q, k, v, beta_b, g_b, o2, st, tinv, y)


def _gdn_bwd(saved, p, dm):
    x, proj, ba, q, k, v, beta_b, g_b, o2, st, tinv, y = saved
    d_w_out = mm(y, dm, "tn", name="gdn_dw_out")
    dy = mm(dm, p["w_out"], "nt", name="gdn_dy")
    do, dz, d_norm_g = mixer_post_bwd(o2, proj, p["norm_g"], dy, A_DK, 3 * A_HEADS, "gdn_post_bwd")
    dq2, dk2, dv2, dbb, dgb = gdn_rec_bwd(q, k, v, beta_b, g_b, st, tinv, do)
    dba, d_alog_row, d_dt_row = gdn_gates_bwd(ba, p["alog_row"], p["dt_row"], dbb, dgb)
    du, dconv = zip(*(gdn_pre_bwd(proj, p["conv"], d2, kind) for d2, kind in ((dq2, "q"), (dk2, "k"), (dv2, "v"))))
    dproj = jnp.concatenate(list(du) + [dz], axis=1)
    d_w_main = mm(x, dproj, "tn", name="gdn_dw_main")
    d_w_gate = mm(x, dba, "tn", name="gdn_dw_gate")
    dx = mm(dba, p["w_gate"], "nt", epi="add", extra=dm, alpha=ALPHA, name="gdn_dx_gate")
    dx = mm(dproj, p["w_main"], "nt", epi="add", extra=dx, name="gdn_dx")
    grads = dict(w_in=jnp.concatenate([d_w_main, d_w_gate[:, :2 * N_GATE]], axis=1), conv=jnp.concatenate(dconv, axis=1),
                 alog=d_alog_row[0, N_GATE:2 * N_GATE].reshape(2, A_HEADS), dt=d_dt_row[0, N_GATE:2 * N_GATE].reshape(2, A_HEADS),
                 norm_g=d_norm_g[0], w_out=d_w_out)
    return dx, grads


def _gla_fwd(x, p):
    proj = mm(x, p["w_main"], name="gla_proj")
    gl = mm(x, p["w_gate"], name="gla_proj_gate")
    log_a = gla_gate(gl, p["w2"], p["gate_b"])
    o2, st = gla_rec_fwd(proj, log_a)
    y = mixer_post(o2, proj, p["norm_g"], B_DV, (2 * B_KW + B_VW) // B_DV, "gla_post")
    m = mm(y, p["w_out"], name="gla_out")
    return m, (x, proj, gl, log_a, o2, st, y)


def _gla_bwd(saved, p, dm):
    x, proj, gl, log_a, o2, st, y = saved
    d_w_out = mm(y, dm, "tn", name="gla_dw_out")
    dy = mm(dm, p["w_out"], "nt", name="gla_dy")
    do, dr, d_norm_g = mixer_post_bwd(o2, proj, p["norm_g"], dy, B_DV, (2 * B_KW + B_VW) // B_DV, "gla_post_bwd")
    dq2, dk2, dv2, dla = gla_rec_bwd(proj, log_a, st, do)
    dgl, dz, d_b0, d_b1 = gla_gate_bwd(gl, p["w2"], p["gate_b"], dla)
    d_w2 = [mm(gl, dz[n], "tn", name="gla_dw_gate_w2") for n in range(2)]
    dproj = jnp.concatenate([dq2[0] + dq2[1], dk2[0] + dk2[1], dv2[0] + dv2[1], dr], axis=1)
    d_w_main = mm(x, dproj, "tn", name="gla_dw_main")
    d_w_gate = mm(x, dgl, "tn", name="gla_dw_gate")
    dx = mm(dgl, p["w_gate"], "nt", epi="add", extra=dm, alpha=ALPHA, name="gla_dx_gate")
    dx = mm(dproj, p["w_main"], "nt", epi="add", extra=dx, name="gla_dx")
    grads = dict(w_in=jnp.concatenate([d_w_main, d_w_gate[:, :2 * B_RANK]], axis=1),
                 gate_w2=jnp.stack([d_w2[n][n * B_RANK:(n + 1) * B_RANK] for n in range(2)]),
                 gate_b=jnp.concatenate([d_b0, d_b1]), norm_g=d_norm_g[0], w_out=d_w_out)
    return dx, grads


def _pad_cols(w, width=LANES):
    return jnp.pad(w, ((0, 0), (0, width - w.shape[1])))


def _local_step(x, target, a_w_in, a_conv, a_alog, a_dt_bias, a_norm_g, a_w_out, b_w_in, b_gate_w2, b_gate_b, b_norm_g,
                b_w_out, ln1_g, ln1_b, mlp_w1, mlp_w2, ln2_g, ln2_b, grad_pack=None):
    layer_p = []
    for i in range(DEPTH):
        j = i // 2
        if i % 2 == 0:
            layer_p.append(dict(
                w_main=a_w_in[j][:, :4 * A_W], w_gate=_pad_cols(a_w_in[j][:, 4 * A_W:]), conv=a_conv[j],
                alog_row=jnp.pad(a_alog[j].reshape(1, N_GATE), ((0, 0), (N_GATE, LANES - 2 * N_GATE))),
                dt_row=jnp.pad(a_dt_bias[j].reshape(1, N_GATE), ((0, 0), (N_GATE, LANES - 2 * N_GATE))),
                norm_g=a_norm_g[j].reshape(1, A_DK), w_out=a_w_out[j]))
        else:
            w2 = jnp.stack([jnp.pad(b_gate_w2[j][n], ((n * B_RANK, LANES - (n + 1) * B_RANK), (0, 0))) for n in range(2)])
            layer_p.append(dict(
                w_main=b_w_in[j][:, :2 * B_KW + 2 * B_VW], w_gate=_pad_cols(b_w_in[j][:, 2 * B_KW + 2 * B_VW:]),
                w2=w2, gate_b=b_gate_b[j].reshape(2, 1, B_KW), norm_g=b_norm_g[j].reshape(1, B_DV), w_out=b_w_out[j]))

    saved = []
    h = x
    for i in range(DEPTH):
        p = layer_p[i]
        m, sv = (_gdn_fwd if i % 2 == 0 else _gla_fwd)(h, p)
        x1 = ln_fwd(h, m, ln1_g[i:i + 1], ln1_b[i:i + 1])
        h1 = mm(x1, mlp_w1[i], name="mlp_up")
        mlp = mm(h1, mlp_w2[i], act="sqrelu", name="mlp_down")
        x2 = ln_fwd(x1, mlp, ln2_g[i:i + 1], ln2_b[i:i + 1])
        saved.append((sv, h, m, x1, h1, mlp))
        h = x2

    dh, loss_part = loss_head(h, target)

    g_a, g_b, g_ln1g, g_ln1b, g_ln2g, g_ln2b, g_w1, g_w2 = {}, {}, {}, {}, {}, {}, {}, {}
    pack = None
    for i in reversed(range(DEPTH)):
        sv, xin, m, x1, h1, mlp = saved[i]
        p = layer_p[i]
        dr2, g_ln2g[i], g_ln2b[i] = ln_bwd(x1, mlp, ln2_g[i:i + 1], dh)
        if grad_pack is None:
            g_w2[i] = mm(h1, dr2, "tn", act="sqrelu", name="mlp_dw_down")
        else:
            pack = mm(h1, dr2, "tn", act="sqrelu", pack=(pack, grad_pack[0], grad_pack[2] + i, "rows"), name="mlp_dw_down")
        dh1 = mm(dr2, mlp_w2[i], "nt", epi="dsqrelu", extra=h1, name="mlp_dh")
        if grad_pack is None:
            g_w1[i] = mm(x1, dh1, "tn", chip_major=True, name="mlp_dw_up")
        else:
            pack = mm(x1, dh1, "tn", pack=(pack, grad_pack[0], grad_pack[1] + i, "cols"), name="mlp_dw_up")
        dx1 = mm(dh1, mlp_w1[i], "nt", epi="add", extra=dr2, alpha=ALPHA, name="mlp_dx")
        dr1, g_ln1g[i], g_ln1b[i] = ln_bwd(xin, m, ln1_g[i:i + 1], dx1)
        dh, g = (_gdn_bwd if i % 2 == 0 else _gla_bwd)(sv, p, dr1)
        (g_a if i % 2 == 0 else g_b)[i // 2] = g

    per_layer = lambda d, key=None: [(d[i] if key is None else d[i][key]) for i in sorted(d)]
    st = lambda d, key=None: jnp.stack(per_layer(d, key))
    grads = dict(
        a_w_in=per_layer(g_a, "w_in"), a_conv=st(g_a, "conv"), a_alog=st(g_a, "alog"), a_dt_bias=st(g_a, "dt"),
        a_norm_g=st(g_a, "norm_g"), a_w_out=per_layer(g_a, "w_out"), b_w_in=per_layer(g_b, "w_in"),
        b_gate_w2=st(g_b, "gate_w2"), b_gate_b=st(g_b, "gate_b"), b_norm_g=st(g_b, "norm_g"),
        b_w_out=per_layer(g_b, "w_out"), ln1_g=st(g_ln1g)[:, 0], ln1_b=st(g_ln1b)[:, 0], mlp_w1=per_layer(g_w1),
        mlp_w2=per_layer(g_w2), ln2_g=st(g_ln2g)[:, 0], ln2_b=st(g_ln2b)[:, 0], pack=pack)
    return loss_part, dh, grads


WEIGHTS = ("a_w_in", "a_conv", "a_alog", "a_dt_bias", "a_norm_g", "a_w_out", "b_w_in", "b_gate_w2", "b_gate_b",
           "b_norm_g", "b_w_out", "ln1_g", "ln1_b", "mlp_w1", "mlp_w2", "ln2_g", "ln2_b")
BIG = ("mlp_w1", "mlp_w2", "a_w_out", "b_w_out", "a_w_in", "b_w_in")
SHARD_AXIS = {"mlp_w1": 2, "mlp_w2": 1, "a_w_out": 1, "b_w_out": 1, "a_w_in": 2, "b_w_in": 2}
SMALL = tuple(n for n in WEIGHTS if n not in BIG)
SMALL_SHARD_AXIS = {"a_conv": 2, "b_gate_w2": 3, "b_gate_b": 2, "b_norm_g": 1}


def _to_chip_major(full, axis):
    shp = full.shape
    t = full.reshape(shp[:axis] + (4, shp[axis] // 4) + shp[axis + 1:])
    return jnp.moveaxis(t, axis, 0)


def _from_chip_major(stacked, axis):
    t = jnp.moveaxis(stacked, 0, axis)
    shp = t.shape
    return t.reshape(shp[:axis] + (shp[axis] * shp[axis + 1],) + shp[axis + 2:])


def kernel(x, a_w_in, a_conv, a_alog, a_dt_bias, a_norm_g, a_w_out, b_w_in, b_gate_w2, b_gate_b, b_norm_g, b_w_out, ln1_g, ln1_b, mlp_w1, mlp_w2, ln2_g, ln2_b, loss_target, m_a_w_in, m_a_conv, m_a_alog, m_a_dt_bias, m_a_norm_g, m_a_w_out, m_b_w_in, m_b_gate_w2, m_b_gate_b, m_b_norm_g, m_b_w_out, m_ln1_g, m_ln1_b, m_mlp_w1, m_mlp_w2, m_ln2_g, m_ln2_b, v_a_w_in, v_a_conv, v_a_alog, v_a_dt_bias, v_a_norm_g, v_a_w_out, v_b_w_in, v_b_gate_w2, v_b_gate_b, v_b_norm_g, v_b_w_out, v_ln1_g, v_ln1_b, v_mlp_w1, v_mlp_w2, v_ln2_g, v_ln2_b):
    w = dict(a_w_in=a_w_in, a_conv=a_conv, a_alog=a_alog, a_dt_bias=a_dt_bias, a_norm_g=a_norm_g, a_w_out=a_w_out,
             b_w_in=b_w_in, b_gate_w2=b_gate_w2, b_gate_b=b_gate_b, b_norm_g=b_norm_g, b_w_out=b_w_out, ln1_g=ln1_g,
             ln1_b=ln1_b, mlp_w1=mlp_w1, mlp_w2=mlp_w2, ln2_g=ln2_g, ln2_b=ln2_b)
    mom = dict(a_w_in=m_a_w_in, a_conv=m_a_conv, a_alog=m_a_alog, a_dt_bias=m_a_dt_bias, a_norm_g=m_a_norm_g,
               a_w_out=m_a_w_out, b_w_in=m_b_w_in, b_gate_w2=m_b_gate_w2, b_gate_b=m_b_gate_b, b_norm_g=m_b_norm_g,
               b_w_out=m_b_w_out, ln1_g=m_ln1_g, ln1_b=m_ln1_b, mlp_w1=m_mlp_w1, mlp_w2=m_mlp_w2, ln2_g=m_ln2_g,
               ln2_b=m_ln2_b)
    var = dict(a_w_in=v_a_w_in, a_conv=v_a_conv, a_alog=v_a_alog, a_dt_bias=v_a_dt_bias, a_norm_g=v_a_norm_g,
               a_w_out=v_a_w_out, b_w_in=v_b_w_in, b_gate_w2=v_b_gate_w2, b_gate_b=v_b_gate_b, b_norm_g=v_b_norm_g,
               b_w_out=v_b_w_out, ln1_g=v_ln1_g, ln1_b=v_ln1_b, mlp_w1=v_mlp_w1, mlp_w2=v_mlp_w2, ln2_g=v_ln2_g,
               ln2_b=v_ln2_b)
    chip = 2 * lax.axis_index("x") + lax.axis_index("y")

    seg_rows = [w[n].size // D_MODEL for n in BIG]
    seg_off = [sum(seg_rows[:i]) for i in range(len(BIG))]
    rows = -(-sum(seg_rows) // PACK_TILE) * PACK_TILE
    shard_pack = jnp.concatenate([w[n].reshape(-1, D_MODEL) for n in BIG]
                                 + [jnp.zeros((rows - sum(seg_rows), D_MODEL), F32)], axis=0)
    chip_idx = chip.astype(jnp.int32).reshape(1)
    gathered = allgather_chips(cast_into_slot(shard_pack, chip_idx))
    full = {}
    for n, off, nr in zip(BIG, seg_off, seg_rows):
        if n in ("mlp_w1", "mlp_w2"):
            kind = "cols" if SHARD_AXIS[n] == 2 else "rows"
            full[n] = [Gathered(gathered, off + i * D_MODEL, kind) for i in range(DEPTH)]
            continue
        stacked = gathered[:, off:off + nr].reshape((4,) + w[n].shape)
        full[n] = _from_chip_major(stacked, SHARD_AXIS[n])
    sharded_small = tuple(SMALL_SHARD_AXIS)
    sm_shapes = [w[n].shape for n in sharded_small]
    sm_rows = _rows_for(sm_shapes, LANES)
    sm_all = exchange_all(_pack_rows([w[n] for n in sharded_small], sm_rows, LANES), "gather_small")
    per_chip = [_unpack_rows(sm_all[2 * pch], sm_shapes) for pch in range(4)]
    for idx, n in enumerate(sharded_small):
        full[n] = jnp.concatenate([per_chip[pch][idx] for pch in range(4)], axis=SMALL_SHARD_AXIS[n])
    for n in WEIGHTS:
        full.setdefault(n, w[n])

    blk_of = {n: off // D_MODEL for n, off in zip(BIG, seg_off)}
    loss_part, grad_x, grads = _local_step(x[0], loss_target[0], *[full[n] for n in WEIGHTS],
                                           grad_pack=(rows, blk_of["mlp_w1"], blk_of["mlp_w2"]))
    loss = lax.psum(jnp.sum(loss_part), ("x", "y", "c"))

    gpack = grads["pack"]
    rest = jnp.concatenate(
        [_to_chip_major(g, SHARD_AXIS[n] - 1).reshape(4, -1, D_MODEL) for n in BIG[2:] for g in grads[n]]
        + [jnp.zeros((4, rows - sum(seg_rows), D_MODEL), F32)], axis=1)
    gpack = lax.dynamic_update_slice(gpack, rest, (0, seg_off[2], 0))
    core = lax.axis_index("c").astype(jnp.int32).reshape(1)
    chip_sum = add_sibling_half(gpack, sibling_halves(gpack), core)
    ax, ay = lax.axis_index("x"), lax.axis_index("y")
    slot_x = (2 * (1 - ax) + ay).astype(jnp.int32).reshape(1)
    slot_y = (2 * ax + (1 - ay)).astype(jnp.int32).reshape(1)
    to_nbrs = merge_first_hop(chip_sum, scatter_first_hop(chip_sum), slot_x, slot_y, core)
    reduced = join_halves(sum_received(chip_sum, scatter_second_hop(to_nbrs), chip_idx, core))
    out_g, out_d, out_m, out_v = {}, {}, {}, {}
    for n, off, nr in zip(BIG, seg_off, seg_rows):
        if w[n].shape[-1] == D_MODEL:
            view = lambda t: t.reshape(-1, D_MODEL)
            res = adamw(view(w[n]), view(mom[n]), view(var[n]), (reduced,), off, "adamw_" + n)
        else:
            cols = w[n].shape[-1]
            view = lambda t: t.reshape(-1, cols)
            res = adamw(view(w[n]), view(mom[n]), view(var[n]), (view(reduced[off:off + nr]),), 0, "adamw_" + n)
        out_g[n], out_d[n], out_m[n], out_v[n] = (t.reshape(w[n].shape) for t in res)

    all_shapes = [full[n].shape for n in SMALL]
    g_rows = _rows_for(all_shapes, LANES)
    g_all = exchange_all(_pack_rows([grads[n] for n in SMALL], g_rows, LANES), "gather_small_grads")
    g_sum = _unpack_rows(sum_slots(g_all, "sum_small_grads"), all_shapes)
    g_mine = []
    for n, g in zip(SMALL, g_sum):
        if n in SMALL_SHARD_AXIS:
            ax = SMALL_SHARD_AXIS[n]
            g = lax.dynamic_slice_in_dim(g, chip * w[n].shape[ax], w[n].shape[ax], axis=ax)
        g_mine.append(g)
    my_shapes = [w[n].shape for n in SMALL]
    s_rows = _rows_for(my_shapes, LANES)
    pk = lambda d: _pack_rows([d[n] for n in SMALL], s_rows, LANES)
    res = adamw(pk(w), pk(mom), pk(var), (_pack_rows(g_mine, s_rows, LANES),), 0, "adamw_small")
    for dst, pack in zip((out_g, out_d, out_m, out_v), res):
        for n, t in zip(SMALL, _unpack_rows(pack, my_shapes)):
            dst[n] = t

    return (loss, grad_x[None], *[out_g[n] for n in WEIGHTS], *[out_d[n] for n in WEIGHTS],
            *[out_m[n] for n in WEIGHTS], *[out_v[n] for n in WEIGHTS])
```

```python
import functools
import math

import jax
import jax.numpy as jnp
from jax import lax
from jax.experimental import pallas as pl
from jax.experimental.pallas import tpu as pltpu

F32 = jnp.float32
BF16 = jnp.bfloat16

D_MODEL = 1024
DEPTH = 4
CHUNK = 64
A_HEADS = 8
A_DK = 128
A_W = 1024
A_CONV = 5
B_HEADS = 4
B_DK = 128
B_DV = 256
B_RANK = 16
B_TAU = 16.0
B_KW = 512
B_VW = 1024
ALPHA = (2 * DEPTH) ** 0.25
LN_EPS = 1e-5
RMS_EPS = 1e-6
L2_EPS = 1e-6
ADAM_LR = 0.001
ADAM_B1 = 0.9
ADAM_B2 = 0.999
ADAM_EPS = 1e-08
ADAM_WD = 0.01
ADAM_STEP = 10
LANES = 128
NEG_INF = float("-inf")
VMEM_LIMIT = 56 * 1024 * 1024


def _cparams(sem=None):
    return pltpu.CompilerParams(dimension_semantics=sem, vmem_limit_bytes=VMEM_LIMIT)


def _dg(a, b, ca, cb):
    return lax.dot_general(a.astype(BF16), b.astype(BF16), (((ca,), (cb,)), ((), ())),
                           preferred_element_type=F32)


def _split(x):
    hi = x.astype(BF16)
    return hi, (x - hi.astype(F32)).astype(BF16)


def _dg3(a, b, ca, cb):
    (a1, a2), (b1, b2) = _split(a), _split(b)
    return (_dg(a1, b2, ca, cb) + _dg(a2, b1, ca, cb)) + _dg(a1, b1, ca, cb)


def _dot_with_vjp(dg):
    @functools.partial(jax.custom_vjp, nondiff_argnums=(2, 3))
    def dot(a, b, ca, cb):
        return dg(a, b, ca, cb)

    def fwd(a, b, ca, cb):
        return dg(a, b, ca, cb), (a, b)

    def bwd(ca, cb, res, g):
        a, b = res
        da = dg(g, b, 1, 1 - cb) if ca == 1 else dg(b, g, 1 - cb, 1)
        db = dg(a, g, 1 - ca, 0) if cb == 0 else dg(g, a, 0, 1 - ca)
        return da, db

    dot.defvjp(fwd, bwd)
    return dot


bdot = _dot_with_vjp(_dg)
xdot3 = _dot_with_vjp(_dg3)


def nn(a, b):
    return bdot(a, b, 1, 0)


def nt(a, b):
    return bdot(a, b, 1, 1)


def tn(a, b):
    return bdot(a, b, 0, 0)


def xdot(a, b):
    return xdot3(a, b, 1, 0)


def _sigmoid(x):
    return 1.0 / (1.0 + jnp.exp(-x))


def _softplus(x):
    return jnp.maximum(x, 0.0) + jnp.log(1.0 + jnp.exp(-jnp.abs(x)))


def _chunk_masks(rev):
    ii = lax.broadcasted_iota(jnp.int32, (CHUNK, CHUNK), 0)
    jj = lax.broadcasted_iota(jnp.int32, (CHUNK, CHUNK), 1)
    d = (ii - jj) * (1 - 2 * rev)
    return d >= 0, d > 0, ii == jj, (ii >> 3) == (jj >> 3)


def _each(f, *lists):
    return [f(*xs) for xs in zip(*lists)]


@jax.custom_vjp
def _unit_triangular_inverse(a, ident, blockdiag):
    return _unit_triangular_inverse_impl(a, ident, blockdiag)


def _unit_triangular_inverse_fwd(a, ident, blockdiag):
    t = _unit_triangular_inverse_impl(a, ident, blockdiag)
    return t, (t, ident, blockdiag)


def _unit_triangular_inverse_bwd(res, g):
    t, ident, blockdiag = res
    left = _each(lambda x, y: xdot3(x, y, 0, 0), t, g)
    da = _each(lambda x, y: -xdot3(x, y, 1, 1), left, t)
    return da, jnp.zeros_like(ident), jnp.zeros_like(blockdiag)


_unit_triangular_inverse.defvjp(_unit_triangular_inverse_fwd, _unit_triangular_inverse_bwd)


@jax.custom_vjp
def _known_inverse(a, t):
    return t


def _known_inverse_bwd(t, g):
    left = _each(lambda x, y: xdot3(x, y, 0, 0), t, g)
    return _each(lambda x, y: -xdot3(x, y, 1, 1), left, t), _each(jnp.zeros_like, t)


_known_inverse.defvjp(lambda a, t: (t, t), _known_inverse_bwd)


def _unit_triangular_inverse_impl(a, ident, blockdiag):
    ad = _each(lambda x: x * blockdiag, a)
    e = _each(lambda x, y: x - y, a, ad)
    dinv = _each(lambda x: ident - x, ad)
    p = _each(xdot, ad, ad)
    dinv = _each(lambda x, y: x + xdot(x, y), dinv, p)
    p = _each(xdot, p, p)
    dinv = _each(lambda x, y: x + xdot(x, y), dinv, p)
    g = _each(lambda x, y: -xdot(x, y), dinv, e)
    finv = _each(lambda x: ident + x, g)
    p = _each(xdot, g, g)
    finv = _each(lambda x, y: x + xdot(x, y), finv, p)
    p = _each(xdot, p, p)
    finv = _each(lambda x, y: x + xdot(x, y), finv, p)
    return _each(xdot, finv, dinv)


def _gdn_step(state, q, k, v, bb, gb, rev, t_saved=None):
    causal, strict, eye, blockdiag = _chunk_masks(rev)
    lower = causal.astype(F32)
    ones = jnp.ones((CHUNK, CHUNK), F32)
    gcb = _each(lambda x: xdot(lower, x), gb)
    gcol = _each(lambda x: x[:, :CHUNK], gcb)
    grow = _each(lambda x: xdot(ones, jnp.where(eye, x, 0.0)), gcol)
    decay = _each(lambda x, y: jnp.exp(jnp.where(causal, x - y, NEG_INF)), gcol, grow)
    kb = _each(lambda x, y: x * y, k, bb)
    a = _each(lambda x, y, z: jnp.where(strict, nt(x, y) * z, 0.0), kb, k, decay)
    if t_saved is None:
        t = _unit_triangular_inverse(a, eye.astype(F32), blockdiag.astype(F32))
    else:
        t = _known_inverse(a, t_saved)
    egc = _each(jnp.exp, gcb)
    u = _each(lambda x, y, z: xdot(x, y * z), t, v, bb)
    w = _each(lambda x, y, z: xdot(x, y * z), t, kb, egc)
    qk = _each(lambda x, y, z: nt(x, y) * z, q, k, decay)
    glast = _each(lambda x: jnp.sum(x, axis=0, keepdims=True), gb)
    v_new = _each(lambda x, y, z: x - nn(y, z), u, w, state)
    o = _each(lambda x, y, z, p, r: nn(x * y, z) + nn(p, r), q, egc, state, qk, v_new)
    k_dec = _each(lambda x, y, z: x * jnp.exp(y - z), k, glast, gcb)
    state_new = _each(lambda x, y, z, p: x * jnp.exp(y) + tn(z, p), state, glast, k_dec, v_new)
    return state_new, o, t


def _gla_step(state_t, q, k, v, la, rev):
    causal, _, _, _ = _chunk_masks(rev)
    lower = causal.astype(F32)
    sign = 1 - 2 * rev
    b = _each(lambda x: xdot(lower, x), la)
    q = _each(lambda x: x * (B_DK ** -0.5), q)
    row = lax.broadcasted_iota(jnp.int32, (CHUNK, B_DK), 0)
    sub = row // GLA_SUB
    parts = []
    for blk in range(CHUNK // GLA_SUB):
        rows = slice(blk * GLA_SUB, (blk + 1) * GLA_SUB)
        r_at = jnp.where(rev == 1, GLA_SUB * (blk + 1), GLA_SUB * blk - 1)
        r = _each(lambda x: jnp.sum(jnp.where(row == r_at, x, 0.0), axis=0, keepdims=True), b)
        q_blk = _each(lambda x, y, z: x[rows] * jnp.exp(y[rows] - z), q, b, r)
        k_past = _each(lambda x, y, z: x * jnp.exp(jnp.where((sub - blk) * sign < 0, z - y, NEG_INF)), k, b, r)
        parts.append(_each(lambda x, y: xdot3(x, y, 1, 1), q_blk, k_past))
    scores = _each(lambda *p: jnp.concatenate(p, axis=0), *parts)
    shp = (GLA_SUB, GLA_SUB, B_DK)
    d3 = (lax.broadcasted_iota(jnp.int32, shp, 0) - lax.broadcasted_iota(jnp.int32, shp, 1)) * sign
    place_r = lax.broadcasted_iota(jnp.int32, (GLA_SUB, CHUNK), 0)
    place_c = lax.broadcasted_iota(jnp.int32, (GLA_SUB, CHUNK), 1)
    diag = []
    for blk in range(CHUNK // GLA_SUB):
        rows = slice(blk * GLA_SUB, (blk + 1) * GLA_SUB)
        place = (place_c == place_r + blk * GLA_SUB).astype(F32)

        def pairs(qh, kh, bh):
            qb, kb, bb = qh[rows], kh[rows], bh[rows]
            dec = jnp.exp(jnp.where(d3 >= 0, bb[:, None, :] - bb[None, :, :], NEG_INF))
            return xdot(jnp.sum(qb[:, None, :] * kb[None, :, :] * dec, axis=-1), place)

        diag.append(_each(pairs, q, k, b))
    scores = _each(lambda x, *d: x + jnp.concatenate(d, axis=0), scores, *diag)
    blast = _each(lambda x: jnp.sum(x, axis=0, keepdims=True), la)
    o = _each(lambda x, y, z, s, w: nt(x * jnp.exp(y), z) + nn(s, w), q, b, state_t, scores, v)
    k_dec = _each(lambda x, y, z: x * jnp.exp(y - z), k, blast, b)
    state_new = _each(lambda x, y, z, w: jnp.exp(x) * y + tn(z, w), blast, state_t, v, k_dec)
    return state_new, o


def _chunk_pos(d, m, n):
    return m + d * (n - 1 - 2 * m)


GLA_SUB = 16
GDN_HEADS_PER_STEP = 8
def gdn_rec_fwd(q, k, v, beta_b, g_b):
    s = q.shape[0]
    n = s // CHUNK

    hb = GDN_HEADS_PER_STEP
    wide = hb * LANES

    def body(q_ref, k_ref, v_ref, bb_ref, gb_ref, o_ref, st_ref, t_ref, state):
        d = pl.program_id(0)

        @pl.when(pl.program_id(2) == 0)
        def _():
            state[...] = jnp.zeros_like(state)

        cols = [slice(hh * LANES, (hh + 1) * LANES) for hh in range(hb)]
        st = [state[hh] for hh in range(hb)]
        new, o, t = _gdn_step(st, *([r[:, c] for c in cols] for r in (q_ref, k_ref, v_ref, bb_ref, gb_ref)), d)
        for hh, c in enumerate(cols):
            st_ref[hh] = st[hh]
            t_ref[hh] = t[hh]
            state[hh] = new[hh]
            o_ref[:, c] = o[hh]

    blk = pl.BlockSpec((CHUNK, wide), lambda d, h, m: (_chunk_pos(d, m, n), h))
    gate = pl.BlockSpec((CHUNK, wide), lambda d, h, m: (_chunk_pos(d, m, n), d * (A_HEADS // hb) + h))
    return pl.pallas_call(
        body, name="gdn_rec_fwd", grid=(2, A_HEADS // hb, n),
        in_specs=[blk, blk, blk, gate, gate],
        out_specs=[pl.BlockSpec((None, CHUNK, wide), lambda d, h, m: (d, _chunk_pos(d, m, n), h)),
                   pl.BlockSpec((None, hb, None, A_DK, LANES), lambda d, h, m: (d, h, _chunk_pos(d, m, n), 0, 0)),
                   pl.BlockSpec((None, hb, None, CHUNK, CHUNK), lambda d, h, m: (d, h, _chunk_pos(d, m, n), 0, 0))],
        out_shape=[jax.ShapeDtypeStruct((2, s, A_W), F32), jax.ShapeDtypeStruct((2, A_HEADS, n, A_DK, LANES), F32),
                   jax.ShapeDtypeStruct((2, A_HEADS, n, CHUNK, CHUNK), F32)],
        scratch_shapes=[pltpu.VMEM((hb, A_DK, LANES), F32)],
        compiler_params=_cparams(("arbitrary", "arbitrary", "arbitrary")),
    )(q, k, v, beta_b, g_b)


def gdn_rec_bwd(q, k, v, beta_b, g_b, states, tinv, do):
    s = q.shape[0]
    n = s // CHUNK

    hb = GDN_HEADS_PER_STEP
    wide = hb * LANES

    def body(q_ref, k_ref, v_ref, bb_ref, gb_ref, st_ref, t_ref, do_ref, dq_ref, dk_ref, dv_ref, dbb_ref, dgb_ref, dstate):
        d = pl.program_id(0)

        @pl.when(pl.program_id(2) == 0)
        def _():
            dstate[...] = jnp.zeros_like(dstate)

        def step(*a):
            return _gdn_step(*a, d, t_saved=[t_ref[hh] for hh in range(hb)])[:2]

        cols = [slice(hh * LANES, (hh + 1) * LANES) for hh in range(hb)]
        _, vjp = jax.vjp(step, [st_ref[hh] for hh in range(hb)],
                         *([r[:, c] for c in cols] for r in (q_ref, k_ref, v_ref, bb_ref, gb_ref)))
        grads = vjp(([dstate[hh] for hh in range(hb)], [do_ref[:, c] for c in cols]))
        for hh, c in enumerate(cols):
            dstate[hh], dq_ref[:, c], dk_ref[:, c], dv_ref[:, c], dbb_ref[:, c], dgb_ref[:, c] = (g[hh] for g in grads)

    pos = lambda d, m: _chunk_pos(1 - d, m, n)
    blk = pl.BlockSpec((CHUNK, wide), lambda d, h, m: (pos(d, m), h))
    gate = pl.BlockSpec((CHUNK, wide), lambda d, h, m: (pos(d, m), d * (A_HEADS // hb) + h))
    oblk = pl.BlockSpec((None, CHUNK, wide), lambda d, h, m: (d, pos(d, m), h))
    return pl.pallas_call(
        body, name="gdn_rec_bwd", grid=(2, A_HEADS // hb, n),
        in_specs=[blk, blk, blk, gate, gate,
                  pl.BlockSpec((None, hb, None, A_DK, LANES), lambda d, h, m: (d, h, pos(d, m), 0, 0)),
                  pl.BlockSpec((None, hb, None, CHUNK, CHUNK), lambda d, h, m: (d, h, pos(d, m), 0, 0)), blk],
        out_specs=[oblk, oblk, oblk, gate, gate],
        out_shape=[jax.ShapeDtypeStruct((2, s, A_W), F32)] * 3 + [jax.ShapeDtypeStruct(beta_b.shape, F32)] * 2,
        scratch_shapes=[pltpu.VMEM((hb, A_DK, LANES), F32)],
        compiler_params=_cparams(("arbitrary", "arbitrary", "arbitrary")),
    )(q, k, v, beta_b, g_b, states, tinv, do)


def gla_rec_fwd(proj, log_a):
    s = proj.shape[0]
    n = s // CHUNK

    kcols = [slice(h * B_DK, (h + 1) * B_DK) for h in range(B_HEADS)]
    vcols = [slice(h * B_DV, (h + 1) * B_DV) for h in range(B_HEADS)]

    def body(q_ref, k_ref, v_ref, la_ref, o_ref, st_ref, state):
        d = pl.program_id(0)

        @pl.when(pl.program_id(1) == 0)
        def _():
            state[...] = jnp.zeros_like(state)

        st = [state[h] for h in range(B_HEADS)]
        new, o = _gla_step(st, [q_ref[:, c] for c in kcols], [k_ref[:, c] for c in kcols], [v_ref[:, c] for c in vcols],
                           [la_ref[:, c] for c in kcols], d)
        for h in range(B_HEADS):
            st_ref[h] = st[h]
            state[h] = new[h]
            o_ref[:, vcols[h]] = o[h]

    pos = lambda d, m: _chunk_pos(d, m, n)
    return pl.pallas_call(
        body, name="gla_rec_fwd", grid=(2, n),
        in_specs=[pl.BlockSpec((CHUNK, B_KW), lambda d, m: (pos(d, m), 0)),
                  pl.BlockSpec((CHUNK, B_KW), lambda d, m: (pos(d, m), 1)),
                  pl.BlockSpec((CHUNK, B_VW), lambda d, m: (pos(d, m), 2 * B_KW // B_VW)),
                  pl.BlockSpec((None, CHUNK, B_KW), lambda d, m: (d, pos(d, m), 0))],
        out_specs=[pl.BlockSpec((None, CHUNK, B_VW), lambda d, m: (d, pos(d, m), 0)),
                   pl.BlockSpec((None, B_HEADS, None, B_DV, B_DK), lambda d, m: (d, 0, pos(d, m), 0, 0))],
        out_shape=[jax.ShapeDtypeStruct((2, s, B_VW), F32), jax.ShapeDtypeStruct((2, B_HEADS, n, B_DV, B_DK), F32)],
        scratch_shapes=[pltpu.VMEM((B_HEADS, B_DV, B_DK), F32)],
        compiler_params=_cparams(("arbitrary", "arbitrary")),
    )(proj, proj, proj, log_a)


def gla_rec_bwd(proj, log_a, states, do):
    s = proj.shape[0]
    n = s // CHUNK

    kcols = [slice(h * B_DK, (h + 1) * B_DK) for h in range(B_HEADS)]
    vcols = [slice(h * B_DV, (h + 1) * B_DV) for h in range(B_HEADS)]

    def body(q_ref, k_ref, v_ref, la_ref, st_ref, do_ref, dq_ref, dk_ref, dv_ref, dla_ref, dstate):
        d = pl.program_id(0)

        @pl.when(pl.program_id(1) == 0)
        def _():
            dstate[...] = jnp.zeros_like(dstate)

        step = functools.partial(_gla_step, rev=d)
        _, vjp = jax.vjp(step, [st_ref[h] for h in range(B_HEADS)], [q_ref[:, c] for c in kcols],
                         [k_ref[:, c] for c in kcols], [v_ref[:, c] for c in vcols], [la_ref[:, c] for c in kcols])
        dst, dq, dk, dv, dla = vjp(([dstate[h] for h in range(B_HEADS)], [do_ref[:, c] for c in vcols]))
        for h in range(B_HEADS):
            dstate[h] = dst[h]
            dq_ref[:, kcols[h]] = dq[h]
            dk_ref[:, kcols[h]] = dk[h]
            dv_ref[:, vcols[h]] = dv[h]
            dla_ref[:, kcols[h]] = dla[h]

    pos = lambda d, m: _chunk_pos(1 - d, m, n)
    kblk = pl.BlockSpec((None, CHUNK, B_KW), lambda d, m: (d, pos(d, m), 0))
    return pl.pallas_call(
        body, name="gla_rec_bwd", grid=(2, n),
        in_specs=[pl.BlockSpec((CHUNK, B_KW), lambda d, m: (pos(d, m), 0)),
                  pl.BlockSpec((CHUNK, B_KW), lambda d, m: (pos(d, m), 1)),
                  pl.BlockSpec((CHUNK, B_VW), lambda d, m: (pos(d, m), 2 * B_KW // B_VW)),
                  kblk,
                  pl.BlockSpec((None, B_HEADS, None, B_DV, B_DK), lambda d, m: (d, 0, pos(d, m), 0, 0)),
                  pl.BlockSpec((CHUNK, B_VW), lambda d, m: (pos(d, m), 0))],
        out_specs=[kblk, kblk, pl.BlockSpec((None, CHUNK, B_VW), lambda d, m: (d, pos(d, m), 0)), kblk],
        out_shape=[jax.ShapeDtypeStruct((2, s, B_KW), F32), jax.ShapeDtypeStruct((2, s, B_KW), F32),
                   jax.ShapeDtypeStruct((2, s, B_VW), F32), jax.ShapeDtypeStruct((2, s, B_KW), F32)],
        scratch_shapes=[pltpu.VMEM((B_HEADS, B_DV, B_DK), F32)],
        compiler_params=_cparams(("arbitrary", "arbitrary")),
    )(proj, proj, proj, log_a, states, do)


MM_TILE_OUT = 1024
MM_TILE_K = 1024


def _tile(n, pref):
    return pref if n % pref == 0 else n


class Gathered:
    def __init__(self, g, off, kind):
        assert off % D_MODEL == 0 and MM_TILE_OUT == D_MODEL and MM_TILE_K == D_MODEL
        self.g, self.blk, self.kind = g, off // D_MODEL, kind
        self.shape = (D_MODEL, 4 * D_MODEL) if kind == "cols" else (4 * D_MODEL, D_MODEL)

    def spec(self, mode):
        blk = self.blk
        chip_is_k = (self.kind == "rows") == (mode == "nn")
        if chip_is_k:
            return pl.BlockSpec((None, D_MODEL, D_MODEL), lambda i, j, k: (k, blk, 0))
        return pl.BlockSpec((None, D_MODEL, D_MODEL), lambda i, j, k: (j, blk, 0))


def mm(a, b, mode="nn", act=None, epi=None, extra=None, alpha=1.0, chip_major=False, pack=None, name="mm"):
    if mode == "tn":
        kk, m = a.shape
    else:
        m, kk = a.shape
    nn_ = b.shape[0] if mode == "nt" else b.shape[1]
    tm, tn_, tk = _tile(m, MM_TILE_OUT), _tile(nn_, MM_TILE_OUT), _tile(kk, MM_TILE_K)
    nk = kk // tk
    ca, cb = {"nn": (1, 0), "nt": (1, 1), "tn": (0, 0)}[mode]

    def body(*refs):
        o_ref = refs[-1]
        a_ref, b_ref = refs[:2]
        if epi is not None:
            e_ref = refs[2]
        kstep = pl.program_id(2)
        at = a_ref[...]
        if act == "sqrelu":
            at = jnp.square(jnp.maximum(at, 0.0))
        part = _dg(at, b_ref[...], ca, cb)

        @pl.when(kstep == 0)
        def _():
            o_ref[...] = part

        @pl.when(kstep > 0)
        def _():
            o_ref[...] += part

        if epi is not None:
            @pl.when(kstep == nk - 1)
            def _():
                if epi == "dsqrelu":
                    o_ref[...] = o_ref[...] * (2.0 * jnp.maximum(e_ref[...], 0.0))
                else:
                    o_ref[...] = o_ref[...] + alpha * e_ref[...]

    a_spec = pl.BlockSpec((tk, tm), lambda i, j, k: (k, i)) if mode == "tn" else pl.BlockSpec((tm, tk), lambda i, j, k: (i, k))
    if isinstance(b, Gathered):
        assert mode in ("nn", "nt") and tn_ == D_MODEL and tk == D_MODEL
        b_spec, b = b.spec(mode), b.g
    elif mode == "nt":
        b_spec = pl.BlockSpec((tn_, tk), lambda i, j, k: (j, k))
    else:
        b_spec = pl.BlockSpec((tk, tn_), lambda i, j, k: (k, j))
    o_spec = pl.BlockSpec((tm, tn_), lambda i, j, k: (i, j))
    ins, specs = [a, b], [a_spec, b_spec]
    if epi is not None:
        ins.append(extra)
        specs.append(o_spec)
    out_shape = jax.ShapeDtypeStruct((m, nn_), F32)
    if chip_major:
        assert nn_ == 4 * D_MODEL and tn_ == D_MODEL
        o_spec = pl.BlockSpec((None, tm, D_MODEL), lambda i, j, k: (j, i, 0))
        out_shape = jax.ShapeDtypeStruct((4, m, D_MODEL), F32)
    aliases = {}
    if pack is not None:
        buf, rows, blk, kind = pack
        logical = (D_MODEL, 4 * D_MODEL) if kind == "cols" else (4 * D_MODEL, D_MODEL)
        assert epi is None and tm == D_MODEL and tn_ == D_MODEL and (m, nn_) == logical
        if kind == "cols":
            o_spec = pl.BlockSpec((None, D_MODEL, D_MODEL), lambda i, j, k: (j, blk, 0))
        else:
            o_spec = pl.BlockSpec((None, D_MODEL, D_MODEL), lambda i, j, k: (i, blk, 0))
        out_shape = jax.ShapeDtypeStruct((4, rows, D_MODEL), F32)
        if buf is not None:
            aliases = {len(ins): 0}
            ins.append(buf)
            specs.append(pl.BlockSpec(memory_space=pl.ANY))
    return pl.pallas_call(
        body, name=name, grid=(m // tm, nn_ // tn_, nk), in_specs=specs, out_specs=o_spec, out_shape=out_shape,
        input_output_aliases=aliases, compiler_params=_cparams(("parallel", "parallel", "arbitrary")),
    )(*ins)


ROWS = 256
POST_ROWS = 1024


def _ln_core(x, m, g, b):
    r = ALPHA * x + m
    mu = jnp.mean(r, axis=-1, keepdims=True)
    xc = r - mu
    var = jnp.mean(xc * xc, axis=-1, keepdims=True)
    rstd = lax.rsqrt(var + LN_EPS)
    xhat = xc * rstd
    return xhat * g + b, xhat, rstd


def ln_fwd(x, m, g, b):
    s, dm = x.shape

    def body(x_ref, m_ref, g_ref, b_ref, o_ref):
        o_ref[...] = _ln_core(x_ref[...], m_ref[...], g_ref[...], b_ref[...])[0]

    row = pl.BlockSpec((ROWS, dm), lambda i: (i, 0))
    vec = pl.BlockSpec((1, dm), lambda i: (0, 0))
    return pl.pallas_call(body, name="ln_fwd", grid=(s // ROWS,), in_specs=[row, row, vec, vec], out_specs=row,
                          out_shape=jax.ShapeDtypeStruct((s, dm), F32), compiler_params=_cparams(("parallel",)))(x, m, g, b)


def ln_bwd(x, m, g, dy):
    s, dm = x.shape

    def body(x_ref, m_ref, g_ref, dy_ref, dr_ref, dg_ref, db_ref):
        gv = g_ref[...]
        _, xhat, rstd = _ln_core(x_ref[...], m_ref[...], gv, jnp.zeros_like(gv))
        dy = dy_ref[...]
        dxh = dy * gv
        dr_ref[...] = rstd * (dxh - jnp.mean(dxh, axis=-1, keepdims=True)
                              - xhat * jnp.mean(dxh * xhat, axis=-1, keepdims=True))

        @pl.when(pl.program_id(0) == 0)
        def _():
            dg_ref[...] = jnp.zeros_like(dg_ref)
            db_ref[...] = jnp.zeros_like(db_ref)

        dg_ref[...] += jnp.sum(dy * xhat, axis=0, keepdims=True)
        db_ref[...] += jnp.sum(dy, axis=0, keepdims=True)

    row = pl.BlockSpec((ROWS, dm), lambda i: (i, 0))
    vec = pl.BlockSpec((1, dm), lambda i: (0, 0))
    return pl.pallas_call(body, name="ln_bwd", grid=(s // ROWS,), in_specs=[row, row, vec, row], out_specs=[row, vec, vec],
                          out_shape=[jax.ShapeDtypeStruct((s, dm), F32), jax.ShapeDtypeStruct((1, dm), F32),
                                     jax.ShapeDtypeStruct((1, dm), F32)],
                          compiler_params=_cparams(("arbitrary",)))(x, m, g, dy)


def loss_head(y, target):
    s, dm = y.shape

    def body(y_ref, t_ref, dy_ref, l_ref):
        e = y_ref[...] - t_ref[...]
        dy_ref[...] = e * (1.0 / dm)

        @pl.when(pl.program_id(0) == 0)
        def _():
            l_ref[...] = jnp.zeros_like(l_ref)

        col = jnp.sum(e * e, axis=0, keepdims=True) * (0.5 / dm)
        acc = col[:, :LANES]
        for c in range(1, dm // LANES):
            acc = acc + col[:, c * LANES:(c + 1) * LANES]
        l_ref[...] += acc

    row = pl.BlockSpec((ROWS, dm), lambda i: (i, 0))
    return pl.pallas_call(body, name="loss_head", grid=(s // ROWS,), in_specs=[row, row],
                          out_specs=[row, pl.BlockSpec((1, LANES), lambda i: (0, 0))],
                          out_shape=[jax.ShapeDtypeStruct((s, dm), F32), jax.ShapeDtypeStruct((1, LANES), F32)],
                          compiler_params=_cparams(("arbitrary",)))(y, target)


def _shift_rows_impl(x, d):
    n = x.shape[0]
    if d == 0:
        return x
    t = lax.broadcasted_iota(jnp.int32, x.shape, 0)
    return jnp.where((t + d >= 0) & (t + d < n), pltpu.roll(x, (-d) % n, 0), 0.0)


@functools.partial(jax.custom_vjp, nondiff_argnums=(1,))
def _shift_rows(x, d):
    return _shift_rows_impl(x, d)


_shift_rows.defvjp(lambda x, d: (_shift_rows_impl(x, d), None), lambda d, _, g: (_shift_rows_impl(g, -d),))


def _gdn_pre_fn(u, w, kind):
    rows = lax.broadcasted_iota(jnp.int32, w.shape, 0)
    c = None
    for tap in range(A_CONV):
        w_tap = jnp.sum(jnp.where(rows == tap, w, 0.0), axis=0, keepdims=True)
        term = _shift_rows(u, tap - A_CONV // 2) * w_tap
        c = term if c is None else c + term
    y = c * _sigmoid(c)
    if kind == "v":
        return y
    y = y * lax.rsqrt(jnp.sum(y * y, axis=-1, keepdims=True) + L2_EPS)
    return y * (A_DK ** -0.5) if kind == "q" else y


_KIND_OFF = {"q": 0, "k": A_HEADS, "v": 2 * A_HEADS}


def gdn_pre(proj, conv_w, kind):
    s = proj.shape[0]
    off = _KIND_OFF[kind]

    def body(u_ref, w_ref, o_ref):
        o_ref[...] = _gdn_pre_fn(u_ref[...], w_ref[...], kind)

    return pl.pallas_call(
        body, name="gdn_pre_" + kind, grid=(A_HEADS,),
        in_specs=[pl.BlockSpec((s, LANES), lambda h: (0, off + h)), pl.BlockSpec((A_CONV, LANES), lambda h: (0, off + h))],
        out_specs=pl.BlockSpec((s, LANES), lambda h: (0, h)),
        out_shape=jax.ShapeDtypeStruct((s, A_W), F32), compiler_params=_cparams(("parallel",)))(proj, conv_w)


def gdn_pre_bwd(proj, conv_w, dt2, kind):
    s = proj.shape[0]
    off = _KIND_OFF[kind]

    def body(u_ref, w_ref, d0_ref, d1_ref, du_ref, dw_ref):
        _, vjp = jax.vjp(functools.partial(_gdn_pre_fn, kind=kind), u_ref[...], w_ref[...])
        du, dw = vjp(d0_ref[...] + d1_ref[...])
        du_ref[...] = du
        dw_ref[...] = dw

    return pl.pallas_call(
        body, name="gdn_pre_bwd_" + kind, grid=(A_HEADS,),
        in_specs=[pl.BlockSpec((s, LANES), lambda h: (0, off + h)), pl.BlockSpec((A_CONV, LANES), lambda h: (0, off + h)),
                  pl.BlockSpec((None, s, LANES), lambda h: (0, 0, h)), pl.BlockSpec((None, s, LANES), lambda h: (1, 0, h))],
        out_specs=[pl.BlockSpec((s, LANES), lambda h: (0, h)), pl.BlockSpec((A_CONV, LANES), lambda h: (0, h))],
        out_shape=[jax.ShapeDtypeStruct((s, A_W), F32), jax.ShapeDtypeStruct((A_CONV, A_W), F32)],
        compiler_params=_cparams(("parallel",)))(proj, conv_w, dt2, dt2)


N_GATE = 2 * A_HEADS


def _gdn_gates_fn(ba, alog_row, dt_row):
    r = lax.broadcasted_iota(jnp.int32, (LANES, N_GATE * LANES), 0)
    c = lax.broadcasted_iota(jnp.int32, (LANES, N_GATE * LANES), 1) >> 7
    beta_b = xdot(_sigmoid(ba), (r == c).astype(F32))
    g = -(jnp.exp(alog_row) * _softplus(ba + dt_row))
    g_b = xdot(g, (r == c + N_GATE).astype(F32))
    return beta_b, g_b


def gdn_gates(ba, alog_row, dt_row):
    s = ba.shape[0]

    def body(ba_ref, al_ref, dt_ref, bb_ref, gb_ref):
        bb_ref[...], gb_ref[...] = _gdn_gates_fn(ba_ref[...], al_ref[...], dt_ref[...])

    row = pl.BlockSpec((ROWS, LANES), lambda i: (i, 0))
    vec = pl.BlockSpec((1, LANES), lambda i: (0, 0))
    wide = pl.BlockSpec((ROWS, N_GATE * LANES), lambda i: (i, 0))
    return pl.pallas_call(body, name="gdn_gates", grid=(s // ROWS,), in_specs=[row, vec, vec], out_specs=[wide, wide],
                          out_shape=[jax.ShapeDtypeStruct((s, N_GATE * LANES), F32)] * 2,
                          compiler_params=_cparams(("parallel",)))(ba, alog_row, dt_row)


def gdn_gates_bwd(ba, alog_row, dt_row, dbeta_b, dg_b):
    s = ba.shape[0]

    def body(ba_ref, al_ref, dt_ref, dbb_ref, dgb_ref, dba_ref, dal_ref, ddt_ref):
        _, vjp = jax.vjp(_gdn_gates_fn, ba_ref[...], al_ref[...], dt_ref[...])
        dba, dal, ddt = vjp((dbb_ref[...], dgb_ref[...]))
        dba_ref[...] = dba

        @pl.when(pl.program_id(0) == 0)
        def _():
            dal_ref[...] = jnp.zeros_like(dal_ref)
            ddt_ref[...] = jnp.zeros_like(ddt_ref)

        dal_ref[...] += dal
        ddt_ref[...] += ddt

    row = pl.BlockSpec((ROWS, LANES), lambda i: (i, 0))
    vec = pl.BlockSpec((1, LANES), lambda i: (0, 0))
    wide = pl.BlockSpec((ROWS, N_GATE * LANES), lambda i: (i, 0))
    return pl.pallas_call(body, name="gdn_gates_bwd", grid=(s // ROWS,), in_specs=[row, vec, vec, wide, wide],
                          out_specs=[row, vec, vec],
                          out_shape=[jax.ShapeDtypeStruct((s, LANES), F32), jax.ShapeDtypeStruct((1, LANES), F32),
                                     jax.ShapeDtypeStruct((1, LANES), F32)],
                          compiler_params=_cparams(("arbitrary",)))(ba, alog_row, dt_row, dbeta_b, dg_b)


def _post_fn(o, z, g):
    y = o * lax.rsqrt(jnp.mean(o * o, axis=-1, keepdims=True) + RMS_EPS) * g
    return y * (z * _sigmoid(z))


def mixer_post(o2, proj, norm_g, width, gate_off, name):
    s = o2.shape[1]
    nh = o2.shape[2] // width

    rows = _tile(s, POST_ROWS)

    def body(o0_ref, o1_ref, z_ref, g_ref, y_ref):
        y_ref[...] = _post_fn(o0_ref[...] + o1_ref[...], z_ref[...], g_ref[...])

    ospec = lambda d: pl.BlockSpec((None, rows, width), lambda i, h: (d, i, h))
    return pl.pallas_call(
        body, name=name, grid=(s // rows, nh),
        in_specs=[ospec(0), ospec(1), pl.BlockSpec((rows, width), lambda i, h: (i, gate_off + h)),
                  pl.BlockSpec((1, width), lambda i, h: (0, 0))],
        out_specs=pl.BlockSpec((rows, width), lambda i, h: (i, h)),
        out_shape=jax.ShapeDtypeStruct((s, o2.shape[2]), F32),
        compiler_params=_cparams(("parallel", "parallel")))(o2, o2, proj, norm_g)


def mixer_post_bwd(o2, proj, norm_g, dy, width, gate_off, name):
    s = o2.shape[1]
    nh = o2.shape[2] // width

    def body(o0_ref, o1_ref, z_ref, g_ref, dy_ref, do_ref, dz_ref, dg_ref):
        _, vjp = jax.vjp(_post_fn, o0_ref[...] + o1_ref[...], z_ref[...], g_ref[...])
        do, dz, dg = vjp(dy_ref[...])
        do_ref[...] = do
        dz_ref[...] = dz

        @pl.when((pl.program_id(0) == 0) & (pl.program_id(1) == 0))
        def _():
            dg_ref[...] = jnp.zeros_like(dg_ref)

        dg_ref[...] += dg

    rows = _tile(s, POST_ROWS)
    ospec = lambda d: pl.BlockSpec((None, rows, width), lambda i, h: (d, i, h))
    blk = pl.BlockSpec((rows, width), lambda i, h: (i, h))
    vec = pl.BlockSpec((1, width), lambda i, h: (0, 0))
    return pl.pallas_call(
        body, name=name, grid=(s // rows, nh),
        in_specs=[ospec(0), ospec(1), pl.BlockSpec((rows, width), lambda i, h: (i, gate_off + h)), vec, blk],
        out_specs=[blk, blk, vec],
        out_shape=[jax.ShapeDtypeStruct((s, o2.shape[2]), F32)] * 2 + [jax.ShapeDtypeStruct((1, width), F32)],
        compiler_params=_cparams(("arbitrary", "arbitrary")))(o2, o2, proj, norm_g, dy)


def _log_gate(z):
    return (jnp.minimum(z, 0.0) - jnp.log(1.0 + jnp.exp(-jnp.abs(z)))) * (1.0 / B_TAU)


def gla_gate(gl, w2, gb):
    s = gl.shape[0]

    def body(gl_ref, w_ref, b_ref, o_ref):
        for n in range(2):
            o_ref[n] = _log_gate(nn(gl_ref[...], w_ref[n]) + b_ref[n])

    full = lambda shp: pl.BlockSpec(shp, lambda i: (0,) * len(shp))
    return pl.pallas_call(
        body, name="gla_gate", grid=(s // ROWS,),
        in_specs=[pl.BlockSpec((ROWS, LANES), lambda i: (i, 0)), full(w2.shape), full(gb.shape)],
        out_specs=pl.BlockSpec((2, ROWS, B_KW), lambda i: (0, i, 0)),
        out_shape=jax.ShapeDtypeStruct((2, s, B_KW), F32), compiler_params=_cparams(("parallel",)))(gl, w2, gb)


def gla_gate_bwd(gl, w2, gb, dla):
    s = gl.shape[0]

    def body(gl_ref, w_ref, b_ref, dla_ref, dgl_ref, dz_ref, db0_ref, db1_ref):
        @pl.when(pl.program_id(0) == 0)
        def _():
            db0_ref[...] = jnp.zeros_like(db0_ref)
            db1_ref[...] = jnp.zeros_like(db1_ref)

        dgl = None
        for n, db_ref in enumerate((db0_ref, db1_ref)):
            _, vjp = jax.vjp(_log_gate, nn(gl_ref[...], w_ref[n]) + b_ref[n])
            dz, = vjp(dla_ref[n])
            dz_ref[n] = dz
            db_ref[...] += jnp.sum(dz, axis=0, keepdims=True)
            part = nt(dz, w_ref[n])
            dgl = part if dgl is None else dgl + part
        dgl_ref[...] = dgl

    full = lambda shp: pl.BlockSpec(shp, lambda i: (0,) * len(shp))
    row = pl.BlockSpec((ROWS, LANES), lambda i: (i, 0))
    wide = pl.BlockSpec((2, ROWS, B_KW), lambda i: (0, i, 0))
    vec = pl.BlockSpec((1, B_KW), lambda i: (0, 0))
    return pl.pallas_call(
        body, name="gla_gate_bwd", grid=(s // ROWS,),
        in_specs=[row, full(w2.shape), full(gb.shape), wide],
        out_specs=[row, wide, vec, vec],
        out_shape=[jax.ShapeDtypeStruct((s, LANES), F32), jax.ShapeDtypeStruct((2, s, B_KW), F32),
                   jax.ShapeDtypeStruct((1, B_KW), F32), jax.ShapeDtypeStruct((1, B_KW), F32)],
        compiler_params=_cparams(("arbitrary",)))(gl, w2, gb, dla)


PACK_TILE = 512


def cast_into_slot(x, chip):
    r, c = x.shape

    def body(chip_ref, x_ref, o_ref):
        o_ref[...] = x_ref[...].astype(BF16)

    return pl.pallas_call(
        body, name="cast_into_slot",
        grid_spec=pltpu.PrefetchScalarGridSpec(
            num_scalar_prefetch=1, grid=(r // PACK_TILE,),
            in_specs=[pl.BlockSpec((PACK_TILE, c), lambda i, chip_ref: (i, 0))],
            out_specs=pl.BlockSpec((None, PACK_TILE, c), lambda i, chip_ref: (chip_ref[0], i, 0))),
        out_shape=jax.ShapeDtypeStruct((4, r, c), BF16), compiler_params=_cparams(("parallel",)))(chip, x)


def sum_received(chip_sum, recv, chip, core):
    _, h, c = chip_sum.shape
    n = recv.shape[0]
    tr = _tile(h, PACK_TILE)
    nblk = h // tr

    def body(chip_ref, core_ref, own_ref, r_ref, o_ref):
        acc = r_ref[0].astype(F32)
        for k in range(1, n):
            acc = acc + r_ref[k].astype(F32)
        o_ref[...] = acc + own_ref[...].astype(F32)

    return pl.pallas_call(
        body, name="sum_received",
        grid_spec=pltpu.PrefetchScalarGridSpec(
            num_scalar_prefetch=2, grid=(nblk,),
            in_specs=[pl.BlockSpec((None, tr, c), lambda i, chip_ref, core_ref: (chip_ref[0], i, 0)),
                      pl.BlockSpec((n, tr, c), lambda i, chip_ref, core_ref: (0, i, 0))],
            out_specs=pl.BlockSpec((tr, c), lambda i, chip_ref, core_ref: (core_ref[0] * nblk + i, 0))),
        out_shape=jax.ShapeDtypeStruct((2 * h, c), F32), compiler_params=_cparams(("parallel",)))(chip, core, chip_sum, recv)


def merge_first_hop(chip_sum, passed, slot_x, slot_y, core):
    _, h, c = chip_sum.shape
    tr = _tile(h, PACK_TILE)

    def body(sx_ref, sy_ref, core_ref, to_x_ref, to_y_ref, p_ref, o_ref):
        p = p_ref[...].astype(F32)
        is_y = core_ref[0].astype(F32)
        o_ref[0] = (to_x_ref[...].astype(F32) + p * (1.0 - is_y)).astype(BF16)
        o_ref[1] = (to_y_ref[...].astype(F32) + p * is_y).astype(BF16)

    return pl.pallas_call(
        body, name="merge_first_hop",
        grid_spec=pltpu.PrefetchScalarGridSpec(
            num_scalar_prefetch=3, grid=(h // tr,),
            in_specs=[pl.BlockSpec((None, tr, c), lambda i, sx_ref, sy_ref, core_ref: (sx_ref[0], i, 0)),
                      pl.BlockSpec((None, tr, c), lambda i, sx_ref, sy_ref, core_ref: (sy_ref[0], i, 0)),
                      pl.BlockSpec((tr, c), lambda i, sx_ref, sy_ref, core_ref: (i, 0))],
            out_specs=pl.BlockSpec((2, tr, c), lambda i, sx_ref, sy_ref, core_ref: (0, i, 0))),
        out_shape=jax.ShapeDtypeStruct((2, h, c), BF16),
        compiler_params=_cparams(("parallel",)))(slot_x, slot_y, core, chip_sum, chip_sum, passed)


def sum_slots(x, name):
    n, r, c = x.shape
    tr = _tile(r, PACK_TILE)

    def body(x_ref, o_ref):
        acc = x_ref[0].astype(F32)
        for k in range(1, n):
            acc = acc + x_ref[k].astype(F32)
        o_ref[...] = acc

    return pl.pallas_call(body, name=name, grid=(r // tr,), in_specs=[pl.BlockSpec((n, tr, c), lambda i: (0, i, 0))],
                          out_specs=pl.BlockSpec((tr, c), lambda i: (i, 0)),
                          out_shape=jax.ShapeDtypeStruct((r, c), F32), compiler_params=_cparams(("parallel",)))(x)


def half_to_bf16(gpack, which, theirs=None):
    n, r, c = gpack.shape
    half_rows = r // 2
    tr = _tile(half_rows, PACK_TILE)
    nblk = half_rows // tr

    def body(which_ref, g_ref, *rest):
        o_ref = rest[-1]
        acc = g_ref[...]
        if theirs is not None:
            acc = acc + rest[0][...].astype(F32)
        o_ref[...] = acc.astype(BF16)

    blk = pl.BlockSpec((None, tr, c), lambda s, i, which_ref: (s, i, 0))
    ins = [gpack] if theirs is None else [gpack, theirs]
    return pl.pallas_call(
        body, name="half_to_bf16" if theirs is None else "add_sibling_half",
        grid_spec=pltpu.PrefetchScalarGridSpec(
            num_scalar_prefetch=1, grid=(n, nblk),
            in_specs=[pl.BlockSpec((None, tr, c), lambda s, i, which_ref: (s, which_ref[0] * nblk + i, 0))]
            + [blk] * (len(ins) - 1),
            out_specs=blk),
        out_shape=jax.ShapeDtypeStruct((n, half_rows, c), BF16),
        compiler_params=_cparams(("parallel", "parallel")))(which, *ins)


def adamw(w, m, v, grads, g_row_off, name):
    r, c = w.shape
    tr = next(t for t in (PACK_TILE, r) if r % t == 0 and g_row_off % t == 0)
    ob = g_row_off // tr
    ng = len(grads)

    def body(*refs):
        w_ref, m_ref, v_ref = refs[:3]
        g_refs = refs[3:3 + ng]
        g_ref, d_ref, nm_ref, nv_ref = refs[3 + ng:]
        g = g_refs[0][...]
        for gr in g_refs[1:]:
            g = g + gr[...]
        m_new = ADAM_B1 * m_ref[...] + (1.0 - ADAM_B1) * g
        v_new = ADAM_B2 * v_ref[...] + (1.0 - ADAM_B2) * jnp.square(g)
        m_hat = m_new / (1.0 - ADAM_B1 ** ADAM_STEP)
        v_hat = v_new / (1.0 - ADAM_B2 ** ADAM_STEP)
        g_ref[...] = g
        d_ref[...] = -ADAM_LR * (m_hat / (jnp.sqrt(v_hat) + ADAM_EPS) + ADAM_WD * w_ref[...])
        nm_ref[...] = m_new
        nv_ref[...] = v_new

    blk = pl.BlockSpec((tr, c), lambda i: (i, 0))
    gblk = pl.BlockSpec((tr, c), lambda i: (i + ob, 0))
    return pl.pallas_call(body, name=name, grid=(r // tr,), in_specs=[blk, blk, blk] + [gblk] * ng, out_specs=[blk] * 4,
                          out_shape=[jax.ShapeDtypeStruct((r, c), F32)] * 4,
                          compiler_params=_cparams(("parallel",)))(w, m, v, *grads)


MESH = pl.DeviceIdType.MESH
HBM = pl.BlockSpec(memory_space=pl.ANY)
CHIP_FLIPS = ((1, 0), (0, 1), (1, 1))


def _place():
    return lax.axis_index("x"), lax.axis_index("y"), lax.axis_index("c")


def allgather_chips(buf):
    _, r, c = buf.shape
    half_rows = r // 2

    def body(_, out_ref, send_sems, recv_sems):
        x, y, cc = _place()
        half = pl.ds(cc * half_rows, half_rows)
        other = pl.ds((1 - cc) * half_rows, half_rows)

        def copy(k, rows, to):
            return pltpu.make_async_remote_copy(src_ref=rows, dst_ref=rows, send_sem=send_sems.at[k],
                                                recv_sem=recv_sems.at[k], device_id=to, device_id_type=MESH)

        nbr_x, nbr_y, diag = (1 - x, y), (x, 1 - y), (1 - x, 1 - y)
        slot = lambda chip: 2 * chip[0] + chip[1]
        sibling = (x, y, 1 - cc)
        first = [copy(0, out_ref.at[slot((x, y)), half], (*nbr_x, cc)), copy(1, out_ref.at[slot((x, y)), half], (*nbr_y, cc))]
        for cp in first:
            cp.start()
        passed = []
        for k, chip in enumerate((nbr_x, nbr_y)):
            landed = out_ref.at[slot(chip), half]
            copy(k, landed, (*chip, cc)).wait_recv()
            passed.append(copy(3 + k, landed, sibling))
            passed[-1].start()
        via = (1 - x + cc * (2 * x - 1), y + cc * (1 - 2 * y))
        to = (x + cc * (1 - 2 * x), 1 - y + cc * (2 * y - 1))
        hop = copy(2, out_ref.at[slot(via), half], (*to, cc))
        hop.start()
        landed = out_ref.at[slot(diag), half]
        copy(2, landed, (*to, cc)).wait_recv()
        passed.append(copy(5, landed, sibling))
        passed[-1].start()
        for k, chip in enumerate((nbr_x, nbr_y, diag)):
            copy(3 + k, out_ref.at[slot(chip), other], sibling).wait_recv()
        for cp in first + [hop] + passed:
            cp.wait_send()

    return pl.pallas_call(
        body, name="allgather_chips", in_specs=[HBM], out_specs=HBM, input_output_aliases={0: 0},
        out_shape=jax.ShapeDtypeStruct(buf.shape, buf.dtype),
        scratch_shapes=[pltpu.SemaphoreType.DMA((6,)), pltpu.SemaphoreType.DMA((6,))],
    )(buf)


def scatter_first_hop(gpack):
    _, r, c = gpack.shape

    def body(src_ref, out_ref, send_sem, recv_sem):
        x, y, cc = _place()
        to = (x + cc * (1 - 2 * x), 1 - y + cc * (2 * y - 1), cc)
        cp = pltpu.make_async_remote_copy(src_ref=src_ref.at[2 * (1 - x) + (1 - y)], dst_ref=out_ref, send_sem=send_sem,
                                          recv_sem=recv_sem, device_id=to, device_id_type=MESH)
        cp.start()
        cp.wait()

    return pl.pallas_call(
        body, name="scatter_first_hop", in_specs=[HBM], out_specs=HBM,
        out_shape=jax.ShapeDtypeStruct((r, c), gpack.dtype),
        scratch_shapes=[pltpu.SemaphoreType.DMA, pltpu.SemaphoreType.DMA],
    )(gpack)


def scatter_second_hop(to_nbrs):
    def body(src_ref, out_ref, send_sems, recv_sems):
        x, y, cc = _place()
        sends = [pltpu.make_async_remote_copy(src_ref=src_ref.at[k], dst_ref=out_ref.at[k], send_sem=send_sems.at[k],
                                              recv_sem=recv_sems.at[k], device_id=to, device_id_type=MESH)
                 for k, to in enumerate(((1 - x, y, cc), (x, 1 - y, cc)))]
        for cp in sends:
            cp.start()
        for cp in sends:
            cp.wait_recv()
        for cp in sends:
            cp.wait_send()

    return pl.pallas_call(
        body, name="scatter_second_hop", in_specs=[HBM], out_specs=HBM,
        out_shape=jax.ShapeDtypeStruct(to_nbrs.shape, to_nbrs.dtype),
        scratch_shapes=[pltpu.SemaphoreType.DMA((2,)), pltpu.SemaphoreType.DMA((2,))],
    )(to_nbrs)


def swap_sibling(mine):
    def body(src_ref, out_ref, send_sem, recv_sem):
        x, y, cc = _place()
        cp = pltpu.make_async_remote_copy(src_ref=src_ref, dst_ref=out_ref, send_sem=send_sem, recv_sem=recv_sem,
                                          device_id=(x, y, 1 - cc), device_id_type=MESH)
        cp.start()
        cp.wait()

    return pl.pallas_call(
        body, name="swap_sibling", in_specs=[HBM], out_specs=HBM,
        out_shape=jax.ShapeDtypeStruct(mine.shape, mine.dtype),
        scratch_shapes=[pltpu.SemaphoreType.DMA, pltpu.SemaphoreType.DMA],
    )(mine)


def join_halves(buf):
    r, c = buf.shape
    half_rows = r // 2

    def body(_, out_ref, send_sem, recv_sem):
        x, y, cc = _place()
        half = out_ref.at[pl.ds(cc * half_rows, half_rows)]
        other = out_ref.at[pl.ds((1 - cc) * half_rows, half_rows)]
        send = pltpu.make_async_remote_copy(src_ref=half, dst_ref=half, send_sem=send_sem, recv_sem=recv_sem,
                                            device_id=(x, y, 1 - cc), device_id_type=MESH)
        send.start()
        pltpu.make_async_remote_copy(src_ref=other, dst_ref=other, send_sem=send_sem, recv_sem=recv_sem,
                                     device_id=(x, y, 1 - cc), device_id_type=MESH).wait_recv()
        send.wait_send()

    return pl.pallas_call(
        body, name="join_halves", in_specs=[HBM], out_specs=HBM, input_output_aliases={0: 0},
        out_shape=jax.ShapeDtypeStruct(buf.shape, buf.dtype),
        scratch_shapes=[pltpu.SemaphoreType.DMA, pltpu.SemaphoreType.DMA],
    )(buf)


def exchange_all(v, name):
    r, c = v.shape

    def body(v_ref, out_ref, send_sems, recv_sems):
        x, y, cc = _place()
        out_ref[4 * x + 2 * y + cc] = v_ref[...]
        sends, recvs = [], []
        for k in range(1, 8):
            px = 1 - x if k & 4 else x
            py = 1 - y if k & 2 else y
            pc = 1 - cc if k & 1 else cc
            sends.append(pltpu.make_async_remote_copy(
                src_ref=v_ref, dst_ref=out_ref.at[4 * x + 2 * y + cc], send_sem=send_sems.at[k - 1],
                recv_sem=recv_sems.at[k - 1], device_id=(px, py, pc), device_id_type=MESH))
            recvs.append(pltpu.make_async_remote_copy(
                src_ref=v_ref, dst_ref=out_ref.at[4 * px + 2 * py + pc], send_sem=send_sems.at[k - 1],
                recv_sem=recv_sems.at[k - 1], device_id=(px, py, pc), device_id_type=MESH))
        for cp in sends:
            cp.start()
        for cp in recvs:
            cp.wait_recv()
        for cp in sends:
            cp.wait_send()

    vm = pl.BlockSpec(memory_space=pltpu.VMEM)
    return pl.pallas_call(
        body, name=name, in_specs=[vm], out_specs=vm, out_shape=jax.ShapeDtypeStruct((8, r, c), v.dtype),
        scratch_shapes=[pltpu.SemaphoreType.DMA((7,)), pltpu.SemaphoreType.DMA((7,))],
        compiler_params=pltpu.CompilerParams(vmem_limit_bytes=VMEM_LIMIT),
    )(v)


def _as_rows(a, width):
    n = math.prod(a.shape)
    if n % width == 0:
        return a.reshape(-1, width)
    return jnp.pad(a.reshape(1, -1), ((0, 0), (0, -n % width))).reshape(-1, width)


def _n_rows(shape, width):
    return -(-math.prod(shape) // width)


def _pack_rows(arrays, rows, width):
    parts = [_as_rows(a, width) for a in arrays]
    used = sum(p.shape[0] for p in parts)
    return jnp.concatenate(parts + [jnp.zeros((rows - used, width), arrays[0].dtype)], axis=0)


def _unpack_rows(pack, shapes):
    width = pack.shape[1]
    out, off = [], 0
    for shp in shapes:
        nr, n = _n_rows(shp, width), math.prod(shp)
        part = pack[off:off + nr]
        out.append(part.reshape(shp) if n % width == 0 else part.reshape(-1)[:n].reshape(shp))
        off += nr
    return out


def _rows_for(shapes, width, mult=8):
    n = sum(_n_rows(s, width) for s in shapes)
    return -(-n // mult) * mult


def _gdn_fwd(x, p):
    proj = mm(x, p["w_main"], name="gdn_proj")
    ba = mm(x, p["w_gate"], name="gdn_proj_gate")
    q, k, v = (gdn_pre(proj, p["conv"], kind) for kind in "qkv")
    beta_b, g_b = gdn_gates(ba, p["alog_row"], p["dt_row"])
    o2, st, tinv = gdn_rec_fwd(q, k, v, beta_b, g_b)
    y = mixer_post(o2, proj, p["norm_g"], A_DK, 3 * A_HEADS, "gdn_post")
    m = mm(y, p["w_out"], name="gdn_out")
    return m, (x, proj, ba, q, k, v, beta_b, g_b, o2, st, tinv, y)


def _gdn_bwd(saved, p, dm):
    x, proj, ba, q, k, v, beta_b, g_b, o2, st, tinv, y = saved
    d_w_out = mm(y, dm, "tn", name="gdn_dw_out")
    dy = mm(dm, p["w_out"], "nt", name="gdn_dy")
    do, dz, d_norm_g = mixer_post_bwd(o2, proj, p["norm_g"], dy, A_DK, 3 * A_HEADS, "gdn_post_bwd")
    dq2, dk2, dv2, dbb, dgb = gdn_rec_bwd(q, k, v, beta_b, g_b, st, tinv, do)
    dba, d_alog_row, d_dt_row = gdn_gates_bwd(ba, p["alog_row"], p["dt_row"], dbb, dgb)
    du, dconv = zip(*(gdn_pre_bwd(proj, p["conv"], d2, kind) for d2, kind in ((dq2, "q"), (dk2, "k"), (dv2, "v"))))
    dproj = jnp.concatenate(list(du) + [dz], axis=1)
    d_w_main = mm(x, dproj, "tn", name="gdn_dw_main")
    d_w_gate = mm(x, dba, "tn", name="gdn_dw_gate")
    dx = mm(dba, p["w_gate"], "nt", epi="add", extra=dm, alpha=ALPHA, name="gdn_dx_gate")
    dx = mm(dproj, p["w_main"], "nt", epi="add", extra=dx, name="gdn_dx")
    grads = dict(w_in=jnp.concatenate([d_w_main, d_w_gate[:, :2 * N_GATE]], axis=1), conv=jnp.concatenate(dconv, axis=1),
                 alog=d_alog_row[0, N_GATE:2 * N_GATE].reshape(2, A_HEADS), dt=d_dt_row[0, N_GATE:2 * N_GATE].reshape(2, A_HEADS),
                 norm_g=d_norm_g[0], w_out=d_w_out)
    return dx, grads


def _gla_fwd(x, p):
    proj = mm(x, p["w_main"], name="gla_proj")
    gl = mm(x, p["w_gate"], name="gla_proj_gate")
    log_a = gla_gate(gl, p["w2"], p["gate_b"])
    o2, st = gla_rec_fwd(proj, log_a)
    y = mixer_post(o2, proj, p["norm_g"], B_DV, (2 * B_KW + B_VW) // B_DV, "gla_post")
    m = mm(y, p["w_out"], name="gla_out")
    return m, (x, proj, gl, log_a, o2, st, y)


def _gla_bwd(saved, p, dm):
    x, proj, gl, log_a, o2, st, y = saved
    d_w_out = mm(y, dm, "tn", name="gla_dw_out")
    dy = mm(dm, p["w_out"], "nt", name="gla_dy")
    do, dr, d_norm_g = mixer_post_bwd(o2, proj, p["norm_g"], dy, B_DV, (2 * B_KW + B_VW) // B_DV, "gla_post_bwd")
    dq2, dk2, dv2, dla = gla_rec_bwd(proj, log_a, st, do)
    dgl, dz, d_b0, d_b1 = gla_gate_bwd(gl, p["w2"], p["gate_b"], dla)
    d_w2 = [mm(gl, dz[n], "tn", name="gla_dw_gate_w2") for n in range(2)]
    dproj = jnp.concatenate([dq2[0] + dq2[1], dk2[0] + dk2[1], dv2[0] + dv2[1], dr], axis=1)
    d_w_main = mm(x, dproj, "tn", name="gla_dw_main")
    d_w_gate = mm(x, dgl, "tn", name="gla_dw_gate")
    dx = mm(dgl, p["w_gate"], "nt", epi="add", extra=dm, alpha=ALPHA, name="gla_dx_gate")
    dx = mm(dproj, p["w_main"], "nt", epi="add", extra=dx, name="gla_dx")
    grads = dict(w_in=jnp.concatenate([d_w_main, d_w_gate[:, :2 * B_RANK]], axis=1),
                 gate_w2=jnp.stack([d_w2[n][n * B_RANK:(n + 1) * B_RANK] for n in range(2)]),
                 gate_b=jnp.concatenate([d_b0, d_b1]), norm_g=d_norm_g[0], w_out=d_w_out)
    return dx, grads


def _pad_cols(w, width=LANES):
    return jnp.pad(w, ((0, 0), (0, width - w.shape[1])))


def _local_step(x, target, a_w_in, a_conv, a_alog, a_dt_bias, a_norm_g, a_w_out, b_w_in, b_gate_w2, b_gate_b, b_norm_g,
                b_w_out, ln1_g, ln1_b, mlp_w1, mlp_w2, ln2_g, ln2_b, grad_pack=None):
    layer_p = []
    for i in range(DEPTH):
        j = i // 2
        if i % 2 == 0:
            layer_p.append(dict(
                w_main=a_w_in[j][:, :4 * A_W], w_gate=_pad_cols(a_w_in[j][:, 4 * A_W:]), conv=a_conv[j],
                alog_row=jnp.pad(a_alog[j].reshape(1, N_GATE), ((0, 0), (N_GATE, LANES - 2 * N_GATE))),
                dt_row=jnp.pad(a_dt_bias[j].reshape(1, N_GATE), ((0, 0), (N_GATE, LANES - 2 * N_GATE))),
                norm_g=a_norm_g[j].reshape(1, A_DK), w_out=a_w_out[j]))
        else:
            w2 = jnp.stack([jnp.pad(b_gate_w2[j][n], ((n * B_RANK, LANES - (n + 1) * B_RANK), (0, 0))) for n in range(2)])
            layer_p.append(dict(
                w_main=b_w_in[j][:, :2 * B_KW + 2 * B_VW], w_gate=_pad_cols(b_w_in[j][:, 2 * B_KW + 2 * B_VW:]),
                w2=w2, gate_b=b_gate_b[j].reshape(2, 1, B_KW), norm_g=b_norm_g[j].reshape(1, B_DV), w_out=b_w_out[j]))

    saved = []
    h = x
    for i in range(DEPTH):
        p = layer_p[i]
        m, sv = (_gdn_fwd if i % 2 == 0 else _gla_fwd)(h, p)
        x1 = ln_fwd(h, m, ln1_g[i:i + 1], ln1_b[i:i + 1])
        h1 = mm(x1, mlp_w1[i], name="mlp_up")
        mlp = mm(h1, mlp_w2[i], act="sqrelu", name="mlp_down")
        x2 = ln_fwd(x1, mlp, ln2_g[i:i + 1], ln2_b[i:i + 1])
        saved.append((sv, h, m, x1, h1, mlp))
        h = x2

    dh, loss_part = loss_head(h, target)

    g_a, g_b, g_ln1g, g_ln1b, g_ln2g, g_ln2b, g_w1, g_w2 = {}, {}, {}, {}, {}, {}, {}, {}
    pack = None
    for i in reversed(range(DEPTH)):
        sv, xin, m, x1, h1, mlp = saved[i]
        p = layer_p[i]
        dr2, g_ln2g[i], g_ln2b[i] = ln_bwd(x1, mlp, ln2_g[i:i + 1], dh)
        if grad_pack is None:
            g_w2[i] = mm(h1, dr2, "tn", act="sqrelu", name="mlp_dw_down")
        else:
            pack = mm(h1, dr2, "tn", act="sqrelu", pack=(pack, grad_pack[0], grad_pack[2] + i, "rows"), name="mlp_dw_down")
        dh1 = mm(dr2, mlp_w2[i], "nt", epi="dsqrelu", extra=h1, name="mlp_dh")
        if grad_pack is None:
            g_w1[i] = mm(x1, dh1, "tn", chip_major=True, name="mlp_dw_up")
        else:
            pack = mm(x1, dh1, "tn", pack=(pack, grad_pack[0], grad_pack[1] + i, "cols"), name="mlp_dw_up")
        dx1 = mm(dh1, mlp_w1[i], "nt", epi="add", extra=dr2, alpha=ALPHA, name="mlp_dx")
        dr1, g_ln1g[i], g_ln1b[i] = ln_bwd(xin, m, ln1_g[i:i + 1], dx1)
        dh, g = (_gdn_bwd if i % 2 == 0 else _gla_bwd)(sv, p, dr1)
        (g_a if i % 2 == 0 else g_b)[i // 2] = g

    per_layer = lambda d, key=None: [(d[i] if key is None else d[i][key]) for i in sorted(d)]
    st = lambda d, key=None: jnp.stack(per_layer(d, key))
    grads = dict(
        a_w_in=per_layer(g_a, "w_in"), a_conv=st(g_a, "conv"), a_alog=st(g_a, "alog"), a_dt_bias=st(g_a, "dt"),
        a_norm_g=st(g_a, "norm_g"), a_w_out=per_layer(g_a, "w_out"), b_w_in=per_layer(g_b, "w_in"),
        b_gate_w2=st(g_b, "gate_w2"), b_gate_b=st(g_b, "gate_b"), b_norm_g=st(g_b, "norm_g"),
        b_w_out=per_layer(g_b, "w_out"), ln1_g=st(g_ln1g)[:, 0], ln1_b=st(g_ln1b)[:, 0], mlp_w1=per_layer(g_w1),
        mlp_w2=per_layer(g_w2), ln2_g=st(g_ln2g)[:, 0], ln2_b=st(g_ln2b)[:, 0], pack=pack)
    return loss_part, dh, grads


WEIGHTS = ("a_w_in", "a_conv", "a_alog", "a_dt_bias", "a_norm_g", "a_w_out", "b_w_in", "b_gate_w2", "b_gate_b",
           "b_norm_g", "b_w_out", "ln1_g", "ln1_b", "mlp_w1", "mlp_w2", "ln2_g", "ln2_b")
BIG = ("mlp_w1", "mlp_w2", "a_w_out", "b_w_out", "a_w_in", "b_w_in")
SHARD_AXIS = {"mlp_w1": 2, "mlp_w2": 1, "a_w_out": 1, "b_w_out": 1, "a_w_in": 2, "b_w_in": 2}
SMALL = tuple(n for n in WEIGHTS if n not in BIG)
SMALL_SHARD_AXIS = {"a_conv": 2, "b_gate_w2": 3, "b_gate_b": 2, "b_norm_g": 1}


def _to_chip_major(full, axis):
    shp = full.shape
    t = full.reshape(shp[:axis] + (4, shp[axis] // 4) + shp[axis + 1:])
    return jnp.moveaxis(t, axis, 0)


def _from_chip_major(stacked, axis):
    t = jnp.moveaxis(stacked, 0, axis)
    shp = t.shape
    return t.reshape(shp[:axis] + (shp[axis] * shp[axis + 1],) + shp[axis + 2:])


def kernel(x, a_w_in, a_conv, a_alog, a_dt_bias, a_norm_g, a_w_out, b_w_in, b_gate_w2, b_gate_b, b_norm_g, b_w_out, ln1_g, ln1_b, mlp_w1, mlp_w2, ln2_g, ln2_b, loss_target, m_a_w_in, m_a_conv, m_a_alog, m_a_dt_bias, m_a_norm_g, m_a_w_out, m_b_w_in, m_b_gate_w2, m_b_gate_b, m_b_norm_g, m_b_w_out, m_ln1_g, m_ln1_b, m_mlp_w1, m_mlp_w2, m_ln2_g, m_ln2_b, v_a_w_in, v_a_conv, v_a_alog, v_a_dt_bias, v_a_norm_g, v_a_w_out, v_b_w_in, v_b_gate_w2, v_b_gate_b, v_b_norm_g, v_b_w_out, v_ln1_g, v_ln1_b, v_mlp_w1, v_mlp_w2, v_ln2_g, v_ln2_b):
    w = dict(a_w_in=a_w_in, a_conv=a_conv, a_alog=a_alog, a_dt_bias=a_dt_bias, a_norm_g=a_norm_g, a_w_out=a_w_out,
             b_w_in=b_w_in, b_gate_w2=b_gate_w2, b_gate_b=b_gate_b, b_norm_g=b_norm_g, b_w_out=b_w_out, ln1_g=ln1_g,
             ln1_b=ln1_b, mlp_w1=mlp_w1, mlp_w2=mlp_w2, ln2_g=ln2_g, ln2_b=ln2_b)
    mom = dict(a_w_in=m_a_w_in, a_conv=m_a_conv, a_alog=m_a_alog, a_dt_bias=m_a_dt_bias, a_norm_g=m_a_norm_g,
               a_w_out=m_a_w_out, b_w_in=m_b_w_in, b_gate_w2=m_b_gate_w2, b_gate_b=m_b_gate_b, b_norm_g=m_b_norm_g,
               b_w_out=m_b_w_out, ln1_g=m_ln1_g, ln1_b=m_ln1_b, mlp_w1=m_mlp_w1, mlp_w2=m_mlp_w2, ln2_g=m_ln2_g,
               ln2_b=m_ln2_b)
    var = dict(a_w_in=v_a_w_in, a_conv=v_a_conv, a_alog=v_a_alog, a_dt_bias=v_a_dt_bias, a_norm_g=v_a_norm_g,
               a_w_out=v_a_w_out, b_w_in=v_b_w_in, b_gate_w2=v_b_gate_w2, b_gate_b=v_b_gate_b, b_norm_g=v_b_norm_g,
               b_w_out=v_b_w_out, ln1_g=v_ln1_g, ln1_b=v_ln1_b, mlp_w1=v_mlp_w1, mlp_w2=v_mlp_w2, ln2_g=v_ln2_g,
               ln2_b=v_ln2_b)
    chip = 2 * lax.axis_index("x") + lax.axis_index("y")

    seg_rows = [w[n].size // D_MODEL for n in BIG]
    seg_off = [sum(seg_rows[:i]) for i in range(len(BIG))]
    rows = -(-sum(seg_rows) // PACK_TILE) * PACK_TILE
    shard_pack = jnp.concatenate([w[n].reshape(-1, D_MODEL) for n in BIG]
                                 + [jnp.zeros((rows - sum(seg_rows), D_MODEL), F32)], axis=0)
    chip_idx = chip.astype(jnp.int32).reshape(1)
    gathered = allgather_chips(cast_into_slot(shard_pack, chip_idx))
    full = {}
    for n, off, nr in zip(BIG, seg_off, seg_rows):
        if n in ("mlp_w1", "mlp_w2"):
            kind = "cols" if SHARD_AXIS[n] == 2 else "rows"
            full[n] = [Gathered(gathered, off + i * D_MODEL, kind) for i in range(DEPTH)]
            continue
        stacked = gathered[:, off:off + nr].reshape((4,) + w[n].shape)
        full[n] = _from_chip_major(stacked, SHARD_AXIS[n])
    sharded_small = tuple(SMALL_SHARD_AXIS)
    sm_shapes = [w[n].shape for n in sharded_small]
    sm_rows = _rows_for(sm_shapes, LANES)
    sm_all = exchange_all(_pack_rows([w[n] for n in sharded_small], sm_rows, LANES), "gather_small")
    per_chip = [_unpack_rows(sm_all[2 * pch], sm_shapes) for pch in range(4)]
    for idx, n in enumerate(sharded_small):
        full[n] = jnp.concatenate([per_chip[pch][idx] for pch in range(4)], axis=SMALL_SHARD_AXIS[n])
    for n in WEIGHTS:
        full.setdefault(n, w[n])

    blk_of = {n: off // D_MODEL for n, off in zip(BIG, seg_off)}
    loss_part, grad_x, grads = _local_step(x[0], loss_target[0], *[full[n] for n in WEIGHTS],
                                           grad_pack=(rows, blk_of["mlp_w1"], blk_of["mlp_w2"]))
    loss = lax.psum(jnp.sum(loss_part), ("x", "y", "c"))

    gpack = grads["pack"]
    rest = jnp.concatenate(
        [_to_chip_major(g, SHARD_AXIS[n] - 1).reshape(4, -1, D_MODEL) for n in BIG[2:] for g in grads[n]]
        + [jnp.zeros((4, rows - sum(seg_rows), D_MODEL), F32)], axis=1)
    gpack = lax.dynamic_update_slice(gpack, rest, (0, seg_off[2], 0))
    core = lax.axis_index("c").astype(jnp.int32).reshape(1)
    theirs = swap_sibling(half_to_bf16(gpack, 1 - core))
    chip_sum = half_to_bf16(gpack, core, theirs)
    ax, ay = lax.axis_index("x"), lax.axis_index("y")
    slot_x = (2 * (1 - ax) + ay).astype(jnp.int32).reshape(1)
    slot_y = (2 * ax + (1 - ay)).astype(jnp.int32).reshape(1)
    to_nbrs = merge_first_hop(chip_sum, scatter_first_hop(chip_sum), slot_x, slot_y, core)
    reduced = join_halves(sum_received(chip_sum, scatter_second_hop(to_nbrs), chip_idx, core))
    out_g, out_d, out_m, out_v = {}, {}, {}, {}
    for n, off, nr in zip(BIG, seg_off, seg_rows):
        if w[n].shape[-1] == D_MODEL:
            view = lambda t: t.reshape(-1, D_MODEL)
            res = adamw(view(w[n]), view(mom[n]), view(var[n]), (reduced,), off, "adamw_" + n)
        else:
            cols = w[n].shape[-1]
            view = lambda t: t.reshape(-1, cols)
            res = adamw(view(w[n]), view(mom[n]), view(var[n]), (view(reduced[off:off + nr]),), 0, "adamw_" + n)
        out_g[n], out_d[n], out_m[n], out_v[n] = (t.reshape(w[n].shape) for t in res)

    all_shapes = [full[n].shape for n in SMALL]
    g_rows = _rows_for(all_shapes, LANES)
    g_all = exchange_all(_pack_rows([grads[n] for n in SMALL], g_rows, LANES), "gather_small_grads")
    g_sum = _unpack_rows(sum_slots(g_all, "sum_small_grads"), all_shapes)
    g_mine = []
    for n, g in zip(SMALL, g_sum):
        if n in SMALL_SHARD_AXIS:
            ax = SMALL_SHARD_AXIS[n]
            g = lax.dynamic_slice_in_dim(g, chip * w[n].shape[ax], w[n].shape[ax], axis=ax)
        g_mine.append(g)
    my_shapes = [w[n].shape for n in SMALL]
    s_rows = _rows_for(my_shapes, LANES)
    pk = lambda d: _pack_rows([d[n] for n in SMALL], s_rows, LANES)
    res = adamw(pk(w), pk(mom), pk(var), (_pack_rows(g_mine, s_rows, LANES),), 0, "adamw_small")
    for dst, pack in zip((out_g, out_d, out_m, out_v), res):
        for n, t in zip(SMALL, _unpack_rows(pack, my_shapes)):
            dst[n] = t

    return (loss, grad_x[None], *[out_g[n] for n in WEIGHTS], *[out_d[n] for n in WEIGHTS],
            *[out_m[n] for n in WEIGHTS], *[out_v[n] for n in WEIGHTS])
```

```python
import functools
import math

import jax
import jax.numpy as jnp
from jax import lax
from jax.experimental import pallas as pl
from jax.experimental.pallas import tpu as pltpu

F32 = jnp.float32
BF16 = jnp.bfloat16

D_MODEL = 1024
DEPTH = 4
CHUNK = 64
A_HEADS = 8
A_DK = 128
A_W = 1024
A_CONV = 5
B_HEADS = 4
B_DK = 128
B_DV = 256
B_RANK = 16
B_TAU = 16.0
B_KW = 512
B_VW = 1024
ALPHA = (2 * DEPTH) ** 0.25
LN_EPS = 1e-5
RMS_EPS = 1e-6
L2_EPS = 1e-6
ADAM_LR = 0.001
ADAM_B1 = 0.9
ADAM_B2 = 0.999
ADAM_EPS = 1e-08
ADAM_WD = 0.01
ADAM_STEP = 10
LANES = 128
NEG_INF = float("-inf")
VMEM_LIMIT = 56 * 1024 * 1024


def _cparams(sem=None):
    return pltpu.CompilerParams(dimension_semantics=sem, vmem_limit_bytes=VMEM_LIMIT)


def _dg(a, b, ca, cb):
    return lax.dot_general(a.astype(BF16), b.astype(BF16), (((ca,), (cb,)), ((), ())),
                           preferred_element_type=F32)


def _split(x):
    hi = x.astype(BF16)
    return hi, (x - hi.astype(F32)).astype(BF16)


def _dg3(a, b, ca, cb):
    (a1, a2), (b1, b2) = _split(a), _split(b)
    return (_dg(a1, b2, ca, cb) + _dg(a2, b1, ca, cb)) + _dg(a1, b1, ca, cb)


def _dot_with_vjp(dg):
    @functools.partial(jax.custom_vjp, nondiff_argnums=(2, 3))
    def dot(a, b, ca, cb):
        return dg(a, b, ca, cb)

    def fwd(a, b, ca, cb):
        return dg(a, b, ca, cb), (a, b)

    def bwd(ca, cb, res, g):
        a, b = res
        da = dg(g, b, 1, 1 - cb) if ca == 1 else dg(b, g, 1 - cb, 1)
        db = dg(a, g, 1 - ca, 0) if cb == 0 else dg(g, a, 0, 1 - ca)
        return da, db

    dot.defvjp(fwd, bwd)
    return dot


bdot = _dot_with_vjp(_dg)
xdot3 = _dot_with_vjp(_dg3)


def nn(a, b):
    return bdot(a, b, 1, 0)


def nt(a, b):
    return bdot(a, b, 1, 1)


def tn(a, b):
    return bdot(a, b, 0, 0)


def xdot(a, b):
    return xdot3(a, b, 1, 0)


def _sigmoid(x):
    return 1.0 / (1.0 + jnp.exp(-x))


def _softplus(x):
    return jnp.maximum(x, 0.0) + jnp.log(1.0 + jnp.exp(-jnp.abs(x)))


def _chunk_masks(rev):
    ii = lax.broadcasted_iota(jnp.int32, (CHUNK, CHUNK), 0)
    jj = lax.broadcasted_iota(jnp.int32, (CHUNK, CHUNK), 1)
    d = (ii - jj) * (1 - 2 * rev)
    return d >= 0, d > 0, ii == jj, (ii >> 3) == (jj >> 3)


def _each(f, *lists):
    return [f(*xs) for xs in zip(*lists)]


@jax.custom_vjp
def _unit_triangular_inverse(a, ident, blockdiag):
    return _unit_triangular_inverse_impl(a, ident, blockdiag)


def _unit_triangular_inverse_fwd(a, ident, blockdiag):
    t = _unit_triangular_inverse_impl(a, ident, blockdiag)
    return t, (t, ident, blockdiag)


def _unit_triangular_inverse_bwd(res, g):
    t, ident, blockdiag = res
    left = _each(lambda x, y: xdot3(x, y, 0, 0), t, g)
    da = _each(lambda x, y: -xdot3(x, y, 1, 1), left, t)
    return da, jnp.zeros_like(ident), jnp.zeros_like(blockdiag)


_unit_triangular_inverse.defvjp(_unit_triangular_inverse_fwd, _unit_triangular_inverse_bwd)


@jax.custom_vjp
def _known_inverse(a, t):
    return t


def _known_inverse_bwd(t, g):
    left = _each(lambda x, y: xdot3(x, y, 0, 0), t, g)
    return _each(lambda x, y: -xdot3(x, y, 1, 1), left, t), _each(jnp.zeros_like, t)


_known_inverse.defvjp(lambda a, t: (t, t), _known_inverse_bwd)


def _unit_triangular_inverse_impl(a, ident, blockdiag):
    ad = _each(lambda x: x * blockdiag, a)
    e = _each(lambda x, y: x - y, a, ad)
    dinv = _each(lambda x: ident - x, ad)
    p = _each(xdot, ad, ad)
    dinv = _each(lambda x, y: x + xdot(x, y), dinv, p)
    p = _each(xdot, p, p)
    dinv = _each(lambda x, y: x + xdot(x, y), dinv, p)
    g = _each(lambda x, y: -xdot(x, y), dinv, e)
    finv = _each(lambda x: ident + x, g)
    p = _each(xdot, g, g)
    finv = _each(lambda x, y: x + xdot(x, y), finv, p)
    p = _each(xdot, p, p)
    finv = _each(lambda x, y: x + xdot(x, y), finv, p)
    return _each(xdot, finv, dinv)


def _gdn_step(state, q, k, v, bb, gb, rev, t_saved=None):
    causal, strict, eye, blockdiag = _chunk_masks(rev)
    lower = causal.astype(F32)
    ones = jnp.ones((CHUNK, CHUNK), F32)
    gcb = _each(lambda x: xdot(lower, x), gb)
    gcol = _each(lambda x: x[:, :CHUNK], gcb)
    grow = _each(lambda x: xdot(ones, jnp.where(eye, x, 0.0)), gcol)
    decay = _each(lambda x, y: jnp.exp(jnp.where(causal, x - y, NEG_INF)), gcol, grow)
    kb = _each(lambda x, y: x * y, k, bb)
    a = _each(lambda x, y, z: jnp.where(strict, nt(x, y) * z, 0.0), kb, k, decay)
    if t_saved is None:
        t = _unit_triangular_inverse(a, eye.astype(F32), blockdiag.astype(F32))
    else:
        t = _known_inverse(a, t_saved)
    egc = _each(jnp.exp, gcb)
    u = _each(lambda x, y, z: xdot(x, y * z), t, v, bb)
    w = _each(lambda x, y, z: xdot(x, y * z), t, kb, egc)
    qk = _each(lambda x, y, z: nt(x, y) * z, q, k, decay)
    glast = _each(lambda x: jnp.sum(x, axis=0, keepdims=True), gb)
    v_new = _each(lambda x, y, z: x - nn(y, z), u, w, state)
    o = _each(lambda x, y, z, p, r: nn(x * y, z) + nn(p, r), q, egc, state, qk, v_new)
    k_dec = _each(lambda x, y, z: x * jnp.exp(y - z), k, glast, gcb)
    state_new = _each(lambda x, y, z, p: x * jnp.exp(y) + tn(z, p), state, glast, k_dec, v_new)
    return state_new, o, t


def _gla_step(state_t, q, k, v, la, rev):
    causal, _, _, _ = _chunk_masks(rev)
    lower = causal.astype(F32)
    sign = 1 - 2 * rev
    b = _each(lambda x: xdot(lower, x), la)
    q = _each(lambda x: x * (B_DK ** -0.5), q)
    row = lax.broadcasted_iota(jnp.int32, (CHUNK, B_DK), 0)
    sub = row // GLA_SUB
    parts = []
    for blk in range(CHUNK // GLA_SUB):
        rows = slice(blk * GLA_SUB, (blk + 1) * GLA_SUB)
        r_at = jnp.where(rev == 1, GLA_SUB * (blk + 1), GLA_SUB * blk - 1)
        r = _each(lambda x: jnp.sum(jnp.where(row == r_at, x, 0.0), axis=0, keepdims=True), b)
        q_blk = _each(lambda x, y, z: x[rows] * jnp.exp(y[rows] - z), q, b, r)
        k_past = _each(lambda x, y, z: x * jnp.exp(jnp.where((sub - blk) * sign < 0, z - y, NEG_INF)), k, b, r)
        parts.append(_each(lambda x, y: xdot3(x, y, 1, 1), q_blk, k_past))
    scores = _each(lambda *p: jnp.concatenate(p, axis=0), *parts)
    shp = (GLA_SUB, GLA_SUB, B_DK)
    d3 = (lax.broadcasted_iota(jnp.int32, shp, 0) - lax.broadcasted_iota(jnp.int32, shp, 1)) * sign
    place_r = lax.broadcasted_iota(jnp.int32, (GLA_SUB, CHUNK), 0)
    place_c = lax.broadcasted_iota(jnp.int32, (GLA_SUB, CHUNK), 1)
    diag = []
    for blk in range(CHUNK // GLA_SUB):
        rows = slice(blk * GLA_SUB, (blk + 1) * GLA_SUB)
        place = (place_c == place_r + blk * GLA_SUB).astype(F32)

        def pairs(qh, kh, bh):
            qb, kb, bb = qh[rows], kh[rows], bh[rows]
            dec = jnp.exp(jnp.where(d3 >= 0, bb[:, None, :] - bb[None, :, :], NEG_INF))
            return xdot(jnp.sum(qb[:, None, :] * kb[None, :, :] * dec, axis=-1), place)

        diag.append(_each(pairs, q, k, b))
    scores = _each(lambda x, *d: x + jnp.concatenate(d, axis=0), scores, *diag)
    blast = _each(lambda x: jnp.sum(x, axis=0, keepdims=True), la)
    o = _each(lambda x, y, z, s, w: nt(x * jnp.exp(y), z) + nn(s, w), q, b, state_t, scores, v)
    k_dec = _each(lambda x, y, z: x * jnp.exp(y - z), k, blast, b)
    state_new = _each(lambda x, y, z, w: jnp.exp(x) * y + tn(z, w), blast, state_t, v, k_dec)
    return state_new, o


def _chunk_pos(d, m, n):
    return m + d * (n - 1 - 2 * m)


GLA_SUB = 16
GDN_HEADS_PER_STEP = 8
def gdn_rec_fwd(q, k, v, beta_b, g_b):
    s = q.shape[0]
    n = s // CHUNK

    hb = GDN_HEADS_PER_STEP
    wide = hb * LANES

    def body(q_ref, k_ref, v_ref, bb_ref, gb_ref, o_ref, st_ref, t_ref, state):
        d = pl.program_id(0)

        @pl.when(pl.program_id(2) == 0)
        def _():
            state[...] = jnp.zeros_like(state)

        cols = [slice(hh * LANES, (hh + 1) * LANES) for hh in range(hb)]
        st = [state[hh] for hh in range(hb)]
        new, o, t = _gdn_step(st, *([r[:, c] for c in cols] for r in (q_ref, k_ref, v_ref, bb_ref, gb_ref)), d)
        for hh, c in enumerate(cols):
            st_ref[hh] = st[hh]
            t_ref[hh] = t[hh]
            state[hh] = new[hh]
            o_ref[:, c] = o[hh]

    blk = pl.BlockSpec((CHUNK, wide), lambda d, h, m: (_chunk_pos(d, m, n), h))
    gate = pl.BlockSpec((CHUNK, wide), lambda d, h, m: (_chunk_pos(d, m, n), d * (A_HEADS // hb) + h))
    return pl.pallas_call(
        body, name="gdn_rec_fwd", grid=(2, A_HEADS // hb, n),
        in_specs=[blk, blk, blk, gate, gate],
        out_specs=[pl.BlockSpec((None, CHUNK, wide), lambda d, h, m: (d, _chunk_pos(d, m, n), h)),
                   pl.BlockSpec((None, hb, None, A_DK, LANES), lambda d, h, m: (d, h, _chunk_pos(d, m, n), 0, 0)),
                   pl.BlockSpec((None, hb, None, CHUNK, CHUNK), lambda d, h, m: (d, h, _chunk_pos(d, m, n), 0, 0))],
        out_shape=[jax.ShapeDtypeStruct((2, s, A_W), F32), jax.ShapeDtypeStruct((2, A_HEADS, n, A_DK, LANES), F32),
                   jax.ShapeDtypeStruct((2, A_HEADS, n, CHUNK, CHUNK), F32)],
        scratch_shapes=[pltpu.VMEM((hb, A_DK, LANES), F32)],
        compiler_params=_cparams(("arbitrary", "arbitrary", "arbitrary")),
    )(q, k, v, beta_b, g_b)


def gdn_rec_bwd(q, k, v, beta_b, g_b, states, tinv, do):
    s = q.shape[0]
    n = s // CHUNK

    hb = GDN_HEADS_PER_STEP
    wide = hb * LANES

    def body(q_ref, k_ref, v_ref, bb_ref, gb_ref, st_ref, t_ref, do_ref, dq_ref, dk_ref, dv_ref, dbb_ref, dgb_ref, dstate):
        d = pl.program_id(0)

        @pl.when(pl.program_id(2) == 0)
        def _():
            dstate[...] = jnp.zeros_like(dstate)

        def step(*a):
            return _gdn_step(*a, d, t_saved=[t_ref[hh] for hh in range(hb)])[:2]

        cols = [slice(hh * LANES, (hh + 1) * LANES) for hh in range(hb)]
        _, vjp = jax.vjp(step, [st_ref[hh] for hh in range(hb)],
                         *([r[:, c] for c in cols] for r in (q_ref, k_ref, v_ref, bb_ref, gb_ref)))
        grads = vjp(([dstate[hh] for hh in range(hb)], [do_ref[:, c] for c in cols]))
        for hh, c in enumerate(cols):
            dstate[hh], dq_ref[:, c], dk_ref[:, c], dv_ref[:, c], dbb_ref[:, c], dgb_ref[:, c] = (g[hh] for g in grads)

    pos = lambda d, m: _chunk_pos(1 - d, m, n)
    blk = pl.BlockSpec((CHUNK, wide), lambda d, h, m: (pos(d, m), h))
    gate = pl.BlockSpec((CHUNK, wide), lambda d, h, m: (pos(d, m), d * (A_HEADS // hb) + h))
    oblk = pl.BlockSpec((None, CHUNK, wide), lambda d, h, m: (d, pos(d, m), h))
    return pl.pallas_call(
        body, name="gdn_rec_bwd", grid=(2, A_HEADS // hb, n),
        in_specs=[blk, blk, blk, gate, gate,
                  pl.BlockSpec((None, hb, None, A_DK, LANES), lambda d, h, m: (d, h, pos(d, m), 0, 0)),
                  pl.BlockSpec((None, hb, None, CHUNK, CHUNK), lambda d, h, m: (d, h, pos(d, m), 0, 0)), blk],
        out_specs=[oblk, oblk, oblk, gate, gate],
        out_shape=[jax.ShapeDtypeStruct((2, s, A_W), F32)] * 3 + [jax.ShapeDtypeStruct(beta_b.shape, F32)] * 2,
        scratch_shapes=[pltpu.VMEM((hb, A_DK, LANES), F32)],
        compiler_params=_cparams(("arbitrary", "arbitrary", "arbitrary")),
    )(q, k, v, beta_b, g_b, states, tinv, do)


def gla_rec_fwd(proj, log_a):
    s = proj.shape[0]
    n = s // CHUNK

    kcols = [slice(h * B_DK, (h + 1) * B_DK) for h in range(B_HEADS)]
    vcols = [slice(h * B_DV, (h + 1) * B_DV) for h in range(B_HEADS)]

    def body(q_ref, k_ref, v_ref, la_ref, o_ref, st_ref, state):
        d = pl.program_id(0)

        @pl.when(pl.program_id(1) == 0)
        def _():
            state[...] = jnp.zeros_like(state)

        st = [state[h] for h in range(B_HEADS)]
        new, o = _gla_step(st, [q_ref[:, c] for c in kcols], [k_ref[:, c] for c in kcols], [v_ref[:, c] for c in vcols],
                           [la_ref[:, c] for c in kcols], d)
        for h in range(B_HEADS):
            st_ref[h] = st[h]
            state[h] = new[h]
            o_ref[:, vcols[h]] = o[h]

    pos = lambda d, m: _chunk_pos(d, m, n)
    return pl.pallas_call(
        body, name="gla_rec_fwd", grid=(2, n),
        in_specs=[pl.BlockSpec((CHUNK, B_KW), lambda d, m: (pos(d, m), 0)),
                  pl.BlockSpec((CHUNK, B_KW), lambda d, m: (pos(d, m), 1)),
                  pl.BlockSpec((CHUNK, B_VW), lambda d, m: (pos(d, m), 2 * B_KW // B_VW)),
                  pl.BlockSpec((None, CHUNK, B_KW), lambda d, m: (d, pos(d, m), 0))],
        out_specs=[pl.BlockSpec((None, CHUNK, B_VW), lambda d, m: (d, pos(d, m), 0)),
                   pl.BlockSpec((None, B_HEADS, None, B_DV, B_DK), lambda d, m: (d, 0, pos(d, m), 0, 0))],
        out_shape=[jax.ShapeDtypeStruct((2, s, B_VW), F32), jax.ShapeDtypeStruct((2, B_HEADS, n, B_DV, B_DK), F32)],
        scratch_shapes=[pltpu.VMEM((B_HEADS, B_DV, B_DK), F32)],
        compiler_params=_cparams(("arbitrary", "arbitrary")),
    )(proj, proj, proj, log_a)


def gla_rec_bwd(proj, log_a, states, do):
    s = proj.shape[0]
    n = s // CHUNK

    kcols = [slice(h * B_DK, (h + 1) * B_DK) for h in range(B_HEADS)]
    vcols = [slice(h * B_DV, (h + 1) * B_DV) for h in range(B_HEADS)]

    def body(q_ref, k_ref, v_ref, la_ref, st_ref, do_ref, dq_ref, dk_ref, dv_ref, dla_ref, dstate):
        d = pl.program_id(0)

        @pl.when(pl.program_id(1) == 0)
        def _():
            dstate[...] = jnp.zeros_like(dstate)

        step = functools.partial(_gla_step, rev=d)
        _, vjp = jax.vjp(step, [st_ref[h] for h in range(B_HEADS)], [q_ref[:, c] for c in kcols],
                         [k_ref[:, c] for c in kcols], [v_ref[:, c] for c in vcols], [la_ref[:, c] for c in kcols])
        dst, dq, dk, dv, dla = vjp(([dstate[h] for h in range(B_HEADS)], [do_ref[:, c] for c in vcols]))
        for h in range(B_HEADS):
            dstate[h] = dst[h]
            dq_ref[:, kcols[h]] = dq[h]
            dk_ref[:, kcols[h]] = dk[h]
            dv_ref[:, vcols[h]] = dv[h]
            dla_ref[:, kcols[h]] = dla[h]

    pos = lambda d, m: _chunk_pos(1 - d, m, n)
    kblk = pl.BlockSpec((None, CHUNK, B_KW), lambda d, m: (d, pos(d, m), 0))
    return pl.pallas_call(
        body, name="gla_rec_bwd", grid=(2, n),
        in_specs=[pl.BlockSpec((CHUNK, B_KW), lambda d, m: (pos(d, m), 0)),
                  pl.BlockSpec((CHUNK, B_KW), lambda d, m: (pos(d, m), 1)),
                  pl.BlockSpec((CHUNK, B_VW), lambda d, m: (pos(d, m), 2 * B_KW // B_VW)),
                  kblk,
                  pl.BlockSpec((None, B_HEADS, None, B_DV, B_DK), lambda d, m: (d, 0, pos(d, m), 0, 0)),
                  pl.BlockSpec((CHUNK, B_VW), lambda d, m: (pos(d, m), 0))],
        out_specs=[kblk, kblk, pl.BlockSpec((None, CHUNK, B_VW), lambda d, m: (d, pos(d, m), 0)), kblk],
        out_shape=[jax.ShapeDtypeStruct((2, s, B_KW), F32), jax.ShapeDtypeStruct((2, s, B_KW), F32),
                   jax.ShapeDtypeStruct((2, s, B_VW), F32), jax.ShapeDtypeStruct((2, s, B_KW), F32)],
        scratch_shapes=[pltpu.VMEM((B_HEADS, B_DV, B_DK), F32)],
        compiler_params=_cparams(("arbitrary", "arbitrary")),
    )(proj, proj, proj, log_a, states, do)


MM_TILE_OUT = 1024
MM_TILE_K = 1024


def _tile(n, pref):
    return pref if n % pref == 0 else n


class Gathered:
    def __init__(self, g, off, kind):
        assert off % D_MODEL == 0 and MM_TILE_OUT == D_MODEL and MM_TILE_K == D_MODEL
        self.g, self.blk, self.kind = g, off // D_MODEL, kind
        self.shape = (D_MODEL, 4 * D_MODEL) if kind == "cols" else (4 * D_MODEL, D_MODEL)

    def spec(self, mode):
        blk = self.blk
        chip_is_k = (self.kind == "rows") == (mode == "nn")
        if chip_is_k:
            return pl.BlockSpec((None, D_MODEL, D_MODEL), lambda i, j, k: (k, blk, 0))
        return pl.BlockSpec((None, D_MODEL, D_MODEL), lambda i, j, k: (j, blk, 0))


def mm(a, b, mode="nn", act=None, epi=None, extra=None, alpha=1.0, chip_major=False, pack=None, name="mm"):
    if mode == "tn":
        kk, m = a.shape
    else:
        m, kk = a.shape
    nn_ = b.shape[0] if mode == "nt" else b.shape[1]
    tm, tn_, tk = _tile(m, MM_TILE_OUT), _tile(nn_, MM_TILE_OUT), _tile(kk, MM_TILE_K)
    nk = kk // tk
    ca, cb = {"nn": (1, 0), "nt": (1, 1), "tn": (0, 0)}[mode]

    def body(*refs):
        o_ref = refs[-1]
        a_ref, b_ref = refs[:2]
        if epi is not None:
            e_ref = refs[2]
        kstep = pl.program_id(2)
        at = a_ref[...]
        if act == "sqrelu":
            at = jnp.square(jnp.maximum(at, 0.0))
        part = _dg(at, b_ref[...], ca, cb)

        @pl.when(kstep == 0)
        def _():
            o_ref[...] = part

        @pl.when(kstep > 0)
        def _():
            o_ref[...] += part

        if epi is not None:
            @pl.when(kstep == nk - 1)
            def _():
                if epi == "dsqrelu":
                    o_ref[...] = o_ref[...] * (2.0 * jnp.maximum(e_ref[...], 0.0))
                else:
                    o_ref[...] = o_ref[...] + alpha * e_ref[...]

    a_spec = pl.BlockSpec((tk, tm), lambda i, j, k: (k, i)) if mode == "tn" else pl.BlockSpec((tm, tk), lambda i, j, k: (i, k))
    if isinstance(b, Gathered):
        assert mode in ("nn", "nt") and tn_ == D_MODEL and tk == D_MODEL
        b_spec, b = b.spec(mode), b.g
    elif mode == "nt":
        b_spec = pl.BlockSpec((tn_, tk), lambda i, j, k: (j, k))
    else:
        b_spec = pl.BlockSpec((tk, tn_), lambda i, j, k: (k, j))
    o_spec = pl.BlockSpec((tm, tn_), lambda i, j, k: (i, j))
    ins, specs = [a, b], [a_spec, b_spec]
    if epi is not None:
        ins.append(extra)
        specs.append(o_spec)
    out_shape = jax.ShapeDtypeStruct((m, nn_), F32)
    if chip_major:
        assert nn_ == 4 * D_MODEL and tn_ == D_MODEL
        o_spec = pl.BlockSpec((None, tm, D_MODEL), lambda i, j, k: (j, i, 0))
        out_shape = jax.ShapeDtypeStruct((4, m, D_MODEL), F32)
    aliases = {}
    if pack is not None:
        buf, rows, blk, kind = pack
        logical = (D_MODEL, 4 * D_MODEL) if kind == "cols" else (4 * D_MODEL, D_MODEL)
        assert epi is None and tm == D_MODEL and tn_ == D_MODEL and (m, nn_) == logical
        if kind == "cols":
            o_spec = pl.BlockSpec((None, D_MODEL, D_MODEL), lambda i, j, k: (j, blk, 0))
        else:
            o_spec = pl.BlockSpec((None, D_MODEL, D_MODEL), lambda i, j, k: (i, blk, 0))
        out_shape = jax.ShapeDtypeStruct((4, rows, D_MODEL), F32)
        if buf is not None:
            aliases = {len(ins): 0}
            ins.append(buf)
            specs.append(pl.BlockSpec(memory_space=pl.ANY))
    return pl.pallas_call(
        body, name=name, grid=(m // tm, nn_ // tn_, nk), in_specs=specs, out_specs=o_spec, out_shape=out_shape,
        input_output_aliases=aliases, compiler_params=_cparams(("parallel", "parallel", "arbitrary")),
    )(*ins)


ROWS = 256
POST_ROWS = 1024
LN_ROWS = 512


def _ln_core(x, m, g, b):
    r = ALPHA * x + m
    mu = jnp.mean(r, axis=-1, keepdims=True)
    xc = r - mu
    var = jnp.mean(xc * xc, axis=-1, keepdims=True)
    rstd = lax.rsqrt(var + LN_EPS)
    xhat = xc * rstd
    return xhat * g + b, xhat, rstd


def ln_fwd(x, m, g, b):
    s, dm = x.shape
    rows = _tile(s, LN_ROWS)

    def body(x_ref, m_ref, g_ref, b_ref, o_ref):
        o_ref[...] = _ln_core(x_ref[...], m_ref[...], g_ref[...], b_ref[...])[0]

    row = pl.BlockSpec((rows, dm), lambda i: (i, 0))
    vec = pl.BlockSpec((1, dm), lambda i: (0, 0))
    return pl.pallas_call(body, name="ln_fwd", grid=(s // rows,), in_specs=[row, row, vec, vec], out_specs=row,
                          out_shape=jax.ShapeDtypeStruct((s, dm), F32), compiler_params=_cparams(("parallel",)))(x, m, g, b)


def ln_bwd(x, m, g, dy):
    s, dm = x.shape
    rows = _tile(s, LN_ROWS)

    def body(x_ref, m_ref, g_ref, dy_ref, dr_ref, dg_ref, db_ref):
        gv = g_ref[...]
        _, xhat, rstd = _ln_core(x_ref[...], m_ref[...], gv, jnp.zeros_like(gv))
        dy = dy_ref[...]
        dxh = dy * gv
        dr_ref[...] = rstd * (dxh - jnp.mean(dxh, axis=-1, keepdims=True)
                              - xhat * jnp.mean(dxh * xhat, axis=-1, keepdims=True))

        @pl.when(pl.program_id(0) == 0)
        def _():
            dg_ref[...] = jnp.zeros_like(dg_ref)
            db_ref[...] = jnp.zeros_like(db_ref)

        dg_ref[...] += jnp.sum(dy * xhat, axis=0, keepdims=True)
        db_ref[...] += jnp.sum(dy, axis=0, keepdims=True)

    row = pl.BlockSpec((rows, dm), lambda i: (i, 0))
    vec = pl.BlockSpec((1, dm), lambda i: (0, 0))
    return pl.pallas_call(body, name="ln_bwd", grid=(s // rows,), in_specs=[row, row, vec, row], out_specs=[row, vec, vec],
                          out_shape=[jax.ShapeDtypeStruct((s, dm), F32), jax.ShapeDtypeStruct((1, dm), F32),
                                     jax.ShapeDtypeStruct((1, dm), F32)],
                          compiler_params=_cparams(("arbitrary",)))(x, m, g, dy)


def loss_head(y, target):
    s, dm = y.shape
    rows = _tile(s, LN_ROWS)

    def body(y_ref, t_ref, dy_ref, l_ref):
        e = y_ref[...] - t_ref[...]
        dy_ref[...] = e * (1.0 / dm)

        @pl.when(pl.program_id(0) == 0)
        def _():
            l_ref[...] = jnp.zeros_like(l_ref)

        col = jnp.sum(e * e, axis=0, keepdims=True) * (0.5 / dm)
        acc = col[:, :LANES]
        for c in range(1, dm // LANES):
            acc = acc + col[:, c * LANES:(c + 1) * LANES]
        l_ref[...] += acc

    row = pl.BlockSpec((rows, dm), lambda i: (i, 0))
    return pl.pallas_call(body, name="loss_head", grid=(s // rows,), in_specs=[row, row],
                          out_specs=[row, pl.BlockSpec((1, LANES), lambda i: (0, 0))],
                          out_shape=[jax.ShapeDtypeStruct((s, dm), F32), jax.ShapeDtypeStruct((1, LANES), F32)],
                          compiler_params=_cparams(("arbitrary",)))(y, target)


def _shift_rows_impl(x, d):
    n = x.shape[0]
    if d == 0:
        return x
    t = lax.broadcasted_iota(jnp.int32, x.shape, 0)
    return jnp.where((t + d >= 0) & (t + d < n), pltpu.roll(x, (-d) % n, 0), 0.0)


@functools.partial(jax.custom_vjp, nondiff_argnums=(1,))
def _shift_rows(x, d):
    return _shift_rows_impl(x, d)


_shift_rows.defvjp(lambda x, d: (_shift_rows_impl(x, d), None), lambda d, _, g: (_shift_rows_impl(g, -d),))


def _gdn_pre_fn(u, w, kind):
    rows = lax.broadcasted_iota(jnp.int32, w.shape, 0)
    c = None
    for tap in range(A_CONV):
        w_tap = jnp.sum(jnp.where(rows == tap, w, 0.0), axis=0, keepdims=True)
        term = _shift_rows(u, tap - A_CONV // 2) * w_tap
        c = term if c is None else c + term
    y = c * _sigmoid(c)
    if kind == "v":
        return y
    y = y * lax.rsqrt(jnp.sum(y * y, axis=-1, keepdims=True) + L2_EPS)
    return y * (A_DK ** -0.5) if kind == "q" else y


_KIND_OFF = {"q": 0, "k": A_HEADS, "v": 2 * A_HEADS}


def gdn_pre(proj, conv_w, kind):
    s = proj.shape[0]
    off = _KIND_OFF[kind]

    def body(u_ref, w_ref, o_ref):
        o_ref[...] = _gdn_pre_fn(u_ref[...], w_ref[...], kind)

    return pl.pallas_call(
        body, name="gdn_pre_" + kind, grid=(A_HEADS,),
        in_specs=[pl.BlockSpec((s, LANES), lambda h: (0, off + h)), pl.BlockSpec((A_CONV, LANES), lambda h: (0, off + h))],
        out_specs=pl.BlockSpec((s, LANES), lambda h: (0, h)),
        out_shape=jax.ShapeDtypeStruct((s, A_W), F32), compiler_params=_cparams(("parallel",)))(proj, conv_w)


def gdn_pre_bwd(proj, conv_w, dt2, kind):
    s = proj.shape[0]
    off = _KIND_OFF[kind]

    def body(u_ref, w_ref, d0_ref, d1_ref, du_ref, dw_ref):
        _, vjp = jax.vjp(functools.partial(_gdn_pre_fn, kind=kind), u_ref[...], w_ref[...])
        du, dw = vjp(d0_ref[...] + d1_ref[...])
        du_ref[...] = du
        dw_ref[...] = dw

    return pl.pallas_call(
        body, name="gdn_pre_bwd_" + kind, grid=(A_HEADS,),
        in_specs=[pl.BlockSpec((s, LANES), lambda h: (0, off + h)), pl.BlockSpec((A_CONV, LANES), lambda h: (0, off + h)),
                  pl.BlockSpec((None, s, LANES), lambda h: (0, 0, h)), pl.BlockSpec((None, s, LANES), lambda h: (1, 0, h))],
        out_specs=[pl.BlockSpec((s, LANES), lambda h: (0, h)), pl.BlockSpec((A_CONV, LANES), lambda h: (0, h))],
        out_shape=[jax.ShapeDtypeStruct((s, A_W), F32), jax.ShapeDtypeStruct((A_CONV, A_W), F32)],
        compiler_params=_cparams(("parallel",)))(proj, conv_w, dt2, dt2)


N_GATE = 2 * A_HEADS


def _gdn_gates_fn(ba, alog_row, dt_row):
    r = lax.broadcasted_iota(jnp.int32, (LANES, N_GATE * LANES), 0)
    c = lax.broadcasted_iota(jnp.int32, (LANES, N_GATE * LANES), 1) >> 7
    beta_b = xdot(_sigmoid(ba), (r == c).astype(F32))
    g = -(jnp.exp(alog_row) * _softplus(ba + dt_row))
    g_b = xdot(g, (r == c + N_GATE).astype(F32))
    return beta_b, g_b


def gdn_gates(ba, alog_row, dt_row):
    s = ba.shape[0]

    def body(ba_ref, al_ref, dt_ref, bb_ref, gb_ref):
        bb_ref[...], gb_ref[...] = _gdn_gates_fn(ba_ref[...], al_ref[...], dt_ref[...])

    row = pl.BlockSpec((ROWS, LANES), lambda i: (i, 0))
    vec = pl.BlockSpec((1, LANES), lambda i: (0, 0))
    wide = pl.BlockSpec((ROWS, N_GATE * LANES), lambda i: (i, 0))
    return pl.pallas_call(body, name="gdn_gates", grid=(s // ROWS,), in_specs=[row, vec, vec], out_specs=[wide, wide],
                          out_shape=[jax.ShapeDtypeStruct((s, N_GATE * LANES), F32)] * 2,
                          compiler_params=_cparams(("parallel",)))(ba, alog_row, dt_row)


def gdn_gates_bwd(ba, alog_row, dt_row, dbeta_b, dg_b):
    s = ba.shape[0]

    def body(ba_ref, al_ref, dt_ref, dbb_ref, dgb_ref, dba_ref, dal_ref, ddt_ref):
        _, vjp = jax.vjp(_gdn_gates_fn, ba_ref[...], al_ref[...], dt_ref[...])
        dba, dal, ddt = vjp((dbb_ref[...], dgb_ref[...]))
        dba_ref[...] = dba

        @pl.when(pl.program_id(0) == 0)
        def _():
            dal_ref[...] = jnp.zeros_like(dal_ref)
            ddt_ref[...] = jnp.zeros_like(ddt_ref)

        dal_ref[...] += dal
        ddt_ref[...] += ddt

    row = pl.BlockSpec((ROWS, LANES), lambda i: (i, 0))
    vec = pl.BlockSpec((1, LANES), lambda i: (0, 0))
    wide = pl.BlockSpec((ROWS, N_GATE * LANES), lambda i: (i, 0))
    return pl.pallas_call(body, name="gdn_gates_bwd", grid=(s // ROWS,), in_specs=[row, vec, vec, wide, wide],
                          out_specs=[row, vec, vec],
                          out_shape=[jax.ShapeDtypeStruct((s, LANES), F32), jax.ShapeDtypeStruct((1, LANES), F32),
                                     jax.ShapeDtypeStruct((1, LANES), F32)],
                          compiler_params=_cparams(("arbitrary",)))(ba, alog_row, dt_row, dbeta_b, dg_b)


def _post_fn(o, z, g):
    y = o * lax.rsqrt(jnp.mean(o * o, axis=-1, keepdims=True) + RMS_EPS) * g
    return y * (z * _sigmoid(z))


def mixer_post(o2, proj, norm_g, width, gate_off, name):
    s = o2.shape[1]
    nh = o2.shape[2] // width

    rows = _tile(s, POST_ROWS)

    def body(o0_ref, o1_ref, z_ref, g_ref, y_ref):
        y_ref[...] = _post_fn(o0_ref[...] + o1_ref[...], z_ref[...], g_ref[...])

    ospec = lambda d: pl.BlockSpec((None, rows, width), lambda i, h: (d, i, h))
    return pl.pallas_call(
        body, name=name, grid=(s // rows, nh),
        in_specs=[ospec(0), ospec(1), pl.BlockSpec((rows, width), lambda i, h: (i, gate_off + h)),
                  pl.BlockSpec((1, width), lambda i, h: (0, 0))],
        out_specs=pl.BlockSpec((rows, width), lambda i, h: (i, h)),
        out_shape=jax.ShapeDtypeStruct((s, o2.shape[2]), F32),
        compiler_params=_cparams(("parallel", "parallel")))(o2, o2, proj, norm_g)


def mixer_post_bwd(o2, proj, norm_g, dy, width, gate_off, name):
    s = o2.shape[1]
    nh = o2.shape[2] // width

    def body(o0_ref, o1_ref, z_ref, g_ref, dy_ref, do_ref, dz_ref, dg_ref):
        _, vjp = jax.vjp(_post_fn, o0_ref[...] + o1_ref[...], z_ref[...], g_ref[...])
        do, dz, dg = vjp(dy_ref[...])
        do_ref[...] = do
        dz_ref[...] = dz

        @pl.when((pl.program_id(0) == 0) & (pl.program_id(1) == 0))
        def _():
            dg_ref[...] = jnp.zeros_like(dg_ref)

        dg_ref[...] += dg

    rows = _tile(s, POST_ROWS)
    ospec = lambda d: pl.BlockSpec((None, rows, width), lambda i, h: (d, i, h))
    blk = pl.BlockSpec((rows, width), lambda i, h: (i, h))
    vec = pl.BlockSpec((1, width), lambda i, h: (0, 0))
    return pl.pallas_call(
        body, name=name, grid=(s // rows, nh),
        in_specs=[ospec(0), ospec(1), pl.BlockSpec((rows, width), lambda i, h: (i, gate_off + h)), vec, blk],
        out_specs=[blk, blk, vec],
        out_shape=[jax.ShapeDtypeStruct((s, o2.shape[2]), F32)] * 2 + [jax.ShapeDtypeStruct((1, width), F32)],
        compiler_params=_cparams(("arbitrary", "arbitrary")))(o2, o2, proj, norm_g, dy)


def _log_gate(z):
    return (jnp.minimum(z, 0.0) - jnp.log(1.0 + jnp.exp(-jnp.abs(z)))) * (1.0 / B_TAU)


def gla_gate(gl, w2, gb):
    s = gl.shape[0]

    def body(gl_ref, w_ref, b_ref, o_ref):
        for n in range(2):
            o_ref[n] = _log_gate(nn(gl_ref[...], w_ref[n]) + b_ref[n])

    full = lambda shp: pl.BlockSpec(shp, lambda i: (0,) * len(shp))
    return pl.pallas_call(
        body, name="gla_gate", grid=(s // ROWS,),
        in_specs=[pl.BlockSpec((ROWS, LANES), lambda i: (i, 0)), full(w2.shape), full(gb.shape)],
        out_specs=pl.BlockSpec((2, ROWS, B_KW), lambda i: (0, i, 0)),
        out_shape=jax.ShapeDtypeStruct((2, s, B_KW), F32), compiler_params=_cparams(("parallel",)))(gl, w2, gb)


def gla_gate_bwd(gl, w2, gb, dla):
    s = gl.shape[0]

    def body(gl_ref, w_ref, b_ref, dla_ref, dgl_ref, dz_ref, db0_ref, db1_ref):
        @pl.when(pl.program_id(0) == 0)
        def _():
            db0_ref[...] = jnp.zeros_like(db0_ref)
            db1_ref[...] = jnp.zeros_like(db1_ref)

        dgl = None
        for n, db_ref in enumerate((db0_ref, db1_ref)):
            _, vjp = jax.vjp(_log_gate, nn(gl_ref[...], w_ref[n]) + b_ref[n])
            dz, = vjp(dla_ref[n])
            dz_ref[n] = dz
            db_ref[...] += jnp.sum(dz, axis=0, keepdims=True)
            part = nt(dz, w_ref[n])
            dgl = part if dgl is None else dgl + part
        dgl_ref[...] = dgl

    full = lambda shp: pl.BlockSpec(shp, lambda i: (0,) * len(shp))
    row = pl.BlockSpec((ROWS, LANES), lambda i: (i, 0))
    wide = pl.BlockSpec((2, ROWS, B_KW), lambda i: (0, i, 0))
    vec = pl.BlockSpec((1, B_KW), lambda i: (0, 0))
    return pl.pallas_call(
        body, name="gla_gate_bwd", grid=(s // ROWS,),
        in_specs=[row, full(w2.shape), full(gb.shape), wide],
        out_specs=[row, wide, vec, vec],
        out_shape=[jax.ShapeDtypeStruct((s, LANES), F32), jax.ShapeDtypeStruct((2, s, B_KW), F32),
                   jax.ShapeDtypeStruct((1, B_KW), F32), jax.ShapeDtypeStruct((1, B_KW), F32)],
        compiler_params=_cparams(("arbitrary",)))(gl, w2, gb, dla)


PACK_TILE = 512


def cast_into_slot(x, chip):
    r, c = x.shape

    def body(chip_ref, x_ref, o_ref):
        o_ref[...] = x_ref[...].astype(BF16)

    return pl.pallas_call(
        body, name="cast_into_slot",
        grid_spec=pltpu.PrefetchScalarGridSpec(
            num_scalar_prefetch=1, grid=(r // PACK_TILE,),
            in_specs=[pl.BlockSpec((PACK_TILE, c), lambda i, chip_ref: (i, 0))],
            out_specs=pl.BlockSpec((None, PACK_TILE, c), lambda i, chip_ref: (chip_ref[0], i, 0))),
        out_shape=jax.ShapeDtypeStruct((4, r, c), BF16), compiler_params=_cparams(("parallel",)))(chip, x)


def sum_received(chip_sum, recv, chip, core):
    _, h, c = chip_sum.shape
    n = recv.shape[0]
    tr = _tile(h, PACK_TILE)
    nblk = h // tr

    def body(chip_ref, core_ref, own_ref, r_ref, o_ref):
        acc = r_ref[0].astype(F32)
        for k in range(1, n):
            acc = acc + r_ref[k].astype(F32)
        o_ref[...] = acc + own_ref[...].astype(F32)

    return pl.pallas_call(
        body, name="sum_received",
        grid_spec=pltpu.PrefetchScalarGridSpec(
            num_scalar_prefetch=2, grid=(nblk,),
            in_specs=[pl.BlockSpec((None, tr, c), lambda i, chip_ref, core_ref: (chip_ref[0], i, 0)),
                      pl.BlockSpec((n, tr, c), lambda i, chip_ref, core_ref: (0, i, 0))],
            out_specs=pl.BlockSpec((tr, c), lambda i, chip_ref, core_ref: (core_ref[0] * nblk + i, 0))),
        out_shape=jax.ShapeDtypeStruct((2 * h, c), F32), compiler_params=_cparams(("parallel",)))(chip, core, chip_sum, recv)


def merge_first_hop(chip_sum, passed, slot_x, slot_y, core):
    _, h, c = chip_sum.shape
    tr = _tile(h, PACK_TILE)

    def body(sx_ref, sy_ref, core_ref, to_x_ref, to_y_ref, p_ref, o_ref):
        p = p_ref[...].astype(F32)
        is_y = core_ref[0].astype(F32)
        o_ref[0] = (to_x_ref[...].astype(F32) + p * (1.0 - is_y)).astype(BF16)
        o_ref[1] = (to_y_ref[...].astype(F32) + p * is_y).astype(BF16)

    return pl.pallas_call(
        body, name="merge_first_hop",
        grid_spec=pltpu.PrefetchScalarGridSpec(
            num_scalar_prefetch=3, grid=(h // tr,),
            in_specs=[pl.BlockSpec((None, tr, c), lambda i, sx_ref, sy_ref, core_ref: (sx_ref[0], i, 0)),
                      pl.BlockSpec((None, tr, c), lambda i, sx_ref, sy_ref, core_ref: (sy_ref[0], i, 0)),
                      pl.BlockSpec((tr, c), lambda i, sx_ref, sy_ref, core_ref: (i, 0))],
            out_specs=pl.BlockSpec((2, tr, c), lambda i, sx_ref, sy_ref, core_ref: (0, i, 0))),
        out_shape=jax.ShapeDtypeStruct((2, h, c), BF16),
        compiler_params=_cparams(("parallel",)))(slot_x, slot_y, core, chip_sum, chip_sum, passed)


def sum_slots(x, name):
    n, r, c = x.shape
    tr = _tile(r, PACK_TILE)

    def body(x_ref, o_ref):
        acc = x_ref[0].astype(F32)
        for k in range(1, n):
            acc = acc + x_ref[k].astype(F32)
        o_ref[...] = acc

    return pl.pallas_call(body, name=name, grid=(r // tr,), in_specs=[pl.BlockSpec((n, tr, c), lambda i: (0, i, 0))],
                          out_specs=pl.BlockSpec((tr, c), lambda i: (i, 0)),
                          out_shape=jax.ShapeDtypeStruct((r, c), F32), compiler_params=_cparams(("parallel",)))(x)


def half_to_bf16(gpack, which, theirs=None):
    n, r, c = gpack.shape
    half_rows = r // 2
    tr = _tile(half_rows, PACK_TILE)
    nblk = half_rows // tr

    def body(which_ref, g_ref, *rest):
        o_ref = rest[-1]
        acc = g_ref[...]
        if theirs is not None:
            acc = acc + rest[0][...].astype(F32)
        o_ref[...] = acc.astype(BF16)

    blk = pl.BlockSpec((None, tr, c), lambda s, i, which_ref: (s, i, 0))
    ins = [gpack] if theirs is None else [gpack, theirs]
    return pl.pallas_call(
        body, name="half_to_bf16" if theirs is None else "add_sibling_half",
        grid_spec=pltpu.PrefetchScalarGridSpec(
            num_scalar_prefetch=1, grid=(n, nblk),
            in_specs=[pl.BlockSpec((None, tr, c), lambda s, i, which_ref: (s, which_ref[0] * nblk + i, 0))]
            + [blk] * (len(ins) - 1),
            out_specs=blk),
        out_shape=jax.ShapeDtypeStruct((n, half_rows, c), BF16),
        compiler_params=_cparams(("parallel", "parallel")))(which, *ins)


def adamw(w, m, v, grads, g_row_off, name):
    r, c = w.shape
    tr = next(t for t in (PACK_TILE, r) if r % t == 0 and g_row_off % t == 0)
    ob = g_row_off // tr
    ng = len(grads)

    def body(*refs):
        w_ref, m_ref, v_ref = refs[:3]
        g_refs = refs[3:3 + ng]
        g_ref, d_ref, nm_ref, nv_ref = refs[3 + ng:]
        g = g_refs[0][...]
        for gr in g_refs[1:]:
            g = g + gr[...]
        m_new = ADAM_B1 * m_ref[...] + (1.0 - ADAM_B1) * g
        v_new = ADAM_B2 * v_ref[...] + (1.0 - ADAM_B2) * jnp.square(g)
        m_hat = m_new / (1.0 - ADAM_B1 ** ADAM_STEP)
        v_hat = v_new / (1.0 - ADAM_B2 ** ADAM_STEP)
        g_ref[...] = g
        d_ref[...] = -ADAM_LR * (m_hat / (jnp.sqrt(v_hat) + ADAM_EPS) + ADAM_WD * w_ref[...])
        nm_ref[...] = m_new
        nv_ref[...] = v_new

    blk = pl.BlockSpec((tr, c), lambda i: (i, 0))
    gblk = pl.BlockSpec((tr, c), lambda i: (i + ob, 0))
    return pl.pallas_call(body, name=name, grid=(r // tr,), in_specs=[blk, blk, blk] + [gblk] * ng, out_specs=[blk] * 4,
                          out_shape=[jax.ShapeDtypeStruct((r, c), F32)] * 4,
                          compiler_params=_cparams(("parallel",)))(w, m, v, *grads)


MESH = pl.DeviceIdType.MESH
HBM = pl.BlockSpec(memory_space=pl.ANY)


def _place():
    return lax.axis_index("x"), lax.axis_index("y"), lax.axis_index("c")


def allgather_chips(buf):
    _, r, c = buf.shape
    half_rows = r // 2

    def body(_, out_ref, send_sems, recv_sems):
        x, y, cc = _place()
        half = pl.ds(cc * half_rows, half_rows)
        other = pl.ds((1 - cc) * half_rows, half_rows)

        def copy(k, rows, to):
            return pltpu.make_async_remote_copy(src_ref=rows, dst_ref=rows, send_sem=send_sems.at[k],
                                                recv_sem=recv_sems.at[k], device_id=to, device_id_type=MESH)

        nbr_x, nbr_y, diag = (1 - x, y), (x, 1 - y), (1 - x, 1 - y)
        slot = lambda chip: 2 * chip[0] + chip[1]
        sibling = (x, y, 1 - cc)
        first = [copy(0, out_ref.at[slot((x, y)), half], (*nbr_x, cc)), copy(1, out_ref.at[slot((x, y)), half], (*nbr_y, cc))]
        for cp in first:
            cp.start()
        passed = []
        for k, chip in enumerate((nbr_x, nbr_y)):
            landed = out_ref.at[slot(chip), half]
            copy(k, landed, (*chip, cc)).wait_recv()
            passed.append(copy(3 + k, landed, sibling))
            passed[-1].start()
        via = (1 - x + cc * (2 * x - 1), y + cc * (1 - 2 * y))
        to = (x + cc * (1 - 2 * x), 1 - y + cc * (2 * y - 1))
        hop = copy(2, out_ref.at[slot(via), half], (*to, cc))
        hop.start()
        landed = out_ref.at[slot(diag), half]
        copy(2, landed, (*to, cc)).wait_recv()
        passed.append(copy(5, landed, sibling))
        passed[-1].start()
        for k, chip in enumerate((nbr_x, nbr_y, diag)):
            copy(3 + k, out_ref.at[slot(chip), other], sibling).wait_recv()
        for cp in first + [hop] + passed:
            cp.wait_send()

    return pl.pallas_call(
        body, name="allgather_chips", in_specs=[HBM], out_specs=HBM, input_output_aliases={0: 0},
        out_shape=jax.ShapeDtypeStruct(buf.shape, buf.dtype),
        scratch_shapes=[pltpu.SemaphoreType.DMA((6,)), pltpu.SemaphoreType.DMA((6,))],
    )(buf)


def scatter_first_hop(gpack):
    _, r, c = gpack.shape

    def body(src_ref, out_ref, send_sem, recv_sem):
        x, y, cc = _place()
        to = (x + cc * (1 - 2 * x), 1 - y + cc * (2 * y - 1), cc)
        cp = pltpu.make_async_remote_copy(src_ref=src_ref.at[2 * (1 - x) + (1 - y)], dst_ref=out_ref, send_sem=send_sem,
                                          recv_sem=recv_sem, device_id=to, device_id_type=MESH)
        cp.start()
        cp.wait()

    return pl.pallas_call(
        body, name="scatter_first_hop", in_specs=[HBM], out_specs=HBM,
        out_shape=jax.ShapeDtypeStruct((r, c), gpack.dtype),
        scratch_shapes=[pltpu.SemaphoreType.DMA, pltpu.SemaphoreType.DMA],
    )(gpack)


def scatter_second_hop(to_nbrs):
    def body(src_ref, out_ref, send_sems, recv_sems):
        x, y, cc = _place()
        sends = [pltpu.make_async_remote_copy(src_ref=src_ref.at[k], dst_ref=out_ref.at[k], send_sem=send_sems.at[k],
                                              recv_sem=recv_sems.at[k], device_id=to, device_id_type=MESH)
                 for k, to in enumerate(((1 - x, y, cc), (x, 1 - y, cc)))]
        for cp in sends:
            cp.start()
        for cp in sends:
            cp.wait_recv()
        for cp in sends:
            cp.wait_send()

    return pl.pallas_call(
        body, name="scatter_second_hop", in_specs=[HBM], out_specs=HBM,
        out_shape=jax.ShapeDtypeStruct(to_nbrs.shape, to_nbrs.dtype),
        scratch_shapes=[pltpu.SemaphoreType.DMA((2,)), pltpu.SemaphoreType.DMA((2,))],
    )(to_nbrs)


def swap_sibling(mine):
    def body(src_ref, out_ref, send_sem, recv_sem):
        x, y, cc = _place()
        cp = pltpu.make_async_remote_copy(src_ref=src_ref, dst_ref=out_ref, send_sem=send_sem, recv_sem=recv_sem,
                                          device_id=(x, y, 1 - cc), device_id_type=MESH)
        cp.start()
        cp.wait()

    return pl.pallas_call(
        body, name="swap_sibling", in_specs=[HBM], out_specs=HBM,
        out_shape=jax.ShapeDtypeStruct(mine.shape, mine.dtype),
        scratch_shapes=[pltpu.SemaphoreType.DMA, pltpu.SemaphoreType.DMA],
    )(mine)


def join_halves(buf):
    r, c = buf.shape
    half_rows = r // 2

    def body(_, out_ref, send_sem, recv_sem):
        x, y, cc = _place()
        half = out_ref.at[pl.ds(cc * half_rows, half_rows)]
        other = out_ref.at[pl.ds((1 - cc) * half_rows, half_rows)]
        send = pltpu.make_async_remote_copy(src_ref=half, dst_ref=half, send_sem=send_sem, recv_sem=recv_sem,
                                            device_id=(x, y, 1 - cc), device_id_type=MESH)
        send.start()
        pltpu.make_async_remote_copy(src_ref=other, dst_ref=other, send_sem=send_sem, recv_sem=recv_sem,
                                     device_id=(x, y, 1 - cc), device_id_type=MESH).wait_recv()
        send.wait_send()

    return pl.pallas_call(
        body, name="join_halves", in_specs=[HBM], out_specs=HBM, input_output_aliases={0: 0},
        out_shape=jax.ShapeDtypeStruct(buf.shape, buf.dtype),
        scratch_shapes=[pltpu.SemaphoreType.DMA, pltpu.SemaphoreType.DMA],
    )(buf)


def exchange_all(v, name):
    r, c = v.shape

    def body(v_ref, out_ref, send_sems, recv_sems):
        x, y, cc = _place()
        out_ref[4 * x + 2 * y + cc] = v_ref[...]
        sends, recvs = [], []
        for k in range(1, 8):
            px = 1 - x if k & 4 else x
            py = 1 - y if k & 2 else y
            pc = 1 - cc if k & 1 else cc
            sends.append(pltpu.make_async_remote_copy(
                src_ref=v_ref, dst_ref=out_ref.at[4 * x + 2 * y + cc], send_sem=send_sems.at[k - 1],
                recv_sem=recv_sems.at[k - 1], device_id=(px, py, pc), device_id_type=MESH))
            recvs.append(pltpu.make_async_remote_copy(
                src_ref=v_ref, dst_ref=out_ref.at[4 * px + 2 * py + pc], send_sem=send_sems.at[k - 1],
                recv_sem=recv_sems.at[k - 1], device_id=(px, py, pc), device_id_type=MESH))
        for cp in sends:
            cp.start()
        for cp in recvs:
            cp.wait_recv()
        for cp in sends:
            cp.wait_send()

    vm = pl.BlockSpec(memory_space=pltpu.VMEM)
    return pl.pallas_call(
        body, name=name, in_specs=[vm], out_specs=vm, out_shape=jax.ShapeDtypeStruct((8, r, c), v.dtype),
        scratch_shapes=[pltpu.SemaphoreType.DMA((7,)), pltpu.SemaphoreType.DMA((7,))],
        compiler_params=pltpu.CompilerParams(vmem_limit_bytes=VMEM_LIMIT),
    )(v)


def _as_rows(a, width):
    n = math.prod(a.shape)
    if n % width == 0:
        return a.reshape(-1, width)
    return jnp.pad(a.reshape(1, -1), ((0, 0), (0, -n % width))).reshape(-1, width)


def _n_rows(shape, width):
    return -(-math.prod(shape) // width)


def _pack_rows(arrays, rows, width):
    parts = [_as_rows(a, width) for a in arrays]
    used = sum(p.shape[0] for p in parts)
    return jnp.concatenate(parts + [jnp.zeros((rows - used, width), arrays[0].dtype)], axis=0)


def _unpack_rows(pack, shapes):
    width = pack.shape[1]
    out, off = [], 0
    for shp in shapes:
        nr, n = _n_rows(shp, width), math.prod(shp)
        part = pack[off:off + nr]
        out.append(part.reshape(shp) if n % width == 0 else part.reshape(-1)[:n].reshape(shp))
        off += nr
    return out


def _rows_for(shapes, width, mult=8):
    n = sum(_n_rows(s, width) for s in shapes)
    return -(-n // mult) * mult


def _gdn_fwd(x, p):
    proj = mm(x, p["w_main"], name="gdn_proj")
    ba = mm(x, p["w_gate"], name="gdn_proj_gate")
    q, k, v = (gdn_pre(proj, p["conv"], kind) for kind in "qkv")
    beta_b, g_b = gdn_gates(ba, p["alog_row"], p["dt_row"])
    o2, st, tinv = gdn_rec_fwd(q, k, v, beta_b, g_b)
    y = mixer_post(o2, proj, p["norm_g"], A_DK, 3 * A_HEADS, "gdn_post")
    m = mm(y, p["w_out"], name="gdn_out")
    return m, (x, proj, ba, q, k, v, beta_b, g_b, o2, st, tinv, y)


def _gdn_bwd(saved, p, dm):
    x, proj, ba, q, k, v, beta_b, g_b, o2, st, tinv, y = saved
    d_w_out = mm(y, dm, "tn", name="gdn_dw_out")
    dy = mm(dm, p["w_out"], "nt", name="gdn_dy")
    do, dz, d_norm_g = mixer_post_bwd(o2, proj, p["norm_g"], dy, A_DK, 3 * A_HEADS, "gdn_post_bwd")
    dq2, dk2, dv2, dbb, dgb = gdn_rec_bwd(q, k, v, beta_b, g_b, st, tinv, do)
    dba, d_alog_row, d_dt_row = gdn_gates_bwd(ba, p["alog_row"], p["dt_row"], dbb, dgb)
    du, dconv = zip(*(gdn_pre_bwd(proj, p["conv"], d2, kind) for d2, kind in ((dq2, "q"), (dk2, "k"), (dv2, "v"))))
    dproj = jnp.concatenate(list(du) + [dz], axis=1)
    d_w_main = mm(x, dproj, "tn", name="gdn_dw_main")
    d_w_gate = mm(x, dba, "tn", name="gdn_dw_gate")
    dx = mm(dba, p["w_gate"], "nt", epi="add", extra=dm, alpha=ALPHA, name="gdn_dx_gate")
    dx = mm(dproj, p["w_main"], "nt", epi="add", extra=dx, name="gdn_dx")
    grads = dict(w_in=jnp.concatenate([d_w_main, d_w_gate[:, :2 * N_GATE]], axis=1), conv=jnp.concatenate(dconv, axis=1),
                 alog=d_alog_row[0, N_GATE:2 * N_GATE].reshape(2, A_HEADS), dt=d_dt_row[0, N_GATE:2 * N_GATE].reshape(2, A_HEADS),
                 norm_g=d_norm_g[0], w_out=d_w_out)
    return dx, grads


def _gla_fwd(x, p):
    proj = mm(x, p["w_main"], name="gla_proj")
    gl = mm(x, p["w_gate"], name="gla_proj_gate")
    log_a = gla_gate(gl, p["w2"], p["gate_b"])
    o2, st = gla_rec_fwd(proj, log_a)
    y = mixer_post(o2, proj, p["norm_g"], B_DV, (2 * B_KW + B_VW) // B_DV, "gla_post")
    m = mm(y, p["w_out"], name="gla_out")
    return m, (x, proj, gl, log_a, o2, st, y)


def _gla_bwd(saved, p, dm):
    x, proj, gl, log_a, o2, st, y = saved
    d_w_out = mm(y, dm, "tn", name="gla_dw_out")
    dy = mm(dm, p["w_out"], "nt", name="gla_dy")
    do, dr, d_norm_g = mixer_post_bwd(o2, proj, p["norm_g"], dy, B_DV, (2 * B_KW + B_VW) // B_DV, "gla_post_bwd")
    dq2, dk2, dv2, dla = gla_rec_bwd(proj, log_a, st, do)
    dgl, dz, d_b0, d_b1 = gla_gate_bwd(gl, p["w2"], p["gate_b"], dla)
    d_w2 = [mm(gl, dz[n], "tn", name="gla_dw_gate_w2") for n in range(2)]
    dproj = jnp.concatenate([dq2[0] + dq2[1], dk2[0] + dk2[1], dv2[0] + dv2[1], dr], axis=1)
    d_w_main = mm(x, dproj, "tn", name="gla_dw_main")
    d_w_gate = mm(x, dgl, "tn", name="gla_dw_gate")
    dx = mm(dgl, p["w_gate"], "nt", epi="add", extra=dm, alpha=ALPHA, name="gla_dx_gate")
    dx = mm(dproj, p["w_main"], "nt", epi="add", extra=dx, name="gla_dx")
    grads = dict(w_in=jnp.concatenate([d_w_main, d_w_gate[:, :2 * B_RANK]], axis=1),
                 gate_w2=jnp.stack([d_w2[n][n * B_RANK:(n + 1) * B_RANK] for n in range(2)]),
                 gate_b=jnp.concatenate([d_b0, d_b1]), norm_g=d_norm_g[0], w_out=d_w_out)
    return dx, grads


def _pad_cols(w, width=LANES):
    return jnp.pad(w, ((0, 0), (0, width - w.shape[1])))


def _local_step(x, target, a_w_in, a_conv, a_alog, a_dt_bias, a_norm_g, a_w_out, b_w_in, b_gate_w2, b_gate_b, b_norm_g,
                b_w_out, ln1_g, ln1_b, mlp_w1, mlp_w2, ln2_g, ln2_b, grad_pack=None):
    layer_p = []
    for i in range(DEPTH):
        j = i // 2
        if i % 2 == 0:
            layer_p.append(dict(
                w_main=a_w_in[j][:, :4 * A_W], w_gate=_pad_cols(a_w_in[j][:, 4 * A_W:]), conv=a_conv[j],
                alog_row=jnp.pad(a_alog[j].reshape(1, N_GATE), ((0, 0), (N_GATE, LANES - 2 * N_GATE))),
                dt_row=jnp.pad(a_dt_bias[j].reshape(1, N_GATE), ((0, 0), (N_GATE, LANES - 2 * N_GATE))),
                norm_g=a_norm_g[j].reshape(1, A_DK), w_out=a_w_out[j]))
        else:
            w2 = jnp.stack([jnp.pad(b_gate_w2[j][n], ((n * B_RANK, LANES - (n + 1) * B_RANK), (0, 0))) for n in range(2)])
            layer_p.append(dict(
                w_main=b_w_in[j][:, :2 * B_KW + 2 * B_VW], w_gate=_pad_cols(b_w_in[j][:, 2 * B_KW + 2 * B_VW:]),
                w2=w2, gate_b=b_gate_b[j].reshape(2, 1, B_KW), norm_g=b_norm_g[j].reshape(1, B_DV), w_out=b_w_out[j]))

    saved = []
    h = x
    for i in range(DEPTH):
        p = layer_p[i]
        m, sv = (_gdn_fwd if i % 2 == 0 else _gla_fwd)(h, p)
        x1 = ln_fwd(h, m, ln1_g[i:i + 1], ln1_b[i:i + 1])
        h1 = mm(x1, mlp_w1[i], name="mlp_up")
        mlp = mm(h1, mlp_w2[i], act="sqrelu", name="mlp_down")
        x2 = ln_fwd(x1, mlp, ln2_g[i:i + 1], ln2_b[i:i + 1])
        saved.append((sv, h, m, x1, h1, mlp))
        h = x2

    dh, loss_part = loss_head(h, target)

    g_a, g_b, g_ln1g, g_ln1b, g_ln2g, g_ln2b, g_w1, g_w2 = {}, {}, {}, {}, {}, {}, {}, {}
    pack = None
    for i in reversed(range(DEPTH)):
        sv, xin, m, x1, h1, mlp = saved[i]
        p = layer_p[i]
        dr2, g_ln2g[i], g_ln2b[i] = ln_bwd(x1, mlp, ln2_g[i:i + 1], dh)
        if grad_pack is None:
            g_w2[i] = mm(h1, dr2, "tn", act="sqrelu", name="mlp_dw_down")
        else:
            pack = mm(h1, dr2, "tn", act="sqrelu", pack=(pack, grad_pack[0], grad_pack[2] + i, "rows"), name="mlp_dw_down")
        dh1 = mm(dr2, mlp_w2[i], "nt", epi="dsqrelu", extra=h1, name="mlp_dh")
        if grad_pack is None:
            g_w1[i] = mm(x1, dh1, "tn", chip_major=True, name="mlp_dw_up")
        else:
            pack = mm(x1, dh1, "tn", pack=(pack, grad_pack[0], grad_pack[1] + i, "cols"), name="mlp_dw_up")
        dx1 = mm(dh1, mlp_w1[i], "nt", epi="add", extra=dr2, alpha=ALPHA, name="mlp_dx")
        dr1, g_ln1g[i], g_ln1b[i] = ln_bwd(xin, m, ln1_g[i:i + 1], dx1)
        dh, g = (_gdn_bwd if i % 2 == 0 else _gla_bwd)(sv, p, dr1)
        (g_a if i % 2 == 0 else g_b)[i // 2] = g

    per_layer = lambda d, key=None: [(d[i] if key is None else d[i][key]) for i in sorted(d)]
    st = lambda d, key=None: jnp.stack(per_layer(d, key))
    grads = dict(
        a_w_in=per_layer(g_a, "w_in"), a_conv=st(g_a, "conv"), a_alog=st(g_a, "alog"), a_dt_bias=st(g_a, "dt"),
        a_norm_g=st(g_a, "norm_g"), a_w_out=per_layer(g_a, "w_out"), b_w_in=per_layer(g_b, "w_in"),
        b_gate_w2=st(g_b, "gate_w2"), b_gate_b=st(g_b, "gate_b"), b_norm_g=st(g_b, "norm_g"),
        b_w_out=per_layer(g_b, "w_out"), ln1_g=st(g_ln1g)[:, 0], ln1_b=st(g_ln1b)[:, 0], mlp_w1=per_layer(g_w1),
        mlp_w2=per_layer(g_w2), ln2_g=st(g_ln2g)[:, 0], ln2_b=st(g_ln2b)[:, 0], pack=pack)
    return loss_part, dh, grads


WEIGHTS = ("a_w_in", "a_conv", "a_alog", "a_dt_bias", "a_norm_g", "a_w_out", "b_w_in", "b_gate_w2", "b_gate_b",
           "b_norm_g", "b_w_out", "ln1_g", "ln1_b", "mlp_w1", "mlp_w2", "ln2_g", "ln2_b")
BIG = ("mlp_w1", "mlp_w2", "a_w_out", "b_w_out", "a_w_in", "b_w_in")
SHARD_AXIS = {"mlp_w1": 2, "mlp_w2": 1, "a_w_out": 1, "b_w_out": 1, "a_w_in": 2, "b_w_in": 2}
SMALL = tuple(n for n in WEIGHTS if n not in BIG)
SMALL_SHARD_AXIS = {"a_conv": 2, "b_gate_w2": 3, "b_gate_b": 2, "b_norm_g": 1}


def _to_chip_major(full, axis):
    shp = full.shape
    t = full.reshape(shp[:axis] + (4, shp[axis] // 4) + shp[axis + 1:])
    return jnp.moveaxis(t, axis, 0)


def _from_chip_major(stacked, axis):
    t = jnp.moveaxis(stacked, 0, axis)
    shp = t.shape
    return t.reshape(shp[:axis] + (shp[axis] * shp[axis + 1],) + shp[axis + 2:])


def kernel(x, a_w_in, a_conv, a_alog, a_dt_bias, a_norm_g, a_w_out, b_w_in, b_gate_w2, b_gate_b, b_norm_g, b_w_out, ln1_g, ln1_b, mlp_w1, mlp_w2, ln2_g, ln2_b, loss_target, m_a_w_in, m_a_conv, m_a_alog, m_a_dt_bias, m_a_norm_g, m_a_w_out, m_b_w_in, m_b_gate_w2, m_b_gate_b, m_b_norm_g, m_b_w_out, m_ln1_g, m_ln1_b, m_mlp_w1, m_mlp_w2, m_ln2_g, m_ln2_b, v_a_w_in, v_a_conv, v_a_alog, v_a_dt_bias, v_a_norm_g, v_a_w_out, v_b_w_in, v_b_gate_w2, v_b_gate_b, v_b_norm_g, v_b_w_out, v_ln1_g, v_ln1_b, v_mlp_w1, v_mlp_w2, v_ln2_g, v_ln2_b):
    w = dict(a_w_in=a_w_in, a_conv=a_conv, a_alog=a_alog, a_dt_bias=a_dt_bias, a_norm_g=a_norm_g, a_w_out=a_w_out,
             b_w_in=b_w_in, b_gate_w2=b_gate_w2, b_gate_b=b_gate_b, b_norm_g=b_norm_g, b_w_out=b_w_out, ln1_g=ln1_g,
             ln1_b=ln1_b, mlp_w1=mlp_w1, mlp_w2=mlp_w2, ln2_g=ln2_g, ln2_b=ln2_b)
    mom = dict(a_w_in=m_a_w_in, a_conv=m_a_conv, a_alog=m_a_alog, a_dt_bias=m_a_dt_bias, a_norm_g=m_a_norm_g,
               a_w_out=m_a_w_out, b_w_in=m_b_w_in, b_gate_w2=m_b_gate_w2, b_gate_b=m_b_gate_b, b_norm_g=m_b_norm_g,
               b_w_out=m_b_w_out, ln1_g=m_ln1_g, ln1_b=m_ln1_b, mlp_w1=m_mlp_w1, mlp_w2=m_mlp_w2, ln2_g=m_ln2_g,
               ln2_b=m_ln2_b)
    var = dict(a_w_in=v_a_w_in, a_conv=v_a_conv, a_alog=v_a_alog, a_dt_bias=v_a_dt_bias, a_norm_g=v_a_norm_g,
               a_w_out=v_a_w_out, b_w_in=v_b_w_in, b_gate_w2=v_b_gate_w2, b_gate_b=v_b_gate_b, b_norm_g=v_b_norm_g,
               b_w_out=v_b_w_out, ln1_g=v_ln1_g, ln1_b=v_ln1_b, mlp_w1=v_mlp_w1, mlp_w2=v_mlp_w2, ln2_g=v_ln2_g,
               ln2_b=v_ln2_b)
    chip = 2 * lax.axis_index("x") + lax.axis_index("y")

    seg_rows = [w[n].size // D_MODEL for n in BIG]
    seg_off = [sum(seg_rows[:i]) for i in range(len(BIG))]
    rows = -(-sum(seg_rows) // PACK_TILE) * PACK_TILE
    shard_pack = jnp.concatenate([w[n].reshape(-1, D_MODEL) for n in BIG]
                                 + [jnp.zeros((rows - sum(seg_rows), D_MODEL), F32)], axis=0)
    chip_idx = chip.astype(jnp.int32).reshape(1)
    gathered = allgather_chips(cast_into_slot(shard_pack, chip_idx))
    full = {}
    for n, off, nr in zip(BIG, seg_off, seg_rows):
        if n in ("mlp_w1", "mlp_w2"):
            kind = "cols" if SHARD_AXIS[n] == 2 else "rows"
            full[n] = [Gathered(gathered, off + i * D_MODEL, kind) for i in range(DEPTH)]
            continue
        stacked = gathered[:, off:off + nr].reshape((4,) + w[n].shape)
        full[n] = _from_chip_major(stacked, SHARD_AXIS[n])
    sharded_small = tuple(SMALL_SHARD_AXIS)
    sm_shapes = [w[n].shape for n in sharded_small]
    sm_rows = _rows_for(sm_shapes, LANES)
    sm_all = exchange_all(_pack_rows([w[n] for n in sharded_small], sm_rows, LANES), "gather_small")
    per_chip = [_unpack_rows(sm_all[2 * pch], sm_shapes) for pch in range(4)]
    for idx, n in enumerate(sharded_small):
        full[n] = jnp.concatenate([per_chip[pch][idx] for pch in range(4)], axis=SMALL_SHARD_AXIS[n])
    for n in WEIGHTS:
        full.setdefault(n, w[n])

    blk_of = {n: off // D_MODEL for n, off in zip(BIG, seg_off)}
    loss_part, grad_x, grads = _local_step(x[0], loss_target[0], *[full[n] for n in WEIGHTS],
                                           grad_pack=(rows, blk_of["mlp_w1"], blk_of["mlp_w2"]))
    loss = lax.psum(jnp.sum(loss_part), ("x", "y", "c"))

    gpack = grads["pack"]
    rest = jnp.concatenate(
        [_to_chip_major(g, SHARD_AXIS[n] - 1).reshape(4, -1, D_MODEL) for n in BIG[2:] for g in grads[n]]
        + [jnp.zeros((4, rows - sum(seg_rows), D_MODEL), F32)], axis=1)
    gpack = lax.dynamic_update_slice(gpack, rest, (0, seg_off[2], 0))
    core = lax.axis_index("c").astype(jnp.int32).reshape(1)
    theirs = swap_sibling(half_to_bf16(gpack, 1 - core))
    chip_sum = half_to_bf16(gpack, core, theirs)
    ax, ay = lax.axis_index("x"), lax.axis_index("y")
    slot_x = (2 * (1 - ax) + ay).astype(jnp.int32).reshape(1)
    slot_y = (2 * ax + (1 - ay)).astype(jnp.int32).reshape(1)
    to_nbrs = merge_first_hop(chip_sum, scatter_first_hop(chip_sum), slot_x, slot_y, core)
    reduced = join_halves(sum_received(chip_sum, scatter_second_hop(to_nbrs), chip_idx, core))
    out_g, out_d, out_m, out_v = {}, {}, {}, {}
    for n, off, nr in zip(BIG, seg_off, seg_rows):
        if w[n].shape[-1] == D_MODEL:
            view = lambda t: t.reshape(-1, D_MODEL)
            res = adamw(view(w[n]), view(mom[n]), view(var[n]), (reduced,), off, "adamw_" + n)
        else:
            cols = w[n].shape[-1]
            view = lambda t: t.reshape(-1, cols)
            res = adamw(view(w[n]), view(mom[n]), view(var[n]), (view(reduced[off:off + nr]),), 0, "adamw_" + n)
        out_g[n], out_d[n], out_m[n], out_v[n] = (t.reshape(w[n].shape) for t in res)

    all_shapes = [full[n].shape for n in SMALL]
    g_rows = _rows_for(all_shapes, LANES)
    g_all = exchange_all(_pack_rows([grads[n] for n in SMALL], g_rows, LANES), "gather_small_grads")
    g_sum = _unpack_rows(sum_slots(g_all, "sum_small_grads"), all_shapes)
    g_mine = []
    for n, g in zip(SMALL, g_sum):
        if n in SMALL_SHARD_AXIS:
            ax = SMALL_SHARD_AXIS[n]
            g = lax.dynamic_slice_in_dim(g, chip * w[n].shape[ax], w[n].shape[ax], axis=ax)
        g_mine.append(g)
    my_shapes = [w[n].shape for n in SMALL]
    s_rows = _rows_for(my_shapes, LANES)
    pk = lambda d: _pack_rows([d[n] for n in SMALL], s_rows, LANES)
    res = adamw(pk(w), pk(mom), pk(var), (_pack_rows(g_mine, s_rows, LANES),), 0, "adamw_small")
    for dst, pack in zip((out_g, out_d, out_m, out_v), res):
        for n, t in zip(SMALL, _unpack_rows(pack, my_shapes)):
            dst[n] = t

    return (loss, grad_x[None], *[out_g[n] for n in WEIGHTS], *[out_d[n] for n in WEIGHTS],
            *[out_m[n] for n in WEIGHTS], *[out_v[n] for n in WEIGHTS])
```

```python
import functools
import math

import jax
import jax.numpy as jnp
from jax import lax
from jax.experimental import pallas as pl
from jax.experimental.pallas import tpu as pltpu

F32 = jnp.float32
BF16 = jnp.bfloat16

D_MODEL = 1024
DEPTH = 4
CHUNK = 64
A_HEADS = 8
A_DK = 128
A_W = 1024
A_CONV = 5
B_HEADS = 4
B_DK = 128
B_DV = 256
B_RANK = 16
B_TAU = 16.0
B_KW = 512
B_VW = 1024
ALPHA = (2 * DEPTH) ** 0.25
LN_EPS = 1e-5
RMS_EPS = 1e-6
L2_EPS = 1e-6
ADAM_LR = 0.001
ADAM_B1 = 0.9
ADAM_B2 = 0.999
ADAM_EPS = 1e-08
ADAM_WD = 0.01
ADAM_STEP = 10
LANES = 128
NEG_INF = float("-inf")
VMEM_LIMIT = 56 * 1024 * 1024


def _cparams(sem=None):
    return pltpu.CompilerParams(dimension_semantics=sem, vmem_limit_bytes=VMEM_LIMIT)


def _dg(a, b, ca, cb):
    return lax.dot_general(a.astype(BF16), b.astype(BF16), (((ca,), (cb,)), ((), ())),
                           preferred_element_type=F32)


def _split(x):
    hi = x.astype(BF16)
    return hi, (x - hi.astype(F32)).astype(BF16)


def _dg3(a, b, ca, cb):
    (a1, a2), (b1, b2) = _split(a), _split(b)
    return (_dg(a1, b2, ca, cb) + _dg(a2, b1, ca, cb)) + _dg(a1, b1, ca, cb)


def _dot_with_vjp(dg):
    @functools.partial(jax.custom_vjp, nondiff_argnums=(2, 3))
    def dot(a, b, ca, cb):
        return dg(a, b, ca, cb)

    def fwd(a, b, ca, cb):
        return dg(a, b, ca, cb), (a, b)

    def bwd(ca, cb, res, g):
        a, b = res
        da = dg(g, b, 1, 1 - cb) if ca == 1 else dg(b, g, 1 - cb, 1)
        db = dg(a, g, 1 - ca, 0) if cb == 0 else dg(g, a, 0, 1 - ca)
        return da, db

    dot.defvjp(fwd, bwd)
    return dot


bdot = _dot_with_vjp(_dg)
xdot3 = _dot_with_vjp(_dg3)


def nn(a, b):
    return bdot(a, b, 1, 0)


def nt(a, b):
    return bdot(a, b, 1, 1)


def tn(a, b):
    return bdot(a, b, 0, 0)


def xdot(a, b):
    return xdot3(a, b, 1, 0)


def _sigmoid(x):
    return 1.0 / (1.0 + jnp.exp(-x))


def _softplus(x):
    return jnp.maximum(x, 0.0) + jnp.log(1.0 + jnp.exp(-jnp.abs(x)))


def _chunk_masks(rev):
    ii = lax.broadcasted_iota(jnp.int32, (CHUNK, CHUNK), 0)
    jj = lax.broadcasted_iota(jnp.int32, (CHUNK, CHUNK), 1)
    d = (ii - jj) * (1 - 2 * rev)
    return d >= 0, d > 0, ii == jj, (ii >> 3) == (jj >> 3)


def _each(f, *lists):
    return [f(*xs) for xs in zip(*lists)]


@jax.custom_vjp
def _unit_triangular_inverse(a, ident, blockdiag):
    return _unit_triangular_inverse_impl(a, ident, blockdiag)


def _unit_triangular_inverse_fwd(a, ident, blockdiag):
    t = _unit_triangular_inverse_impl(a, ident, blockdiag)
    return t, (t, ident, blockdiag)


def _unit_triangular_inverse_bwd(res, g):
    t, ident, blockdiag = res
    left = _each(lambda x, y: xdot3(x, y, 0, 0), t, g)
    da = _each(lambda x, y: -xdot3(x, y, 1, 1), left, t)
    return da, jnp.zeros_like(ident), jnp.zeros_like(blockdiag)


_unit_triangular_inverse.defvjp(_unit_triangular_inverse_fwd, _unit_triangular_inverse_bwd)


@jax.custom_vjp
def _known_inverse(a, t):
    return t


def _known_inverse_bwd(t, g):
    left = _each(lambda x, y: xdot3(x, y, 0, 0), t, g)
    return _each(lambda x, y: -xdot3(x, y, 1, 1), left, t), _each(jnp.zeros_like, t)


_known_inverse.defvjp(lambda a, t: (t, t), _known_inverse_bwd)


def _unit_triangular_inverse_impl(a, ident, blockdiag):
    ad = _each(lambda x: x * blockdiag, a)
    e = _each(lambda x, y: x - y, a, ad)
    dinv = _each(lambda x: ident - x, ad)
    p = _each(xdot, ad, ad)
    dinv = _each(lambda x, y: x + xdot(x, y), dinv, p)
    p = _each(xdot, p, p)
    dinv = _each(lambda x, y: x + xdot(x, y), dinv, p)
    g = _each(lambda x, y: -xdot(x, y), dinv, e)
    finv = _each(lambda x: ident + x, g)
    p = _each(xdot, g, g)
    finv = _each(lambda x, y: x + xdot(x, y), finv, p)
    p = _each(xdot, p, p)
    finv = _each(lambda x, y: x + xdot(x, y), finv, p)
    return _each(xdot, finv, dinv)


def _gdn_step(state, q, k, v, bb, gb, rev, t_saved=None):
    causal, strict, eye, blockdiag = _chunk_masks(rev)
    lower = causal.astype(F32)
    ones = jnp.ones((CHUNK, CHUNK), F32)
    gcb = _each(lambda x: xdot(lower, x), gb)
    gcol = _each(lambda x: x[:, :CHUNK], gcb)
    grow = _each(lambda x: xdot(ones, jnp.where(eye, x, 0.0)), gcol)
    decay = _each(lambda x, y: jnp.exp(jnp.where(causal, x - y, NEG_INF)), gcol, grow)
    kb = _each(lambda x, y: x * y, k, bb)
    a = _each(lambda x, y, z: jnp.where(strict, nt(x, y) * z, 0.0), kb, k, decay)
    if t_saved is None:
        t = _unit_triangular_inverse(a, eye.astype(F32), blockdiag.astype(F32))
    else:
        t = _known_inverse(a, t_saved)
    egc = _each(jnp.exp, gcb)
    u = _each(lambda x, y, z: xdot(x, y * z), t, v, bb)
    w = _each(lambda x, y, z: xdot(x, y * z), t, kb, egc)
    qk = _each(lambda x, y, z: nt(x, y) * z, q, k, decay)
    glast = _each(lambda x: jnp.sum(x, axis=0, keepdims=True), gb)
    v_new = _each(lambda x, y, z: x - nn(y, z), u, w, state)
    o = _each(lambda x, y, z, p, r: nn(x * y, z) + nn(p, r), q, egc, state, qk, v_new)
    k_dec = _each(lambda x, y, z: x * jnp.exp(y - z), k, glast, gcb)
    state_new = _each(lambda x, y, z, p: x * jnp.exp(y) + tn(z, p), state, glast, k_dec, v_new)
    return state_new, o, t


def _gla_step(state_t, q, k, v, la, rev):
    causal, _, _, _ = _chunk_masks(rev)
    lower = causal.astype(F32)
    sign = 1 - 2 * rev
    b = _each(lambda x: xdot(lower, x), la)
    q = _each(lambda x: x * (B_DK ** -0.5), q)
    row = lax.broadcasted_iota(jnp.int32, (CHUNK, B_DK), 0)
    sub = row // GLA_SUB
    parts = []
    for blk in range(CHUNK // GLA_SUB):
        rows = slice(blk * GLA_SUB, (blk + 1) * GLA_SUB)
        r_at = jnp.where(rev == 1, GLA_SUB * (blk + 1), GLA_SUB * blk - 1)
        r = _each(lambda x: jnp.sum(jnp.where(row == r_at, x, 0.0), axis=0, keepdims=True), b)
        q_blk = _each(lambda x, y, z: x[rows] * jnp.exp(y[rows] - z), q, b, r)
        k_past = _each(lambda x, y, z: x * jnp.exp(jnp.where((sub - blk) * sign < 0, z - y, NEG_INF)), k, b, r)
        parts.append(_each(lambda x, y: xdot3(x, y, 1, 1), q_blk, k_past))
    scores = _each(lambda *p: jnp.concatenate(p, axis=0), *parts)
    shp = (GLA_SUB, GLA_SUB, B_DK)
    d3 = (lax.broadcasted_iota(jnp.int32, shp, 0) - lax.broadcasted_iota(jnp.int32, shp, 1)) * sign
    place_r = lax.broadcasted_iota(jnp.int32, (GLA_SUB, CHUNK), 0)
    place_c = lax.broadcasted_iota(jnp.int32, (GLA_SUB, CHUNK), 1)
    diag = []
    for blk in range(CHUNK // GLA_SUB):
        rows = slice(blk * GLA_SUB, (blk + 1) * GLA_SUB)
        place = (place_c == place_r + blk * GLA_SUB).astype(F32)

        def pairs(qh, kh, bh):
            qb, kb, bb = qh[rows], kh[rows], bh[rows]
            dec = jnp.exp(jnp.where(d3 >= 0, bb[:, None, :] - bb[None, :, :], NEG_INF))
            return xdot(jnp.sum(qb[:, None, :] * kb[None, :, :] * dec, axis=-1), place)

        diag.append(_each(pairs, q, k, b))
    scores = _each(lambda x, *d: x + jnp.concatenate(d, axis=0), scores, *diag)
    blast = _each(lambda x: jnp.sum(x, axis=0, keepdims=True), la)
    o = _each(lambda x, y, z, s, w: nt(x * jnp.exp(y), z) + nn(s, w), q, b, state_t, scores, v)
    k_dec = _each(lambda x, y, z: x * jnp.exp(y - z), k, blast, b)
    state_new = _each(lambda x, y, z, w: jnp.exp(x) * y + tn(z, w), blast, state_t, v, k_dec)
    return state_new, o


def _chunk_pos(d, m, n):
    return m + d * (n - 1 - 2 * m)


GLA_SUB = 16
GDN_HEADS_PER_STEP = 8
def gdn_rec_fwd(q, k, v, beta_b, g_b):
    s = q.shape[0]
    n = s // CHUNK

    hb = GDN_HEADS_PER_STEP
    wide = hb * LANES

    def body(q_ref, k_ref, v_ref, bb_ref, gb_ref, o_ref, st_ref, t_ref, state):
        d = pl.program_id(0)

        @pl.when(pl.program_id(2) == 0)
        def _():
            state[...] = jnp.zeros_like(state)

        cols = [slice(hh * LANES, (hh + 1) * LANES) for hh in range(hb)]
        st = [state[hh] for hh in range(hb)]
        new, o, t = _gdn_step(st, *([r[:, c] for c in cols] for r in (q_ref, k_ref, v_ref, bb_ref, gb_ref)), d)
        for hh, c in enumerate(cols):
            st_ref[hh] = st[hh]
            t_ref[hh] = t[hh]
            state[hh] = new[hh]
            o_ref[:, c] = o[hh]

    blk = pl.BlockSpec((CHUNK, wide), lambda d, h, m: (_chunk_pos(d, m, n), h))
    gate = pl.BlockSpec((CHUNK, wide), lambda d, h, m: (_chunk_pos(d, m, n), d * (A_HEADS // hb) + h))
    return pl.pallas_call(
        body, name="gdn_rec_fwd", grid=(2, A_HEADS // hb, n),
        in_specs=[blk, blk, blk, gate, gate],
        out_specs=[pl.BlockSpec((None, CHUNK, wide), lambda d, h, m: (d, _chunk_pos(d, m, n), h)),
                   pl.BlockSpec((None, hb, None, A_DK, LANES), lambda d, h, m: (d, h, _chunk_pos(d, m, n), 0, 0)),
                   pl.BlockSpec((None, hb, None, CHUNK, CHUNK), lambda d, h, m: (d, h, _chunk_pos(d, m, n), 0, 0))],
        out_shape=[jax.ShapeDtypeStruct((2, s, A_W), F32), jax.ShapeDtypeStruct((2, A_HEADS, n, A_DK, LANES), F32),
                   jax.ShapeDtypeStruct((2, A_HEADS, n, CHUNK, CHUNK), F32)],
        scratch_shapes=[pltpu.VMEM((hb, A_DK, LANES), F32)],
        compiler_params=_cparams(("arbitrary", "arbitrary", "arbitrary")),
    )(q, k, v, beta_b, g_b)


def gdn_rec_bwd(q, k, v, beta_b, g_b, states, tinv, do):
    s = q.shape[0]
    n = s // CHUNK

    hb = GDN_HEADS_PER_STEP
    wide = hb * LANES

    def body(q_ref, k_ref, v_ref, bb_ref, gb_ref, st_ref, t_ref, do_ref, dq_ref, dk_ref, dv_ref, dbb_ref, dgb_ref, dstate):
        d = pl.program_id(0)

        @pl.when(pl.program_id(2) == 0)
        def _():
            dstate[...] = jnp.zeros_like(dstate)

        def step(*a):
            return _gdn_step(*a, d, t_saved=[t_ref[hh] for hh in range(hb)])[:2]

        cols = [slice(hh * LANES, (hh + 1) * LANES) for hh in range(hb)]
        _, vjp = jax.vjp(step, [st_ref[hh] for hh in range(hb)],
                         *([r[:, c] for c in cols] for r in (q_ref, k_ref, v_ref, bb_ref, gb_ref)))
        grads = vjp(([dstate[hh] for hh in range(hb)], [do_ref[:, c] for c in cols]))
        for hh, c in enumerate(cols):
            dstate[hh], dq_ref[:, c], dk_ref[:, c], dv_ref[:, c], dbb_ref[:, c], dgb_ref[:, c] = (g[hh] for g in grads)

    pos = lambda d, m: _chunk_pos(1 - d, m, n)
    blk = pl.BlockSpec((CHUNK, wide), lambda d, h, m: (pos(d, m), h))
    gate = pl.BlockSpec((CHUNK, wide), lambda d, h, m: (pos(d, m), d * (A_HEADS // hb) + h))
    oblk = pl.BlockSpec((None, CHUNK, wide), lambda d, h, m: (d, pos(d, m), h))
    return pl.pallas_call(
        body, name="gdn_rec_bwd", grid=(2, A_HEADS // hb, n),
        in_specs=[blk, blk, blk, gate, gate,
                  pl.BlockSpec((None, hb, None, A_DK, LANES), lambda d, h, m: (d, h, pos(d, m), 0, 0)),
                  pl.BlockSpec((None, hb, None, CHUNK, CHUNK), lambda d, h, m: (d, h, pos(d, m), 0, 0)), blk],
        out_specs=[oblk, oblk, oblk, gate, gate],
        out_shape=[jax.ShapeDtypeStruct((2, s, A_W), F32)] * 3 + [jax.ShapeDtypeStruct(beta_b.shape, F32)] * 2,
        scratch_shapes=[pltpu.VMEM((hb, A_DK, LANES), F32)],
        compiler_params=_cparams(("arbitrary", "arbitrary", "arbitrary")),
    )(q, k, v, beta_b, g_b, states, tinv, do)


def gla_rec_fwd(proj, log_a):
    s = proj.shape[0]
    n = s // CHUNK

    kcols = [slice(h * B_DK, (h + 1) * B_DK) for h in range(B_HEADS)]
    vcols = [slice(h * B_DV, (h + 1) * B_DV) for h in range(B_HEADS)]

    def body(q_ref, k_ref, v_ref, la_ref, o_ref, st_ref, state):
        d = pl.program_id(0)

        @pl.when(pl.program_id(1) == 0)
        def _():
            state[...] = jnp.zeros_like(state)

        st = [state[h] for h in range(B_HEADS)]
        new, o = _gla_step(st, [q_ref[:, c] for c in kcols], [k_ref[:, c] for c in kcols], [v_ref[:, c] for c in vcols],
                           [la_ref[:, c] for c in kcols], d)
        for h in range(B_HEADS):
            st_ref[h] = st[h]
            state[h] = new[h]
            o_ref[:, vcols[h]] = o[h]

    pos = lambda d, m: _chunk_pos(d, m, n)
    return pl.pallas_call(
        body, name="gla_rec_fwd", grid=(2, n),
        in_specs=[pl.BlockSpec((CHUNK, B_KW), lambda d, m: (pos(d, m), 0)),
                  pl.BlockSpec((CHUNK, B_KW), lambda d, m: (pos(d, m), 1)),
                  pl.BlockSpec((CHUNK, B_VW), lambda d, m: (pos(d, m), 2 * B_KW // B_VW)),
                  pl.BlockSpec((None, CHUNK, B_KW), lambda d, m: (d, pos(d, m), 0))],
        out_specs=[pl.BlockSpec((None, CHUNK, B_VW), lambda d, m: (d, pos(d, m), 0)),
                   pl.BlockSpec((None, B_HEADS, None, B_DV, B_DK), lambda d, m: (d, 0, pos(d, m), 0, 0))],
        out_shape=[jax.ShapeDtypeStruct((2, s, B_VW), F32), jax.ShapeDtypeStruct((2, B_HEADS, n, B_DV, B_DK), F32)],
        scratch_shapes=[pltpu.VMEM((B_HEADS, B_DV, B_DK), F32)],
        compiler_params=_cparams(("arbitrary", "arbitrary")),
    )(proj, proj, proj, log_a)


def gla_rec_bwd(proj, log_a, states, do):
    s = proj.shape[0]
    n = s // CHUNK

    kcols = [slice(h * B_DK, (h + 1) * B_DK) for h in range(B_HEADS)]
    vcols = [slice(h * B_DV, (h + 1) * B_DV) for h in range(B_HEADS)]

    def body(q_ref, k_ref, v_ref, la_ref, st_ref, do_ref, dq_ref, dk_ref, dv_ref, dla_ref, dstate):
        d = pl.program_id(0)

        @pl.when(pl.program_id(1) == 0)
        def _():
            dstate[...] = jnp.zeros_like(dstate)

        step = functools.partial(_gla_step, rev=d)
        _, vjp = jax.vjp(step, [st_ref[h] for h in range(B_HEADS)], [q_ref[:, c] for c in kcols],
                         [k_ref[:, c] for c in kcols], [v_ref[:, c] for c in vcols], [la_ref[:, c] for c in kcols])
        dst, dq, dk, dv, dla = vjp(([dstate[h] for h in range(B_HEADS)], [do_ref[:, c] for c in vcols]))
        for h in range(B_HEADS):
            dstate[h] = dst[h]
            dq_ref[:, kcols[h]] = dq[h]
            dk_ref[:, kcols[h]] = dk[h]
            dv_ref[:, vcols[h]] = dv[h]
            dla_ref[:, kcols[h]] = dla[h]

    pos = lambda d, m: _chunk_pos(1 - d, m, n)
    kblk = pl.BlockSpec((None, CHUNK, B_KW), lambda d, m: (d, pos(d, m), 0))
    return pl.pallas_call(
        body, name="gla_rec_bwd", grid=(2, n),
        in_specs=[pl.BlockSpec((CHUNK, B_KW), lambda d, m: (pos(d, m), 0)),
                  pl.BlockSpec((CHUNK, B_KW), lambda d, m: (pos(d, m), 1)),
                  pl.BlockSpec((CHUNK, B_VW), lambda d, m: (pos(d, m), 2 * B_KW // B_VW)),
                  kblk,
                  pl.BlockSpec((None, B_HEADS, None, B_DV, B_DK), lambda d, m: (d, 0, pos(d, m), 0, 0)),
                  pl.BlockSpec((CHUNK, B_VW), lambda d, m: (pos(d, m), 0))],
        out_specs=[kblk, kblk, pl.BlockSpec((None, CHUNK, B_VW), lambda d, m: (d, pos(d, m), 0)), kblk],
        out_shape=[jax.ShapeDtypeStruct((2, s, B_KW), F32), jax.ShapeDtypeStruct((2, s, B_KW), F32),
                   jax.ShapeDtypeStruct((2, s, B_VW), F32), jax.ShapeDtypeStruct((2, s, B_KW), F32)],
        scratch_shapes=[pltpu.VMEM((B_HEADS, B_DV, B_DK), F32)],
        compiler_params=_cparams(("arbitrary", "arbitrary")),
    )(proj, proj, proj, log_a, states, do)


MM_TILE_OUT = 1024
MM_TILE_K = 1024


def _tile(n, pref):
    return pref if n % pref == 0 else n


class Gathered:
    def __init__(self, g, off, kind):
        assert off % D_MODEL == 0 and MM_TILE_OUT == D_MODEL and MM_TILE_K == D_MODEL
        self.g, self.blk, self.kind = g, off // D_MODEL, kind
        self.shape = (D_MODEL, 4 * D_MODEL) if kind == "cols" else (4 * D_MODEL, D_MODEL)

    def spec(self, mode):
        blk = self.blk
        chip_is_k = (self.kind == "rows") == (mode == "nn")
        if chip_is_k:
            return pl.BlockSpec((None, D_MODEL, D_MODEL), lambda i, j, k: (k, blk, 0))
        return pl.BlockSpec((None, D_MODEL, D_MODEL), lambda i, j, k: (j, blk, 0))


def mm(a, b, mode="nn", act=None, epi=None, extra=None, alpha=1.0, chip_major=False, pack=None, name="mm"):
    if mode == "tn":
        kk, m = a.shape
    else:
        m, kk = a.shape
    nn_ = b.shape[0] if mode == "nt" else b.shape[1]
    tm, tn_, tk = _tile(m, MM_TILE_OUT), _tile(nn_, MM_TILE_OUT), _tile(kk, MM_TILE_K)
    nk = kk // tk
    ca, cb = {"nn": (1, 0), "nt": (1, 1), "tn": (0, 0)}[mode]

    def body(*refs):
        o_ref = refs[-1]
        a_ref, b_ref = refs[:2]
        if epi is not None:
            e_ref = refs[2]
        kstep = pl.program_id(2)
        at = a_ref[...]
        if act == "sqrelu":
            at = jnp.square(jnp.maximum(at, 0.0))
        part = _dg(at, b_ref[...], ca, cb)

        @pl.when(kstep == 0)
        def _():
            o_ref[...] = part

        @pl.when(kstep > 0)
        def _():
            o_ref[...] += part

        if epi is not None:
            @pl.when(kstep == nk - 1)
            def _():
                if epi == "dsqrelu":
                    o_ref[...] = o_ref[...] * (2.0 * jnp.maximum(e_ref[...], 0.0))
                else:
                    o_ref[...] = o_ref[...] + alpha * e_ref[...]

    a_spec = pl.BlockSpec((tk, tm), lambda i, j, k: (k, i)) if mode == "tn" else pl.BlockSpec((tm, tk), lambda i, j, k: (i, k))
    if isinstance(b, Gathered):
        assert mode in ("nn", "nt") and tn_ == D_MODEL and tk == D_MODEL
        b_spec, b = b.spec(mode), b.g
    elif mode == "nt":
        b_spec = pl.BlockSpec((tn_, tk), lambda i, j, k: (j, k))
    else:
        b_spec = pl.BlockSpec((tk, tn_), lambda i, j, k: (k, j))
    o_spec = pl.BlockSpec((tm, tn_), lambda i, j, k: (i, j))
    ins, specs = [a, b], [a_spec, b_spec]
    if epi is not None:
        ins.append(extra)
        specs.append(o_spec)
    out_shape = jax.ShapeDtypeStruct((m, nn_), F32)
    if chip_major:
        assert nn_ == 4 * D_MODEL and tn_ == D_MODEL
        o_spec = pl.BlockSpec((None, tm, D_MODEL), lambda i, j, k: (j, i, 0))
        out_shape = jax.ShapeDtypeStruct((4, m, D_MODEL), F32)
    aliases = {}
    if pack is not None:
        buf, rows, blk, kind = pack
        logical = (D_MODEL, 4 * D_MODEL) if kind == "cols" else (4 * D_MODEL, D_MODEL)
        assert epi is None and tm == D_MODEL and tn_ == D_MODEL and (m, nn_) == logical
        if kind == "cols":
            o_spec = pl.BlockSpec((None, D_MODEL, D_MODEL), lambda i, j, k: (j, blk, 0))
        else:
            o_spec = pl.BlockSpec((None, D_MODEL, D_MODEL), lambda i, j, k: (i, blk, 0))
        out_shape = jax.ShapeDtypeStruct((4, rows, D_MODEL), F32)
        if buf is not None:
            aliases = {len(ins): 0}
            ins.append(buf)
            specs.append(pl.BlockSpec(memory_space=pl.ANY))
    return pl.pallas_call(
        body, name=name, grid=(m // tm, nn_ // tn_, nk), in_specs=specs, out_specs=o_spec, out_shape=out_shape,
        input_output_aliases=aliases, compiler_params=_cparams(("parallel", "parallel", "arbitrary")),
    )(*ins)


ROWS = 512
POST_ROWS = 2048
LN_ROWS = 512


def _ln_core(x, m, g, b):
    r = ALPHA * x + m
    mu = jnp.mean(r, axis=-1, keepdims=True)
    xc = r - mu
    var = jnp.mean(xc * xc, axis=-1, keepdims=True)
    rstd = lax.rsqrt(var + LN_EPS)
    xhat = xc * rstd
    return xhat * g + b, xhat, rstd


def ln_fwd(x, m, g, b):
    s, dm = x.shape
    rows = _tile(s, LN_ROWS)

    def body(x_ref, m_ref, g_ref, b_ref, o_ref):
        o_ref[...] = _ln_core(x_ref[...], m_ref[...], g_ref[...], b_ref[...])[0]

    row = pl.BlockSpec((rows, dm), lambda i: (i, 0))
    vec = pl.BlockSpec((1, dm), lambda i: (0, 0))
    return pl.pallas_call(body, name="ln_fwd", grid=(s // rows,), in_specs=[row, row, vec, vec], out_specs=row,
                          out_shape=jax.ShapeDtypeStruct((s, dm), F32), compiler_params=_cparams(("parallel",)))(x, m, g, b)


def ln_bwd(x, m, g, dy):
    s, dm = x.shape
    rows = _tile(s, LN_ROWS)

    def body(x_ref, m_ref, g_ref, dy_ref, dr_ref, dg_ref, db_ref):
        gv = g_ref[...]
        _, xhat, rstd = _ln_core(x_ref[...], m_ref[...], gv, jnp.zeros_like(gv))
        dy = dy_ref[...]
        dxh = dy * gv
        dr_ref[...] = rstd * (dxh - jnp.mean(dxh, axis=-1, keepdims=True)
                              - xhat * jnp.mean(dxh * xhat, axis=-1, keepdims=True))

        @pl.when(pl.program_id(0) == 0)
        def _():
            dg_ref[...] = jnp.zeros_like(dg_ref)
            db_ref[...] = jnp.zeros_like(db_ref)

        dg_ref[...] += jnp.sum(dy * xhat, axis=0, keepdims=True)
        db_ref[...] += jnp.sum(dy, axis=0, keepdims=True)

    row = pl.BlockSpec((rows, dm), lambda i: (i, 0))
    vec = pl.BlockSpec((1, dm), lambda i: (0, 0))
    return pl.pallas_call(body, name="ln_bwd", grid=(s // rows,), in_specs=[row, row, vec, row], out_specs=[row, vec, vec],
                          out_shape=[jax.ShapeDtypeStruct((s, dm), F32), jax.ShapeDtypeStruct((1, dm), F32),
                                     jax.ShapeDtypeStruct((1, dm), F32)],
                          compiler_params=_cparams(("arbitrary",)))(x, m, g, dy)


def loss_head(y, target):
    s, dm = y.shape
    rows = _tile(s, LN_ROWS)

    def body(y_ref, t_ref, dy_ref, l_ref):
        e = y_ref[...] - t_ref[...]
        dy_ref[...] = e * (1.0 / dm)

        @pl.when(pl.program_id(0) == 0)
        def _():
            l_ref[...] = jnp.zeros_like(l_ref)

        col = jnp.sum(e * e, axis=0, keepdims=True) * (0.5 / dm)
        acc = col[:, :LANES]
        for c in range(1, dm // LANES):
            acc = acc + col[:, c * LANES:(c + 1) * LANES]
        l_ref[...] += acc

    row = pl.BlockSpec((rows, dm), lambda i: (i, 0))
    return pl.pallas_call(body, name="loss_head", grid=(s // rows,), in_specs=[row, row],
                          out_specs=[row, pl.BlockSpec((1, LANES), lambda i: (0, 0))],
                          out_shape=[jax.ShapeDtypeStruct((s, dm), F32), jax.ShapeDtypeStruct((1, LANES), F32)],
                          compiler_params=_cparams(("arbitrary",)))(y, target)


def _shift_rows_impl(x, d):
    n = x.shape[0]
    if d == 0:
        return x
    t = lax.broadcasted_iota(jnp.int32, x.shape, 0)
    return jnp.where((t + d >= 0) & (t + d < n), pltpu.roll(x, (-d) % n, 0), 0.0)


@functools.partial(jax.custom_vjp, nondiff_argnums=(1,))
def _shift_rows(x, d):
    return _shift_rows_impl(x, d)


_shift_rows.defvjp(lambda x, d: (_shift_rows_impl(x, d), None), lambda d, _, g: (_shift_rows_impl(g, -d),))


def _gdn_pre_fn(u, w, kind):
    rows = lax.broadcasted_iota(jnp.int32, w.shape, 0)
    c = None
    for tap in range(A_CONV):
        w_tap = jnp.sum(jnp.where(rows == tap, w, 0.0), axis=0, keepdims=True)
        term = _shift_rows(u, tap - A_CONV // 2) * w_tap
        c = term if c is None else c + term
    y = c * _sigmoid(c)
    if kind == "v":
        return y
    y = y * lax.rsqrt(jnp.sum(y * y, axis=-1, keepdims=True) + L2_EPS)
    return y * (A_DK ** -0.5) if kind == "q" else y


_KIND_OFF = {"q": 0, "k": A_HEADS, "v": 2 * A_HEADS}


def gdn_pre(proj, conv_w, kind):
    s = proj.shape[0]
    off = _KIND_OFF[kind]

    def body(u_ref, w_ref, o_ref):
        o_ref[...] = _gdn_pre_fn(u_ref[...], w_ref[...], kind)

    return pl.pallas_call(
        body, name="gdn_pre_" + kind, grid=(A_HEADS,),
        in_specs=[pl.BlockSpec((s, LANES), lambda h: (0, off + h)), pl.BlockSpec((A_CONV, LANES), lambda h: (0, off + h))],
        out_specs=pl.BlockSpec((s, LANES), lambda h: (0, h)),
        out_shape=jax.ShapeDtypeStruct((s, A_W), F32), compiler_params=_cparams(("parallel",)))(proj, conv_w)


def gdn_pre_bwd(proj, conv_w, dt2, kind):
    s = proj.shape[0]
    off = _KIND_OFF[kind]

    def body(u_ref, w_ref, d0_ref, d1_ref, du_ref, dw_ref):
        _, vjp = jax.vjp(functools.partial(_gdn_pre_fn, kind=kind), u_ref[...], w_ref[...])
        du, dw = vjp(d0_ref[...] + d1_ref[...])
        du_ref[...] = du
        dw_ref[...] = dw

    return pl.pallas_call(
        body, name="gdn_pre_bwd_" + kind, grid=(A_HEADS,),
        in_specs=[pl.BlockSpec((s, LANES), lambda h: (0, off + h)), pl.BlockSpec((A_CONV, LANES), lambda h: (0, off + h)),
                  pl.BlockSpec((None, s, LANES), lambda h: (0, 0, h)), pl.BlockSpec((None, s, LANES), lambda h: (1, 0, h))],
        out_specs=[pl.BlockSpec((s, LANES), lambda h: (0, h)), pl.BlockSpec((A_CONV, LANES), lambda h: (0, h))],
        out_shape=[jax.ShapeDtypeStruct((s, A_W), F32), jax.ShapeDtypeStruct((A_CONV, A_W), F32)],
        compiler_params=_cparams(("parallel",)))(proj, conv_w, dt2, dt2)


N_GATE = 2 * A_HEADS


def _gdn_gates_fn(ba, alog_row, dt_row):
    r = lax.broadcasted_iota(jnp.int32, (LANES, N_GATE * LANES), 0)
    c = lax.broadcasted_iota(jnp.int32, (LANES, N_GATE * LANES), 1) >> 7
    beta_b = xdot(_sigmoid(ba), (r == c).astype(F32))
    g = -(jnp.exp(alog_row) * _softplus(ba + dt_row))
    g_b = xdot(g, (r == c + N_GATE).astype(F32))
    return beta_b, g_b


def gdn_gates(ba, alog_row, dt_row):
    s = ba.shape[0]

    def body(ba_ref, al_ref, dt_ref, bb_ref, gb_ref):
        bb_ref[...], gb_ref[...] = _gdn_gates_fn(ba_ref[...], al_ref[...], dt_ref[...])

    row = pl.BlockSpec((ROWS, LANES), lambda i: (i, 0))
    vec = pl.BlockSpec((1, LANES), lambda i: (0, 0))
    wide = pl.BlockSpec((ROWS, N_GATE * LANES), lambda i: (i, 0))
    return pl.pallas_call(body, name="gdn_gates", grid=(s // ROWS,), in_specs=[row, vec, vec], out_specs=[wide, wide],
                          out_shape=[jax.ShapeDtypeStruct((s, N_GATE * LANES), F32)] * 2,
                          compiler_params=_cparams(("parallel",)))(ba, alog_row, dt_row)


def gdn_gates_bwd(ba, alog_row, dt_row, dbeta_b, dg_b):
    s = ba.shape[0]

    def body(ba_ref, al_ref, dt_ref, dbb_ref, dgb_ref, dba_ref, dal_ref, ddt_ref):
        _, vjp = jax.vjp(_gdn_gates_fn, ba_ref[...], al_ref[...], dt_ref[...])
        dba, dal, ddt = vjp((dbb_ref[...], dgb_ref[...]))
        dba_ref[...] = dba

        @pl.when(pl.program_id(0) == 0)
        def _():
            dal_ref[...] = jnp.zeros_like(dal_ref)
            ddt_ref[...] = jnp.zeros_like(ddt_ref)

        dal_ref[...] += dal
        ddt_ref[...] += ddt

    row = pl.BlockSpec((ROWS, LANES), lambda i: (i, 0))
    vec = pl.BlockSpec((1, LANES), lambda i: (0, 0))
    wide = pl.BlockSpec((ROWS, N_GATE * LANES), lambda i: (i, 0))
    return pl.pallas_call(body, name="gdn_gates_bwd", grid=(s // ROWS,), in_specs=[row, vec, vec, wide, wide],
                          out_specs=[row, vec, vec],
                          out_shape=[jax.ShapeDtypeStruct((s, LANES), F32), jax.ShapeDtypeStruct((1, LANES), F32),
                                     jax.ShapeDtypeStruct((1, LANES), F32)],
                          compiler_params=_cparams(("arbitrary",)))(ba, alog_row, dt_row, dbeta_b, dg_b)


def _post_fn(o, z, g):
    y = o * lax.rsqrt(jnp.mean(o * o, axis=-1, keepdims=True) + RMS_EPS) * g
    return y * (z * _sigmoid(z))


def mixer_post(o2, proj, norm_g, width, gate_off, name):
    s = o2.shape[1]
    nh = o2.shape[2] // width

    rows = _tile(s, POST_ROWS)

    def body(o0_ref, o1_ref, z_ref, g_ref, y_ref):
        y_ref[...] = _post_fn(o0_ref[...] + o1_ref[...], z_ref[...], g_ref[...])

    ospec = lambda d: pl.BlockSpec((None, rows, width), lambda i, h: (d, i, h))
    return pl.pallas_call(
        body, name=name, grid=(s // rows, nh),
        in_specs=[ospec(0), ospec(1), pl.BlockSpec((rows, width), lambda i, h: (i, gate_off + h)),
                  pl.BlockSpec((1, width), lambda i, h: (0, 0))],
        out_specs=pl.BlockSpec((rows, width), lambda i, h: (i, h)),
        out_shape=jax.ShapeDtypeStruct((s, o2.shape[2]), F32),
        compiler_params=_cparams(("parallel", "parallel")))(o2, o2, proj, norm_g)


def mixer_post_bwd(o2, proj, norm_g, dy, width, gate_off, name):
    s = o2.shape[1]
    nh = o2.shape[2] // width

    def body(o0_ref, o1_ref, z_ref, g_ref, dy_ref, do_ref, dz_ref, dg_ref):
        _, vjp = jax.vjp(_post_fn, o0_ref[...] + o1_ref[...], z_ref[...], g_ref[...])
        do, dz, dg = vjp(dy_ref[...])
        do_ref[...] = do
        dz_ref[...] = dz

        @pl.when((pl.program_id(0) == 0) & (pl.program_id(1) == 0))
        def _():
            dg_ref[...] = jnp.zeros_like(dg_ref)

        dg_ref[...] += dg

    rows = _tile(s, POST_ROWS)
    ospec = lambda d: pl.BlockSpec((None, rows, width), lambda i, h: (d, i, h))
    blk = pl.BlockSpec((rows, width), lambda i, h: (i, h))
    vec = pl.BlockSpec((1, width), lambda i, h: (0, 0))
    return pl.pallas_call(
        body, name=name, grid=(s // rows, nh),
        in_specs=[ospec(0), ospec(1), pl.BlockSpec((rows, width), lambda i, h: (i, gate_off + h)), vec, blk],
        out_specs=[blk, blk, vec],
        out_shape=[jax.ShapeDtypeStruct((s, o2.shape[2]), F32)] * 2 + [jax.ShapeDtypeStruct((1, width), F32)],
        compiler_params=_cparams(("arbitrary", "arbitrary")))(o2, o2, proj, norm_g, dy)


def _log_gate(z):
    return (jnp.minimum(z, 0.0) - jnp.log(1.0 + jnp.exp(-jnp.abs(z)))) * (1.0 / B_TAU)


def gla_gate(gl, w2, gb):
    s = gl.shape[0]

    def body(gl_ref, w_ref, b_ref, o_ref):
        for n in range(2):
            o_ref[n] = _log_gate(nn(gl_ref[...], w_ref[n]) + b_ref[n])

    full = lambda shp: pl.BlockSpec(shp, lambda i: (0,) * len(shp))
    return pl.pallas_call(
        body, name="gla_gate", grid=(s // ROWS,),
        in_specs=[pl.BlockSpec((ROWS, LANES), lambda i: (i, 0)), full(w2.shape), full(gb.shape)],
        out_specs=pl.BlockSpec((2, ROWS, B_KW), lambda i: (0, i, 0)),
        out_shape=jax.ShapeDtypeStruct((2, s, B_KW), F32), compiler_params=_cparams(("parallel",)))(gl, w2, gb)


def gla_gate_bwd(gl, w2, gb, dla):
    s = gl.shape[0]

    def body(gl_ref, w_ref, b_ref, dla_ref, dgl_ref, dz_ref, db0_ref, db1_ref):
        @pl.when(pl.program_id(0) == 0)
        def _():
            db0_ref[...] = jnp.zeros_like(db0_ref)
            db1_ref[...] = jnp.zeros_like(db1_ref)

        dgl = None
        for n, db_ref in enumerate((db0_ref, db1_ref)):
            _, vjp = jax.vjp(_log_gate, nn(gl_ref[...], w_ref[n]) + b_ref[n])
            dz, = vjp(dla_ref[n])
            dz_ref[n] = dz
            db_ref[...] += jnp.sum(dz, axis=0, keepdims=True)
            part = nt(dz, w_ref[n])
            dgl = part if dgl is None else dgl + part
        dgl_ref[...] = dgl

    full = lambda shp: pl.BlockSpec(shp, lambda i: (0,) * len(shp))
    row = pl.BlockSpec((ROWS, LANES), lambda i: (i, 0))
    wide = pl.BlockSpec((2, ROWS, B_KW), lambda i: (0, i, 0))
    vec = pl.BlockSpec((1, B_KW), lambda i: (0, 0))
    return pl.pallas_call(
        body, name="gla_gate_bwd", grid=(s // ROWS,),
        in_specs=[row, full(w2.shape), full(gb.shape), wide],
        out_specs=[row, wide, vec, vec],
        out_shape=[jax.ShapeDtypeStruct((s, LANES), F32), jax.ShapeDtypeStruct((2, s, B_KW), F32),
                   jax.ShapeDtypeStruct((1, B_KW), F32), jax.ShapeDtypeStruct((1, B_KW), F32)],
        compiler_params=_cparams(("arbitrary",)))(gl, w2, gb, dla)


PACK_TILE = 512


def cast_into_slot(x, chip):
    r, c = x.shape

    def body(chip_ref, x_ref, o_ref):
        o_ref[...] = x_ref[...].astype(BF16)

    return pl.pallas_call(
        body, name="cast_into_slot",
        grid_spec=pltpu.PrefetchScalarGridSpec(
            num_scalar_prefetch=1, grid=(r // PACK_TILE,),
            in_specs=[pl.BlockSpec((PACK_TILE, c), lambda i, chip_ref: (i, 0))],
            out_specs=pl.BlockSpec((None, PACK_TILE, c), lambda i, chip_ref: (chip_ref[0], i, 0))),
        out_shape=jax.ShapeDtypeStruct((4, r, c), BF16), compiler_params=_cparams(("parallel",)))(chip, x)


def sum_received(chip_sum, recv, chip, core):
    _, h, c = chip_sum.shape
    n = recv.shape[0]
    tr = _tile(h, PACK_TILE)
    nblk = h // tr

    def body(chip_ref, core_ref, own_ref, r_ref, o_ref):
        acc = r_ref[0].astype(F32)
        for k in range(1, n):
            acc = acc + r_ref[k].astype(F32)
        o_ref[...] = acc + own_ref[...].astype(F32)

    return pl.pallas_call(
        body, name="sum_received",
        grid_spec=pltpu.PrefetchScalarGridSpec(
            num_scalar_prefetch=2, grid=(nblk,),
            in_specs=[pl.BlockSpec((None, tr, c), lambda i, chip_ref, core_ref: (chip_ref[0], i, 0)),
                      pl.BlockSpec((n, tr, c), lambda i, chip_ref, core_ref: (0, i, 0))],
            out_specs=pl.BlockSpec((tr, c), lambda i, chip_ref, core_ref: (core_ref[0] * nblk + i, 0))),
        out_shape=jax.ShapeDtypeStruct((2 * h, c), F32), compiler_params=_cparams(("parallel",)))(chip, core, chip_sum, recv)


def merge_first_hop(chip_sum, passed, slot_x, slot_y, core):
    _, h, c = chip_sum.shape
    tr = _tile(h, PACK_TILE)

    def body(sx_ref, sy_ref, core_ref, to_x_ref, to_y_ref, p_ref, o_ref):
        p = p_ref[...].astype(F32)
        is_y = core_ref[0].astype(F32)
        o_ref[0] = (to_x_ref[...].astype(F32) + p * (1.0 - is_y)).astype(BF16)
        o_ref[1] = (to_y_ref[...].astype(F32) + p * is_y).astype(BF16)

    return pl.pallas_call(
        body, name="merge_first_hop",
        grid_spec=pltpu.PrefetchScalarGridSpec(
            num_scalar_prefetch=3, grid=(h // tr,),
            in_specs=[pl.BlockSpec((None, tr, c), lambda i, sx_ref, sy_ref, core_ref: (sx_ref[0], i, 0)),
                      pl.BlockSpec((None, tr, c), lambda i, sx_ref, sy_ref, core_ref: (sy_ref[0], i, 0)),
                      pl.BlockSpec((tr, c), lambda i, sx_ref, sy_ref, core_ref: (i, 0))],
            out_specs=pl.BlockSpec((2, tr, c), lambda i, sx_ref, sy_ref, core_ref: (0, i, 0))),
        out_shape=jax.ShapeDtypeStruct((2, h, c), BF16),
        compiler_params=_cparams(("parallel",)))(slot_x, slot_y, core, chip_sum, chip_sum, passed)


def sum_slots(x, name):
    n, r, c = x.shape
    tr = _tile(r, PACK_TILE)

    def body(x_ref, o_ref):
        acc = x_ref[0].astype(F32)
        for k in range(1, n):
            acc = acc + x_ref[k].astype(F32)
        o_ref[...] = acc

    return pl.pallas_call(body, name=name, grid=(r // tr,), in_specs=[pl.BlockSpec((n, tr, c), lambda i: (0, i, 0))],
                          out_specs=pl.BlockSpec((tr, c), lambda i: (i, 0)),
                          out_shape=jax.ShapeDtypeStruct((r, c), F32), compiler_params=_cparams(("parallel",)))(x)


def half_to_bf16(gpack, which, theirs=None):
    n, r, c = gpack.shape
    half_rows = r // 2
    tr = _tile(half_rows, PACK_TILE)
    nblk = half_rows // tr

    def body(which_ref, g_ref, *rest):
        o_ref = rest[-1]
        acc = g_ref[...]
        if theirs is not None:
            acc = acc + rest[0][...].astype(F32)
        o_ref[...] = acc.astype(BF16)

    blk = pl.BlockSpec((None, tr, c), lambda s, i, which_ref: (s, i, 0))
    ins = [gpack] if theirs is None else [gpack, theirs]
    return pl.pallas_call(
        body, name="half_to_bf16" if theirs is None else "add_sibling_half",
        grid_spec=pltpu.PrefetchScalarGridSpec(
            num_scalar_prefetch=1, grid=(n, nblk),
            in_specs=[pl.BlockSpec((None, tr, c), lambda s, i, which_ref: (s, which_ref[0] * nblk + i, 0))]
            + [blk] * (len(ins) - 1),
            out_specs=blk),
        out_shape=jax.ShapeDtypeStruct((n, half_rows, c), BF16),
        compiler_params=_cparams(("parallel", "parallel")))(which, *ins)


def adamw(w, m, v, grads, g_row_off, name):
    r, c = w.shape
    tr = next(t for t in (PACK_TILE, r) if r % t == 0 and g_row_off % t == 0)
    ob = g_row_off // tr
    ng = len(grads)

    def body(*refs):
        w_ref, m_ref, v_ref = refs[:3]
        g_refs = refs[3:3 + ng]
        g_ref, d_ref, nm_ref, nv_ref = refs[3 + ng:]
        g = g_refs[0][...]
        for gr in g_refs[1:]:
            g = g + gr[...]
        m_new = ADAM_B1 * m_ref[...] + (1.0 - ADAM_B1) * g
        v_new = ADAM_B2 * v_ref[...] + (1.0 - ADAM_B2) * jnp.square(g)
        m_hat = m_new / (1.0 - ADAM_B1 ** ADAM_STEP)
        v_hat = v_new / (1.0 - ADAM_B2 ** ADAM_STEP)
        g_ref[...] = g
        d_ref[...] = -ADAM_LR * (m_hat / (jnp.sqrt(v_hat) + ADAM_EPS) + ADAM_WD * w_ref[...])
        nm_ref[...] = m_new
        nv_ref[...] = v_new

    blk = pl.BlockSpec((tr, c), lambda i: (i, 0))
    gblk = pl.BlockSpec((tr, c), lambda i: (i + ob, 0))
    return pl.pallas_call(body, name=name, grid=(r // tr,), in_specs=[blk, blk, blk] + [gblk] * ng, out_specs=[blk] * 4,
                          out_shape=[jax.ShapeDtypeStruct((r, c), F32)] * 4,
                          compiler_params=_cparams(("parallel",)))(w, m, v, *grads)


MESH = pl.DeviceIdType.MESH
HBM = pl.BlockSpec(memory_space=pl.ANY)


def _place():
    return lax.axis_index("x"), lax.axis_index("y"), lax.axis_index("c")


def allgather_chips(buf):
    _, r, c = buf.shape
    half_rows = r // 2

    def body(_, out_ref, send_sems, recv_sems):
        x, y, cc = _place()
        half = pl.ds(cc * half_rows, half_rows)
        other = pl.ds((1 - cc) * half_rows, half_rows)

        def copy(k, rows, to):
            return pltpu.make_async_remote_copy(src_ref=rows, dst_ref=rows, send_sem=send_sems.at[k],
                                                recv_sem=recv_sems.at[k], device_id=to, device_id_type=MESH)

        nbr_x, nbr_y, diag = (1 - x, y), (x, 1 - y), (1 - x, 1 - y)
        slot = lambda chip: 2 * chip[0] + chip[1]
        sibling = (x, y, 1 - cc)
        first = [copy(0, out_ref.at[slot((x, y)), half], (*nbr_x, cc)), copy(1, out_ref.at[slot((x, y)), half], (*nbr_y, cc))]
        for cp in first:
            cp.start()
        passed = []
        for k, chip in enumerate((nbr_x, nbr_y)):
            landed = out_ref.at[slot(chip), half]
            copy(k, landed, (*chip, cc)).wait_recv()
            passed.append(copy(3 + k, landed, sibling))
            passed[-1].start()
        via = (1 - x + cc * (2 * x - 1), y + cc * (1 - 2 * y))
        to = (x + cc * (1 - 2 * x), 1 - y + cc * (2 * y - 1))
        hop = copy(2, out_ref.at[slot(via), half], (*to, cc))
        hop.start()
        landed = out_ref.at[slot(diag), half]
        copy(2, landed, (*to, cc)).wait_recv()
        passed.append(copy(5, landed, sibling))
        passed[-1].start()
        for k, chip in enumerate((nbr_x, nbr_y, diag)):
            copy(3 + k, out_ref.at[slot(chip), other], sibling).wait_recv()
        for cp in first + [hop] + passed:
            cp.wait_send()

    return pl.pallas_call(
        body, name="allgather_chips", in_specs=[HBM], out_specs=HBM, input_output_aliases={0: 0},
        out_shape=jax.ShapeDtypeStruct(buf.shape, buf.dtype),
        scratch_shapes=[pltpu.SemaphoreType.DMA((6,)), pltpu.SemaphoreType.DMA((6,))],
    )(buf)


def scatter_first_hop(gpack):
    _, r, c = gpack.shape

    def body(src_ref, out_ref, send_sem, recv_sem):
        x, y, cc = _place()
        to = (x + cc * (1 - 2 * x), 1 - y + cc * (2 * y - 1), cc)
        cp = pltpu.make_async_remote_copy(src_ref=src_ref.at[2 * (1 - x) + (1 - y)], dst_ref=out_ref, send_sem=send_sem,
                                          recv_sem=recv_sem, device_id=to, device_id_type=MESH)
        cp.start()
        cp.wait()

    return pl.pallas_call(
        body, name="scatter_first_hop", in_specs=[HBM], out_specs=HBM,
        out_shape=jax.ShapeDtypeStruct((r, c), gpack.dtype),
        scratch_shapes=[pltpu.SemaphoreType.DMA, pltpu.SemaphoreType.DMA],
    )(gpack)


def scatter_second_hop(to_nbrs):
    def body(src_ref, out_ref, send_sems, recv_sems):
        x, y, cc = _place()
        sends = [pltpu.make_async_remote_copy(src_ref=src_ref.at[k], dst_ref=out_ref.at[k], send_sem=send_sems.at[k],
                                              recv_sem=recv_sems.at[k], device_id=to, device_id_type=MESH)
                 for k, to in enumerate(((1 - x, y, cc), (x, 1 - y, cc)))]
        for cp in sends:
            cp.start()
        for cp in sends:
            cp.wait_recv()
        for cp in sends:
            cp.wait_send()

    return pl.pallas_call(
        body, name="scatter_second_hop", in_specs=[HBM], out_specs=HBM,
        out_shape=jax.ShapeDtypeStruct(to_nbrs.shape, to_nbrs.dtype),
        scratch_shapes=[pltpu.SemaphoreType.DMA((2,)), pltpu.SemaphoreType.DMA((2,))],
    )(to_nbrs)


def swap_sibling(mine):
    def body(src_ref, out_ref, send_sem, recv_sem):
        x, y, cc = _place()
        cp = pltpu.make_async_remote_copy(src_ref=src_ref, dst_ref=out_ref, send_sem=send_sem, recv_sem=recv_sem,
                                          device_id=(x, y, 1 - cc), device_id_type=MESH)
        cp.start()
        cp.wait()

    return pl.pallas_call(
        body, name="swap_sibling", in_specs=[HBM], out_specs=HBM,
        out_shape=jax.ShapeDtypeStruct(mine.shape, mine.dtype),
        scratch_shapes=[pltpu.SemaphoreType.DMA, pltpu.SemaphoreType.DMA],
    )(mine)


def join_halves(buf):
    r, c = buf.shape
    half_rows = r // 2

    def body(_, out_ref, send_sem, recv_sem):
        x, y, cc = _place()
        half = out_ref.at[pl.ds(cc * half_rows, half_rows)]
        other = out_ref.at[pl.ds((1 - cc) * half_rows, half_rows)]
        send = pltpu.make_async_remote_copy(src_ref=half, dst_ref=half, send_sem=send_sem, recv_sem=recv_sem,
                                            device_id=(x, y, 1 - cc), device_id_type=MESH)
        send.start()
        pltpu.make_async_remote_copy(src_ref=other, dst_ref=other, send_sem=send_sem, recv_sem=recv_sem,
                                     device_id=(x, y, 1 - cc), device_id_type=MESH).wait_recv()
        send.wait_send()

    return pl.pallas_call(
        body, name="join_halves", in_specs=[HBM], out_specs=HBM, input_output_aliases={0: 0},
        out_shape=jax.ShapeDtypeStruct(buf.shape, buf.dtype),
        scratch_shapes=[pltpu.SemaphoreType.DMA, pltpu.SemaphoreType.DMA],
    )(buf)


def exchange_all(v, name):
    r, c = v.shape

    def body(v_ref, out_ref, send_sems, recv_sems):
        x, y, cc = _place()
        out_ref[4 * x + 2 * y + cc] = v_ref[...]
        sends, recvs = [], []
        for k in range(1, 8):
            px = 1 - x if k & 4 else x
            py = 1 - y if k & 2 else y
            pc = 1 - cc if k & 1 else cc
            sends.append(pltpu.make_async_remote_copy(
                src_ref=v_ref, dst_ref=out_ref.at[4 * x + 2 * y + cc], send_sem=send_sems.at[k - 1],
                recv_sem=recv_sems.at[k - 1], device_id=(px, py, pc), device_id_type=MESH))
            recvs.append(pltpu.make_async_remote_copy(
                src_ref=v_ref, dst_ref=out_ref.at[4 * px + 2 * py + pc], send_sem=send_sems.at[k - 1],
                recv_sem=recv_sems.at[k - 1], device_id=(px, py, pc), device_id_type=MESH))
        for cp in sends:
            cp.start()
        for cp in recvs:
            cp.wait_recv()
        for cp in sends:
            cp.wait_send()

    vm = pl.BlockSpec(memory_space=pltpu.VMEM)
    return pl.pallas_call(
        body, name=name, in_specs=[vm], out_specs=vm, out_shape=jax.ShapeDtypeStruct((8, r, c), v.dtype),
        scratch_shapes=[pltpu.SemaphoreType.DMA((7,)), pltpu.SemaphoreType.DMA((7,))],
        compiler_params=pltpu.CompilerParams(vmem_limit_bytes=VMEM_LIMIT),
    )(v)


def _as_rows(a, width):
    n = math.prod(a.shape)
    if n % width == 0:
        return a.reshape(-1, width)
    return jnp.pad(a.reshape(1, -1), ((0, 0), (0, -n % width))).reshape(-1, width)


def _n_rows(shape, width):
    return -(-math.prod(shape) // width)


def _pack_rows(arrays, rows, width):
    parts = [_as_rows(a, width) for a in arrays]
    used = sum(p.shape[0] for p in parts)
    return jnp.concatenate(parts + [jnp.zeros((rows - used, width), arrays[0].dtype)], axis=0)


def _unpack_rows(pack, shapes):
    width = pack.shape[1]
    out, off = [], 0
    for shp in shapes:
        nr, n = _n_rows(shp, width), math.prod(shp)
        part = pack[off:off + nr]
        out.append(part.reshape(shp) if n % width == 0 else part.reshape(-1)[:n].reshape(shp))
        off += nr
    return out


def _rows_for(shapes, width, mult=8):
    n = sum(_n_rows(s, width) for s in shapes)
    return -(-n // mult) * mult


def _gdn_fwd(x, p):
    proj = mm(x, p["w_main"], name="gdn_proj")
    ba = mm(x, p["w_gate"], name="gdn_proj_gate")
    q, k, v = (gdn_pre(proj, p["conv"], kind) for kind in "qkv")
    beta_b, g_b = gdn_gates(ba, p["alog_row"], p["dt_row"])
    o2, st, tinv = gdn_rec_fwd(q, k, v, beta_b, g_b)
    y = mixer_post(o2, proj, p["norm_g"], A_DK, 3 * A_HEADS, "gdn_post")
    m = mm(y, p["w_out"], name="gdn_out")
    return m, (x, proj, ba, q, k, v, beta_b, g_b, o2, st, tinv, y)


def _gdn_bwd(saved, p, dm):
    x, proj, ba, q, k, v, beta_b, g_b, o2, st, tinv, y = saved
    d_w_out = mm(y, dm, "tn", name="gdn_dw_out")
    dy = mm(dm, p["w_out"], "nt", name="gdn_dy")
    do, dz, d_norm_g = mixer_post_bwd(o2, proj, p["norm_g"], dy, A_DK, 3 * A_HEADS, "gdn_post_bwd")
    dq2, dk2, dv2, dbb, dgb = gdn_rec_bwd(q, k, v, beta_b, g_b, st, tinv, do)
    dba, d_alog_row, d_dt_row = gdn_gates_bwd(ba, p["alog_row"], p["dt_row"], dbb, dgb)
    du, dconv = zip(*(gdn_pre_bwd(proj, p["conv"], d2, kind) for d2, kind in ((dq2, "q"), (dk2, "k"), (dv2, "v"))))
    dproj = jnp.concatenate(list(du) + [dz], axis=1)
    d_w_main = mm(x, dproj, "tn", name="gdn_dw_main")
    d_w_gate = mm(x, dba, "tn", name="gdn_dw_gate")
    dx = mm(dba, p["w_gate"], "nt", epi="add", extra=dm, alpha=ALPHA, name="gdn_dx_gate")
    dx = mm(dproj, p["w_main"], "nt", epi="add", extra=dx, name="gdn_dx")
    grads = dict(w_in=jnp.concatenate([d_w_main, d_w_gate[:, :2 * N_GATE]], axis=1), conv=jnp.concatenate(dconv, axis=1),
                 alog=d_alog_row[0, N_GATE:2 * N_GATE].reshape(2, A_HEADS), dt=d_dt_row[0, N_GATE:2 * N_GATE].reshape(2, A_HEADS),
                 norm_g=d_norm_g[0], w_out=d_w_out)
    return dx, grads


def _gla_fwd(x, p):
    proj = mm(x, p["w_main"], name="gla_proj")
    gl = mm(x, p["w_gate"], name="gla_proj_gate")
    log_a = gla_gate(gl, p["w2"], p["gate_b"])
    o2, st = gla_rec_fwd(proj, log_a)
    y = mixer_post(o2, proj, p["norm_g"], B_DV, (2 * B_KW + B_VW) // B_DV, "gla_post")
    m = mm(y, p["w_out"], name="gla_out")
    return m, (x, proj, gl, log_a, o2, st, y)


def _gla_bwd(saved, p, dm):
    x, proj, gl, log_a, o2, st, y = saved
    d_w_out = mm(y, dm, "tn", name="gla_dw_out")
    dy = mm(dm, p["w_out"], "nt", name="gla_dy")
    do, dr, d_norm_g = mixer_post_bwd(o2, proj, p["norm_g"], dy, B_DV, (2 * B_KW + B_VW) // B_DV, "gla_post_bwd")
    dq2, dk2, dv2, dla = gla_rec_bwd(proj, log_a, st, do)
    dgl, dz, d_b0, d_b1 = gla_gate_bwd(gl, p["w2"], p["gate_b"], dla)
    d_w2 = [mm(gl, dz[n], "tn", name="gla_dw_gate_w2") for n in range(2)]
    dproj = jnp.concatenate([dq2[0] + dq2[1], dk2[0] + dk2[1], dv2[0] + dv2[1], dr], axis=1)
    d_w_main = mm(x, dproj, "tn", name="gla_dw_main")
    d_w_gate = mm(x, dgl, "tn", name="gla_dw_gate")
    dx = mm(dgl, p["w_gate"], "nt", epi="add", extra=dm, alpha=ALPHA, name="gla_dx_gate")
    dx = mm(dproj, p["w_main"], "nt", epi="add", extra=dx, name="gla_dx")
    grads = dict(w_in=jnp.concatenate([d_w_main, d_w_gate[:, :2 * B_RANK]], axis=1),
                 gate_w2=jnp.stack([d_w2[n][n * B_RANK:(n + 1) * B_RANK] for n in range(2)]),
                 gate_b=jnp.concatenate([d_b0, d_b1]), norm_g=d_norm_g[0], w_out=d_w_out)
    return dx, grads


def _pad_cols(w, width=LANES):
    return jnp.pad(w, ((0, 0), (0, width - w.shape[1])))


def _local_step(x, target, a_w_in, a_conv, a_alog, a_dt_bias, a_norm_g, a_w_out, b_w_in, b_gate_w2, b_gate_b, b_norm_g,
                b_w_out, ln1_g, ln1_b, mlp_w1, mlp_w2, ln2_g, ln2_b, grad_pack=None):
    layer_p = []
    for i in range(DEPTH):
        j = i // 2
        if i % 2 == 0:
            layer_p.append(dict(
                w_main=a_w_in[j][:, :4 * A_W], w_gate=_pad_cols(a_w_in[j][:, 4 * A_W:]), conv=a_conv[j],
                alog_row=jnp.pad(a_alog[j].reshape(1, N_GATE), ((0, 0), (N_GATE, LANES - 2 * N_GATE))),
                dt_row=jnp.pad(a_dt_bias[j].reshape(1, N_GATE), ((0, 0), (N_GATE, LANES - 2 * N_GATE))),
                norm_g=a_norm_g[j].reshape(1, A_DK), w_out=a_w_out[j]))
        else:
            w2 = jnp.stack([jnp.pad(b_gate_w2[j][n], ((n * B_RANK, LANES - (n + 1) * B_RANK), (0, 0))) for n in range(2)])
            layer_p.append(dict(
                w_main=b_w_in[j][:, :2 * B_KW + 2 * B_VW], w_gate=_pad_cols(b_w_in[j][:, 2 * B_KW + 2 * B_VW:]),
                w2=w2, gate_b=b_gate_b[j].reshape(2, 1, B_KW), norm_g=b_norm_g[j].reshape(1, B_DV), w_out=b_w_out[j]))

    saved = []
    h = x
    for i in range(DEPTH):
        p = layer_p[i]
        m, sv = (_gdn_fwd if i % 2 == 0 else _gla_fwd)(h, p)
        x1 = ln_fwd(h, m, ln1_g[i:i + 1], ln1_b[i:i + 1])
        h1 = mm(x1, mlp_w1[i], name="mlp_up")
        mlp = mm(h1, mlp_w2[i], act="sqrelu", name="mlp_down")
        x2 = ln_fwd(x1, mlp, ln2_g[i:i + 1], ln2_b[i:i + 1])
        saved.append((sv, h, m, x1, h1, mlp))
        h = x2

    dh, loss_part = loss_head(h, target)

    g_a, g_b, g_ln1g, g_ln1b, g_ln2g, g_ln2b, g_w1, g_w2 = {}, {}, {}, {}, {}, {}, {}, {}
    pack = None
    for i in reversed(range(DEPTH)):
        sv, xin, m, x1, h1, mlp = saved[i]
        p = layer_p[i]
        dr2, g_ln2g[i], g_ln2b[i] = ln_bwd(x1, mlp, ln2_g[i:i + 1], dh)
        if grad_pack is None:
            g_w2[i] = mm(h1, dr2, "tn", act="sqrelu", name="mlp_dw_down")
        else:
            pack = mm(h1, dr2, "tn", act="sqrelu", pack=(pack, grad_pack[0], grad_pack[2] + i, "rows"), name="mlp_dw_down")
        dh1 = mm(dr2, mlp_w2[i], "nt", epi="dsqrelu", extra=h1, name="mlp_dh")
        if grad_pack is None:
            g_w1[i] = mm(x1, dh1, "tn", chip_major=True, name="mlp_dw_up")
        else:
            pack = mm(x1, dh1, "tn", pack=(pack, grad_pack[0], grad_pack[1] + i, "cols"), name="mlp_dw_up")
        dx1 = mm(dh1, mlp_w1[i], "nt", epi="add", extra=dr2, alpha=ALPHA, name="mlp_dx")
        dr1, g_ln1g[i], g_ln1b[i] = ln_bwd(xin, m, ln1_g[i:i + 1], dx1)
        dh, g = (_gdn_bwd if i % 2 == 0 else _gla_bwd)(sv, p, dr1)
        (g_a if i % 2 == 0 else g_b)[i // 2] = g

    per_layer = lambda d, key=None: [(d[i] if key is None else d[i][key]) for i in sorted(d)]
    st = lambda d, key=None: jnp.stack(per_layer(d, key))
    grads = dict(
        a_w_in=per_layer(g_a, "w_in"), a_conv=st(g_a, "conv"), a_alog=st(g_a, "alog"), a_dt_bias=st(g_a, "dt"),
        a_norm_g=st(g_a, "norm_g"), a_w_out=per_layer(g_a, "w_out"), b_w_in=per_layer(g_b, "w_in"),
        b_gate_w2=st(g_b, "gate_w2"), b_gate_b=st(g_b, "gate_b"), b_norm_g=st(g_b, "norm_g"),
        b_w_out=per_layer(g_b, "w_out"), ln1_g=st(g_ln1g)[:, 0], ln1_b=st(g_ln1b)[:, 0], mlp_w1=per_layer(g_w1),
        mlp_w2=per_layer(g_w2), ln2_g=st(g_ln2g)[:, 0], ln2_b=st(g_ln2b)[:, 0], pack=pack)
    return loss_part, dh, grads


WEIGHTS = ("a_w_in", "a_conv", "a_alog", "a_dt_bias", "a_norm_g", "a_w_out", "b_w_in", "b_gate_w2", "b_gate_b",
           "b_norm_g", "b_w_out", "ln1_g", "ln1_b", "mlp_w1", "mlp_w2", "ln2_g", "ln2_b")
BIG = ("mlp_w1", "mlp_w2", "a_w_out", "b_w_out", "a_w_in", "b_w_in")
SHARD_AXIS = {"mlp_w1": 2, "mlp_w2": 1, "a_w_out": 1, "b_w_out": 1, "a_w_in": 2, "b_w_in": 2}
SMALL = tuple(n for n in WEIGHTS if n not in BIG)
SMALL_SHARD_AXIS = {"a_conv": 2, "b_gate_w2": 3, "b_gate_b": 2, "b_norm_g": 1}


def _to_chip_major(full, axis):
    shp = full.shape
    t = full.reshape(shp[:axis] + (4, shp[axis] // 4) + shp[axis + 1:])
    return jnp.moveaxis(t, axis, 0)


def _from_chip_major(stacked, axis):
    t = jnp.moveaxis(stacked, 0, axis)
    shp = t.shape
    return t.reshape(shp[:axis] + (shp[axis] * shp[axis + 1],) + shp[axis + 2:])


def kernel(x, a_w_in, a_conv, a_alog, a_dt_bias, a_norm_g, a_w_out, b_w_in, b_gate_w2, b_gate_b, b_norm_g, b_w_out, ln1_g, ln1_b, mlp_w1, mlp_w2, ln2_g, ln2_b, loss_target, m_a_w_in, m_a_conv, m_a_alog, m_a_dt_bias, m_a_norm_g, m_a_w_out, m_b_w_in, m_b_gate_w2, m_b_gate_b, m_b_norm_g, m_b_w_out, m_ln1_g, m_ln1_b, m_mlp_w1, m_mlp_w2, m_ln2_g, m_ln2_b, v_a_w_in, v_a_conv, v_a_alog, v_a_dt_bias, v_a_norm_g, v_a_w_out, v_b_w_in, v_b_gate_w2, v_b_gate_b, v_b_norm_g, v_b_w_out, v_ln1_g, v_ln1_b, v_mlp_w1, v_mlp_w2, v_ln2_g, v_ln2_b):
    w = dict(a_w_in=a_w_in, a_conv=a_conv, a_alog=a_alog, a_dt_bias=a_dt_bias, a_norm_g=a_norm_g, a_w_out=a_w_out,
             b_w_in=b_w_in, b_gate_w2=b_gate_w2, b_gate_b=b_gate_b, b_norm_g=b_norm_g, b_w_out=b_w_out, ln1_g=ln1_g,
             ln1_b=ln1_b, mlp_w1=mlp_w1, mlp_w2=mlp_w2, ln2_g=ln2_g, ln2_b=ln2_b)
    mom = dict(a_w_in=m_a_w_in, a_conv=m_a_conv, a_alog=m_a_alog, a_dt_bias=m_a_dt_bias, a_norm_g=m_a_norm_g,
               a_w_out=m_a_w_out, b_w_in=m_b_w_in, b_gate_w2=m_b_gate_w2, b_gate_b=m_b_gate_b, b_norm_g=m_b_norm_g,
               b_w_out=m_b_w_out, ln1_g=m_ln1_g, ln1_b=m_ln1_b, mlp_w1=m_mlp_w1, mlp_w2=m_mlp_w2, ln2_g=m_ln2_g,
               ln2_b=m_ln2_b)
    var = dict(a_w_in=v_a_w_in, a_conv=v_a_conv, a_alog=v_a_alog, a_dt_bias=v_a_dt_bias, a_norm_g=v_a_norm_g,
               a_w_out=v_a_w_out, b_w_in=v_b_w_in, b_gate_w2=v_b_gate_w2, b_gate_b=v_b_gate_b, b_norm_g=v_b_norm_g,
               b_w_out=v_b_w_out, ln1_g=v_ln1_g, ln1_b=v_ln1_b, mlp_w1=v_mlp_w1, mlp_w2=v_mlp_w2, ln2_g=v_ln2_g,
               ln2_b=v_ln2_b)
    chip = 2 * lax.axis_index("x") + lax.axis_index("y")

    seg_rows = [w[n].size // D_MODEL for n in BIG]
    seg_off = [sum(seg_rows[:i]) for i in range(len(BIG))]
    rows = -(-sum(seg_rows) // PACK_TILE) * PACK_TILE
    shard_pack = jnp.concatenate([w[n].reshape(-1, D_MODEL) for n in BIG]
                                 + [jnp.zeros((rows - sum(seg_rows), D_MODEL), F32)], axis=0)
    chip_idx = chip.astype(jnp.int32).reshape(1)
    gathered = allgather_chips(cast_into_slot(shard_pack, chip_idx))
    full = {}
    for n, off, nr in zip(BIG, seg_off, seg_rows):
        if n in ("mlp_w1", "mlp_w2"):
            kind = "cols" if SHARD_AXIS[n] == 2 else "rows"
            full[n] = [Gathered(gathered, off + i * D_MODEL, kind) for i in range(DEPTH)]
            continue
        stacked = gathered[:, off:off + nr].reshape((4,) + w[n].shape)
        full[n] = _from_chip_major(stacked, SHARD_AXIS[n])
    sharded_small = tuple(SMALL_SHARD_AXIS)
    sm_shapes = [w[n].shape for n in sharded_small]
    sm_rows = _rows_for(sm_shapes, LANES)
    sm_all = exchange_all(_pack_rows([w[n] for n in sharded_small], sm_rows, LANES), "gather_small")
    per_chip = [_unpack_rows(sm_all[2 * pch], sm_shapes) for pch in range(4)]
    for idx, n in enumerate(sharded_small):
        full[n] = jnp.concatenate([per_chip[pch][idx] for pch in range(4)], axis=SMALL_SHARD_AXIS[n])
    for n in WEIGHTS:
        full.setdefault(n, w[n])

    blk_of = {n: off // D_MODEL for n, off in zip(BIG, seg_off)}
    loss_part, grad_x, grads = _local_step(x[0], loss_target[0], *[full[n] for n in WEIGHTS],
                                           grad_pack=(rows, blk_of["mlp_w1"], blk_of["mlp_w2"]))
    loss = lax.psum(jnp.sum(loss_part), ("x", "y", "c"))

    gpack = grads["pack"]
    rest = jnp.concatenate(
        [_to_chip_major(g, SHARD_AXIS[n] - 1).reshape(4, -1, D_MODEL) for n in BIG[2:] for g in grads[n]]
        + [jnp.zeros((4, rows - sum(seg_rows), D_MODEL), F32)], axis=1)
    gpack = lax.dynamic_update_slice(gpack, rest, (0, seg_off[2], 0))
    core = lax.axis_index("c").astype(jnp.int32).reshape(1)
    theirs = swap_sibling(half_to_bf16(gpack, 1 - core))
    chip_sum = half_to_bf16(gpack, core, theirs)
    ax, ay = lax.axis_index("x"), lax.axis_index("y")
    slot_x = (2 * (1 - ax) + ay).astype(jnp.int32).reshape(1)
    slot_y = (2 * ax + (1 - ay)).astype(jnp.int32).reshape(1)
    to_nbrs = merge_first_hop(chip_sum, scatter_first_hop(chip_sum), slot_x, slot_y, core)
    reduced = join_halves(sum_received(chip_sum, scatter_second_hop(to_nbrs), chip_idx, core))
    out_g, out_d, out_m, out_v = {}, {}, {}, {}
    for n, off, nr in zip(BIG, seg_off, seg_rows):
        if w[n].shape[-1] == D_MODEL:
            view = lambda t: t.reshape(-1, D_MODEL)
            res = adamw(view(w[n]), view(mom[n]), view(var[n]), (reduced,), off, "adamw_" + n)
        else:
            cols = w[n].shape[-1]
            view = lambda t: t.reshape(-1, cols)
            res = adamw(view(w[n]), view(mom[n]), view(var[n]), (view(reduced[off:off + nr]),), 0, "adamw_" + n)
        out_g[n], out_d[n], out_m[n], out_v[n] = (t.reshape(w[n].shape) for t in res)

    all_shapes = [full[n].shape for n in SMALL]
    g_rows = _rows_for(all_shapes, LANES)
    g_all = exchange_all(_pack_rows([grads[n] for n in SMALL], g_rows, LANES), "gather_small_grads")
    g_sum = _unpack_rows(sum_slots(g_all, "sum_small_grads"), all_shapes)
    g_mine = []
    for n, g in zip(SMALL, g_sum):
        if n in SMALL_SHARD_AXIS:
            ax = SMALL_SHARD_AXIS[n]
            g = lax.dynamic_slice_in_dim(g, chip * w[n].shape[ax], w[n].shape[ax], axis=ax)
        g_mine.append(g)
    my_shapes = [w[n].shape for n in SMALL]
    s_rows = _rows_for(my_shapes, LANES)
    pk = lambda d: _pack_rows([d[n] for n in SMALL], s_rows, LANES)
    res = adamw(pk(w), pk(mom), pk(var), (_pack_rows(g_mine, s_rows, LANES),), 0, "adamw_small")
    for dst, pack in zip((out_g, out_d, out_m, out_v), res):
        for n, t in zip(SMALL, _unpack_rows(pack, my_shapes)):
            dst[n] = t

    return (loss, grad_x[None], *[out_g[n] for n in WEIGHTS], *[out_d[n] for n in WEIGHTS],
            *[out_m[n] for n in WEIGHTS], *[out_v[n] for n in WEIGHTS])
```

```python
import functools
import math

import jax
import jax.numpy as jnp
from jax import lax
from jax.experimental import pallas as pl
from jax.experimental.pallas import tpu as pltpu

F32 = jnp.float32
BF16 = jnp.bfloat16

D_MODEL = 1024
DEPTH = 4
CHUNK = 64
A_HEADS = 8
A_DK = 128
A_W = 1024
A_CONV = 5
B_HEADS = 4
B_DK = 128
B_DV = 256
B_RANK = 16
B_TAU = 16.0
B_KW = 512
B_VW = 1024
ALPHA = (2 * DEPTH) ** 0.25
LN_EPS = 1e-5
RMS_EPS = 1e-6
L2_EPS = 1e-6
ADAM_LR = 0.001
ADAM_B1 = 0.9
ADAM_B2 = 0.999
ADAM_EPS = 1e-08
ADAM_WD = 0.01
ADAM_STEP = 10
LANES = 128
NEG_INF = float("-inf")
VMEM_LIMIT = 56 * 1024 * 1024


def _cparams(sem=None):
    return pltpu.CompilerParams(dimension_semantics=sem, vmem_limit_bytes=VMEM_LIMIT)


def _dg(a, b, ca, cb):
    return lax.dot_general(a.astype(BF16), b.astype(BF16), (((ca,), (cb,)), ((), ())),
                           preferred_element_type=F32)


def _split(x):
    hi = x.astype(BF16)
    return hi, (x - hi.astype(F32)).astype(BF16)


def _dg3(a, b, ca, cb):
    (a1, a2), (b1, b2) = _split(a), _split(b)
    return (_dg(a1, b2, ca, cb) + _dg(a2, b1, ca, cb)) + _dg(a1, b1, ca, cb)


def _dot_with_vjp(dg):
    @functools.partial(jax.custom_vjp, nondiff_argnums=(2, 3))
    def dot(a, b, ca, cb):
        return dg(a, b, ca, cb)

    def fwd(a, b, ca, cb):
        return dg(a, b, ca, cb), (a, b)

    def bwd(ca, cb, res, g):
        a, b = res
        da = dg(g, b, 1, 1 - cb) if ca == 1 else dg(b, g, 1 - cb, 1)
        db = dg(a, g, 1 - ca, 0) if cb == 0 else dg(g, a, 0, 1 - ca)
        return da, db

    dot.defvjp(fwd, bwd)
    return dot


bdot = _dot_with_vjp(_dg)
xdot3 = _dot_with_vjp(_dg3)


def nn(a, b):
    return bdot(a, b, 1, 0)


def nt(a, b):
    return bdot(a, b, 1, 1)


def tn(a, b):
    return bdot(a, b, 0, 0)


def xdot(a, b):
    return xdot3(a, b, 1, 0)


def _sigmoid(x):
    return 1.0 / (1.0 + jnp.exp(-x))


def _softplus(x):
    return jnp.maximum(x, 0.0) + jnp.log(1.0 + jnp.exp(-jnp.abs(x)))


def _chunk_masks(rev):
    ii = lax.broadcasted_iota(jnp.int32, (CHUNK, CHUNK), 0)
    jj = lax.broadcasted_iota(jnp.int32, (CHUNK, CHUNK), 1)
    d = (ii - jj) * (1 - 2 * rev)
    return d >= 0, d > 0, ii == jj, (ii >> 3) == (jj >> 3)


def _each(f, *lists):
    return [f(*xs) for xs in zip(*lists)]


@jax.custom_vjp
def _unit_triangular_inverse(a, ident, blockdiag):
    return _unit_triangular_inverse_impl(a, ident, blockdiag)


def _unit_triangular_inverse_fwd(a, ident, blockdiag):
    t = _unit_triangular_inverse_impl(a, ident, blockdiag)
    return t, (t, ident, blockdiag)


def _unit_triangular_inverse_bwd(res, g):
    t, ident, blockdiag = res
    left = _each(lambda x, y: xdot3(x, y, 0, 0), t, g)
    da = _each(lambda x, y: -xdot3(x, y, 1, 1), left, t)
    return da, jnp.zeros_like(ident), jnp.zeros_like(blockdiag)


_unit_triangular_inverse.defvjp(_unit_triangular_inverse_fwd, _unit_triangular_inverse_bwd)


@jax.custom_vjp
def _known_inverse(a, t):
    return t


def _known_inverse_bwd(t, g):
    left = _each(lambda x, y: xdot3(x, y, 0, 0), t, g)
    return _each(lambda x, y: -xdot3(x, y, 1, 1), left, t), _each(jnp.zeros_like, t)


_known_inverse.defvjp(lambda a, t: (t, t), _known_inverse_bwd)


def _unit_triangular_inverse_impl(a, ident, blockdiag):
    ad = _each(lambda x: x * blockdiag, a)
    e = _each(lambda x, y: x - y, a, ad)
    dinv = _each(lambda x: ident - x, ad)
    p = _each(xdot, ad, ad)
    dinv = _each(lambda x, y: x + xdot(x, y), dinv, p)
    p = _each(xdot, p, p)
    dinv = _each(lambda x, y: x + xdot(x, y), dinv, p)
    g = _each(lambda x, y: -xdot(x, y), dinv, e)
    finv = _each(lambda x: ident + x, g)
    p = _each(xdot, g, g)
    finv = _each(lambda x, y: x + xdot(x, y), finv, p)
    p = _each(xdot, p, p)
    finv = _each(lambda x, y: x + xdot(x, y), finv, p)
    return _each(xdot, finv, dinv)


def _gdn_step(state, q, k, v, bb, gb, rev, t_saved=None):
    causal, strict, eye, blockdiag = _chunk_masks(rev)
    lower = causal.astype(F32)
    ones = jnp.ones((CHUNK, CHUNK), F32)
    gcb = _each(lambda x: xdot(lower, x), gb)
    gcol = _each(lambda x: x[:, :CHUNK], gcb)
    grow = _each(lambda x: xdot(ones, jnp.where(eye, x, 0.0)), gcol)
    decay = _each(lambda x, y: jnp.exp(jnp.where(causal, x - y, NEG_INF)), gcol, grow)
    kb = _each(lambda x, y: x * y, k, bb)
    a = _each(lambda x, y, z: jnp.where(strict, nt(x, y) * z, 0.0), kb, k, decay)
    if t_saved is None:
        t = _unit_triangular_inverse(a, eye.astype(F32), blockdiag.astype(F32))
    else:
        t = _known_inverse(a, t_saved)
    egc = _each(jnp.exp, gcb)
    u = _each(lambda x, y, z: xdot(x, y * z), t, v, bb)
    w = _each(lambda x, y, z: xdot(x, y * z), t, kb, egc)
    qk = _each(lambda x, y, z: nt(x, y) * z, q, k, decay)
    glast = _each(lambda x: jnp.sum(x, axis=0, keepdims=True), gb)
    v_new = _each(lambda x, y, z: x - nn(y, z), u, w, state)
    o = _each(lambda x, y, z, p, r: nn(x * y, z) + nn(p, r), q, egc, state, qk, v_new)
    k_dec = _each(lambda x, y, z: x * jnp.exp(y - z), k, glast, gcb)
    state_new = _each(lambda x, y, z, p: x * jnp.exp(y) + tn(z, p), state, glast, k_dec, v_new)
    return state_new, o, t


def _gla_step(state_t, q, k, v, la, rev):
    causal, _, _, _ = _chunk_masks(rev)
    lower = causal.astype(F32)
    sign = 1 - 2 * rev
    b = _each(lambda x: xdot(lower, x), la)
    q = _each(lambda x: x * (B_DK ** -0.5), q)
    row = lax.broadcasted_iota(jnp.int32, (CHUNK, B_DK), 0)
    sub = row // GLA_SUB
    parts = []
    for blk in range(CHUNK // GLA_SUB):
        rows = slice(blk * GLA_SUB, (blk + 1) * GLA_SUB)
        r_at = jnp.where(rev == 1, GLA_SUB * (blk + 1), GLA_SUB * blk - 1)
        r = _each(lambda x: jnp.sum(jnp.where(row == r_at, x, 0.0), axis=0, keepdims=True), b)
        q_blk = _each(lambda x, y, z: x[rows] * jnp.exp(y[rows] - z), q, b, r)
        k_past = _each(lambda x, y, z: x * jnp.exp(jnp.where((sub - blk) * sign < 0, z - y, NEG_INF)), k, b, r)
        parts.append(_each(lambda x, y: xdot3(x, y, 1, 1), q_blk, k_past))
    scores = _each(lambda *p: jnp.concatenate(p, axis=0), *parts)
    shp = (GLA_SUB, GLA_SUB, B_DK)
    d3 = (lax.broadcasted_iota(jnp.int32, shp, 0) - lax.broadcasted_iota(jnp.int32, shp, 1)) * sign
    place_r = lax.broadcasted_iota(jnp.int32, (GLA_SUB, CHUNK), 0)
    place_c = lax.broadcasted_iota(jnp.int32, (GLA_SUB, CHUNK), 1)
    diag = []
    for blk in range(CHUNK // GLA_SUB):
        rows = slice(blk * GLA_SUB, (blk + 1) * GLA_SUB)
        place = (place_c == place_r + blk * GLA_SUB).astype(F32)

        def pairs(qh, kh, bh):
            qb, kb, bb = qh[rows], kh[rows], bh[rows]
            dec = jnp.exp(jnp.where(d3 >= 0, bb[:, None, :] - bb[None, :, :], NEG_INF))
            return xdot(jnp.sum(qb[:, None, :] * kb[None, :, :] * dec, axis=-1), place)

        diag.append(_each(pairs, q, k, b))
    scores = _each(lambda x, *d: x + jnp.concatenate(d, axis=0), scores, *diag)
    blast = _each(lambda x: jnp.sum(x, axis=0, keepdims=True), la)
    o = _each(lambda x, y, z, s, w: nt(x * jnp.exp(y), z) + nn(s, w), q, b, state_t, scores, v)
    k_dec = _each(lambda x, y, z: x * jnp.exp(y - z), k, blast, b)
    state_new = _each(lambda x, y, z, w: jnp.exp(x) * y + tn(z, w), blast, state_t, v, k_dec)
    return state_new, o


def _chunk_pos(d, m, n):
    return m + d * (n - 1 - 2 * m)


GLA_SUB = 16
GDN_HEADS_PER_STEP = 8
def gdn_rec_fwd(q, k, v, beta_b, g_b):
    s = q.shape[0]
    n = s // CHUNK

    hb = GDN_HEADS_PER_STEP
    wide = hb * LANES

    def body(q_ref, k_ref, v_ref, bb_ref, gb_ref, o_ref, st_ref, t_ref, state):
        d = pl.program_id(0)

        @pl.when(pl.program_id(2) == 0)
        def _():
            state[...] = jnp.zeros_like(state)

        cols = [slice(hh * LANES, (hh + 1) * LANES) for hh in range(hb)]
        st = [state[hh] for hh in range(hb)]
        new, o, t = _gdn_step(st, *([r[:, c] for c in cols] for r in (q_ref, k_ref, v_ref, bb_ref, gb_ref)), d)
        for hh, c in enumerate(cols):
            st_ref[hh] = st[hh]
            t_ref[hh] = t[hh]
            state[hh] = new[hh]
            o_ref[:, c] = o[hh]

    blk = pl.BlockSpec((CHUNK, wide), lambda d, h, m: (_chunk_pos(d, m, n), h))
    gate = pl.BlockSpec((CHUNK, wide), lambda d, h, m: (_chunk_pos(d, m, n), d * (A_HEADS // hb) + h))
    return pl.pallas_call(
        body, name="gdn_rec_fwd", grid=(2, A_HEADS // hb, n),
        in_specs=[blk, blk, blk, gate, gate],
        out_specs=[pl.BlockSpec((None, CHUNK, wide), lambda d, h, m: (d, _chunk_pos(d, m, n), h)),
                   pl.BlockSpec((None, hb, None, A_DK, LANES), lambda d, h, m: (d, h, _chunk_pos(d, m, n), 0, 0)),
                   pl.BlockSpec((None, hb, None, CHUNK, CHUNK), lambda d, h, m: (d, h, _chunk_pos(d, m, n), 0, 0))],
        out_shape=[jax.ShapeDtypeStruct((2, s, A_W), F32), jax.ShapeDtypeStruct((2, A_HEADS, n, A_DK, LANES), F32),
                   jax.ShapeDtypeStruct((2, A_HEADS, n, CHUNK, CHUNK), F32)],
        scratch_shapes=[pltpu.VMEM((hb, A_DK, LANES), F32)],
        compiler_params=_cparams(("arbitrary", "arbitrary", "arbitrary")),
    )(q, k, v, beta_b, g_b)


def gdn_rec_bwd(q, k, v, beta_b, g_b, states, tinv, do):
    s = q.shape[0]
    n = s // CHUNK

    hb = GDN_HEADS_PER_STEP
    wide = hb * LANES

    def body(q_ref, k_ref, v_ref, bb_ref, gb_ref, st_ref, t_ref, do_ref, dq_ref, dk_ref, dv_ref, dbb_ref, dgb_ref, dstate):
        d = pl.program_id(0)

        @pl.when(pl.program_id(2) == 0)
        def _():
            dstate[...] = jnp.zeros_like(dstate)

        def step(*a):
            return _gdn_step(*a, d, t_saved=[t_ref[hh] for hh in range(hb)])[:2]

        cols = [slice(hh * LANES, (hh + 1) * LANES) for hh in range(hb)]
        _, vjp = jax.vjp(step, [st_ref[hh] for hh in range(hb)],
                         *([r[:, c] for c in cols] for r in (q_ref, k_ref, v_ref, bb_ref, gb_ref)))
        grads = vjp(([dstate[hh] for hh in range(hb)], [do_ref[:, c] for c in cols]))
        for hh, c in enumerate(cols):
            dstate[hh], dq_ref[:, c], dk_ref[:, c], dv_ref[:, c], dbb_ref[:, c], dgb_ref[:, c] = (g[hh] for g in grads)

    pos = lambda d, m: _chunk_pos(1 - d, m, n)
    blk = pl.BlockSpec((CHUNK, wide), lambda d, h, m: (pos(d, m), h))
    gate = pl.BlockSpec((CHUNK, wide), lambda d, h, m: (pos(d, m), d * (A_HEADS // hb) + h))
    oblk = pl.BlockSpec((None, CHUNK, wide), lambda d, h, m: (d, pos(d, m), h))
    return pl.pallas_call(
        body, name="gdn_rec_bwd", grid=(2, A_HEADS // hb, n),
        in_specs=[blk, blk, blk, gate, gate,
                  pl.BlockSpec((None, hb, None, A_DK, LANES), lambda d, h, m: (d, h, pos(d, m), 0, 0)),
                  pl.BlockSpec((None, hb, None, CHUNK, CHUNK), lambda d, h, m: (d, h, pos(d, m), 0, 0)), blk],
        out_specs=[oblk, oblk, oblk, gate, gate],
        out_shape=[jax.ShapeDtypeStruct((2, s, A_W), F32)] * 3 + [jax.ShapeDtypeStruct(beta_b.shape, F32)] * 2,
        scratch_shapes=[pltpu.VMEM((hb, A_DK, LANES), F32)],
        compiler_params=_cparams(("arbitrary", "arbitrary", "arbitrary")),
    )(q, k, v, beta_b, g_b, states, tinv, do)


def gla_rec_fwd(proj, log_a):
    s = proj.shape[0]
    n = s // CHUNK

    kcols = [slice(h * B_DK, (h + 1) * B_DK) for h in range(B_HEADS)]
    vcols = [slice(h * B_DV, (h + 1) * B_DV) for h in range(B_HEADS)]

    def body(q_ref, k_ref, v_ref, la_ref, o_ref, st_ref, state):
        d = pl.program_id(0)

        @pl.when(pl.program_id(1) == 0)
        def _():
            state[...] = jnp.zeros_like(state)

        st = [state[h] for h in range(B_HEADS)]
        new, o = _gla_step(st, [q_ref[:, c] for c in kcols], [k_ref[:, c] for c in kcols], [v_ref[:, c] for c in vcols],
                           [la_ref[:, c] for c in kcols], d)
        for h in range(B_HEADS):
            st_ref[h] = st[h]
            state[h] = new[h]
            o_ref[:, vcols[h]] = o[h]

    pos = lambda d, m: _chunk_pos(d, m, n)
    return pl.pallas_call(
        body, name="gla_rec_fwd", grid=(2, n),
        in_specs=[pl.BlockSpec((CHUNK, B_KW), lambda d, m: (pos(d, m), 0)),
                  pl.BlockSpec((CHUNK, B_KW), lambda d, m: (pos(d, m), 1)),
                  pl.BlockSpec((CHUNK, B_VW), lambda d, m: (pos(d, m), 2 * B_KW // B_VW)),
                  pl.BlockSpec((None, CHUNK, B_KW), lambda d, m: (d, pos(d, m), 0))],
        out_specs=[pl.BlockSpec((None, CHUNK, B_VW), lambda d, m: (d, pos(d, m), 0)),
                   pl.BlockSpec((None, B_HEADS, None, B_DV, B_DK), lambda d, m: (d, 0, pos(d, m), 0, 0))],
        out_shape=[jax.ShapeDtypeStruct((2, s, B_VW), F32), jax.ShapeDtypeStruct((2, B_HEADS, n, B_DV, B_DK), F32)],
        scratch_shapes=[pltpu.VMEM((B_HEADS, B_DV, B_DK), F32)],
        compiler_params=_cparams(("arbitrary", "arbitrary")),
    )(proj, proj, proj, log_a)


def gla_rec_bwd(proj, log_a, states, do):
    s = proj.shape[0]
    n = s // CHUNK

    kcols = [slice(h * B_DK, (h + 1) * B_DK) for h in range(B_HEADS)]
    vcols = [slice(h * B_DV, (h + 1) * B_DV) for h in range(B_HEADS)]

    def body(q_ref, k_ref, v_ref, la_ref, st_ref, do_ref, dq_ref, dk_ref, dv_ref, dla_ref, dstate):
        d = pl.program_id(0)

        @pl.when(pl.program_id(1) == 0)
        def _():
            dstate[...] = jnp.zeros_like(dstate)

        step = functools.partial(_gla_step, rev=d)
        _, vjp = jax.vjp(step, [st_ref[h] for h in range(B_HEADS)], [q_ref[:, c] for c in kcols],
                         [k_ref[:, c] for c in kcols], [v_ref[:, c] for c in vcols], [la_ref[:, c] for c in kcols])
        dst, dq, dk, dv, dla = vjp(([dstate[h] for h in range(B_HEADS)], [do_ref[:, c] for c in vcols]))
        for h in range(B_HEADS):
            dstate[h] = dst[h]
            dq_ref[:, kcols[h]] = dq[h]
            dk_ref[:, kcols[h]] = dk[h]
            dv_ref[:, vcols[h]] = dv[h]
            dla_ref[:, kcols[h]] = dla[h]

    pos = lambda d, m: _chunk_pos(1 - d, m, n)
    kblk = pl.BlockSpec((None, CHUNK, B_KW), lambda d, m: (d, pos(d, m), 0))
    return pl.pallas_call(
        body, name="gla_rec_bwd", grid=(2, n),
        in_specs=[pl.BlockSpec((CHUNK, B_KW), lambda d, m: (pos(d, m), 0)),
                  pl.BlockSpec((CHUNK, B_KW), lambda d, m: (pos(d, m), 1)),
                  pl.BlockSpec((CHUNK, B_VW), lambda d, m: (pos(d, m), 2 * B_KW // B_VW)),
                  kblk,
                  pl.BlockSpec((None, B_HEADS, None, B_DV, B_DK), lambda d, m: (d, 0, pos(d, m), 0, 0)),
                  pl.BlockSpec((CHUNK, B_VW), lambda d, m: (pos(d, m), 0))],
        out_specs=[kblk, kblk, pl.BlockSpec((None, CHUNK, B_VW), lambda d, m: (d, pos(d, m), 0)), kblk],
        out_shape=[jax.ShapeDtypeStruct((2, s, B_KW), F32), jax.ShapeDtypeStruct((2, s, B_KW), F32),
                   jax.ShapeDtypeStruct((2, s, B_VW), F32), jax.ShapeDtypeStruct((2, s, B_KW), F32)],
        scratch_shapes=[pltpu.VMEM((B_HEADS, B_DV, B_DK), F32)],
        compiler_params=_cparams(("arbitrary", "arbitrary")),
    )(proj, proj, proj, log_a, states, do)


MM_TILE_OUT = 1024
MM_TILE_K = 1024


def _tile(n, pref):
    return pref if n % pref == 0 else n


class Gathered:
    def __init__(self, g, off, kind):
        assert off % D_MODEL == 0 and MM_TILE_OUT == D_MODEL and MM_TILE_K == D_MODEL
        self.g, self.blk, self.kind = g, off // D_MODEL, kind
        self.shape = (D_MODEL, 4 * D_MODEL) if kind == "cols" else (4 * D_MODEL, D_MODEL)

    def spec(self, mode):
        blk = self.blk
        chip_is_k = (self.kind == "rows") == (mode == "nn")
        if chip_is_k:
            return pl.BlockSpec((None, D_MODEL, D_MODEL), lambda i, j, k: (k, blk, 0))
        return pl.BlockSpec((None, D_MODEL, D_MODEL), lambda i, j, k: (j, blk, 0))


def mm(a, b, mode="nn", act=None, epi=None, extra=None, alpha=1.0, chip_major=False, pack=None, name="mm"):
    if mode == "tn":
        kk, m = a.shape
    else:
        m, kk = a.shape
    nn_ = b.shape[0] if mode == "nt" else b.shape[1]
    tm, tn_, tk = _tile(m, MM_TILE_OUT), _tile(nn_, MM_TILE_OUT), _tile(kk, MM_TILE_K)
    nk = kk // tk
    ca, cb = {"nn": (1, 0), "nt": (1, 1), "tn": (0, 0)}[mode]

    def body(*refs):
        o_ref = refs[-1]
        a_ref, b_ref = refs[:2]
        if epi is not None:
            e_ref = refs[2]
        kstep = pl.program_id(2)
        at = a_ref[...]
        if act == "sqrelu":
            at = jnp.square(jnp.maximum(at, 0.0))
        part = _dg(at, b_ref[...], ca, cb)

        @pl.when(kstep == 0)
        def _():
            o_ref[...] = part

        @pl.when(kstep > 0)
        def _():
            o_ref[...] += part

        if epi is not None:
            @pl.when(kstep == nk - 1)
            def _():
                if epi == "dsqrelu":
                    o_ref[...] = o_ref[...] * (2.0 * jnp.maximum(e_ref[...], 0.0))
                else:
                    o_ref[...] = o_ref[...] + alpha * e_ref[...]

    a_spec = pl.BlockSpec((tk, tm), lambda i, j, k: (k, i)) if mode == "tn" else pl.BlockSpec((tm, tk), lambda i, j, k: (i, k))
    if isinstance(b, Gathered):
        assert mode in ("nn", "nt") and tn_ == D_MODEL and tk == D_MODEL
        b_spec, b = b.spec(mode), b.g
    elif mode == "nt":
        b_spec = pl.BlockSpec((tn_, tk), lambda i, j, k: (j, k))
    else:
        b_spec = pl.BlockSpec((tk, tn_), lambda i, j, k: (k, j))
    o_spec = pl.BlockSpec((tm, tn_), lambda i, j, k: (i, j))
    ins, specs = [a, b], [a_spec, b_spec]
    if epi is not None:
        ins.append(extra)
        specs.append(o_spec)
    out_shape = jax.ShapeDtypeStruct((m, nn_), F32)
    if chip_major:
        assert nn_ == 4 * D_MODEL and tn_ == D_MODEL
        o_spec = pl.BlockSpec((None, tm, D_MODEL), lambda i, j, k: (j, i, 0))
        out_shape = jax.ShapeDtypeStruct((4, m, D_MODEL), F32)
    aliases = {}
    if pack is not None:
        buf, rows, blk, kind = pack
        logical = (D_MODEL, 4 * D_MODEL) if kind == "cols" else (4 * D_MODEL, D_MODEL)
        assert epi is None and tm == D_MODEL and tn_ == D_MODEL and (m, nn_) == logical
        if kind == "cols":
            o_spec = pl.BlockSpec((None, D_MODEL, D_MODEL), lambda i, j, k: (j, blk, 0))
        else:
            o_spec = pl.BlockSpec((None, D_MODEL, D_MODEL), lambda i, j, k: (i, blk, 0))
        out_shape = jax.ShapeDtypeStruct((4, rows, D_MODEL), F32)
        if buf is not None:
            aliases = {len(ins): 0}
            ins.append(buf)
            specs.append(pl.BlockSpec(memory_space=pl.ANY))
    return pl.pallas_call(
        body, name=name, grid=(m // tm, nn_ // tn_, nk), in_specs=specs, out_specs=o_spec, out_shape=out_shape,
        input_output_aliases=aliases, compiler_params=_cparams(("parallel", "parallel", "arbitrary")),
    )(*ins)


ROWS = 512
POST_ROWS = 2048
LN_ROWS = 512


def _ln_core(x, m, g, b):
    r = ALPHA * x + m
    mu = jnp.mean(r, axis=-1, keepdims=True)
    xc = r - mu
    var = jnp.mean(xc * xc, axis=-1, keepdims=True)
    rstd = lax.rsqrt(var + LN_EPS)
    xhat = xc * rstd
    return xhat * g + b, xhat, rstd


def ln_fwd(x, m, g, b):
    s, dm = x.shape
    rows = _tile(s, LN_ROWS)

    def body(x_ref, m_ref, g_ref, b_ref, o_ref, ob_ref):
        y = _ln_core(x_ref[...], m_ref[...], g_ref[...], b_ref[...])[0]
        o_ref[...] = y
        ob_ref[...] = y.astype(BF16)

    row = pl.BlockSpec((rows, dm), lambda i: (i, 0))
    vec = pl.BlockSpec((1, dm), lambda i: (0, 0))
    return pl.pallas_call(body, name="ln_fwd", grid=(s // rows,), in_specs=[row, row, vec, vec], out_specs=[row, row],
                          out_shape=[jax.ShapeDtypeStruct((s, dm), F32), jax.ShapeDtypeStruct((s, dm), BF16)],
                          compiler_params=_cparams(("parallel",)))(x, m, g, b)


def ln_bwd(x, m, g, dy):
    s, dm = x.shape
    rows = _tile(s, LN_ROWS)

    def body(x_ref, m_ref, g_ref, dy_ref, dr_ref, dg_ref, db_ref):
        gv = g_ref[...]
        _, xhat, rstd = _ln_core(x_ref[...], m_ref[...], gv, jnp.zeros_like(gv))
        dy = dy_ref[...]
        dxh = dy * gv
        dr_ref[...] = rstd * (dxh - jnp.mean(dxh, axis=-1, keepdims=True)
                              - xhat * jnp.mean(dxh * xhat, axis=-1, keepdims=True))

        @pl.when(pl.program_id(0) == 0)
        def _():
            dg_ref[...] = jnp.zeros_like(dg_ref)
            db_ref[...] = jnp.zeros_like(db_ref)

        dg_ref[...] += jnp.sum(dy * xhat, axis=0, keepdims=True)
        db_ref[...] += jnp.sum(dy, axis=0, keepdims=True)

    row = pl.BlockSpec((rows, dm), lambda i: (i, 0))
    vec = pl.BlockSpec((1, dm), lambda i: (0, 0))
    return pl.pallas_call(body, name="ln_bwd", grid=(s // rows,), in_specs=[row, row, vec, row], out_specs=[row, vec, vec],
                          out_shape=[jax.ShapeDtypeStruct((s, dm), F32), jax.ShapeDtypeStruct((1, dm), F32),
                                     jax.ShapeDtypeStruct((1, dm), F32)],
                          compiler_params=_cparams(("arbitrary",)))(x, m, g, dy)


def loss_head(y, target):
    s, dm = y.shape
    rows = _tile(s, LN_ROWS)

    def body(y_ref, t_ref, dy_ref, l_ref):
        e = y_ref[...] - t_ref[...]
        dy_ref[...] = e * (1.0 / dm)

        @pl.when(pl.program_id(0) == 0)
        def _():
            l_ref[...] = jnp.zeros_like(l_ref)

        col = jnp.sum(e * e, axis=0, keepdims=True) * (0.5 / dm)
        acc = col[:, :LANES]
        for c in range(1, dm // LANES):
            acc = acc + col[:, c * LANES:(c + 1) * LANES]
        l_ref[...] += acc

    row = pl.BlockSpec((rows, dm), lambda i: (i, 0))
    return pl.pallas_call(body, name="loss_head", grid=(s // rows,), in_specs=[row, row],
                          out_specs=[row, pl.BlockSpec((1, LANES), lambda i: (0, 0))],
                          out_shape=[jax.ShapeDtypeStruct((s, dm), F32), jax.ShapeDtypeStruct((1, LANES), F32)],
                          compiler_params=_cparams(("arbitrary",)))(y, target)


def _shift_rows_impl(x, d):
    n = x.shape[0]
    if d == 0:
        return x
    t = lax.broadcasted_iota(jnp.int32, x.shape, 0)
    return jnp.where((t + d >= 0) & (t + d < n), pltpu.roll(x, (-d) % n, 0), 0.0)


@functools.partial(jax.custom_vjp, nondiff_argnums=(1,))
def _shift_rows(x, d):
    return _shift_rows_impl(x, d)


_shift_rows.defvjp(lambda x, d: (_shift_rows_impl(x, d), None), lambda d, _, g: (_shift_rows_impl(g, -d),))


def _gdn_pre_fn(u, w, kind):
    rows = lax.broadcasted_iota(jnp.int32, w.shape, 0)
    c = None
    for tap in range(A_CONV):
        w_tap = jnp.sum(jnp.where(rows == tap, w, 0.0), axis=0, keepdims=True)
        term = _shift_rows(u, tap - A_CONV // 2) * w_tap
        c = term if c is None else c + term
    y = c * _sigmoid(c)
    if kind == "v":
        return y
    y = y * lax.rsqrt(jnp.sum(y * y, axis=-1, keepdims=True) + L2_EPS)
    return y * (A_DK ** -0.5) if kind == "q" else y


_KIND_OFF = {"q": 0, "k": A_HEADS, "v": 2 * A_HEADS}


def gdn_pre(proj, conv_w, kind):
    s = proj.shape[0]
    off = _KIND_OFF[kind]

    def body(u_ref, w_ref, o_ref):
        o_ref[...] = _gdn_pre_fn(u_ref[...], w_ref[...], kind)

    return pl.pallas_call(
        body, name="gdn_pre_" + kind, grid=(A_HEADS,),
        in_specs=[pl.BlockSpec((s, LANES), lambda h: (0, off + h)), pl.BlockSpec((A_CONV, LANES), lambda h: (0, off + h))],
        out_specs=pl.BlockSpec((s, LANES), lambda h: (0, h)),
        out_shape=jax.ShapeDtypeStruct((s, A_W), F32), compiler_params=_cparams(("parallel",)))(proj, conv_w)


def gdn_pre_bwd(proj, conv_w, dt2, kind):
    s = proj.shape[0]
    off = _KIND_OFF[kind]

    def body(u_ref, w_ref, d0_ref, d1_ref, du_ref, dw_ref):
        _, vjp = jax.vjp(functools.partial(_gdn_pre_fn, kind=kind), u_ref[...], w_ref[...])
        du, dw = vjp(d0_ref[...] + d1_ref[...])
        du_ref[...] = du
        dw_ref[...] = dw

    return pl.pallas_call(
        body, name="gdn_pre_bwd_" + kind, grid=(A_HEADS,),
        in_specs=[pl.BlockSpec((s, LANES), lambda h: (0, off + h)), pl.BlockSpec((A_CONV, LANES), lambda h: (0, off + h)),
                  pl.BlockSpec((None, s, LANES), lambda h: (0, 0, h)), pl.BlockSpec((None, s, LANES), lambda h: (1, 0, h))],
        out_specs=[pl.BlockSpec((s, LANES), lambda h: (0, h)), pl.BlockSpec((A_CONV, LANES), lambda h: (0, h))],
        out_shape=[jax.ShapeDtypeStruct((s, A_W), F32), jax.ShapeDtypeStruct((A_CONV, A_W), F32)],
        compiler_params=_cparams(("parallel",)))(proj, conv_w, dt2, dt2)


N_GATE = 2 * A_HEADS


def _gdn_gates_fn(ba, alog_row, dt_row):
    r = lax.broadcasted_iota(jnp.int32, (LANES, N_GATE * LANES), 0)
    c = lax.broadcasted_iota(jnp.int32, (LANES, N_GATE * LANES), 1) >> 7
    beta_b = xdot(_sigmoid(ba), (r == c).astype(F32))
    g = -(jnp.exp(alog_row) * _softplus(ba + dt_row))
    g_b = xdot(g, (r == c + N_GATE).astype(F32))
    return beta_b, g_b


def gdn_gates(ba, alog_row, dt_row):
    s = ba.shape[0]

    def body(ba_ref, al_ref, dt_ref, bb_ref, gb_ref):
        bb_ref[...], gb_ref[...] = _gdn_gates_fn(ba_ref[...], al_ref[...], dt_ref[...])

    row = pl.BlockSpec((ROWS, LANES), lambda i: (i, 0))
    vec = pl.BlockSpec((1, LANES), lambda i: (0, 0))
    wide = pl.BlockSpec((ROWS, N_GATE * LANES), lambda i: (i, 0))
    return pl.pallas_call(body, name="gdn_gates", grid=(s // ROWS,), in_specs=[row, vec, vec], out_specs=[wide, wide],
                          out_shape=[jax.ShapeDtypeStruct((s, N_GATE * LANES), F32)] * 2,
                          compiler_params=_cparams(("parallel",)))(ba, alog_row, dt_row)


def gdn_gates_bwd(ba, alog_row, dt_row, dbeta_b, dg_b):
    s = ba.shape[0]

    def body(ba_ref, al_ref, dt_ref, dbb_ref, dgb_ref, dba_ref, dal_ref, ddt_ref):
        _, vjp = jax.vjp(_gdn_gates_fn, ba_ref[...], al_ref[...], dt_ref[...])
        dba, dal, ddt = vjp((dbb_ref[...], dgb_ref[...]))
        dba_ref[...] = dba

        @pl.when(pl.program_id(0) == 0)
        def _():
            dal_ref[...] = jnp.zeros_like(dal_ref)
            ddt_ref[...] = jnp.zeros_like(ddt_ref)

        dal_ref[...] += dal
        ddt_ref[...] += ddt

    row = pl.BlockSpec((ROWS, LANES), lambda i: (i, 0))
    vec = pl.BlockSpec((1, LANES), lambda i: (0, 0))
    wide = pl.BlockSpec((ROWS, N_GATE * LANES), lambda i: (i, 0))
    return pl.pallas_call(body, name="gdn_gates_bwd", grid=(s // ROWS,), in_specs=[row, vec, vec, wide, wide],
                          out_specs=[row, vec, vec],
                          out_shape=[jax.ShapeDtypeStruct((s, LANES), F32), jax.ShapeDtypeStruct((1, LANES), F32),
                                     jax.ShapeDtypeStruct((1, LANES), F32)],
                          compiler_params=_cparams(("arbitrary",)))(ba, alog_row, dt_row, dbeta_b, dg_b)


def _post_fn(o, z, g):
    y = o * lax.rsqrt(jnp.mean(o * o, axis=-1, keepdims=True) + RMS_EPS) * g
    return y * (z * _sigmoid(z))


def mixer_post(o2, proj, norm_g, width, gate_off, name):
    s = o2.shape[1]
    nh = o2.shape[2] // width

    rows = _tile(s, POST_ROWS)

    def body(o0_ref, o1_ref, z_ref, g_ref, y_ref):
        y_ref[...] = _post_fn(o0_ref[...] + o1_ref[...], z_ref[...], g_ref[...])

    ospec = lambda d: pl.BlockSpec((None, rows, width), lambda i, h: (d, i, h))
    return pl.pallas_call(
        body, name=name, grid=(s // rows, nh),
        in_specs=[ospec(0), ospec(1), pl.BlockSpec((rows, width), lambda i, h: (i, gate_off + h)),
                  pl.BlockSpec((1, width), lambda i, h: (0, 0))],
        out_specs=pl.BlockSpec((rows, width), lambda i, h: (i, h)),
        out_shape=jax.ShapeDtypeStruct((s, o2.shape[2]), F32),
        compiler_params=_cparams(("parallel", "parallel")))(o2, o2, proj, norm_g)


def mixer_post_bwd(o2, proj, norm_g, dy, width, gate_off, name):
    s = o2.shape[1]
    nh = o2.shape[2] // width

    def body(o0_ref, o1_ref, z_ref, g_ref, dy_ref, do_ref, dz_ref, dg_ref):
        _, vjp = jax.vjp(_post_fn, o0_ref[...] + o1_ref[...], z_ref[...], g_ref[...])
        do, dz, dg = vjp(dy_ref[...])
        do_ref[...] = do
        dz_ref[...] = dz

        @pl.when((pl.program_id(0) == 0) & (pl.program_id(1) == 0))
        def _():
            dg_ref[...] = jnp.zeros_like(dg_ref)

        dg_ref[...] += dg

    rows = _tile(s, POST_ROWS)
    ospec = lambda d: pl.BlockSpec((None, rows, width), lambda i, h: (d, i, h))
    blk = pl.BlockSpec((rows, width), lambda i, h: (i, h))
    vec = pl.BlockSpec((1, width), lambda i, h: (0, 0))
    return pl.pallas_call(
        body, name=name, grid=(s // rows, nh),
        in_specs=[ospec(0), ospec(1), pl.BlockSpec((rows, width), lambda i, h: (i, gate_off + h)), vec, blk],
        out_specs=[blk, blk, vec],
        out_shape=[jax.ShapeDtypeStruct((s, o2.shape[2]), F32)] * 2 + [jax.ShapeDtypeStruct((1, width), F32)],
        compiler_params=_cparams(("arbitrary", "arbitrary")))(o2, o2, proj, norm_g, dy)


def _log_gate(z):
    return (jnp.minimum(z, 0.0) - jnp.log(1.0 + jnp.exp(-jnp.abs(z)))) * (1.0 / B_TAU)


def gla_gate(gl, w2, gb):
    s = gl.shape[0]

    def body(gl_ref, w_ref, b_ref, o_ref):
        for n in range(2):
            o_ref[n] = _log_gate(nn(gl_ref[...], w_ref[n]) + b_ref[n])

    full = lambda shp: pl.BlockSpec(shp, lambda i: (0,) * len(shp))
    return pl.pallas_call(
        body, name="gla_gate", grid=(s // ROWS,),
        in_specs=[pl.BlockSpec((ROWS, LANES), lambda i: (i, 0)), full(w2.shape), full(gb.shape)],
        out_specs=pl.BlockSpec((2, ROWS, B_KW), lambda i: (0, i, 0)),
        out_shape=jax.ShapeDtypeStruct((2, s, B_KW), F32), compiler_params=_cparams(("parallel",)))(gl, w2, gb)


def gla_gate_bwd(gl, w2, gb, dla):
    s = gl.shape[0]

    def body(gl_ref, w_ref, b_ref, dla_ref, dgl_ref, dz_ref, db0_ref, db1_ref):
        @pl.when(pl.program_id(0) == 0)
        def _():
            db0_ref[...] = jnp.zeros_like(db0_ref)
            db1_ref[...] = jnp.zeros_like(db1_ref)

        dgl = None
        for n, db_ref in enumerate((db0_ref, db1_ref)):
            _, vjp = jax.vjp(_log_gate, nn(gl_ref[...], w_ref[n]) + b_ref[n])
            dz, = vjp(dla_ref[n])
            dz_ref[n] = dz
            db_ref[...] += jnp.sum(dz, axis=0, keepdims=True)
            part = nt(dz, w_ref[n])
            dgl = part if dgl is None else dgl + part
        dgl_ref[...] = dgl

    full = lambda shp: pl.BlockSpec(shp, lambda i: (0,) * len(shp))
    row = pl.BlockSpec((ROWS, LANES), lambda i: (i, 0))
    wide = pl.BlockSpec((2, ROWS, B_KW), lambda i: (0, i, 0))
    vec = pl.BlockSpec((1, B_KW), lambda i: (0, 0))
    return pl.pallas_call(
        body, name="gla_gate_bwd", grid=(s // ROWS,),
        in_specs=[row, full(w2.shape), full(gb.shape), wide],
        out_specs=[row, wide, vec, vec],
        out_shape=[jax.ShapeDtypeStruct((s, LANES), F32), jax.ShapeDtypeStruct((2, s, B_KW), F32),
                   jax.ShapeDtypeStruct((1, B_KW), F32), jax.ShapeDtypeStruct((1, B_KW), F32)],
        compiler_params=_cparams(("arbitrary",)))(gl, w2, gb, dla)


PACK_TILE = 512


def cast_into_slot(x, chip):
    r, c = x.shape

    def body(chip_ref, x_ref, o_ref):
        o_ref[...] = x_ref[...].astype(BF16)

    return pl.pallas_call(
        body, name="cast_into_slot",
        grid_spec=pltpu.PrefetchScalarGridSpec(
            num_scalar_prefetch=1, grid=(r // PACK_TILE,),
            in_specs=[pl.BlockSpec((PACK_TILE, c), lambda i, chip_ref: (i, 0))],
            out_specs=pl.BlockSpec((None, PACK_TILE, c), lambda i, chip_ref: (chip_ref[0], i, 0))),
        out_shape=jax.ShapeDtypeStruct((4, r, c), BF16), compiler_params=_cparams(("parallel",)))(chip, x)


def sum_received(chip_sum, recv, chip, core):
    _, h, c = chip_sum.shape
    n = recv.shape[0]
    tr = _tile(h, PACK_TILE)
    nblk = h // tr

    def body(chip_ref, core_ref, own_ref, r_ref, o_ref):
        acc = r_ref[0].astype(F32)
        for k in range(1, n):
            acc = acc + r_ref[k].astype(F32)
        o_ref[...] = acc + own_ref[...].astype(F32)

    return pl.pallas_call(
        body, name="sum_received",
        grid_spec=pltpu.PrefetchScalarGridSpec(
            num_scalar_prefetch=2, grid=(nblk,),
            in_specs=[pl.BlockSpec((None, tr, c), lambda i, chip_ref, core_ref: (chip_ref[0], i, 0)),
                      pl.BlockSpec((n, tr, c), lambda i, chip_ref, core_ref: (0, i, 0))],
            out_specs=pl.BlockSpec((tr, c), lambda i, chip_ref, core_ref: (core_ref[0] * nblk + i, 0))),
        out_shape=jax.ShapeDtypeStruct((2 * h, c), F32), compiler_params=_cparams(("parallel",)))(chip, core, chip_sum, recv)


def merge_first_hop(chip_sum, passed, slot_x, slot_y, core):
    _, h, c = chip_sum.shape
    tr = _tile(h, PACK_TILE)

    def body(sx_ref, sy_ref, core_ref, to_x_ref, to_y_ref, p_ref, o_ref):
        p = p_ref[...].astype(F32)
        is_y = core_ref[0].astype(F32)
        o_ref[0] = (to_x_ref[...].astype(F32) + p * (1.0 - is_y)).astype(BF16)
        o_ref[1] = (to_y_ref[...].astype(F32) + p * is_y).astype(BF16)

    return pl.pallas_call(
        body, name="merge_first_hop",
        grid_spec=pltpu.PrefetchScalarGridSpec(
            num_scalar_prefetch=3, grid=(h // tr,),
            in_specs=[pl.BlockSpec((None, tr, c), lambda i, sx_ref, sy_ref, core_ref: (sx_ref[0], i, 0)),
                      pl.BlockSpec((None, tr, c), lambda i, sx_ref, sy_ref, core_ref: (sy_ref[0], i, 0)),
                      pl.BlockSpec((tr, c), lambda i, sx_ref, sy_ref, core_ref: (i, 0))],
            out_specs=pl.BlockSpec((2, tr, c), lambda i, sx_ref, sy_ref, core_ref: (0, i, 0))),
        out_shape=jax.ShapeDtypeStruct((2, h, c), BF16),
        compiler_params=_cparams(("parallel",)))(slot_x, slot_y, core, chip_sum, chip_sum, passed)


def sum_slots(x, name):
    n, r, c = x.shape
    tr = _tile(r, PACK_TILE)

    def body(x_ref, o_ref):
        acc = x_ref[0].astype(F32)
        for k in range(1, n):
            acc = acc + x_ref[k].astype(F32)
        o_ref[...] = acc

    return pl.pallas_call(body, name=name, grid=(r // tr,), in_specs=[pl.BlockSpec((n, tr, c), lambda i: (0, i, 0))],
                          out_specs=pl.BlockSpec((tr, c), lambda i: (i, 0)),
                          out_shape=jax.ShapeDtypeStruct((r, c), F32), compiler_params=_cparams(("parallel",)))(x)


def half_to_bf16(gpack, which, theirs=None):
    n, r, c = gpack.shape
    half_rows = r // 2
    tr = _tile(half_rows, PACK_TILE)
    nblk = half_rows // tr

    def body(which_ref, g_ref, *rest):
        o_ref = rest[-1]
        acc = g_ref[...]
        if theirs is not None:
            acc = acc + rest[0][...].astype(F32)
        o_ref[...] = acc.astype(BF16)

    blk = pl.BlockSpec((None, tr, c), lambda s, i, which_ref: (s, i, 0))
    ins = [gpack] if theirs is None else [gpack, theirs]
    return pl.pallas_call(
        body, name="half_to_bf16" if theirs is None else "add_sibling_half",
        grid_spec=pltpu.PrefetchScalarGridSpec(
            num_scalar_prefetch=1, grid=(n, nblk),
            in_specs=[pl.BlockSpec((None, tr, c), lambda s, i, which_ref: (s, which_ref[0] * nblk + i, 0))]
            + [blk] * (len(ins) - 1),
            out_specs=blk),
        out_shape=jax.ShapeDtypeStruct((n, half_rows, c), BF16),
        compiler_params=_cparams(("parallel", "parallel")))(which, *ins)


def adamw(w, m, v, grads, g_row_off, name):
    r, c = w.shape
    tr = next(t for t in (PACK_TILE, r) if r % t == 0 and g_row_off % t == 0)
    ob = g_row_off // tr
    ng = len(grads)

    def body(*refs):
        w_ref, m_ref, v_ref = refs[:3]
        g_refs = refs[3:3 + ng]
        g_ref, d_ref, nm_ref, nv_ref = refs[3 + ng:]
        g = g_refs[0][...]
        for gr in g_refs[1:]:
            g = g + gr[...]
        m_new = ADAM_B1 * m_ref[...] + (1.0 - ADAM_B1) * g
        v_new = ADAM_B2 * v_ref[...] + (1.0 - ADAM_B2) * jnp.square(g)
        m_hat = m_new / (1.0 - ADAM_B1 ** ADAM_STEP)
        v_hat = v_new / (1.0 - ADAM_B2 ** ADAM_STEP)
        g_ref[...] = g
        d_ref[...] = -ADAM_LR * (m_hat / (jnp.sqrt(v_hat) + ADAM_EPS) + ADAM_WD * w_ref[...])
        nm_ref[...] = m_new
        nv_ref[...] = v_new

    blk = pl.BlockSpec((tr, c), lambda i: (i, 0))
    gblk = pl.BlockSpec((tr, c), lambda i: (i + ob, 0))
    return pl.pallas_call(body, name=name, grid=(r // tr,), in_specs=[blk, blk, blk] + [gblk] * ng, out_specs=[blk] * 4,
                          out_shape=[jax.ShapeDtypeStruct((r, c), F32)] * 4,
                          compiler_params=_cparams(("parallel",)))(w, m, v, *grads)


MESH = pl.DeviceIdType.MESH
HBM = pl.BlockSpec(memory_space=pl.ANY)


def _place():
    return lax.axis_index("x"), lax.axis_index("y"), lax.axis_index("c")


def allgather_chips(buf):
    _, r, c = buf.shape
    half_rows = r // 2

    def body(_, out_ref, send_sems, recv_sems):
        x, y, cc = _place()
        half = pl.ds(cc * half_rows, half_rows)
        other = pl.ds((1 - cc) * half_rows, half_rows)

        def copy(k, rows, to):
            return pltpu.make_async_remote_copy(src_ref=rows, dst_ref=rows, send_sem=send_sems.at[k],
                                                recv_sem=recv_sems.at[k], device_id=to, device_id_type=MESH)

        nbr_x, nbr_y, diag = (1 - x, y), (x, 1 - y), (1 - x, 1 - y)
        slot = lambda chip: 2 * chip[0] + chip[1]
        sibling = (x, y, 1 - cc)
        first = [copy(0, out_ref.at[slot((x, y)), half], (*nbr_x, cc)), copy(1, out_ref.at[slot((x, y)), half], (*nbr_y, cc))]
        for cp in first:
            cp.start()
        passed = []
        for k, chip in enumerate((nbr_x, nbr_y)):
            landed = out_ref.at[slot(chip), half]
            copy(k, landed, (*chip, cc)).wait_recv()
            passed.append(copy(3 + k, landed, sibling))
            passed[-1].start()
        via = (1 - x + cc * (2 * x - 1), y + cc * (1 - 2 * y))
        to = (x + cc * (1 - 2 * x), 1 - y + cc * (2 * y - 1))
        hop = copy(2, out_ref.at[slot(via), half], (*to, cc))
        hop.start()
        landed = out_ref.at[slot(diag), half]
        copy(2, landed, (*to, cc)).wait_recv()
        passed.append(copy(5, landed, sibling))
        passed[-1].start()
        for k, chip in enumerate((nbr_x, nbr_y, diag)):
            copy(3 + k, out_ref.at[slot(chip), other], sibling).wait_recv()
        for cp in first + [hop] + passed:
            cp.wait_send()

    return pl.pallas_call(
        body, name="allgather_chips", in_specs=[HBM], out_specs=HBM, input_output_aliases={0: 0},
        out_shape=jax.ShapeDtypeStruct(buf.shape, buf.dtype),
        scratch_shapes=[pltpu.SemaphoreType.DMA((6,)), pltpu.SemaphoreType.DMA((6,))],
    )(buf)


def scatter_first_hop(gpack):
    _, r, c = gpack.shape

    def body(src_ref, out_ref, send_sem, recv_sem):
        x, y, cc = _place()
        to = (x + cc * (1 - 2 * x), 1 - y + cc * (2 * y - 1), cc)
        cp = pltpu.make_async_remote_copy(src_ref=src_ref.at[2 * (1 - x) + (1 - y)], dst_ref=out_ref, send_sem=send_sem,
                                          recv_sem=recv_sem, device_id=to, device_id_type=MESH)
        cp.start()
        cp.wait()

    return pl.pallas_call(
        body, name="scatter_first_hop", in_specs=[HBM], out_specs=HBM,
        out_shape=jax.ShapeDtypeStruct((r, c), gpack.dtype),
        scratch_shapes=[pltpu.SemaphoreType.DMA, pltpu.SemaphoreType.DMA],
    )(gpack)


def scatter_second_hop(to_nbrs):
    def body(src_ref, out_ref, send_sems, recv_sems):
        x, y, cc = _place()
        sends = [pltpu.make_async_remote_copy(src_ref=src_ref.at[k], dst_ref=out_ref.at[k], send_sem=send_sems.at[k],
                                              recv_sem=recv_sems.at[k], device_id=to, device_id_type=MESH)
                 for k, to in enumerate(((1 - x, y, cc), (x, 1 - y, cc)))]
        for cp in sends:
            cp.start()
        for cp in sends:
            cp.wait_recv()
        for cp in sends:
            cp.wait_send()

    return pl.pallas_call(
        body, name="scatter_second_hop", in_specs=[HBM], out_specs=HBM,
        out_shape=jax.ShapeDtypeStruct(to_nbrs.shape, to_nbrs.dtype),
        scratch_shapes=[pltpu.SemaphoreType.DMA((2,)), pltpu.SemaphoreType.DMA((2,))],
    )(to_nbrs)


def swap_sibling(mine):
    def body(src_ref, out_ref, send_sem, recv_sem):
        x, y, cc = _place()
        cp = pltpu.make_async_remote_copy(src_ref=src_ref, dst_ref=out_ref, send_sem=send_sem, recv_sem=recv_sem,
                                          device_id=(x, y, 1 - cc), device_id_type=MESH)
        cp.start()
        cp.wait()

    return pl.pallas_call(
        body, name="swap_sibling", in_specs=[HBM], out_specs=HBM,
        out_shape=jax.ShapeDtypeStruct(mine.shape, mine.dtype),
        scratch_shapes=[pltpu.SemaphoreType.DMA, pltpu.SemaphoreType.DMA],
    )(mine)


def join_halves(buf):
    r, c = buf.shape
    half_rows = r // 2

    def body(_, out_ref, send_sem, recv_sem):
        x, y, cc = _place()
        half = out_ref.at[pl.ds(cc * half_rows, half_rows)]
        other = out_ref.at[pl.ds((1 - cc) * half_rows, half_rows)]
        send = pltpu.make_async_remote_copy(src_ref=half, dst_ref=half, send_sem=send_sem, recv_sem=recv_sem,
                                            device_id=(x, y, 1 - cc), device_id_type=MESH)
        send.start()
        pltpu.make_async_remote_copy(src_ref=other, dst_ref=other, send_sem=send_sem, recv_sem=recv_sem,
                                     device_id=(x, y, 1 - cc), device_id_type=MESH).wait_recv()
        send.wait_send()

    return pl.pallas_call(
        body, name="join_halves", in_specs=[HBM], out_specs=HBM, input_output_aliases={0: 0},
        out_shape=jax.ShapeDtypeStruct(buf.shape, buf.dtype),
        scratch_shapes=[pltpu.SemaphoreType.DMA, pltpu.SemaphoreType.DMA],
    )(buf)


def exchange_all(v, name):
    r, c = v.shape

    def body(v_ref, out_ref, send_sems, recv_sems):
        x, y, cc = _place()
        out_ref[4 * x + 2 * y + cc] = v_ref[...]
        sends, recvs = [], []
        for k in range(1, 8):
            px = 1 - x if k & 4 else x
            py = 1 - y if k & 2 else y
            pc = 1 - cc if k & 1 else cc
            sends.append(pltpu.make_async_remote_copy(
                src_ref=v_ref, dst_ref=out_ref.at[4 * x + 2 * y + cc], send_sem=send_sems.at[k - 1],
                recv_sem=recv_sems.at[k - 1], device_id=(px, py, pc), device_id_type=MESH))
            recvs.append(pltpu.make_async_remote_copy(
                src_ref=v_ref, dst_ref=out_ref.at[4 * px + 2 * py + pc], send_sem=send_sems.at[k - 1],
                recv_sem=recv_sems.at[k - 1], device_id=(px, py, pc), device_id_type=MESH))
        for cp in sends:
            cp.start()
        for cp in recvs:
            cp.wait_recv()
        for cp in sends:
            cp.wait_send()

    vm = pl.BlockSpec(memory_space=pltpu.VMEM)
    return pl.pallas_call(
        body, name=name, in_specs=[vm], out_specs=vm, out_shape=jax.ShapeDtypeStruct((8, r, c), v.dtype),
        scratch_shapes=[pltpu.SemaphoreType.DMA((7,)), pltpu.SemaphoreType.DMA((7,))],
        compiler_params=pltpu.CompilerParams(vmem_limit_bytes=VMEM_LIMIT),
    )(v)


def _as_rows(a, width):
    n = math.prod(a.shape)
    if n % width == 0:
        return a.reshape(-1, width)
    return jnp.pad(a.reshape(1, -1), ((0, 0), (0, -n % width))).reshape(-1, width)


def _n_rows(shape, width):
    return -(-math.prod(shape) // width)


def _pack_rows(arrays, rows, width):
    parts = [_as_rows(a, width) for a in arrays]
    used = sum(p.shape[0] for p in parts)
    return jnp.concatenate(parts + [jnp.zeros((rows - used, width), arrays[0].dtype)], axis=0)


def _unpack_rows(pack, shapes):
    width = pack.shape[1]
    out, off = [], 0
    for shp in shapes:
        nr, n = _n_rows(shp, width), math.prod(shp)
        part = pack[off:off + nr]
        out.append(part.reshape(shp) if n % width == 0 else part.reshape(-1)[:n].reshape(shp))
        off += nr
    return out


def _rows_for(shapes, width, mult=8):
    n = sum(_n_rows(s, width) for s in shapes)
    return -(-n // mult) * mult


def _gdn_fwd(x, p):
    proj = mm(x, p["w_main"], name="gdn_proj")
    ba = mm(x, p["w_gate"], name="gdn_proj_gate")
    q, k, v = (gdn_pre(proj, p["conv"], kind) for kind in "qkv")
    beta_b, g_b = gdn_gates(ba, p["alog_row"], p["dt_row"])
    o2, st, tinv = gdn_rec_fwd(q, k, v, beta_b, g_b)
    y = mixer_post(o2, proj, p["norm_g"], A_DK, 3 * A_HEADS, "gdn_post")
    m = mm(y, p["w_out"], name="gdn_out")
    return m, (x, proj, ba, q, k, v, beta_b, g_b, o2, st, tinv, y)


def _gdn_bwd(saved, p, dm):
    x, proj, ba, q, k, v, beta_b, g_b, o2, st, tinv, y = saved
    d_w_out = mm(y, dm, "tn", name="gdn_dw_out")
    dy = mm(dm, p["w_out"], "nt", name="gdn_dy")
    do, dz, d_norm_g = mixer_post_bwd(o2, proj, p["norm_g"], dy, A_DK, 3 * A_HEADS, "gdn_post_bwd")
    dq2, dk2, dv2, dbb, dgb = gdn_rec_bwd(q, k, v, beta_b, g_b, st, tinv, do)
    dba, d_alog_row, d_dt_row = gdn_gates_bwd(ba, p["alog_row"], p["dt_row"], dbb, dgb)
    du, dconv = zip(*(gdn_pre_bwd(proj, p["conv"], d2, kind) for d2, kind in ((dq2, "q"), (dk2, "k"), (dv2, "v"))))
    dproj = jnp.concatenate(list(du) + [dz], axis=1)
    d_w_main = mm(x, dproj, "tn", name="gdn_dw_main")
    d_w_gate = mm(x, dba, "tn", name="gdn_dw_gate")
    dx = mm(dba, p["w_gate"], "nt", epi="add", extra=dm, alpha=ALPHA, name="gdn_dx_gate")
    dx = mm(dproj, p["w_main"], "nt", epi="add", extra=dx, name="gdn_dx")
    grads = dict(w_in=jnp.concatenate([d_w_main, d_w_gate[:, :2 * N_GATE]], axis=1), conv=jnp.concatenate(dconv, axis=1),
                 alog=d_alog_row[0, N_GATE:2 * N_GATE].reshape(2, A_HEADS), dt=d_dt_row[0, N_GATE:2 * N_GATE].reshape(2, A_HEADS),
                 norm_g=d_norm_g[0], w_out=d_w_out)
    return dx, grads


def _gla_fwd(x, p):
    proj = mm(x, p["w_main"], name="gla_proj")
    gl = mm(x, p["w_gate"], name="gla_proj_gate")
    log_a = gla_gate(gl, p["w2"], p["gate_b"])
    o2, st = gla_rec_fwd(proj, log_a)
    y = mixer_post(o2, proj, p["norm_g"], B_DV, (2 * B_KW + B_VW) // B_DV, "gla_post")
    m = mm(y, p["w_out"], name="gla_out")
    return m, (x, proj, gl, log_a, o2, st, y)


def _gla_bwd(saved, p, dm):
    x, proj, gl, log_a, o2, st, y = saved
    d_w_out = mm(y, dm, "tn", name="gla_dw_out")
    dy = mm(dm, p["w_out"], "nt", name="gla_dy")
    do, dr, d_norm_g = mixer_post_bwd(o2, proj, p["norm_g"], dy, B_DV, (2 * B_KW + B_VW) // B_DV, "gla_post_bwd")
    dq2, dk2, dv2, dla = gla_rec_bwd(proj, log_a, st, do)
    dgl, dz, d_b0, d_b1 = gla_gate_bwd(gl, p["w2"], p["gate_b"], dla)
    d_w2 = [mm(gl, dz[n], "tn", name="gla_dw_gate_w2") for n in range(2)]
    dproj = jnp.concatenate([dq2[0] + dq2[1], dk2[0] + dk2[1], dv2[0] + dv2[1], dr], axis=1)
    d_w_main = mm(x, dproj, "tn", name="gla_dw_main")
    d_w_gate = mm(x, dgl, "tn", name="gla_dw_gate")
    dx = mm(dgl, p["w_gate"], "nt", epi="add", extra=dm, alpha=ALPHA, name="gla_dx_gate")
    dx = mm(dproj, p["w_main"], "nt", epi="add", extra=dx, name="gla_dx")
    grads = dict(w_in=jnp.concatenate([d_w_main, d_w_gate[:, :2 * B_RANK]], axis=1),
                 gate_w2=jnp.stack([d_w2[n][n * B_RANK:(n + 1) * B_RANK] for n in range(2)]),
                 gate_b=jnp.concatenate([d_b0, d_b1]), norm_g=d_norm_g[0], w_out=d_w_out)
    return dx, grads


def _pad_cols(w, width=LANES):
    return jnp.pad(w, ((0, 0), (0, width - w.shape[1])))


def _local_step(x, target, a_w_in, a_conv, a_alog, a_dt_bias, a_norm_g, a_w_out, b_w_in, b_gate_w2, b_gate_b, b_norm_g,
                b_w_out, ln1_g, ln1_b, mlp_w1, mlp_w2, ln2_g, ln2_b, grad_pack=None):
    layer_p = []
    for i in range(DEPTH):
        j = i // 2
        if i % 2 == 0:
            layer_p.append(dict(
                w_main=a_w_in[j][:, :4 * A_W], w_gate=_pad_cols(a_w_in[j][:, 4 * A_W:]), conv=a_conv[j],
                alog_row=jnp.pad(a_alog[j].reshape(1, N_GATE), ((0, 0), (N_GATE, LANES - 2 * N_GATE))),
                dt_row=jnp.pad(a_dt_bias[j].reshape(1, N_GATE), ((0, 0), (N_GATE, LANES - 2 * N_GATE))),
                norm_g=a_norm_g[j].reshape(1, A_DK), w_out=a_w_out[j]))
        else:
            w2 = jnp.stack([jnp.pad(b_gate_w2[j][n], ((n * B_RANK, LANES - (n + 1) * B_RANK), (0, 0))) for n in range(2)])
            layer_p.append(dict(
                w_main=b_w_in[j][:, :2 * B_KW + 2 * B_VW], w_gate=_pad_cols(b_w_in[j][:, 2 * B_KW + 2 * B_VW:]),
                w2=w2, gate_b=b_gate_b[j].reshape(2, 1, B_KW), norm_g=b_norm_g[j].reshape(1, B_DV), w_out=b_w_out[j]))

    saved = []
    h = h_op = x
    for i in range(DEPTH):
        p = layer_p[i]
        m, sv = (_gdn_fwd if i % 2 == 0 else _gla_fwd)(h_op, p)
        x1, x1_op = ln_fwd(h, m, ln1_g[i:i + 1], ln1_b[i:i + 1])
        h1 = mm(x1_op, mlp_w1[i], name="mlp_up")
        mlp = mm(h1, mlp_w2[i], act="sqrelu", name="mlp_down")
        x2, x2_op = ln_fwd(x1, mlp, ln2_g[i:i + 1], ln2_b[i:i + 1])
        saved.append((sv, h, m, x1, x1_op, h1, mlp))
        h, h_op = x2, x2_op

    dh, loss_part = loss_head(h, target)

    g_a, g_b, g_ln1g, g_ln1b, g_ln2g, g_ln2b, g_w1, g_w2 = {}, {}, {}, {}, {}, {}, {}, {}
    pack = None
    for i in reversed(range(DEPTH)):
        sv, xin, m, x1, x1_op, h1, mlp = saved[i]
        p = layer_p[i]
        dr2, g_ln2g[i], g_ln2b[i] = ln_bwd(x1, mlp, ln2_g[i:i + 1], dh)
        if grad_pack is None:
            g_w2[i] = mm(h1, dr2, "tn", act="sqrelu", name="mlp_dw_down")
        else:
            pack = mm(h1, dr2, "tn", act="sqrelu", pack=(pack, grad_pack[0], grad_pack[2] + i, "rows"), name="mlp_dw_down")
        dh1 = mm(dr2, mlp_w2[i], "nt", epi="dsqrelu", extra=h1, name="mlp_dh")
        if grad_pack is None:
            g_w1[i] = mm(x1_op, dh1, "tn", chip_major=True, name="mlp_dw_up")
        else:
            pack = mm(x1_op, dh1, "tn", pack=(pack, grad_pack[0], grad_pack[1] + i, "cols"), name="mlp_dw_up")
        dx1 = mm(dh1, mlp_w1[i], "nt", epi="add", extra=dr2, alpha=ALPHA, name="mlp_dx")
        dr1, g_ln1g[i], g_ln1b[i] = ln_bwd(xin, m, ln1_g[i:i + 1], dx1)
        dh, g = (_gdn_bwd if i % 2 == 0 else _gla_bwd)(sv, p, dr1)
        (g_a if i % 2 == 0 else g_b)[i // 2] = g

    per_layer = lambda d, key=None: [(d[i] if key is None else d[i][key]) for i in sorted(d)]
    st = lambda d, key=None: jnp.stack(per_layer(d, key))
    grads = dict(
        a_w_in=per_layer(g_a, "w_in"), a_conv=st(g_a, "conv"), a_alog=st(g_a, "alog"), a_dt_bias=st(g_a, "dt"),
        a_norm_g=st(g_a, "norm_g"), a_w_out=per_layer(g_a, "w_out"), b_w_in=per_layer(g_b, "w_in"),
        b_gate_w2=st(g_b, "gate_w2"), b_gate_b=st(g_b, "gate_b"), b_norm_g=st(g_b, "norm_g"),
        b_w_out=per_layer(g_b, "w_out"), ln1_g=st(g_ln1g)[:, 0], ln1_b=st(g_ln1b)[:, 0], mlp_w1=per_layer(g_w1),
        mlp_w2=per_layer(g_w2), ln2_g=st(g_ln2g)[:, 0], ln2_b=st(g_ln2b)[:, 0], pack=pack)
    return loss_part, dh, grads


WEIGHTS = ("a_w_in", "a_conv", "a_alog", "a_dt_bias", "a_norm_g", "a_w_out", "b_w_in", "b_gate_w2", "b_gate_b",
           "b_norm_g", "b_w_out", "ln1_g", "ln1_b", "mlp_w1", "mlp_w2", "ln2_g", "ln2_b")
BIG = ("mlp_w1", "mlp_w2", "a_w_out", "b_w_out", "a_w_in", "b_w_in")
SHARD_AXIS = {"mlp_w1": 2, "mlp_w2": 1, "a_w_out": 1, "b_w_out": 1, "a_w_in": 2, "b_w_in": 2}
SMALL = tuple(n for n in WEIGHTS if n not in BIG)
SMALL_SHARD_AXIS = {"a_conv": 2, "b_gate_w2": 3, "b_gate_b": 2, "b_norm_g": 1}


def _to_chip_major(full, axis):
    shp = full.shape
    t = full.reshape(shp[:axis] + (4, shp[axis] // 4) + shp[axis + 1:])
    return jnp.moveaxis(t, axis, 0)


def _from_chip_major(stacked, axis):
    t = jnp.moveaxis(stacked, 0, axis)
    shp = t.shape
    return t.reshape(shp[:axis] + (shp[axis] * shp[axis + 1],) + shp[axis + 2:])


def kernel(x, a_w_in, a_conv, a_alog, a_dt_bias, a_norm_g, a_w_out, b_w_in, b_gate_w2, b_gate_b, b_norm_g, b_w_out, ln1_g, ln1_b, mlp_w1, mlp_w2, ln2_g, ln2_b, loss_target, m_a_w_in, m_a_conv, m_a_alog, m_a_dt_bias, m_a_norm_g, m_a_w_out, m_b_w_in, m_b_gate_w2, m_b_gate_b, m_b_norm_g, m_b_w_out, m_ln1_g, m_ln1_b, m_mlp_w1, m_mlp_w2, m_ln2_g, m_ln2_b, v_a_w_in, v_a_conv, v_a_alog, v_a_dt_bias, v_a_norm_g, v_a_w_out, v_b_w_in, v_b_gate_w2, v_b_gate_b, v_b_norm_g, v_b_w_out, v_ln1_g, v_ln1_b, v_mlp_w1, v_mlp_w2, v_ln2_g, v_ln2_b):
    w = dict(a_w_in=a_w_in, a_conv=a_conv, a_alog=a_alog, a_dt_bias=a_dt_bias, a_norm_g=a_norm_g, a_w_out=a_w_out,
             b_w_in=b_w_in, b_gate_w2=b_gate_w2, b_gate_b=b_gate_b, b_norm_g=b_norm_g, b_w_out=b_w_out, ln1_g=ln1_g,
             ln1_b=ln1_b, mlp_w1=mlp_w1, mlp_w2=mlp_w2, ln2_g=ln2_g, ln2_b=ln2_b)
    mom = dict(a_w_in=m_a_w_in, a_conv=m_a_conv, a_alog=m_a_alog, a_dt_bias=m_a_dt_bias, a_norm_g=m_a_norm_g,
               a_w_out=m_a_w_out, b_w_in=m_b_w_in, b_gate_w2=m_b_gate_w2, b_gate_b=m_b_gate_b, b_norm_g=m_b_norm_g,
               b_w_out=m_b_w_out, ln1_g=m_ln1_g, ln1_b=m_ln1_b, mlp_w1=m_mlp_w1, mlp_w2=m_mlp_w2, ln2_g=m_ln2_g,
               ln2_b=m_ln2_b)
    var = dict(a_w_in=v_a_w_in, a_conv=v_a_conv, a_alog=v_a_alog, a_dt_bias=v_a_dt_bias, a_norm_g=v_a_norm_g,
               a_w_out=v_a_w_out, b_w_in=v_b_w_in, b_gate_w2=v_b_gate_w2, b_gate_b=v_b_gate_b, b_norm_g=v_b_norm_g,
               b_w_out=v_b_w_out, ln1_g=v_ln1_g, ln1_b=v_ln1_b, mlp_w1=v_mlp_w1, mlp_w2=v_mlp_w2, ln2_g=v_ln2_g,
               ln2_b=v_ln2_b)
    chip = 2 * lax.axis_index("x") + lax.axis_index("y")

    seg_rows = [w[n].size // D_MODEL for n in BIG]
    seg_off = [sum(seg_rows[:i]) for i in range(len(BIG))]
    rows = -(-sum(seg_rows) // PACK_TILE) * PACK_TILE
    shard_pack = jnp.concatenate([w[n].reshape(-1, D_MODEL) for n in BIG]
                                 + [jnp.zeros((rows - sum(seg_rows), D_MODEL), F32)], axis=0)
    chip_idx = chip.astype(jnp.int32).reshape(1)
    gathered = allgather_chips(cast_into_slot(shard_pack, chip_idx))
    full = {}
    for n, off, nr in zip(BIG, seg_off, seg_rows):
        if n in ("mlp_w1", "mlp_w2"):
            kind = "cols" if SHARD_AXIS[n] == 2 else "rows"
            full[n] = [Gathered(gathered, off + i * D_MODEL, kind) for i in range(DEPTH)]
            continue
        stacked = gathered[:, off:off + nr].reshape((4,) + w[n].shape)
        full[n] = _from_chip_major(stacked, SHARD_AXIS[n])
    sharded_small = tuple(SMALL_SHARD_AXIS)
    sm_shapes = [w[n].shape for n in sharded_small]
    sm_rows = _rows_for(sm_shapes, LANES)
    sm_all = exchange_all(_pack_rows([w[n] for n in sharded_small], sm_rows, LANES), "gather_small")
    per_chip = [_unpack_rows(sm_all[2 * pch], sm_shapes) for pch in range(4)]
    for idx, n in enumerate(sharded_small):
        full[n] = jnp.concatenate([per_chip[pch][idx] for pch in range(4)], axis=SMALL_SHARD_AXIS[n])
    for n in WEIGHTS:
        full.setdefault(n, w[n])

    blk_of = {n: off // D_MODEL for n, off in zip(BIG, seg_off)}
    loss_part, grad_x, grads = _local_step(x[0], loss_target[0], *[full[n] for n in WEIGHTS],
                                           grad_pack=(rows, blk_of["mlp_w1"], blk_of["mlp_w2"]))
    loss = lax.psum(jnp.sum(loss_part), ("x", "y", "c"))

    gpack = grads["pack"]
    rest = jnp.concatenate(
        [_to_chip_major(g, SHARD_AXIS[n] - 1).reshape(4, -1, D_MODEL) for n in BIG[2:] for g in grads[n]]
        + [jnp.zeros((4, rows - sum(seg_rows), D_MODEL), F32)], axis=1)
    gpack = lax.dynamic_update_slice(gpack, rest, (0, seg_off[2], 0))
    core = lax.axis_index("c").astype(jnp.int32).reshape(1)
    theirs = swap_sibling(half_to_bf16(gpack, 1 - core))
    chip_sum = half_to_bf16(gpack, core, theirs)
    ax, ay = lax.axis_index("x"), lax.axis_index("y")
    slot_x = (2 * (1 - ax) + ay).astype(jnp.int32).reshape(1)
    slot_y = (2 * ax + (1 - ay)).astype(jnp.int32).reshape(1)
    to_nbrs = merge_first_hop(chip_sum, scatter_first_hop(chip_sum), slot_x, slot_y, core)
    reduced = join_halves(sum_received(chip_sum, scatter_second_hop(to_nbrs), chip_idx, core))
    out_g, out_d, out_m, out_v = {}, {}, {}, {}
    for n, off, nr in zip(BIG, seg_off, seg_rows):
        if w[n].shape[-1] == D_MODEL:
            view = lambda t: t.reshape(-1, D_MODEL)
            res = adamw(view(w[n]), view(mom[n]), view(var[n]), (reduced,), off, "adamw_" + n)
        else:
            cols = w[n].shape[-1]
            view = lambda t: t.reshape(-1, cols)
            res = adamw(view(w[n]), view(mom[n]), view(var[n]), (view(reduced[off:off + nr]),), 0, "adamw_" + n)
        out_g[n], out_d[n], out_m[n], out_v[n] = (t.reshape(w[n].shape) for t in res)

    all_shapes = [full[n].shape for n in SMALL]
    g_rows = _rows_for(all_shapes, LANES)
    g_all = exchange_all(_pack_rows([grads[n] for n in SMALL], g_rows, LANES), "gather_small_grads")
    g_sum = _unpack_rows(sum_slots(g_all, "sum_small_grads"), all_shapes)
    g_mine = []
    for n, g in zip(SMALL, g_sum):
        if n in SMALL_SHARD_AXIS:
            ax = SMALL_SHARD_AXIS[n]
            g = lax.dynamic_slice_in_dim(g, chip * w[n].shape[ax], w[n].shape[ax], axis=ax)
        g_mine.append(g)
    my_shapes = [w[n].shape for n in SMALL]
    s_rows = _rows_for(my_shapes, LANES)
    pk = lambda d: _pack_rows([d[n] for n in SMALL], s_rows, LANES)
    res = adamw(pk(w), pk(mom), pk(var), (_pack_rows(g_mine, s_rows, LANES),), 0, "adamw_small")
    for dst, pack in zip((out_g, out_d, out_m, out_v), res):
        for n, t in zip(SMALL, _unpack_rows(pack, my_shapes)):
            dst[n] = t

    return (loss, grad_x[None], *[out_g[n] for n in WEIGHTS], *[out_d[n] for n in WEIGHTS],
            *[out_m[n] for n in WEIGHTS], *[out_v[n] for n in WEIGHTS])
```

```python
import functools
import math

import jax
import jax.numpy as jnp
from jax import lax
from jax.experimental import pallas as pl
from jax.experimental.pallas import tpu as pltpu

F32 = jnp.float32
BF16 = jnp.bfloat16

D_MODEL = 1024
DEPTH = 4
CHUNK = 64
A_HEADS = 8
A_DK = 128
A_W = 1024
A_CONV = 5
B_HEADS = 4
B_DK = 128
B_DV = 256
B_RANK = 16
B_TAU = 16.0
B_KW = 512
B_VW = 1024
ALPHA = (2 * DEPTH) ** 0.25
LN_EPS = 1e-5
RMS_EPS = 1e-6
L2_EPS = 1e-6
ADAM_LR = 0.001
ADAM_B1 = 0.9
ADAM_B2 = 0.999
ADAM_EPS = 1e-08
ADAM_WD = 0.01
ADAM_STEP = 10
LANES = 128
NEG_INF = float("-inf")
VMEM_LIMIT = 56 * 1024 * 1024


def _cparams(sem=None):
    return pltpu.CompilerParams(dimension_semantics=sem, vmem_limit_bytes=VMEM_LIMIT)


def _dg(a, b, ca, cb):
    return lax.dot_general(a.astype(BF16), b.astype(BF16), (((ca,), (cb,)), ((), ())),
                           preferred_element_type=F32)


def _split(x):
    hi = x.astype(BF16)
    return hi, (x - hi.astype(F32)).astype(BF16)


def _dg3(a, b, ca, cb):
    (a1, a2), (b1, b2) = _split(a), _split(b)
    return (_dg(a1, b2, ca, cb) + _dg(a2, b1, ca, cb)) + _dg(a1, b1, ca, cb)


def _dot_with_vjp(dg):
    @functools.partial(jax.custom_vjp, nondiff_argnums=(2, 3))
    def dot(a, b, ca, cb):
        return dg(a, b, ca, cb)

    def fwd(a, b, ca, cb):
        return dg(a, b, ca, cb), (a, b)

    def bwd(ca, cb, res, g):
        a, b = res
        da = dg(g, b, 1, 1 - cb) if ca == 1 else dg(b, g, 1 - cb, 1)
        db = dg(a, g, 1 - ca, 0) if cb == 0 else dg(g, a, 0, 1 - ca)
        return da, db

    dot.defvjp(fwd, bwd)
    return dot


bdot = _dot_with_vjp(_dg)
xdot3 = _dot_with_vjp(_dg3)


def nn(a, b):
    return bdot(a, b, 1, 0)


def nt(a, b):
    return bdot(a, b, 1, 1)


def tn(a, b):
    return bdot(a, b, 0, 0)


def xdot(a, b):
    return xdot3(a, b, 1, 0)


def _sigmoid(x):
    return 1.0 / (1.0 + jnp.exp(-x))


def _softplus(x):
    return jnp.maximum(x, 0.0) + jnp.log(1.0 + jnp.exp(-jnp.abs(x)))


def _chunk_masks(rev):
    ii = lax.broadcasted_iota(jnp.int32, (CHUNK, CHUNK), 0)
    jj = lax.broadcasted_iota(jnp.int32, (CHUNK, CHUNK), 1)
    d = (ii - jj) * (1 - 2 * rev)
    return d >= 0, d > 0, ii == jj, (ii >> 3) == (jj >> 3)


def _each(f, *lists):
    return [f(*xs) for xs in zip(*lists)]


@jax.custom_vjp
def _unit_triangular_inverse(a, ident, blockdiag):
    return _unit_triangular_inverse_impl(a, ident, blockdiag)


def _unit_triangular_inverse_fwd(a, ident, blockdiag):
    t = _unit_triangular_inverse_impl(a, ident, blockdiag)
    return t, (t, ident, blockdiag)


def _unit_triangular_inverse_bwd(res, g):
    t, ident, blockdiag = res
    left = _each(lambda x, y: xdot3(x, y, 0, 0), t, g)
    da = _each(lambda x, y: -xdot3(x, y, 1, 1), left, t)
    return da, jnp.zeros_like(ident), jnp.zeros_like(blockdiag)


_unit_triangular_inverse.defvjp(_unit_triangular_inverse_fwd, _unit_triangular_inverse_bwd)


@jax.custom_vjp
def _known_inverse(a, t):
    return t


def _known_inverse_bwd(t, g):
    left = _each(lambda x, y: xdot3(x, y, 0, 0), t, g)
    return _each(lambda x, y: -xdot3(x, y, 1, 1), left, t), _each(jnp.zeros_like, t)


_known_inverse.defvjp(lambda a, t: (t, t), _known_inverse_bwd)


def _unit_triangular_inverse_impl(a, ident, blockdiag):
    ad = _each(lambda x: x * blockdiag, a)
    e = _each(lambda x, y: x - y, a, ad)
    dinv = _each(lambda x: ident - x, ad)
    p = _each(xdot, ad, ad)
    dinv = _each(lambda x, y: x + xdot(x, y), dinv, p)
    p = _each(xdot, p, p)
    dinv = _each(lambda x, y: x + xdot(x, y), dinv, p)
    g = _each(lambda x, y: -xdot(x, y), dinv, e)
    finv = _each(lambda x: ident + x, g)
    p = _each(xdot, g, g)
    finv = _each(lambda x, y: x + xdot(x, y), finv, p)
    p = _each(xdot, p, p)
    finv = _each(lambda x, y: x + xdot(x, y), finv, p)
    return _each(xdot, finv, dinv)


def _gdn_step(state, q, k, v, bb, gb, rev, t_saved=None):
    causal, strict, eye, blockdiag = _chunk_masks(rev)
    lower = causal.astype(F32)
    ones = jnp.ones((CHUNK, CHUNK), F32)
    gcb = _each(lambda x: xdot(lower, x), gb)
    gcol = _each(lambda x: x[:, :CHUNK], gcb)
    grow = _each(lambda x: xdot(ones, jnp.where(eye, x, 0.0)), gcol)
    decay = _each(lambda x, y: jnp.exp(jnp.where(causal, x - y, NEG_INF)), gcol, grow)
    kb = _each(lambda x, y: x * y, k, bb)
    a = _each(lambda x, y, z: jnp.where(strict, nt(x, y) * z, 0.0), kb, k, decay)
    if t_saved is None:
        t = _unit_triangular_inverse(a, eye.astype(F32), blockdiag.astype(F32))
    else:
        t = _known_inverse(a, t_saved)
    egc = _each(jnp.exp, gcb)
    u = _each(lambda x, y, z: xdot(x, y * z), t, v, bb)
    w = _each(lambda x, y, z: xdot(x, y * z), t, kb, egc)
    qk = _each(lambda x, y, z: nt(x, y) * z, q, k, decay)
    glast = _each(lambda x: jnp.sum(x, axis=0, keepdims=True), gb)
    v_new = _each(lambda x, y, z: x - nn(y, z), u, w, state)
    o = _each(lambda x, y, z, p, r: nn(x * y, z) + nn(p, r), q, egc, state, qk, v_new)
    k_dec = _each(lambda x, y, z: x * jnp.exp(y - z), k, glast, gcb)
    state_new = _each(lambda x, y, z, p: x * jnp.exp(y) + tn(z, p), state, glast, k_dec, v_new)
    return state_new, o, t


def _gla_step(state_t, q, k, v, la, rev):
    causal, _, _, _ = _chunk_masks(rev)
    lower = causal.astype(F32)
    sign = 1 - 2 * rev
    b = _each(lambda x: xdot(lower, x), la)
    q = _each(lambda x: x * (B_DK ** -0.5), q)
    row = lax.broadcasted_iota(jnp.int32, (CHUNK, B_DK), 0)
    sub = row // GLA_SUB
    parts = []
    for blk in range(CHUNK // GLA_SUB):
        rows = slice(blk * GLA_SUB, (blk + 1) * GLA_SUB)
        r_at = jnp.where(rev == 1, GLA_SUB * (blk + 1), GLA_SUB * blk - 1)
        r = _each(lambda x: jnp.sum(jnp.where(row == r_at, x, 0.0), axis=0, keepdims=True), b)
        q_blk = _each(lambda x, y, z: x[rows] * jnp.exp(y[rows] - z), q, b, r)
        k_past = _each(lambda x, y, z: x * jnp.exp(jnp.where((sub - blk) * sign < 0, z - y, NEG_INF)), k, b, r)
        parts.append(_each(lambda x, y: xdot3(x, y, 1, 1), q_blk, k_past))
    scores = _each(lambda *p: jnp.concatenate(p, axis=0), *parts)
    shp = (GLA_SUB, GLA_SUB, B_DK)
    d3 = (lax.broadcasted_iota(jnp.int32, shp, 0) - lax.broadcasted_iota(jnp.int32, shp, 1)) * sign
    place_r = lax.broadcasted_iota(jnp.int32, (GLA_SUB, CHUNK), 0)
    place_c = lax.broadcasted_iota(jnp.int32, (GLA_SUB, CHUNK), 1)
    diag = []
    for blk in range(CHUNK // GLA_SUB):
        rows = slice(blk * GLA_SUB, (blk + 1) * GLA_SUB)
        place = (place_c == place_r + blk * GLA_SUB).astype(F32)

        def pairs(qh, kh, bh):
            qb, kb, bb = qh[rows], kh[rows], bh[rows]
            dec = jnp.exp(jnp.where(d3 >= 0, bb[:, None, :] - bb[None, :, :], NEG_INF))
            return xdot(jnp.sum(qb[:, None, :] * kb[None, :, :] * dec, axis=-1), place)

        diag.append(_each(pairs, q, k, b))
    scores = _each(lambda x, *d: x + jnp.concatenate(d, axis=0), scores, *diag)
    blast = _each(lambda x: jnp.sum(x, axis=0, keepdims=True), la)
    o = _each(lambda x, y, z, s, w: nt(x * jnp.exp(y), z) + nn(s, w), q, b, state_t, scores, v)
    k_dec = _each(lambda x, y, z: x * jnp.exp(y - z), k, blast, b)
    state_new = _each(lambda x, y, z, w: jnp.exp(x) * y + tn(z, w), blast, state_t, v, k_dec)
    return state_new, o


def _chunk_pos(d, m, n):
    return m + d * (n - 1 - 2 * m)


GLA_SUB = 16
GDN_HEADS_PER_STEP = 8
def gdn_rec_fwd(q, k, v, beta_b, g_b):
    s = q.shape[0]
    n = s // CHUNK

    hb = GDN_HEADS_PER_STEP
    wide = hb * LANES

    def body(q_ref, k_ref, v_ref, bb_ref, gb_ref, o_ref, st_ref, t_ref, state):
        d = pl.program_id(0)

        @pl.when(pl.program_id(2) == 0)
        def _():
            state[...] = jnp.zeros_like(state)

        cols = [slice(hh * LANES, (hh + 1) * LANES) for hh in range(hb)]
        st = [state[hh] for hh in range(hb)]
        new, o, t = _gdn_step(st, *([r[:, c] for c in cols] for r in (q_ref, k_ref, v_ref, bb_ref, gb_ref)), d)
        for hh, c in enumerate(cols):
            st_ref[hh] = st[hh]
            t_ref[hh] = t[hh]
            state[hh] = new[hh]
            o_ref[:, c] = o[hh]

    blk = pl.BlockSpec((CHUNK, wide), lambda d, h, m: (_chunk_pos(d, m, n), h))
    gate = pl.BlockSpec((CHUNK, wide), lambda d, h, m: (_chunk_pos(d, m, n), d * (A_HEADS // hb) + h))
    return pl.pallas_call(
        body, name="gdn_rec_fwd", grid=(2, A_HEADS // hb, n),
        in_specs=[blk, blk, blk, gate, gate],
        out_specs=[pl.BlockSpec((None, CHUNK, wide), lambda d, h, m: (d, _chunk_pos(d, m, n), h)),
                   pl.BlockSpec((None, hb, None, A_DK, LANES), lambda d, h, m: (d, h, _chunk_pos(d, m, n), 0, 0)),
                   pl.BlockSpec((None, hb, None, CHUNK, CHUNK), lambda d, h, m: (d, h, _chunk_pos(d, m, n), 0, 0))],
        out_shape=[jax.ShapeDtypeStruct((2, s, A_W), F32), jax.ShapeDtypeStruct((2, A_HEADS, n, A_DK, LANES), F32),
                   jax.ShapeDtypeStruct((2, A_HEADS, n, CHUNK, CHUNK), F32)],
        scratch_shapes=[pltpu.VMEM((hb, A_DK, LANES), F32)],
        compiler_params=_cparams(("arbitrary", "arbitrary", "arbitrary")),
    )(q, k, v, beta_b, g_b)


def gdn_rec_bwd(q, k, v, beta_b, g_b, states, tinv, do):
    s = q.shape[0]
    n = s // CHUNK

    hb = GDN_HEADS_PER_STEP
    wide = hb * LANES

    def body(q_ref, k_ref, v_ref, bb_ref, gb_ref, st_ref, t_ref, do_ref, dq_ref, dk_ref, dv_ref, dbb_ref, dgb_ref, dstate):
        d = pl.program_id(0)

        @pl.when(pl.program_id(2) == 0)
        def _():
            dstate[...] = jnp.zeros_like(dstate)

        def step(*a):
            return _gdn_step(*a, d, t_saved=[t_ref[hh] for hh in range(hb)])[:2]

        cols = [slice(hh * LANES, (hh + 1) * LANES) for hh in range(hb)]
        _, vjp = jax.vjp(step, [st_ref[hh] for hh in range(hb)],
                         *([r[:, c] for c in cols] for r in (q_ref, k_ref, v_ref, bb_ref, gb_ref)))
        grads = vjp(([dstate[hh] for hh in range(hb)], [do_ref[:, c] for c in cols]))
        for hh, c in enumerate(cols):
            dstate[hh], dq_ref[:, c], dk_ref[:, c], dv_ref[:, c], dbb_ref[:, c], dgb_ref[:, c] = (g[hh] for g in grads)

    pos = lambda d, m: _chunk_pos(1 - d, m, n)
    blk = pl.BlockSpec((CHUNK, wide), lambda d, h, m: (pos(d, m), h))
    gate = pl.BlockSpec((CHUNK, wide), lambda d, h, m: (pos(d, m), d * (A_HEADS // hb) + h))
    oblk = pl.BlockSpec((None, CHUNK, wide), lambda d, h, m: (d, pos(d, m), h))
    return pl.pallas_call(
        body, name="gdn_rec_bwd", grid=(2, A_HEADS // hb, n),
        in_specs=[blk, blk, blk, gate, gate,
                  pl.BlockSpec((None, hb, None, A_DK, LANES), lambda d, h, m: (d, h, pos(d, m), 0, 0)),
                  pl.BlockSpec((None, hb, None, CHUNK, CHUNK), lambda d, h, m: (d, h, pos(d, m), 0, 0)), blk],
        out_specs=[oblk, oblk, oblk, gate, gate],
        out_shape=[jax.ShapeDtypeStruct((2, s, A_W), F32)] * 3 + [jax.ShapeDtypeStruct(beta_b.shape, F32)] * 2,
        scratch_shapes=[pltpu.VMEM((hb, A_DK, LANES), F32)],
        compiler_params=_cparams(("arbitrary", "arbitrary", "arbitrary")),
    )(q, k, v, beta_b, g_b, states, tinv, do)


def gla_rec_fwd(proj, log_a):
    s = proj.shape[0]
    n = s // CHUNK

    kcols = [slice(h * B_DK, (h + 1) * B_DK) for h in range(B_HEADS)]
    vcols = [slice(h * B_DV, (h + 1) * B_DV) for h in range(B_HEADS)]

    def body(q_ref, k_ref, v_ref, la_ref, o_ref, st_ref, state):
        d = pl.program_id(0)

        @pl.when(pl.program_id(1) == 0)
        def _():
            state[...] = jnp.zeros_like(state)

        st = [state[h] for h in range(B_HEADS)]
        new, o = _gla_step(st, [q_ref[:, c] for c in kcols], [k_ref[:, c] for c in kcols], [v_ref[:, c] for c in vcols],
                           [la_ref[:, c] for c in kcols], d)
        for h in range(B_HEADS):
            st_ref[h] = st[h]
            state[h] = new[h]
            o_ref[:, vcols[h]] = o[h]

    pos = lambda d, m: _chunk_pos(d, m, n)
    return pl.pallas_call(
        body, name="gla_rec_fwd", grid=(2, n),
        in_specs=[pl.BlockSpec((CHUNK, B_KW), lambda d, m: (pos(d, m), 0)),
                  pl.BlockSpec((CHUNK, B_KW), lambda d, m: (pos(d, m), 1)),
                  pl.BlockSpec((CHUNK, B_VW), lambda d, m: (pos(d, m), 2 * B_KW // B_VW)),
                  pl.BlockSpec((None, CHUNK, B_KW), lambda d, m: (d, pos(d, m), 0))],
        out_specs=[pl.BlockSpec((None, CHUNK, B_VW), lambda d, m: (d, pos(d, m), 0)),
                   pl.BlockSpec((None, B_HEADS, None, B_DV, B_DK), lambda d, m: (d, 0, pos(d, m), 0, 0))],
        out_shape=[jax.ShapeDtypeStruct((2, s, B_VW), F32), jax.ShapeDtypeStruct((2, B_HEADS, n, B_DV, B_DK), F32)],
        scratch_shapes=[pltpu.VMEM((B_HEADS, B_DV, B_DK), F32)],
        compiler_params=_cparams(("arbitrary", "arbitrary")),
    )(proj, proj, proj, log_a)


def gla_rec_bwd(proj, log_a, states, do):
    s = proj.shape[0]
    n = s // CHUNK

    kcols = [slice(h * B_DK, (h + 1) * B_DK) for h in range(B_HEADS)]
    vcols = [slice(h * B_DV, (h + 1) * B_DV) for h in range(B_HEADS)]

    def body(q_ref, k_ref, v_ref, la_ref, st_ref, do_ref, dq_ref, dk_ref, dv_ref, dla_ref, dstate):
        d = pl.program_id(0)

        @pl.when(pl.program_id(1) == 0)
        def _():
            dstate[...] = jnp.zeros_like(dstate)

        step = functools.partial(_gla_step, rev=d)
        _, vjp = jax.vjp(step, [st_ref[h] for h in range(B_HEADS)], [q_ref[:, c] for c in kcols],
                         [k_ref[:, c] for c in kcols], [v_ref[:, c] for c in vcols], [la_ref[:, c] for c in kcols])
        dst, dq, dk, dv, dla = vjp(([dstate[h] for h in range(B_HEADS)], [do_ref[:, c] for c in vcols]))
        for h in range(B_HEADS):
            dstate[h] = dst[h]
            dq_ref[:, kcols[h]] = dq[h]
            dk_ref[:, kcols[h]] = dk[h]
            dv_ref[:, vcols[h]] = dv[h]
            dla_ref[:, kcols[h]] = dla[h]

    pos = lambda d, m: _chunk_pos(1 - d, m, n)
    kblk = pl.BlockSpec((None, CHUNK, B_KW), lambda d, m: (d, pos(d, m), 0))
    return pl.pallas_call(
        body, name="gla_rec_bwd", grid=(2, n),
        in_specs=[pl.BlockSpec((CHUNK, B_KW), lambda d, m: (pos(d, m), 0)),
                  pl.BlockSpec((CHUNK, B_KW), lambda d, m: (pos(d, m), 1)),
                  pl.BlockSpec((CHUNK, B_VW), lambda d, m: (pos(d, m), 2 * B_KW // B_VW)),
                  kblk,
                  pl.BlockSpec((None, B_HEADS, None, B_DV, B_DK), lambda d, m: (d, 0, pos(d, m), 0, 0)),
                  pl.BlockSpec((CHUNK, B_VW), lambda d, m: (pos(d, m), 0))],
        out_specs=[kblk, kblk, pl.BlockSpec((None, CHUNK, B_VW), lambda d, m: (d, pos(d, m), 0)), kblk],
        out_shape=[jax.ShapeDtypeStruct((2, s, B_KW), F32), jax.ShapeDtypeStruct((2, s, B_KW), F32),
                   jax.ShapeDtypeStruct((2, s, B_VW), F32), jax.ShapeDtypeStruct((2, s, B_KW), F32)],
        scratch_shapes=[pltpu.VMEM((B_HEADS, B_DV, B_DK), F32)],
        compiler_params=_cparams(("arbitrary", "arbitrary")),
    )(proj, proj, proj, log_a, states, do)


MM_TILE_OUT = 1024
MM_TILE_K = 1024


def _tile(n, pref):
    return pref if n % pref == 0 else n


class Gathered:
    def __init__(self, g, off, kind):
        assert off % D_MODEL == 0 and MM_TILE_OUT == D_MODEL and MM_TILE_K == D_MODEL
        self.g, self.blk, self.kind = g, off // D_MODEL, kind
        self.shape = (D_MODEL, 4 * D_MODEL) if kind == "cols" else (4 * D_MODEL, D_MODEL)

    def spec(self, mode):
        blk = self.blk
        chip_is_k = (self.kind == "rows") == (mode == "nn")
        if chip_is_k:
            return pl.BlockSpec((None, D_MODEL, D_MODEL), lambda i, j, k: (k, blk, 0))
        return pl.BlockSpec((None, D_MODEL, D_MODEL), lambda i, j, k: (j, blk, 0))


def mm(a, b, mode="nn", act=None, epi=None, extra=None, alpha=1.0, chip_major=False, pack=None, act_out=False,
       name="mm"):
    if mode == "tn":
        kk, m = a.shape
    else:
        m, kk = a.shape
    nn_ = b.shape[0] if mode == "nt" else b.shape[1]
    tm, tn_, tk = _tile(m, MM_TILE_OUT), _tile(nn_, MM_TILE_OUT), _tile(kk, MM_TILE_K)
    nk = kk // tk
    ca, cb = {"nn": (1, 0), "nt": (1, 1), "tn": (0, 0)}[mode]

    def body(*refs):
        o_ref = refs[-2] if act_out else refs[-1]
        a_ref, b_ref = refs[:2]
        if epi is not None:
            e_ref = refs[2]
        kstep = pl.program_id(2)
        at = a_ref[...]
        if act == "sqrelu":
            at = jnp.square(jnp.maximum(at, 0.0))
        part = _dg(at, b_ref[...], ca, cb)
        if act_out:
            refs[-1][...] = jnp.square(jnp.maximum(part, 0.0)).astype(BF16)

        @pl.when(kstep == 0)
        def _():
            o_ref[...] = part

        @pl.when(kstep > 0)
        def _():
            o_ref[...] += part

        if epi is not None:
            @pl.when(kstep == nk - 1)
            def _():
                if epi == "dsqrelu":
                    o_ref[...] = o_ref[...] * (2.0 * jnp.maximum(e_ref[...], 0.0))
                else:
                    o_ref[...] = o_ref[...] + alpha * e_ref[...]

    a_spec = pl.BlockSpec((tk, tm), lambda i, j, k: (k, i)) if mode == "tn" else pl.BlockSpec((tm, tk), lambda i, j, k: (i, k))
    if isinstance(b, Gathered):
        assert mode in ("nn", "nt") and tn_ == D_MODEL and tk == D_MODEL
        b_spec, b = b.spec(mode), b.g
    elif mode == "nt":
        b_spec = pl.BlockSpec((tn_, tk), lambda i, j, k: (j, k))
    else:
        b_spec = pl.BlockSpec((tk, tn_), lambda i, j, k: (k, j))
    o_spec = pl.BlockSpec((tm, tn_), lambda i, j, k: (i, j))
    ins, specs = [a, b], [a_spec, b_spec]
    if epi is not None:
        ins.append(extra)
        specs.append(o_spec)
    out_shape = jax.ShapeDtypeStruct((m, nn_), F32)
    if chip_major:
        assert nn_ == 4 * D_MODEL and tn_ == D_MODEL
        o_spec = pl.BlockSpec((None, tm, D_MODEL), lambda i, j, k: (j, i, 0))
        out_shape = jax.ShapeDtypeStruct((4, m, D_MODEL), F32)
    aliases = {}
    if pack is not None:
        buf, rows, blk, kind = pack
        logical = (D_MODEL, 4 * D_MODEL) if kind == "cols" else (4 * D_MODEL, D_MODEL)
        assert epi is None and tm == D_MODEL and tn_ == D_MODEL and (m, nn_) == logical
        if kind == "cols":
            o_spec = pl.BlockSpec((None, D_MODEL, D_MODEL), lambda i, j, k: (j, blk, 0))
        else:
            o_spec = pl.BlockSpec((None, D_MODEL, D_MODEL), lambda i, j, k: (i, blk, 0))
        out_shape = jax.ShapeDtypeStruct((4, rows, D_MODEL), F32)
        if buf is not None:
            aliases = {len(ins): 0}
            ins.append(buf)
            specs.append(pl.BlockSpec(memory_space=pl.ANY))
    if act_out:
        assert nk == 1 and epi is None and pack is None and not chip_major
        o_spec, out_shape = [o_spec, o_spec], [out_shape, jax.ShapeDtypeStruct((m, nn_), BF16)]
    return pl.pallas_call(
        body, name=name, grid=(m // tm, nn_ // tn_, nk), in_specs=specs, out_specs=o_spec, out_shape=out_shape,
        input_output_aliases=aliases, compiler_params=_cparams(("parallel", "parallel", "arbitrary")),
    )(*ins)


ROWS = 512
POST_ROWS = 2048
LN_ROWS = 512


def _ln_core(x, m, g, b):
    r = ALPHA * x + m
    mu = jnp.mean(r, axis=-1, keepdims=True)
    xc = r - mu
    var = jnp.mean(xc * xc, axis=-1, keepdims=True)
    rstd = lax.rsqrt(var + LN_EPS)
    xhat = xc * rstd
    return xhat * g + b, xhat, rstd


def ln_fwd(x, m, g, b):
    s, dm = x.shape
    rows = _tile(s, LN_ROWS)

    def body(x_ref, m_ref, g_ref, b_ref, o_ref, ob_ref):
        y = _ln_core(x_ref[...], m_ref[...], g_ref[...], b_ref[...])[0]
        o_ref[...] = y
        ob_ref[...] = y.astype(BF16)

    row = pl.BlockSpec((rows, dm), lambda i: (i, 0))
    vec = pl.BlockSpec((1, dm), lambda i: (0, 0))
    return pl.pallas_call(body, name="ln_fwd", grid=(s // rows,), in_specs=[row, row, vec, vec], out_specs=[row, row],
                          out_shape=[jax.ShapeDtypeStruct((s, dm), F32), jax.ShapeDtypeStruct((s, dm), BF16)],
                          compiler_params=_cparams(("parallel",)))(x, m, g, b)


def ln_bwd(x, m, g, dy):
    s, dm = x.shape
    rows = _tile(s, LN_ROWS)

    def body(x_ref, m_ref, g_ref, dy_ref, dr_ref, dg_ref, db_ref):
        gv = g_ref[...]
        _, xhat, rstd = _ln_core(x_ref[...], m_ref[...], gv, jnp.zeros_like(gv))
        dy = dy_ref[...]
        dxh = dy * gv
        dr_ref[...] = rstd * (dxh - jnp.mean(dxh, axis=-1, keepdims=True)
                              - xhat * jnp.mean(dxh * xhat, axis=-1, keepdims=True))

        @pl.when(pl.program_id(0) == 0)
        def _():
            dg_ref[...] = jnp.zeros_like(dg_ref)
            db_ref[...] = jnp.zeros_like(db_ref)

        dg_ref[...] += jnp.sum(dy * xhat, axis=0, keepdims=True)
        db_ref[...] += jnp.sum(dy, axis=0, keepdims=True)

    row = pl.BlockSpec((rows, dm), lambda i: (i, 0))
    vec = pl.BlockSpec((1, dm), lambda i: (0, 0))
    return pl.pallas_call(body, name="ln_bwd", grid=(s // rows,), in_specs=[row, row, vec, row], out_specs=[row, vec, vec],
                          out_shape=[jax.ShapeDtypeStruct((s, dm), F32), jax.ShapeDtypeStruct((1, dm), F32),
                                     jax.ShapeDtypeStruct((1, dm), F32)],
                          compiler_params=_cparams(("arbitrary",)))(x, m, g, dy)


def loss_head(y, target):
    s, dm = y.shape
    rows = _tile(s, LN_ROWS)

    def body(y_ref, t_ref, dy_ref, l_ref):
        e = y_ref[...] - t_ref[...]
        dy_ref[...] = e * (1.0 / dm)

        @pl.when(pl.program_id(0) == 0)
        def _():
            l_ref[...] = jnp.zeros_like(l_ref)

        col = jnp.sum(e * e, axis=0, keepdims=True) * (0.5 / dm)
        acc = col[:, :LANES]
        for c in range(1, dm // LANES):
            acc = acc + col[:, c * LANES:(c + 1) * LANES]
        l_ref[...] += acc

    row = pl.BlockSpec((rows, dm), lambda i: (i, 0))
    return pl.pallas_call(body, name="loss_head", grid=(s // rows,), in_specs=[row, row],
                          out_specs=[row, pl.BlockSpec((1, LANES), lambda i: (0, 0))],
                          out_shape=[jax.ShapeDtypeStruct((s, dm), F32), jax.ShapeDtypeStruct((1, LANES), F32)],
                          compiler_params=_cparams(("arbitrary",)))(y, target)


def _shift_rows_impl(x, d):
    n = x.shape[0]
    if d == 0:
        return x
    t = lax.broadcasted_iota(jnp.int32, x.shape, 0)
    return jnp.where((t + d >= 0) & (t + d < n), pltpu.roll(x, (-d) % n, 0), 0.0)


@functools.partial(jax.custom_vjp, nondiff_argnums=(1,))
def _shift_rows(x, d):
    return _shift_rows_impl(x, d)


_shift_rows.defvjp(lambda x, d: (_shift_rows_impl(x, d), None), lambda d, _, g: (_shift_rows_impl(g, -d),))


def _gdn_pre_fn(u, w, kind):
    rows = lax.broadcasted_iota(jnp.int32, w.shape, 0)
    c = None
    for tap in range(A_CONV):
        w_tap = jnp.sum(jnp.where(rows == tap, w, 0.0), axis=0, keepdims=True)
        term = _shift_rows(u, tap - A_CONV // 2) * w_tap
        c = term if c is None else c + term
    y = c * _sigmoid(c)
    if kind == "v":
        return y
    y = y * lax.rsqrt(jnp.sum(y * y, axis=-1, keepdims=True) + L2_EPS)
    return y * (A_DK ** -0.5) if kind == "q" else y


_KIND_OFF = {"q": 0, "k": A_HEADS, "v": 2 * A_HEADS}


def gdn_pre(proj, conv_w, kind):
    s = proj.shape[0]
    off = _KIND_OFF[kind]

    def body(u_ref, w_ref, o_ref):
        o_ref[...] = _gdn_pre_fn(u_ref[...], w_ref[...], kind)

    return pl.pallas_call(
        body, name="gdn_pre_" + kind, grid=(A_HEADS,),
        in_specs=[pl.BlockSpec((s, LANES), lambda h: (0, off + h)), pl.BlockSpec((A_CONV, LANES), lambda h: (0, off + h))],
        out_specs=pl.BlockSpec((s, LANES), lambda h: (0, h)),
        out_shape=jax.ShapeDtypeStruct((s, A_W), F32), compiler_params=_cparams(("parallel",)))(proj, conv_w)


def gdn_pre_bwd(proj, conv_w, dt2, kind):
    s = proj.shape[0]
    off = _KIND_OFF[kind]

    def body(u_ref, w_ref, d0_ref, d1_ref, du_ref, dw_ref):
        _, vjp = jax.vjp(functools.partial(_gdn_pre_fn, kind=kind), u_ref[...], w_ref[...])
        du, dw = vjp(d0_ref[...] + d1_ref[...])
        du_ref[...] = du
        dw_ref[...] = dw

    return pl.pallas_call(
        body, name="gdn_pre_bwd_" + kind, grid=(A_HEADS,),
        in_specs=[pl.BlockSpec((s, LANES), lambda h: (0, off + h)), pl.BlockSpec((A_CONV, LANES), lambda h: (0, off + h)),
                  pl.BlockSpec((None, s, LANES), lambda h: (0, 0, h)), pl.BlockSpec((None, s, LANES), lambda h: (1, 0, h))],
        out_specs=[pl.BlockSpec((s, LANES), lambda h: (0, h)), pl.BlockSpec((A_CONV, LANES), lambda h: (0, h))],
        out_shape=[jax.ShapeDtypeStruct((s, A_W), F32), jax.ShapeDtypeStruct((A_CONV, A_W), F32)],
        compiler_params=_cparams(("parallel",)))(proj, conv_w, dt2, dt2)


N_GATE = 2 * A_HEADS


def _gdn_gates_fn(ba, alog_row, dt_row):
    r = lax.broadcasted_iota(jnp.int32, (LANES, N_GATE * LANES), 0)
    c = lax.broadcasted_iota(jnp.int32, (LANES, N_GATE * LANES), 1) >> 7
    beta_b = xdot(_sigmoid(ba), (r == c).astype(F32))
    g = -(jnp.exp(alog_row) * _softplus(ba + dt_row))
    g_b = xdot(g, (r == c + N_GATE).astype(F32))
    return beta_b, g_b


def gdn_gates(ba, alog_row, dt_row):
    s = ba.shape[0]

    def body(ba_ref, al_ref, dt_ref, bb_ref, gb_ref):
        bb_ref[...], gb_ref[...] = _gdn_gates_fn(ba_ref[...], al_ref[...], dt_ref[...])

    row = pl.BlockSpec((ROWS, LANES), lambda i: (i, 0))
    vec = pl.BlockSpec((1, LANES), lambda i: (0, 0))
    wide = pl.BlockSpec((ROWS, N_GATE * LANES), lambda i: (i, 0))
    return pl.pallas_call(body, name="gdn_gates", grid=(s // ROWS,), in_specs=[row, vec, vec], out_specs=[wide, wide],
                          out_shape=[jax.ShapeDtypeStruct((s, N_GATE * LANES), F32)] * 2,
                          compiler_params=_cparams(("parallel",)))(ba, alog_row, dt_row)


def gdn_gates_bwd(ba, alog_row, dt_row, dbeta_b, dg_b):
    s = ba.shape[0]

    def body(ba_ref, al_ref, dt_ref, dbb_ref, dgb_ref, dba_ref, dal_ref, ddt_ref):
        _, vjp = jax.vjp(_gdn_gates_fn, ba_ref[...], al_ref[...], dt_ref[...])
        dba, dal, ddt = vjp((dbb_ref[...], dgb_ref[...]))
        dba_ref[...] = dba

        @pl.when(pl.program_id(0) == 0)
        def _():
            dal_ref[...] = jnp.zeros_like(dal_ref)
            ddt_ref[...] = jnp.zeros_like(ddt_ref)

        dal_ref[...] += dal
        ddt_ref[...] += ddt

    row = pl.BlockSpec((ROWS, LANES), lambda i: (i, 0))
    vec = pl.BlockSpec((1, LANES), lambda i: (0, 0))
    wide = pl.BlockSpec((ROWS, N_GATE * LANES), lambda i: (i, 0))
    return pl.pallas_call(body, name="gdn_gates_bwd", grid=(s // ROWS,), in_specs=[row, vec, vec, wide, wide],
                          out_specs=[row, vec, vec],
                          out_shape=[jax.ShapeDtypeStruct((s, LANES), F32), jax.ShapeDtypeStruct((1, LANES), F32),
                                     jax.ShapeDtypeStruct((1, LANES), F32)],
                          compiler_params=_cparams(("arbitrary",)))(ba, alog_row, dt_row, dbeta_b, dg_b)


def _post_fn(o, z, g):
    y = o * lax.rsqrt(jnp.mean(o * o, axis=-1, keepdims=True) + RMS_EPS) * g
    return y * (z * _sigmoid(z))


def mixer_post(o2, proj, norm_g, width, gate_off, name):
    s = o2.shape[1]
    nh = o2.shape[2] // width

    rows = _tile(s, POST_ROWS)

    def body(o0_ref, o1_ref, z_ref, g_ref, y_ref):
        y_ref[...] = _post_fn(o0_ref[...] + o1_ref[...], z_ref[...], g_ref[...])

    ospec = lambda d: pl.BlockSpec((None, rows, width), lambda i, h: (d, i, h))
    return pl.pallas_call(
        body, name=name, grid=(s // rows, nh),
        in_specs=[ospec(0), ospec(1), pl.BlockSpec((rows, width), lambda i, h: (i, gate_off + h)),
                  pl.BlockSpec((1, width), lambda i, h: (0, 0))],
        out_specs=pl.BlockSpec((rows, width), lambda i, h: (i, h)),
        out_shape=jax.ShapeDtypeStruct((s, o2.shape[2]), F32),
        compiler_params=_cparams(("parallel", "parallel")))(o2, o2, proj, norm_g)


def mixer_post_bwd(o2, proj, norm_g, dy, width, gate_off, name):
    s = o2.shape[1]
    nh = o2.shape[2] // width

    def body(o0_ref, o1_ref, z_ref, g_ref, dy_ref, do_ref, dz_ref, dg_ref):
        _, vjp = jax.vjp(_post_fn, o0_ref[...] + o1_ref[...], z_ref[...], g_ref[...])
        do, dz, dg = vjp(dy_ref[...])
        do_ref[...] = do
        dz_ref[...] = dz

        @pl.when((pl.program_id(0) == 0) & (pl.program_id(1) == 0))
        def _():
            dg_ref[...] = jnp.zeros_like(dg_ref)

        dg_ref[...] += dg

    rows = _tile(s, POST_ROWS)
    ospec = lambda d: pl.BlockSpec((None, rows, width), lambda i, h: (d, i, h))
    blk = pl.BlockSpec((rows, width), lambda i, h: (i, h))
    vec = pl.BlockSpec((1, width), lambda i, h: (0, 0))
    return pl.pallas_call(
        body, name=name, grid=(s // rows, nh),
        in_specs=[ospec(0), ospec(1), pl.BlockSpec((rows, width), lambda i, h: (i, gate_off + h)), vec, blk],
        out_specs=[blk, blk, vec],
        out_shape=[jax.ShapeDtypeStruct((s, o2.shape[2]), F32)] * 2 + [jax.ShapeDtypeStruct((1, width), F32)],
        compiler_params=_cparams(("arbitrary", "arbitrary")))(o2, o2, proj, norm_g, dy)


def _log_gate(z):
    return (jnp.minimum(z, 0.0) - jnp.log(1.0 + jnp.exp(-jnp.abs(z)))) * (1.0 / B_TAU)


def gla_gate(gl, w2, gb):
    s = gl.shape[0]

    def body(gl_ref, w_ref, b_ref, o_ref):
        for n in range(2):
            o_ref[n] = _log_gate(nn(gl_ref[...], w_ref[n]) + b_ref[n])

    full = lambda shp: pl.BlockSpec(shp, lambda i: (0,) * len(shp))
    return pl.pallas_call(
        body, name="gla_gate", grid=(s // ROWS,),
        in_specs=[pl.BlockSpec((ROWS, LANES), lambda i: (i, 0)), full(w2.shape), full(gb.shape)],
        out_specs=pl.BlockSpec((2, ROWS, B_KW), lambda i: (0, i, 0)),
        out_shape=jax.ShapeDtypeStruct((2, s, B_KW), F32), compiler_params=_cparams(("parallel",)))(gl, w2, gb)


def gla_gate_bwd(gl, w2, gb, dla):
    s = gl.shape[0]

    def body(gl_ref, w_ref, b_ref, dla_ref, dgl_ref, dz_ref, db0_ref, db1_ref):
        @pl.when(pl.program_id(0) == 0)
        def _():
            db0_ref[...] = jnp.zeros_like(db0_ref)
            db1_ref[...] = jnp.zeros_like(db1_ref)

        dgl = None
        for n, db_ref in enumerate((db0_ref, db1_ref)):
            _, vjp = jax.vjp(_log_gate, nn(gl_ref[...], w_ref[n]) + b_ref[n])
            dz, = vjp(dla_ref[n])
            dz_ref[n] = dz
            db_ref[...] += jnp.sum(dz, axis=0, keepdims=True)
            part = nt(dz, w_ref[n])
            dgl = part if dgl is None else dgl + part
        dgl_ref[...] = dgl

    full = lambda shp: pl.BlockSpec(shp, lambda i: (0,) * len(shp))
    row = pl.BlockSpec((ROWS, LANES), lambda i: (i, 0))
    wide = pl.BlockSpec((2, ROWS, B_KW), lambda i: (0, i, 0))
    vec = pl.BlockSpec((1, B_KW), lambda i: (0, 0))
    return pl.pallas_call(
        body, name="gla_gate_bwd", grid=(s // ROWS,),
        in_specs=[row, full(w2.shape), full(gb.shape), wide],
        out_specs=[row, wide, vec, vec],
        out_shape=[jax.ShapeDtypeStruct((s, LANES), F32), jax.ShapeDtypeStruct((2, s, B_KW), F32),
                   jax.ShapeDtypeStruct((1, B_KW), F32), jax.ShapeDtypeStruct((1, B_KW), F32)],
        compiler_params=_cparams(("arbitrary",)))(gl, w2, gb, dla)


PACK_TILE = 512


def cast_into_slot(x, chip):
    r, c = x.shape

    def body(chip_ref, x_ref, o_ref):
        o_ref[...] = x_ref[...].astype(BF16)

    return pl.pallas_call(
        body, name="cast_into_slot",
        grid_spec=pltpu.PrefetchScalarGridSpec(
            num_scalar_prefetch=1, grid=(r // PACK_TILE,),
            in_specs=[pl.BlockSpec((PACK_TILE, c), lambda i, chip_ref: (i, 0))],
            out_specs=pl.BlockSpec((None, PACK_TILE, c), lambda i, chip_ref: (chip_ref[0], i, 0))),
        out_shape=jax.ShapeDtypeStruct((4, r, c), BF16), compiler_params=_cparams(("parallel",)))(chip, x)


def sum_received(chip_sum, recv, chip, core):
    _, h, c = chip_sum.shape
    n = recv.shape[0]
    tr = _tile(h, PACK_TILE)
    nblk = h // tr

    def body(chip_ref, core_ref, own_ref, r_ref, o_ref):
        acc = r_ref[0].astype(F32)
        for k in range(1, n):
            acc = acc + r_ref[k].astype(F32)
        o_ref[...] = acc + own_ref[...].astype(F32)

    return pl.pallas_call(
        body, name="sum_received",
        grid_spec=pltpu.PrefetchScalarGridSpec(
            num_scalar_prefetch=2, grid=(nblk,),
            in_specs=[pl.BlockSpec((None, tr, c), lambda i, chip_ref, core_ref: (chip_ref[0], i, 0)),
                      pl.BlockSpec((n, tr, c), lambda i, chip_ref, core_ref: (0, i, 0))],
            out_specs=pl.BlockSpec((tr, c), lambda i, chip_ref, core_ref: (core_ref[0] * nblk + i, 0))),
        out_shape=jax.ShapeDtypeStruct((2 * h, c), F32), compiler_params=_cparams(("parallel",)))(chip, core, chip_sum, recv)


def merge_first_hop(chip_sum, passed, slot_x, slot_y, core):
    _, h, c = chip_sum.shape
    tr = _tile(h, PACK_TILE)

    def body(sx_ref, sy_ref, core_ref, to_x_ref, to_y_ref, p_ref, o_ref):
        p = p_ref[...].astype(F32)
        is_y = core_ref[0].astype(F32)
        o_ref[0] = (to_x_ref[...].astype(F32) + p * (1.0 - is_y)).astype(BF16)
        o_ref[1] = (to_y_ref[...].astype(F32) + p * is_y).astype(BF16)

    return pl.pallas_call(
        body, name="merge_first_hop",
        grid_spec=pltpu.PrefetchScalarGridSpec(
            num_scalar_prefetch=3, grid=(h // tr,),
            in_specs=[pl.BlockSpec((None, tr, c), lambda i, sx_ref, sy_ref, core_ref: (sx_ref[0], i, 0)),
                      pl.BlockSpec((None, tr, c), lambda i, sx_ref, sy_ref, core_ref: (sy_ref[0], i, 0)),
                      pl.BlockSpec((tr, c), lambda i, sx_ref, sy_ref, core_ref: (i, 0))],
            out_specs=pl.BlockSpec((2, tr, c), lambda i, sx_ref, sy_ref, core_ref: (0, i, 0))),
        out_shape=jax.ShapeDtypeStruct((2, h, c), BF16),
        compiler_params=_cparams(("parallel",)))(slot_x, slot_y, core, chip_sum, chip_sum, passed)


def sum_slots(x, name):
    n, r, c = x.shape
    tr = _tile(r, PACK_TILE)

    def body(x_ref, o_ref):
        acc = x_ref[0].astype(F32)
        for k in range(1, n):
            acc = acc + x_ref[k].astype(F32)
        o_ref[...] = acc

    return pl.pallas_call(body, name=name, grid=(r // tr,), in_specs=[pl.BlockSpec((n, tr, c), lambda i: (0, i, 0))],
                          out_specs=pl.BlockSpec((tr, c), lambda i: (i, 0)),
                          out_shape=jax.ShapeDtypeStruct((r, c), F32), compiler_params=_cparams(("parallel",)))(x)


def half_to_bf16(gpack, which, theirs=None):
    n, r, c = gpack.shape
    half_rows = r // 2
    tr = _tile(half_rows, PACK_TILE)
    nblk = half_rows // tr

    def body(which_ref, g_ref, *rest):
        o_ref = rest[-1]
        acc = g_ref[...]
        if theirs is not None:
            acc = acc + rest[0][...].astype(F32)
        o_ref[...] = acc.astype(BF16)

    blk = pl.BlockSpec((None, tr, c), lambda s, i, which_ref: (s, i, 0))
    ins = [gpack] if theirs is None else [gpack, theirs]
    return pl.pallas_call(
        body, name="half_to_bf16" if theirs is None else "add_sibling_half",
        grid_spec=pltpu.PrefetchScalarGridSpec(
            num_scalar_prefetch=1, grid=(n, nblk),
            in_specs=[pl.BlockSpec((None, tr, c), lambda s, i, which_ref: (s, which_ref[0] * nblk + i, 0))]
            + [blk] * (len(ins) - 1),
            out_specs=blk),
        out_shape=jax.ShapeDtypeStruct((n, half_rows, c), BF16),
        compiler_params=_cparams(("parallel", "parallel")))(which, *ins)


def adamw(w, m, v, grads, g_row_off, name):
    r, c = w.shape
    tr = next(t for t in (PACK_TILE, r) if r % t == 0 and g_row_off % t == 0)
    ob = g_row_off // tr
    ng = len(grads)

    def body(*refs):
        w_ref, m_ref, v_ref = refs[:3]
        g_refs = refs[3:3 + ng]
        g_ref, d_ref, nm_ref, nv_ref = refs[3 + ng:]
        g = g_refs[0][...]
        for gr in g_refs[1:]:
            g = g + gr[...]
        m_new = ADAM_B1 * m_ref[...] + (1.0 - ADAM_B1) * g
        v_new = ADAM_B2 * v_ref[...] + (1.0 - ADAM_B2) * jnp.square(g)
        m_hat = m_new / (1.0 - ADAM_B1 ** ADAM_STEP)
        v_hat = v_new / (1.0 - ADAM_B2 ** ADAM_STEP)
        g_ref[...] = g
        d_ref[...] = -ADAM_LR * (m_hat / (jnp.sqrt(v_hat) + ADAM_EPS) + ADAM_WD * w_ref[...])
        nm_ref[...] = m_new
        nv_ref[...] = v_new

    blk = pl.BlockSpec((tr, c), lambda i: (i, 0))
    gblk = pl.BlockSpec((tr, c), lambda i: (i + ob, 0))
    return pl.pallas_call(body, name=name, grid=(r // tr,), in_specs=[blk, blk, blk] + [gblk] * ng, out_specs=[blk] * 4,
                          out_shape=[jax.ShapeDtypeStruct((r, c), F32)] * 4,
                          compiler_params=_cparams(("parallel",)))(w, m, v, *grads)


MESH = pl.DeviceIdType.MESH
HBM = pl.BlockSpec(memory_space=pl.ANY)


def _place():
    return lax.axis_index("x"), lax.axis_index("y"), lax.axis_index("c")


def allgather_chips(buf):
    _, r, c = buf.shape
    half_rows = r // 2

    def body(_, out_ref, send_sems, recv_sems):
        x, y, cc = _place()
        half = pl.ds(cc * half_rows, half_rows)
        other = pl.ds((1 - cc) * half_rows, half_rows)

        def copy(k, rows, to):
            return pltpu.make_async_remote_copy(src_ref=rows, dst_ref=rows, send_sem=send_sems.at[k],
                                                recv_sem=recv_sems.at[k], device_id=to, device_id_type=MESH)

        nbr_x, nbr_y, diag = (1 - x, y), (x, 1 - y), (1 - x, 1 - y)
        slot = lambda chip: 2 * chip[0] + chip[1]
        sibling = (x, y, 1 - cc)
        first = [copy(0, out_ref.at[slot((x, y)), half], (*nbr_x, cc)), copy(1, out_ref.at[slot((x, y)), half], (*nbr_y, cc))]
        for cp in first:
            cp.start()
        passed = []
        for k, chip in enumerate((nbr_x, nbr_y)):
            landed = out_ref.at[slot(chip), half]
            copy(k, landed, (*chip, cc)).wait_recv()
            passed.append(copy(3 + k, landed, sibling))
            passed[-1].start()
        via = (1 - x + cc * (2 * x - 1), y + cc * (1 - 2 * y))
        to = (x + cc * (1 - 2 * x), 1 - y + cc * (2 * y - 1))
        hop = copy(2, out_ref.at[slot(via), half], (*to, cc))
        hop.start()
        landed = out_ref.at[slot(diag), half]
        copy(2, landed, (*to, cc)).wait_recv()
        passed.append(copy(5, landed, sibling))
        passed[-1].start()
        for k, chip in enumerate((nbr_x, nbr_y, diag)):
            copy(3 + k, out_ref.at[slot(chip), other], sibling).wait_recv()
        for cp in first + [hop] + passed:
            cp.wait_send()

    return pl.pallas_call(
        body, name="allgather_chips", in_specs=[HBM], out_specs=HBM, input_output_aliases={0: 0},
        out_shape=jax.ShapeDtypeStruct(buf.shape, buf.dtype),
        scratch_shapes=[pltpu.SemaphoreType.DMA((6,)), pltpu.SemaphoreType.DMA((6,))],
    )(buf)


def scatter_first_hop(gpack):
    _, r, c = gpack.shape

    def body(src_ref, out_ref, send_sem, recv_sem):
        x, y, cc = _place()
        to = (x + cc * (1 - 2 * x), 1 - y + cc * (2 * y - 1), cc)
        cp = pltpu.make_async_remote_copy(src_ref=src_ref.at[2 * (1 - x) + (1 - y)], dst_ref=out_ref, send_sem=send_sem,
                                          recv_sem=recv_sem, device_id=to, device_id_type=MESH)
        cp.start()
        cp.wait()

    return pl.pallas_call(
        body, name="scatter_first_hop", in_specs=[HBM], out_specs=HBM,
        out_shape=jax.ShapeDtypeStruct((r, c), gpack.dtype),
        scratch_shapes=[pltpu.SemaphoreType.DMA, pltpu.SemaphoreType.DMA],
    )(gpack)


def scatter_second_hop(to_nbrs):
    def body(src_ref, out_ref, send_sems, recv_sems):
        x, y, cc = _place()
        sends = [pltpu.make_async_remote_copy(src_ref=src_ref.at[k], dst_ref=out_ref.at[k], send_sem=send_sems.at[k],
                                              recv_sem=recv_sems.at[k], device_id=to, device_id_type=MESH)
                 for k, to in enumerate(((1 - x, y, cc), (x, 1 - y, cc)))]
        for cp in sends:
            cp.start()
        for cp in sends:
            cp.wait_recv()
        for cp in sends:
            cp.wait_send()

    return pl.pallas_call(
        body, name="scatter_second_hop", in_specs=[HBM], out_specs=HBM,
        out_shape=jax.ShapeDtypeStruct(to_nbrs.shape, to_nbrs.dtype),
        scratch_shapes=[pltpu.SemaphoreType.DMA((2,)), pltpu.SemaphoreType.DMA((2,))],
    )(to_nbrs)


def swap_sibling(mine):
    def body(src_ref, out_ref, send_sem, recv_sem):
        x, y, cc = _place()
        cp = pltpu.make_async_remote_copy(src_ref=src_ref, dst_ref=out_ref, send_sem=send_sem, recv_sem=recv_sem,
                                          device_id=(x, y, 1 - cc), device_id_type=MESH)
        cp.start()
        cp.wait()

    return pl.pallas_call(
        body, name="swap_sibling", in_specs=[HBM], out_specs=HBM,
        out_shape=jax.ShapeDtypeStruct(mine.shape, mine.dtype),
        scratch_shapes=[pltpu.SemaphoreType.DMA, pltpu.SemaphoreType.DMA],
    )(mine)


def join_halves(buf):
    r, c = buf.shape
    half_rows = r // 2

    def body(_, out_ref, send_sem, recv_sem):
        x, y, cc = _place()
        half = out_ref.at[pl.ds(cc * half_rows, half_rows)]
        other = out_ref.at[pl.ds((1 - cc) * half_rows, half_rows)]
        send = pltpu.make_async_remote_copy(src_ref=half, dst_ref=half, send_sem=send_sem, recv_sem=recv_sem,
                                            device_id=(x, y, 1 - cc), device_id_type=MESH)
        send.start()
        pltpu.make_async_remote_copy(src_ref=other, dst_ref=other, send_sem=send_sem, recv_sem=recv_sem,
                                     device_id=(x, y, 1 - cc), device_id_type=MESH).wait_recv()
        send.wait_send()

    return pl.pallas_call(
        body, name="join_halves", in_specs=[HBM], out_specs=HBM, input_output_aliases={0: 0},
        out_shape=jax.ShapeDtypeStruct(buf.shape, buf.dtype),
        scratch_shapes=[pltpu.SemaphoreType.DMA, pltpu.SemaphoreType.DMA],
    )(buf)


def exchange_all(v, name):
    r, c = v.shape

    def body(v_ref, out_ref, send_sems, recv_sems):
        x, y, cc = _place()
        out_ref[4 * x + 2 * y + cc] = v_ref[...]
        sends, recvs = [], []
        for k in range(1, 8):
            px = 1 - x if k & 4 else x
            py = 1 - y if k & 2 else y
            pc = 1 - cc if k & 1 else cc
            sends.append(pltpu.make_async_remote_copy(
                src_ref=v_ref, dst_ref=out_ref.at[4 * x + 2 * y + cc], send_sem=send_sems.at[k - 1],
                recv_sem=recv_sems.at[k - 1], device_id=(px, py, pc), device_id_type=MESH))
            recvs.append(pltpu.make_async_remote_copy(
                src_ref=v_ref, dst_ref=out_ref.at[4 * px + 2 * py + pc], send_sem=send_sems.at[k - 1],
                recv_sem=recv_sems.at[k - 1], device_id=(px, py, pc), device_id_type=MESH))
        for cp in sends:
            cp.start()
        for cp in recvs:
            cp.wait_recv()
        for cp in sends:
            cp.wait_send()

    vm = pl.BlockSpec(memory_space=pltpu.VMEM)
    return pl.pallas_call(
        body, name=name, in_specs=[vm], out_specs=vm, out_shape=jax.ShapeDtypeStruct((8, r, c), v.dtype),
        scratch_shapes=[pltpu.SemaphoreType.DMA((7,)), pltpu.SemaphoreType.DMA((7,))],
        compiler_params=pltpu.CompilerParams(vmem_limit_bytes=VMEM_LIMIT),
    )(v)


def _as_rows(a, width):
    n = math.prod(a.shape)
    if n % width == 0:
        return a.reshape(-1, width)
    return jnp.pad(a.reshape(1, -1), ((0, 0), (0, -n % width))).reshape(-1, width)


def _n_rows(shape, width):
    return -(-math.prod(shape) // width)


def _pack_rows(arrays, rows, width):
    parts = [_as_rows(a, width) for a in arrays]
    used = sum(p.shape[0] for p in parts)
    return jnp.concatenate(parts + [jnp.zeros((rows - used, width), arrays[0].dtype)], axis=0)


def _unpack_rows(pack, shapes):
    width = pack.shape[1]
    out, off = [], 0
    for shp in shapes:
        nr, n = _n_rows(shp, width), math.prod(shp)
        part = pack[off:off + nr]
        out.append(part.reshape(shp) if n % width == 0 else part.reshape(-1)[:n].reshape(shp))
        off += nr
    return out


def _rows_for(shapes, width, mult=8):
    n = sum(_n_rows(s, width) for s in shapes)
    return -(-n // mult) * mult


def _gdn_fwd(x, p):
    proj = mm(x, p["w_main"], name="gdn_proj")
    ba = mm(x, p["w_gate"], name="gdn_proj_gate")
    q, k, v = (gdn_pre(proj, p["conv"], kind) for kind in "qkv")
    beta_b, g_b = gdn_gates(ba, p["alog_row"], p["dt_row"])
    o2, st, tinv = gdn_rec_fwd(q, k, v, beta_b, g_b)
    y = mixer_post(o2, proj, p["norm_g"], A_DK, 3 * A_HEADS, "gdn_post")
    m = mm(y, p["w_out"], name="gdn_out")
    return m, (x, proj, ba, q, k, v, beta_b, g_b, o2, st, tinv, y)


def _gdn_bwd(saved, p, dm):
    x, proj, ba, q, k, v, beta_b, g_b, o2, st, tinv, y = saved
    d_w_out = mm(y, dm, "tn", name="gdn_dw_out")
    dy = mm(dm, p["w_out"], "nt", name="gdn_dy")
    do, dz, d_norm_g = mixer_post_bwd(o2, proj, p["norm_g"], dy, A_DK, 3 * A_HEADS, "gdn_post_bwd")
    dq2, dk2, dv2, dbb, dgb = gdn_rec_bwd(q, k, v, beta_b, g_b, st, tinv, do)
    dba, d_alog_row, d_dt_row = gdn_gates_bwd(ba, p["alog_row"], p["dt_row"], dbb, dgb)
    du, dconv = zip(*(gdn_pre_bwd(proj, p["conv"], d2, kind) for d2, kind in ((dq2, "q"), (dk2, "k"), (dv2, "v"))))
    dproj = jnp.concatenate(list(du) + [dz], axis=1)
    d_w_main = mm(x, dproj, "tn", name="gdn_dw_main")
    d_w_gate = mm(x, dba, "tn", name="gdn_dw_gate")
    dx = mm(dba, p["w_gate"], "nt", epi="add", extra=dm, alpha=ALPHA, name="gdn_dx_gate")
    dx = mm(dproj, p["w_main"], "nt", epi="add", extra=dx, name="gdn_dx")
    grads = dict(w_in=jnp.concatenate([d_w_main, d_w_gate[:, :2 * N_GATE]], axis=1), conv=jnp.concatenate(dconv, axis=1),
                 alog=d_alog_row[0, N_GATE:2 * N_GATE].reshape(2, A_HEADS), dt=d_dt_row[0, N_GATE:2 * N_GATE].reshape(2, A_HEADS),
                 norm_g=d_norm_g[0], w_out=d_w_out)
    return dx, grads


def _gla_fwd(x, p):
    proj = mm(x, p["w_main"], name="gla_proj")
    gl = mm(x, p["w_gate"], name="gla_proj_gate")
    log_a = gla_gate(gl, p["w2"], p["gate_b"])
    o2, st = gla_rec_fwd(proj, log_a)
    y = mixer_post(o2, proj, p["norm_g"], B_DV, (2 * B_KW + B_VW) // B_DV, "gla_post")
    m = mm(y, p["w_out"], name="gla_out")
    return m, (x, proj, gl, log_a, o2, st, y)


def _gla_bwd(saved, p, dm):
    x, proj, gl, log_a, o2, st, y = saved
    d_w_out = mm(y, dm, "tn", name="gla_dw_out")
    dy = mm(dm, p["w_out"], "nt", name="gla_dy")
    do, dr, d_norm_g = mixer_post_bwd(o2, proj, p["norm_g"], dy, B_DV, (2 * B_KW + B_VW) // B_DV, "gla_post_bwd")
    dq2, dk2, dv2, dla = gla_rec_bwd(proj, log_a, st, do)
    dgl, dz, d_b0, d_b1 = gla_gate_bwd(gl, p["w2"], p["gate_b"], dla)
    d_w2 = [mm(gl, dz[n], "tn", name="gla_dw_gate_w2") for n in range(2)]
    dproj = jnp.concatenate([dq2[0] + dq2[1], dk2[0] + dk2[1], dv2[0] + dv2[1], dr], axis=1)
    d_w_main = mm(x, dproj, "tn", name="gla_dw_main")
    d_w_gate = mm(x, dgl, "tn", name="gla_dw_gate")
    dx = mm(dgl, p["w_gate"], "nt", epi="add", extra=dm, alpha=ALPHA, name="gla_dx_gate")
    dx = mm(dproj, p["w_main"], "nt", epi="add", extra=dx, name="gla_dx")
    grads = dict(w_in=jnp.concatenate([d_w_main, d_w_gate[:, :2 * B_RANK]], axis=1),
                 gate_w2=jnp.stack([d_w2[n][n * B_RANK:(n + 1) * B_RANK] for n in range(2)]),
                 gate_b=jnp.concatenate([d_b0, d_b1]), norm_g=d_norm_g[0], w_out=d_w_out)
    return dx, grads


def _pad_cols(w, width=LANES):
    return jnp.pad(w, ((0, 0), (0, width - w.shape[1])))


def _local_step(x, target, a_w_in, a_conv, a_alog, a_dt_bias, a_norm_g, a_w_out, b_w_in, b_gate_w2, b_gate_b, b_norm_g,
                b_w_out, ln1_g, ln1_b, mlp_w1, mlp_w2, ln2_g, ln2_b, grad_pack=None):
    layer_p = []
    for i in range(DEPTH):
        j = i // 2
        if i % 2 == 0:
            layer_p.append(dict(
                w_main=a_w_in[j][:, :4 * A_W], w_gate=_pad_cols(a_w_in[j][:, 4 * A_W:]), conv=a_conv[j],
                alog_row=jnp.pad(a_alog[j].reshape(1, N_GATE), ((0, 0), (N_GATE, LANES - 2 * N_GATE))),
                dt_row=jnp.pad(a_dt_bias[j].reshape(1, N_GATE), ((0, 0), (N_GATE, LANES - 2 * N_GATE))),
                norm_g=a_norm_g[j].reshape(1, A_DK), w_out=a_w_out[j]))
        else:
            w2 = jnp.stack([jnp.pad(b_gate_w2[j][n], ((n * B_RANK, LANES - (n + 1) * B_RANK), (0, 0))) for n in range(2)])
            layer_p.append(dict(
                w_main=b_w_in[j][:, :2 * B_KW + 2 * B_VW], w_gate=_pad_cols(b_w_in[j][:, 2 * B_KW + 2 * B_VW:]),
                w2=w2, gate_b=b_gate_b[j].reshape(2, 1, B_KW), norm_g=b_norm_g[j].reshape(1, B_DV), w_out=b_w_out[j]))

    saved = []
    h = h_op = x
    for i in range(DEPTH):
        p = layer_p[i]
        m, sv = (_gdn_fwd if i % 2 == 0 else _gla_fwd)(h_op, p)
        x1, x1_op = ln_fwd(h, m, ln1_g[i:i + 1], ln1_b[i:i + 1])
        h1, act_op = mm(x1_op, mlp_w1[i], act_out=True, name="mlp_up")
        mlp = mm(act_op, mlp_w2[i], name="mlp_down")
        x2, x2_op = ln_fwd(x1, mlp, ln2_g[i:i + 1], ln2_b[i:i + 1])
        saved.append((sv, h, m, x1, x1_op, h1, act_op, mlp))
        h, h_op = x2, x2_op

    dh, loss_part = loss_head(h, target)

    g_a, g_b, g_ln1g, g_ln1b, g_ln2g, g_ln2b, g_w1, g_w2 = {}, {}, {}, {}, {}, {}, {}, {}
    pack = None
    for i in reversed(range(DEPTH)):
        sv, xin, m, x1, x1_op, h1, act_op, mlp = saved[i]
        p = layer_p[i]
        dr2, g_ln2g[i], g_ln2b[i] = ln_bwd(x1, mlp, ln2_g[i:i + 1], dh)
        if grad_pack is None:
            g_w2[i] = mm(act_op, dr2, "tn", name="mlp_dw_down")
        else:
            pack = mm(act_op, dr2, "tn", pack=(pack, grad_pack[0], grad_pack[2] + i, "rows"), name="mlp_dw_down")
        dh1 = mm(dr2, mlp_w2[i], "nt", epi="dsqrelu", extra=h1, name="mlp_dh")
        if grad_pack is None:
            g_w1[i] = mm(x1_op, dh1, "tn", chip_major=True, name="mlp_dw_up")
        else:
            pack = mm(x1_op, dh1, "tn", pack=(pack, grad_pack[0], grad_pack[1] + i, "cols"), name="mlp_dw_up")
        dx1 = mm(dh1, mlp_w1[i], "nt", epi="add", extra=dr2, alpha=ALPHA, name="mlp_dx")
        dr1, g_ln1g[i], g_ln1b[i] = ln_bwd(xin, m, ln1_g[i:i + 1], dx1)
        dh, g = (_gdn_bwd if i % 2 == 0 else _gla_bwd)(sv, p, dr1)
        (g_a if i % 2 == 0 else g_b)[i // 2] = g

    per_layer = lambda d, key=None: [(d[i] if key is None else d[i][key]) for i in sorted(d)]
    st = lambda d, key=None: jnp.stack(per_layer(d, key))
    grads = dict(
        a_w_in=per_layer(g_a, "w_in"), a_conv=st(g_a, "conv"), a_alog=st(g_a, "alog"), a_dt_bias=st(g_a, "dt"),
        a_norm_g=st(g_a, "norm_g"), a_w_out=per_layer(g_a, "w_out"), b_w_in=per_layer(g_b, "w_in"),
        b_gate_w2=st(g_b, "gate_w2"), b_gate_b=st(g_b, "gate_b"), b_norm_g=st(g_b, "norm_g"),
        b_w_out=per_layer(g_b, "w_out"), ln1_g=st(g_ln1g)[:, 0], ln1_b=st(g_ln1b)[:, 0], mlp_w1=per_layer(g_w1),
        mlp_w2=per_layer(g_w2), ln2_g=st(g_ln2g)[:, 0], ln2_b=st(g_ln2b)[:, 0], pack=pack)
    return loss_part, dh, grads


WEIGHTS = ("a_w_in", "a_conv", "a_alog", "a_dt_bias", "a_norm_g", "a_w_out", "b_w_in", "b_gate_w2", "b_gate_b",
           "b_norm_g", "b_w_out", "ln1_g", "ln1_b", "mlp_w1", "mlp_w2", "ln2_g", "ln2_b")
BIG = ("mlp_w1", "mlp_w2", "a_w_out", "b_w_out", "a_w_in", "b_w_in")
SHARD_AXIS = {"mlp_w1": 2, "mlp_w2": 1, "a_w_out": 1, "b_w_out": 1, "a_w_in": 2, "b_w_in": 2}
SMALL = tuple(n for n in WEIGHTS if n not in BIG)
SMALL_SHARD_AXIS = {"a_conv": 2, "b_gate_w2": 3, "b_gate_b": 2, "b_norm_g": 1}


def _to_chip_major(full, axis):
    shp = full.shape
    t = full.reshape(shp[:axis] + (4, shp[axis] // 4) + shp[axis + 1:])
    return jnp.moveaxis(t, axis, 0)


def _from_chip_major(stacked, axis):
    t = jnp.moveaxis(stacked, 0, axis)
    shp = t.shape
    return t.reshape(shp[:axis] + (shp[axis] * shp[axis + 1],) + shp[axis + 2:])


def kernel(x, a_w_in, a_conv, a_alog, a_dt_bias, a_norm_g, a_w_out, b_w_in, b_gate_w2, b_gate_b, b_norm_g, b_w_out, ln1_g, ln1_b, mlp_w1, mlp_w2, ln2_g, ln2_b, loss_target, m_a_w_in, m_a_conv, m_a_alog, m_a_dt_bias, m_a_norm_g, m_a_w_out, m_b_w_in, m_b_gate_w2, m_b_gate_b, m_b_norm_g, m_b_w_out, m_ln1_g, m_ln1_b, m_mlp_w1, m_mlp_w2, m_ln2_g, m_ln2_b, v_a_w_in, v_a_conv, v_a_alog, v_a_dt_bias, v_a_norm_g, v_a_w_out, v_b_w_in, v_b_gate_w2, v_b_gate_b, v_b_norm_g, v_b_w_out, v_ln1_g, v_ln1_b, v_mlp_w1, v_mlp_w2, v_ln2_g, v_ln2_b):
    w = dict(a_w_in=a_w_in, a_conv=a_conv, a_alog=a_alog, a_dt_bias=a_dt_bias, a_norm_g=a_norm_g, a_w_out=a_w_out,
             b_w_in=b_w_in, b_gate_w2=b_gate_w2, b_gate_b=b_gate_b, b_norm_g=b_norm_g, b_w_out=b_w_out, ln1_g=ln1_g,
             ln1_b=ln1_b, mlp_w1=mlp_w1, mlp_w2=mlp_w2, ln2_g=ln2_g, ln2_b=ln2_b)
    mom = dict(a_w_in=m_a_w_in, a_conv=m_a_conv, a_alog=m_a_alog, a_dt_bias=m_a_dt_bias, a_norm_g=m_a_norm_g,
               a_w_out=m_a_w_out, b_w_in=m_b_w_in, b_gate_w2=m_b_gate_w2, b_gate_b=m_b_gate_b, b_norm_g=m_b_norm_g,
               b_w_out=m_b_w_out, ln1_g=m_ln1_g, ln1_b=m_ln1_b, mlp_w1=m_mlp_w1, mlp_w2=m_mlp_w2, ln2_g=m_ln2_g,
               ln2_b=m_ln2_b)
    var = dict(a_w_in=v_a_w_in, a_conv=v_a_conv, a_alog=v_a_alog, a_dt_bias=v_a_dt_bias, a_norm_g=v_a_norm_g,
               a_w_out=v_a_w_out, b_w_in=v_b_w_in, b_gate_w2=v_b_gate_w2, b_gate_b=v_b_gate_b, b_norm_g=v_b_norm_g,
               b_w_out=v_b_w_out, ln1_g=v_ln1_g, ln1_b=v_ln1_b, mlp_w1=v_mlp_w1, mlp_w2=v_mlp_w2, ln2_g=v_ln2_g,
               ln2_b=v_ln2_b)
    chip = 2 * lax.axis_index("x") + lax.axis_index("y")

    seg_rows = [w[n].size // D_MODEL for n in BIG]
    seg_off = [sum(seg_rows[:i]) for i in range(len(BIG))]
    rows = -(-sum(seg_rows) // PACK_TILE) * PACK_TILE
    shard_pack = jnp.concatenate([w[n].reshape(-1, D_MODEL) for n in BIG]
                                 + [jnp.zeros((rows - sum(seg_rows), D_MODEL), F32)], axis=0)
    chip_idx = chip.astype(jnp.int32).reshape(1)
    gathered = allgather_chips(cast_into_slot(shard_pack, chip_idx))
    full = {}
    for n, off, nr in zip(BIG, seg_off, seg_rows):
        if n in ("mlp_w1", "mlp_w2"):
            kind = "cols" if SHARD_AXIS[n] == 2 else "rows"
            full[n] = [Gathered(gathered, off + i * D_MODEL, kind) for i in range(DEPTH)]
            continue
        stacked = gathered[:, off:off + nr].reshape((4,) + w[n].shape)
        full[n] = _from_chip_major(stacked, SHARD_AXIS[n])
    sharded_small = tuple(SMALL_SHARD_AXIS)
    sm_shapes = [w[n].shape for n in sharded_small]
    sm_rows = _rows_for(sm_shapes, LANES)
    sm_all = exchange_all(_pack_rows([w[n] for n in sharded_small], sm_rows, LANES), "gather_small")
    per_chip = [_unpack_rows(sm_all[2 * pch], sm_shapes) for pch in range(4)]
    for idx, n in enumerate(sharded_small):
        full[n] = jnp.concatenate([per_chip[pch][idx] for pch in range(4)], axis=SMALL_SHARD_AXIS[n])
    for n in WEIGHTS:
        full.setdefault(n, w[n])

    blk_of = {n: off // D_MODEL for n, off in zip(BIG, seg_off)}
    loss_part, grad_x, grads = _local_step(x[0], loss_target[0], *[full[n] for n in WEIGHTS],
                                           grad_pack=(rows, blk_of["mlp_w1"], blk_of["mlp_w2"]))
    loss = lax.psum(jnp.sum(loss_part), ("x", "y", "c"))

    gpack = grads["pack"]
    rest = jnp.concatenate(
        [_to_chip_major(g, SHARD_AXIS[n] - 1).reshape(4, -1, D_MODEL) for n in BIG[2:] for g in grads[n]]
        + [jnp.zeros((4, rows - sum(seg_rows), D_MODEL), F32)], axis=1)
    gpack = lax.dynamic_update_slice(gpack, rest, (0, seg_off[2], 0))
    core = lax.axis_index("c").astype(jnp.int32).reshape(1)
    theirs = swap_sibling(half_to_bf16(gpack, 1 - core))
    chip_sum = half_to_bf16(gpack, core, theirs)
    ax, ay = lax.axis_index("x"), lax.axis_index("y")
    slot_x = (2 * (1 - ax) + ay).astype(jnp.int32).reshape(1)
    slot_y = (2 * ax + (1 - ay)).astype(jnp.int32).reshape(1)
    to_nbrs = merge_first_hop(chip_sum, scatter_first_hop(chip_sum), slot_x, slot_y, core)
    reduced = join_halves(sum_received(chip_sum, scatter_second_hop(to_nbrs), chip_idx, core))
    out_g, out_d, out_m, out_v = {}, {}, {}, {}
    for n, off, nr in zip(BIG, seg_off, seg_rows):
        if w[n].shape[-1] == D_MODEL:
            view = lambda t: t.reshape(-1, D_MODEL)
            res = adamw(view(w[n]), view(mom[n]), view(var[n]), (reduced,), off, "adamw_" + n)
        else:
            cols = w[n].shape[-1]
            view = lambda t: t.reshape(-1, cols)
            res = adamw(view(w[n]), view(mom[n]), view(var[n]), (view(reduced[off:off + nr]),), 0, "adamw_" + n)
        out_g[n], out_d[n], out_m[n], out_v[n] = (t.reshape(w[n].shape) for t in res)

    all_shapes = [full[n].shape for n in SMALL]
    g_rows = _rows_for(all_shapes, LANES)
    g_all = exchange_all(_pack_rows([grads[n] for n in SMALL], g_rows, LANES), "gather_small_grads")
    g_sum = _unpack_rows(sum_slots(g_all, "sum_small_grads"), all_shapes)
    g_mine = []
    for n, g in zip(SMALL, g_sum):
        if n in SMALL_SHARD_AXIS:
            ax = SMALL_SHARD_AXIS[n]
            g = lax.dynamic_slice_in_dim(g, chip * w[n].shape[ax], w[n].shape[ax], axis=ax)
        g_mine.append(g)
    my_shapes = [w[n].shape for n in SMALL]
    s_rows = _rows_for(my_shapes, LANES)
    pk = lambda d: _pack_rows([d[n] for n in SMALL], s_rows, LANES)
    res = adamw(pk(w), pk(mom), pk(var), (_pack_rows(g_mine, s_rows, LANES),), 0, "adamw_small")
    for dst, pack in zip((out_g, out_d, out_m, out_v), res):
        for n, t in zip(SMALL, _unpack_rows(pack, my_shapes)):
            dst[n] = t

    return (loss, grad_x[None], *[out_g[n] for n in WEIGHTS], *[out_d[n] for n in WEIGHTS],
            *[out_m[n] for n in WEIGHTS], *[out_v[n] for n in WEIGHTS])
```
